```python
import math
import jax, jax.numpy as jnp
from jax import lax
import numpy as np

D_MODEL = 1024
BATCH = 8
SEQ = 4096
DEPTH = 4

N_MIXERS = 2
N_SSD = (DEPTH + 1) // 2
N_FOX = DEPTH // 2
EPS = 1e-6

SSD_EXPAND = 2
SSD_D_INNER = SSD_EXPAND * D_MODEL
SSD_HEAD_DIM = 64
SSD_HEADS = SSD_D_INNER // SSD_HEAD_DIM
SSD_GROUPS = 4
SSD_HPG = SSD_HEADS // SSD_GROUPS
SSD_STATE = 128
SSD_CONV = 4
SSD_CHUNK = 128
SSD_CONV_DIM = SSD_D_INNER + 2 * SSD_GROUPS * SSD_STATE
SSD_IN_DIM = SSD_D_INNER + SSD_CONV_DIM + SSD_HEADS

FOX_HEAD_DIM = 64
FOX_HEADS = D_MODEL // FOX_HEAD_DIM
FOX_D = FOX_HEADS * FOX_HEAD_DIM
FOX_IN_DIM = 4 * FOX_D + FOX_HEADS
Q_BLOCK = 128

D_FF = 2816
FFN_CONV = 3

kernel_name = 'hybrid_ssd_fox_convglu_trunk'


def rmsnorm(x, g):
    xf = x.astype(jnp.float32)
    y = xf * lax.rsqrt(jnp.mean(xf * xf, axis=-1, keepdims=True) + EPS)
    return (y * g.astype(jnp.float32)).astype(x.dtype)


def causal_dwconv(x, w, b):
    k_w, c = w.shape
    y = lax.conv_general_dilated(x, w[:, None, :].astype(x.dtype), window_strides=(1,),
                                 padding=[(k_w - 1, 0)],
                                 dimension_numbers=('NWC', 'WIO', 'NWC'),
                                 feature_group_count=c)
    return y + b.astype(x.dtype)


def ssd_chunked(xs, dt, a, bm, cm):
    bsz, s_len = xs.shape[0], xs.shape[1]
    nc, l_c = s_len // SSD_CHUNK, SSD_CHUNK
    x = xs.reshape(bsz, nc, l_c, SSD_GROUPS, SSD_HPG, SSD_HEAD_DIM)
    dtc = dt.reshape(bsz, nc, l_c, SSD_GROUPS, SSD_HPG)
    bc = bm.reshape(bsz, nc, l_c, SSD_GROUPS, SSD_STATE)
    cc = cm.reshape(bsz, nc, l_c, SSD_GROUPS, SSD_STATE)
    da = jnp.moveaxis(dtc * a.reshape(SSD_GROUPS, SSD_HPG), 2, -1)
    a_cs = jnp.cumsum(da, axis=-1)
    xdt = x * dtc[..., None]
    causal = jnp.tril(jnp.ones((l_c, l_c), dtype=bool))
    decay = jnp.exp(jnp.where(causal, a_cs[..., :, None] - a_cs[..., None, :], -jnp.inf))
    cb = jnp.einsum('bclgn,bcsgn->bcgls', cc, bc)
    y_diag = jnp.einsum('bcgrls,bcsgrp->bclgrp', cb[:, :, :, None] * decay, xdt)
    decay_states = jnp.exp(a_cs[..., -1:] - a_cs)
    states = jnp.einsum('bclgn,bcgrl,bclgrp->bcgrpn', bc, decay_states, xdt).astype(jnp.float32)
    chunk_decay = jnp.exp(a_cs[..., -1])

    def step(h, inp):
        st, dec = inp
        return h * dec[..., None, None] + st, h

    h0 = jnp.zeros((bsz, SSD_GROUPS, SSD_HPG, SSD_HEAD_DIM, SSD_STATE), jnp.float32)
    _, prev = lax.scan(step, h0, (jnp.moveaxis(states, 1, 0), jnp.moveaxis(chunk_decay, 1, 0)))
    prev = jnp.moveaxis(prev, 0, 1)
    y_off = jnp.einsum('bclgn,bcgrpn,bcgrl->bclgrp', cc, prev, jnp.exp(a_cs))
    return (y_diag + y_off).reshape(bsz, s_len, SSD_HEADS, SSD_HEAD_DIM).astype(xs.dtype)


def mamba2_mixer(h, w_in, conv_w, conv_b, dt_bias, a_log, d_skip, norm_g, w_out):
    bsz, s_len, _ = h.shape
    proj = h @ w_in
    z = proj[..., :SSD_D_INNER]
    xbc = proj[..., SSD_D_INNER:SSD_D_INNER + SSD_CONV_DIM]
    dt_raw = proj[..., SSD_D_INNER + SSD_CONV_DIM:]
    xbc = jax.nn.silu(causal_dwconv(xbc, conv_w, conv_b))
    gn = SSD_GROUPS * SSD_STATE
    xs = xbc[..., :SSD_D_INNER].reshape(bsz, s_len, SSD_HEADS, SSD_HEAD_DIM)
    bm = xbc[..., SSD_D_INNER:SSD_D_INNER + gn].reshape(bsz, s_len, SSD_GROUPS, SSD_STATE)
    cm = xbc[..., SSD_D_INNER + gn:].reshape(bsz, s_len, SSD_GROUPS, SSD_STATE)
    dt = jax.nn.softplus((dt_raw + dt_bias).astype(jnp.float32))
    a = -jnp.exp(a_log.astype(jnp.float32))
    y = ssd_chunked(xs, dt, a, bm, cm) + xs * d_skip[:, None]
    y = y.reshape(bsz, s_len, SSD_D_INNER)
    yz = (y * jax.nn.silu(z)).astype(jnp.float32).reshape(bsz, s_len, SSD_GROUPS, SSD_D_INNER // SSD_GROUPS)
    yz = yz * lax.rsqrt(jnp.mean(yz * yz, axis=-1, keepdims=True) + EPS)
    y = (yz.reshape(bsz, s_len, SSD_D_INNER) * norm_g.astype(jnp.float32)).astype(h.dtype)
    return y @ w_out


def fox_attention(h, w_in, b_f, q_norm_g, k_norm_g, w_out):
    bsz, s_len, _ = h.shape
    proj = h @ w_in
    q = rmsnorm(proj[..., :FOX_D].reshape(bsz, s_len, FOX_HEADS, FOX_HEAD_DIM), q_norm_g)
    k = rmsnorm(proj[..., FOX_D:2 * FOX_D].reshape(bsz, s_len, FOX_HEADS, FOX_HEAD_DIM), k_norm_g)
    v = proj[..., 2 * FOX_D:3 * FOX_D].reshape(bsz, s_len, FOX_HEADS, FOX_HEAD_DIM)
    gate = proj[..., 3 * FOX_D:4 * FOX_D]
    log_f = jax.nn.log_sigmoid((proj[..., 4 * FOX_D:] + b_f).astype(jnp.float32))
    cum = jnp.transpose(jnp.cumsum(log_f, axis=1), (0, 2, 1))
    q, k, v = (jnp.transpose(t, (0, 2, 1, 3)) for t in (q, k, v))
    scale = FOX_HEAD_DIM ** -0.5
    outs = []
    for i in range(s_len // Q_BLOCK):
        qs, qe = i * Q_BLOCK, (i + 1) * Q_BLOCK
        sc = jnp.einsum('bhqd,bhkd->bhqk', q[:, :, qs:qe], k[:, :, :qe]).astype(jnp.float32) * scale
        sc = sc + cum[:, :, qs:qe, None] - cum[:, :, None, :qe]
        mask = (qs + jnp.arange(Q_BLOCK))[:, None] >= jnp.arange(qe)[None, :]
        p = jax.nn.softmax(jnp.where(mask, sc, -jnp.inf), axis=-1)
        outs.append(jnp.einsum('bhqk,bhkd->bhqd', p.astype(v.dtype), v[:, :, :qe]))
    o = jnp.transpose(jnp.concatenate(outs, axis=2), (0, 2, 1, 3)).reshape(bsz, s_len, FOX_D)
    return (o * jax.nn.sigmoid(gate)) @ w_out


def conv_glu_ffn(h, w_up, conv_w, conv_b, w_down):
    u = h @ w_up
    gate = causal_dwconv(u[..., :D_FF], conv_w, conv_b)
    return (jax.nn.silu(gate) * u[..., D_FF:]) @ w_down


def _fwd_setup_inputs(seed: int = 0) -> dict:
    key = jax.random.key(seed)
    ks = jax.random.split(key, 24)
    f32 = jnp.float32

    def nrm(k, shape, scale):
        return jax.random.normal(k, shape, f32) * scale

    res_scale = (2 * DEPTH) ** -0.5
    dt_init = jnp.exp(jax.random.uniform(ks[4], (N_SSD, SSD_HEADS), f32, math.log(1e-3), math.log(1e-1)))
    return {
        'x': nrm(ks[0], (BATCH, SEQ, D_MODEL), 1.0),
        'mix_norm_g': 1.0 + nrm(ks[1], (DEPTH, D_MODEL), 0.02),
        'ffn_norm_g': 1.0 + nrm(ks[2], (DEPTH, D_MODEL), 0.02),
        'ssd_w_in': nrm(ks[3], (N_SSD, D_MODEL, SSD_IN_DIM), D_MODEL ** -0.5),
        'ssd_conv_w': nrm(ks[5], (N_SSD, SSD_CONV, SSD_CONV_DIM), SSD_CONV ** -0.5),
        'ssd_conv_b': nrm(ks[6], (N_SSD, SSD_CONV_DIM), 0.02),
        'ssd_dt_bias': dt_init + jnp.log(-jnp.expm1(-dt_init)),
        'ssd_a_log': jnp.log(jax.random.uniform(ks[7], (N_SSD, SSD_HEADS), f32, 1.0, 16.0)),
        'ssd_d': 1.0 + nrm(ks[8], (N_SSD, SSD_HEADS), 0.1),
        'ssd_norm_g': 1.0 + nrm(ks[9], (N_SSD, SSD_D_INNER), 0.02),
        'ssd_w_out': nrm(ks[10], (N_SSD, SSD_D_INNER, D_MODEL), SSD_D_INNER ** -0.5 * res_scale),
        'fox_w_in': nrm(ks[11], (N_FOX, D_MODEL, FOX_IN_DIM), D_MODEL ** -0.5),
        'fox_b_f': jax.random.uniform(ks[12], (N_FOX, FOX_HEADS), f32, 2.0, 6.0),
        'fox_q_norm_g': 1.0 + nrm(ks[13], (N_FOX, FOX_HEAD_DIM), 0.02),
        'fox_k_norm_g': 1.0 + nrm(ks[14], (N_FOX, FOX_HEAD_DIM), 0.02),
        'fox_w_out': nrm(ks[15], (N_FOX, FOX_D, D_MODEL), FOX_D ** -0.5 * res_scale),
        'ffn_w_up': nrm(ks[16], (DEPTH, D_MODEL, 2 * D_FF), D_MODEL ** -0.5),
        'ffn_conv_w': nrm(ks[17], (DEPTH, FFN_CONV, D_FF), FFN_CONV ** -0.5),
        'ffn_conv_b': nrm(ks[18], (DEPTH, D_FF), 0.02),
        'ffn_w_down': nrm(ks[19], (DEPTH, D_FF, D_MODEL), D_FF ** -0.5 * res_scale),
        'final_norm_g': 1.0 + nrm(ks[20], (D_MODEL,), 0.02),
    }


def _fwd_reference(x, mix_norm_g, ffn_norm_g,
              ssd_w_in, ssd_conv_w, ssd_conv_b, ssd_dt_bias, ssd_a_log, ssd_d, ssd_norm_g, ssd_w_out,
              fox_w_in, fox_b_f, fox_q_norm_g, fox_k_norm_g, fox_w_out,
              ffn_w_up, ffn_conv_w, ffn_conv_b, ffn_w_down, final_norm_g):
    h = x
    for i in range(DEPTH):
        hn = rmsnorm(h, mix_norm_g[i])
        j = i // N_MIXERS
        if i % N_MIXERS == 0:
            h = h + mamba2_mixer(hn, ssd_w_in[j], ssd_conv_w[j], ssd_conv_b[j], ssd_dt_bias[j],
                                 ssd_a_log[j], ssd_d[j], ssd_norm_g[j], ssd_w_out[j])
        else:
            h = h + fox_attention(hn, fox_w_in[j], fox_b_f[j], fox_q_norm_g[j], fox_k_norm_g[j], fox_w_out[j])
        h = h + conv_glu_ffn(rmsnorm(h, ffn_norm_g[i]), ffn_w_up[i], ffn_conv_w[i], ffn_conv_b[i], ffn_w_down[i])
    return rmsnorm(h, final_norm_g)


import jax as _jax
import jax.numpy as _jnp

TWIN_FORMAT = 'train_step'
FWD_PARAMS = ['x', 'mix_norm_g', 'ffn_norm_g', 'ssd_w_in', 'ssd_conv_w', 'ssd_conv_b', 'ssd_dt_bias', 'ssd_a_log', 'ssd_d', 'ssd_norm_g', 'ssd_w_out', 'fox_w_in', 'fox_b_f', 'fox_q_norm_g', 'fox_k_norm_g', 'fox_w_out', 'ffn_w_up', 'ffn_conv_w', 'ffn_conv_b', 'ffn_w_down', 'final_norm_g']
TWIN_WEIGHTS = ['mix_norm_g', 'ffn_norm_g', 'ssd_w_in', 'ssd_conv_w', 'ssd_conv_b', 'ssd_dt_bias', 'ssd_a_log', 'ssd_d', 'ssd_norm_g', 'ssd_w_out', 'fox_w_in', 'fox_b_f', 'fox_q_norm_g', 'fox_k_norm_g', 'fox_w_out', 'ffn_w_up', 'ffn_conv_w', 'ffn_conv_b', 'ffn_w_down', 'final_norm_g']
TWIN_DIFF_INPUT = 'x'
TWIN_INPUTS = ['x', 'mix_norm_g', 'ffn_norm_g', 'ssd_w_in', 'ssd_conv_w', 'ssd_conv_b', 'ssd_dt_bias', 'ssd_a_log', 'ssd_d', 'ssd_norm_g', 'ssd_w_out', 'fox_w_in', 'fox_b_f', 'fox_q_norm_g', 'fox_k_norm_g', 'fox_w_out', 'ffn_w_up', 'ffn_conv_w', 'ffn_conv_b', 'ffn_w_down', 'final_norm_g', 'loss_target', 'm_mix_norm_g', 'm_ffn_norm_g', 'm_ssd_w_in', 'm_ssd_conv_w', 'm_ssd_conv_b', 'm_ssd_dt_bias', 'm_ssd_a_log', 'm_ssd_d', 'm_ssd_norm_g', 'm_ssd_w_out', 'm_fox_w_in', 'm_fox_b_f', 'm_fox_q_norm_g', 'm_fox_k_norm_g', 'm_fox_w_out', 'm_ffn_w_up', 'm_ffn_conv_w', 'm_ffn_conv_b', 'm_ffn_w_down', 'm_final_norm_g', 'v_mix_norm_g', 'v_ffn_norm_g', 'v_ssd_w_in', 'v_ssd_conv_w', 'v_ssd_conv_b', 'v_ssd_dt_bias', 'v_ssd_a_log', 'v_ssd_d', 'v_ssd_norm_g', 'v_ssd_w_out', 'v_fox_w_in', 'v_fox_b_f', 'v_fox_q_norm_g', 'v_fox_k_norm_g', 'v_fox_w_out', 'v_ffn_w_up', 'v_ffn_conv_w', 'v_ffn_conv_b', 'v_ffn_w_down', 'v_final_norm_g']
TWIN_OUTPUTS = ['loss', 'grad_x', 'grad_mix_norm_g', 'grad_ffn_norm_g', 'grad_ssd_w_in', 'grad_ssd_conv_w', 'grad_ssd_conv_b', 'grad_ssd_dt_bias', 'grad_ssd_a_log', 'grad_ssd_d', 'grad_ssd_norm_g', 'grad_ssd_w_out', 'grad_fox_w_in', 'grad_fox_b_f', 'grad_fox_q_norm_g', 'grad_fox_k_norm_g', 'grad_fox_w_out', 'grad_ffn_w_up', 'grad_ffn_conv_w', 'grad_ffn_conv_b', 'grad_ffn_w_down', 'grad_final_norm_g', 'delta_mix_norm_g', 'delta_ffn_norm_g', 'delta_ssd_w_in', 'delta_ssd_conv_w', 'delta_ssd_conv_b', 'delta_ssd_dt_bias', 'delta_ssd_a_log', 'delta_ssd_d', 'delta_ssd_norm_g', 'delta_ssd_w_out', 'delta_fox_w_in', 'delta_fox_b_f', 'delta_fox_q_norm_g', 'delta_fox_k_norm_g', 'delta_fox_w_out', 'delta_ffn_w_up', 'delta_ffn_conv_w', 'delta_ffn_conv_b', 'delta_ffn_w_down', 'delta_final_norm_g', 'new_m_mix_norm_g', 'new_m_ffn_norm_g', 'new_m_ssd_w_in', 'new_m_ssd_conv_w', 'new_m_ssd_conv_b', 'new_m_ssd_dt_bias', 'new_m_ssd_a_log', 'new_m_ssd_d', 'new_m_ssd_norm_g', 'new_m_ssd_w_out', 'new_m_fox_w_in', 'new_m_fox_b_f', 'new_m_fox_q_norm_g', 'new_m_fox_k_norm_g', 'new_m_fox_w_out', 'new_m_ffn_w_up', 'new_m_ffn_conv_w', 'new_m_ffn_conv_b', 'new_m_ffn_w_down', 'new_m_final_norm_g', 'new_v_mix_norm_g', 'new_v_ffn_norm_g', 'new_v_ssd_w_in', 'new_v_ssd_conv_w', 'new_v_ssd_conv_b', 'new_v_ssd_dt_bias', 'new_v_ssd_a_log', 'new_v_ssd_d', 'new_v_ssd_norm_g', 'new_v_ssd_w_out', 'new_v_fox_w_in', 'new_v_fox_b_f', 'new_v_fox_q_norm_g', 'new_v_fox_k_norm_g', 'new_v_fox_w_out', 'new_v_ffn_w_up', 'new_v_ffn_conv_w', 'new_v_ffn_conv_b', 'new_v_ffn_w_down', 'new_v_final_norm_g']
TWIN_LEAF_KINDS = {'loss': 'loss', 'grad_x': 'grad_x', 'grad_mix_norm_g': 'grad_w', 'grad_ffn_norm_g': 'grad_w', 'grad_ssd_w_in': 'grad_w', 'grad_ssd_conv_w': 'grad_w', 'grad_ssd_conv_b': 'grad_w', 'grad_ssd_dt_bias': 'grad_w', 'grad_ssd_a_log': 'grad_w', 'grad_ssd_d': 'grad_w', 'grad_ssd_norm_g': 'grad_w', 'grad_ssd_w_out': 'grad_w', 'grad_fox_w_in': 'grad_w', 'grad_fox_b_f': 'grad_w', 'grad_fox_q_norm_g': 'grad_w', 'grad_fox_k_norm_g': 'grad_w', 'grad_fox_w_out': 'grad_w', 'grad_ffn_w_up': 'grad_w', 'grad_ffn_conv_w': 'grad_w', 'grad_ffn_conv_b': 'grad_w', 'grad_ffn_w_down': 'grad_w', 'grad_final_norm_g': 'grad_w', 'delta_mix_norm_g': 'delta_w', 'delta_ffn_norm_g': 'delta_w', 'delta_ssd_w_in': 'delta_w', 'delta_ssd_conv_w': 'delta_w', 'delta_ssd_conv_b': 'delta_w', 'delta_ssd_dt_bias': 'delta_w', 'delta_ssd_a_log': 'delta_w', 'delta_ssd_d': 'delta_w', 'delta_ssd_norm_g': 'delta_w', 'delta_ssd_w_out': 'delta_w', 'delta_fox_w_in': 'delta_w', 'delta_fox_b_f': 'delta_w', 'delta_fox_q_norm_g': 'delta_w', 'delta_fox_k_norm_g': 'delta_w', 'delta_fox_w_out': 'delta_w', 'delta_ffn_w_up': 'delta_w', 'delta_ffn_conv_w': 'delta_w', 'delta_ffn_conv_b': 'delta_w', 'delta_ffn_w_down': 'delta_w', 'delta_final_norm_g': 'delta_w', 'new_m_mix_norm_g': 'new_m', 'new_m_ffn_norm_g': 'new_m', 'new_m_ssd_w_in': 'new_m', 'new_m_ssd_conv_w': 'new_m', 'new_m_ssd_conv_b': 'new_m', 'new_m_ssd_dt_bias': 'new_m', 'new_m_ssd_a_log': 'new_m', 'new_m_ssd_d': 'new_m', 'new_m_ssd_norm_g': 'new_m', 'new_m_ssd_w_out': 'new_m', 'new_m_fox_w_in': 'new_m', 'new_m_fox_b_f': 'new_m', 'new_m_fox_q_norm_g': 'new_m', 'new_m_fox_k_norm_g': 'new_m', 'new_m_fox_w_out': 'new_m', 'new_m_ffn_w_up': 'new_m', 'new_m_ffn_conv_w': 'new_m', 'new_m_ffn_conv_b': 'new_m', 'new_m_ffn_w_down': 'new_m', 'new_m_final_norm_g': 'new_m', 'new_v_mix_norm_g': 'new_v', 'new_v_ffn_norm_g': 'new_v', 'new_v_ssd_w_in': 'new_v', 'new_v_ssd_conv_w': 'new_v', 'new_v_ssd_conv_b': 'new_v', 'new_v_ssd_dt_bias': 'new_v', 'new_v_ssd_a_log': 'new_v', 'new_v_ssd_d': 'new_v', 'new_v_ssd_norm_g': 'new_v', 'new_v_ssd_w_out': 'new_v', 'new_v_fox_w_in': 'new_v', 'new_v_fox_b_f': 'new_v', 'new_v_fox_q_norm_g': 'new_v', 'new_v_fox_k_norm_g': 'new_v', 'new_v_fox_w_out': 'new_v', 'new_v_ffn_w_up': 'new_v', 'new_v_ffn_conv_w': 'new_v', 'new_v_ffn_conv_b': 'new_v', 'new_v_ffn_w_down': 'new_v', 'new_v_final_norm_g': 'new_v'}


def _forward(args):
    return _fwd_reference(*[args[k] for k in FWD_PARAMS])


def _output_shape():
    def fwd():
        inp = _fwd_setup_inputs(0)
        return _fwd_reference(*[inp[k] for k in FWD_PARAMS])
    out = _jax.eval_shape(fwd)
    return out.shape, out.dtype

N_MICROBATCH = 1
ADAM_LR = 0.001
ADAM_B1 = 0.9
ADAM_B2 = 0.999
ADAM_EPS = 1e-08
ADAM_WD = 0.01
ADAM_STEP = 10
PER_EXAMPLE_BATCH_AXIS = {'x': 0, 'loss_target': 0}
SHARED_INPUTS = []
_WEIGHT_DTYPES = {'mix_norm_g': _jnp.float32, 'ffn_norm_g': _jnp.float32, 'ssd_w_in': _jnp.float32, 'ssd_conv_w': _jnp.float32, 'ssd_conv_b': _jnp.float32, 'ssd_dt_bias': _jnp.float32, 'ssd_a_log': _jnp.float32, 'ssd_d': _jnp.float32, 'ssd_norm_g': _jnp.float32, 'ssd_w_out': _jnp.float32, 'fox_w_in': _jnp.float32, 'fox_b_f': _jnp.float32, 'fox_q_norm_g': _jnp.float32, 'fox_k_norm_g': _jnp.float32, 'fox_w_out': _jnp.float32, 'ffn_w_up': _jnp.float32, 'ffn_conv_w': _jnp.float32, 'ffn_conv_b': _jnp.float32, 'ffn_w_down': _jnp.float32, 'final_norm_g': _jnp.float32}
MOMENT_SCALE = {'mix_norm_g': 6.650589e-02, 'ffn_norm_g': 5.186599e-02, 'ssd_w_in': 4.127947e-02, 'ssd_conv_w': 3.829965e-02, 'ssd_conv_b': 5.101295e-02, 'ssd_dt_bias': 9.214406e-02, 'ssd_a_log': 1.278004e-01, 'ssd_d': 2.654298e-01, 'ssd_norm_g': 4.618386e-02, 'ssd_w_out': 1.764598e-01, 'fox_w_in': 7.124061e-03, 'fox_b_f': 4.280513e-02, 'fox_q_norm_g': 2.921506e-02, 'fox_k_norm_g': 2.928471e-02, 'fox_w_out': 2.602554e-02, 'ffn_w_up': 2.193296e-02, 'ffn_conv_w': 2.228040e-02, 'ffn_conv_b': 2.105851e-02, 'ffn_w_down': 1.012562e-01, 'final_norm_g': 3.202910e+01}


def _to_microbatches(a, axis):
    t = _jnp.moveaxis(a, axis, 0)
    t = t.reshape((N_MICROBATCH, t.shape[0] // N_MICROBATCH) + t.shape[1:])
    return _jnp.moveaxis(t, 1, axis + 1)


def setup_inputs(seed: int = 0) -> dict:
    inp = _fwd_setup_inputs(seed)
    key = _jax.random.fold_in(_jax.random.key(seed), 7919)
    shape, _ = _output_shape()
    out = dict(inp)
    out["loss_target"] = _jax.random.normal(_jax.random.fold_in(key, 0), shape, _jnp.float32)
    for i, name in enumerate(TWIN_WEIGHTS):
        w = inp[name].astype(_jnp.float32)
        if MOMENT_SCALE is None:
            s = _jnp.sqrt(_jnp.mean(_jnp.square(w)) + 1e-30)
        else:
            s = MOMENT_SCALE[name]
        km, kv = _jax.random.split(_jax.random.fold_in(key, i + 1))
        out[name] = w
        out["m_" + name] = s * _jax.random.normal(km, w.shape, _jnp.float32)
        out["v_" + name] = (s * s) * _jax.random.uniform(kv, w.shape, _jnp.float32, 0.5, 1.5)
    if N_MICROBATCH > 1:
        for name, axis in PER_EXAMPLE_BATCH_AXIS.items():
            out[name] = _to_microbatches(out[name], axis)
    return {'x': out['x'], 'mix_norm_g': out['mix_norm_g'], 'ffn_norm_g': out['ffn_norm_g'], 'ssd_w_in': out['ssd_w_in'], 'ssd_conv_w': out['ssd_conv_w'], 'ssd_conv_b': out['ssd_conv_b'], 'ssd_dt_bias': out['ssd_dt_bias'], 'ssd_a_log': out['ssd_a_log'], 'ssd_d': out['ssd_d'], 'ssd_norm_g': out['ssd_norm_g'], 'ssd_w_out': out['ssd_w_out'], 'fox_w_in': out['fox_w_in'], 'fox_b_f': out['fox_b_f'], 'fox_q_norm_g': out['fox_q_norm_g'], 'fox_k_norm_g': out['fox_k_norm_g'], 'fox_w_out': out['fox_w_out'], 'ffn_w_up': out['ffn_w_up'], 'ffn_conv_w': out['ffn_conv_w'], 'ffn_conv_b': out['ffn_conv_b'], 'ffn_w_down': out['ffn_w_down'], 'final_norm_g': out['final_norm_g'], 'loss_target': out['loss_target'], 'm_mix_norm_g': out['m_mix_norm_g'], 'm_ffn_norm_g': out['m_ffn_norm_g'], 'm_ssd_w_in': out['m_ssd_w_in'], 'm_ssd_conv_w': out['m_ssd_conv_w'], 'm_ssd_conv_b': out['m_ssd_conv_b'], 'm_ssd_dt_bias': out['m_ssd_dt_bias'], 'm_ssd_a_log': out['m_ssd_a_log'], 'm_ssd_d': out['m_ssd_d'], 'm_ssd_norm_g': out['m_ssd_norm_g'], 'm_ssd_w_out': out['m_ssd_w_out'], 'm_fox_w_in': out['m_fox_w_in'], 'm_fox_b_f': out['m_fox_b_f'], 'm_fox_q_norm_g': out['m_fox_q_norm_g'], 'm_fox_k_norm_g': out['m_fox_k_norm_g'], 'm_fox_w_out': out['m_fox_w_out'], 'm_ffn_w_up': out['m_ffn_w_up'], 'm_ffn_conv_w': out['m_ffn_conv_w'], 'm_ffn_conv_b': out['m_ffn_conv_b'], 'm_ffn_w_down': out['m_ffn_w_down'], 'm_final_norm_g': out['m_final_norm_g'], 'v_mix_norm_g': out['v_mix_norm_g'], 'v_ffn_norm_g': out['v_ffn_norm_g'], 'v_ssd_w_in': out['v_ssd_w_in'], 'v_ssd_conv_w': out['v_ssd_conv_w'], 'v_ssd_conv_b': out['v_ssd_conv_b'], 'v_ssd_dt_bias': out['v_ssd_dt_bias'], 'v_ssd_a_log': out['v_ssd_a_log'], 'v_ssd_d': out['v_ssd_d'], 'v_ssd_norm_g': out['v_ssd_norm_g'], 'v_ssd_w_out': out['v_ssd_w_out'], 'v_fox_w_in': out['v_fox_w_in'], 'v_fox_b_f': out['v_fox_b_f'], 'v_fox_q_norm_g': out['v_fox_q_norm_g'], 'v_fox_k_norm_g': out['v_fox_k_norm_g'], 'v_fox_w_out': out['v_fox_w_out'], 'v_ffn_w_up': out['v_ffn_w_up'], 'v_ffn_conv_w': out['v_ffn_conv_w'], 'v_ffn_conv_b': out['v_ffn_conv_b'], 'v_ffn_w_down': out['v_ffn_w_down'], 'v_final_norm_g': out['v_final_norm_g']}


def _loss(weights, diff, rest, loss_target):
    with _jax.named_scope("forward"):
        args = {**rest, TWIN_DIFF_INPUT: diff, **{k: w.astype(_WEIGHT_DTYPES[k]) for k, w in weights.items()}}
        y = _forward(args)
    with _jax.named_scope("loss_head"):
        err = _jnp.square(y.astype(_jnp.float32) - loss_target)
        return 0.5 * _jnp.sum(_jnp.mean(err, axis=-1)) if err.ndim else 0.5 * err


def _adamw(w, g, m, v):
    m = ADAM_B1 * m + (1.0 - ADAM_B1) * g
    v = ADAM_B2 * v + (1.0 - ADAM_B2) * _jnp.square(g)
    m_hat = m / (1.0 - ADAM_B1 ** ADAM_STEP)
    v_hat = v / (1.0 - ADAM_B2 ** ADAM_STEP)
    delta = -ADAM_LR * (m_hat / (_jnp.sqrt(v_hat) + ADAM_EPS) + ADAM_WD * w)
    return delta, m, v


def reference(x, mix_norm_g, ffn_norm_g, ssd_w_in, ssd_conv_w, ssd_conv_b, ssd_dt_bias, ssd_a_log, ssd_d, ssd_norm_g, ssd_w_out, fox_w_in, fox_b_f, fox_q_norm_g, fox_k_norm_g, fox_w_out, ffn_w_up, ffn_conv_w, ffn_conv_b, ffn_w_down, final_norm_g, loss_target, m_mix_norm_g, m_ffn_norm_g, m_ssd_w_in, m_ssd_conv_w, m_ssd_conv_b, m_ssd_dt_bias, m_ssd_a_log, m_ssd_d, m_ssd_norm_g, m_ssd_w_out, m_fox_w_in, m_fox_b_f, m_fox_q_norm_g, m_fox_k_norm_g, m_fox_w_out, m_ffn_w_up, m_ffn_conv_w, m_ffn_conv_b, m_ffn_w_down, m_final_norm_g, v_mix_norm_g, v_ffn_norm_g, v_ssd_w_in, v_ssd_conv_w, v_ssd_conv_b, v_ssd_dt_bias, v_ssd_a_log, v_ssd_d, v_ssd_norm_g, v_ssd_w_out, v_fox_w_in, v_fox_b_f, v_fox_q_norm_g, v_fox_k_norm_g, v_fox_w_out, v_ffn_w_up, v_ffn_conv_w, v_ffn_conv_b, v_ffn_w_down, v_final_norm_g):
    given = dict(x=x, mix_norm_g=mix_norm_g, ffn_norm_g=ffn_norm_g, ssd_w_in=ssd_w_in, ssd_conv_w=ssd_conv_w, ssd_conv_b=ssd_conv_b, ssd_dt_bias=ssd_dt_bias, ssd_a_log=ssd_a_log, ssd_d=ssd_d, ssd_norm_g=ssd_norm_g, ssd_w_out=ssd_w_out, fox_w_in=fox_w_in, fox_b_f=fox_b_f, fox_q_norm_g=fox_q_norm_g, fox_k_norm_g=fox_k_norm_g, fox_w_out=fox_w_out, ffn_w_up=ffn_w_up, ffn_conv_w=ffn_conv_w, ffn_conv_b=ffn_conv_b, ffn_w_down=ffn_w_down, final_norm_g=final_norm_g, loss_target=loss_target, m_mix_norm_g=m_mix_norm_g, m_ffn_norm_g=m_ffn_norm_g, m_ssd_w_in=m_ssd_w_in, m_ssd_conv_w=m_ssd_conv_w, m_ssd_conv_b=m_ssd_conv_b, m_ssd_dt_bias=m_ssd_dt_bias, m_ssd_a_log=m_ssd_a_log, m_ssd_d=m_ssd_d, m_ssd_norm_g=m_ssd_norm_g, m_ssd_w_out=m_ssd_w_out, m_fox_w_in=m_fox_w_in, m_fox_b_f=m_fox_b_f, m_fox_q_norm_g=m_fox_q_norm_g, m_fox_k_norm_g=m_fox_k_norm_g, m_fox_w_out=m_fox_w_out, m_ffn_w_up=m_ffn_w_up, m_ffn_conv_w=m_ffn_conv_w, m_ffn_conv_b=m_ffn_conv_b, m_ffn_w_down=m_ffn_w_down, m_final_norm_g=m_final_norm_g, v_mix_norm_g=v_mix_norm_g, v_ffn_norm_g=v_ffn_norm_g, v_ssd_w_in=v_ssd_w_in, v_ssd_conv_w=v_ssd_conv_w, v_ssd_conv_b=v_ssd_conv_b, v_ssd_dt_bias=v_ssd_dt_bias, v_ssd_a_log=v_ssd_a_log, v_ssd_d=v_ssd_d, v_ssd_norm_g=v_ssd_norm_g, v_ssd_w_out=v_ssd_w_out, v_fox_w_in=v_fox_w_in, v_fox_b_f=v_fox_b_f, v_fox_q_norm_g=v_fox_q_norm_g, v_fox_k_norm_g=v_fox_k_norm_g, v_fox_w_out=v_fox_w_out, v_ffn_w_up=v_ffn_w_up, v_ffn_conv_w=v_ffn_conv_w, v_ffn_conv_b=v_ffn_conv_b, v_ffn_w_down=v_ffn_w_down, v_final_norm_g=v_final_norm_g)
    weights = {n: given[n] for n in TWIN_WEIGHTS}
    shared = {n: given[n] for n in SHARED_INPUTS}
    per_example = {n: given[n] for n in ['x']}
    grad_fn = _jax.value_and_grad(_loss, argnums=(0, 1))

    def one_microbatch(ex, loss_target):
        ex = dict(ex)
        diff = ex.pop(TWIN_DIFF_INPUT)
        return grad_fn(weights, diff, {**shared, **ex}, loss_target)

    if N_MICROBATCH == 1:
        loss, (grad_w, grad_x) = one_microbatch(per_example, given["loss_target"])
    else:
        def body(carry, xs):
            loss_sum, grad_sum = carry
            l_k, (gw_k, gx_k) = one_microbatch(xs[0], xs[1])
            with _jax.named_scope("update"):
                return (loss_sum + l_k, _jax.tree.map(_jnp.add, grad_sum, gw_k)), gx_k

        init = (_jnp.zeros((), _jnp.float32), _jax.tree.map(_jnp.zeros_like, weights))
        (loss, grad_w), grad_x = _jax.lax.scan(body, init, (per_example, given["loss_target"]))
    with _jax.named_scope("update"):
        delta_w, new_m, new_v = {}, {}, {}
        for n in TWIN_WEIGHTS:
            delta_w[n], new_m[n], new_v[n] = _adamw(weights[n], grad_w[n], given["m_" + n], given["v_" + n])
    return (loss, grad_x, *[grad_w[n] for n in TWIN_WEIGHTS], *[delta_w[n] for n in TWIN_WEIGHTS],
            *[new_m[n] for n in TWIN_WEIGHTS], *[new_v[n] for n in TWIN_WEIGHTS])
```

```python
import functools

import jax
import jax.numpy as jnp
from jax import lax
from jax.experimental import pallas as pl
from jax.experimental.pallas import tpu as pltpu

F32 = jnp.float32
BF16 = jnp.bfloat16
HI = lax.Precision.HIGHEST
MESH = pl.DeviceIdType.MESH

D_MODEL = 1024
DEPTH = 4
EPS = 1e-6
SSD_DI = 2048
SSD_HD = 64
SSD_G = 4
SSD_HPG = 8
SSD_N = 128
SSD_K = 4
CHUNK = 128
SSD_CONV_DIM = 3072
SSD_ZX = SSD_DI + SSD_CONV_DIM
SSD_H = 32
FOX_HD = 64
FOX_H = 16
FOX_D = 1024
D_FF = 2816
FFN_K = 3
LANES = 128
VMEM_LIMIT = 56 * 1024 * 1024

ADAM_LR = 0.001
ADAM_B1 = 0.9
ADAM_B2 = 0.999
ADAM_EPS = 1e-08
ADAM_WD = 0.01
ADAM_STEP = 10

NN = (((1,), (0,)), ((), ()))
NT = (((1,), (1,)), ((), ()))
TN = (((0,), (0,)), ((), ()))


def _pick(n, cap, mult=LANES):
    best = None
    for t in range(mult, min(n, cap) + 1, mult):
        if n % t == 0:
            best = t
    return best if best is not None else n


def _cp(sem):
    return pltpu.CompilerParams(dimension_semantics=sem, vmem_limit_bytes=VMEM_LIMIT)


def _sigmoid(x):
    return jax.nn.sigmoid(x)


def _silu(x):
    return x * _sigmoid(x)


def _dsilu(x):
    s = _sigmoid(x)
    return s * (1.0 + x * (1.0 - s))


def _softplus(x):
    e = jnp.exp(-jnp.abs(x))
    u = 1.0 + e
    l1p = jnp.where(u == 1.0, e, jnp.log(u) * (e / (u - 1.0)))
    return jnp.maximum(x, 0.0) + l1p


def _dotf(a, b, dn=NN):
    return lax.dot_general(a, b, dn, precision=HI, preferred_element_type=F32)


def _dotb(a, b, dn=NN):
    return lax.dot_general(a.astype(BF16), b.astype(BF16), dn, preferred_element_type=F32)


def _group_matrix(width, sub, transpose=False):
    ng = width // sub
    shape = (ng, width) if transpose else (width, ng)
    lane = lax.broadcasted_iota(jnp.int32, shape, 1 if transpose else 0)
    grp = lax.broadcasted_iota(jnp.int32, shape, 0 if transpose else 1)
    return (lane // sub == grp).astype(F32)


def _gmean(v, sub):
    width = v.shape[-1]
    if sub == width:
        return jnp.mean(v, axis=-1, keepdims=True)
    s = _dotf(v, _group_matrix(width, sub))
    return _dotf(s, _group_matrix(width, sub, transpose=True)) * (1.0 / sub)


def _matmul(a, b, *, mode, name, out_dtype=F32, add=None):
    if mode == "nn":
        (m, k), (k2, n) = a.shape, b.shape
    elif mode == "nt":
        (m, k), (n, k2) = a.shape, b.shape
    else:
        (k, m), (k2, n) = a.shape, b.shape
    assert k == k2, (a.shape, b.shape, mode)
    tm, tn, tk = _pick(m, 512), _pick(n, 1536), _pick(k, 1536)
    nk = k // tk
    dn = {"nn": NN, "nt": NT, "tn": TN}[mode]
    has_add = add is not None

    def body(*refs):
        if has_add:
            a_ref, b_ref, add_ref, o_ref, acc_ref = refs
        else:
            a_ref, b_ref, o_ref, acc_ref = refs
            add_ref = None
        kk = pl.program_id(2)
        part = _dotb(a_ref[...], b_ref[...], dn)

        def finish(r):
            if has_add:
                r = r + add_ref[...]
            o_ref[...] = r.astype(out_dtype)

        if nk == 1:
            finish(part)
        else:
            @pl.when(kk == 0)
            def _():
                acc_ref[...] = part

            @pl.when(kk > 0)
            def _():
                acc_ref[...] += part

            @pl.when(kk == nk - 1)
            def _():
                finish(acc_ref[...])

    if mode == "nn":
        a_spec = pl.BlockSpec((tm, tk), lambda i, j, q: (i, q))
        b_spec = pl.BlockSpec((tk, tn), lambda i, j, q: (q, j))
    elif mode == "nt":
        a_spec = pl.BlockSpec((tm, tk), lambda i, j, q: (i, q))
        b_spec = pl.BlockSpec((tn, tk), lambda i, j, q: (j, q))
    else:
        a_spec = pl.BlockSpec((tk, tm), lambda i, j, q: (q, i))
        b_spec = pl.BlockSpec((tk, tn), lambda i, j, q: (q, j))
    o_spec = pl.BlockSpec((tm, tn), lambda i, j, q: (i, j))
    in_specs = [a_spec, b_spec] + ([o_spec] if has_add else [])
    args = (a, b) + ((add,) if has_add else ())
    return pl.pallas_call(
        body, name=name, grid=(m // tm, n // tn, nk), in_specs=in_specs, out_specs=o_spec,
        out_shape=jax.ShapeDtypeStruct((m, n), out_dtype),
        scratch_shapes=[pltpu.VMEM((tm, tn), F32)],
        compiler_params=_cp(("parallel", "parallel", "arbitrary")),
    )(*args)


def _rms_fwd(x, g, *, gw, ncol, name, x_col0=0, sub=None, z=None, z_col0=0, out_dtype=BF16):
    rows = x.shape[0]
    tr = _pick(rows, 512, 8)
    sub = gw if sub is None else sub
    gated = z is not None

    def body(*refs):
        if gated:
            x_ref, z_ref, g_ref, o_ref = refs
            xv = x_ref[...] * _silu(z_ref[...])
        else:
            x_ref, g_ref, o_ref = refs
            xv = x_ref[...]
        r = lax.rsqrt(_gmean(xv * xv, sub) + EPS)
        o_ref[...] = (xv * r * g_ref[...]).astype(out_dtype)

    specs = [pl.BlockSpec((tr, gw), lambda j, i: (i, x_col0 + j))]
    args = [x]
    if gated:
        specs.append(pl.BlockSpec((tr, gw), lambda j, i: (i, z_col0 + j)))
        args.append(z)
    specs.append(pl.BlockSpec((1, gw), lambda j, i: (0, j)))
    args.append(g)
    return pl.pallas_call(
        body, name=name, grid=(ncol, rows // tr), in_specs=specs,
        out_specs=pl.BlockSpec((tr, gw), lambda j, i: (i, j)),
        out_shape=jax.ShapeDtypeStruct((rows, gw * ncol), out_dtype),
        compiler_params=_cp(("parallel", "parallel")),
    )(*args)


def _rms_bwd(x, g, dy, *, gw, ncol, name, x_col0=0, sub=None, z=None, z_col0=0, add=None, dx_dtype=F32):
    rows = x.shape[0]
    tr = _pick(rows, 512, 8)
    sub = gw if sub is None else sub
    gated = z is not None
    has_add = add is not None

    def body(*refs):
        refs = list(refs)
        x_ref = refs.pop(0)
        z_ref = refs.pop(0) if gated else None
        g_ref = refs.pop(0)
        dy_ref = refs.pop(0)
        add_ref = refs.pop(0) if has_add else None
        dx_ref = refs.pop(0)
        dz_ref = refs.pop(0) if gated else None
        dg_ref = refs.pop(0)
        i = pl.program_id(1)
        xv = x_ref[...]
        if gated:
            zz = z_ref[...]
            yz = xv * _silu(zz)
        else:
            yz = xv
        r = lax.rsqrt(_gmean(yz * yz, sub) + EPS)
        xh = yz * r
        dy = dy_ref[...].astype(F32)
        dyg = dy * g_ref[...]
        d_yz = r * (dyg - xh * _gmean(dyg * xh, sub))
        if gated:
            dx_ref[...] = (d_yz * _silu(zz)).astype(dx_dtype)
            dz_ref[...] = (d_yz * xv * _dsilu(zz)).astype(dx_dtype)
        elif has_add:
            dx_ref[...] = (d_yz + add_ref[...]).astype(dx_dtype)
        else:
            dx_ref[...] = d_yz.astype(dx_dtype)
        part = jnp.sum(dy * xh, axis=0, keepdims=True)

        @pl.when(i == 0)
        def _():
            dg_ref[...] = part

        @pl.when(i > 0)
        def _():
            dg_ref[...] += part

    tile = pl.BlockSpec((tr, gw), lambda j, i: (i, j))
    specs = [pl.BlockSpec((tr, gw), lambda j, i: (i, x_col0 + j))]
    args = [x]
    if gated:
        specs.append(pl.BlockSpec((tr, gw), lambda j, i: (i, z_col0 + j)))
        args.append(z)
    specs += [pl.BlockSpec((1, gw), lambda j, i: (0, j)), tile]
    args += [g, dy]
    if has_add:
        specs.append(tile)
        args.append(add)
    width = gw * ncol
    out_shape = [jax.ShapeDtypeStruct((rows, width), dx_dtype)]
    out_specs = [tile]
    if gated:
        out_shape.append(jax.ShapeDtypeStruct((rows, width), dx_dtype))
        out_specs.append(tile)
    out_shape.append(jax.ShapeDtypeStruct((1, width), F32))
    out_specs.append(pl.BlockSpec((1, gw), lambda j, i: (0, j)))
    return pl.pallas_call(
        body, name=name, grid=(ncol, rows // tr), in_specs=specs, out_specs=out_specs, out_shape=out_shape,
        compiler_params=_cp(("parallel", "arbitrary")),
    )(*args)


HALO = 8


def _conv_fwd(u, w8, b, *, kw, width, name, u_col0=0, mul_col0=None, out_dtype=F32):
    rows = u.shape[0]
    ts = _pick(rows, 512, 8)
    tc = _pick(width, 512)
    gated = mul_col0 is not None
    c0 = u_col0 // tc
    m0 = (mul_col0 // tc) if gated else 0
    assert u_col0 % tc == 0 and (not gated or mul_col0 % tc == 0)

    def body(*refs):
        if gated:
            cur_ref, halo_ref, mul_ref, w_ref, b_ref, o_ref, ext = refs
        else:
            cur_ref, halo_ref, w_ref, b_ref, o_ref, ext = refs
        i = pl.program_id(0)
        ext[pl.ds(0, HALO), :] = jnp.where(i == 0, 0.0, halo_ref[...])
        ext[pl.ds(HALO, ts), :] = cur_ref[...]
        pre = jnp.zeros((ts, tc), F32) + b_ref[...]
        for k in range(kw):
            pre = pre + w_ref[k:k + 1, :] * ext[pl.ds(HALO - (kw - 1) + k, ts), :]
        act = _silu(pre)
        if gated:
            act = act * mul_ref[...]
        o_ref[...] = act.astype(out_dtype)

    hb = ts // HALO
    specs = [pl.BlockSpec((ts, tc), lambda i, j: (i, c0 + j)),
             pl.BlockSpec((HALO, tc), lambda i, j: (jnp.maximum(i * hb - 1, 0), c0 + j))]
    args = [u, u]
    if gated:
        specs.append(pl.BlockSpec((ts, tc), lambda i, j: (i, m0 + j)))
        args.append(u)
    specs += [pl.BlockSpec((8, tc), lambda i, j: (0, j)), pl.BlockSpec((1, tc), lambda i, j: (0, j))]
    args += [w8, b]
    return pl.pallas_call(
        body, name=name, grid=(rows // ts, width // tc), in_specs=specs,
        out_specs=pl.BlockSpec((ts, tc), lambda i, j: (i, j)),
        out_shape=jax.ShapeDtypeStruct((rows, width), out_dtype),
        scratch_shapes=[pltpu.VMEM((ts + HALO, tc), F32)],
        compiler_params=_cp(("parallel", "parallel")),
    )(*args)


def _conv_bwd(u, w8, b, dact, *, kw, width, name, u_col0=0, mul_col0=None, du_dtype=BF16):
    rows = u.shape[0]
    ts = _pick(rows, 512, 8)
    tc = _pick(width, 512)
    gated = mul_col0 is not None
    c0 = u_col0 // tc
    m0 = (mul_col0 // tc) if gated else 0
    nt = rows // ts
    hb = ts // HALO

    def body(*refs):
        refs = list(refs)
        cur_ref, halo_ref = refs.pop(0), refs.pop(0)
        mul_ref = refs.pop(0) if gated else None
        w_ref, b_ref, da_ref = refs.pop(0), refs.pop(0), refs.pop(0)
        du_ref = refs.pop(0)
        dmul_ref = refs.pop(0) if gated else None
        dwb_ref, ext_u, ext_d = refs
        t = pl.program_id(1)
        ti = nt - 1 - t
        ext_u[pl.ds(0, HALO), :] = jnp.where(ti == 0, 0.0, halo_ref[...])
        ext_u[pl.ds(HALO, ts), :] = cur_ref[...]
        pre = jnp.zeros((ts, tc), F32) + b_ref[...]
        for k in range(kw):
            pre = pre + w_ref[k:k + 1, :] * ext_u[pl.ds(HALO - (kw - 1) + k, ts), :]
        da = da_ref[...].astype(F32)
        if gated:
            mul = mul_ref[...]
            dmul_ref[...] = (da * _silu(pre)).astype(du_dtype)
            dgp = da * mul * _dsilu(pre)
        else:
            dgp = da * _dsilu(pre)

        @pl.when(t == 0)
        def _():
            ext_d[pl.ds(ts, HALO), :] = jnp.zeros((HALO, tc), F32)
            dwb_ref[...] = jnp.zeros((8, tc), F32)

        ext_d[pl.ds(0, ts), :] = dgp
        du = jnp.zeros((ts, tc), F32)
        for k in range(kw):
            du = du + w_ref[k:k + 1, :] * ext_d[pl.ds(kw - 1 - k, ts), :]
        du_ref[...] = du.astype(du_dtype)
        for k in range(kw):
            dwb_ref[k:k + 1, :] += jnp.sum(dgp * ext_u[pl.ds(HALO - (kw - 1) + k, ts), :], axis=0, keepdims=True)
        dwb_ref[7:8, :] += jnp.sum(dgp, axis=0, keepdims=True)
        ext_d[pl.ds(ts, HALO), :] = dgp[0:HALO, :]

    specs = [pl.BlockSpec((ts, tc), lambda j, t: (nt - 1 - t, c0 + j)),
             pl.BlockSpec((HALO, tc), lambda j, t: (jnp.maximum((nt - 1 - t) * hb - 1, 0), c0 + j))]
    args = [u, u]
    if gated:
        specs.append(pl.BlockSpec((ts, tc), lambda j, t: (nt - 1 - t, m0 + j)))
        args.append(u)
    tile = pl.BlockSpec((ts, tc), lambda j, t: (nt - 1 - t, j))
    specs += [pl.BlockSpec((8, tc), lambda j, t: (0, j)), pl.BlockSpec((1, tc), lambda j, t: (0, j)), tile]
    args += [w8, b, dact]
    out_shape = [jax.ShapeDtypeStruct((rows, width), du_dtype)]
    out_specs = [tile]
    if gated:
        out_shape.append(jax.ShapeDtypeStruct((rows, width), du_dtype))
        out_specs.append(tile)
    out_shape.append(jax.ShapeDtypeStruct((8, width), F32))
    out_specs.append(pl.BlockSpec((8, tc), lambda j, t: (0, j)))
    return pl.pallas_call(
        body, name=name, grid=(width // tc, nt), in_specs=specs, out_specs=out_specs, out_shape=out_shape,
        scratch_shapes=[pltpu.VMEM((ts + HALO, tc), F32), pltpu.VMEM((ts + HALO, tc), F32)],
        compiler_params=_cp(("parallel", "arbitrary")),
    )(*args)


GW = SSD_HPG * SSD_HD


def _ssd_common(x, bm, cm, dt_raw, dt_raw_t, bias_r, bias_c, alog_r, alog_c):
    row = lax.broadcasted_iota(jnp.int32, (CHUNK, CHUNK), 0)
    col = lax.broadcasted_iota(jnp.int32, (CHUNK, CHUNK), 1)
    causal = row >= col
    tril = causal.astype(F32)
    triu = (row <= col).astype(F32)
    spread = _group_matrix(GW, SSD_HD, transpose=True)
    dt = _softplus(dt_raw + bias_r)
    dt_t = _softplus(dt_raw_t + bias_c)
    a_r = -jnp.exp(alog_r)
    a_c = -jnp.exp(alog_c)
    acs = _dotf(tril, dt * a_r)
    acs_t = _dotf(dt_t * a_c, triu)
    last = acs[CHUNK - 1:CHUNK, :]
    ds = jnp.exp(last - acs)
    cd = jnp.exp(last)
    c = dict(causal=causal, tril=tril, triu=triu, spread=spread, dt=dt, a_r=a_r, acs=acs, acs_t=acs_t, ds=ds, cd=cd)
    c["eb"] = _dotf(jnp.exp(acs), spread)
    c["dsb"] = _dotf(ds, spread)
    c["cdb"] = _dotf(cd, spread)
    c["dtb"] = _dotf(dt, spread)
    c["xdt"] = x * c["dtb"]
    c["cb"] = _dotb(cm, bm, NT)
    return c


def _ssd_lam(c, r):
    diff = c["acs"][:, r:r + 1] - c["acs_t"][r:r + 1, :]
    return jnp.exp(jnp.where(c["causal"], diff, -jnp.inf))


def _ssd_specs(nc, rev):
    def ci(t):
        return (nc - 1 - t) if rev else t
    xs = pl.BlockSpec((CHUNK, GW), lambda g, t: (ci(t), g))
    bs = pl.BlockSpec((CHUNK, SSD_N), lambda g, t: (ci(t), SSD_DI // SSD_N + g))
    cs = pl.BlockSpec((CHUNK, SSD_N), lambda g, t: (ci(t), SSD_DI // SSD_N + SSD_G + g))
    dts = pl.BlockSpec((1, CHUNK, 8), lambda g, t: (g, ci(t), 0))
    dtts = pl.BlockSpec((1, 8, CHUNK), lambda g, t: (g, 0, ci(t)))
    pr = pl.BlockSpec((1, 1, 8), lambda g, t: (g, 0, 0))
    pc = pl.BlockSpec((1, 8, 1), lambda g, t: (g, 0, 0))
    hs = pl.BlockSpec((1, 1, SSD_N, GW), lambda g, t: (ci(t), g, 0, 0))
    return xs, bs, cs, dts, dtts, pr, pc, hs


def _ssd_fwd(xbc, dtg, dtg_t, bias_r, bias_c, alog_r, alog_c, d_r, *, name):
    s = xbc.shape[0]
    nc = s // CHUNK
    xs, bs, cs, dts, dtts, pr, pc, hs = _ssd_specs(nc, False)

    def body(x_ref, b_ref, c_ref, dt_ref, dtt_ref, br_ref, bc_ref, ar_ref, ac_ref, d_ref, y_ref, hp_ref, h_sc):
        t = pl.program_id(1)

        @pl.when(t == 0)
        def _():
            h_sc[...] = jnp.zeros_like(h_sc)

        x, bm, cm = x_ref[...], b_ref[...], c_ref[...]
        c = _ssd_common(x, bm, cm, dt_ref[0], dtt_ref[0], br_ref[0], bc_ref[0], ar_ref[0], ac_ref[0])
        h = h_sc[...]
        hp_ref[0, 0] = h
        xdt = c["xdt"]
        pieces = []
        for r in range(SSD_HPG):
            m = c["cb"] * _ssd_lam(c, r)
            pieces.append(_dotb(m, xdt[:, r * SSD_HD:(r + 1) * SSD_HD]))
        y = jnp.concatenate(pieces, axis=1) + c["eb"] * _dotb(cm, h) + x * _dotf(d_ref[0], c["spread"])
        y_ref[...] = y
        h_sc[...] = h * c["cdb"] + _dotb(bm, xdt * c["dsb"], TN)

    return pl.pallas_call(
        body, name=name, grid=(SSD_G, nc),
        in_specs=[xs, bs, cs, dts, dtts, pr, pc, pr, pc, pr],
        out_specs=[xs, hs],
        out_shape=[jax.ShapeDtypeStruct((s, SSD_DI), F32), jax.ShapeDtypeStruct((nc, SSD_G, SSD_N, GW), F32)],
        scratch_shapes=[pltpu.VMEM((SSD_N, GW), F32)],
        compiler_params=_cp(("parallel", "arbitrary")),
    )(xbc, xbc, xbc, dtg, dtg_t, bias_r, bias_c, alog_r, alog_c, d_r)


def _ssd_bwd(xbc, dtg, dtg_t, bias_r, bias_c, alog_r, alog_c, d_r, hprev, dy, *, name):
    s = xbc.shape[0]
    nc = s // CHUNK
    xs, bs, cs, dts, dtts, pr, pc, hs = _ssd_specs(nc, True)
    gsum = functools.partial(_group_matrix, GW, SSD_HD)

    def body(x_ref, b_ref, c_ref, dt_ref, dtt_ref, br_ref, bc_ref, ar_ref, ac_ref, d_ref, hp_ref, dy_ref,
             dx_ref, db_ref, dc_ref, ddt_ref, dbias_ref, dalog_ref, dd_ref, dh_sc):
        t = pl.program_id(1)

        @pl.when(t == 0)
        def _():
            dh_sc[...] = jnp.zeros_like(dh_sc)
            dbias_ref[...] = jnp.zeros_like(dbias_ref)
            dalog_ref[...] = jnp.zeros_like(dalog_ref)
            dd_ref[...] = jnp.zeros_like(dd_ref)

        x, bm, cm = x_ref[...], b_ref[...], c_ref[...]
        c = _ssd_common(x, bm, cm, dt_ref[0], dtt_ref[0], br_ref[0], bc_ref[0], ar_ref[0], ac_ref[0])
        lanesum = gsum()
        h = hp_ref[0, 0]
        dh = dh_sc[...]
        dy = dy_ref[...]
        xdt, dsb = c["xdt"], c["dsb"]
        skip = _dotf(d_ref[0], c["spread"])
        dd_ref[0] += jnp.sum(_dotf(dy * x, lanesum), axis=0, keepdims=True)
        dacs = _dotf(dy * (c["eb"] * _dotb(cm, h)), lanesum)
        edy = c["eb"] * dy
        dcm = _dotb(edy, h, NT)
        dh_prev = _dotb(cm, edy, TN)
        bdh = _dotb(bm, dh)
        dxdt = dsb * bdh
        dbm = _dotb(dsb * xdt, dh, NT)
        t1 = _dotf(xdt * bdh, lanesum) * c["ds"]
        dacs = dacs - t1
        dlast = jnp.sum(t1, axis=0, keepdims=True) + jnp.sum(_dotf(dh * h, lanesum), axis=0, keepdims=True) * c["cd"]
        dcb = jnp.zeros((CHUNK, CHUNK), F32)
        pieces = []
        ones8 = jnp.ones((CHUNK, 8), F32)
        head = lax.broadcasted_iota(jnp.int32, (1, 8), 1)
        for r in range(SSD_HPG):
            sl = slice(r * SSD_HD, (r + 1) * SSD_HD)
            lam = _ssd_lam(c, r)
            m = c["cb"] * lam
            dm = _dotb(dy[:, sl], xdt[:, sl], NT)
            dcb = dcb + dm * lam
            gm = dm * m
            dacs = dacs + (jnp.sum(gm, axis=1, keepdims=True) - _dotf(gm, ones8, TN)) * (head == r).astype(F32)
            pieces.append(_dotb(m, dy[:, sl], TN))
        dxdt = dxdt + jnp.concatenate(pieces, axis=1)
        dcm = dcm + _dotb(dcb, bm)
        dbm = dbm + _dotb(dcb, cm, TN)
        dx_ref[...] = dy * skip + dxdt * c["dtb"]
        db_ref[...] = dbm
        dc_ref[...] = dcm
        rowid = lax.broadcasted_iota(jnp.int32, (CHUNK, 8), 0)
        dacs = dacs + jnp.where(rowid == CHUNK - 1, dlast, 0.0)
        dda = _dotf(c["triu"], dacs)
        ddt = _dotf(dxdt * x, lanesum) + dda * c["a_r"]
        ddt_raw = ddt * _sigmoid(dt_ref[0] + br_ref[0])
        ddt_ref[0] = ddt_raw
        dbias_ref[0] += jnp.sum(ddt_raw, axis=0, keepdims=True)
        dalog_ref[0] += jnp.sum(dda * c["dt"], axis=0, keepdims=True) * c["a_r"]
        dh_sc[...] = dh_prev + dh * c["cdb"]

    ci = lambda t: nc - 1 - t
    nspec = pl.BlockSpec((CHUNK, SSD_N), lambda g, t: (ci(t), g))
    return pl.pallas_call(
        body, name=name, grid=(SSD_G, nc),
        in_specs=[xs, bs, cs, dts, dtts, pr, pc, pr, pc, pr, hs, xs],
        out_specs=[xs, nspec, nspec, dts, pr, pr, pr],
        out_shape=[jax.ShapeDtypeStruct((s, SSD_DI), F32), jax.ShapeDtypeStruct((s, SSD_G * SSD_N), F32),
                   jax.ShapeDtypeStruct((s, SSD_G * SSD_N), F32), jax.ShapeDtypeStruct((SSD_G, s, 8), F32),
                   jax.ShapeDtypeStruct((SSD_G, 1, 8), F32), jax.ShapeDtypeStruct((SSD_G, 1, 8), F32),
                   jax.ShapeDtypeStruct((SSD_G, 1, 8), F32)],
        scratch_shapes=[pltpu.VMEM((SSD_N, GW), F32)],
        compiler_params=_cp(("parallel", "arbitrary")),
    )(xbc, xbc, xbc, dtg, dtg_t, bias_r, bias_c, alog_r, alog_c, d_r, hprev, dy)


FOX_PAIRS = FOX_H // 2
FOX_SCALE = FOX_HD ** -0.5
NEG_INF = -jnp.inf


def _fgate_fwd(f_t, b_c, *, name):
    hh, s = f_t.shape
    tb = _pick(s, 512)
    nb = s // tb

    def body(f_ref, b_ref, o_ref, carry):
        t = pl.program_id(0)

        @pl.when(t == 0)
        def _():
            carry[...] = jnp.zeros_like(carry)

        lf = -_softplus(-(f_ref[...] + b_ref[...]))
        row = lax.broadcasted_iota(jnp.int32, (tb, tb), 0)
        col = lax.broadcasted_iota(jnp.int32, (tb, tb), 1)
        cum = _dotf(lf, (row <= col).astype(F32)) + carry[:, 0:1]
        o_ref[...] = cum
        carry[:, 0:1] = cum[:, tb - 1:tb]

    return pl.pallas_call(
        body, name=name, grid=(nb,),
        in_specs=[pl.BlockSpec((hh, tb), lambda t: (0, t)), pl.BlockSpec((hh, 1), lambda t: (0, 0))],
        out_specs=pl.BlockSpec((hh, tb), lambda t: (0, t)),
        out_shape=jax.ShapeDtypeStruct((hh, s), F32),
        scratch_shapes=[pltpu.VMEM((hh, LANES), F32)],
        compiler_params=_cp(("arbitrary",)),
    )(f_t, b_c)


def _fgate_bwd(dcum_q_t, dcum_k_t, f_t, b_c, *, name):
    hh, s = f_t.shape
    tb = _pick(s, 512)
    nb = s // tb

    def body(dq_ref, d_ref, f_ref, b_ref, df_ref, db_ref, carry):
        t = pl.program_id(0)

        @pl.when(t == 0)
        def _():
            carry[...] = jnp.zeros_like(carry)
            db_ref[...] = jnp.zeros_like(db_ref)

        d = d_ref[...] + dq_ref[...]
        row = lax.broadcasted_iota(jnp.int32, (tb, tb), 0)
        col = lax.broadcasted_iota(jnp.int32, (tb, tb), 1)
        rev = _dotf(d, (row >= col).astype(F32)) + carry[:, 0:1]
        df = rev * _sigmoid(-(f_ref[...] + b_ref[...]))
        df_ref[...] = df
        db_ref[...] += jnp.sum(df, axis=1, keepdims=True)
        carry[:, 0:1] = rev[:, 0:1]

    blk = pl.BlockSpec((hh, tb), lambda t: (0, nb - 1 - t))
    return pl.pallas_call(
        body, name=name, grid=(nb,),
        in_specs=[blk, blk, blk, pl.BlockSpec((hh, 1), lambda t: (0, 0))],
        out_specs=[blk, pl.BlockSpec((hh, 1), lambda t: (0, 0))],
        out_shape=[jax.ShapeDtypeStruct((hh, s), F32), jax.ShapeDtypeStruct((hh, 1), F32)],
        scratch_shapes=[pltpu.VMEM((hh, LANES), F32)],
        compiler_params=_cp(("arbitrary",)),
    )(dcum_q_t, dcum_k_t, f_t, b_c)


def _fox_tile(s):
    return _pick(s, 256, 8)


def _fox_scores(q_ref, k_ref, cq_ref, ck_ref, hh, diag_or_below):
    sl = slice(hh * FOX_HD, (hh + 1) * FOX_HD)
    sc = _dotb(q_ref[:, sl], k_ref[:, sl], NT) * FOX_SCALE + cq_ref[0][:, hh:hh + 1] - ck_ref[0][hh:hh + 1, :]
    row = lax.broadcasted_iota(jnp.int32, sc.shape, 0)
    col = lax.broadcasted_iota(jnp.int32, sc.shape, 1)
    return jnp.where(jnp.logical_or(row >= col, diag_or_below), sc, NEG_INF)


def _flash_fwd(qn, kn, qkvg, cq, ck, *, name):
    s = qn.shape[0]
    tt = _fox_tile(s)
    nq = s // tt
    v0 = 2 * FOX_D // LANES

    def body(q_ref, k_ref, v_ref, cq_ref, ck_ref, o_ref, lse_ref, m_sc, l_sc, acc_sc):
        i, j = pl.program_id(1), pl.program_id(2)

        @pl.when(j == 0)
        def _():
            m_sc[...] = jnp.full_like(m_sc, NEG_INF)
            l_sc[...] = jnp.zeros_like(l_sc)
            acc_sc[...] = jnp.zeros_like(acc_sc)

        @pl.when(j <= i)
        def _():
            for hh in range(2):
                sl = slice(hh * FOX_HD, (hh + 1) * FOX_HD)
                sc = _fox_scores(q_ref, k_ref, cq_ref, ck_ref, hh, j < i)
                m_prev = m_sc[hh]
                m_new = jnp.maximum(m_prev, jnp.max(sc, axis=1, keepdims=True))
                alpha = jnp.exp(m_prev - m_new)
                p = jnp.exp(sc - m_new)
                l_sc[hh] = alpha * l_sc[hh] + jnp.sum(p, axis=1, keepdims=True)
                acc_sc[hh] = alpha * acc_sc[hh] + _dotb(p, v_ref[:, sl])
                m_sc[hh] = m_new

        @pl.when(j == i)
        def _():
            o_ref[...] = jnp.concatenate([acc_sc[hh] / l_sc[hh] for hh in range(2)], axis=1)
            lse_ref[0] = jnp.concatenate([m_sc[hh] + jnp.log(l_sc[hh]) for hh in range(2)], axis=1)

    kj = lambda i, j: jnp.minimum(i, j)
    return pl.pallas_call(
        body, name=name, grid=(FOX_PAIRS, nq, nq),
        in_specs=[pl.BlockSpec((tt, LANES), lambda p, i, j: (i, p)),
                  pl.BlockSpec((tt, LANES), lambda p, i, j: (kj(i, j), p)),
                  pl.BlockSpec((tt, LANES), lambda p, i, j: (kj(i, j), v0 + p)),
                  pl.BlockSpec((1, tt, 2), lambda p, i, j: (p, i, 0)),
                  pl.BlockSpec((1, 2, tt), lambda p, i, j: (p, 0, kj(i, j)))],
        out_specs=[pl.BlockSpec((tt, LANES), lambda p, i, j: (i, p)),
                   pl.BlockSpec((1, tt, 2), lambda p, i, j: (p, i, 0))],
        out_shape=[jax.ShapeDtypeStruct((s, FOX_D), F32), jax.ShapeDtypeStruct((FOX_PAIRS, s, 2), F32)],
        scratch_shapes=[pltpu.VMEM((2, tt, 1), F32), pltpu.VMEM((2, tt, 1), F32), pltpu.VMEM((2, tt, FOX_HD), F32)],
        compiler_params=_cp(("parallel", "parallel", "arbitrary")),
    )(qn, kn, qkvg, cq, ck)


def _flash_dq(qn, kn, qkvg, do, lse, delta, cq, ck, *, name):
    s = qn.shape[0]
    tt = _fox_tile(s)
    nq = s // tt
    v0 = 2 * FOX_D // LANES

    def body(q_ref, k_ref, v_ref, do_ref, lse_ref, dl_ref, cq_ref, ck_ref, dq_ref, dc_ref, acc_sc, rs_sc):
        i, j = pl.program_id(1), pl.program_id(2)

        @pl.when(j == 0)
        def _():
            acc_sc[...] = jnp.zeros_like(acc_sc)
            rs_sc[...] = jnp.zeros_like(rs_sc)

        @pl.when(j <= i)
        def _():
            for hh in range(2):
                sl = slice(hh * FOX_HD, (hh + 1) * FOX_HD)
                sc = _fox_scores(q_ref, k_ref, cq_ref, ck_ref, hh, j < i)
                p = jnp.exp(sc - lse_ref[0][:, hh:hh + 1])
                dp = _dotb(do_ref[:, sl], v_ref[:, sl], NT)
                ds = p * (dp - dl_ref[0][:, hh:hh + 1])
                acc_sc[hh] += _dotb(ds, k_ref[:, sl])
                rs_sc[hh] += jnp.sum(ds, axis=1, keepdims=True)

        @pl.when(j == i)
        def _():
            dq_ref[...] = jnp.concatenate([acc_sc[hh] * FOX_SCALE for hh in range(2)], axis=1)
            dc_ref[0] = jnp.concatenate([rs_sc[hh] for hh in range(2)], axis=1)

    kj = lambda i, j: jnp.minimum(i, j)
    qside = pl.BlockSpec((tt, LANES), lambda p, i, j: (i, p))
    stat = pl.BlockSpec((1, tt, 2), lambda p, i, j: (p, i, 0))
    return pl.pallas_call(
        body, name=name, grid=(FOX_PAIRS, nq, nq),
        in_specs=[qside,
                  pl.BlockSpec((tt, LANES), lambda p, i, j: (kj(i, j), p)),
                  pl.BlockSpec((tt, LANES), lambda p, i, j: (kj(i, j), v0 + p)),
                  qside, stat, stat, stat,
                  pl.BlockSpec((1, 2, tt), lambda p, i, j: (p, 0, kj(i, j)))],
        out_specs=[qside, stat],
        out_shape=[jax.ShapeDtypeStruct((s, FOX_D), F32), jax.ShapeDtypeStruct((FOX_PAIRS, s, 2), F32)],
        scratch_shapes=[pltpu.VMEM((2, tt, FOX_HD), F32), pltpu.VMEM((2, tt, 1), F32)],
        compiler_params=_cp(("parallel", "parallel", "arbitrary")),
    )(qn, kn, qkvg, do, lse, delta, cq, ck)


def _flash_dkv(qn, kn, qkvg, do, lse, delta, cq, ck, *, name):
    s = qn.shape[0]
    tt = _fox_tile(s)
    nq = s // tt
    v0 = 2 * FOX_D // LANES

    def body(q_ref, k_ref, v_ref, do_ref, lse_ref, dl_ref, cq_ref, ck_ref, dk_ref, dv_ref, dc_ref, dk_sc, dv_sc, dc_sc):
        j, i = pl.program_id(1), pl.program_id(2)

        @pl.when(i == 0)
        def _():
            dk_sc[...] = jnp.zeros_like(dk_sc)
            dv_sc[...] = jnp.zeros_like(dv_sc)
            dc_sc[...] = jnp.zeros_like(dc_sc)

        @pl.when(i >= j)
        def _():
            for hh in range(2):
                sl = slice(hh * FOX_HD, (hh + 1) * FOX_HD)
                sc = _fox_scores(q_ref, k_ref, cq_ref, ck_ref, hh, i > j)
                p = jnp.exp(sc - lse_ref[0][:, hh:hh + 1])
                dp = _dotb(do_ref[:, sl], v_ref[:, sl], NT)
                ds = p * (dp - dl_ref[0][:, hh:hh + 1])
                dv_sc[hh] += _dotb(p, do_ref[:, sl], TN)
                dk_sc[hh] += _dotb(ds, q_ref[:, sl], TN)
                dc_sc[hh] += jnp.sum(ds, axis=0, keepdims=True)

        @pl.when(i == nq - 1)
        def _():
            dk_ref[...] = jnp.concatenate([dk_sc[hh] * FOX_SCALE for hh in range(2)], axis=1)
            dv_ref[...] = jnp.concatenate([dv_sc[hh] for hh in range(2)], axis=1)
            dc_ref[0] = -jnp.concatenate([dc_sc[hh] for hh in range(2)], axis=0)

    qi = lambda j, i: jnp.maximum(i, j)
    qside = pl.BlockSpec((tt, LANES), lambda p, j, i: (qi(j, i), p))
    stat = pl.BlockSpec((1, tt, 2), lambda p, j, i: (p, qi(j, i), 0))
    kside = pl.BlockSpec((tt, LANES), lambda p, j, i: (j, p))
    ckspec = pl.BlockSpec((1, 2, tt), lambda p, j, i: (p, 0, j))
    return pl.pallas_call(
        body, name=name, grid=(FOX_PAIRS, nq, nq),
        in_specs=[qside, kside, pl.BlockSpec((tt, LANES), lambda p, j, i: (j, v0 + p)), qside, stat, stat, stat, ckspec],
        out_specs=[kside, kside, ckspec],
        out_shape=[jax.ShapeDtypeStruct((s, FOX_D), F32), jax.ShapeDtypeStruct((s, FOX_D), F32),
                   jax.ShapeDtypeStruct((FOX_PAIRS, 2, s), F32)],
        scratch_shapes=[pltpu.VMEM((2, tt, FOX_HD), F32), pltpu.VMEM((2, tt, FOX_HD), F32), pltpu.VMEM((2, 1, tt), F32)],
        compiler_params=_cp(("parallel", "parallel", "arbitrary")),
    )(qn, kn, qkvg, do, lse, delta, cq, ck)


def _ogate_fwd(o, qkvg, *, name):
    s = o.shape[0]
    tr = _pick(s, 512, 8)

    def body(o_ref, g_ref, out_ref):
        out_ref[...] = (o_ref[...] * _sigmoid(g_ref[...])).astype(BF16)

    tile = pl.BlockSpec((tr, FOX_D), lambda i: (i, 0))
    return pl.pallas_call(
        body, name=name, grid=(s // tr,), in_specs=[tile, pl.BlockSpec((tr, FOX_D), lambda i: (i, 3))],
        out_specs=tile, out_shape=jax.ShapeDtypeStruct((s, FOX_D), BF16), compiler_params=_cp(("parallel",)),
    )(o, qkvg)


def _ogate_bwd(dog, o, qkvg, *, name):
    s = o.shape[0]
    tr = _pick(s, 512, 8)

    def body(dog_ref, o_ref, g_ref, do_ref, dg_ref, dl_ref):
        sg = _sigmoid(g_ref[...])
        ov = o_ref[...]
        dog_v = dog_ref[...]
        do = dog_v * sg
        do_ref[...] = do
        dg_ref[...] = (dog_v * ov * sg * (1.0 - sg)).astype(BF16)
        dl_ref[...] = _dotf(do * ov, _group_matrix(FOX_D, FOX_HD))

    tile = pl.BlockSpec((tr, FOX_D), lambda i: (i, 0))
    return pl.pallas_call(
        body, name=name, grid=(s // tr,), in_specs=[tile, tile, pl.BlockSpec((tr, FOX_D), lambda i: (i, 3))],
        out_specs=[tile, tile, pl.BlockSpec((tr, FOX_H), lambda i: (i, 0))],
        out_shape=[jax.ShapeDtypeStruct((s, FOX_D), F32), jax.ShapeDtypeStruct((s, FOX_D), BF16),
                   jax.ShapeDtypeStruct((s, FOX_H), F32)],
        compiler_params=_cp(("parallel",)),
    )(dog, o, qkvg)


def _loss_head(h, g, target, *, name):
    s, d = h.shape
    tr = _pick(s, 512, 8)

    def body(h_ref, g_ref, t_ref, loss_ref, dh_ref, dg_ref):
        i = pl.program_id(0)
        x = h_ref[...]
        gv = g_ref[...]
        r = lax.rsqrt(jnp.mean(x * x, axis=-1, keepdims=True) + EPS)
        xh = x * r
        err = xh * gv - t_ref[...]
        part = 0.5 * jnp.sum(jnp.sum(err * err, axis=1, keepdims=True) * (1.0 / d), axis=0, keepdims=True)
        dy = err * (1.0 / d)
        dyg = dy * gv
        dh_ref[...] = r * (dyg - xh * jnp.mean(dyg * xh, axis=-1, keepdims=True))
        dgp = jnp.sum(dy * xh, axis=0, keepdims=True)

        @pl.when(i == 0)
        def _():
            loss_ref[...] = jnp.zeros_like(loss_ref) + part
            dg_ref[...] = dgp

        @pl.when(i > 0)
        def _():
            loss_ref[...] += part
            dg_ref[...] += dgp

    tile = pl.BlockSpec((tr, d), lambda i: (i, 0))
    vec = pl.BlockSpec((1, d), lambda i: (0, 0))
    return pl.pallas_call(
        body, name=name, grid=(s // tr,), in_specs=[tile, vec, tile],
        out_specs=[pl.BlockSpec((1, LANES), lambda i: (0, 0)), tile, vec],
        out_shape=[jax.ShapeDtypeStruct((1, LANES), F32), jax.ShapeDtypeStruct((s, d), F32),
                   jax.ShapeDtypeStruct((1, d), F32)],
        compiler_params=_cp(("arbitrary",)),
    )(h, g, target)


def _adamw(w, g, m, v, *, name):
    rows, cols = w.shape
    tr = _pick(rows, 256, 8)
    c1 = 1.0 - ADAM_B1 ** ADAM_STEP
    c2 = 1.0 - ADAM_B2 ** ADAM_STEP

    def body(w_ref, g_ref, m_ref, v_ref, d_ref, nm_ref, nv_ref):
        gv = g_ref[...]
        nm = ADAM_B1 * m_ref[...] + (1.0 - ADAM_B1) * gv
        nv = ADAM_B2 * v_ref[...] + (1.0 - ADAM_B2) * (gv * gv)
        d_ref[...] = -ADAM_LR * ((nm / c1) / (jnp.sqrt(nv / c2) + ADAM_EPS) + ADAM_WD * w_ref[...])
        nm_ref[...] = nm
        nv_ref[...] = nv

    tile = pl.BlockSpec((tr, cols), lambda i: (i, 0))
    shp = jax.ShapeDtypeStruct((rows, cols), F32)
    return pl.pallas_call(
        body, name=name, grid=(rows // tr,), in_specs=[tile] * 4, out_specs=[tile] * 3, out_shape=[shp] * 3,
        compiler_params=_cp(("parallel",)),
    )(w, g, m, v)


ANY = pl.BlockSpec(memory_space=pl.ANY)
N_DEV = 8


def _coords():
    return lax.axis_index("x"), lax.axis_index("y"), lax.axis_index("c")


def _other_chips(x, y):
    return [(1 - x, y), (x, 1 - y), (1 - x, 1 - y)]


def _allgather_small(buf, *, name, with_sum):
    rows = buf.shape[0]

    def body(*refs):
        if with_sum:
            x_ref, out_ref, sum_ref, send_sems, recv_sems = refs
        else:
            x_ref, out_ref, send_sems, recv_sems = refs
        x, y, c = _coords()
        me = 4 * x + 2 * y + c
        out_ref[me] = x_ref[...]
        copies = []
        for rel in range(1, N_DEV):
            px = (1 - x) if rel & 4 else x
            py = (1 - y) if rel & 2 else y
            pc = (1 - c) if rel & 1 else c
            cp = pltpu.make_async_remote_copy(
                src_ref=x_ref, dst_ref=out_ref.at[me], send_sem=send_sems.at[rel - 1], recv_sem=recv_sems.at[rel - 1],
                device_id=(px, py, pc), device_id_type=MESH)
            cp.start()
            copies.append(cp)
        for cp in copies:
            cp.wait()
        if with_sum:
            acc = out_ref[0]
            for k in range(1, N_DEV):
                acc = acc + out_ref[k]
            sum_ref[...] = acc

    slots = jax.ShapeDtypeStruct((N_DEV, rows, LANES), F32)
    vm = pl.BlockSpec(memory_space=pltpu.VMEM)
    out_shape = [slots, jax.ShapeDtypeStruct((rows, LANES), F32)] if with_sum else [slots]
    return pl.pallas_call(
        body, name=name, in_specs=[vm], out_specs=[vm] * len(out_shape), out_shape=out_shape,
        scratch_shapes=[pltpu.SemaphoreType.DMA((N_DEV - 1,)), pltpu.SemaphoreType.DMA((N_DEV - 1,))],
    )(buf)


def _allgather_chips(shards, *, name):
    n = len(shards)

    def body(*refs):
        ins, outs = refs[:n], refs[n:2 * n]
        send_sems, recv_sems, local_sems = refs[2 * n:]
        x, y, c = _coords()
        k = 2 * x + y
        chips = _other_chips(x, y)
        sibling = (x, y, 1 - c)
        local = []
        for t in range(n):
            cp = pltpu.make_async_copy(ins[t], outs[t].at[k], local_sems.at[t])
            cp.start()
            local.append(cp)
        sends = []
        for t in range(n):
            for j, (px, py) in enumerate(chips):
                cp = pltpu.make_async_remote_copy(
                    src_ref=ins[t].at[c], dst_ref=outs[t].at[k, c], send_sem=send_sems.at[6 * t + j],
                    recv_sem=recv_sems.at[6 * t + j], device_id=(px, py, c), device_id_type=MESH)
                cp.start()
                sends.append(cp)
        for t in range(n):
            for j, (px, py) in enumerate(chips):
                kj = 2 * px + py
                pltpu.make_async_remote_copy(
                    src_ref=ins[t].at[c], dst_ref=outs[t].at[kj, c], send_sem=send_sems.at[6 * t + j],
                    recv_sem=recv_sems.at[6 * t + j], device_id=(px, py, c), device_id_type=MESH).wait_recv()
                cp = pltpu.make_async_remote_copy(
                    src_ref=outs[t].at[kj, c], dst_ref=outs[t].at[kj, c], send_sem=send_sems.at[6 * t + 3 + j],
                    recv_sem=recv_sems.at[6 * t + 3 + j], device_id=sibling, device_id_type=MESH)
                cp.start()
                sends.append(cp)
        for t in range(n):
            for j, (px, py) in enumerate(chips):
                kj = 2 * px + py
                pltpu.make_async_remote_copy(
                    src_ref=outs[t].at[kj, 1 - c], dst_ref=outs[t].at[kj, 1 - c], send_sem=send_sems.at[6 * t + 3 + j],
                    recv_sem=recv_sems.at[6 * t + 3 + j], device_id=sibling, device_id_type=MESH).wait_recv()
        for cp in sends:
            cp.wait_send()
        for cp in local:
            cp.wait()

    return pl.pallas_call(
        body, name=name, in_specs=[ANY] * n, out_specs=[ANY] * n,
        out_shape=[jax.ShapeDtypeStruct((4,) + s.shape, s.dtype) for s in shards],
        scratch_shapes=[pltpu.SemaphoreType.DMA((6 * n,)), pltpu.SemaphoreType.DMA((6 * n,)), pltpu.SemaphoreType.DMA((n,))],
    )(*shards)


def _sibling_swap(arrs, *, name, other_half):
    n = len(arrs)

    def body(*refs):
        ins, outs = refs[:n], refs[n:2 * n]
        send_sems, recv_sems = refs[2 * n:]
        x, y, c = _coords()
        copies = []
        for t in range(n):
            cp = pltpu.make_async_remote_copy(
                src_ref=ins[t].at[1 - c] if other_half else ins[t], dst_ref=outs[t], send_sem=send_sems.at[t],
                recv_sem=recv_sems.at[t], device_id=(x, y, 1 - c), device_id_type=MESH)
            cp.start()
            copies.append(cp)
        for cp in copies:
            cp.wait()

    return pl.pallas_call(
        body, name=name, in_specs=[ANY] * n, out_specs=[ANY] * n,
        out_shape=[jax.ShapeDtypeStruct(a.shape[1:] if other_half else a.shape, a.dtype) for a in arrs],
        scratch_shapes=[pltpu.SemaphoreType.DMA((n,)), pltpu.SemaphoreType.DMA((n,))],
    )(*arrs)


def _chip_exchange(arrs, *, name):
    n = len(arrs)

    def body(*refs):
        ins, outs = refs[:n], refs[n:2 * n]
        send_sems, recv_sems = refs[2 * n:]
        x, y, c = _coords()
        copies = []
        for t in range(n):
            for j, (px, py) in enumerate(_other_chips(x, y)):
                cp = pltpu.make_async_remote_copy(
                    src_ref=ins[t].at[2 * px + py], dst_ref=outs[t].at[j], send_sem=send_sems.at[3 * t + j],
                    recv_sem=recv_sems.at[3 * t + j], device_id=(px, py, c), device_id_type=MESH)
                cp.start()
                copies.append(cp)
        for cp in copies:
            cp.wait()

    return pl.pallas_call(
        body, name=name, in_specs=[ANY] * n, out_specs=[ANY] * n,
        out_shape=[jax.ShapeDtypeStruct((3,) + a.shape[1:], a.dtype) for a in arrs],
        scratch_shapes=[pltpu.SemaphoreType.DMA((3 * n,)), pltpu.SemaphoreType.DMA((3 * n,))],
    )(*arrs)


def _add_selected(stack, others, sel, *, name):
    _, m, cols = stack.shape
    q = others.shape[0]
    tr = _pick(m, 256, 8)

    def body(sel_ref, s_ref, o_ref, out_ref):
        acc = s_ref[0]
        for i in range(q):
            acc = acc + o_ref[i]
        out_ref[...] = acc

    return pl.pallas_call(
        body, name=name,
        grid_spec=pltpu.PrefetchScalarGridSpec(
            num_scalar_prefetch=1, grid=(m // tr,),
            in_specs=[pl.BlockSpec((1, tr, cols), lambda i, sel_ref: (sel_ref[0], i, 0)),
                      pl.BlockSpec((q, tr, cols), lambda i, sel_ref: (0, i, 0))],
            out_specs=pl.BlockSpec((tr, cols), lambda i, sel_ref: (i, 0))),
        out_shape=jax.ShapeDtypeStruct((m, cols), F32),
        compiler_params=_cp(("parallel",)),
    )(sel, stack, others)


def _reduce_scatter(grads, c, k):
    n = len(grads)
    csel, ksel = jnp.reshape(c, (1,)).astype(jnp.int32), jnp.reshape(k, (1,)).astype(jnp.int32)
    from_sib = _sibling_swap(grads, name="rs_pair_swap", other_half=True)
    chip_sums = []
    for t in range(n):
        _, _, m, cols = grads[t].shape
        mine = grads[t].reshape(2, 4 * m, cols)
        s = _add_selected(mine, from_sib[t].reshape(1, 4 * m, cols), csel, name=f"rs_pair_add{t}")
        chip_sums.append(s.reshape(4, m, cols))
    from_chips = _chip_exchange(chip_sums, name="rs_chip_exchange")
    finals = [_add_selected(chip_sums[t], from_chips[t], ksel, name=f"rs_chip_add{t}") for t in range(n)]
    others = _sibling_swap(finals, name="rs_result_swap", other_half=False)
    return finals, others


BIG = ("ssd_w_in", "ssd_w_out", "fox_w_in", "fox_w_out", "ffn_w_up", "ffn_w_down")
COL_SHARDED = ("ssd_w_in", "fox_w_in", "ffn_w_up")
SMALL = (("mix_norm_g", (4, 1024)), ("ffn_norm_g", (4, 1024)), ("ssd_conv_w", (2, 4, 3072)), ("ssd_conv_b", (2, 3072)),
         ("ssd_dt_bias", (2, 32)), ("ssd_a_log", (2, 32)), ("ssd_d", (2, 32)), ("ssd_norm_g", (2, 2048)),
         ("fox_b_f", (2, 16)), ("fox_q_norm_g", (2, 64)), ("fox_k_norm_g", (2, 64)), ("ffn_conv_w", (4, 3, 2816)),
         ("ffn_conv_b", (4, 2816)), ("final_norm_g", (1024,)), ("loss", (1,)))
NAMES = ("mix_norm_g", "ffn_norm_g", "ssd_w_in", "ssd_conv_w", "ssd_conv_b", "ssd_dt_bias", "ssd_a_log", "ssd_d",
         "ssd_norm_g", "ssd_w_out", "fox_w_in", "fox_b_f", "fox_q_norm_g", "fox_k_norm_g", "fox_w_out", "ffn_w_up",
         "ffn_conv_w", "ffn_conv_b", "ffn_w_down", "final_norm_g")


def _pack(parts):
    flat = jnp.concatenate([jnp.reshape(p, (-1,)).astype(F32) for p in parts])
    rows = -(-flat.shape[0] // (8 * LANES)) * 8
    return jnp.pad(flat, (0, rows * LANES - flat.shape[0])).reshape(rows, LANES)


def _unpack(buf, shapes):
    flat = buf.reshape(-1)
    out, off = [], 0
    for shp in shapes:
        size = 1
        for d in shp:
            size *= d
        out.append(flat[off:off + size].reshape(shp))
        off += size
    return out


def _pad_lanes(a):
    return jnp.pad(a, ((0, 0), (0, LANES - a.shape[1])))


def _pad8(w):
    return jnp.pad(w, ((0, 8 - w.shape[0]), (0, 0)))


def _ssd_forward(h, p, name):
    s = h.shape[0]
    hn = _rms_fwd(h, p["mix_g"], gw=D_MODEL, ncol=1, name=f"{name}_norm")
    zx = _matmul(hn, p["w_zx"], mode="nn", name=f"{name}_proj")
    dtp = _matmul(hn, p["w_dt"], mode="nn", name=f"{name}_proj_dt")
    xbc = _conv_fwd(zx, p["conv_w8"], p["conv_b"], kw=SSD_K, width=SSD_CONV_DIM, u_col0=SSD_DI, name=f"{name}_conv")
    dt3 = dtp[:, :SSD_H].reshape(s, SSD_G, SSD_HPG)
    dtg, dtg_t = jnp.transpose(dt3, (1, 0, 2)), jnp.transpose(dt3, (1, 2, 0))
    sp = (p["bias_r"], p["bias_c"], p["alog_r"], p["alog_c"], p["d_r"])
    y, hprev = _ssd_fwd(xbc, dtg, dtg_t, *sp, name=f"{name}_scan")
    y2 = _rms_fwd(y, p["norm_g"], gw=SSD_DI // SSD_G, ncol=SSD_G, z=zx, name=f"{name}_gnorm")
    out = _matmul(y2, p["w_out"], mode="nn", add=h, name=f"{name}_out")
    return out, dict(h=h, hn=hn, zx=zx, xbc=xbc, dtg=dtg, dtg_t=dtg_t, y=y, hprev=hprev, y2=y2)


def _ssd_backward(dh1, p, a, name):
    s = dh1.shape[0]
    g = {}
    dy2 = _matmul(dh1, p["w_out"], mode="nt", name=f"{name}_out_dx")
    g["w_out"] = _matmul(a["y2"], dh1, mode="tn", name=f"{name}_out_dw")
    dy, dz, g["norm_g"] = _rms_bwd(a["y"], p["norm_g"], dy2, gw=SSD_DI // SSD_G, ncol=SSD_G, z=a["zx"], name=f"{name}_gnorm_b")
    sp = (p["bias_r"], p["bias_c"], p["alog_r"], p["alog_c"], p["d_r"])
    dx, dbm, dcm, ddt, g["dt_bias"], g["a_log"], g["d"] = _ssd_bwd(
        a["xbc"], a["dtg"], a["dtg_t"], *sp, a["hprev"], dy, name=f"{name}_scan_b")
    dact = jnp.concatenate([dx, dbm, dcm], axis=1)
    dxbc, dwb = _conv_bwd(a["zx"], p["conv_w8"], p["conv_b"], dact, kw=SSD_K, width=SSD_CONV_DIM, u_col0=SSD_DI,
                          name=f"{name}_conv_b")
    g["conv_w"], g["conv_b"] = dwb[:SSD_K], dwb[7]
    dzx = jnp.concatenate([dz.astype(BF16), dxbc], axis=1)
    ddtp = _pad_lanes(jnp.transpose(ddt, (1, 0, 2)).reshape(s, SSD_H))
    dhn = _matmul(dzx, p["w_zx"], mode="nt", name=f"{name}_proj_dx")
    dhn = _matmul(ddtp, p["w_dt"], mode="nt", add=dhn, name=f"{name}_proj_dt_dx")
    dw_zx = _matmul(a["hn"], dzx, mode="tn", name=f"{name}_proj_dw")
    dw_dt = _matmul(a["hn"], ddtp, mode="tn", name=f"{name}_proj_dt_dw")
    g["w_in"] = jnp.concatenate([dw_zx, dw_dt[:, :SSD_H]], axis=1)
    dh, g["mix_g"] = _rms_bwd(a["h"], p["mix_g"], dhn, gw=D_MODEL, ncol=1, add=dh1, name=f"{name}_norm_b")
    return dh, g


def _fox_forward(h, p, name):
    s = h.shape[0]
    hn = _rms_fwd(h, p["mix_g"], gw=D_MODEL, ncol=1, name=f"{name}_norm")
    qkvg = _matmul(hn, p["w_qkvg"], mode="nn", name=f"{name}_proj")
    fp = _matmul(hn, p["w_f"], mode="nn", name=f"{name}_proj_f")
    qn = _rms_fwd(qkvg, p["gq"], gw=FOX_D, ncol=1, x_col0=0, sub=FOX_HD, name=f"{name}_qnorm")
    kn = _rms_fwd(qkvg, p["gk"], gw=FOX_D, ncol=1, x_col0=1, sub=FOX_HD, name=f"{name}_knorm")
    f_t = jnp.transpose(fp[:, :FOX_H])
    cum_t = _fgate_fwd(f_t, p["b_f"], name=f"{name}_fgate")
    ck = cum_t.reshape(FOX_PAIRS, 2, s)
    cq = jnp.transpose(ck, (0, 2, 1))
    o, lse = _flash_fwd(qn, kn, qkvg, cq, ck, name=f"{name}_attn")
    og = _ogate_fwd(o, qkvg, name=f"{name}_ogate")
    out = _matmul(og, p["w_out"], mode="nn", add=h, name=f"{name}_out")
    return out, dict(h=h, hn=hn, qkvg=qkvg, qn=qn, kn=kn, f_t=f_t, cq=cq, ck=ck, o=o, lse=lse, og=og)


def _fox_backward(dh1, p, a, name):
    s = dh1.shape[0]
    g = {}
    dog = _matmul(dh1, p["w_out"], mode="nt", name=f"{name}_out_dx")
    g["w_out"] = _matmul(a["og"], dh1, mode="tn", name=f"{name}_out_dw")
    do, dgate, delta = _ogate_bwd(dog, a["o"], a["qkvg"], name=f"{name}_ogate_b")
    dl = jnp.transpose(delta.reshape(s, FOX_PAIRS, 2), (1, 0, 2))
    fa = (a["qn"], a["kn"], a["qkvg"], do, a["lse"], dl, a["cq"], a["ck"])
    dq, dcq = _flash_dq(*fa, name=f"{name}_attn_dq")
    dk, dv, dck = _flash_dkv(*fa, name=f"{name}_attn_dkv")
    dq_raw, dgq = _rms_bwd(a["qkvg"], p["gq"], dq, gw=FOX_D, ncol=1, x_col0=0, sub=FOX_HD, dx_dtype=BF16, name=f"{name}_qnorm_b")
    dk_raw, dgk = _rms_bwd(a["qkvg"], p["gk"], dk, gw=FOX_D, ncol=1, x_col0=1, sub=FOX_HD, dx_dtype=BF16, name=f"{name}_knorm_b")
    g["gq"] = dgq.reshape(FOX_H, FOX_HD).sum(axis=0)
    g["gk"] = dgk.reshape(FOX_H, FOX_HD).sum(axis=0)
    dcq_t = jnp.transpose(dcq, (0, 2, 1)).reshape(FOX_H, s)
    df_t, dbf = _fgate_bwd(dcq_t, dck.reshape(FOX_H, s), a["f_t"], p["b_f"], name=f"{name}_fgate_b")
    g["b_f"] = dbf[:, 0]
    dproj = jnp.concatenate([dq_raw, dk_raw, dv.astype(BF16), dgate], axis=1)
    dfp = _pad_lanes(jnp.transpose(df_t))
    dhn = _matmul(dproj, p["w_qkvg"], mode="nt", name=f"{name}_proj_dx")
    dhn = _matmul(dfp, p["w_f"], mode="nt", add=dhn, name=f"{name}_proj_f_dx")
    dw_qkvg = _matmul(a["hn"], dproj, mode="tn", name=f"{name}_proj_dw")
    dw_f = _matmul(a["hn"], dfp, mode="tn", name=f"{name}_proj_f_dw")
    g["w_in"] = jnp.concatenate([dw_qkvg, dw_f[:, :FOX_H]], axis=1)
    dh, g["mix_g"] = _rms_bwd(a["h"], p["mix_g"], dhn, gw=D_MODEL, ncol=1, add=dh1, name=f"{name}_norm_b")
    return dh, g


def _ffn_forward(h, p, name):
    hn = _rms_fwd(h, p["ffn_g"], gw=D_MODEL, ncol=1, name=f"{name}_norm")
    u = _matmul(hn, p["w_up"], mode="nn", name=f"{name}_up")
    act = _conv_fwd(u, p["conv_w8"], p["conv_b"], kw=FFN_K, width=D_FF, u_col0=0, mul_col0=D_FF, out_dtype=BF16,
                    name=f"{name}_glu")
    out = _matmul(act, p["w_down"], mode="nn", add=h, name=f"{name}_down")
    return out, dict(h=h, hn=hn, u=u, act=act)


def _ffn_backward(dh2, p, a, name):
    g = {}
    dact = _matmul(dh2, p["w_down"], mode="nt", name=f"{name}_down_dx")
    g["w_down"] = _matmul(a["act"], dh2, mode="tn", name=f"{name}_down_dw")
    du1, du2, dwb = _conv_bwd(a["u"], p["conv_w8"], p["conv_b"], dact, kw=FFN_K, width=D_FF, u_col0=0, mul_col0=D_FF,
                              name=f"{name}_glu_b")
    g["conv_w"], g["conv_b"] = dwb[:FFN_K], dwb[7]
    du = jnp.concatenate([du1, du2], axis=1)
    dhn = _matmul(du, p["w_up"], mode="nt", name=f"{name}_up_dx")
    g["w_up"] = _matmul(a["hn"], du, mode="tn", name=f"{name}_up_dw")
    dh, g["ffn_g"] = _rms_bwd(a["h"], p["ffn_g"], dhn, gw=D_MODEL, ncol=1, add=dh2, name=f"{name}_norm_b")
    return dh, g


def _to_slabs(full, col_sharded):
    nl, r, ct = full.shape
    if col_sharded:
        t = jnp.transpose(full.reshape(2, nl // 2, r, 4, ct // 4), (0, 3, 1, 2, 4))
        return t.reshape(2, 4, (nl // 2) * r, ct // 4)
    t = jnp.transpose(full.reshape(2, nl // 2, 4, r // 4, ct), (0, 2, 1, 3, 4))
    return t.reshape(2, 4, (nl // 2) * (r // 4), ct)


def kernel(x, mix_norm_g, ffn_norm_g, ssd_w_in, ssd_conv_w, ssd_conv_b, ssd_dt_bias, ssd_a_log, ssd_d, ssd_norm_g, ssd_w_out, fox_w_in, fox_b_f, fox_q_norm_g, fox_k_norm_g, fox_w_out, ffn_w_up, ffn_conv_w, ffn_conv_b, ffn_w_down, final_norm_g, loss_target, m_mix_norm_g, m_ffn_norm_g, m_ssd_w_in, m_ssd_conv_w, m_ssd_conv_b, m_ssd_dt_bias, m_ssd_a_log, m_ssd_d, m_ssd_norm_g, m_ssd_w_out, m_fox_w_in, m_fox_b_f, m_fox_q_norm_g, m_fox_k_norm_g, m_fox_w_out, m_ffn_w_up, m_ffn_conv_w, m_ffn_conv_b, m_ffn_w_down, m_final_norm_g, v_mix_norm_g, v_ffn_norm_g, v_ssd_w_in, v_ssd_conv_w, v_ssd_conv_b, v_ssd_dt_bias, v_ssd_a_log, v_ssd_d, v_ssd_norm_g, v_ssd_w_out, v_fox_w_in, v_fox_b_f, v_fox_q_norm_g, v_fox_k_norm_g, v_fox_w_out, v_ffn_w_up, v_ffn_conv_w, v_ffn_conv_b, v_ffn_w_down, v_final_norm_g):
    w = dict(mix_norm_g=mix_norm_g, ffn_norm_g=ffn_norm_g, ssd_w_in=ssd_w_in, ssd_conv_w=ssd_conv_w, ssd_conv_b=ssd_conv_b,
             ssd_dt_bias=ssd_dt_bias, ssd_a_log=ssd_a_log, ssd_d=ssd_d, ssd_norm_g=ssd_norm_g, ssd_w_out=ssd_w_out,
             fox_w_in=fox_w_in, fox_b_f=fox_b_f, fox_q_norm_g=fox_q_norm_g, fox_k_norm_g=fox_k_norm_g, fox_w_out=fox_w_out,
             ffn_w_up=ffn_w_up, ffn_conv_w=ffn_conv_w, ffn_conv_b=ffn_conv_b, ffn_w_down=ffn_w_down, final_norm_g=final_norm_g)
    m_in = dict(zip(NAMES, (m_mix_norm_g, m_ffn_norm_g, m_ssd_w_in, m_ssd_conv_w, m_ssd_conv_b, m_ssd_dt_bias, m_ssd_a_log,
                            m_ssd_d, m_ssd_norm_g, m_ssd_w_out, m_fox_w_in, m_fox_b_f, m_fox_q_norm_g, m_fox_k_norm_g,
                            m_fox_w_out, m_ffn_w_up, m_ffn_conv_w, m_ffn_conv_b, m_ffn_w_down, m_final_norm_g)))
    v_in = dict(zip(NAMES, (v_mix_norm_g, v_ffn_norm_g, v_ssd_w_in, v_ssd_conv_w, v_ssd_conv_b, v_ssd_dt_bias, v_ssd_a_log,
                            v_ssd_d, v_ssd_norm_g, v_ssd_w_out, v_fox_w_in, v_fox_b_f, v_fox_q_norm_g, v_fox_k_norm_g,
                            v_fox_w_out, v_ffn_w_up, v_ffn_conv_w, v_ffn_conv_b, v_ffn_w_down, v_final_norm_g)))
    cx, cy, cc = _coords()
    chip = 2 * cx + cy
    h = x[0]
    target = loss_target[0]

    conv_shapes = [ssd_conv_w.shape, ffn_conv_w.shape]
    slots = _allgather_small(_pack([ssd_conv_w, ffn_conv_w]), name="gather_conv_w", with_sum=False)[0]
    per_chip = [_unpack(slots[2 * q], conv_shapes) for q in range(4)]
    ssd_conv_full = jnp.concatenate([pc[0] for pc in per_chip], axis=2)
    ffn_conv_full = jnp.concatenate([pc[1] for pc in per_chip], axis=2)
    shards = [w[n].astype(BF16) for n in BIG]
    gathered = _allgather_chips([s.reshape((2, s.shape[0] // 2) + s.shape[1:]) for s in shards], name="gather_weights")
    full = {}
    for n, gth in zip(BIG, gathered):
        gth = gth.reshape((4, gth.shape[1] * gth.shape[2]) + gth.shape[3:])
        if n in COL_SHARDED:
            full[n] = jnp.transpose(gth, (1, 2, 0, 3)).reshape(gth.shape[1], gth.shape[2], 4 * gth.shape[3])
        else:
            full[n] = jnp.transpose(gth, (1, 0, 2, 3)).reshape(gth.shape[1], 4 * gth.shape[2], gth.shape[3])

    def ssd_params(j, i):
        w_in = full["ssd_w_in"][j]
        g3 = lambda v: v.reshape(SSD_G, 1, SSD_HPG)
        g3c = lambda v: v.reshape(SSD_G, SSD_HPG, 1)
        return dict(mix_g=mix_norm_g[i][None], w_zx=w_in[:, :SSD_ZX], w_dt=_pad_lanes(w_in[:, SSD_ZX:]),
                    conv_w8=_pad8(ssd_conv_full[j]), conv_b=ssd_conv_b[j][None], bias_r=g3(ssd_dt_bias[j]),
                    bias_c=g3c(ssd_dt_bias[j]), alog_r=g3(ssd_a_log[j]), alog_c=g3c(ssd_a_log[j]), d_r=g3(ssd_d[j]),
                    norm_g=ssd_norm_g[j][None], w_out=full["ssd_w_out"][j])

    def fox_params(j, i):
        w_in = full["fox_w_in"][j]
        return dict(mix_g=mix_norm_g[i][None], w_qkvg=w_in[:, :4 * FOX_D], w_f=_pad_lanes(w_in[:, 4 * FOX_D:]),
                    gq=jnp.tile(fox_q_norm_g[j], FOX_H)[None], gk=jnp.tile(fox_k_norm_g[j], FOX_H)[None],
                    b_f=fox_b_f[j][:, None], w_out=full["fox_w_out"][j])

    def ffn_params(i):
        return dict(ffn_g=ffn_norm_g[i][None], w_up=full["ffn_w_up"][i], conv_w8=_pad8(ffn_conv_full[i]),
                    conv_b=ffn_conv_b[i][None], w_down=full["ffn_w_down"][i])

    mix_p, ffn_p, mix_a, ffn_a = [], [], [], []
    for i in range(DEPTH):
        j = i // 2
        if i % 2 == 0:
            mix_p.append(ssd_params(j, i))
            h, act = _ssd_forward(h, mix_p[i], f"ssd{j}")
        else:
            mix_p.append(fox_params(j, i))
            h, act = _fox_forward(h, mix_p[i], f"fox{j}")
        mix_a.append(act)
        ffn_p.append(ffn_params(i))
        h, act = _ffn_forward(h, ffn_p[i], f"ffn{i}")
        ffn_a.append(act)
    loss_part, dh, d_final_g = _loss_head(h, final_norm_g[None], target, name="loss_head")

    mix_g, ffn_g = [None] * DEPTH, [None] * DEPTH
    for i in reversed(range(DEPTH)):
        j = i // 2
        dh, ffn_g[i] = _ffn_backward(dh, ffn_p[i], ffn_a[i], f"ffn{i}")
        if i % 2 == 0:
            dh, mix_g[i] = _ssd_backward(dh, mix_p[i], mix_a[i], f"ssd{j}")
        else:
            dh, mix_g[i] = _fox_backward(dh, mix_p[i], mix_a[i], f"fox{j}")
    grad_x = dh[None]
    ssd_g, fox_g = [mix_g[0], mix_g[2]], [mix_g[1], mix_g[3]]

    big_full = dict(
        ssd_w_in=jnp.stack([g["w_in"] for g in ssd_g]), ssd_w_out=jnp.stack([g["w_out"] for g in ssd_g]),
        fox_w_in=jnp.stack([g["w_in"] for g in fox_g]), fox_w_out=jnp.stack([g["w_out"] for g in fox_g]),
        ffn_w_up=jnp.stack([g["w_up"] for g in ffn_g]), ffn_w_down=jnp.stack([g["w_down"] for g in ffn_g]))
    mine, theirs = _reduce_scatter([_to_slabs(big_full[n], n in COL_SHARDED) for n in BIG], cc, chip)
    grads = {}
    for n, a, b in zip(BIG, mine, theirs):
        shp = w[n].shape
        a = a.reshape((shp[0] // 2,) + shp[1:])
        b = b.reshape((shp[0] // 2,) + shp[1:])
        grads[n] = jnp.where(cc == 0, jnp.concatenate([a, b], axis=0), jnp.concatenate([b, a], axis=0))
    small = dict(
        mix_norm_g=jnp.concatenate([g["mix_g"] for g in mix_g], axis=0),
        ffn_norm_g=jnp.concatenate([g["ffn_g"] for g in ffn_g], axis=0),
        ssd_conv_w=jnp.stack([g["conv_w"] for g in ssd_g]), ssd_conv_b=jnp.stack([g["conv_b"] for g in ssd_g]),
        ssd_dt_bias=jnp.stack([g["dt_bias"].reshape(SSD_H) for g in ssd_g]),
        ssd_a_log=jnp.stack([g["a_log"].reshape(SSD_H) for g in ssd_g]),
        ssd_d=jnp.stack([g["d"].reshape(SSD_H) for g in ssd_g]),
        ssd_norm_g=jnp.concatenate([g["norm_g"] for g in ssd_g], axis=0),
        fox_b_f=jnp.stack([g["b_f"] for g in fox_g]), fox_q_norm_g=jnp.stack([g["gq"] for g in fox_g]),
        fox_k_norm_g=jnp.stack([g["gk"] for g in fox_g]),
        ffn_conv_w=jnp.stack([g["conv_w"] for g in ffn_g]), ffn_conv_b=jnp.stack([g["conv_b"] for g in ffn_g]),
        final_norm_g=d_final_g[0], loss=loss_part[0, :1])
    _, total = _allgather_small(_pack([small[n] for n, _ in SMALL]), name="reduce_small", with_sum=True)
    for (n, shp), val in zip(SMALL, _unpack(total, [shp for _, shp in SMALL])):
        grads[n] = val
    loss = grads.pop("loss")[0]
    grads["ssd_conv_w"] = lax.dynamic_slice_in_dim(grads["ssd_conv_w"], chip * ssd_conv_w.shape[2], ssd_conv_w.shape[2], axis=2)
    grads["ffn_conv_w"] = lax.dynamic_slice_in_dim(grads["ffn_conv_w"], chip * ffn_conv_w.shape[2], ffn_conv_w.shape[2], axis=2)

    deltas, new_m, new_v = {}, {}, {}
    for n in NAMES:
        shp = w[n].shape
        two_d = (1, shp[0]) if len(shp) == 1 else (-1, shp[-1])
        r2 = lambda a: a.reshape(two_d)
        d, nm, nv = _adamw(r2(w[n]), r2(grads[n]), r2(m_in[n]), r2(v_in[n]), name=f"adamw_{n}")
        deltas[n], new_m[n], new_v[n] = d.reshape(shp), nm.reshape(shp), nv.reshape(shp)
    return (loss, grad_x, *[grads[n] for n in NAMES], *[deltas[n] for n in NAMES], *[new_m[n] for n in NAMES],
            *[new_v[n] for n in NAMES])
```

```python
import functools

import jax
import jax.numpy as jnp
from jax import lax
from jax.experimental import pallas as pl
from jax.experimental.pallas import tpu as pltpu

F32 = jnp.float32
BF16 = jnp.bfloat16
HI = lax.Precision.HIGHEST
MESH = pl.DeviceIdType.MESH

D_MODEL = 1024
DEPTH = 4
EPS = 1e-6
SSD_DI = 2048
SSD_HD = 64
SSD_G = 4
SSD_HPG = 8
SSD_N = 128
SSD_K = 4
CHUNK = 128
SSD_CONV_DIM = 3072
SSD_ZX = SSD_DI + SSD_CONV_DIM
SSD_H = 32
FOX_HD = 64
FOX_H = 16
FOX_D = 1024
D_FF = 2816
FFN_K = 3
LANES = 128
VMEM_LIMIT = 56 * 1024 * 1024

ADAM_LR = 0.001
ADAM_B1 = 0.9
ADAM_B2 = 0.999
ADAM_EPS = 1e-08
ADAM_WD = 0.01
ADAM_STEP = 10

NN = (((1,), (0,)), ((), ()))
NT = (((1,), (1,)), ((), ()))
TN = (((0,), (0,)), ((), ()))


def _pick(n, cap, mult=LANES):
    best = None
    for t in range(mult, min(n, cap) + 1, mult):
        if n % t == 0:
            best = t
    return best if best is not None else n


def _cp(sem):
    return pltpu.CompilerParams(dimension_semantics=sem, vmem_limit_bytes=VMEM_LIMIT)


def _sigmoid(x):
    return jax.nn.sigmoid(x)


def _silu(x):
    return x * _sigmoid(x)


def _dsilu(x):
    s = _sigmoid(x)
    return s * (1.0 + x * (1.0 - s))


def _softplus(x):
    e = jnp.exp(-jnp.abs(x))
    u = 1.0 + e
    l1p = jnp.where(u == 1.0, e, jnp.log(u) * (e / (u - 1.0)))
    return jnp.maximum(x, 0.0) + l1p


def _dotf(a, b, dn=NN):
    return lax.dot_general(a, b, dn, precision=HI, preferred_element_type=F32)


def _dotb(a, b, dn=NN):
    return lax.dot_general(a.astype(BF16), b.astype(BF16), dn, preferred_element_type=F32)


def _group_matrix(width, sub, transpose=False):
    ng = width // sub
    shape = (ng, width) if transpose else (width, ng)
    lane = lax.broadcasted_iota(jnp.int32, shape, 1 if transpose else 0)
    grp = lax.broadcasted_iota(jnp.int32, shape, 0 if transpose else 1)
    return (lane // sub == grp).astype(F32)


def _gmean(v, sub):
    width = v.shape[-1]
    if sub == width:
        return jnp.mean(v, axis=-1, keepdims=True)
    s = _dotf(v, _group_matrix(width, sub))
    return _dotf(s, _group_matrix(width, sub, transpose=True)) * (1.0 / sub)


def _matmul(a, b, *, mode, name, out_dtype=F32, add=None):
    if mode == "nn":
        (m, k), (k2, n) = a.shape, b.shape
    elif mode == "nt":
        (m, k), (n, k2) = a.shape, b.shape
    else:
        (k, m), (k2, n) = a.shape, b.shape
    assert k == k2, (a.shape, b.shape, mode)
    tm, tn, tk = _pick(m, 512), _pick(n, 1536), _pick(k, 1536)
    nk = k // tk
    dn = {"nn": NN, "nt": NT, "tn": TN}[mode]
    has_add = add is not None

    def body(*refs):
        if has_add:
            a_ref, b_ref, add_ref, o_ref, acc_ref = refs
        else:
            a_ref, b_ref, o_ref, acc_ref = refs
            add_ref = None
        kk = pl.program_id(2)
        part = _dotb(a_ref[...], b_ref[...], dn)

        def finish(r):
            if has_add:
                r = r + add_ref[...]
            o_ref[...] = r.astype(out_dtype)

        if nk == 1:
            finish(part)
        else:
            @pl.when(kk == 0)
            def _():
                acc_ref[...] = part

            @pl.when(kk > 0)
            def _():
                acc_ref[...] += part

            @pl.when(kk == nk - 1)
            def _():
                finish(acc_ref[...])

    if mode == "nn":
        a_spec = pl.BlockSpec((tm, tk), lambda i, j, q: (i, q))
        b_spec = pl.BlockSpec((tk, tn), lambda i, j, q: (q, j))
    elif mode == "nt":
        a_spec = pl.BlockSpec((tm, tk), lambda i, j, q: (i, q))
        b_spec = pl.BlockSpec((tn, tk), lambda i, j, q: (j, q))
    else:
        a_spec = pl.BlockSpec((tk, tm), lambda i, j, q: (q, i))
        b_spec = pl.BlockSpec((tk, tn), lambda i, j, q: (q, j))
    o_spec = pl.BlockSpec((tm, tn), lambda i, j, q: (i, j))
    in_specs = [a_spec, b_spec] + ([o_spec] if has_add else [])
    args = (a, b) + ((add,) if has_add else ())
    return pl.pallas_call(
        body, name=name, grid=(m // tm, n // tn, nk), in_specs=in_specs, out_specs=o_spec,
        out_shape=jax.ShapeDtypeStruct((m, n), out_dtype),
        scratch_shapes=[pltpu.VMEM((tm, tn), F32)],
        compiler_params=_cp(("parallel", "parallel", "arbitrary")),
    )(*args)


def _rms_fwd(x, g, *, gw, ncol, name, x_col0=0, sub=None, z=None, z_col0=0, out_dtype=BF16):
    rows = x.shape[0]
    tr = _pick(rows, 512, 8)
    sub = gw if sub is None else sub
    gated = z is not None

    def body(*refs):
        if gated:
            x_ref, z_ref, g_ref, o_ref = refs
            xv = x_ref[...] * _silu(z_ref[...])
        else:
            x_ref, g_ref, o_ref = refs
            xv = x_ref[...]
        r = lax.rsqrt(_gmean(xv * xv, sub) + EPS)
        o_ref[...] = (xv * r * g_ref[...]).astype(out_dtype)

    specs = [pl.BlockSpec((tr, gw), lambda j, i: (i, x_col0 + j))]
    args = [x]
    if gated:
        specs.append(pl.BlockSpec((tr, gw), lambda j, i: (i, z_col0 + j)))
        args.append(z)
    specs.append(pl.BlockSpec((1, gw), lambda j, i: (0, j)))
    args.append(g)
    return pl.pallas_call(
        body, name=name, grid=(ncol, rows // tr), in_specs=specs,
        out_specs=pl.BlockSpec((tr, gw), lambda j, i: (i, j)),
        out_shape=jax.ShapeDtypeStruct((rows, gw * ncol), out_dtype),
        compiler_params=_cp(("parallel", "parallel")),
    )(*args)


def _rms_bwd(x, g, dy, *, gw, ncol, name, x_col0=0, sub=None, z=None, z_col0=0, add=None, dx_dtype=F32):
    rows = x.shape[0]
    tr = _pick(rows, 512, 8)
    sub = gw if sub is None else sub
    gated = z is not None
    has_add = add is not None

    def body(*refs):
        refs = list(refs)
        x_ref = refs.pop(0)
        z_ref = refs.pop(0) if gated else None
        g_ref = refs.pop(0)
        dy_ref = refs.pop(0)
        add_ref = refs.pop(0) if has_add else None
        dx_ref = refs.pop(0)
        dz_ref = refs.pop(0) if gated else None
        dg_ref = refs.pop(0)
        i = pl.program_id(1)
        xv = x_ref[...]
        if gated:
            zz = z_ref[...]
            yz = xv * _silu(zz)
        else:
            yz = xv
        r = lax.rsqrt(_gmean(yz * yz, sub) + EPS)
        xh = yz * r
        dy = dy_ref[...].astype(F32)
        dyg = dy * g_ref[...]
        d_yz = r * (dyg - xh * _gmean(dyg * xh, sub))
        if gated:
            dx_ref[...] = (d_yz * _silu(zz)).astype(dx_dtype)
            dz_ref[...] = (d_yz * xv * _dsilu(zz)).astype(dx_dtype)
        elif has_add:
            dx_ref[...] = (d_yz + add_ref[...]).astype(dx_dtype)
        else:
            dx_ref[...] = d_yz.astype(dx_dtype)
        part = jnp.sum(dy * xh, axis=0, keepdims=True)

        @pl.when(i == 0)
        def _():
            dg_ref[...] = part

        @pl.when(i > 0)
        def _():
            dg_ref[...] += part

    tile = pl.BlockSpec((tr, gw), lambda j, i: (i, j))
    specs = [pl.BlockSpec((tr, gw), lambda j, i: (i, x_col0 + j))]
    args = [x]
    if gated:
        specs.append(pl.BlockSpec((tr, gw), lambda j, i: (i, z_col0 + j)))
        args.append(z)
    specs += [pl.BlockSpec((1, gw), lambda j, i: (0, j)), tile]
    args += [g, dy]
    if has_add:
        specs.append(tile)
        args.append(add)
    width = gw * ncol
    out_shape = [jax.ShapeDtypeStruct((rows, width), dx_dtype)]
    out_specs = [tile]
    if gated:
        out_shape.append(jax.ShapeDtypeStruct((rows, width), dx_dtype))
        out_specs.append(tile)
    out_shape.append(jax.ShapeDtypeStruct((1, width), F32))
    out_specs.append(pl.BlockSpec((1, gw), lambda j, i: (0, j)))
    return pl.pallas_call(
        body, name=name, grid=(ncol, rows // tr), in_specs=specs, out_specs=out_specs, out_shape=out_shape,
        compiler_params=_cp(("parallel", "arbitrary")),
    )(*args)


HALO = 8


def _conv_fwd(u, w8, b, *, kw, width, name, u_col0=0, mul_col0=None, out_dtype=F32):
    rows = u.shape[0]
    ts = _pick(rows, 512, 8)
    tc = _pick(width, 512)
    gated = mul_col0 is not None
    c0 = u_col0 // tc
    m0 = (mul_col0 // tc) if gated else 0
    assert u_col0 % tc == 0 and (not gated or mul_col0 % tc == 0)

    def body(*refs):
        if gated:
            cur_ref, halo_ref, mul_ref, w_ref, b_ref, o_ref, ext = refs
        else:
            cur_ref, halo_ref, w_ref, b_ref, o_ref, ext = refs
        i = pl.program_id(0)
        ext[pl.ds(0, HALO), :] = jnp.where(i == 0, 0.0, halo_ref[...])
        ext[pl.ds(HALO, ts), :] = cur_ref[...]
        pre = jnp.zeros((ts, tc), F32) + b_ref[...]
        for k in range(kw):
            pre = pre + w_ref[k:k + 1, :] * ext[pl.ds(HALO - (kw - 1) + k, ts), :]
        act = _silu(pre)
        if gated:
            act = act * mul_ref[...]
        o_ref[...] = act.astype(out_dtype)

    hb = ts // HALO
    specs = [pl.BlockSpec((ts, tc), lambda i, j: (i, c0 + j)),
             pl.BlockSpec((HALO, tc), lambda i, j: (jnp.maximum(i * hb - 1, 0), c0 + j))]
    args = [u, u]
    if gated:
        specs.append(pl.BlockSpec((ts, tc), lambda i, j: (i, m0 + j)))
        args.append(u)
    specs += [pl.BlockSpec((8, tc), lambda i, j: (0, j)), pl.BlockSpec((1, tc), lambda i, j: (0, j))]
    args += [w8, b]
    return pl.pallas_call(
        body, name=name, grid=(rows // ts, width // tc), in_specs=specs,
        out_specs=pl.BlockSpec((ts, tc), lambda i, j: (i, j)),
        out_shape=jax.ShapeDtypeStruct((rows, width), out_dtype),
        scratch_shapes=[pltpu.VMEM((ts + HALO, tc), F32)],
        compiler_params=_cp(("parallel", "parallel")),
    )(*args)


def _conv_bwd(u, w8, b, dact, *, kw, width, name, u_col0=0, mul_col0=None, du_dtype=BF16):
    rows = u.shape[0]
    ts = _pick(rows, 512, 8)
    tc = _pick(width, 512)
    gated = mul_col0 is not None
    c0 = u_col0 // tc
    m0 = (mul_col0 // tc) if gated else 0
    nt = rows // ts
    hb = ts // HALO

    def body(*refs):
        refs = list(refs)
        cur_ref, halo_ref = refs.pop(0), refs.pop(0)
        mul_ref = refs.pop(0) if gated else None
        w_ref, b_ref, da_ref = refs.pop(0), refs.pop(0), refs.pop(0)
        du_ref = refs.pop(0)
        dmul_ref = refs.pop(0) if gated else None
        dwb_ref, ext_u, ext_d = refs
        t = pl.program_id(1)
        ti = nt - 1 - t
        ext_u[pl.ds(0, HALO), :] = jnp.where(ti == 0, 0.0, halo_ref[...])
        ext_u[pl.ds(HALO, ts), :] = cur_ref[...]
        pre = jnp.zeros((ts, tc), F32) + b_ref[...]
        for k in range(kw):
            pre = pre + w_ref[k:k + 1, :] * ext_u[pl.ds(HALO - (kw - 1) + k, ts), :]
        da = da_ref[...].astype(F32)
        if gated:
            mul = mul_ref[...]
            dmul_ref[...] = (da * _silu(pre)).astype(du_dtype)
            dgp = da * mul * _dsilu(pre)
        else:
            dgp = da * _dsilu(pre)

        @pl.when(t == 0)
        def _():
            ext_d[pl.ds(ts, HALO), :] = jnp.zeros((HALO, tc), F32)
            dwb_ref[...] = jnp.zeros((8, tc), F32)

        ext_d[pl.ds(0, ts), :] = dgp
        du = jnp.zeros((ts, tc), F32)
        for k in range(kw):
            du = du + w_ref[k:k + 1, :] * ext_d[pl.ds(kw - 1 - k, ts), :]
        du_ref[...] = du.astype(du_dtype)
        for k in range(kw):
            dwb_ref[k:k + 1, :] += jnp.sum(dgp * ext_u[pl.ds(HALO - (kw - 1) + k, ts), :], axis=0, keepdims=True)
        dwb_ref[7:8, :] += jnp.sum(dgp, axis=0, keepdims=True)
        ext_d[pl.ds(ts, HALO), :] = dgp[0:HALO, :]

    specs = [pl.BlockSpec((ts, tc), lambda j, t: (nt - 1 - t, c0 + j)),
             pl.BlockSpec((HALO, tc), lambda j, t: (jnp.maximum((nt - 1 - t) * hb - 1, 0), c0 + j))]
    args = [u, u]
    if gated:
        specs.append(pl.BlockSpec((ts, tc), lambda j, t: (nt - 1 - t, m0 + j)))
        args.append(u)
    tile = pl.BlockSpec((ts, tc), lambda j, t: (nt - 1 - t, j))
    specs += [pl.BlockSpec((8, tc), lambda j, t: (0, j)), pl.BlockSpec((1, tc), lambda j, t: (0, j)), tile]
    args += [w8, b, dact]
    out_shape = [jax.ShapeDtypeStruct((rows, width), du_dtype)]
    out_specs = [tile]
    if gated:
        out_shape.append(jax.ShapeDtypeStruct((rows, width), du_dtype))
        out_specs.append(tile)
    out_shape.append(jax.ShapeDtypeStruct((8, width), F32))
    out_specs.append(pl.BlockSpec((8, tc), lambda j, t: (0, j)))
    return pl.pallas_call(
        body, name=name, grid=(width // tc, nt), in_specs=specs, out_specs=out_specs, out_shape=out_shape,
        scratch_shapes=[pltpu.VMEM((ts + HALO, tc), F32), pltpu.VMEM((ts + HALO, tc), F32)],
        compiler_params=_cp(("parallel", "arbitrary")),
    )(*args)


GW = SSD_HPG * SSD_HD


def _ssd_common(x, bm, cm, dt_raw, dt_raw_t, bias_r, bias_c, alog_r, alog_c):
    row = lax.broadcasted_iota(jnp.int32, (CHUNK, CHUNK), 0)
    col = lax.broadcasted_iota(jnp.int32, (CHUNK, CHUNK), 1)
    causal = row >= col
    tril = causal.astype(F32)
    triu = (row <= col).astype(F32)
    spread = _group_matrix(GW, SSD_HD, transpose=True)
    dt = _softplus(dt_raw + bias_r)
    dt_t = _softplus(dt_raw_t + bias_c)
    a_r = -jnp.exp(alog_r)
    a_c = -jnp.exp(alog_c)
    acs = _dotf(tril, dt * a_r)
    acs_t = _dotf(dt_t * a_c, triu)
    last = acs[CHUNK - 1:CHUNK, :]
    ds = jnp.exp(last - acs)
    cd = jnp.exp(last)
    c = dict(causal=causal, tril=tril, triu=triu, spread=spread, dt=dt, a_r=a_r, acs=acs, acs_t=acs_t, ds=ds, cd=cd)
    c["eb"] = _dotf(jnp.exp(acs), spread)
    c["dsb"] = _dotf(ds, spread)
    c["cdb"] = _dotf(cd, spread)
    c["dtb"] = _dotf(dt, spread)
    c["xdt"] = x * c["dtb"]
    c["cb"] = _dotb(cm, bm, NT)
    return c


def _ssd_lam(c, r):
    diff = c["acs"][:, r:r + 1] - c["acs_t"][r:r + 1, :]
    return jnp.exp(jnp.where(c["causal"], diff, -jnp.inf))


def _ssd_specs(nc, rev):
    def ci(t):
        return (nc - 1 - t) if rev else t
    xs = pl.BlockSpec((CHUNK, GW), lambda g, t: (ci(t), g))
    bs = pl.BlockSpec((CHUNK, SSD_N), lambda g, t: (ci(t), SSD_DI // SSD_N + g))
    cs = pl.BlockSpec((CHUNK, SSD_N), lambda g, t: (ci(t), SSD_DI // SSD_N + SSD_G + g))
    dts = pl.BlockSpec((1, CHUNK, 8), lambda g, t: (g, ci(t), 0))
    dtts = pl.BlockSpec((1, 8, CHUNK), lambda g, t: (g, 0, ci(t)))
    pr = pl.BlockSpec((1, 1, 8), lambda g, t: (g, 0, 0))
    pc = pl.BlockSpec((1, 8, 1), lambda g, t: (g, 0, 0))
    hs = pl.BlockSpec((1, 1, SSD_N, GW), lambda g, t: (ci(t), g, 0, 0))
    return xs, bs, cs, dts, dtts, pr, pc, hs


def _ssd_fwd(xbc, dtg, dtg_t, bias_r, bias_c, alog_r, alog_c, d_r, *, name):
    s = xbc.shape[0]
    nc = s // CHUNK
    xs, bs, cs, dts, dtts, pr, pc, hs = _ssd_specs(nc, False)

    def body(x_ref, b_ref, c_ref, dt_ref, dtt_ref, br_ref, bc_ref, ar_ref, ac_ref, d_ref, y_ref, hp_ref, h_sc):
        t = pl.program_id(1)

        @pl.when(t == 0)
        def _():
            h_sc[...] = jnp.zeros_like(h_sc)

        x, bm, cm = x_ref[...], b_ref[...], c_ref[...]
        c = _ssd_common(x, bm, cm, dt_ref[0], dtt_ref[0], br_ref[0], bc_ref[0], ar_ref[0], ac_ref[0])
        h = h_sc[...]
        hp_ref[0, 0] = h
        xdt = c["xdt"]
        pieces = []
        for r in range(SSD_HPG):
            m = c["cb"] * _ssd_lam(c, r)
            pieces.append(_dotb(m, xdt[:, r * SSD_HD:(r + 1) * SSD_HD]))
        y = jnp.concatenate(pieces, axis=1) + c["eb"] * _dotb(cm, h) + x * _dotf(d_ref[0], c["spread"])
        y_ref[...] = y
        h_sc[...] = h * c["cdb"] + _dotb(bm, xdt * c["dsb"], TN)

    return pl.pallas_call(
        body, name=name, grid=(SSD_G, nc),
        in_specs=[xs, bs, cs, dts, dtts, pr, pc, pr, pc, pr],
        out_specs=[xs, hs],
        out_shape=[jax.ShapeDtypeStruct((s, SSD_DI), F32), jax.ShapeDtypeStruct((nc, SSD_G, SSD_N, GW), F32)],
        scratch_shapes=[pltpu.VMEM((SSD_N, GW), F32)],
        compiler_params=_cp(("parallel", "arbitrary")),
    )(xbc, xbc, xbc, dtg, dtg_t, bias_r, bias_c, alog_r, alog_c, d_r)


def _ssd_bwd(xbc, dtg, dtg_t, bias_r, bias_c, alog_r, alog_c, d_r, hprev, dy, *, name):
    s = xbc.shape[0]
    nc = s // CHUNK
    xs, bs, cs, dts, dtts, pr, pc, hs = _ssd_specs(nc, True)
    gsum = functools.partial(_group_matrix, GW, SSD_HD)

    def body(x_ref, b_ref, c_ref, dt_ref, dtt_ref, br_ref, bc_ref, ar_ref, ac_ref, d_ref, hp_ref, dy_ref,
             dx_ref, db_ref, dc_ref, ddt_ref, dbias_ref, dalog_ref, dd_ref, dh_sc):
        t = pl.program_id(1)

        @pl.when(t == 0)
        def _():
            dh_sc[...] = jnp.zeros_like(dh_sc)
            dbias_ref[...] = jnp.zeros_like(dbias_ref)
            dalog_ref[...] = jnp.zeros_like(dalog_ref)
            dd_ref[...] = jnp.zeros_like(dd_ref)

        x, bm, cm = x_ref[...], b_ref[...], c_ref[...]
        c = _ssd_common(x, bm, cm, dt_ref[0], dtt_ref[0], br_ref[0], bc_ref[0], ar_ref[0], ac_ref[0])
        lanesum = gsum()
        h = hp_ref[0, 0]
        dh = dh_sc[...]
        dy = dy_ref[...]
        xdt, dsb = c["xdt"], c["dsb"]
        skip = _dotf(d_ref[0], c["spread"])
        dd_ref[0] += jnp.sum(_dotf(dy * x, lanesum), axis=0, keepdims=True)
        dacs = _dotf(dy * (c["eb"] * _dotb(cm, h)), lanesum)
        edy = c["eb"] * dy
        dcm = _dotb(edy, h, NT)
        dh_prev = _dotb(cm, edy, TN)
        bdh = _dotb(bm, dh)
        dxdt = dsb * bdh
        dbm = _dotb(dsb * xdt, dh, NT)
        t1 = _dotf(xdt * bdh, lanesum) * c["ds"]
        dacs = dacs - t1
        dlast = jnp.sum(t1, axis=0, keepdims=True) + jnp.sum(_dotf(dh * h, lanesum), axis=0, keepdims=True) * c["cd"]
        dcb = jnp.zeros((CHUNK, CHUNK), F32)
        pieces = []
        ones8 = jnp.ones((CHUNK, 8), F32)
        head = lax.broadcasted_iota(jnp.int32, (1, 8), 1)
        for r in range(SSD_HPG):
            sl = slice(r * SSD_HD, (r + 1) * SSD_HD)
            lam = _ssd_lam(c, r)
            m = c["cb"] * lam
            dm = _dotb(dy[:, sl], xdt[:, sl], NT)
            dcb = dcb + dm * lam
            gm = dm * m
            dacs = dacs + (jnp.sum(gm, axis=1, keepdims=True) - _dotf(gm, ones8, TN)) * (head == r).astype(F32)
            pieces.append(_dotb(m, dy[:, sl], TN))
        dxdt = dxdt + jnp.concatenate(pieces, axis=1)
        dcm = dcm + _dotb(dcb, bm)
        dbm = dbm + _dotb(dcb, cm, TN)
        dx_ref[...] = dy * skip + dxdt * c["dtb"]
        db_ref[...] = dbm
        dc_ref[...] = dcm
        rowid = lax.broadcasted_iota(jnp.int32, (CHUNK, 8), 0)
        dacs = dacs + jnp.where(rowid == CHUNK - 1, dlast, 0.0)
        dda = _dotf(c["triu"], dacs)
        ddt = _dotf(dxdt * x, lanesum) + dda * c["a_r"]
        ddt_raw = ddt * _sigmoid(dt_ref[0] + br_ref[0])
        ddt_ref[0] = ddt_raw
        dbias_ref[0] += jnp.sum(ddt_raw, axis=0, keepdims=True)
        dalog_ref[0] += jnp.sum(dda * c["dt"], axis=0, keepdims=True) * c["a_r"]
        dh_sc[...] = dh_prev + dh * c["cdb"]

    ci = lambda t: nc - 1 - t
    nspec = pl.BlockSpec((CHUNK, SSD_N), lambda g, t: (ci(t), g))
    return pl.pallas_call(
        body, name=name, grid=(SSD_G, nc),
        in_specs=[xs, bs, cs, dts, dtts, pr, pc, pr, pc, pr, hs, xs],
        out_specs=[xs, nspec, nspec, dts, pr, pr, pr],
        out_shape=[jax.ShapeDtypeStruct((s, SSD_DI), F32), jax.ShapeDtypeStruct((s, SSD_G * SSD_N), F32),
                   jax.ShapeDtypeStruct((s, SSD_G * SSD_N), F32), jax.ShapeDtypeStruct((SSD_G, s, 8), F32),
                   jax.ShapeDtypeStruct((SSD_G, 1, 8), F32), jax.ShapeDtypeStruct((SSD_G, 1, 8), F32),
                   jax.ShapeDtypeStruct((SSD_G, 1, 8), F32)],
        scratch_shapes=[pltpu.VMEM((SSD_N, GW), F32)],
        compiler_params=_cp(("parallel", "arbitrary")),
    )(xbc, xbc, xbc, dtg, dtg_t, bias_r, bias_c, alog_r, alog_c, d_r, hprev, dy)


FOX_PAIRS = FOX_H // 2
FOX_SCALE = FOX_HD ** -0.5
NEG_INF = -jnp.inf


def _fgate_fwd(f_t, b_c, *, name):
    hh, s = f_t.shape
    tb = _pick(s, 512)
    nb = s // tb

    def body(f_ref, b_ref, o_ref, carry):
        t = pl.program_id(0)

        @pl.when(t == 0)
        def _():
            carry[...] = jnp.zeros_like(carry)

        lf = -_softplus(-(f_ref[...] + b_ref[...]))
        row = lax.broadcasted_iota(jnp.int32, (tb, tb), 0)
        col = lax.broadcasted_iota(jnp.int32, (tb, tb), 1)
        cum = _dotf(lf, (row <= col).astype(F32)) + carry[:, 0:1]
        o_ref[...] = cum
        carry[:, 0:1] = cum[:, tb - 1:tb]

    return pl.pallas_call(
        body, name=name, grid=(nb,),
        in_specs=[pl.BlockSpec((hh, tb), lambda t: (0, t)), pl.BlockSpec((hh, 1), lambda t: (0, 0))],
        out_specs=pl.BlockSpec((hh, tb), lambda t: (0, t)),
        out_shape=jax.ShapeDtypeStruct((hh, s), F32),
        scratch_shapes=[pltpu.VMEM((hh, LANES), F32)],
        compiler_params=_cp(("arbitrary",)),
    )(f_t, b_c)


def _fgate_bwd(dcum_q_t, dcum_k_t, f_t, b_c, *, name):
    hh, s = f_t.shape
    tb = _pick(s, 512)
    nb = s // tb

    def body(dq_ref, d_ref, f_ref, b_ref, df_ref, db_ref, carry):
        t = pl.program_id(0)

        @pl.when(t == 0)
        def _():
            carry[...] = jnp.zeros_like(carry)
            db_ref[...] = jnp.zeros_like(db_ref)

        d = d_ref[...] + dq_ref[...]
        row = lax.broadcasted_iota(jnp.int32, (tb, tb), 0)
        col = lax.broadcasted_iota(jnp.int32, (tb, tb), 1)
        rev = _dotf(d, (row >= col).astype(F32)) + carry[:, 0:1]
        df = rev * _sigmoid(-(f_ref[...] + b_ref[...]))
        df_ref[...] = df
        db_ref[...] += jnp.sum(df, axis=1, keepdims=True)
        carry[:, 0:1] = rev[:, 0:1]

    blk = pl.BlockSpec((hh, tb), lambda t: (0, nb - 1 - t))
    return pl.pallas_call(
        body, name=name, grid=(nb,),
        in_specs=[blk, blk, blk, pl.BlockSpec((hh, 1), lambda t: (0, 0))],
        out_specs=[blk, pl.BlockSpec((hh, 1), lambda t: (0, 0))],
        out_shape=[jax.ShapeDtypeStruct((hh, s), F32), jax.ShapeDtypeStruct((hh, 1), F32)],
        scratch_shapes=[pltpu.VMEM((hh, LANES), F32)],
        compiler_params=_cp(("arbitrary",)),
    )(dcum_q_t, dcum_k_t, f_t, b_c)


def _fox_tile(s):
    return min(512, max(s // 2, 8))


def _tri_tables(nq, kv_major):
    if kv_major:
        pairs = [(i, j) for j in range(nq) for i in range(j, nq)]
    else:
        pairs = [(i, j) for i in range(nq) for j in range(i + 1)]
    return (jnp.asarray([p[0] for p in pairs], jnp.int32), jnp.asarray([p[1] for p in pairs], jnp.int32))


def _lane_tile(col, width):
    return col if width == LANES else jnp.tile(col, (1, width // LANES))


def _flash_fwd(qs, kn, qkvg, ck, *, name):
    s = qs.shape[0]
    tt = _fox_tile(s)
    nq = s // tt
    itab, jtab = _tri_tables(nq, kv_major=False)
    v0 = 2 * FOX_D // LANES

    def body(itab_ref, jtab_ref, q_ref, k_ref, v_ref, ck_ref, o_ref, lse_ref, m_sc, l_sc, acc_sc):
        t = pl.program_id(1)
        i, j = itab_ref[t], jtab_ref[t]

        @pl.when(j == 0)
        def _():
            m_sc[...] = jnp.full_like(m_sc, NEG_INF)
            l_sc[...] = jnp.zeros_like(l_sc)
            acc_sc[...] = jnp.zeros_like(acc_sc)

        low = lax.broadcasted_iota(jnp.int32, (tt, LANES), 1) < FOX_HD

        def step(diagonal):
            q2, k2 = q_ref[...], k_ref[...]
            v2 = v_ref[...].astype(BF16)
            alphas, outs = [], []
            for hh in range(2):
                qh = jnp.where(low if hh == 0 else jnp.logical_not(low), q2, jnp.zeros_like(q2))
                sc = lax.dot_general(qh, k2, NT, preferred_element_type=F32) - ck_ref[0][hh:hh + 1, :]
                if diagonal:
                    row = lax.broadcasted_iota(jnp.int32, sc.shape, 0)
                    col = lax.broadcasted_iota(jnp.int32, sc.shape, 1)
                    sc = jnp.where(row >= col, sc, NEG_INF)
                m_prev = m_sc[hh]
                m_new = jnp.maximum(m_prev, jnp.max(sc, axis=1, keepdims=True))
                alpha = jnp.exp(m_prev - m_new)
                p = jnp.exp(sc - _lane_tile(m_new, tt))
                l_sc[hh] = alpha * l_sc[hh] + jnp.sum(p, axis=1, keepdims=True)
                m_sc[hh] = m_new
                alphas.append(alpha)
                outs.append(lax.dot_general(p.astype(BF16), v2, NN, preferred_element_type=F32))
            acc_sc[...] = jnp.where(low, alphas[0], alphas[1]) * acc_sc[...] + jnp.where(low, outs[0], outs[1])

        @pl.when(j < i)
        def _():
            step(False)

        @pl.when(j == i)
        def _():
            step(True)
            o_ref[...] = acc_sc[...] / jnp.where(low, l_sc[0], l_sc[1])
            lse_ref[0] = jnp.concatenate([m_sc[hh][:, 0:1] + jnp.log(l_sc[hh][:, 0:1]) for hh in range(2)], axis=1)

    return pl.pallas_call(
        body, name=name,
        grid_spec=pltpu.PrefetchScalarGridSpec(
            num_scalar_prefetch=2, grid=(FOX_PAIRS, itab.shape[0]),
            in_specs=[pl.BlockSpec((tt, LANES), lambda p, t, it, jt: (it[t], p)),
                      pl.BlockSpec((tt, LANES), lambda p, t, it, jt: (jt[t], p)),
                      pl.BlockSpec((tt, LANES), lambda p, t, it, jt: (jt[t], v0 + p)),
                      pl.BlockSpec((1, 2, tt), lambda p, t, it, jt: (p, 0, jt[t]))],
            out_specs=[pl.BlockSpec((tt, LANES), lambda p, t, it, jt: (it[t], p)),
                       pl.BlockSpec((1, tt, 2), lambda p, t, it, jt: (p, it[t], 0))],
            scratch_shapes=[pltpu.VMEM((2, tt, LANES), F32), pltpu.VMEM((2, tt, LANES), F32), pltpu.VMEM((tt, LANES), F32)]),
        out_shape=[jax.ShapeDtypeStruct((s, FOX_D), F32), jax.ShapeDtypeStruct((FOX_PAIRS, s, 2), F32)],
        compiler_params=_cp(("parallel", "arbitrary")),
    )(itab, jtab, qs, kn, qkvg, ck)


def _flash_bwd(qs, kn, qkvg, do, lse, delta, ck, *, name):
    s = qs.shape[0]
    tt = _fox_tile(s)
    nq = s // tt
    itab, jtab = _tri_tables(nq, kv_major=True)
    nsteps = itab.shape[0]
    v0 = 2 * FOX_D // LANES

    def body(itab_ref, jtab_ref, q_ref, k_ref, v_ref, do_ref, lse_ref, dl_ref, ck_ref,
             dq_ref, dk_ref, dv_ref, dcq_ref, dck_ref, dq_sc, rs_sc, dk_sc, dv_sc, dc_sc):
        t = pl.program_id(1)
        i, j = itab_ref[t], jtab_ref[t]

        @pl.when(t == 0)
        def _():
            dq_sc[...] = jnp.zeros_like(dq_sc)
            rs_sc[...] = jnp.zeros_like(rs_sc)

        @pl.when(i == j)
        def _():
            dk_sc[...] = jnp.zeros_like(dk_sc)
            dv_sc[...] = jnp.zeros_like(dv_sc)
            dc_sc[...] = jnp.zeros_like(dc_sc)

        low = lax.broadcasted_iota(jnp.int32, (tt, LANES), 1) < FOX_HD
        rows = pl.ds(pl.multiple_of(i * tt, tt), tt)

        def step(diagonal):
            q2, k2 = q_ref[...], k_ref[...]
            v2 = v_ref[...].astype(BF16)
            do2 = do_ref[...].astype(BF16)
            dqs, dks, dvs = [], [], []
            for hh in range(2):
                sel = low if hh == 0 else jnp.logical_not(low)
                qh = jnp.where(sel, q2, jnp.zeros_like(q2))
                doh = jnp.where(sel, do2, jnp.zeros_like(do2))
                sc = lax.dot_general(qh, k2, NT, preferred_element_type=F32) - ck_ref[0][hh:hh + 1, :]
                if diagonal:
                    row = lax.broadcasted_iota(jnp.int32, sc.shape, 0)
                    col = lax.broadcasted_iota(jnp.int32, sc.shape, 1)
                    sc = jnp.where(row >= col, sc, NEG_INF)
                lse_b = jnp.broadcast_to(lse_ref[0][:, hh:hh + 1], (tt, LANES))
                dl_b = jnp.broadcast_to(dl_ref[0][:, hh:hh + 1], (tt, LANES))
                p = jnp.exp(sc - _lane_tile(lse_b, tt))
                dp = lax.dot_general(doh, v2, NT, preferred_element_type=F32)
                ds = p * (dp - _lane_tile(dl_b, tt))
                pb, dsb = p.astype(BF16), ds.astype(BF16)
                dvs.append(lax.dot_general(pb, do2, TN, preferred_element_type=F32))
                dks.append(lax.dot_general(dsb, q2, TN, preferred_element_type=F32))
                dqs.append(lax.dot_general(dsb, k2, NN, preferred_element_type=F32))
                rs_sc[hh, rows, :] += jnp.sum(ds, axis=1, keepdims=True)
                dc_sc[hh] += jnp.sum(ds, axis=0, keepdims=True)
            dv_sc[...] += jnp.where(low, dvs[0], dvs[1])
            dk_sc[...] += jnp.where(low, dks[0], dks[1])
            dq_sc[rows, :] += jnp.where(low, dqs[0], dqs[1])

        @pl.when(j < i)
        def _():
            step(False)

        @pl.when(j == i)
        def _():
            step(True)

        @pl.when(i == nq - 1)
        def _():
            dk_ref[...] = dk_sc[...]
            dv_ref[...] = dv_sc[...]
            dck_ref[0] = -jnp.concatenate([dc_sc[hh] for hh in range(2)], axis=0)

        @pl.when(t == nsteps - 1)
        def _():
            dq_ref[...] = dq_sc[...] * FOX_SCALE
            dcq_ref[0] = jnp.concatenate([rs_sc[hh] for hh in range(2)], axis=1)

    qside = pl.BlockSpec((tt, LANES), lambda p, t, it, jt: (it[t], p))
    kside = pl.BlockSpec((tt, LANES), lambda p, t, it, jt: (jt[t], p))
    stat = pl.BlockSpec((1, tt, 2), lambda p, t, it, jt: (p, it[t], 0))
    ckspec = pl.BlockSpec((1, 2, tt), lambda p, t, it, jt: (p, 0, jt[t]))
    return pl.pallas_call(
        body, name=name,
        grid_spec=pltpu.PrefetchScalarGridSpec(
            num_scalar_prefetch=2, grid=(FOX_PAIRS, nsteps),
            in_specs=[qside, kside, pl.BlockSpec((tt, LANES), lambda p, t, it, jt: (jt[t], v0 + p)), qside, stat, stat, ckspec],
            out_specs=[pl.BlockSpec((s, LANES), lambda p, t, it, jt: (0, p)), kside, kside,
                       pl.BlockSpec((1, s, 2), lambda p, t, it, jt: (p, 0, 0)), ckspec],
            scratch_shapes=[pltpu.VMEM((s, LANES), F32), pltpu.VMEM((2, s, 1), F32), pltpu.VMEM((tt, LANES), F32),
                            pltpu.VMEM((tt, LANES), F32), pltpu.VMEM((2, 1, tt), F32)]),
        out_shape=[jax.ShapeDtypeStruct((s, FOX_D), F32), jax.ShapeDtypeStruct((s, FOX_D), F32),
                   jax.ShapeDtypeStruct((s, FOX_D), F32), jax.ShapeDtypeStruct((FOX_PAIRS, s, 2), F32),
                   jax.ShapeDtypeStruct((FOX_PAIRS, 2, s), F32)],
        compiler_params=_cp(("parallel", "arbitrary")),
    )(itab, jtab, qs, kn, qkvg, do, lse, delta, ck)


def _ogate_fwd(o, qkvg, *, name):
    s = o.shape[0]
    tr = _pick(s, 512, 8)

    def body(o_ref, g_ref, out_ref):
        out_ref[...] = (o_ref[...] * _sigmoid(g_ref[...])).astype(BF16)

    tile = pl.BlockSpec((tr, FOX_D), lambda i: (i, 0))
    return pl.pallas_call(
        body, name=name, grid=(s // tr,), in_specs=[tile, pl.BlockSpec((tr, FOX_D), lambda i: (i, 3))],
        out_specs=tile, out_shape=jax.ShapeDtypeStruct((s, FOX_D), BF16), compiler_params=_cp(("parallel",)),
    )(o, qkvg)


def _ogate_bwd(dog, o, qkvg, *, name):
    s = o.shape[0]
    tr = _pick(s, 512, 8)

    def body(dog_ref, o_ref, g_ref, do_ref, dg_ref, dl_ref):
        sg = _sigmoid(g_ref[...])
        ov = o_ref[...]
        dog_v = dog_ref[...]
        do = dog_v * sg
        do_ref[...] = do
        dg_ref[...] = (dog_v * ov * sg * (1.0 - sg)).astype(BF16)
        dl_ref[...] = _dotf(do * ov, _group_matrix(FOX_D, FOX_HD))

    tile = pl.BlockSpec((tr, FOX_D), lambda i: (i, 0))
    return pl.pallas_call(
        body, name=name, grid=(s // tr,), in_specs=[tile, tile, pl.BlockSpec((tr, FOX_D), lambda i: (i, 3))],
        out_specs=[tile, tile, pl.BlockSpec((tr, FOX_H), lambda i: (i, 0))],
        out_shape=[jax.ShapeDtypeStruct((s, FOX_D), F32), jax.ShapeDtypeStruct((s, FOX_D), BF16),
                   jax.ShapeDtypeStruct((s, FOX_H), F32)],
        compiler_params=_cp(("parallel",)),
    )(dog, o, qkvg)


def _loss_head(h, g, target, *, name):
    s, d = h.shape
    tr = _pick(s, 512, 8)

    def body(h_ref, g_ref, t_ref, loss_ref, dh_ref, dg_ref):
        i = pl.program_id(0)
        x = h_ref[...]
        gv = g_ref[...]
        r = lax.rsqrt(jnp.mean(x * x, axis=-1, keepdims=True) + EPS)
        xh = x * r
        err = xh * gv - t_ref[...]
        part = 0.5 * jnp.sum(jnp.sum(err * err, axis=1, keepdims=True) * (1.0 / d), axis=0, keepdims=True)
        dy = err * (1.0 / d)
        dyg = dy * gv
        dh_ref[...] = r * (dyg - xh * jnp.mean(dyg * xh, axis=-1, keepdims=True))
        dgp = jnp.sum(dy * xh, axis=0, keepdims=True)

        @pl.when(i == 0)
        def _():
            loss_ref[...] = jnp.zeros_like(loss_ref) + part
            dg_ref[...] = dgp

        @pl.when(i > 0)
        def _():
            loss_ref[...] += part
            dg_ref[...] += dgp

    tile = pl.BlockSpec((tr, d), lambda i: (i, 0))
    vec = pl.BlockSpec((1, d), lambda i: (0, 0))
    return pl.pallas_call(
        body, name=name, grid=(s // tr,), in_specs=[tile, vec, tile],
        out_specs=[pl.BlockSpec((1, LANES), lambda i: (0, 0)), tile, vec],
        out_shape=[jax.ShapeDtypeStruct((1, LANES), F32), jax.ShapeDtypeStruct((s, d), F32),
                   jax.ShapeDtypeStruct((1, d), F32)],
        compiler_params=_cp(("arbitrary",)),
    )(h, g, target)


def _adamw(w, g, m, v, *, name):
    rows, cols = w.shape
    tr = _pick(rows, 256, 8)
    c1 = 1.0 - ADAM_B1 ** ADAM_STEP
    c2 = 1.0 - ADAM_B2 ** ADAM_STEP

    def body(w_ref, g_ref, m_ref, v_ref, d_ref, nm_ref, nv_ref):
        gv = g_ref[...]
        nm = ADAM_B1 * m_ref[...] + (1.0 - ADAM_B1) * gv
        nv = ADAM_B2 * v_ref[...] + (1.0 - ADAM_B2) * (gv * gv)
        d_ref[...] = -ADAM_LR * ((nm / c1) / (jnp.sqrt(nv / c2) + ADAM_EPS) + ADAM_WD * w_ref[...])
        nm_ref[...] = nm
        nv_ref[...] = nv

    tile = pl.BlockSpec((tr, cols), lambda i: (i, 0))
    shp = jax.ShapeDtypeStruct((rows, cols), F32)
    return pl.pallas_call(
        body, name=name, grid=(rows // tr,), in_specs=[tile] * 4, out_specs=[tile] * 3, out_shape=[shp] * 3,
        compiler_params=_cp(("parallel",)),
    )(w, g, m, v)


ANY = pl.BlockSpec(memory_space=pl.ANY)
N_DEV = 8


def _coords():
    return lax.axis_index("x"), lax.axis_index("y"), lax.axis_index("c")


def _other_chips(x, y):
    return [(1 - x, y), (x, 1 - y), (1 - x, 1 - y)]


def _allgather_small(buf, *, name, with_sum):
    rows = buf.shape[0]

    def body(*refs):
        if with_sum:
            x_ref, out_ref, sum_ref, send_sems, recv_sems = refs
        else:
            x_ref, out_ref, send_sems, recv_sems = refs
        x, y, c = _coords()
        me = 4 * x + 2 * y + c
        out_ref[me] = x_ref[...]
        copies = []
        for rel in range(1, N_DEV):
            px = (1 - x) if rel & 4 else x
            py = (1 - y) if rel & 2 else y
            pc = (1 - c) if rel & 1 else c
            cp = pltpu.make_async_remote_copy(
                src_ref=x_ref, dst_ref=out_ref.at[me], send_sem=send_sems.at[rel - 1], recv_sem=recv_sems.at[rel - 1],
                device_id=(px, py, pc), device_id_type=MESH)
            cp.start()
            copies.append(cp)
        for cp in copies:
            cp.wait()
        if with_sum:
            acc = out_ref[0]
            for k in range(1, N_DEV):
                acc = acc + out_ref[k]
            sum_ref[...] = acc

    slots = jax.ShapeDtypeStruct((N_DEV, rows, LANES), F32)
    vm = pl.BlockSpec(memory_space=pltpu.VMEM)
    out_shape = [slots, jax.ShapeDtypeStruct((rows, LANES), F32)] if with_sum else [slots]
    return pl.pallas_call(
        body, name=name, in_specs=[vm], out_specs=[vm] * len(out_shape), out_shape=out_shape,
        scratch_shapes=[pltpu.SemaphoreType.DMA((N_DEV - 1,)), pltpu.SemaphoreType.DMA((N_DEV - 1,))],
    )(buf)


def _allgather_chips(shards, *, name):
    n = len(shards)

    def body(*refs):
        ins, outs = refs[:n], refs[n:2 * n]
        send_sems, recv_sems, local_sems = refs[2 * n:]
        x, y, c = _coords()
        k = 2 * x + y
        chips = _other_chips(x, y)
        sibling = (x, y, 1 - c)
        local = []
        for t in range(n):
            cp = pltpu.make_async_copy(ins[t], outs[t].at[k], local_sems.at[t])
            cp.start()
            local.append(cp)
        sends = []
        for t in range(n):
            for j, (px, py) in enumerate(chips):
                cp = pltpu.make_async_remote_copy(
                    src_ref=ins[t].at[c], dst_ref=outs[t].at[k, c], send_sem=send_sems.at[6 * t + j],
                    recv_sem=recv_sems.at[6 * t + j], device_id=(px, py, c), device_id_type=MESH)
                cp.start()
                sends.append(cp)
        for t in range(n):
            for j, (px, py) in enumerate(chips):
                kj = 2 * px + py
                pltpu.make_async_remote_copy(
                    src_ref=ins[t].at[c], dst_ref=outs[t].at[kj, c], send_sem=send_sems.at[6 * t + j],
                    recv_sem=recv_sems.at[6 * t + j], device_id=(px, py, c), device_id_type=MESH).wait_recv()
                cp = pltpu.make_async_remote_copy(
                    src_ref=outs[t].at[kj, c], dst_ref=outs[t].at[kj, c], send_sem=send_sems.at[6 * t + 3 + j],
                    recv_sem=recv_sems.at[6 * t + 3 + j], device_id=sibling, device_id_type=MESH)
                cp.start()
                sends.append(cp)
        for t in range(n):
            for j, (px, py) in enumerate(chips):
                kj = 2 * px + py
                pltpu.make_async_remote_copy(
                    src_ref=outs[t].at[kj, 1 - c], dst_ref=outs[t].at[kj, 1 - c], send_sem=send_sems.at[6 * t + 3 + j],
                    recv_sem=recv_sems.at[6 * t + 3 + j], device_id=sibling, device_id_type=MESH).wait_recv()
        for cp in sends:
            cp.wait_send()
        for cp in local:
            cp.wait()

    return pl.pallas_call(
        body, name=name, in_specs=[ANY] * n, out_specs=[ANY] * n,
        out_shape=[jax.ShapeDtypeStruct((4,) + s.shape, s.dtype) for s in shards],
        scratch_shapes=[pltpu.SemaphoreType.DMA((6 * n,)), pltpu.SemaphoreType.DMA((6 * n,)), pltpu.SemaphoreType.DMA((n,))],
    )(*shards)


def _sibling_swap(arrs, *, name, other_half):
    n = len(arrs)

    def body(*refs):
        ins, outs = refs[:n], refs[n:2 * n]
        send_sems, recv_sems = refs[2 * n:]
        x, y, c = _coords()
        copies = []
        for t in range(n):
            cp = pltpu.make_async_remote_copy(
                src_ref=ins[t].at[1 - c] if other_half else ins[t], dst_ref=outs[t], send_sem=send_sems.at[t],
                recv_sem=recv_sems.at[t], device_id=(x, y, 1 - c), device_id_type=MESH)
            cp.start()
            copies.append(cp)
        for cp in copies:
            cp.wait()

    return pl.pallas_call(
        body, name=name, in_specs=[ANY] * n, out_specs=[ANY] * n,
        out_shape=[jax.ShapeDtypeStruct(a.shape[1:] if other_half else a.shape, a.dtype) for a in arrs],
        scratch_shapes=[pltpu.SemaphoreType.DMA((n,)), pltpu.SemaphoreType.DMA((n,))],
    )(*arrs)


def _chip_exchange(arrs, *, name):
    n = len(arrs)

    def body(*refs):
        ins, outs = refs[:n], refs[n:2 * n]
        send_sems, recv_sems = refs[2 * n:]
        x, y, c = _coords()
        copies = []
        for t in range(n):
            for j, (px, py) in enumerate(_other_chips(x, y)):
                cp = pltpu.make_async_remote_copy(
                    src_ref=ins[t].at[2 * px + py], dst_ref=outs[t].at[j], send_sem=send_sems.at[3 * t + j],
                    recv_sem=recv_sems.at[3 * t + j], device_id=(px, py, c), device_id_type=MESH)
                cp.start()
                copies.append(cp)
        for cp in copies:
            cp.wait()

    return pl.pallas_call(
        body, name=name, in_specs=[ANY] * n, out_specs=[ANY] * n,
        out_shape=[jax.ShapeDtypeStruct((3,) + a.shape[1:], a.dtype) for a in arrs],
        scratch_shapes=[pltpu.SemaphoreType.DMA((3 * n,)), pltpu.SemaphoreType.DMA((3 * n,))],
    )(*arrs)


def _add_selected(stack, others, sel, *, name):
    _, m, cols = stack.shape
    q = others.shape[0]
    tr = _pick(m, 256, 8)

    def body(sel_ref, s_ref, o_ref, out_ref):
        acc = s_ref[0]
        for i in range(q):
            acc = acc + o_ref[i]
        out_ref[...] = acc

    return pl.pallas_call(
        body, name=name,
        grid_spec=pltpu.PrefetchScalarGridSpec(
            num_scalar_prefetch=1, grid=(m // tr,),
            in_specs=[pl.BlockSpec((1, tr, cols), lambda i, sel_ref: (sel_ref[0], i, 0)),
                      pl.BlockSpec((q, tr, cols), lambda i, sel_ref: (0, i, 0))],
            out_specs=pl.BlockSpec((tr, cols), lambda i, sel_ref: (i, 0))),
        out_shape=jax.ShapeDtypeStruct((m, cols), F32),
        compiler_params=_cp(("parallel",)),
    )(sel, stack, others)


def _reduce_scatter(grads, c, k):
    n = len(grads)
    csel, ksel = jnp.reshape(c, (1,)).astype(jnp.int32), jnp.reshape(k, (1,)).astype(jnp.int32)
    from_sib = _sibling_swap(grads, name="rs_pair_swap", other_half=True)
    chip_sums = []
    for t in range(n):
        _, _, m, cols = grads[t].shape
        mine = grads[t].reshape(2, 4 * m, cols)
        s = _add_selected(mine, from_sib[t].reshape(1, 4 * m, cols), csel, name=f"rs_pair_add{t}")
        chip_sums.append(s.reshape(4, m, cols))
    from_chips = _chip_exchange(chip_sums, name="rs_chip_exchange")
    finals = [_add_selected(chip_sums[t], from_chips[t], ksel, name=f"rs_chip_add{t}") for t in range(n)]
    others = _sibling_swap(finals, name="rs_result_swap", other_half=False)
    return finals, others


BIG = ("ssd_w_in", "ssd_w_out", "fox_w_in", "fox_w_out", "ffn_w_up", "ffn_w_down")
COL_SHARDED = ("ssd_w_in", "fox_w_in", "ffn_w_up")
SMALL = (("mix_norm_g", (4, 1024)), ("ffn_norm_g", (4, 1024)), ("ssd_conv_w", (2, 4, 3072)), ("ssd_conv_b", (2, 3072)),
         ("ssd_dt_bias", (2, 32)), ("ssd_a_log", (2, 32)), ("ssd_d", (2, 32)), ("ssd_norm_g", (2, 2048)),
         ("fox_b_f", (2, 16)), ("fox_q_norm_g", (2, 64)), ("fox_k_norm_g", (2, 64)), ("ffn_conv_w", (4, 3, 2816)),
         ("ffn_conv_b", (4, 2816)), ("final_norm_g", (1024,)), ("loss", (1,)))
NAMES = ("mix_norm_g", "ffn_norm_g", "ssd_w_in", "ssd_conv_w", "ssd_conv_b", "ssd_dt_bias", "ssd_a_log", "ssd_d",
         "ssd_norm_g", "ssd_w_out", "fox_w_in", "fox_b_f", "fox_q_norm_g", "fox_k_norm_g", "fox_w_out", "ffn_w_up",
         "ffn_conv_w", "ffn_conv_b", "ffn_w_down", "final_norm_g")


def _pack(parts):
    flat = jnp.concatenate([jnp.reshape(p, (-1,)).astype(F32) for p in parts])
    rows = -(-flat.shape[0] // (8 * LANES)) * 8
    return jnp.pad(flat, (0, rows * LANES - flat.shape[0])).reshape(rows, LANES)


def _unpack(buf, shapes):
    flat = buf.reshape(-1)
    out, off = [], 0
    for shp in shapes:
        size = 1
        for d in shp:
            size *= d
        out.append(flat[off:off + size].reshape(shp))
        off += size
    return out


def _pad_lanes(a):
    return jnp.pad(a, ((0, 0), (0, LANES - a.shape[1])))


def _pad8(w):
    return jnp.pad(w, ((0, 8 - w.shape[0]), (0, 0)))


def _ssd_forward(h, p, name):
    s = h.shape[0]
    hn = _rms_fwd(h, p["mix_g"], gw=D_MODEL, ncol=1, name=f"{name}_norm")
    zx = _matmul(hn, p["w_zx"], mode="nn", name=f"{name}_proj")
    dtp = _matmul(hn, p["w_dt"], mode="nn", name=f"{name}_proj_dt")
    xbc = _conv_fwd(zx, p["conv_w8"], p["conv_b"], kw=SSD_K, width=SSD_CONV_DIM, u_col0=SSD_DI, name=f"{name}_conv")
    dt3 = dtp[:, :SSD_H].reshape(s, SSD_G, SSD_HPG)
    dtg, dtg_t = jnp.transpose(dt3, (1, 0, 2)), jnp.transpose(dt3, (1, 2, 0))
    sp = (p["bias_r"], p["bias_c"], p["alog_r"], p["alog_c"], p["d_r"])
    y, hprev = _ssd_fwd(xbc, dtg, dtg_t, *sp, name=f"{name}_scan")
    y2 = _rms_fwd(y, p["norm_g"], gw=SSD_DI // SSD_G, ncol=SSD_G, z=zx, name=f"{name}_gnorm")
    out = _matmul(y2, p["w_out"], mode="nn", add=h, name=f"{name}_out")
    return out, dict(h=h, hn=hn, zx=zx, xbc=xbc, dtg=dtg, dtg_t=dtg_t, y=y, hprev=hprev, y2=y2)


def _ssd_backward(dh1, p, a, name):
    s = dh1.shape[0]
    g = {}
    dy2 = _matmul(dh1, p["w_out"], mode="nt", name=f"{name}_out_dx")
    g["w_out"] = _matmul(a["y2"], dh1, mode="tn", name=f"{name}_out_dw")
    dy, dz, g["norm_g"] = _rms_bwd(a["y"], p["norm_g"], dy2, gw=SSD_DI // SSD_G, ncol=SSD_G, z=a["zx"], name=f"{name}_gnorm_b")
    sp = (p["bias_r"], p["bias_c"], p["alog_r"], p["alog_c"], p["d_r"])
    dx, dbm, dcm, ddt, g["dt_bias"], g["a_log"], g["d"] = _ssd_bwd(
        a["xbc"], a["dtg"], a["dtg_t"], *sp, a["hprev"], dy, name=f"{name}_scan_b")
    dact = jnp.concatenate([dx, dbm, dcm], axis=1)
    dxbc, dwb = _conv_bwd(a["zx"], p["conv_w8"], p["conv_b"], dact, kw=SSD_K, width=SSD_CONV_DIM, u_col0=SSD_DI,
                          name=f"{name}_conv_b")
    g["conv_w"], g["conv_b"] = dwb[:SSD_K], dwb[7]
    dzx = jnp.concatenate([dz.astype(BF16), dxbc], axis=1)
    ddtp = _pad_lanes(jnp.transpose(ddt, (1, 0, 2)).reshape(s, SSD_H))
    dhn = _matmul(dzx, p["w_zx"], mode="nt", name=f"{name}_proj_dx")
    dhn = _matmul(ddtp, p["w_dt"], mode="nt", add=dhn, name=f"{name}_proj_dt_dx")
    dw_zx = _matmul(a["hn"], dzx, mode="tn", name=f"{name}_proj_dw")
    dw_dt = _matmul(a["hn"], ddtp, mode="tn", name=f"{name}_proj_dt_dw")
    g["w_in"] = jnp.concatenate([dw_zx, dw_dt[:, :SSD_H]], axis=1)
    dh, g["mix_g"] = _rms_bwd(a["h"], p["mix_g"], dhn, gw=D_MODEL, ncol=1, add=dh1, name=f"{name}_norm_b")
    return dh, g


def _fox_forward(h, p, name):
    s = h.shape[0]
    hn = _rms_fwd(h, p["mix_g"], gw=D_MODEL, ncol=1, name=f"{name}_norm")
    qkvg = _matmul(hn, p["w_qkvg"], mode="nn", name=f"{name}_proj")
    fp = _matmul(hn, p["w_f"], mode="nn", name=f"{name}_proj_f")
    qs = _rms_fwd(qkvg, p["gq"] * FOX_SCALE, gw=FOX_D, ncol=1, x_col0=0, sub=FOX_HD, name=f"{name}_qnorm")
    kn = _rms_fwd(qkvg, p["gk"], gw=FOX_D, ncol=1, x_col0=1, sub=FOX_HD, name=f"{name}_knorm")
    f_t = jnp.transpose(fp[:, :FOX_H])
    cum_t = _fgate_fwd(f_t, p["b_f"], name=f"{name}_fgate")
    ck = cum_t.reshape(FOX_PAIRS, 2, s)
    o, lse = _flash_fwd(qs, kn, qkvg, ck, name=f"{name}_attn")
    og = _ogate_fwd(o, qkvg, name=f"{name}_ogate")
    out = _matmul(og, p["w_out"], mode="nn", add=h, name=f"{name}_out")
    return out, dict(h=h, hn=hn, qkvg=qkvg, qs=qs, kn=kn, f_t=f_t, ck=ck, o=o, lse=lse, og=og)


def _fox_backward(dh1, p, a, name):
    s = dh1.shape[0]
    g = {}
    dog = _matmul(dh1, p["w_out"], mode="nt", name=f"{name}_out_dx")
    g["w_out"] = _matmul(a["og"], dh1, mode="tn", name=f"{name}_out_dw")
    do, dgate, delta = _ogate_bwd(dog, a["o"], a["qkvg"], name=f"{name}_ogate_b")
    dl = jnp.transpose(delta.reshape(s, FOX_PAIRS, 2), (1, 0, 2))
    dq, dk, dv, dcq, dck = _flash_bwd(a["qs"], a["kn"], a["qkvg"], do, a["lse"], dl, a["ck"], name=f"{name}_attn_b")
    dq_raw, dgq = _rms_bwd(a["qkvg"], p["gq"], dq, gw=FOX_D, ncol=1, x_col0=0, sub=FOX_HD, dx_dtype=BF16, name=f"{name}_qnorm_b")
    dk_raw, dgk = _rms_bwd(a["qkvg"], p["gk"], dk, gw=FOX_D, ncol=1, x_col0=1, sub=FOX_HD, dx_dtype=BF16, name=f"{name}_knorm_b")
    g["gq"] = dgq.reshape(FOX_H, FOX_HD).sum(axis=0)
    g["gk"] = dgk.reshape(FOX_H, FOX_HD).sum(axis=0)
    dcq_t = jnp.transpose(dcq, (0, 2, 1)).reshape(FOX_H, s)
    df_t, dbf = _fgate_bwd(dcq_t, dck.reshape(FOX_H, s), a["f_t"], p["b_f"], name=f"{name}_fgate_b")
    g["b_f"] = dbf[:, 0]
    dproj = jnp.concatenate([dq_raw, dk_raw, dv.astype(BF16), dgate], axis=1)
    dfp = _pad_lanes(jnp.transpose(df_t))
    dhn = _matmul(dproj, p["w_qkvg"], mode="nt", name=f"{name}_proj_dx")
    dhn = _matmul(dfp, p["w_f"], mode="nt", add=dhn, name=f"{name}_proj_f_dx")
    dw_qkvg = _matmul(a["hn"], dproj, mode="tn", name=f"{name}_proj_dw")
    dw_f = _matmul(a["hn"], dfp, mode="tn", name=f"{name}_proj_f_dw")
    g["w_in"] = jnp.concatenate([dw_qkvg, dw_f[:, :FOX_H]], axis=1)
    dh, g["mix_g"] = _rms_bwd(a["h"], p["mix_g"], dhn, gw=D_MODEL, ncol=1, add=dh1, name=f"{name}_norm_b")
    return dh, g


def _ffn_forward(h, p, name):
    hn = _rms_fwd(h, p["ffn_g"], gw=D_MODEL, ncol=1, name=f"{name}_norm")
    u = _matmul(hn, p["w_up"], mode="nn", name=f"{name}_up")
    act = _conv_fwd(u, p["conv_w8"], p["conv_b"], kw=FFN_K, width=D_FF, u_col0=0, mul_col0=D_FF, out_dtype=BF16,
                    name=f"{name}_glu")
    out = _matmul(act, p["w_down"], mode="nn", add=h, name=f"{name}_down")
    return out, dict(h=h, hn=hn, u=u, act=act)


def _ffn_backward(dh2, p, a, name):
    g = {}
    dact = _matmul(dh2, p["w_down"], mode="nt", name=f"{name}_down_dx")
    g["w_down"] = _matmul(a["act"], dh2, mode="tn", name=f"{name}_down_dw")
    du1, du2, dwb = _conv_bwd(a["u"], p["conv_w8"], p["conv_b"], dact, kw=FFN_K, width=D_FF, u_col0=0, mul_col0=D_FF,
                              name=f"{name}_glu_b")
    g["conv_w"], g["conv_b"] = dwb[:FFN_K], dwb[7]
    du = jnp.concatenate([du1, du2], axis=1)
    dhn = _matmul(du, p["w_up"], mode="nt", name=f"{name}_up_dx")
    g["w_up"] = _matmul(a["hn"], du, mode="tn", name=f"{name}_up_dw")
    dh, g["ffn_g"] = _rms_bwd(a["h"], p["ffn_g"], dhn, gw=D_MODEL, ncol=1, add=dh2, name=f"{name}_norm_b")
    return dh, g


def _to_slabs(full, col_sharded):
    nl, r, ct = full.shape
    if col_sharded:
        t = jnp.transpose(full.reshape(2, nl // 2, r, 4, ct // 4), (0, 3, 1, 2, 4))
        return t.reshape(2, 4, (nl // 2) * r, ct // 4)
    t = jnp.transpose(full.reshape(2, nl // 2, 4, r // 4, ct), (0, 2, 1, 3, 4))
    return t.reshape(2, 4, (nl // 2) * (r // 4), ct)


def kernel(x, mix_norm_g, ffn_norm_g, ssd_w_in, ssd_conv_w, ssd_conv_b, ssd_dt_bias, ssd_a_log, ssd_d, ssd_norm_g, ssd_w_out, fox_w_in, fox_b_f, fox_q_norm_g, fox_k_norm_g, fox_w_out, ffn_w_up, ffn_conv_w, ffn_conv_b, ffn_w_down, final_norm_g, loss_target, m_mix_norm_g, m_ffn_norm_g, m_ssd_w_in, m_ssd_conv_w, m_ssd_conv_b, m_ssd_dt_bias, m_ssd_a_log, m_ssd_d, m_ssd_norm_g, m_ssd_w_out, m_fox_w_in, m_fox_b_f, m_fox_q_norm_g, m_fox_k_norm_g, m_fox_w_out, m_ffn_w_up, m_ffn_conv_w, m_ffn_conv_b, m_ffn_w_down, m_final_norm_g, v_mix_norm_g, v_ffn_norm_g, v_ssd_w_in, v_ssd_conv_w, v_ssd_conv_b, v_ssd_dt_bias, v_ssd_a_log, v_ssd_d, v_ssd_norm_g, v_ssd_w_out, v_fox_w_in, v_fox_b_f, v_fox_q_norm_g, v_fox_k_norm_g, v_fox_w_out, v_ffn_w_up, v_ffn_conv_w, v_ffn_conv_b, v_ffn_w_down, v_final_norm_g):
    w = dict(mix_norm_g=mix_norm_g, ffn_norm_g=ffn_norm_g, ssd_w_in=ssd_w_in, ssd_conv_w=ssd_conv_w, ssd_conv_b=ssd_conv_b,
             ssd_dt_bias=ssd_dt_bias, ssd_a_log=ssd_a_log, ssd_d=ssd_d, ssd_norm_g=ssd_norm_g, ssd_w_out=ssd_w_out,
             fox_w_in=fox_w_in, fox_b_f=fox_b_f, fox_q_norm_g=fox_q_norm_g, fox_k_norm_g=fox_k_norm_g, fox_w_out=fox_w_out,
             ffn_w_up=ffn_w_up, ffn_conv_w=ffn_conv_w, ffn_conv_b=ffn_conv_b, ffn_w_down=ffn_w_down, final_norm_g=final_norm_g)
    m_in = dict(zip(NAMES, (m_mix_norm_g, m_ffn_norm_g, m_ssd_w_in, m_ssd_conv_w, m_ssd_conv_b, m_ssd_dt_bias, m_ssd_a_log,
                            m_ssd_d, m_ssd_norm_g, m_ssd_w_out, m_fox_w_in, m_fox_b_f, m_fox_q_norm_g, m_fox_k_norm_g,
                            m_fox_w_out, m_ffn_w_up, m_ffn_conv_w, m_ffn_conv_b, m_ffn_w_down, m_final_norm_g)))
    v_in = dict(zip(NAMES, (v_mix_norm_g, v_ffn_norm_g, v_ssd_w_in, v_ssd_conv_w, v_ssd_conv_b, v_ssd_dt_bias, v_ssd_a_log,
                            v_ssd_d, v_ssd_norm_g, v_ssd_w_out, v_fox_w_in, v_fox_b_f, v_fox_q_norm_g, v_fox_k_norm_g,
                            v_fox_w_out, v_ffn_w_up, v_ffn_conv_w, v_ffn_conv_b, v_ffn_w_down, v_final_norm_g)))
    cx, cy, cc = _coords()
    chip = 2 * cx + cy
    h = x[0]
    target = loss_target[0]

    conv_shapes = [ssd_conv_w.shape, ffn_conv_w.shape]
    slots = _allgather_small(_pack([ssd_conv_w, ffn_conv_w]), name="gather_conv_w", with_sum=False)[0]
    per_chip = [_unpack(slots[2 * q], conv_shapes) for q in range(4)]
    ssd_conv_full = jnp.concatenate([pc[0] for pc in per_chip], axis=2)
    ffn_conv_full = jnp.concatenate([pc[1] for pc in per_chip], axis=2)
    shards = [w[n].astype(BF16) for n in BIG]
    gathered = _allgather_chips([s.reshape((2, s.shape[0] // 2) + s.shape[1:]) for s in shards], name="gather_weights")
    full = {}
    for n, gth in zip(BIG, gathered):
        gth = gth.reshape((4, gth.shape[1] * gth.shape[2]) + gth.shape[3:])
        if n in COL_SHARDED:
            full[n] = jnp.transpose(gth, (1, 2, 0, 3)).reshape(gth.shape[1], gth.shape[2], 4 * gth.shape[3])
        else:
            full[n] = jnp.transpose(gth, (1, 0, 2, 3)).reshape(gth.shape[1], 4 * gth.shape[2], gth.shape[3])

    def ssd_params(j, i):
        w_in = full["ssd_w_in"][j]
        g3 = lambda v: v.reshape(SSD_G, 1, SSD_HPG)
        g3c = lambda v: v.reshape(SSD_G, SSD_HPG, 1)
        return dict(mix_g=mix_norm_g[i][None], w_zx=w_in[:, :SSD_ZX], w_dt=_pad_lanes(w_in[:, SSD_ZX:]),
                    conv_w8=_pad8(ssd_conv_full[j]), conv_b=ssd_conv_b[j][None], bias_r=g3(ssd_dt_bias[j]),
                    bias_c=g3c(ssd_dt_bias[j]), alog_r=g3(ssd_a_log[j]), alog_c=g3c(ssd_a_log[j]), d_r=g3(ssd_d[j]),
                    norm_g=ssd_norm_g[j][None], w_out=full["ssd_w_out"][j])

    def fox_params(j, i):
        w_in = full["fox_w_in"][j]
        return dict(mix_g=mix_norm_g[i][None], w_qkvg=w_in[:, :4 * FOX_D], w_f=_pad_lanes(w_in[:, 4 * FOX_D:]),
                    gq=jnp.tile(fox_q_norm_g[j], FOX_H)[None], gk=jnp.tile(fox_k_norm_g[j], FOX_H)[None],
                    b_f=fox_b_f[j][:, None], w_out=full["fox_w_out"][j])

    def ffn_params(i):
        return dict(ffn_g=ffn_norm_g[i][None], w_up=full["ffn_w_up"][i], conv_w8=_pad8(ffn_conv_full[i]),
                    conv_b=ffn_conv_b[i][None], w_down=full["ffn_w_down"][i])

    mix_p, ffn_p, mix_a, ffn_a = [], [], [], []
    for i in range(DEPTH):
        j = i // 2
        if i % 2 == 0:
            mix_p.append(ssd_params(j, i))
            h, act = _ssd_forward(h, mix_p[i], f"ssd{j}")
        else:
            mix_p.append(fox_params(j, i))
            h, act = _fox_forward(h, mix_p[i], f"fox{j}")
        mix_a.append(act)
        ffn_p.append(ffn_params(i))
        h, act = _ffn_forward(h, ffn_p[i], f"ffn{i}")
        ffn_a.append(act)
    loss_part, dh, d_final_g = _loss_head(h, final_norm_g[None], target, name="loss_head")

    mix_g, ffn_g = [None] * DEPTH, [None] * DEPTH
    for i in reversed(range(DEPTH)):
        j = i // 2
        dh, ffn_g[i] = _ffn_backward(dh, ffn_p[i], ffn_a[i], f"ffn{i}")
        if i % 2 == 0:
            dh, mix_g[i] = _ssd_backward(dh, mix_p[i], mix_a[i], f"ssd{j}")
        else:
            dh, mix_g[i] = _fox_backward(dh, mix_p[i], mix_a[i], f"fox{j}")
    grad_x = dh[None]
    ssd_g, fox_g = [mix_g[0], mix_g[2]], [mix_g[1], mix_g[3]]

    big_full = dict(
        ssd_w_in=jnp.stack([g["w_in"] for g in ssd_g]), ssd_w_out=jnp.stack([g["w_out"] for g in ssd_g]),
        fox_w_in=jnp.stack([g["w_in"] for g in fox_g]), fox_w_out=jnp.stack([g["w_out"] for g in fox_g]),
        ffn_w_up=jnp.stack([g["w_up"] for g in ffn_g]), ffn_w_down=jnp.stack([g["w_down"] for g in ffn_g]))
    mine, theirs = _reduce_scatter([_to_slabs(big_full[n], n in COL_SHARDED) for n in BIG], cc, chip)
    grads = {}
    for n, a, b in zip(BIG, mine, theirs):
        shp = w[n].shape
        a = a.reshape((shp[0] // 2,) + shp[1:])
        b = b.reshape((shp[0] // 2,) + shp[1:])
        grads[n] = jnp.where(cc == 0, jnp.concatenate([a, b], axis=0), jnp.concatenate([b, a], axis=0))
    small = dict(
        mix_norm_g=jnp.concatenate([g["mix_g"] for g in mix_g], axis=0),
        ffn_norm_g=jnp.concatenate([g["ffn_g"] for g in ffn_g], axis=0),
        ssd_conv_w=jnp.stack([g["conv_w"] for g in ssd_g]), ssd_conv_b=jnp.stack([g["conv_b"] for g in ssd_g]),
        ssd_dt_bias=jnp.stack([g["dt_bias"].reshape(SSD_H) for g in ssd_g]),
        ssd_a_log=jnp.stack([g["a_log"].reshape(SSD_H) for g in ssd_g]),
        ssd_d=jnp.stack([g["d"].reshape(SSD_H) for g in ssd_g]),
        ssd_norm_g=jnp.concatenate([g["norm_g"] for g in ssd_g], axis=0),
        fox_b_f=jnp.stack([g["b_f"] for g in fox_g]), fox_q_norm_g=jnp.stack([g["gq"] for g in fox_g]),
        fox_k_norm_g=jnp.stack([g["gk"] for g in fox_g]),
        ffn_conv_w=jnp.stack([g["conv_w"] for g in ffn_g]), ffn_conv_b=jnp.stack([g["conv_b"] for g in ffn_g]),
        final_norm_g=d_final_g[0], loss=loss_part[0, :1])
    _, total = _allgather_small(_pack([small[n] for n, _ in SMALL]), name="reduce_small", with_sum=True)
    for (n, shp), val in zip(SMALL, _unpack(total, [shp for _, shp in SMALL])):
        grads[n] = val
    loss = grads.pop("loss")[0]
    grads["ssd_conv_w"] = lax.dynamic_slice_in_dim(grads["ssd_conv_w"], chip * ssd_conv_w.shape[2], ssd_conv_w.shape[2], axis=2)
    grads["ffn_conv_w"] = lax.dynamic_slice_in_dim(grads["ffn_conv_w"], chip * ffn_conv_w.shape[2], ffn_conv_w.shape[2], axis=2)

    deltas, new_m, new_v = {}, {}, {}
    for n in NAMES:
        shp = w[n].shape
        two_d = (1, shp[0]) if len(shp) == 1 else (-1, shp[-1])
        r2 = lambda a: a.reshape(two_d)
        d, nm, nv = _adamw(r2(w[n]), r2(grads[n]), r2(m_in[n]), r2(v_in[n]), name=f"adamw_{n}")
        deltas[n], new_m[n], new_v[n] = d.reshape(shp), nm.reshape(shp), nv.reshape(shp)
    return (loss, grad_x, *[grads[n] for n in NAMES], *[deltas[n] for n in NAMES], *[new_m[n] for n in NAMES],
            *[new_v[n] for n in NAMES])
```

```python
import functools

import jax
import jax.numpy as jnp
from jax import lax
from jax.experimental import pallas as pl
from jax.experimental.pallas import tpu as pltpu

F32 = jnp.float32
BF16 = jnp.bfloat16
HI = lax.Precision.HIGHEST
MESH = pl.DeviceIdType.MESH

D_MODEL = 1024
DEPTH = 4
EPS = 1e-6
SSD_DI = 2048
SSD_HD = 64
SSD_G = 4
SSD_HPG = 8
SSD_N = 128
SSD_K = 4
CHUNK = 128
SSD_CONV_DIM = 3072
SSD_ZX = SSD_DI + SSD_CONV_DIM
SSD_H = 32
FOX_HD = 64
FOX_H = 16
FOX_D = 1024
D_FF = 2816
FFN_K = 3
LANES = 128
VMEM_LIMIT = 56 * 1024 * 1024

ADAM_LR = 0.001
ADAM_B1 = 0.9
ADAM_B2 = 0.999
ADAM_EPS = 1e-08
ADAM_WD = 0.01
ADAM_STEP = 10

NN = (((1,), (0,)), ((), ()))
NT = (((1,), (1,)), ((), ()))
TN = (((0,), (0,)), ((), ()))


def _pick(n, cap, mult=LANES):
    best = None
    for t in range(mult, min(n, cap) + 1, mult):
        if n % t == 0:
            best = t
    return best if best is not None else n


def _cp(sem):
    return pltpu.CompilerParams(dimension_semantics=sem, vmem_limit_bytes=VMEM_LIMIT)


def _sigmoid(x):
    return jax.nn.sigmoid(x)


def _silu(x):
    return x * _sigmoid(x)


def _dsilu(x):
    s = _sigmoid(x)
    return s * (1.0 + x * (1.0 - s))


def _softplus(x):
    e = jnp.exp(-jnp.abs(x))
    u = 1.0 + e
    l1p = jnp.where(u == 1.0, e, jnp.log(u) * (e / (u - 1.0)))
    return jnp.maximum(x, 0.0) + l1p


def _dotf(a, b, dn=NN, *, onehot="b", pieces=2):
    x, e = (a, b) if onehot == "b" else (b, a)
    e = e.astype(BF16)
    acc = None
    for n in range(pieces):
        hi = x.astype(BF16)
        part = lax.dot_general(hi, e, dn, preferred_element_type=F32) if onehot == "b" else \
            lax.dot_general(e, hi, dn, preferred_element_type=F32)
        acc = part if acc is None else acc + part
        if n + 1 < pieces:
            x = x - hi.astype(F32)
    return acc


def _dotb(a, b, dn=NN):
    return lax.dot_general(a.astype(BF16), b.astype(BF16), dn, preferred_element_type=F32)


def _group_matrix(width, sub, transpose=False):
    ng = width // sub
    shape = (ng, width) if transpose else (width, ng)
    lane = lax.broadcasted_iota(jnp.int32, shape, 1 if transpose else 0)
    grp = lax.broadcasted_iota(jnp.int32, shape, 0 if transpose else 1)
    return (lane // sub == grp).astype(F32)


def _gmean(v, sub):
    width = v.shape[-1]
    if sub == width:
        return jnp.mean(v, axis=-1, keepdims=True)
    s = _dotf(v, _group_matrix(width, sub))
    return _dotf(s, _group_matrix(width, sub, transpose=True)) * (1.0 / sub)


def _matmul(a, b, *, mode, name, out_dtype=F32, add=None):
    if mode == "nn":
        (m, k), (k2, n) = a.shape, b.shape
    elif mode == "nt":
        (m, k), (n, k2) = a.shape, b.shape
    else:
        (k, m), (k2, n) = a.shape, b.shape
    assert k == k2, (a.shape, b.shape, mode)
    tm, tn, tk = _pick(m, 512), _pick(n, 1536), _pick(k, 1536)
    nk = k // tk
    dn = {"nn": NN, "nt": NT, "tn": TN}[mode]
    has_add = add is not None

    def body(*refs):
        if has_add:
            a_ref, b_ref, add_ref, o_ref, acc_ref = refs
        else:
            a_ref, b_ref, o_ref, acc_ref = refs
            add_ref = None
        kk = pl.program_id(2)
        part = _dotb(a_ref[...], b_ref[...], dn)

        def finish(r):
            if has_add:
                r = r + add_ref[...]
            o_ref[...] = r.astype(out_dtype)

        if nk == 1:
            finish(part)
        else:
            @pl.when(kk == 0)
            def _():
                acc_ref[...] = part

            @pl.when(kk > 0)
            def _():
                acc_ref[...] += part

            @pl.when(kk == nk - 1)
            def _():
                finish(acc_ref[...])

    if mode == "nn":
        a_spec = pl.BlockSpec((tm, tk), lambda i, j, q: (i, q))
        b_spec = pl.BlockSpec((tk, tn), lambda i, j, q: (q, j))
    elif mode == "nt":
        a_spec = pl.BlockSpec((tm, tk), lambda i, j, q: (i, q))
        b_spec = pl.BlockSpec((tn, tk), lambda i, j, q: (j, q))
    else:
        a_spec = pl.BlockSpec((tk, tm), lambda i, j, q: (q, i))
        b_spec = pl.BlockSpec((tk, tn), lambda i, j, q: (q, j))
    o_spec = pl.BlockSpec((tm, tn), lambda i, j, q: (i, j))
    in_specs = [a_spec, b_spec] + ([o_spec] if has_add else [])
    args = (a, b) + ((add,) if has_add else ())
    return pl.pallas_call(
        body, name=name, grid=(m // tm, n // tn, nk), in_specs=in_specs, out_specs=o_spec,
        out_shape=jax.ShapeDtypeStruct((m, n), out_dtype),
        scratch_shapes=[pltpu.VMEM((tm, tn), F32)],
        compiler_params=_cp(("parallel", "parallel", "arbitrary")),
    )(*args)


def _rms_fwd(x, g, *, gw, ncol, name, x_col0=0, sub=None, z=None, z_col0=0, out_dtype=BF16):
    rows = x.shape[0]
    tr = _pick(rows, 512, 8)
    sub = gw if sub is None else sub
    gated = z is not None

    def body(*refs):
        if gated:
            x_ref, z_ref, g_ref, o_ref = refs
            xv = x_ref[...] * _silu(z_ref[...])
        else:
            x_ref, g_ref, o_ref = refs
            xv = x_ref[...]
        r = lax.rsqrt(_gmean(xv * xv, sub) + EPS)
        o_ref[...] = (xv * r * g_ref[...]).astype(out_dtype)

    specs = [pl.BlockSpec((tr, gw), lambda j, i: (i, x_col0 + j))]
    args = [x]
    if gated:
        specs.append(pl.BlockSpec((tr, gw), lambda j, i: (i, z_col0 + j)))
        args.append(z)
    specs.append(pl.BlockSpec((1, gw), lambda j, i: (0, j)))
    args.append(g)
    return pl.pallas_call(
        body, name=name, grid=(ncol, rows // tr), in_specs=specs,
        out_specs=pl.BlockSpec((tr, gw), lambda j, i: (i, j)),
        out_shape=jax.ShapeDtypeStruct((rows, gw * ncol), out_dtype),
        compiler_params=_cp(("parallel", "parallel")),
    )(*args)


def _rms_bwd(x, g, dy, *, gw, ncol, name, x_col0=0, sub=None, z=None, z_col0=0, add=None, dx_dtype=F32):
    rows = x.shape[0]
    tr = _pick(rows, 512, 8)
    sub = gw if sub is None else sub
    gated = z is not None
    has_add = add is not None

    def body(*refs):
        refs = list(refs)
        x_ref = refs.pop(0)
        z_ref = refs.pop(0) if gated else None
        g_ref = refs.pop(0)
        dy_ref = refs.pop(0)
        add_ref = refs.pop(0) if has_add else None
        dx_ref = refs.pop(0)
        dz_ref = refs.pop(0) if gated else None
        dg_ref = refs.pop(0)
        i = pl.program_id(1)
        xv = x_ref[...]
        if gated:
            zz = z_ref[...]
            yz = xv * _silu(zz)
        else:
            yz = xv
        r = lax.rsqrt(_gmean(yz * yz, sub) + EPS)
        xh = yz * r
        dy = dy_ref[...].astype(F32)
        dyg = dy * g_ref[...]
        d_yz = r * (dyg - xh * _gmean(dyg * xh, sub))
        if gated:
            dx_ref[...] = (d_yz * _silu(zz)).astype(dx_dtype)
            dz_ref[...] = (d_yz * xv * _dsilu(zz)).astype(dx_dtype)
        elif has_add:
            dx_ref[...] = (d_yz + add_ref[...]).astype(dx_dtype)
        else:
            dx_ref[...] = d_yz.astype(dx_dtype)
        part = jnp.sum(dy * xh, axis=0, keepdims=True)

        @pl.when(i == 0)
        def _():
            dg_ref[...] = part

        @pl.when(i > 0)
        def _():
            dg_ref[...] += part

    tile = pl.BlockSpec((tr, gw), lambda j, i: (i, j))
    specs = [pl.BlockSpec((tr, gw), lambda j, i: (i, x_col0 + j))]
    args = [x]
    if gated:
        specs.append(pl.BlockSpec((tr, gw), lambda j, i: (i, z_col0 + j)))
        args.append(z)
    specs += [pl.BlockSpec((1, gw), lambda j, i: (0, j)), tile]
    args += [g, dy]
    if has_add:
        specs.append(tile)
        args.append(add)
    width = gw * ncol
    out_shape = [jax.ShapeDtypeStruct((rows, width), dx_dtype)]
    out_specs = [tile]
    if gated:
        out_shape.append(jax.ShapeDtypeStruct((rows, width), dx_dtype))
        out_specs.append(tile)
    out_shape.append(jax.ShapeDtypeStruct((1, width), F32))
    out_specs.append(pl.BlockSpec((1, gw), lambda j, i: (0, j)))
    return pl.pallas_call(
        body, name=name, grid=(ncol, rows // tr), in_specs=specs, out_specs=out_specs, out_shape=out_shape,
        compiler_params=_cp(("parallel", "arbitrary")),
    )(*args)


HALO = 8


def _conv_fwd(u, w8, b, *, kw, width, name, u_col0=0, mul_col0=None, out_dtype=F32):
    rows = u.shape[0]
    ts = _pick(rows, 512, 8)
    tc = _pick(width, 512)
    gated = mul_col0 is not None
    c0 = u_col0 // tc
    m0 = (mul_col0 // tc) if gated else 0
    assert u_col0 % tc == 0 and (not gated or mul_col0 % tc == 0)

    def body(*refs):
        if gated:
            cur_ref, halo_ref, mul_ref, w_ref, b_ref, o_ref, ext = refs
        else:
            cur_ref, halo_ref, w_ref, b_ref, o_ref, ext = refs
        i = pl.program_id(0)
        ext[pl.ds(0, HALO), :] = jnp.where(i == 0, 0.0, halo_ref[...])
        ext[pl.ds(HALO, ts), :] = cur_ref[...]
        pre = jnp.zeros((ts, tc), F32) + b_ref[...]
        for k in range(kw):
            pre = pre + w_ref[k:k + 1, :] * ext[pl.ds(HALO - (kw - 1) + k, ts), :]
        act = _silu(pre)
        if gated:
            act = act * mul_ref[...]
        o_ref[...] = act.astype(out_dtype)

    hb = ts // HALO
    specs = [pl.BlockSpec((ts, tc), lambda i, j: (i, c0 + j)),
             pl.BlockSpec((HALO, tc), lambda i, j: (jnp.maximum(i * hb - 1, 0), c0 + j))]
    args = [u, u]
    if gated:
        specs.append(pl.BlockSpec((ts, tc), lambda i, j: (i, m0 + j)))
        args.append(u)
    specs += [pl.BlockSpec((8, tc), lambda i, j: (0, j)), pl.BlockSpec((1, tc), lambda i, j: (0, j))]
    args += [w8, b]
    return pl.pallas_call(
        body, name=name, grid=(rows // ts, width // tc), in_specs=specs,
        out_specs=pl.BlockSpec((ts, tc), lambda i, j: (i, j)),
        out_shape=jax.ShapeDtypeStruct((rows, width), out_dtype),
        scratch_shapes=[pltpu.VMEM((ts + HALO, tc), F32)],
        compiler_params=_cp(("parallel", "parallel")),
    )(*args)


def _conv_bwd(u, w8, b, dact, *, kw, width, name, u_col0=0, mul_col0=None, du_dtype=BF16):
    rows = u.shape[0]
    ts = _pick(rows, 512, 8)
    tc = _pick(width, 512)
    gated = mul_col0 is not None
    c0 = u_col0 // tc
    m0 = (mul_col0 // tc) if gated else 0
    nt = rows // ts
    hb = ts // HALO

    def body(*refs):
        refs = list(refs)
        cur_ref, halo_ref = refs.pop(0), refs.pop(0)
        mul_ref = refs.pop(0) if gated else None
        w_ref, b_ref, da_ref = refs.pop(0), refs.pop(0), refs.pop(0)
        du_ref = refs.pop(0)
        dmul_ref = refs.pop(0) if gated else None
        dwb_ref, ext_u, ext_d = refs
        t = pl.program_id(1)
        ti = nt - 1 - t
        ext_u[pl.ds(0, HALO), :] = jnp.where(ti == 0, 0.0, halo_ref[...])
        ext_u[pl.ds(HALO, ts), :] = cur_ref[...]
        pre = jnp.zeros((ts, tc), F32) + b_ref[...]
        for k in range(kw):
            pre = pre + w_ref[k:k + 1, :] * ext_u[pl.ds(HALO - (kw - 1) + k, ts), :]
        da = da_ref[...].astype(F32)
        if gated:
            mul = mul_ref[...]
            dmul_ref[...] = (da * _silu(pre)).astype(du_dtype)
            dgp = da * mul * _dsilu(pre)
        else:
            dgp = da * _dsilu(pre)

        @pl.when(t == 0)
        def _():
            ext_d[pl.ds(ts, HALO), :] = jnp.zeros((HALO, tc), F32)
            dwb_ref[...] = jnp.zeros((8, tc), F32)

        ext_d[pl.ds(0, ts), :] = dgp
        du = jnp.zeros((ts, tc), F32)
        for k in range(kw):
            du = du + w_ref[k:k + 1, :] * ext_d[pl.ds(kw - 1 - k, ts), :]
        du_ref[...] = du.astype(du_dtype)
        for k in range(kw):
            dwb_ref[k:k + 1, :] += jnp.sum(dgp * ext_u[pl.ds(HALO - (kw - 1) + k, ts), :], axis=0, keepdims=True)
        dwb_ref[7:8, :] += jnp.sum(dgp, axis=0, keepdims=True)
        ext_d[pl.ds(ts, HALO), :] = dgp[0:HALO, :]

    specs = [pl.BlockSpec((ts, tc), lambda j, t: (nt - 1 - t, c0 + j)),
             pl.BlockSpec((HALO, tc), lambda j, t: (jnp.maximum((nt - 1 - t) * hb - 1, 0), c0 + j))]
    args = [u, u]
    if gated:
        specs.append(pl.BlockSpec((ts, tc), lambda j, t: (nt - 1 - t, m0 + j)))
        args.append(u)
    tile = pl.BlockSpec((ts, tc), lambda j, t: (nt - 1 - t, j))
    specs += [pl.BlockSpec((8, tc), lambda j, t: (0, j)), pl.BlockSpec((1, tc), lambda j, t: (0, j)), tile]
    args += [w8, b, dact]
    out_shape = [jax.ShapeDtypeStruct((rows, width), du_dtype)]
    out_specs = [tile]
    if gated:
        out_shape.append(jax.ShapeDtypeStruct((rows, width), du_dtype))
        out_specs.append(tile)
    out_shape.append(jax.ShapeDtypeStruct((8, width), F32))
    out_specs.append(pl.BlockSpec((8, tc), lambda j, t: (0, j)))
    return pl.pallas_call(
        body, name=name, grid=(width // tc, nt), in_specs=specs, out_specs=out_specs, out_shape=out_shape,
        scratch_shapes=[pltpu.VMEM((ts + HALO, tc), F32), pltpu.VMEM((ts + HALO, tc), F32)],
        compiler_params=_cp(("parallel", "arbitrary")),
    )(*args)


GW = SSD_HPG * SSD_HD


def _ssd_common(x, bm, cm, dt_raw, dt_raw_t, bias_r, bias_c, alog_r, alog_c):
    row = lax.broadcasted_iota(jnp.int32, (CHUNK, CHUNK), 0)
    col = lax.broadcasted_iota(jnp.int32, (CHUNK, CHUNK), 1)
    causal = row >= col
    tril = causal.astype(F32)
    triu = (row <= col).astype(F32)
    spread = _group_matrix(GW, SSD_HD, transpose=True)
    dt = _softplus(dt_raw + bias_r)
    dt_t = _softplus(dt_raw_t + bias_c)
    a_r = -jnp.exp(alog_r)
    a_c = -jnp.exp(alog_c)
    acs = _dotf(tril, dt * a_r, onehot="a", pieces=3)
    acs_t = _dotf(dt_t * a_c, triu, pieces=3)
    last = acs[CHUNK - 1:CHUNK, :]
    ds = jnp.exp(last - acs)
    cd = jnp.exp(last)
    c = dict(causal=causal, tril=tril, triu=triu, spread=spread, dt=dt, a_r=a_r, acs=acs, acs_t=acs_t, ds=ds, cd=cd)
    c["eb"] = _dotf(jnp.exp(acs), spread)
    c["dsb"] = _dotf(ds, spread)
    c["cdb"] = _dotf(cd, spread)
    c["dtb"] = _dotf(dt, spread)
    c["xdt"] = x * c["dtb"]
    c["cb"] = _dotb(cm, bm, NT)
    return c


def _ssd_lam(c, r):
    diff = c["acs"][:, r:r + 1] - c["acs_t"][r:r + 1, :]
    return jnp.exp(jnp.where(c["causal"], diff, -jnp.inf))


def _ssd_specs(nc, rev):
    def ci(t):
        return (nc - 1 - t) if rev else t
    xs = pl.BlockSpec((CHUNK, GW), lambda g, t: (ci(t), g))
    bs = pl.BlockSpec((CHUNK, SSD_N), lambda g, t: (ci(t), SSD_DI // SSD_N + g))
    cs = pl.BlockSpec((CHUNK, SSD_N), lambda g, t: (ci(t), SSD_DI // SSD_N + SSD_G + g))
    dts = pl.BlockSpec((1, CHUNK, 8), lambda g, t: (g, ci(t), 0))
    dtts = pl.BlockSpec((1, 8, CHUNK), lambda g, t: (g, 0, ci(t)))
    pr = pl.BlockSpec((1, 1, 8), lambda g, t: (g, 0, 0))
    pc = pl.BlockSpec((1, 8, 1), lambda g, t: (g, 0, 0))
    hs = pl.BlockSpec((1, 1, SSD_N, GW), lambda g, t: (ci(t), g, 0, 0))
    return xs, bs, cs, dts, dtts, pr, pc, hs


def _ssd_fwd(xbc, dtg, dtg_t, bias_r, bias_c, alog_r, alog_c, d_r, *, name):
    s = xbc.shape[0]
    nc = s // CHUNK
    xs, bs, cs, dts, dtts, pr, pc, hs = _ssd_specs(nc, False)

    def body(x_ref, b_ref, c_ref, dt_ref, dtt_ref, br_ref, bc_ref, ar_ref, ac_ref, d_ref, y_ref, hp_ref, h_sc):
        t = pl.program_id(1)

        @pl.when(t == 0)
        def _():
            h_sc[...] = jnp.zeros_like(h_sc)

        x, bm, cm = x_ref[...], b_ref[...], c_ref[...]
        c = _ssd_common(x, bm, cm, dt_ref[0], dtt_ref[0], br_ref[0], bc_ref[0], ar_ref[0], ac_ref[0])
        h = h_sc[...]
        hp_ref[0, 0] = h
        xdt = c["xdt"]
        pieces = []
        for r in range(SSD_HPG):
            m = c["cb"] * _ssd_lam(c, r)
            pieces.append(_dotb(m, xdt[:, r * SSD_HD:(r + 1) * SSD_HD]))
        y = jnp.concatenate(pieces, axis=1) + c["eb"] * _dotb(cm, h) + x * _dotf(d_ref[0], c["spread"])
        y_ref[...] = y
        h_sc[...] = h * c["cdb"] + _dotb(bm, xdt * c["dsb"], TN)

    return pl.pallas_call(
        body, name=name, grid=(SSD_G, nc),
        in_specs=[xs, bs, cs, dts, dtts, pr, pc, pr, pc, pr],
        out_specs=[xs, hs],
        out_shape=[jax.ShapeDtypeStruct((s, SSD_DI), F32), jax.ShapeDtypeStruct((nc, SSD_G, SSD_N, GW), F32)],
        scratch_shapes=[pltpu.VMEM((SSD_N, GW), F32)],
        compiler_params=_cp(("parallel", "arbitrary")),
    )(xbc, xbc, xbc, dtg, dtg_t, bias_r, bias_c, alog_r, alog_c, d_r)


def _ssd_bwd(xbc, dtg, dtg_t, bias_r, bias_c, alog_r, alog_c, d_r, hprev, dy, *, name):
    s = xbc.shape[0]
    nc = s // CHUNK
    xs, bs, cs, dts, dtts, pr, pc, hs = _ssd_specs(nc, True)
    gsum = functools.partial(_group_matrix, GW, SSD_HD)

    def body(x_ref, b_ref, c_ref, dt_ref, dtt_ref, br_ref, bc_ref, ar_ref, ac_ref, d_ref, hp_ref, dy_ref,
             dx_ref, db_ref, dc_ref, ddt_ref, dbias_ref, dalog_ref, dd_ref, dh_sc):
        t = pl.program_id(1)

        @pl.when(t == 0)
        def _():
            dh_sc[...] = jnp.zeros_like(dh_sc)
            dbias_ref[...] = jnp.zeros_like(dbias_ref)
            dalog_ref[...] = jnp.zeros_like(dalog_ref)
            dd_ref[...] = jnp.zeros_like(dd_ref)

        x, bm, cm = x_ref[...], b_ref[...], c_ref[...]
        c = _ssd_common(x, bm, cm, dt_ref[0], dtt_ref[0], br_ref[0], bc_ref[0], ar_ref[0], ac_ref[0])
        lanesum = gsum()
        h = hp_ref[0, 0]
        dh = dh_sc[...]
        dy = dy_ref[...]
        xdt, dsb = c["xdt"], c["dsb"]
        skip = _dotf(d_ref[0], c["spread"])
        dd_ref[0] += jnp.sum(_dotf(dy * x, lanesum), axis=0, keepdims=True)
        dacs = _dotf(dy * (c["eb"] * _dotb(cm, h)), lanesum)
        edy = c["eb"] * dy
        dcm = _dotb(edy, h, NT)
        dh_prev = _dotb(cm, edy, TN)
        bdh = _dotb(bm, dh)
        dxdt = dsb * bdh
        dbm = _dotb(dsb * xdt, dh, NT)
        t1 = _dotf(xdt * bdh, lanesum) * c["ds"]
        dacs = dacs - t1
        dlast = jnp.sum(t1, axis=0, keepdims=True) + jnp.sum(_dotf(dh * h, lanesum), axis=0, keepdims=True) * c["cd"]
        dcb = jnp.zeros((CHUNK, CHUNK), F32)
        pieces = []
        ones8 = jnp.ones((CHUNK, 8), F32)
        head = lax.broadcasted_iota(jnp.int32, (1, 8), 1)
        for r in range(SSD_HPG):
            sl = slice(r * SSD_HD, (r + 1) * SSD_HD)
            lam = _ssd_lam(c, r)
            m = c["cb"] * lam
            dm = _dotb(dy[:, sl], xdt[:, sl], NT)
            dcb = dcb + dm * lam
            gm = dm * m
            dacs = dacs + (jnp.sum(gm, axis=1, keepdims=True) - _dotf(gm, ones8, TN, pieces=3)) * (head == r).astype(F32)
            pieces.append(_dotb(m, dy[:, sl], TN))
        dxdt = dxdt + jnp.concatenate(pieces, axis=1)
        dcm = dcm + _dotb(dcb, bm)
        dbm = dbm + _dotb(dcb, cm, TN)
        dx_ref[...] = dy * skip + dxdt * c["dtb"]
        db_ref[...] = dbm
        dc_ref[...] = dcm
        rowid = lax.broadcasted_iota(jnp.int32, (CHUNK, 8), 0)
        dacs = dacs + jnp.where(rowid == CHUNK - 1, dlast, 0.0)
        dda = _dotf(c["triu"], dacs, onehot="a", pieces=3)
        ddt = _dotf(dxdt * x, lanesum) + dda * c["a_r"]
        ddt_raw = ddt * _sigmoid(dt_ref[0] + br_ref[0])
        ddt_ref[0] = ddt_raw
        dbias_ref[0] += jnp.sum(ddt_raw, axis=0, keepdims=True)
        dalog_ref[0] += jnp.sum(dda * c["dt"], axis=0, keepdims=True) * c["a_r"]
        dh_sc[...] = dh_prev + dh * c["cdb"]

    ci = lambda t: nc - 1 - t
    nspec = pl.BlockSpec((CHUNK, SSD_N), lambda g, t: (ci(t), g))
    return pl.pallas_call(
        body, name=name, grid=(SSD_G, nc),
        in_specs=[xs, bs, cs, dts, dtts, pr, pc, pr, pc, pr, hs, xs],
        out_specs=[xs, nspec, nspec, dts, pr, pr, pr],
        out_shape=[jax.ShapeDtypeStruct((s, SSD_DI), F32), jax.ShapeDtypeStruct((s, SSD_G * SSD_N), F32),
                   jax.ShapeDtypeStruct((s, SSD_G * SSD_N), F32), jax.ShapeDtypeStruct((SSD_G, s, 8), F32),
                   jax.ShapeDtypeStruct((SSD_G, 1, 8), F32), jax.ShapeDtypeStruct((SSD_G, 1, 8), F32),
                   jax.ShapeDtypeStruct((SSD_G, 1, 8), F32)],
        scratch_shapes=[pltpu.VMEM((SSD_N, GW), F32)],
        compiler_params=_cp(("parallel", "arbitrary")),
    )(xbc, xbc, xbc, dtg, dtg_t, bias_r, bias_c, alog_r, alog_c, d_r, hprev, dy)


FOX_PAIRS = FOX_H // 2
FOX_SCALE = FOX_HD ** -0.5
NEG_INF = -jnp.inf


def _fgate_fwd(f_t, b_c, *, name):
    hh, s = f_t.shape
    tb = _pick(s, 512)
    nb = s // tb

    def body(f_ref, b_ref, o_ref, carry):
        t = pl.program_id(0)

        @pl.when(t == 0)
        def _():
            carry[...] = jnp.zeros_like(carry)

        lf = -_softplus(-(f_ref[...] + b_ref[...]))
        row = lax.broadcasted_iota(jnp.int32, (tb, tb), 0)
        col = lax.broadcasted_iota(jnp.int32, (tb, tb), 1)
        cum = _dotf(lf, (row <= col).astype(F32), pieces=3) + carry[:, 0:1]
        o_ref[...] = cum
        carry[:, 0:1] = cum[:, tb - 1:tb]

    return pl.pallas_call(
        body, name=name, grid=(nb,),
        in_specs=[pl.BlockSpec((hh, tb), lambda t: (0, t)), pl.BlockSpec((hh, 1), lambda t: (0, 0))],
        out_specs=pl.BlockSpec((hh, tb), lambda t: (0, t)),
        out_shape=jax.ShapeDtypeStruct((hh, s), F32),
        scratch_shapes=[pltpu.VMEM((hh, LANES), F32)],
        compiler_params=_cp(("arbitrary",)),
    )(f_t, b_c)


def _fgate_bwd(dcum_q_t, dcum_k_t, f_t, b_c, *, name):
    hh, s = f_t.shape
    tb = _pick(s, 512)
    nb = s // tb

    def body(dq_ref, d_ref, f_ref, b_ref, df_ref, db_ref, carry):
        t = pl.program_id(0)

        @pl.when(t == 0)
        def _():
            carry[...] = jnp.zeros_like(carry)
            db_ref[...] = jnp.zeros_like(db_ref)

        d = d_ref[...] + dq_ref[...]
        row = lax.broadcasted_iota(jnp.int32, (tb, tb), 0)
        col = lax.broadcasted_iota(jnp.int32, (tb, tb), 1)
        rev = _dotf(d, (row >= col).astype(F32), pieces=3) + carry[:, 0:1]
        df = rev * _sigmoid(-(f_ref[...] + b_ref[...]))
        df_ref[...] = df
        db_ref[...] += jnp.sum(df, axis=1, keepdims=True)
        carry[:, 0:1] = rev[:, 0:1]

    blk = pl.BlockSpec((hh, tb), lambda t: (0, nb - 1 - t))
    return pl.pallas_call(
        body, name=name, grid=(nb,),
        in_specs=[blk, blk, blk, pl.BlockSpec((hh, 1), lambda t: (0, 0))],
        out_specs=[blk, pl.BlockSpec((hh, 1), lambda t: (0, 0))],
        out_shape=[jax.ShapeDtypeStruct((hh, s), F32), jax.ShapeDtypeStruct((hh, 1), F32)],
        scratch_shapes=[pltpu.VMEM((hh, LANES), F32)],
        compiler_params=_cp(("arbitrary",)),
    )(dcum_q_t, dcum_k_t, f_t, b_c)


def _fox_tile(s):
    return min(512, max(s // 2, 8))


def _tri_tables(nq, kv_major):
    if kv_major:
        pairs = [(i, j) for j in range(nq) for i in range(j, nq)]
    else:
        pairs = [(i, j) for i in range(nq) for j in range(i + 1)]
    return (jnp.asarray([p[0] for p in pairs], jnp.int32), jnp.asarray([p[1] for p in pairs], jnp.int32))


def _lane_tile(col, width):
    return col if width == LANES else jnp.tile(col, (1, width // LANES))


def _flash_fwd(qs, kn, qkvg, ck, *, name):
    s = qs.shape[0]
    tt = _fox_tile(s)
    nq = s // tt
    itab, jtab = _tri_tables(nq, kv_major=False)
    v0 = 2 * FOX_D // LANES

    def body(itab_ref, jtab_ref, q_ref, k_ref, v_ref, ck_ref, o_ref, lse_ref, m_sc, l_sc, acc_sc):
        t = pl.program_id(1)
        i, j = itab_ref[t], jtab_ref[t]

        @pl.when(j == 0)
        def _():
            m_sc[...] = jnp.full_like(m_sc, NEG_INF)
            l_sc[...] = jnp.zeros_like(l_sc)
            acc_sc[...] = jnp.zeros_like(acc_sc)

        low = lax.broadcasted_iota(jnp.int32, (tt, LANES), 1) < FOX_HD

        def step(diagonal):
            q2, k2 = q_ref[...], k_ref[...]
            v2 = v_ref[...].astype(BF16)
            alphas, outs = [], []
            for hh in range(2):
                qh = jnp.where(low if hh == 0 else jnp.logical_not(low), q2, jnp.zeros_like(q2))
                sc = lax.dot_general(qh, k2, NT, preferred_element_type=F32) - ck_ref[0][hh:hh + 1, :]
                if diagonal:
                    row = lax.broadcasted_iota(jnp.int32, sc.shape, 0)
                    col = lax.broadcasted_iota(jnp.int32, sc.shape, 1)
                    sc = jnp.where(row >= col, sc, NEG_INF)
                m_prev = m_sc[hh]
                m_new = jnp.maximum(m_prev, jnp.max(sc, axis=1, keepdims=True))
                alpha = jnp.exp(m_prev - m_new)
                p = jnp.exp(sc - _lane_tile(m_new, tt))
                l_sc[hh] = alpha * l_sc[hh] + jnp.sum(p, axis=1, keepdims=True)
                m_sc[hh] = m_new
                alphas.append(alpha)
                outs.append(lax.dot_general(p.astype(BF16), v2, NN, preferred_element_type=F32))
            acc_sc[...] = jnp.where(low, alphas[0], alphas[1]) * acc_sc[...] + jnp.where(low, outs[0], outs[1])

        @pl.when(j < i)
        def _():
            step(False)

        @pl.when(j == i)
        def _():
            step(True)
            o_ref[...] = acc_sc[...] / jnp.where(low, l_sc[0], l_sc[1])
            lse_ref[0] = jnp.concatenate([m_sc[hh][:, 0:1] + jnp.log(l_sc[hh][:, 0:1]) for hh in range(2)], axis=1)

    return pl.pallas_call(
        body, name=name,
        grid_spec=pltpu.PrefetchScalarGridSpec(
            num_scalar_prefetch=2, grid=(FOX_PAIRS, itab.shape[0]),
            in_specs=[pl.BlockSpec((tt, LANES), lambda p, t, it, jt: (it[t], p)),
                      pl.BlockSpec((tt, LANES), lambda p, t, it, jt: (jt[t], p)),
                      pl.BlockSpec((tt, LANES), lambda p, t, it, jt: (jt[t], v0 + p)),
                      pl.BlockSpec((1, 2, tt), lambda p, t, it, jt: (p, 0, jt[t]))],
            out_specs=[pl.BlockSpec((tt, LANES), lambda p, t, it, jt: (it[t], p)),
                       pl.BlockSpec((1, tt, 2), lambda p, t, it, jt: (p, it[t], 0))],
            scratch_shapes=[pltpu.VMEM((2, tt, LANES), F32), pltpu.VMEM((2, tt, LANES), F32), pltpu.VMEM((tt, LANES), F32)]),
        out_shape=[jax.ShapeDtypeStruct((s, FOX_D), F32), jax.ShapeDtypeStruct((FOX_PAIRS, s, 2), F32)],
        compiler_params=_cp(("parallel", "arbitrary")),
    )(itab, jtab, qs, kn, qkvg, ck)


def _flash_bwd(qs, kn, qkvg, do, lse, delta, ck, *, name):
    s = qs.shape[0]
    tt = _fox_tile(s)
    nq = s // tt
    itab, jtab = _tri_tables(nq, kv_major=True)
    nsteps = itab.shape[0]
    v0 = 2 * FOX_D // LANES

    def body(itab_ref, jtab_ref, q_ref, k_ref, v_ref, do_ref, lse_ref, dl_ref, ck_ref,
             dq_ref, dk_ref, dv_ref, dcq_ref, dck_ref, dq_sc, rs_sc, dk_sc, dv_sc, dc_sc):
        t = pl.program_id(1)
        i, j = itab_ref[t], jtab_ref[t]

        @pl.when(t == 0)
        def _():
            dq_sc[...] = jnp.zeros_like(dq_sc)
            rs_sc[...] = jnp.zeros_like(rs_sc)

        @pl.when(i == j)
        def _():
            dk_sc[...] = jnp.zeros_like(dk_sc)
            dv_sc[...] = jnp.zeros_like(dv_sc)
            dc_sc[...] = jnp.zeros_like(dc_sc)

        low = lax.broadcasted_iota(jnp.int32, (tt, LANES), 1) < FOX_HD
        rows = pl.ds(pl.multiple_of(i * tt, tt), tt)

        def step(diagonal):
            q2, k2 = q_ref[...], k_ref[...]
            v2 = v_ref[...].astype(BF16)
            do2 = do_ref[...].astype(BF16)
            dqs, dks, dvs = [], [], []
            for hh in range(2):
                sel = low if hh == 0 else jnp.logical_not(low)
                qh = jnp.where(sel, q2, jnp.zeros_like(q2))
                doh = jnp.where(sel, do2, jnp.zeros_like(do2))
                sc = lax.dot_general(qh, k2, NT, preferred_element_type=F32) - ck_ref[0][hh:hh + 1, :]
                if diagonal:
                    row = lax.broadcasted_iota(jnp.int32, sc.shape, 0)
                    col = lax.broadcasted_iota(jnp.int32, sc.shape, 1)
                    sc = jnp.where(row >= col, sc, NEG_INF)
                lse_b = jnp.broadcast_to(lse_ref[0][:, hh:hh + 1], (tt, LANES))
                dl_b = jnp.broadcast_to(dl_ref[0][:, hh:hh + 1], (tt, LANES))
                p = jnp.exp(sc - _lane_tile(lse_b, tt))
                dp = lax.dot_general(doh, v2, NT, preferred_element_type=F32)
                ds = p * (dp - _lane_tile(dl_b, tt))
                pb, dsb = p.astype(BF16), ds.astype(BF16)
                dvs.append(lax.dot_general(pb, do2, TN, preferred_element_type=F32))
                dks.append(lax.dot_general(dsb, q2, TN, preferred_element_type=F32))
                dqs.append(lax.dot_general(dsb, k2, NN, preferred_element_type=F32))
                rs_sc[hh, rows, :] += jnp.sum(ds, axis=1, keepdims=True)
                dc_sc[hh] += jnp.sum(ds, axis=0, keepdims=True)
            dv_sc[...] += jnp.where(low, dvs[0], dvs[1])
            dk_sc[...] += jnp.where(low, dks[0], dks[1])
            dq_sc[rows, :] += jnp.where(low, dqs[0], dqs[1])

        @pl.when(j < i)
        def _():
            step(False)

        @pl.when(j == i)
        def _():
            step(True)

        @pl.when(i == nq - 1)
        def _():
            dk_ref[...] = dk_sc[...]
            dv_ref[...] = dv_sc[...]
            dck_ref[0] = -jnp.concatenate([dc_sc[hh] for hh in range(2)], axis=0)

        @pl.when(t == nsteps - 1)
        def _():
            dq_ref[...] = dq_sc[...] * FOX_SCALE
            dcq_ref[0] = jnp.concatenate([rs_sc[hh] for hh in range(2)], axis=1)

    qside = pl.BlockSpec((tt, LANES), lambda p, t, it, jt: (it[t], p))
    kside = pl.BlockSpec((tt, LANES), lambda p, t, it, jt: (jt[t], p))
    stat = pl.BlockSpec((1, tt, 2), lambda p, t, it, jt: (p, it[t], 0))
    ckspec = pl.BlockSpec((1, 2, tt), lambda p, t, it, jt: (p, 0, jt[t]))
    return pl.pallas_call(
        body, name=name,
        grid_spec=pltpu.PrefetchScalarGridSpec(
            num_scalar_prefetch=2, grid=(FOX_PAIRS, nsteps),
            in_specs=[qside, kside, pl.BlockSpec((tt, LANES), lambda p, t, it, jt: (jt[t], v0 + p)), qside, stat, stat, ckspec],
            out_specs=[pl.BlockSpec((s, LANES), lambda p, t, it, jt: (0, p)), kside, kside,
                       pl.BlockSpec((1, s, 2), lambda p, t, it, jt: (p, 0, 0)), ckspec],
            scratch_shapes=[pltpu.VMEM((s, LANES), F32), pltpu.VMEM((2, s, 1), F32), pltpu.VMEM((tt, LANES), F32),
                            pltpu.VMEM((tt, LANES), F32), pltpu.VMEM((2, 1, tt), F32)]),
        out_shape=[jax.ShapeDtypeStruct((s, FOX_D), F32), jax.ShapeDtypeStruct((s, FOX_D), F32),
                   jax.ShapeDtypeStruct((s, FOX_D), F32), jax.ShapeDtypeStruct((FOX_PAIRS, s, 2), F32),
                   jax.ShapeDtypeStruct((FOX_PAIRS, 2, s), F32)],
        compiler_params=_cp(("parallel", "arbitrary")),
    )(itab, jtab, qs, kn, qkvg, do, lse, delta, ck)


def _ogate_fwd(o, qkvg, *, name):
    s = o.shape[0]
    tr = _pick(s, 512, 8)

    def body(o_ref, g_ref, out_ref):
        out_ref[...] = (o_ref[...] * _sigmoid(g_ref[...])).astype(BF16)

    tile = pl.BlockSpec((tr, FOX_D), lambda i: (i, 0))
    return pl.pallas_call(
        body, name=name, grid=(s // tr,), in_specs=[tile, pl.BlockSpec((tr, FOX_D), lambda i: (i, 3))],
        out_specs=tile, out_shape=jax.ShapeDtypeStruct((s, FOX_D), BF16), compiler_params=_cp(("parallel",)),
    )(o, qkvg)


def _ogate_bwd(dog, o, qkvg, *, name):
    s = o.shape[0]
    tr = _pick(s, 512, 8)

    def body(dog_ref, o_ref, g_ref, do_ref, dg_ref, dl_ref):
        sg = _sigmoid(g_ref[...])
        ov = o_ref[...]
        dog_v = dog_ref[...]
        do = dog_v * sg
        do_ref[...] = do
        dg_ref[...] = (dog_v * ov * sg * (1.0 - sg)).astype(BF16)
        dl_ref[...] = _dotf(do * ov, _group_matrix(FOX_D, FOX_HD))

    tile = pl.BlockSpec((tr, FOX_D), lambda i: (i, 0))
    return pl.pallas_call(
        body, name=name, grid=(s // tr,), in_specs=[tile, tile, pl.BlockSpec((tr, FOX_D), lambda i: (i, 3))],
        out_specs=[tile, tile, pl.BlockSpec((tr, FOX_H), lambda i: (i, 0))],
        out_shape=[jax.ShapeDtypeStruct((s, FOX_D), F32), jax.ShapeDtypeStruct((s, FOX_D), BF16),
                   jax.ShapeDtypeStruct((s, FOX_H), F32)],
        compiler_params=_cp(("parallel",)),
    )(dog, o, qkvg)


def _loss_head(h, g, target, *, name):
    s, d = h.shape
    tr = _pick(s, 512, 8)

    def body(h_ref, g_ref, t_ref, loss_ref, dh_ref, dg_ref):
        i = pl.program_id(0)
        x = h_ref[...]
        gv = g_ref[...]
        r = lax.rsqrt(jnp.mean(x * x, axis=-1, keepdims=True) + EPS)
        xh = x * r
        err = xh * gv - t_ref[...]
        part = 0.5 * jnp.sum(jnp.sum(err * err, axis=1, keepdims=True) * (1.0 / d), axis=0, keepdims=True)
        dy = err * (1.0 / d)
        dyg = dy * gv
        dh_ref[...] = r * (dyg - xh * jnp.mean(dyg * xh, axis=-1, keepdims=True))
        dgp = jnp.sum(dy * xh, axis=0, keepdims=True)

        @pl.when(i == 0)
        def _():
            loss_ref[...] = jnp.zeros_like(loss_ref) + part
            dg_ref[...] = dgp

        @pl.when(i > 0)
        def _():
            loss_ref[...] += part
            dg_ref[...] += dgp

    tile = pl.BlockSpec((tr, d), lambda i: (i, 0))
    vec = pl.BlockSpec((1, d), lambda i: (0, 0))
    return pl.pallas_call(
        body, name=name, grid=(s // tr,), in_specs=[tile, vec, tile],
        out_specs=[pl.BlockSpec((1, LANES), lambda i: (0, 0)), tile, vec],
        out_shape=[jax.ShapeDtypeStruct((1, LANES), F32), jax.ShapeDtypeStruct((s, d), F32),
                   jax.ShapeDtypeStruct((1, d), F32)],
        compiler_params=_cp(("arbitrary",)),
    )(h, g, target)


def _adamw(w, g, m, v, *, name):
    rows, cols = w.shape
    tr = _pick(rows, 256, 8)
    c1 = 1.0 - ADAM_B1 ** ADAM_STEP
    c2 = 1.0 - ADAM_B2 ** ADAM_STEP

    def body(w_ref, g_ref, m_ref, v_ref, d_ref, nm_ref, nv_ref):
        gv = g_ref[...]
        nm = ADAM_B1 * m_ref[...] + (1.0 - ADAM_B1) * gv
        nv = ADAM_B2 * v_ref[...] + (1.0 - ADAM_B2) * (gv * gv)
        d_ref[...] = -ADAM_LR * ((nm / c1) / (jnp.sqrt(nv / c2) + ADAM_EPS) + ADAM_WD * w_ref[...])
        nm_ref[...] = nm
        nv_ref[...] = nv

    tile = pl.BlockSpec((tr, cols), lambda i: (i, 0))
    shp = jax.ShapeDtypeStruct((rows, cols), F32)
    return pl.pallas_call(
        body, name=name, grid=(rows // tr,), in_specs=[tile] * 4, out_specs=[tile] * 3, out_shape=[shp] * 3,
        compiler_params=_cp(("parallel",)),
    )(w, g, m, v)


ANY = pl.BlockSpec(memory_space=pl.ANY)
N_DEV = 8


def _coords():
    return lax.axis_index("x"), lax.axis_index("y"), lax.axis_index("c")


def _other_chips(x, y):
    return [(1 - x, y), (x, 1 - y), (1 - x, 1 - y)]


def _allgather_small(buf, *, name, with_sum):
    rows = buf.shape[0]

    def body(*refs):
        if with_sum:
            x_ref, out_ref, sum_ref, send_sems, recv_sems = refs
        else:
            x_ref, out_ref, send_sems, recv_sems = refs
        x, y, c = _coords()
        me = 4 * x + 2 * y + c
        out_ref[me] = x_ref[...]
        copies = []
        for rel in range(1, N_DEV):
            px = (1 - x) if rel & 4 else x
            py = (1 - y) if rel & 2 else y
            pc = (1 - c) if rel & 1 else c
            cp = pltpu.make_async_remote_copy(
                src_ref=x_ref, dst_ref=out_ref.at[me], send_sem=send_sems.at[rel - 1], recv_sem=recv_sems.at[rel - 1],
                device_id=(px, py, pc), device_id_type=MESH)
            cp.start()
            copies.append(cp)
        for cp in copies:
            cp.wait()
        if with_sum:
            acc = out_ref[0]
            for k in range(1, N_DEV):
                acc = acc + out_ref[k]
            sum_ref[...] = acc

    slots = jax.ShapeDtypeStruct((N_DEV, rows, LANES), F32)
    vm = pl.BlockSpec(memory_space=pltpu.VMEM)
    out_shape = [slots, jax.ShapeDtypeStruct((rows, LANES), F32)] if with_sum else [slots]
    return pl.pallas_call(
        body, name=name, in_specs=[vm], out_specs=[vm] * len(out_shape), out_shape=out_shape,
        scratch_shapes=[pltpu.SemaphoreType.DMA((N_DEV - 1,)), pltpu.SemaphoreType.DMA((N_DEV - 1,))],
    )(buf)


def _allgather_chips(shards, *, name):
    n = len(shards)

    def body(*refs):
        ins, outs = refs[:n], refs[n:2 * n]
        send_sems, recv_sems, local_sems = refs[2 * n:]
        x, y, c = _coords()
        k = 2 * x + y
        chips = _other_chips(x, y)
        sibling = (x, y, 1 - c)
        local = []
        for t in range(n):
            cp = pltpu.make_async_copy(ins[t], outs[t].at[k], local_sems.at[t])
            cp.start()
            local.append(cp)
        sends = []
        for t in range(n):
            for j, (px, py) in enumerate(chips):
                cp = pltpu.make_async_remote_copy(
                    src_ref=ins[t].at[c], dst_ref=outs[t].at[k, c], send_sem=send_sems.at[6 * t + j],
                    recv_sem=recv_sems.at[6 * t + j], device_id=(px, py, c), device_id_type=MESH)
                cp.start()
                sends.append(cp)
        for t in range(n):
            for j, (px, py) in enumerate(chips):
                kj = 2 * px + py
                pltpu.make_async_remote_copy(
                    src_ref=ins[t].at[c], dst_ref=outs[t].at[kj, c], send_sem=send_sems.at[6 * t + j],
                    recv_sem=recv_sems.at[6 * t + j], device_id=(px, py, c), device_id_type=MESH).wait_recv()
                cp = pltpu.make_async_remote_copy(
                    src_ref=outs[t].at[kj, c], dst_ref=outs[t].at[kj, c], send_sem=send_sems.at[6 * t + 3 + j],
                    recv_sem=recv_sems.at[6 * t + 3 + j], device_id=sibling, device_id_type=MESH)
                cp.start()
                sends.append(cp)
        for t in range(n):
            for j, (px, py) in enumerate(chips):
                kj = 2 * px + py
                pltpu.make_async_remote_copy(
                    src_ref=outs[t].at[kj, 1 - c], dst_ref=outs[t].at[kj, 1 - c], send_sem=send_sems.at[6 * t + 3 + j],
                    recv_sem=recv_sems.at[6 * t + 3 + j], device_id=sibling, device_id_type=MESH).wait_recv()
        for cp in sends:
            cp.wait_send()
        for cp in local:
            cp.wait()

    return pl.pallas_call(
        body, name=name, in_specs=[ANY] * n, out_specs=[ANY] * n,
        out_shape=[jax.ShapeDtypeStruct((4,) + s.shape, s.dtype) for s in shards],
        scratch_shapes=[pltpu.SemaphoreType.DMA((6 * n,)), pltpu.SemaphoreType.DMA((6 * n,)), pltpu.SemaphoreType.DMA((n,))],
    )(*shards)


def _sibling_swap(arrs, *, name, other_half):
    n = len(arrs)

    def body(*refs):
        ins, outs = refs[:n], refs[n:2 * n]
        send_sems, recv_sems = refs[2 * n:]
        x, y, c = _coords()
        copies = []
        for t in range(n):
            cp = pltpu.make_async_remote_copy(
                src_ref=ins[t].at[1 - c] if other_half else ins[t], dst_ref=outs[t], send_sem=send_sems.at[t],
                recv_sem=recv_sems.at[t], device_id=(x, y, 1 - c), device_id_type=MESH)
            cp.start()
            copies.append(cp)
        for cp in copies:
            cp.wait()

    return pl.pallas_call(
        body, name=name, in_specs=[ANY] * n, out_specs=[ANY] * n,
        out_shape=[jax.ShapeDtypeStruct(a.shape[1:] if other_half else a.shape, a.dtype) for a in arrs],
        scratch_shapes=[pltpu.SemaphoreType.DMA((n,)), pltpu.SemaphoreType.DMA((n,))],
    )(*arrs)


def _chip_exchange(arrs, *, name):
    n = len(arrs)

    def body(*refs):
        ins, outs = refs[:n], refs[n:2 * n]
        send_sems, recv_sems = refs[2 * n:]
        x, y, c = _coords()
        copies = []
        for t in range(n):
            for j, (px, py) in enumerate(_other_chips(x, y)):
                cp = pltpu.make_async_remote_copy(
                    src_ref=ins[t].at[2 * px + py], dst_ref=outs[t].at[j], send_sem=send_sems.at[3 * t + j],
                    recv_sem=recv_sems.at[3 * t + j], device_id=(px, py, c), device_id_type=MESH)
                cp.start()
                copies.append(cp)
        for cp in copies:
            cp.wait()

    return pl.pallas_call(
        body, name=name, in_specs=[ANY] * n, out_specs=[ANY] * n,
        out_shape=[jax.ShapeDtypeStruct((3,) + a.shape[1:], a.dtype) for a in arrs],
        scratch_shapes=[pltpu.SemaphoreType.DMA((3 * n,)), pltpu.SemaphoreType.DMA((3 * n,))],
    )(*arrs)


def _add_selected(stack, others, sel, *, name, with_bf16=False):
    _, m, cols = stack.shape
    q = others.shape[0]
    tr = _pick(m, 256, 16)

    def body(sel_ref, s_ref, o_ref, out_ref, *low_ref):
        acc = s_ref[0].astype(F32)
        for i in range(q):
            acc = acc + o_ref[i].astype(F32)
        out_ref[...] = acc
        if with_bf16:
            low_ref[0][...] = acc.astype(BF16)

    tile = pl.BlockSpec((tr, cols), lambda i, sel_ref: (i, 0))
    out_shape = [jax.ShapeDtypeStruct((m, cols), F32)] + ([jax.ShapeDtypeStruct((m, cols), BF16)] if with_bf16 else [])
    return pl.pallas_call(
        body, name=name,
        grid_spec=pltpu.PrefetchScalarGridSpec(
            num_scalar_prefetch=1, grid=(m // tr,),
            in_specs=[pl.BlockSpec((1, tr, cols), lambda i, sel_ref: (sel_ref[0], i, 0)),
                      pl.BlockSpec((q, tr, cols), lambda i, sel_ref: (0, i, 0))],
            out_specs=[tile] * len(out_shape)),
        out_shape=out_shape,
        compiler_params=_cp(("parallel",)),
    )(sel, stack, others)


def _reduce_scatter(grads, c, k):
    n = len(grads)
    csel, ksel = jnp.reshape(c, (1,)).astype(jnp.int32), jnp.reshape(k, (1,)).astype(jnp.int32)
    from_sib = _sibling_swap(grads, name="rs_pair_swap", other_half=True)
    chip_sums, chip_sums_low = [], []
    for t in range(n):
        _, _, m, cols = grads[t].shape
        mine = grads[t].reshape(2, 4 * m, cols)
        s, low = _add_selected(mine, from_sib[t].reshape(1, 4 * m, cols), csel, name=f"rs_pair_add{t}", with_bf16=True)
        chip_sums.append(s.reshape(4, m, cols))
        chip_sums_low.append(low.reshape(4, m, cols))
    from_chips = _chip_exchange(chip_sums_low, name="rs_chip_exchange")
    finals = [_add_selected(chip_sums[t], from_chips[t], ksel, name=f"rs_chip_add{t}")[0] for t in range(n)]
    others = _sibling_swap(finals, name="rs_result_swap", other_half=False)
    return finals, others


BIG = ("ssd_w_in", "ssd_w_out", "fox_w_in", "fox_w_out", "ffn_w_up", "ffn_w_down")
COL_SHARDED = ("ssd_w_in", "fox_w_in", "ffn_w_up")
SMALL = (("mix_norm_g", (4, 1024)), ("ffn_norm_g", (4, 1024)), ("ssd_conv_w", (2, 4, 3072)), ("ssd_conv_b", (2, 3072)),
         ("ssd_dt_bias", (2, 32)), ("ssd_a_log", (2, 32)), ("ssd_d", (2, 32)), ("ssd_norm_g", (2, 2048)),
         ("fox_b_f", (2, 16)), ("fox_q_norm_g", (2, 64)), ("fox_k_norm_g", (2, 64)), ("ffn_conv_w", (4, 3, 2816)),
         ("ffn_conv_b", (4, 2816)), ("final_norm_g", (1024,)), ("loss", (1,)))
NAMES = ("mix_norm_g", "ffn_norm_g", "ssd_w_in", "ssd_conv_w", "ssd_conv_b", "ssd_dt_bias", "ssd_a_log", "ssd_d",
         "ssd_norm_g", "ssd_w_out", "fox_w_in", "fox_b_f", "fox_q_norm_g", "fox_k_norm_g", "fox_w_out", "ffn_w_up",
         "ffn_conv_w", "ffn_conv_b", "ffn_w_down", "final_norm_g")


def _pack(parts):
    flat = jnp.concatenate([jnp.reshape(p, (-1,)).astype(F32) for p in parts])
    rows = -(-flat.shape[0] // (8 * LANES)) * 8
    return jnp.pad(flat, (0, rows * LANES - flat.shape[0])).reshape(rows, LANES)


def _unpack(buf, shapes):
    flat = buf.reshape(-1)
    out, off = [], 0
    for shp in shapes:
        size = 1
        for d in shp:
            size *= d
        out.append(flat[off:off + size].reshape(shp))
        off += size
    return out


def _pad_lanes(a):
    return jnp.pad(a, ((0, 0), (0, LANES - a.shape[1])))


def _pad8(w):
    return jnp.pad(w, ((0, 8 - w.shape[0]), (0, 0)))


def _ssd_forward(h, p, name):
    s = h.shape[0]
    hn = _rms_fwd(h, p["mix_g"], gw=D_MODEL, ncol=1, name=f"{name}_norm")
    zx = _matmul(hn, p["w_zx"], mode="nn", name=f"{name}_proj")
    dtp = _matmul(hn, p["w_dt"], mode="nn", name=f"{name}_proj_dt")
    xbc = _conv_fwd(zx, p["conv_w8"], p["conv_b"], kw=SSD_K, width=SSD_CONV_DIM, u_col0=SSD_DI, name=f"{name}_conv")
    dt3 = dtp[:, :SSD_H].reshape(s, SSD_G, SSD_HPG)
    dtg, dtg_t = jnp.transpose(dt3, (1, 0, 2)), jnp.transpose(dt3, (1, 2, 0))
    sp = (p["bias_r"], p["bias_c"], p["alog_r"], p["alog_c"], p["d_r"])
    y, hprev = _ssd_fwd(xbc, dtg, dtg_t, *sp, name=f"{name}_scan")
    y2 = _rms_fwd(y, p["norm_g"], gw=SSD_DI // SSD_G, ncol=SSD_G, z=zx, name=f"{name}_gnorm")
    out = _matmul(y2, p["w_out"], mode="nn", add=h, name=f"{name}_out")
    return out, dict(h=h, hn=hn, zx=zx, xbc=xbc, dtg=dtg, dtg_t=dtg_t, y=y, hprev=hprev, y2=y2)


def _ssd_backward(dh1, p, a, name):
    s = dh1.shape[0]
    g = {}
    dy2 = _matmul(dh1, p["w_out"], mode="nt", name=f"{name}_out_dx")
    g["w_out"] = _matmul(a["y2"], dh1, mode="tn", name=f"{name}_out_dw")
    dy, dz, g["norm_g"] = _rms_bwd(a["y"], p["norm_g"], dy2, gw=SSD_DI // SSD_G, ncol=SSD_G, z=a["zx"], name=f"{name}_gnorm_b")
    sp = (p["bias_r"], p["bias_c"], p["alog_r"], p["alog_c"], p["d_r"])
    dx, dbm, dcm, ddt, g["dt_bias"], g["a_log"], g["d"] = _ssd_bwd(
        a["xbc"], a["dtg"], a["dtg_t"], *sp, a["hprev"], dy, name=f"{name}_scan_b")
    dact = jnp.concatenate([dx, dbm, dcm], axis=1)
    dxbc, dwb = _conv_bwd(a["zx"], p["conv_w8"], p["conv_b"], dact, kw=SSD_K, width=SSD_CONV_DIM, u_col0=SSD_DI,
                          name=f"{name}_conv_b")
    g["conv_w"], g["conv_b"] = dwb[:SSD_K], dwb[7]
    dzx = jnp.concatenate([dz.astype(BF16), dxbc], axis=1)
    ddtp = _pad_lanes(jnp.transpose(ddt, (1, 0, 2)).reshape(s, SSD_H))
    dhn = _matmul(dzx, p["w_zx"], mode="nt", name=f"{name}_proj_dx")
    dhn = _matmul(ddtp, p["w_dt"], mode="nt", add=dhn, name=f"{name}_proj_dt_dx")
    dw_zx = _matmul(a["hn"], dzx, mode="tn", name=f"{name}_proj_dw")
    dw_dt = _matmul(a["hn"], ddtp, mode="tn", name=f"{name}_proj_dt_dw")
    g["w_in"] = jnp.concatenate([dw_zx, dw_dt[:, :SSD_H]], axis=1)
    dh, g["mix_g"] = _rms_bwd(a["h"], p["mix_g"], dhn, gw=D_MODEL, ncol=1, add=dh1, name=f"{name}_norm_b")
    return dh, g


def _fox_forward(h, p, name):
    s = h.shape[0]
    hn = _rms_fwd(h, p["mix_g"], gw=D_MODEL, ncol=1, name=f"{name}_norm")
    qkvg = _matmul(hn, p["w_qkvg"], mode="nn", name=f"{name}_proj")
    fp = _matmul(hn, p["w_f"], mode="nn", name=f"{name}_proj_f")
    qs = _rms_fwd(qkvg, p["gq"] * FOX_SCALE, gw=FOX_D, ncol=1, x_col0=0, sub=FOX_HD, name=f"{name}_qnorm")
    kn = _rms_fwd(qkvg, p["gk"], gw=FOX_D, ncol=1, x_col0=1, sub=FOX_HD, name=f"{name}_knorm")
    f_t = jnp.transpose(fp[:, :FOX_H])
    cum_t = _fgate_fwd(f_t, p["b_f"], name=f"{name}_fgate")
    ck = cum_t.reshape(FOX_PAIRS, 2, s)
    o, lse = _flash_fwd(qs, kn, qkvg, ck, name=f"{name}_attn")
    og = _ogate_fwd(o, qkvg, name=f"{name}_ogate")
    out = _matmul(og, p["w_out"], mode="nn", add=h, name=f"{name}_out")
    return out, dict(h=h, hn=hn, qkvg=qkvg, qs=qs, kn=kn, f_t=f_t, ck=ck, o=o, lse=lse, og=og)


def _fox_backward(dh1, p, a, name):
    s = dh1.shape[0]
    g = {}
    dog = _matmul(dh1, p["w_out"], mode="nt", name=f"{name}_out_dx")
    g["w_out"] = _matmul(a["og"], dh1, mode="tn", name=f"{name}_out_dw")
    do, dgate, delta = _ogate_bwd(dog, a["o"], a["qkvg"], name=f"{name}_ogate_b")
    dl = jnp.transpose(delta.reshape(s, FOX_PAIRS, 2), (1, 0, 2))
    dq, dk, dv, dcq, dck = _flash_bwd(a["qs"], a["kn"], a["qkvg"], do, a["lse"], dl, a["ck"], name=f"{name}_attn_b")
    dq_raw, dgq = _rms_bwd(a["qkvg"], p["gq"], dq, gw=FOX_D, ncol=1, x_col0=0, sub=FOX_HD, dx_dtype=BF16, name=f"{name}_qnorm_b")
    dk_raw, dgk = _rms_bwd(a["qkvg"], p["gk"], dk, gw=FOX_D, ncol=1, x_col0=1, sub=FOX_HD, dx_dtype=BF16, name=f"{name}_knorm_b")
    g["gq"] = dgq.reshape(FOX_H, FOX_HD).sum(axis=0)
    g["gk"] = dgk.reshape(FOX_H, FOX_HD).sum(axis=0)
    dcq_t = jnp.transpose(dcq, (0, 2, 1)).reshape(FOX_H, s)
    df_t, dbf = _fgate_bwd(dcq_t, dck.reshape(FOX_H, s), a["f_t"], p["b_f"], name=f"{name}_fgate_b")
    g["b_f"] = dbf[:, 0]
    dproj = jnp.concatenate([dq_raw, dk_raw, dv.astype(BF16), dgate], axis=1)
    dfp = _pad_lanes(jnp.transpose(df_t))
    dhn = _matmul(dproj, p["w_qkvg"], mode="nt", name=f"{name}_proj_dx")
    dhn = _matmul(dfp, p["w_f"], mode="nt", add=dhn, name=f"{name}_proj_f_dx")
    dw_qkvg = _matmul(a["hn"], dproj, mode="tn", name=f"{name}_proj_dw")
    dw_f = _matmul(a["hn"], dfp, mode="tn", name=f"{name}_proj_f_dw")
    g["w_in"] = jnp.concatenate([dw_qkvg, dw_f[:, :FOX_H]], axis=1)
    dh, g["mix_g"] = _rms_bwd(a["h"], p["mix_g"], dhn, gw=D_MODEL, ncol=1, add=dh1, name=f"{name}_norm_b")
    return dh, g


def _ffn_forward(h, p, name):
    hn = _rms_fwd(h, p["ffn_g"], gw=D_MODEL, ncol=1, name=f"{name}_norm")
    u = _matmul(hn, p["w_up"], mode="nn", name=f"{name}_up")
    act = _conv_fwd(u, p["conv_w8"], p["conv_b"], kw=FFN_K, width=D_FF, u_col0=0, mul_col0=D_FF, out_dtype=BF16,
                    name=f"{name}_glu")
    out = _matmul(act, p["w_down"], mode="nn", add=h, name=f"{name}_down")
    return out, dict(h=h, hn=hn, u=u, act=act)


def _ffn_backward(dh2, p, a, name):
    g = {}
    dact = _matmul(dh2, p["w_down"], mode="nt", name=f"{name}_down_dx")
    g["w_down"] = _matmul(a["act"], dh2, mode="tn", name=f"{name}_down_dw")
    du1, du2, dwb = _conv_bwd(a["u"], p["conv_w8"], p["conv_b"], dact, kw=FFN_K, width=D_FF, u_col0=0, mul_col0=D_FF,
                              name=f"{name}_glu_b")
    g["conv_w"], g["conv_b"] = dwb[:FFN_K], dwb[7]
    du = jnp.concatenate([du1, du2], axis=1)
    dhn = _matmul(du, p["w_up"], mode="nt", name=f"{name}_up_dx")
    g["w_up"] = _matmul(a["hn"], du, mode="tn", name=f"{name}_up_dw")
    dh, g["ffn_g"] = _rms_bwd(a["h"], p["ffn_g"], dhn, gw=D_MODEL, ncol=1, add=dh2, name=f"{name}_norm_b")
    return dh, g


def _to_slabs(full, col_sharded):
    nl, r, ct = full.shape
    if col_sharded:
        t = jnp.transpose(full.reshape(2, nl // 2, r, 4, ct // 4), (0, 3, 1, 2, 4))
        return t.reshape(2, 4, (nl // 2) * r, ct // 4)
    t = jnp.transpose(full.reshape(2, nl // 2, 4, r // 4, ct), (0, 2, 1, 3, 4))
    return t.reshape(2, 4, (nl // 2) * (r // 4), ct)


def kernel(x, mix_norm_g, ffn_norm_g, ssd_w_in, ssd_conv_w, ssd_conv_b, ssd_dt_bias, ssd_a_log, ssd_d, ssd_norm_g, ssd_w_out, fox_w_in, fox_b_f, fox_q_norm_g, fox_k_norm_g, fox_w_out, ffn_w_up, ffn_conv_w, ffn_conv_b, ffn_w_down, final_norm_g, loss_target, m_mix_norm_g, m_ffn_norm_g, m_ssd_w_in, m_ssd_conv_w, m_ssd_conv_b, m_ssd_dt_bias, m_ssd_a_log, m_ssd_d, m_ssd_norm_g, m_ssd_w_out, m_fox_w_in, m_fox_b_f, m_fox_q_norm_g, m_fox_k_norm_g, m_fox_w_out, m_ffn_w_up, m_ffn_conv_w, m_ffn_conv_b, m_ffn_w_down, m_final_norm_g, v_mix_norm_g, v_ffn_norm_g, v_ssd_w_in, v_ssd_conv_w, v_ssd_conv_b, v_ssd_dt_bias, v_ssd_a_log, v_ssd_d, v_ssd_norm_g, v_ssd_w_out, v_fox_w_in, v_fox_b_f, v_fox_q_norm_g, v_fox_k_norm_g, v_fox_w_out, v_ffn_w_up, v_ffn_conv_w, v_ffn_conv_b, v_ffn_w_down, v_final_norm_g):
    w = dict(mix_norm_g=mix_norm_g, ffn_norm_g=ffn_norm_g, ssd_w_in=ssd_w_in, ssd_conv_w=ssd_conv_w, ssd_conv_b=ssd_conv_b,
             ssd_dt_bias=ssd_dt_bias, ssd_a_log=ssd_a_log, ssd_d=ssd_d, ssd_norm_g=ssd_norm_g, ssd_w_out=ssd_w_out,
             fox_w_in=fox_w_in, fox_b_f=fox_b_f, fox_q_norm_g=fox_q_norm_g, fox_k_norm_g=fox_k_norm_g, fox_w_out=fox_w_out,
             ffn_w_up=ffn_w_up, ffn_conv_w=ffn_conv_w, ffn_conv_b=ffn_conv_b, ffn_w_down=ffn_w_down, final_norm_g=final_norm_g)
    m_in = dict(zip(NAMES, (m_mix_norm_g, m_ffn_norm_g, m_ssd_w_in, m_ssd_conv_w, m_ssd_conv_b, m_ssd_dt_bias, m_ssd_a_log,
                            m_ssd_d, m_ssd_norm_g, m_ssd_w_out, m_fox_w_in, m_fox_b_f, m_fox_q_norm_g, m_fox_k_norm_g,
                            m_fox_w_out, m_ffn_w_up, m_ffn_conv_w, m_ffn_conv_b, m_ffn_w_down, m_final_norm_g)))
    v_in = dict(zip(NAMES, (v_mix_norm_g, v_ffn_norm_g, v_ssd_w_in, v_ssd_conv_w, v_ssd_conv_b, v_ssd_dt_bias, v_ssd_a_log,
                            v_ssd_d, v_ssd_norm_g, v_ssd_w_out, v_fox_w_in, v_fox_b_f, v_fox_q_norm_g, v_fox_k_norm_g,
                            v_fox_w_out, v_ffn_w_up, v_ffn_conv_w, v_ffn_conv_b, v_ffn_w_down, v_final_norm_g)))
    cx, cy, cc = _coords()
    chip = 2 * cx + cy
    h = x[0]
    target = loss_target[0]

    conv_shapes = [ssd_conv_w.shape, ffn_conv_w.shape]
    slots = _allgather_small(_pack([ssd_conv_w, ffn_conv_w]), name="gather_conv_w", with_sum=False)[0]
    per_chip = [_unpack(slots[2 * q], conv_shapes) for q in range(4)]
    ssd_conv_full = jnp.concatenate([pc[0] for pc in per_chip], axis=2)
    ffn_conv_full = jnp.concatenate([pc[1] for pc in per_chip], axis=2)
    shards = [w[n].astype(BF16) for n in BIG]
    gathered = _allgather_chips([s.reshape((2, s.shape[0] // 2) + s.shape[1:]) for s in shards], name="gather_weights")
    full = {}
    for n, gth in zip(BIG, gathered):
        gth = gth.reshape((4, gth.shape[1] * gth.shape[2]) + gth.shape[3:])
        if n in COL_SHARDED:
            full[n] = jnp.transpose(gth, (1, 2, 0, 3)).reshape(gth.shape[1], gth.shape[2], 4 * gth.shape[3])
        else:
            full[n] = jnp.transpose(gth, (1, 0, 2, 3)).reshape(gth.shape[1], 4 * gth.shape[2], gth.shape[3])

    def ssd_params(j, i):
        w_in = full["ssd_w_in"][j]
        g3 = lambda v: v.reshape(SSD_G, 1, SSD_HPG)
        g3c = lambda v: v.reshape(SSD_G, SSD_HPG, 1)
        return dict(mix_g=mix_norm_g[i][None], w_zx=w_in[:, :SSD_ZX], w_dt=_pad_lanes(w_in[:, SSD_ZX:]),
                    conv_w8=_pad8(ssd_conv_full[j]), conv_b=ssd_conv_b[j][None], bias_r=g3(ssd_dt_bias[j]),
                    bias_c=g3c(ssd_dt_bias[j]), alog_r=g3(ssd_a_log[j]), alog_c=g3c(ssd_a_log[j]), d_r=g3(ssd_d[j]),
                    norm_g=ssd_norm_g[j][None], w_out=full["ssd_w_out"][j])

    def fox_params(j, i):
        w_in = full["fox_w_in"][j]
        return dict(mix_g=mix_norm_g[i][None], w_qkvg=w_in[:, :4 * FOX_D], w_f=_pad_lanes(w_in[:, 4 * FOX_D:]),
                    gq=jnp.tile(fox_q_norm_g[j], FOX_H)[None], gk=jnp.tile(fox_k_norm_g[j], FOX_H)[None],
                    b_f=fox_b_f[j][:, None], w_out=full["fox_w_out"][j])

    def ffn_params(i):
        return dict(ffn_g=ffn_norm_g[i][None], w_up=full["ffn_w_up"][i], conv_w8=_pad8(ffn_conv_full[i]),
                    conv_b=ffn_conv_b[i][None], w_down=full["ffn_w_down"][i])

    mix_p, ffn_p, mix_a, ffn_a = [], [], [], []
    for i in range(DEPTH):
        j = i // 2
        if i % 2 == 0:
            mix_p.append(ssd_params(j, i))
            h, act = _ssd_forward(h, mix_p[i], f"ssd{j}")
        else:
            mix_p.append(fox_params(j, i))
            h, act = _fox_forward(h, mix_p[i], f"fox{j}")
        mix_a.append(act)
        ffn_p.append(ffn_params(i))
        h, act = _ffn_forward(h, ffn_p[i], f"ffn{i}")
        ffn_a.append(act)
    loss_part, dh, d_final_g = _loss_head(h, final_norm_g[None], target, name="loss_head")

    mix_g, ffn_g = [None] * DEPTH, [None] * DEPTH
    for i in reversed(range(DEPTH)):
        j = i // 2
        dh, ffn_g[i] = _ffn_backward(dh, ffn_p[i], ffn_a[i], f"ffn{i}")
        if i % 2 == 0:
            dh, mix_g[i] = _ssd_backward(dh, mix_p[i], mix_a[i], f"ssd{j}")
        else:
            dh, mix_g[i] = _fox_backward(dh, mix_p[i], mix_a[i], f"fox{j}")
    grad_x = dh[None]
    ssd_g, fox_g = [mix_g[0], mix_g[2]], [mix_g[1], mix_g[3]]

    big_full = dict(
        ssd_w_in=jnp.stack([g["w_in"] for g in ssd_g]), ssd_w_out=jnp.stack([g["w_out"] for g in ssd_g]),
        fox_w_in=jnp.stack([g["w_in"] for g in fox_g]), fox_w_out=jnp.stack([g["w_out"] for g in fox_g]),
        ffn_w_up=jnp.stack([g["w_up"] for g in ffn_g]), ffn_w_down=jnp.stack([g["w_down"] for g in ffn_g]))
    mine, theirs = _reduce_scatter([_to_slabs(big_full[n], n in COL_SHARDED) for n in BIG], cc, chip)
    grads = {}
    for n, a, b in zip(BIG, mine, theirs):
        shp = w[n].shape
        a = a.reshape((shp[0] // 2,) + shp[1:])
        b = b.reshape((shp[0] // 2,) + shp[1:])
        grads[n] = jnp.where(cc == 0, jnp.concatenate([a, b], axis=0), jnp.concatenate([b, a], axis=0))
    small = dict(
        mix_norm_g=jnp.concatenate([g["mix_g"] for g in mix_g], axis=0),
        ffn_norm_g=jnp.concatenate([g["ffn_g"] for g in ffn_g], axis=0),
        ssd_conv_w=jnp.stack([g["conv_w"] for g in ssd_g]), ssd_conv_b=jnp.stack([g["conv_b"] for g in ssd_g]),
        ssd_dt_bias=jnp.stack([g["dt_bias"].reshape(SSD_H) for g in ssd_g]),
        ssd_a_log=jnp.stack([g["a_log"].reshape(SSD_H) for g in ssd_g]),
        ssd_d=jnp.stack([g["d"].reshape(SSD_H) for g in ssd_g]),
        ssd_norm_g=jnp.concatenate([g["norm_g"] for g in ssd_g], axis=0),
        fox_b_f=jnp.stack([g["b_f"] for g in fox_g]), fox_q_norm_g=jnp.stack([g["gq"] for g in fox_g]),
        fox_k_norm_g=jnp.stack([g["gk"] for g in fox_g]),
        ffn_conv_w=jnp.stack([g["conv_w"] for g in ffn_g]), ffn_conv_b=jnp.stack([g["conv_b"] for g in ffn_g]),
        final_norm_g=d_final_g[0], loss=loss_part[0, :1])
    _, total = _allgather_small(_pack([small[n] for n, _ in SMALL]), name="reduce_small", with_sum=True)
    for (n, shp), val in zip(SMALL, _unpack(total, [shp for _, shp in SMALL])):
        grads[n] = val
    loss = grads.pop("loss")[0]
    grads["ssd_conv_w"] = lax.dynamic_slice_in_dim(grads["ssd_conv_w"], chip * ssd_conv_w.shape[2], ssd_conv_w.shape[2], axis=2)
    grads["ffn_conv_w"] = lax.dynamic_slice_in_dim(grads["ffn_conv_w"], chip * ffn_conv_w.shape[2], ffn_conv_w.shape[2], axis=2)

    deltas, new_m, new_v = {}, {}, {}
    for n in NAMES:
        shp = w[n].shape
        two_d = (1, shp[0]) if len(shp) == 1 else (-1, shp[-1])
        r2 = lambda a: a.reshape(two_d)
        d, nm, nv = _adamw(r2(w[n]), r2(grads[n]), r2(m_in[n]), r2(v_in[n]), name=f"adamw_{n}")
        deltas[n], new_m[n], new_v[n] = d.reshape(shp), nm.reshape(shp), nv.reshape(shp)
    return (loss, grad_x, *[grads[n] for n in NAMES], *[deltas[n] for n in NAMES], *[new_m[n] for n in NAMES],
            *[new_v[n] for n in NAMES])
```

```python
import functools

import jax
import jax.numpy as jnp
from jax import lax
from jax.experimental import pallas as pl
from jax.experimental.pallas import tpu as pltpu

F32 = jnp.float32
BF16 = jnp.bfloat16
HI = lax.Precision.HIGHEST
MESH = pl.DeviceIdType.MESH

D_MODEL = 1024
DEPTH = 4
EPS = 1e-6
SSD_DI = 2048
SSD_HD = 64
SSD_G = 4
SSD_HPG = 8
SSD_N = 128
SSD_K = 4
CHUNK = 128
SSD_CONV_DIM = 3072
SSD_ZX = SSD_DI + SSD_CONV_DIM
SSD_H = 32
FOX_HD = 64
FOX_H = 16
FOX_D = 1024
D_FF = 2816
FFN_K = 3
LANES = 128
VMEM_LIMIT = 56 * 1024 * 1024

ADAM_LR = 0.001
ADAM_B1 = 0.9
ADAM_B2 = 0.999
ADAM_EPS = 1e-08
ADAM_WD = 0.01
ADAM_STEP = 10

NN = (((1,), (0,)), ((), ()))
NT = (((1,), (1,)), ((), ()))
TN = (((0,), (0,)), ((), ()))


def _pick(n, cap, mult=LANES):
    best = None
    for t in range(mult, min(n, cap) + 1, mult):
        if n % t == 0:
            best = t
    return best if best is not None else n


def _cp(sem):
    return pltpu.CompilerParams(dimension_semantics=sem, vmem_limit_bytes=VMEM_LIMIT)


def _sigmoid(x):
    return jax.nn.sigmoid(x)


def _silu(x):
    return x * _sigmoid(x)


def _dsilu(x):
    s = _sigmoid(x)
    return s * (1.0 + x * (1.0 - s))


def _softplus(x):
    e = jnp.exp(-jnp.abs(x))
    u = 1.0 + e
    l1p = jnp.where(u == 1.0, e, jnp.log(u) * (e / (u - 1.0)))
    return jnp.maximum(x, 0.0) + l1p


def _dotf(a, b, dn=NN, *, onehot="b", pieces=2):
    x, e = (a, b) if onehot == "b" else (b, a)
    e = e.astype(BF16)
    acc = None
    for n in range(pieces):
        hi = x.astype(BF16)
        part = lax.dot_general(hi, e, dn, preferred_element_type=F32) if onehot == "b" else \
            lax.dot_general(e, hi, dn, preferred_element_type=F32)
        acc = part if acc is None else acc + part
        if n + 1 < pieces:
            x = x - hi.astype(F32)
    return acc


def _dotb(a, b, dn=NN):
    return lax.dot_general(a.astype(BF16), b.astype(BF16), dn, preferred_element_type=F32)


def _group_matrix(width, sub, transpose=False):
    ng = width // sub
    shape = (ng, width) if transpose else (width, ng)
    lane = lax.broadcasted_iota(jnp.int32, shape, 1 if transpose else 0)
    grp = lax.broadcasted_iota(jnp.int32, shape, 0 if transpose else 1)
    return (lane // sub == grp).astype(F32)


def _gmean(v, sub):
    width = v.shape[-1]
    if sub == width:
        return jnp.mean(v, axis=-1, keepdims=True)
    s = _dotf(v, _group_matrix(width, sub))
    return _dotf(s, _group_matrix(width, sub, transpose=True)) * (1.0 / sub)


def _matmul(a, b, *, mode, name, out_dtype=F32, add=None):
    if mode == "nn":
        (m, k), (k2, n) = a.shape, b.shape
    elif mode == "nt":
        (m, k), (n, k2) = a.shape, b.shape
    else:
        (k, m), (k2, n) = a.shape, b.shape
    assert k == k2, (a.shape, b.shape, mode)
    tm, tn, tk = _pick(m, 1024), _pick(n, 1536), _pick(k, 1536)
    nk = k // tk
    dn = {"nn": NN, "nt": NT, "tn": TN}[mode]
    has_add = add is not None

    def body(*refs):
        if has_add:
            a_ref, b_ref, add_ref, o_ref, acc_ref = refs
        else:
            a_ref, b_ref, o_ref, acc_ref = refs
            add_ref = None
        kk = pl.program_id(2)
        part = _dotb(a_ref[...], b_ref[...], dn)

        def finish(r):
            if has_add:
                r = r + add_ref[...]
            o_ref[...] = r.astype(out_dtype)

        if nk == 1:
            finish(part)
        else:
            @pl.when(kk == 0)
            def _():
                acc_ref[...] = part

            @pl.when(kk > 0)
            def _():
                acc_ref[...] += part

            @pl.when(kk == nk - 1)
            def _():
                finish(acc_ref[...])

    if mode == "nn":
        a_spec = pl.BlockSpec((tm, tk), lambda i, j, q: (i, q))
        b_spec = pl.BlockSpec((tk, tn), lambda i, j, q: (q, j))
    elif mode == "nt":
        a_spec = pl.BlockSpec((tm, tk), lambda i, j, q: (i, q))
        b_spec = pl.BlockSpec((tn, tk), lambda i, j, q: (j, q))
    else:
        a_spec = pl.BlockSpec((tk, tm), lambda i, j, q: (q, i))
        b_spec = pl.BlockSpec((tk, tn), lambda i, j, q: (q, j))
    o_spec = pl.BlockSpec((tm, tn), lambda i, j, q: (i, j))
    in_specs = [a_spec, b_spec] + ([o_spec] if has_add else [])
    args = (a, b) + ((add,) if has_add else ())
    return pl.pallas_call(
        body, name=name, grid=(m // tm, n // tn, nk), in_specs=in_specs, out_specs=o_spec,
        out_shape=jax.ShapeDtypeStruct((m, n), out_dtype),
        scratch_shapes=[pltpu.VMEM((tm, tn) if nk > 1 else (8, LANES), F32)],
        compiler_params=_cp(("parallel", "parallel", "arbitrary")),
    )(*args)


def _rms_fwd(x, g, *, gw, ncol, name, x_col0=0, sub=None, z=None, z_col0=0, out_dtype=BF16):
    rows = x.shape[0]
    tr = _pick(rows, 512, 8)
    sub = gw if sub is None else sub
    gated = z is not None

    def body(*refs):
        if gated:
            x_ref, z_ref, g_ref, o_ref = refs
            xv = x_ref[...] * _silu(z_ref[...])
        else:
            x_ref, g_ref, o_ref = refs
            xv = x_ref[...]
        r = lax.rsqrt(_gmean(xv * xv, sub) + EPS)
        o_ref[...] = (xv * r * g_ref[...]).astype(out_dtype)

    specs = [pl.BlockSpec((tr, gw), lambda j, i: (i, x_col0 + j))]
    args = [x]
    if gated:
        specs.append(pl.BlockSpec((tr, gw), lambda j, i: (i, z_col0 + j)))
        args.append(z)
    specs.append(pl.BlockSpec((1, gw), lambda j, i: (0, j)))
    args.append(g)
    return pl.pallas_call(
        body, name=name, grid=(ncol, rows // tr), in_specs=specs,
        out_specs=pl.BlockSpec((tr, gw), lambda j, i: (i, j)),
        out_shape=jax.ShapeDtypeStruct((rows, gw * ncol), out_dtype),
        compiler_params=_cp(("parallel", "parallel")),
    )(*args)


def _rms_bwd(x, g, dy, *, gw, ncol, name, x_col0=0, sub=None, z=None, z_col0=0, add=None, dx_dtype=F32):
    rows = x.shape[0]
    tr = _pick(rows, 512, 8)
    sub = gw if sub is None else sub
    gated = z is not None
    has_add = add is not None

    def body(*refs):
        refs = list(refs)
        x_ref = refs.pop(0)
        z_ref = refs.pop(0) if gated else None
        g_ref = refs.pop(0)
        dy_ref = refs.pop(0)
        add_ref = refs.pop(0) if has_add else None
        dx_ref = refs.pop(0)
        dz_ref = refs.pop(0) if gated else None
        dg_ref = refs.pop(0)
        i = pl.program_id(1)
        xv = x_ref[...]
        if gated:
            zz = z_ref[...]
            yz = xv * _silu(zz)
        else:
            yz = xv
        r = lax.rsqrt(_gmean(yz * yz, sub) + EPS)
        xh = yz * r
        dy = dy_ref[...].astype(F32)
        dyg = dy * g_ref[...]
        d_yz = r * (dyg - xh * _gmean(dyg * xh, sub))
        if gated:
            dx_ref[...] = (d_yz * _silu(zz)).astype(dx_dtype)
            dz_ref[...] = (d_yz * xv * _dsilu(zz)).astype(dx_dtype)
        elif has_add:
            dx_ref[...] = (d_yz + add_ref[...]).astype(dx_dtype)
        else:
            dx_ref[...] = d_yz.astype(dx_dtype)
        part = jnp.sum(dy * xh, axis=0, keepdims=True)

        @pl.when(i == 0)
        def _():
            dg_ref[...] = part

        @pl.when(i > 0)
        def _():
            dg_ref[...] += part

    tile = pl.BlockSpec((tr, gw), lambda j, i: (i, j))
    specs = [pl.BlockSpec((tr, gw), lambda j, i: (i, x_col0 + j))]
    args = [x]
    if gated:
        specs.append(pl.BlockSpec((tr, gw), lambda j, i: (i, z_col0 + j)))
        args.append(z)
    specs += [pl.BlockSpec((1, gw), lambda j, i: (0, j)), tile]
    args += [g, dy]
    if has_add:
        specs.append(tile)
        args.append(add)
    width = gw * ncol
    out_shape = [jax.ShapeDtypeStruct((rows, width), dx_dtype)]
    out_specs = [tile]
    if gated:
        out_shape.append(jax.ShapeDtypeStruct((rows, width), dx_dtype))
        out_specs.append(tile)
    out_shape.append(jax.ShapeDtypeStruct((1, width), F32))
    out_specs.append(pl.BlockSpec((1, gw), lambda j, i: (0, j)))
    return pl.pallas_call(
        body, name=name, grid=(ncol, rows // tr), in_specs=specs, out_specs=out_specs, out_shape=out_shape,
        compiler_params=_cp(("parallel", "arbitrary")),
    )(*args)


HALO = 8


def _conv_rows(tc):
    return 16 * 8 * LANES // tc


def _conv_fwd(u, w8, b, *, kw, width, name, u_col0=0, mul_col0=None, out_dtype=F32):
    rows = u.shape[0]
    ts = _pick(rows, 512, 8)
    tc = _pick(width, 512)
    gated = mul_col0 is not None
    c0 = u_col0 // tc
    m0 = (mul_col0 // tc) if gated else 0
    assert u_col0 % tc == 0 and (not gated or mul_col0 % tc == 0)

    def body(*refs):
        if gated:
            cur_ref, halo_ref, mul_ref, w_ref, b_ref, o_ref, ext = refs
        else:
            cur_ref, halo_ref, w_ref, b_ref, o_ref, ext = refs
        i = pl.program_id(0)
        ext[pl.ds(0, HALO), :] = jnp.where(i == 0, 0.0, halo_ref[...])
        ext[pl.ds(HALO, ts), :] = cur_ref[...]
        bias = b_ref[...]
        taps = [w_ref[k:k + 1, :] for k in range(kw)]
        rb = _conv_rows(tc)
        for r0 in range(0, ts, rb):
            pre = bias + taps[0] * ext[pl.ds(r0 + HALO - (kw - 1), rb), :]
            for k in range(1, kw):
                pre = pre + taps[k] * ext[pl.ds(r0 + HALO - (kw - 1) + k, rb), :]
            act = _silu(pre)
            if gated:
                act = act * mul_ref[pl.ds(r0, rb), :]
            o_ref[pl.ds(r0, rb), :] = act.astype(out_dtype)

    hb = ts // HALO
    specs = [pl.BlockSpec((ts, tc), lambda i, j: (i, c0 + j)),
             pl.BlockSpec((HALO, tc), lambda i, j: (jnp.maximum(i * hb - 1, 0), c0 + j))]
    args = [u, u]
    if gated:
        specs.append(pl.BlockSpec((ts, tc), lambda i, j: (i, m0 + j)))
        args.append(u)
    specs += [pl.BlockSpec((8, tc), lambda i, j: (0, j)), pl.BlockSpec((1, tc), lambda i, j: (0, j))]
    args += [w8, b]
    return pl.pallas_call(
        body, name=name, grid=(rows // ts, width // tc), in_specs=specs,
        out_specs=pl.BlockSpec((ts, tc), lambda i, j: (i, j)),
        out_shape=jax.ShapeDtypeStruct((rows, width), out_dtype),
        scratch_shapes=[pltpu.VMEM((ts + HALO, tc), F32)],
        compiler_params=_cp(("parallel", "parallel")),
    )(*args)


def _conv_bwd(u, w8, b, dact, *, kw, width, name, u_col0=0, mul_col0=None, du_dtype=BF16):
    rows = u.shape[0]
    ts = _pick(rows, 512, 8)
    tc = _pick(width, 512)
    gated = mul_col0 is not None
    c0 = u_col0 // tc
    m0 = (mul_col0 // tc) if gated else 0
    nt = rows // ts
    hb = ts // HALO

    def body(*refs):
        refs = list(refs)
        cur_ref, halo_ref = refs.pop(0), refs.pop(0)
        mul_ref = refs.pop(0) if gated else None
        w_ref, b_ref, da_ref = refs.pop(0), refs.pop(0), refs.pop(0)
        du_ref = refs.pop(0)
        dmul_ref = refs.pop(0) if gated else None
        dwb_ref, ext_u, ext_d = refs
        t = pl.program_id(1)
        ti = nt - 1 - t
        ext_u[pl.ds(0, HALO), :] = jnp.where(ti == 0, 0.0, halo_ref[...])
        ext_u[pl.ds(HALO, ts), :] = cur_ref[...]

        @pl.when(t == 0)
        def _():
            ext_d[pl.ds(ts, HALO), :] = jnp.zeros((HALO, tc), F32)
            dwb_ref[...] = jnp.zeros((8, tc), F32)

        bias = b_ref[...]
        taps = [w_ref[k:k + 1, :] for k in range(kw)]
        rb = _conv_rows(tc)
        dw_acc = [jnp.zeros((1, tc), F32) for _ in range(kw)]
        db_acc = jnp.zeros((1, tc), F32)
        for r0 in reversed(range(0, ts, rb)):
            shifted = [ext_u[pl.ds(r0 + HALO - (kw - 1) + k, rb), :] for k in range(kw)]
            pre = bias + taps[0] * shifted[0]
            for k in range(1, kw):
                pre = pre + taps[k] * shifted[k]
            sg = _sigmoid(pre)
            dsilu = sg * (1.0 + pre * (1.0 - sg))
            da = da_ref[pl.ds(r0, rb), :].astype(F32)
            if gated:
                dmul_ref[pl.ds(r0, rb), :] = (da * (pre * sg)).astype(du_dtype)
                dgp = da * mul_ref[pl.ds(r0, rb), :] * dsilu
            else:
                dgp = da * dsilu
            ext_d[pl.ds(r0, rb), :] = dgp
            du = taps[kw - 1] * dgp
            for k in range(kw - 1):
                du = du + taps[k] * ext_d[pl.ds(r0 + kw - 1 - k, rb), :]
            du_ref[pl.ds(r0, rb), :] = du.astype(du_dtype)
            for k in range(kw):
                dw_acc[k] = dw_acc[k] + jnp.sum(dgp * shifted[k], axis=0, keepdims=True)
            db_acc = db_acc + jnp.sum(dgp, axis=0, keepdims=True)
        for k in range(kw):
            dwb_ref[k:k + 1, :] += dw_acc[k]
        dwb_ref[7:8, :] += db_acc
        ext_d[pl.ds(ts, HALO), :] = ext_d[pl.ds(0, HALO), :]

    specs = [pl.BlockSpec((ts, tc), lambda j, t: (nt - 1 - t, c0 + j)),
             pl.BlockSpec((HALO, tc), lambda j, t: (jnp.maximum((nt - 1 - t) * hb - 1, 0), c0 + j))]
    args = [u, u]
    if gated:
        specs.append(pl.BlockSpec((ts, tc), lambda j, t: (nt - 1 - t, m0 + j)))
        args.append(u)
    tile = pl.BlockSpec((ts, tc), lambda j, t: (nt - 1 - t, j))
    specs += [pl.BlockSpec((8, tc), lambda j, t: (0, j)), pl.BlockSpec((1, tc), lambda j, t: (0, j)), tile]
    args += [w8, b, dact]
    out_shape = [jax.ShapeDtypeStruct((rows, width), du_dtype)]
    out_specs = [tile]
    if gated:
        out_shape.append(jax.ShapeDtypeStruct((rows, width), du_dtype))
        out_specs.append(tile)
    out_shape.append(jax.ShapeDtypeStruct((8, width), F32))
    out_specs.append(pl.BlockSpec((8, tc), lambda j, t: (0, j)))
    return pl.pallas_call(
        body, name=name, grid=(width // tc, nt), in_specs=specs, out_specs=out_specs, out_shape=out_shape,
        scratch_shapes=[pltpu.VMEM((ts + HALO, tc), F32), pltpu.VMEM((ts + HALO, tc), F32)],
        compiler_params=_cp(("parallel", "arbitrary")),
    )(*args)


GW = SSD_HPG * SSD_HD


def _ssd_common(x, bm, cm, dt_raw, dt_raw_t, bias_r, bias_c, alog_r, alog_c):
    row = lax.broadcasted_iota(jnp.int32, (CHUNK, CHUNK), 0)
    col = lax.broadcasted_iota(jnp.int32, (CHUNK, CHUNK), 1)
    causal = row >= col
    tril = causal.astype(F32)
    triu = (row <= col).astype(F32)
    spread = _group_matrix(GW, SSD_HD, transpose=True)
    dt = _softplus(dt_raw + bias_r)
    dt_t = _softplus(dt_raw_t + bias_c)
    a_r = -jnp.exp(alog_r)
    a_c = -jnp.exp(alog_c)
    acs = _dotf(tril, dt * a_r, onehot="a", pieces=3)
    acs_t = _dotf(dt_t * a_c, triu, pieces=3)
    last = acs[CHUNK - 1:CHUNK, :]
    ds = jnp.exp(last - acs)
    cd = jnp.exp(last)
    c = dict(causal=causal, tril=tril, triu=triu, spread=spread, dt=dt, a_r=a_r, acs=acs, acs_t=acs_t, ds=ds, cd=cd)
    c["eb"] = _dotf(jnp.exp(acs), spread)
    c["dsb"] = _dotf(ds, spread)
    c["cdb"] = _dotf(cd, spread)
    c["dtb"] = _dotf(dt, spread)
    c["xdt"] = x * c["dtb"]
    c["cb"] = _dotb(cm, bm, NT)
    return c


def _ssd_lam(c, r):
    diff = c["acs"][:, r:r + 1] - c["acs_t"][r:r + 1, :]
    return jnp.exp(jnp.where(c["causal"], diff, -jnp.inf))


def _ssd_specs(nc, rev):
    def ci(t):
        return (nc - 1 - t) if rev else t
    xs = pl.BlockSpec((CHUNK, GW), lambda g, t: (ci(t), g))
    bs = pl.BlockSpec((CHUNK, SSD_N), lambda g, t: (ci(t), SSD_DI // SSD_N + g))
    cs = pl.BlockSpec((CHUNK, SSD_N), lambda g, t: (ci(t), SSD_DI // SSD_N + SSD_G + g))
    dts = pl.BlockSpec((1, CHUNK, 8), lambda g, t: (g, ci(t), 0))
    dtts = pl.BlockSpec((1, 8, CHUNK), lambda g, t: (g, 0, ci(t)))
    pr = pl.BlockSpec((1, 1, 8), lambda g, t: (g, 0, 0))
    pc = pl.BlockSpec((1, 8, 1), lambda g, t: (g, 0, 0))
    hs = pl.BlockSpec((1, 1, SSD_N, GW), lambda g, t: (ci(t), g, 0, 0))
    return xs, bs, cs, dts, dtts, pr, pc, hs


def _ssd_fwd(xbc, dtg, dtg_t, bias_r, bias_c, alog_r, alog_c, d_r, *, name):
    s = xbc.shape[0]
    nc = s // CHUNK
    xs, bs, cs, dts, dtts, pr, pc, hs = _ssd_specs(nc, False)

    def body(x_ref, b_ref, c_ref, dt_ref, dtt_ref, br_ref, bc_ref, ar_ref, ac_ref, d_ref, y_ref, hp_ref, h_sc):
        t = pl.program_id(1)

        @pl.when(t == 0)
        def _():
            h_sc[...] = jnp.zeros_like(h_sc)

        x, bm, cm = x_ref[...], b_ref[...], c_ref[...]
        c = _ssd_common(x, bm, cm, dt_ref[0], dtt_ref[0], br_ref[0], bc_ref[0], ar_ref[0], ac_ref[0])
        h = h_sc[...]
        hp_ref[0, 0] = h
        xdt = c["xdt"]
        pieces = []
        for r in range(SSD_HPG):
            m = c["cb"] * _ssd_lam(c, r)
            pieces.append(_dotb(m, xdt[:, r * SSD_HD:(r + 1) * SSD_HD]))
        y = jnp.concatenate(pieces, axis=1) + c["eb"] * _dotb(cm, h) + x * _dotf(d_ref[0], c["spread"])
        y_ref[...] = y
        h_sc[...] = h * c["cdb"] + _dotb(bm, xdt * c["dsb"], TN)

    return pl.pallas_call(
        body, name=name, grid=(SSD_G, nc),
        in_specs=[xs, bs, cs, dts, dtts, pr, pc, pr, pc, pr],
        out_specs=[xs, hs],
        out_shape=[jax.ShapeDtypeStruct((s, SSD_DI), F32), jax.ShapeDtypeStruct((nc, SSD_G, SSD_N, GW), F32)],
        scratch_shapes=[pltpu.VMEM((SSD_N, GW), F32)],
        compiler_params=_cp(("parallel", "arbitrary")),
    )(xbc, xbc, xbc, dtg, dtg_t, bias_r, bias_c, alog_r, alog_c, d_r)


def _ssd_bwd(xbc, dtg, dtg_t, bias_r, bias_c, alog_r, alog_c, d_r, hprev, dy, *, name):
    s = xbc.shape[0]
    nc = s // CHUNK
    xs, bs, cs, dts, dtts, pr, pc, hs = _ssd_specs(nc, True)
    gsum = functools.partial(_group_matrix, GW, SSD_HD)

    def body(x_ref, b_ref, c_ref, dt_ref, dtt_ref, br_ref, bc_ref, ar_ref, ac_ref, d_ref, hp_ref, dy_ref,
             dx_ref, db_ref, dc_ref, ddt_ref, dbias_ref, dalog_ref, dd_ref, dh_sc):
        t = pl.program_id(1)

        @pl.when(t == 0)
        def _():
            dh_sc[...] = jnp.zeros_like(dh_sc)
            dbias_ref[...] = jnp.zeros_like(dbias_ref)
            dalog_ref[...] = jnp.zeros_like(dalog_ref)
            dd_ref[...] = jnp.zeros_like(dd_ref)

        x, bm, cm = x_ref[...], b_ref[...], c_ref[...]
        c = _ssd_common(x, bm, cm, dt_ref[0], dtt_ref[0], br_ref[0], bc_ref[0], ar_ref[0], ac_ref[0])
        lanesum = gsum()
        h = hp_ref[0, 0]
        dh = dh_sc[...]
        dy = dy_ref[...]
        xdt, dsb = c["xdt"], c["dsb"]
        skip = _dotf(d_ref[0], c["spread"])
        dd_ref[0] += jnp.sum(_dotf(dy * x, lanesum), axis=0, keepdims=True)
        dacs = _dotf(dy * (c["eb"] * _dotb(cm, h)), lanesum)
        edy = c["eb"] * dy
        dcm = _dotb(edy, h, NT)
        dh_prev = _dotb(cm, edy, TN)
        bdh = _dotb(bm, dh)
        dxdt = dsb * bdh
        dbm = _dotb(dsb * xdt, dh, NT)
        t1 = _dotf(xdt * bdh, lanesum) * c["ds"]
        dacs = dacs - t1
        dlast = jnp.sum(t1, axis=0, keepdims=True) + jnp.sum(_dotf(dh * h, lanesum), axis=0, keepdims=True) * c["cd"]
        dcb = jnp.zeros((CHUNK, CHUNK), F32)
        pieces = []
        ones8 = jnp.ones((CHUNK, 8), F32)
        head = lax.broadcasted_iota(jnp.int32, (1, 8), 1)
        for r in range(SSD_HPG):
            sl = slice(r * SSD_HD, (r + 1) * SSD_HD)
            lam = _ssd_lam(c, r)
            m = c["cb"] * lam
            dm = _dotb(dy[:, sl], xdt[:, sl], NT)
            dcb = dcb + dm * lam
            gm = dm * m
            dacs = dacs + (jnp.sum(gm, axis=1, keepdims=True) - _dotf(gm, ones8, TN, pieces=3)) * (head == r).astype(F32)
            pieces.append(_dotb(m, dy[:, sl], TN))
        dxdt = dxdt + jnp.concatenate(pieces, axis=1)
        dcm = dcm + _dotb(dcb, bm)
        dbm = dbm + _dotb(dcb, cm, TN)
        dx_ref[...] = dy * skip + dxdt * c["dtb"]
        db_ref[...] = dbm
        dc_ref[...] = dcm
        rowid = lax.broadcasted_iota(jnp.int32, (CHUNK, 8), 0)
        dacs = dacs + jnp.where(rowid == CHUNK - 1, dlast, 0.0)
        dda = _dotf(c["triu"], dacs, onehot="a", pieces=3)
        ddt = _dotf(dxdt * x, lanesum) + dda * c["a_r"]
        ddt_raw = ddt * _sigmoid(dt_ref[0] + br_ref[0])
        ddt_ref[0] = ddt_raw
        dbias_ref[0] += jnp.sum(ddt_raw, axis=0, keepdims=True)
        dalog_ref[0] += jnp.sum(dda * c["dt"], axis=0, keepdims=True) * c["a_r"]
        dh_sc[...] = dh_prev + dh * c["cdb"]

    ci = lambda t: nc - 1 - t
    nspec = pl.BlockSpec((CHUNK, SSD_N), lambda g, t: (ci(t), g))
    return pl.pallas_call(
        body, name=name, grid=(SSD_G, nc),
        in_specs=[xs, bs, cs, dts, dtts, pr, pc, pr, pc, pr, hs, xs],
        out_specs=[xs, nspec, nspec, dts, pr, pr, pr],
        out_shape=[jax.ShapeDtypeStruct((s, SSD_DI), F32), jax.ShapeDtypeStruct((s, SSD_G * SSD_N), F32),
                   jax.ShapeDtypeStruct((s, SSD_G * SSD_N), F32), jax.ShapeDtypeStruct((SSD_G, s, 8), F32),
                   jax.ShapeDtypeStruct((SSD_G, 1, 8), F32), jax.ShapeDtypeStruct((SSD_G, 1, 8), F32),
                   jax.ShapeDtypeStruct((SSD_G, 1, 8), F32)],
        scratch_shapes=[pltpu.VMEM((SSD_N, GW), F32)],
        compiler_params=_cp(("parallel", "arbitrary")),
    )(xbc, xbc, xbc, dtg, dtg_t, bias_r, bias_c, alog_r, alog_c, d_r, hprev, dy)


FOX_PAIRS = FOX_H // 2
FOX_SCALE = FOX_HD ** -0.5
NEG_INF = -jnp.inf


def _fgate_fwd(f_t, b_c, *, name):
    hh, s = f_t.shape
    tb = _pick(s, 512)
    nb = s // tb

    def body(f_ref, b_ref, o_ref, carry):
        t = pl.program_id(0)

        @pl.when(t == 0)
        def _():
            carry[...] = jnp.zeros_like(carry)

        lf = -_softplus(-(f_ref[...] + b_ref[...]))
        row = lax.broadcasted_iota(jnp.int32, (tb, tb), 0)
        col = lax.broadcasted_iota(jnp.int32, (tb, tb), 1)
        cum = _dotf(lf, (row <= col).astype(F32), pieces=3) + carry[:, 0:1]
        o_ref[...] = cum
        carry[:, 0:1] = cum[:, tb - 1:tb]

    return pl.pallas_call(
        body, name=name, grid=(nb,),
        in_specs=[pl.BlockSpec((hh, tb), lambda t: (0, t)), pl.BlockSpec((hh, 1), lambda t: (0, 0))],
        out_specs=pl.BlockSpec((hh, tb), lambda t: (0, t)),
        out_shape=jax.ShapeDtypeStruct((hh, s), F32),
        scratch_shapes=[pltpu.VMEM((hh, LANES), F32)],
        compiler_params=_cp(("arbitrary",)),
    )(f_t, b_c)


def _fgate_bwd(dcum_q_t, dcum_k_t, f_t, b_c, *, name):
    hh, s = f_t.shape
    tb = _pick(s, 512)
    nb = s // tb

    def body(dq_ref, d_ref, f_ref, b_ref, df_ref, db_ref, carry):
        t = pl.program_id(0)

        @pl.when(t == 0)
        def _():
            carry[...] = jnp.zeros_like(carry)
            db_ref[...] = jnp.zeros_like(db_ref)

        d = d_ref[...] + dq_ref[...]
        row = lax.broadcasted_iota(jnp.int32, (tb, tb), 0)
        col = lax.broadcasted_iota(jnp.int32, (tb, tb), 1)
        rev = _dotf(d, (row >= col).astype(F32), pieces=3) + carry[:, 0:1]
        df = rev * _sigmoid(-(f_ref[...] + b_ref[...]))
        df_ref[...] = df
        db_ref[...] += jnp.sum(df, axis=1, keepdims=True)
        carry[:, 0:1] = rev[:, 0:1]

    blk = pl.BlockSpec((hh, tb), lambda t: (0, nb - 1 - t))
    return pl.pallas_call(
        body, name=name, grid=(nb,),
        in_specs=[blk, blk, blk, pl.BlockSpec((hh, 1), lambda t: (0, 0))],
        out_specs=[blk, pl.BlockSpec((hh, 1), lambda t: (0, 0))],
        out_shape=[jax.ShapeDtypeStruct((hh, s), F32), jax.ShapeDtypeStruct((hh, 1), F32)],
        scratch_shapes=[pltpu.VMEM((hh, LANES), F32)],
        compiler_params=_cp(("arbitrary",)),
    )(dcum_q_t, dcum_k_t, f_t, b_c)


def _fox_tile(s):
    return min(512, max(s // 2, 8))


def _tri_tables(nq, kv_major):
    if kv_major:
        pairs = [(i, j) for j in range(nq) for i in range(j, nq)]
    else:
        pairs = [(i, j) for i in range(nq) for j in range(i + 1)]
    return (jnp.asarray([p[0] for p in pairs], jnp.int32), jnp.asarray([p[1] for p in pairs], jnp.int32))


def _lane_tile(col, width):
    return col if width == LANES else jnp.tile(col, (1, width // LANES))


def _flash_fwd(qs, kn, qkvg, ck, *, name):
    s = qs.shape[0]
    tt = _fox_tile(s)
    nq = s // tt
    itab, jtab = _tri_tables(nq, kv_major=False)
    v0 = 2 * FOX_D // LANES

    def body(itab_ref, jtab_ref, q_ref, k_ref, v_ref, ck_ref, o_ref, lse_ref, m_sc, l_sc, acc_sc):
        t = pl.program_id(1)
        i, j = itab_ref[t], jtab_ref[t]

        @pl.when(j == 0)
        def _():
            m_sc[...] = jnp.full_like(m_sc, NEG_INF)
            l_sc[...] = jnp.zeros_like(l_sc)
            acc_sc[...] = jnp.zeros_like(acc_sc)

        low = lax.broadcasted_iota(jnp.int32, (tt, LANES), 1) < FOX_HD

        def step(diagonal):
            q2, k2 = q_ref[...], k_ref[...]
            v2 = v_ref[...].astype(BF16)
            alphas, outs = [], []
            for hh in range(2):
                qh = jnp.where(low if hh == 0 else jnp.logical_not(low), q2, jnp.zeros_like(q2))
                sc = lax.dot_general(qh, k2, NT, preferred_element_type=F32) - ck_ref[0][hh:hh + 1, :]
                if diagonal:
                    row = lax.broadcasted_iota(jnp.int32, sc.shape, 0)
                    col = lax.broadcasted_iota(jnp.int32, sc.shape, 1)
                    sc = jnp.where(row >= col, sc, NEG_INF)
                m_prev = m_sc[hh]
                m_new = jnp.maximum(m_prev, jnp.max(sc, axis=1, keepdims=True))
                alpha = jnp.exp(m_prev - m_new)
                p = jnp.exp(sc - _lane_tile(m_new, tt))
                l_sc[hh] = alpha * l_sc[hh] + jnp.sum(p, axis=1, keepdims=True)
                m_sc[hh] = m_new
                alphas.append(alpha)
                outs.append(lax.dot_general(p.astype(BF16), v2, NN, preferred_element_type=F32))
            acc_sc[...] = jnp.where(low, alphas[0], alphas[1]) * acc_sc[...] + jnp.where(low, outs[0], outs[1])

        @pl.when(j < i)
        def _():
            step(False)

        @pl.when(j == i)
        def _():
            step(True)
            o_ref[...] = acc_sc[...] / jnp.where(low, l_sc[0], l_sc[1])
            lse_ref[0] = jnp.concatenate([m_sc[hh][:, 0:1] + jnp.log(l_sc[hh][:, 0:1]) for hh in range(2)], axis=1)

    return pl.pallas_call(
        body, name=name,
        grid_spec=pltpu.PrefetchScalarGridSpec(
            num_scalar_prefetch=2, grid=(FOX_PAIRS, itab.shape[0]),
            in_specs=[pl.BlockSpec((tt, LANES), lambda p, t, it, jt: (it[t], p)),
                      pl.BlockSpec((tt, LANES), lambda p, t, it, jt: (jt[t], p)),
                      pl.BlockSpec((tt, LANES), lambda p, t, it, jt: (jt[t], v0 + p)),
                      pl.BlockSpec((1, 2, tt), lambda p, t, it, jt: (p, 0, jt[t]))],
            out_specs=[pl.BlockSpec((tt, LANES), lambda p, t, it, jt: (it[t], p)),
                       pl.BlockSpec((1, tt, 2), lambda p, t, it, jt: (p, it[t], 0))],
            scratch_shapes=[pltpu.VMEM((2, tt, LANES), F32), pltpu.VMEM((2, tt, LANES), F32), pltpu.VMEM((tt, LANES), F32)]),
        out_shape=[jax.ShapeDtypeStruct((s, FOX_D), F32), jax.ShapeDtypeStruct((FOX_PAIRS, s, 2), F32)],
        compiler_params=_cp(("parallel", "arbitrary")),
    )(itab, jtab, qs, kn, qkvg, ck)


def _flash_bwd(qs, kn, qkvg, do, lse, delta, ck, *, name):
    s = qs.shape[0]
    tt = _fox_tile(s)
    nq = s // tt
    itab, jtab = _tri_tables(nq, kv_major=True)
    nsteps = itab.shape[0]
    v0 = 2 * FOX_D // LANES

    def body(itab_ref, jtab_ref, q_ref, k_ref, v_ref, do_ref, lse_ref, dl_ref, ck_ref,
             dq_ref, dk_ref, dv_ref, dcq_ref, dck_ref, dq_sc, rs_sc, dk_sc, dv_sc, dc_sc):
        t = pl.program_id(1)
        i, j = itab_ref[t], jtab_ref[t]

        @pl.when(t == 0)
        def _():
            dq_sc[...] = jnp.zeros_like(dq_sc)
            rs_sc[...] = jnp.zeros_like(rs_sc)

        @pl.when(i == j)
        def _():
            dk_sc[...] = jnp.zeros_like(dk_sc)
            dv_sc[...] = jnp.zeros_like(dv_sc)
            dc_sc[...] = jnp.zeros_like(dc_sc)

        low = lax.broadcasted_iota(jnp.int32, (tt, LANES), 1) < FOX_HD
        rows = pl.ds(pl.multiple_of(i * tt, tt), tt)

        def step(diagonal):
            q2, k2 = q_ref[...], k_ref[...]
            v2 = v_ref[...].astype(BF16)
            do2 = do_ref[...].astype(BF16)
            dqs, dks, dvs = [], [], []
            for hh in range(2):
                sel = low if hh == 0 else jnp.logical_not(low)
                qh = jnp.where(sel, q2, jnp.zeros_like(q2))
                doh = jnp.where(sel, do2, jnp.zeros_like(do2))
                sc = lax.dot_general(qh, k2, NT, preferred_element_type=F32) - ck_ref[0][hh:hh + 1, :]
                if diagonal:
                    row = lax.broadcasted_iota(jnp.int32, sc.shape, 0)
                    col = lax.broadcasted_iota(jnp.int32, sc.shape, 1)
                    sc = jnp.where(row >= col, sc, NEG_INF)
                lse_b = jnp.broadcast_to(lse_ref[0][:, hh:hh + 1], (tt, LANES))
                dl_b = jnp.broadcast_to(dl_ref[0][:, hh:hh + 1], (tt, LANES))
                p = jnp.exp(sc - _lane_tile(lse_b, tt))
                dp = lax.dot_general(doh, v2, NT, preferred_element_type=F32)
                ds = p * (dp - _lane_tile(dl_b, tt))
                pb, dsb = p.astype(BF16), ds.astype(BF16)
                dvs.append(lax.dot_general(pb, do2, TN, preferred_element_type=F32))
                dks.append(lax.dot_general(dsb, q2, TN, preferred_element_type=F32))
                dqs.append(lax.dot_general(dsb, k2, NN, preferred_element_type=F32))
                rs_sc[hh, rows, :] += jnp.sum(ds, axis=1, keepdims=True)
                dc_sc[hh] += jnp.sum(ds, axis=0, keepdims=True)
            dv_sc[...] += jnp.where(low, dvs[0], dvs[1])
            dk_sc[...] += jnp.where(low, dks[0], dks[1])
            dq_sc[rows, :] += jnp.where(low, dqs[0], dqs[1])

        @pl.when(j < i)
        def _():
            step(False)

        @pl.when(j == i)
        def _():
            step(True)

        @pl.when(i == nq - 1)
        def _():
            dk_ref[...] = dk_sc[...]
            dv_ref[...] = dv_sc[...]
            dck_ref[0] = -jnp.concatenate([dc_sc[hh] for hh in range(2)], axis=0)

        @pl.when(t == nsteps - 1)
        def _():
            dq_ref[...] = dq_sc[...] * FOX_SCALE
            dcq_ref[0] = jnp.concatenate([rs_sc[hh] for hh in range(2)], axis=1)

    qside = pl.BlockSpec((tt, LANES), lambda p, t, it, jt: (it[t], p))
    kside = pl.BlockSpec((tt, LANES), lambda p, t, it, jt: (jt[t], p))
    stat = pl.BlockSpec((1, tt, 2), lambda p, t, it, jt: (p, it[t], 0))
    ckspec = pl.BlockSpec((1, 2, tt), lambda p, t, it, jt: (p, 0, jt[t]))
    return pl.pallas_call(
        body, name=name,
        grid_spec=pltpu.PrefetchScalarGridSpec(
            num_scalar_prefetch=2, grid=(FOX_PAIRS, nsteps),
            in_specs=[qside, kside, pl.BlockSpec((tt, LANES), lambda p, t, it, jt: (jt[t], v0 + p)), qside, stat, stat, ckspec],
            out_specs=[pl.BlockSpec((s, LANES), lambda p, t, it, jt: (0, p)), kside, kside,
                       pl.BlockSpec((1, s, 2), lambda p, t, it, jt: (p, 0, 0)), ckspec],
            scratch_shapes=[pltpu.VMEM((s, LANES), F32), pltpu.VMEM((2, s, 1), F32), pltpu.VMEM((tt, LANES), F32),
                            pltpu.VMEM((tt, LANES), F32), pltpu.VMEM((2, 1, tt), F32)]),
        out_shape=[jax.ShapeDtypeStruct((s, FOX_D), F32), jax.ShapeDtypeStruct((s, FOX_D), F32),
                   jax.ShapeDtypeStruct((s, FOX_D), F32), jax.ShapeDtypeStruct((FOX_PAIRS, s, 2), F32),
                   jax.ShapeDtypeStruct((FOX_PAIRS, 2, s), F32)],
        compiler_params=_cp(("parallel", "arbitrary")),
    )(itab, jtab, qs, kn, qkvg, do, lse, delta, ck)


def _ogate_fwd(o, qkvg, *, name):
    s = o.shape[0]
    tr = _pick(s, 512, 8)

    def body(o_ref, g_ref, out_ref):
        out_ref[...] = (o_ref[...] * _sigmoid(g_ref[...])).astype(BF16)

    tile = pl.BlockSpec((tr, FOX_D), lambda i: (i, 0))
    return pl.pallas_call(
        body, name=name, grid=(s // tr,), in_specs=[tile, pl.BlockSpec((tr, FOX_D), lambda i: (i, 3))],
        out_specs=tile, out_shape=jax.ShapeDtypeStruct((s, FOX_D), BF16), compiler_params=_cp(("parallel",)),
    )(o, qkvg)


def _ogate_bwd(dog, o, qkvg, *, name):
    s = o.shape[0]
    tr = _pick(s, 512, 8)

    def body(dog_ref, o_ref, g_ref, do_ref, dg_ref, dl_ref):
        sg = _sigmoid(g_ref[...])
        ov = o_ref[...]
        dog_v = dog_ref[...]
        do = dog_v * sg
        do_ref[...] = do
        dg_ref[...] = (dog_v * ov * sg * (1.0 - sg)).astype(BF16)
        dl_ref[...] = _dotf(do * ov, _group_matrix(FOX_D, FOX_HD))

    tile = pl.BlockSpec((tr, FOX_D), lambda i: (i, 0))
    return pl.pallas_call(
        body, name=name, grid=(s // tr,), in_specs=[tile, tile, pl.BlockSpec((tr, FOX_D), lambda i: (i, 3))],
        out_specs=[tile, tile, pl.BlockSpec((tr, FOX_H), lambda i: (i, 0))],
        out_shape=[jax.ShapeDtypeStruct((s, FOX_D), F32), jax.ShapeDtypeStruct((s, FOX_D), BF16),
                   jax.ShapeDtypeStruct((s, FOX_H), F32)],
        compiler_params=_cp(("parallel",)),
    )(dog, o, qkvg)


def _loss_head(h, g, target, *, name):
    s, d = h.shape
    tr = _pick(s, 512, 8)

    def body(h_ref, g_ref, t_ref, loss_ref, dh_ref, dg_ref):
        i = pl.program_id(0)
        x = h_ref[...]
        gv = g_ref[...]
        r = lax.rsqrt(jnp.mean(x * x, axis=-1, keepdims=True) + EPS)
        xh = x * r
        err = xh * gv - t_ref[...]
        part = 0.5 * jnp.sum(jnp.sum(err * err, axis=1, keepdims=True) * (1.0 / d), axis=0, keepdims=True)
        dy = err * (1.0 / d)
        dyg = dy * gv
        dh_ref[...] = r * (dyg - xh * jnp.mean(dyg * xh, axis=-1, keepdims=True))
        dgp = jnp.sum(dy * xh, axis=0, keepdims=True)

        @pl.when(i == 0)
        def _():
            loss_ref[...] = jnp.zeros_like(loss_ref) + part
            dg_ref[...] = dgp

        @pl.when(i > 0)
        def _():
            loss_ref[...] += part
            dg_ref[...] += dgp

    tile = pl.BlockSpec((tr, d), lambda i: (i, 0))
    vec = pl.BlockSpec((1, d), lambda i: (0, 0))
    return pl.pallas_call(
        body, name=name, grid=(s // tr,), in_specs=[tile, vec, tile],
        out_specs=[pl.BlockSpec((1, LANES), lambda i: (0, 0)), tile, vec],
        out_shape=[jax.ShapeDtypeStruct((1, LANES), F32), jax.ShapeDtypeStruct((s, d), F32),
                   jax.ShapeDtypeStruct((1, d), F32)],
        compiler_params=_cp(("arbitrary",)),
    )(h, g, target)


def _adamw(w, g, m, v, *, name):
    rows, cols = w.shape
    tr = _pick(rows, 256, 8)
    c1 = 1.0 - ADAM_B1 ** ADAM_STEP
    c2 = 1.0 - ADAM_B2 ** ADAM_STEP

    def body(w_ref, g_ref, m_ref, v_ref, d_ref, nm_ref, nv_ref):
        gv = g_ref[...]
        nm = ADAM_B1 * m_ref[...] + (1.0 - ADAM_B1) * gv
        nv = ADAM_B2 * v_ref[...] + (1.0 - ADAM_B2) * (gv * gv)
        d_ref[...] = -ADAM_LR * ((nm / c1) / (jnp.sqrt(nv / c2) + ADAM_EPS) + ADAM_WD * w_ref[...])
        nm_ref[...] = nm
        nv_ref[...] = nv

    tile = pl.BlockSpec((tr, cols), lambda i: (i, 0))
    shp = jax.ShapeDtypeStruct((rows, cols), F32)
    return pl.pallas_call(
        body, name=name, grid=(rows // tr,), in_specs=[tile] * 4, out_specs=[tile] * 3, out_shape=[shp] * 3,
        compiler_params=_cp(("parallel",)),
    )(w, g, m, v)


ANY = pl.BlockSpec(memory_space=pl.ANY)
N_DEV = 8


def _coords():
    return lax.axis_index("x"), lax.axis_index("y"), lax.axis_index("c")


def _other_chips(x, y):
    return [(1 - x, y), (x, 1 - y), (1 - x, 1 - y)]


def _allgather_small(buf, *, name, with_sum):
    rows = buf.shape[0]

    def body(*refs):
        if with_sum:
            x_ref, out_ref, sum_ref, send_sems, recv_sems = refs
        else:
            x_ref, out_ref, send_sems, recv_sems = refs
        x, y, c = _coords()
        me = 4 * x + 2 * y + c
        out_ref[me] = x_ref[...]
        copies = []
        for rel in range(1, N_DEV):
            px = (1 - x) if rel & 4 else x
            py = (1 - y) if rel & 2 else y
            pc = (1 - c) if rel & 1 else c
            cp = pltpu.make_async_remote_copy(
                src_ref=x_ref, dst_ref=out_ref.at[me], send_sem=send_sems.at[rel - 1], recv_sem=recv_sems.at[rel - 1],
                device_id=(px, py, pc), device_id_type=MESH)
            cp.start()
            copies.append(cp)
        for cp in copies:
            cp.wait()
        if with_sum:
            acc = out_ref[0]
            for k in range(1, N_DEV):
                acc = acc + out_ref[k]
            sum_ref[...] = acc

    slots = jax.ShapeDtypeStruct((N_DEV, rows, LANES), F32)
    vm = pl.BlockSpec(memory_space=pltpu.VMEM)
    out_shape = [slots, jax.ShapeDtypeStruct((rows, LANES), F32)] if with_sum else [slots]
    return pl.pallas_call(
        body, name=name, in_specs=[vm], out_specs=[vm] * len(out_shape), out_shape=out_shape,
        scratch_shapes=[pltpu.SemaphoreType.DMA((N_DEV - 1,)), pltpu.SemaphoreType.DMA((N_DEV - 1,))],
    )(buf)


def _allgather_chips(shards, *, name):
    n = len(shards)

    def body(*refs):
        ins, outs = refs[:n], refs[n:2 * n]
        send_sems, recv_sems, local_sems = refs[2 * n:]
        x, y, c = _coords()
        k = 2 * x + y
        chips = _other_chips(x, y)
        sibling = (x, y, 1 - c)
        local = []
        for t in range(n):
            cp = pltpu.make_async_copy(ins[t], outs[t].at[:, k], local_sems.at[t])
            cp.start()
            local.append(cp)
        sends = []
        for t in range(n):
            for j, (px, py) in enumerate(chips):
                cp = pltpu.make_async_remote_copy(
                    src_ref=ins[t].at[:, c], dst_ref=outs[t].at[:, k, c], send_sem=send_sems.at[6 * t + j],
                    recv_sem=recv_sems.at[6 * t + j], device_id=(px, py, c), device_id_type=MESH)
                cp.start()
                sends.append(cp)
        for t in range(n):
            for j, (px, py) in enumerate(chips):
                kj = 2 * px + py
                pltpu.make_async_remote_copy(
                    src_ref=ins[t].at[:, c], dst_ref=outs[t].at[:, kj, c], send_sem=send_sems.at[6 * t + j],
                    recv_sem=recv_sems.at[6 * t + j], device_id=(px, py, c), device_id_type=MESH).wait_recv()
                cp = pltpu.make_async_remote_copy(
                    src_ref=outs[t].at[:, kj, c], dst_ref=outs[t].at[:, kj, c], send_sem=send_sems.at[6 * t + 3 + j],
                    recv_sem=recv_sems.at[6 * t + 3 + j], device_id=sibling, device_id_type=MESH)
                cp.start()
                sends.append(cp)
        for t in range(n):
            for j, (px, py) in enumerate(chips):
                kj = 2 * px + py
                pltpu.make_async_remote_copy(
                    src_ref=outs[t].at[:, kj, 1 - c], dst_ref=outs[t].at[:, kj, 1 - c], send_sem=send_sems.at[6 * t + 3 + j],
                    recv_sem=recv_sems.at[6 * t + 3 + j], device_id=sibling, device_id_type=MESH).wait_recv()
        for cp in sends:
            cp.wait_send()
        for cp in local:
            cp.wait()

    return pl.pallas_call(
        body, name=name, in_specs=[ANY] * n, out_specs=[ANY] * n,
        out_shape=[jax.ShapeDtypeStruct((s.shape[0], 4) + s.shape[1:], s.dtype) for s in shards],
        scratch_shapes=[pltpu.SemaphoreType.DMA((6 * n,)), pltpu.SemaphoreType.DMA((6 * n,)), pltpu.SemaphoreType.DMA((n,))],
    )(*shards)


def _sibling_swap(arrs, *, name):
    n = len(arrs)

    def body(*refs):
        ins, outs = refs[:n], refs[n:2 * n]
        send_sems, recv_sems = refs[2 * n:]
        x, y, c = _coords()
        copies = []
        for t in range(n):
            cp = pltpu.make_async_remote_copy(
                src_ref=ins[t].at[:, 1 - c], dst_ref=outs[t], send_sem=send_sems.at[t],
                recv_sem=recv_sems.at[t], device_id=(x, y, 1 - c), device_id_type=MESH)
            cp.start()
            copies.append(cp)
        for cp in copies:
            cp.wait()

    return pl.pallas_call(
        body, name=name, in_specs=[ANY] * n, out_specs=[ANY] * n,
        out_shape=[jax.ShapeDtypeStruct(a.shape[:1] + a.shape[2:], a.dtype) for a in arrs],
        scratch_shapes=[pltpu.SemaphoreType.DMA((n,)), pltpu.SemaphoreType.DMA((n,))],
    )(*arrs)


def _assemble_halves(groups, *, name):
    flat = [a for grp in groups for a in grp]
    where = [(g, l) for g, grp in enumerate(groups) for l in range(len(grp))]
    n, ng = len(flat), len(groups)

    def body(*refs):
        ins, outs = refs[:n], refs[n:n + ng]
        send_sems, recv_sems, local_sems = refs[n + ng:]
        x, y, c = _coords()
        copies = []
        for t, (g, l) in enumerate(where):
            lc = pltpu.make_async_copy(ins[t], outs[g].at[l, c], local_sems.at[t])
            lc.start()
            cp = pltpu.make_async_remote_copy(
                src_ref=ins[t], dst_ref=outs[g].at[l, c], send_sem=send_sems.at[t], recv_sem=recv_sems.at[t],
                device_id=(x, y, 1 - c), device_id_type=MESH)
            cp.start()
            copies += [lc, cp]
        for cp in copies:
            cp.wait()

    return pl.pallas_call(
        body, name=name, in_specs=[ANY] * n, out_specs=[ANY] * ng,
        out_shape=[jax.ShapeDtypeStruct((len(grp), 2) + grp[0].shape, grp[0].dtype) for grp in groups],
        scratch_shapes=[pltpu.SemaphoreType.DMA((n,)), pltpu.SemaphoreType.DMA((n,)), pltpu.SemaphoreType.DMA((n,))],
    )(*flat)


def _chip_exchange(arrs, *, name):
    n = len(arrs)

    def body(*refs):
        ins, outs = refs[:n], refs[n:2 * n]
        send_sems, recv_sems = refs[2 * n:]
        x, y, c = _coords()
        copies = []
        for t in range(n):
            for j, (px, py) in enumerate(_other_chips(x, y)):
                cp = pltpu.make_async_remote_copy(
                    src_ref=ins[t].at[2 * px + py], dst_ref=outs[t].at[j], send_sem=send_sems.at[3 * t + j],
                    recv_sem=recv_sems.at[3 * t + j], device_id=(px, py, c), device_id_type=MESH)
                cp.start()
                copies.append(cp)
        for cp in copies:
            cp.wait()

    return pl.pallas_call(
        body, name=name, in_specs=[ANY] * n, out_specs=[ANY] * n,
        out_shape=[jax.ShapeDtypeStruct((3,) + a.shape[1:], a.dtype) for a in arrs],
        scratch_shapes=[pltpu.SemaphoreType.DMA((3 * n,)), pltpu.SemaphoreType.DMA((3 * n,))],
    )(*arrs)


def _pair_add(g, recv, sel, *, name):
    _, _, m, cols = g.shape
    tr = _pick(m, 256, 16)

    def body(sel_ref, g_ref, r_ref, out_ref, low_ref):
        acc = g_ref[0, 0] + r_ref[0]
        out_ref[0] = acc
        low_ref[0] = acc.astype(BF16)

    tile = pl.BlockSpec((1, tr, cols), lambda q, i, sel_ref: (q, i, 0))
    return pl.pallas_call(
        body, name=name,
        grid_spec=pltpu.PrefetchScalarGridSpec(
            num_scalar_prefetch=1, grid=(4, m // tr),
            in_specs=[pl.BlockSpec((1, 1, tr, cols), lambda q, i, sel_ref: (q, sel_ref[0], i, 0)), tile],
            out_specs=[tile, tile]),
        out_shape=[jax.ShapeDtypeStruct((4, m, cols), F32), jax.ShapeDtypeStruct((4, m, cols), BF16)],
        compiler_params=_cp(("parallel", "parallel")),
    )(sel, g, recv)


def _add_selected(stack, others, sel, *, name):
    _, m, cols = stack.shape
    q = others.shape[0]
    tr = _pick(m, 256, 16)

    def body(sel_ref, s_ref, o_ref, out_ref):
        acc = s_ref[0].astype(F32)
        for i in range(q):
            acc = acc + o_ref[i].astype(F32)
        out_ref[...] = acc

    return pl.pallas_call(
        body, name=name,
        grid_spec=pltpu.PrefetchScalarGridSpec(
            num_scalar_prefetch=1, grid=(m // tr,),
            in_specs=[pl.BlockSpec((1, tr, cols), lambda i, sel_ref: (sel_ref[0], i, 0)),
                      pl.BlockSpec((q, tr, cols), lambda i, sel_ref: (0, i, 0))],
            out_specs=pl.BlockSpec((tr, cols), lambda i, sel_ref: (i, 0))),
        out_shape=jax.ShapeDtypeStruct((m, cols), F32),
        compiler_params=_cp(("parallel",)),
    )(sel, stack, others)


def _reduce_scatter(grads, c, k):
    csel, ksel = jnp.reshape(c, (1,)).astype(jnp.int32), jnp.reshape(k, (1,)).astype(jnp.int32)
    from_sib = _sibling_swap(grads, name="rs_pair_swap")
    sums = [_pair_add(g, r, csel, name=f"rs_pair_add{t}") for t, (g, r) in enumerate(zip(grads, from_sib))]
    from_chips = _chip_exchange([low for _, low in sums], name="rs_chip_exchange")
    return [_add_selected(s, r, ksel, name=f"rs_chip_add{t}") for t, ((s, _), r) in enumerate(zip(sums, from_chips))]


BIG = ("ssd_w_in", "ssd_w_out", "fox_w_in", "fox_w_out", "ffn_w_up", "ffn_w_down")
COL_SHARDED = ("ssd_w_in", "fox_w_in", "ffn_w_up")
SMALL = (("mix_norm_g", (4, 1024)), ("ffn_norm_g", (4, 1024)), ("ssd_conv_w", (2, 4, 3072)), ("ssd_conv_b", (2, 3072)),
         ("ssd_dt_bias", (2, 32)), ("ssd_a_log", (2, 32)), ("ssd_d", (2, 32)), ("ssd_norm_g", (2, 2048)),
         ("fox_b_f", (2, 16)), ("fox_q_norm_g", (2, 64)), ("fox_k_norm_g", (2, 64)), ("ffn_conv_w", (4, 3, 2816)),
         ("ffn_conv_b", (4, 2816)), ("final_norm_g", (1024,)), ("loss", (1,)))
NAMES = ("mix_norm_g", "ffn_norm_g", "ssd_w_in", "ssd_conv_w", "ssd_conv_b", "ssd_dt_bias", "ssd_a_log", "ssd_d",
         "ssd_norm_g", "ssd_w_out", "fox_w_in", "fox_b_f", "fox_q_norm_g", "fox_k_norm_g", "fox_w_out", "ffn_w_up",
         "ffn_conv_w", "ffn_conv_b", "ffn_w_down", "final_norm_g")


def _pack(parts):
    flat = jnp.concatenate([jnp.reshape(p, (-1,)).astype(F32) for p in parts])
    rows = -(-flat.shape[0] // (8 * LANES)) * 8
    return jnp.pad(flat, (0, rows * LANES - flat.shape[0])).reshape(rows, LANES)


def _unpack(buf, shapes):
    flat = buf.reshape(-1)
    out, off = [], 0
    for shp in shapes:
        size = 1
        for d in shp:
            size *= d
        out.append(flat[off:off + size].reshape(shp))
        off += size
    return out


def _pad_lanes(a):
    return jnp.pad(a, ((0, 0), (0, LANES - a.shape[1])))


def _pad8(w):
    return jnp.pad(w, ((0, 8 - w.shape[0]), (0, 0)))


def _ssd_forward(h, p, name):
    s = h.shape[0]
    hn = _rms_fwd(h, p["mix_g"], gw=D_MODEL, ncol=1, name=f"{name}_norm")
    zx = _matmul(hn, p["w_zx"], mode="nn", name=f"{name}_proj")
    dtp = _matmul(hn, p["w_dt"], mode="nn", name=f"{name}_proj_dt")
    xbc = _conv_fwd(zx, p["conv_w8"], p["conv_b"], kw=SSD_K, width=SSD_CONV_DIM, u_col0=SSD_DI, name=f"{name}_conv")
    dt3 = dtp[:, :SSD_H].reshape(s, SSD_G, SSD_HPG)
    dtg, dtg_t = jnp.transpose(dt3, (1, 0, 2)), jnp.transpose(dt3, (1, 2, 0))
    sp = (p["bias_r"], p["bias_c"], p["alog_r"], p["alog_c"], p["d_r"])
    y, hprev = _ssd_fwd(xbc, dtg, dtg_t, *sp, name=f"{name}_scan")
    y2 = _rms_fwd(y, p["norm_g"], gw=SSD_DI // SSD_G, ncol=SSD_G, z=zx, name=f"{name}_gnorm")
    out = _matmul(y2, p["w_out"], mode="nn", add=h, name=f"{name}_out")
    return out, dict(h=h, hn=hn, zx=zx, xbc=xbc, dtg=dtg, dtg_t=dtg_t, y=y, hprev=hprev, y2=y2)


def _ssd_backward(dh1, p, a, name):
    s = dh1.shape[0]
    g = {}
    dy2 = _matmul(dh1, p["w_out"], mode="nt", name=f"{name}_out_dx")
    g["w_out"] = _matmul(a["y2"], dh1, mode="tn", name=f"{name}_out_dw")
    dy, dz, g["norm_g"] = _rms_bwd(a["y"], p["norm_g"], dy2, gw=SSD_DI // SSD_G, ncol=SSD_G, z=a["zx"], name=f"{name}_gnorm_b")
    sp = (p["bias_r"], p["bias_c"], p["alog_r"], p["alog_c"], p["d_r"])
    dx, dbm, dcm, ddt, g["dt_bias"], g["a_log"], g["d"] = _ssd_bwd(
        a["xbc"], a["dtg"], a["dtg_t"], *sp, a["hprev"], dy, name=f"{name}_scan_b")
    dact = jnp.concatenate([dx, dbm, dcm], axis=1)
    dxbc, dwb = _conv_bwd(a["zx"], p["conv_w8"], p["conv_b"], dact, kw=SSD_K, width=SSD_CONV_DIM, u_col0=SSD_DI,
                          name=f"{name}_conv_b")
    g["conv_w"], g["conv_b"] = dwb[:SSD_K], dwb[7]
    dzx = jnp.concatenate([dz.astype(BF16), dxbc], axis=1)
    ddtp = _pad_lanes(jnp.transpose(ddt, (1, 0, 2)).reshape(s, SSD_H))
    dhn = _matmul(dzx, p["w_zx"], mode="nt", name=f"{name}_proj_dx")
    dhn = _matmul(ddtp, p["w_dt"], mode="nt", add=dhn, name=f"{name}_proj_dt_dx")
    dw_zx = _matmul(a["hn"], dzx, mode="tn", name=f"{name}_proj_dw")
    dw_dt = _matmul(a["hn"], ddtp, mode="tn", name=f"{name}_proj_dt_dw")
    g["w_in"] = jnp.concatenate([dw_zx, dw_dt[:, :SSD_H]], axis=1)
    dh, g["mix_g"] = _rms_bwd(a["h"], p["mix_g"], dhn, gw=D_MODEL, ncol=1, add=dh1, name=f"{name}_norm_b")
    return dh, g


def _fox_forward(h, p, name):
    s = h.shape[0]
    hn = _rms_fwd(h, p["mix_g"], gw=D_MODEL, ncol=1, name=f"{name}_norm")
    qkvg = _matmul(hn, p["w_qkvg"], mode="nn", name=f"{name}_proj")
    fp = _matmul(hn, p["w_f"], mode="nn", name=f"{name}_proj_f")
    qs = _rms_fwd(qkvg, p["gq"] * FOX_SCALE, gw=FOX_D, ncol=1, x_col0=0, sub=FOX_HD, name=f"{name}_qnorm")
    kn = _rms_fwd(qkvg, p["gk"], gw=FOX_D, ncol=1, x_col0=1, sub=FOX_HD, name=f"{name}_knorm")
    f_t = jnp.transpose(fp[:, :FOX_H])
    cum_t = _fgate_fwd(f_t, p["b_f"], name=f"{name}_fgate")
    ck = cum_t.reshape(FOX_PAIRS, 2, s)
    o, lse = _flash_fwd(qs, kn, qkvg, ck, name=f"{name}_attn")
    og = _ogate_fwd(o, qkvg, name=f"{name}_ogate")
    out = _matmul(og, p["w_out"], mode="nn", add=h, name=f"{name}_out")
    return out, dict(h=h, hn=hn, qkvg=qkvg, qs=qs, kn=kn, f_t=f_t, ck=ck, o=o, lse=lse, og=og)


def _fox_backward(dh1, p, a, name):
    s = dh1.shape[0]
    g = {}
    dog = _matmul(dh1, p["w_out"], mode="nt", name=f"{name}_out_dx")
    g["w_out"] = _matmul(a["og"], dh1, mode="tn", name=f"{name}_out_dw")
    do, dgate, delta = _ogate_bwd(dog, a["o"], a["qkvg"], name=f"{name}_ogate_b")
    dl = jnp.transpose(delta.reshape(s, FOX_PAIRS, 2), (1, 0, 2))
    dq, dk, dv, dcq, dck = _flash_bwd(a["qs"], a["kn"], a["qkvg"], do, a["lse"], dl, a["ck"], name=f"{name}_attn_b")
    dq_raw, dgq = _rms_bwd(a["qkvg"], p["gq"], dq, gw=FOX_D, ncol=1, x_col0=0, sub=FOX_HD, dx_dtype=BF16, name=f"{name}_qnorm_b")
    dk_raw, dgk = _rms_bwd(a["qkvg"], p["gk"], dk, gw=FOX_D, ncol=1, x_col0=1, sub=FOX_HD, dx_dtype=BF16, name=f"{name}_knorm_b")
    g["gq"] = dgq.reshape(FOX_H, FOX_HD).sum(axis=0)
    g["gk"] = dgk.reshape(FOX_H, FOX_HD).sum(axis=0)
    dcq_t = jnp.transpose(dcq, (0, 2, 1)).reshape(FOX_H, s)
    df_t, dbf = _fgate_bwd(dcq_t, dck.reshape(FOX_H, s), a["f_t"], p["b_f"], name=f"{name}_fgate_b")
    g["b_f"] = dbf[:, 0]
    dproj = jnp.concatenate([dq_raw, dk_raw, dv.astype(BF16), dgate], axis=1)
    dfp = _pad_lanes(jnp.transpose(df_t))
    dhn = _matmul(dproj, p["w_qkvg"], mode="nt", name=f"{name}_proj_dx")
    dhn = _matmul(dfp, p["w_f"], mode="nt", add=dhn, name=f"{name}_proj_f_dx")
    dw_qkvg = _matmul(a["hn"], dproj, mode="tn", name=f"{name}_proj_dw")
    dw_f = _matmul(a["hn"], dfp, mode="tn", name=f"{name}_proj_f_dw")
    g["w_in"] = jnp.concatenate([dw_qkvg, dw_f[:, :FOX_H]], axis=1)
    dh, g["mix_g"] = _rms_bwd(a["h"], p["mix_g"], dhn, gw=D_MODEL, ncol=1, add=dh1, name=f"{name}_norm_b")
    return dh, g


def _ffn_forward(h, p, name):
    hn = _rms_fwd(h, p["ffn_g"], gw=D_MODEL, ncol=1, name=f"{name}_norm")
    u = _matmul(hn, p["w_up"], mode="nn", name=f"{name}_up")
    act = _conv_fwd(u, p["conv_w8"], p["conv_b"], kw=FFN_K, width=D_FF, u_col0=0, mul_col0=D_FF, out_dtype=BF16,
                    name=f"{name}_glu")
    out = _matmul(act, p["w_down"], mode="nn", add=h, name=f"{name}_down")
    return out, dict(h=h, hn=hn, u=u, act=act)


def _ffn_backward(dh2, p, a, name):
    g = {}
    dact = _matmul(dh2, p["w_down"], mode="nt", name=f"{name}_down_dx")
    g["w_down"] = _matmul(a["act"], dh2, mode="tn", name=f"{name}_down_dw")
    du1, du2, dwb = _conv_bwd(a["u"], p["conv_w8"], p["conv_b"], dact, kw=FFN_K, width=D_FF, u_col0=0, mul_col0=D_FF,
                              name=f"{name}_glu_b")
    g["conv_w"], g["conv_b"] = dwb[:FFN_K], dwb[7]
    du = jnp.concatenate([du1, du2], axis=1)
    dhn = _matmul(du, p["w_up"], mode="nt", name=f"{name}_up_dx")
    g["w_up"] = _matmul(a["hn"], du, mode="tn", name=f"{name}_up_dw")
    dh, g["ffn_g"] = _rms_bwd(a["h"], p["ffn_g"], dhn, gw=D_MODEL, ncol=1, add=dh2, name=f"{name}_norm_b")
    return dh, g


def _to_slabs(dw, col_sharded):
    rows, cols = dw.shape
    if col_sharded:
        return jnp.transpose(dw.reshape(rows, 4, cols // 4), (1, 0, 2)).reshape(4, 2, rows // 2, cols // 4)
    return dw.reshape(4, 2, rows // 8, cols)


def kernel(x, mix_norm_g, ffn_norm_g, ssd_w_in, ssd_conv_w, ssd_conv_b, ssd_dt_bias, ssd_a_log, ssd_d, ssd_norm_g, ssd_w_out, fox_w_in, fox_b_f, fox_q_norm_g, fox_k_norm_g, fox_w_out, ffn_w_up, ffn_conv_w, ffn_conv_b, ffn_w_down, final_norm_g, loss_target, m_mix_norm_g, m_ffn_norm_g, m_ssd_w_in, m_ssd_conv_w, m_ssd_conv_b, m_ssd_dt_bias, m_ssd_a_log, m_ssd_d, m_ssd_norm_g, m_ssd_w_out, m_fox_w_in, m_fox_b_f, m_fox_q_norm_g, m_fox_k_norm_g, m_fox_w_out, m_ffn_w_up, m_ffn_conv_w, m_ffn_conv_b, m_ffn_w_down, m_final_norm_g, v_mix_norm_g, v_ffn_norm_g, v_ssd_w_in, v_ssd_conv_w, v_ssd_conv_b, v_ssd_dt_bias, v_ssd_a_log, v_ssd_d, v_ssd_norm_g, v_ssd_w_out, v_fox_w_in, v_fox_b_f, v_fox_q_norm_g, v_fox_k_norm_g, v_fox_w_out, v_ffn_w_up, v_ffn_conv_w, v_ffn_conv_b, v_ffn_w_down, v_final_norm_g):
    w = dict(mix_norm_g=mix_norm_g, ffn_norm_g=ffn_norm_g, ssd_w_in=ssd_w_in, ssd_conv_w=ssd_conv_w, ssd_conv_b=ssd_conv_b,
             ssd_dt_bias=ssd_dt_bias, ssd_a_log=ssd_a_log, ssd_d=ssd_d, ssd_norm_g=ssd_norm_g, ssd_w_out=ssd_w_out,
             fox_w_in=fox_w_in, fox_b_f=fox_b_f, fox_q_norm_g=fox_q_norm_g, fox_k_norm_g=fox_k_norm_g, fox_w_out=fox_w_out,
             ffn_w_up=ffn_w_up, ffn_conv_w=ffn_conv_w, ffn_conv_b=ffn_conv_b, ffn_w_down=ffn_w_down, final_norm_g=final_norm_g)
    m_in = dict(zip(NAMES, (m_mix_norm_g, m_ffn_norm_g, m_ssd_w_in, m_ssd_conv_w, m_ssd_conv_b, m_ssd_dt_bias, m_ssd_a_log,
                            m_ssd_d, m_ssd_norm_g, m_ssd_w_out, m_fox_w_in, m_fox_b_f, m_fox_q_norm_g, m_fox_k_norm_g,
                            m_fox_w_out, m_ffn_w_up, m_ffn_conv_w, m_ffn_conv_b, m_ffn_w_down, m_final_norm_g)))
    v_in = dict(zip(NAMES, (v_mix_norm_g, v_ffn_norm_g, v_ssd_w_in, v_ssd_conv_w, v_ssd_conv_b, v_ssd_dt_bias, v_ssd_a_log,
                            v_ssd_d, v_ssd_norm_g, v_ssd_w_out, v_fox_w_in, v_fox_b_f, v_fox_q_norm_g, v_fox_k_norm_g,
                            v_fox_w_out, v_ffn_w_up, v_ffn_conv_w, v_ffn_conv_b, v_ffn_w_down, v_final_norm_g)))
    cx, cy, cc = _coords()
    chip = 2 * cx + cy
    h = x[0]
    target = loss_target[0]

    conv_shapes = [ssd_conv_w.shape, ffn_conv_w.shape]
    slots = _allgather_small(_pack([ssd_conv_w, ffn_conv_w]), name="gather_conv_w", with_sum=False)[0]
    per_chip = [_unpack(slots[2 * q], conv_shapes) for q in range(4)]
    ssd_conv_full = jnp.concatenate([pc[0] for pc in per_chip], axis=2)
    ffn_conv_full = jnp.concatenate([pc[1] for pc in per_chip], axis=2)
    shards = [w[n].astype(BF16) for n in BIG]
    gathered = _allgather_chips([s.reshape((s.shape[0], 2, s.shape[1] // 2, s.shape[2])) for s in shards],
                                name="gather_weights")
    full = {}
    for n, gth in zip(BIG, gathered):
        nl, _, _, half, cols = gth.shape
        if n in COL_SHARDED:
            full[n] = jnp.transpose(gth.reshape(nl, 4, 2 * half, cols), (0, 2, 1, 3)).reshape(nl, 2 * half, 4 * cols)
        else:
            full[n] = gth.reshape(nl, 8 * half, cols)

    def ssd_params(j, i):
        w_in = full["ssd_w_in"][j]
        g3 = lambda v: v.reshape(SSD_G, 1, SSD_HPG)
        g3c = lambda v: v.reshape(SSD_G, SSD_HPG, 1)
        return dict(mix_g=mix_norm_g[i][None], w_zx=w_in[:, :SSD_ZX], w_dt=_pad_lanes(w_in[:, SSD_ZX:]),
                    conv_w8=_pad8(ssd_conv_full[j]), conv_b=ssd_conv_b[j][None], bias_r=g3(ssd_dt_bias[j]),
                    bias_c=g3c(ssd_dt_bias[j]), alog_r=g3(ssd_a_log[j]), alog_c=g3c(ssd_a_log[j]), d_r=g3(ssd_d[j]),
                    norm_g=ssd_norm_g[j][None], w_out=full["ssd_w_out"][j])

    def fox_params(j, i):
        w_in = full["fox_w_in"][j]
        return dict(mix_g=mix_norm_g[i][None], w_qkvg=w_in[:, :4 * FOX_D], w_f=_pad_lanes(w_in[:, 4 * FOX_D:]),
                    gq=jnp.tile(fox_q_norm_g[j], FOX_H)[None], gk=jnp.tile(fox_k_norm_g[j], FOX_H)[None],
                    b_f=fox_b_f[j][:, None], w_out=full["fox_w_out"][j])

    def ffn_params(i):
        return dict(ffn_g=ffn_norm_g[i][None], w_up=full["ffn_w_up"][i], conv_w8=_pad8(ffn_conv_full[i]),
                    conv_b=ffn_conv_b[i][None], w_down=full["ffn_w_down"][i])

    mix_p, ffn_p, mix_a, ffn_a = [], [], [], []
    for i in range(DEPTH):
        j = i // 2
        if i % 2 == 0:
            mix_p.append(ssd_params(j, i))
            h, act = _ssd_forward(h, mix_p[i], f"ssd{j}")
        else:
            mix_p.append(fox_params(j, i))
            h, act = _fox_forward(h, mix_p[i], f"fox{j}")
        mix_a.append(act)
        ffn_p.append(ffn_params(i))
        h, act = _ffn_forward(h, ffn_p[i], f"ffn{i}")
        ffn_a.append(act)
    loss_part, dh, d_final_g = _loss_head(h, final_norm_g[None], target, name="loss_head")

    mix_g, ffn_g = [None] * DEPTH, [None] * DEPTH
    for i in reversed(range(DEPTH)):
        j = i // 2
        dh, ffn_g[i] = _ffn_backward(dh, ffn_p[i], ffn_a[i], f"ffn{i}")
        if i % 2 == 0:
            dh, mix_g[i] = _ssd_backward(dh, mix_p[i], mix_a[i], f"ssd{j}")
        else:
            dh, mix_g[i] = _fox_backward(dh, mix_p[i], mix_a[i], f"fox{j}")
    grad_x = dh[None]
    ssd_g, fox_g = [mix_g[0], mix_g[2]], [mix_g[1], mix_g[3]]

    per_layer = dict(
        ssd_w_in=[g["w_in"] for g in ssd_g], ssd_w_out=[g["w_out"] for g in ssd_g],
        fox_w_in=[g["w_in"] for g in fox_g], fox_w_out=[g["w_out"] for g in fox_g],
        ffn_w_up=[g["w_up"] for g in ffn_g], ffn_w_down=[g["w_down"] for g in ffn_g])
    slabs = [_to_slabs(dw, n in COL_SHARDED) for n in BIG for dw in per_layer[n]]
    finals = _reduce_scatter(slabs, cc, chip)
    groups, pos = [], 0
    for n in BIG:
        groups.append(finals[pos:pos + len(per_layer[n])])
        pos += len(per_layer[n])
    grads = {n: a.reshape(w[n].shape) for n, a in zip(BIG, _assemble_halves(groups, name="rs_result_swap"))}
    small = dict(
        mix_norm_g=jnp.concatenate([g["mix_g"] for g in mix_g], axis=0),
        ffn_norm_g=jnp.concatenate([g["ffn_g"] for g in ffn_g], axis=0),
        ssd_conv_w=jnp.stack([g["conv_w"] for g in ssd_g]), ssd_conv_b=jnp.stack([g["conv_b"] for g in ssd_g]),
        ssd_dt_bias=jnp.stack([g["dt_bias"].reshape(SSD_H) for g in ssd_g]),
        ssd_a_log=jnp.stack([g["a_log"].reshape(SSD_H) for g in ssd_g]),
        ssd_d=jnp.stack([g["d"].reshape(SSD_H) for g in ssd_g]),
        ssd_norm_g=jnp.concatenate([g["norm_g"] for g in ssd_g], axis=0),
        fox_b_f=jnp.stack([g["b_f"] for g in fox_g]), fox_q_norm_g=jnp.stack([g["gq"] for g in fox_g]),
        fox_k_norm_g=jnp.stack([g["gk"] for g in fox_g]),
        ffn_conv_w=jnp.stack([g["conv_w"] for g in ffn_g]), ffn_conv_b=jnp.stack([g["conv_b"] for g in ffn_g]),
        final_norm_g=d_final_g[0], loss=loss_part[0, :1])
    _, total = _allgather_small(_pack([small[n] for n, _ in SMALL]), name="reduce_small", with_sum=True)
    for (n, shp), val in zip(SMALL, _unpack(total, [shp for _, shp in SMALL])):
        grads[n] = val
    loss = grads.pop("loss")[0]
    grads["ssd_conv_w"] = lax.dynamic_slice_in_dim(grads["ssd_conv_w"], chip * ssd_conv_w.shape[2], ssd_conv_w.shape[2], axis=2)
    grads["ffn_conv_w"] = lax.dynamic_slice_in_dim(grads["ffn_conv_w"], chip * ffn_conv_w.shape[2], ffn_conv_w.shape[2], axis=2)

    deltas, new_m, new_v = {}, {}, {}
    for n in NAMES:
        shp = w[n].shape
        two_d = (1, shp[0]) if len(shp) == 1 else (-1, shp[-1])
        r2 = lambda a: a.reshape(two_d)
        d, nm, nv = _adamw(r2(w[n]), r2(grads[n]), r2(m_in[n]), r2(v_in[n]), name=f"adamw_{n}")
        deltas[n], new_m[n], new_v[n] = d.reshape(shp), nm.reshape(shp), nv.reshape(shp)
    return (loss, grad_x, *[grads[n] for n in NAMES], *[deltas[n] for n in NAMES], *[new_m[n] for n in NAMES],
            *[new_v[n] for n in NAMES])
```

```python
import functools

import jax
import jax.numpy as jnp
from jax import lax
from jax.experimental import pallas as pl
from jax.experimental.pallas import tpu as pltpu

F32 = jnp.float32
BF16 = jnp.bfloat16
HI = lax.Precision.HIGHEST
MESH = pl.DeviceIdType.MESH

D_MODEL = 1024
DEPTH = 4
EPS = 1e-6
SSD_DI = 2048
SSD_HD = 64
SSD_G = 4
SSD_HPG = 8
SSD_N = 128
SSD_K = 4
CHUNK = 128
SSD_CONV_DIM = 3072
SSD_ZX = SSD_DI + SSD_CONV_DIM
SSD_H = 32
FOX_HD = 64
FOX_H = 16
FOX_D = 1024
D_FF = 2816
FFN_K = 3
LANES = 128
VMEM_LIMIT = 56 * 1024 * 1024

ADAM_LR = 0.001
ADAM_B1 = 0.9
ADAM_B2 = 0.999
ADAM_EPS = 1e-08
ADAM_WD = 0.01
ADAM_STEP = 10

NN = (((1,), (0,)), ((), ()))
NT = (((1,), (1,)), ((), ()))
TN = (((0,), (0,)), ((), ()))


def _pick(n, cap, mult=LANES):
    best = None
    for t in range(mult, min(n, cap) + 1, mult):
        if n % t == 0:
            best = t
    return best if best is not None else n


def _cp(sem):
    return pltpu.CompilerParams(dimension_semantics=sem, vmem_limit_bytes=VMEM_LIMIT)


def _sigmoid(x):
    return jax.nn.sigmoid(x)


def _silu(x):
    return x * _sigmoid(x)


def _dsilu(x):
    s = _sigmoid(x)
    return s * (1.0 + x * (1.0 - s))


def _softplus(x):
    e = jnp.exp(-jnp.abs(x))
    u = 1.0 + e
    l1p = jnp.where(u == 1.0, e, jnp.log(u) * (e / (u - 1.0)))
    return jnp.maximum(x, 0.0) + l1p


def _dotf(a, b, dn=NN, *, onehot="b", pieces=2):
    x, e = (a, b) if onehot == "b" else (b, a)
    e = e.astype(BF16)
    acc = None
    for n in range(pieces):
        hi = x.astype(BF16)
        part = lax.dot_general(hi, e, dn, preferred_element_type=F32) if onehot == "b" else \
            lax.dot_general(e, hi, dn, preferred_element_type=F32)
        acc = part if acc is None else acc + part
        if n + 1 < pieces:
            x = x - hi.astype(F32)
    return acc


def _dotb(a, b, dn=NN):
    return lax.dot_general(a.astype(BF16), b.astype(BF16), dn, preferred_element_type=F32)


def _group_matrix(width, sub, transpose=False):
    ng = width // sub
    shape = (ng, width) if transpose else (width, ng)
    lane = lax.broadcasted_iota(jnp.int32, shape, 1 if transpose else 0)
    grp = lax.broadcasted_iota(jnp.int32, shape, 0 if transpose else 1)
    return (lane // sub == grp).astype(F32)


def _gmean(v, sub):
    width = v.shape[-1]
    if sub == width:
        return jnp.mean(v, axis=-1, keepdims=True)
    s = _dotf(v, _group_matrix(width, sub))
    return _dotf(s, _group_matrix(width, sub, transpose=True)) * (1.0 / sub)


def _matmul(a, b, *, mode, name, out_dtype=F32, add=None):
    if mode == "nn":
        (m, k), (k2, n) = a.shape, b.shape
    elif mode == "nt":
        (m, k), (n, k2) = a.shape, b.shape
    else:
        (k, m), (k2, n) = a.shape, b.shape
    assert k == k2, (a.shape, b.shape, mode)
    tm, tn, tk = _pick(m, 1024), _pick(n, 1536), _pick(k, 1536)
    nk = k // tk
    dn = {"nn": NN, "nt": NT, "tn": TN}[mode]
    has_add = add is not None

    def body(*refs):
        if has_add:
            a_ref, b_ref, add_ref, o_ref, acc_ref = refs
        else:
            a_ref, b_ref, o_ref, acc_ref = refs
            add_ref = None
        kk = pl.program_id(2)
        part = _dotb(a_ref[...], b_ref[...], dn)

        def finish(r):
            if has_add:
                r = r + add_ref[...]
            o_ref[...] = r.astype(out_dtype)

        if nk == 1:
            finish(part)
        else:
            @pl.when(kk == 0)
            def _():
                acc_ref[...] = part

            @pl.when(kk > 0)
            def _():
                acc_ref[...] += part

            @pl.when(kk == nk - 1)
            def _():
                finish(acc_ref[...])

    if mode == "nn":
        a_spec = pl.BlockSpec((tm, tk), lambda i, j, q: (i, q))
        b_spec = pl.BlockSpec((tk, tn), lambda i, j, q: (q, j))
    elif mode == "nt":
        a_spec = pl.BlockSpec((tm, tk), lambda i, j, q: (i, q))
        b_spec = pl.BlockSpec((tn, tk), lambda i, j, q: (j, q))
    else:
        a_spec = pl.BlockSpec((tk, tm), lambda i, j, q: (q, i))
        b_spec = pl.BlockSpec((tk, tn), lambda i, j, q: (q, j))
    o_spec = pl.BlockSpec((tm, tn), lambda i, j, q: (i, j))
    in_specs = [a_spec, b_spec] + ([o_spec] if has_add else [])
    args = (a, b) + ((add,) if has_add else ())
    return pl.pallas_call(
        body, name=name, grid=(m // tm, n // tn, nk), in_specs=in_specs, out_specs=o_spec,
        out_shape=jax.ShapeDtypeStruct((m, n), out_dtype),
        scratch_shapes=[pltpu.VMEM((tm, tn) if nk > 1 else (8, LANES), F32)],
        compiler_params=_cp(("parallel", "parallel", "arbitrary")),
    )(*args)


def _rms_fwd(x, g, *, gw, ncol, name, x_col0=0, sub=None, z=None, z_col0=0, out_dtype=BF16):
    rows = x.shape[0]
    tr = _pick(rows, 512, 8)
    sub = gw if sub is None else sub
    gated = z is not None

    def body(*refs):
        if gated:
            x_ref, z_ref, g_ref, o_ref = refs
            xv = x_ref[...] * _silu(z_ref[...])
        else:
            x_ref, g_ref, o_ref = refs
            xv = x_ref[...]
        r = lax.rsqrt(_gmean(xv * xv, sub) + EPS)
        o_ref[...] = (xv * r * g_ref[...]).astype(out_dtype)

    specs = [pl.BlockSpec((tr, gw), lambda j, i: (i, x_col0 + j))]
    args = [x]
    if gated:
        specs.append(pl.BlockSpec((tr, gw), lambda j, i: (i, z_col0 + j)))
        args.append(z)
    specs.append(pl.BlockSpec((1, gw), lambda j, i: (0, j)))
    args.append(g)
    return pl.pallas_call(
        body, name=name, grid=(ncol, rows // tr), in_specs=specs,
        out_specs=pl.BlockSpec((tr, gw), lambda j, i: (i, j)),
        out_shape=jax.ShapeDtypeStruct((rows, gw * ncol), out_dtype),
        compiler_params=_cp(("parallel", "parallel")),
    )(*args)


def _rms_bwd(x, g, dy, *, gw, ncol, name, x_col0=0, sub=None, z=None, z_col0=0, add=None, dx_dtype=F32):
    rows = x.shape[0]
    tr = _pick(rows, 512, 8)
    sub = gw if sub is None else sub
    gated = z is not None
    has_add = add is not None

    def body(*refs):
        refs = list(refs)
        x_ref = refs.pop(0)
        z_ref = refs.pop(0) if gated else None
        g_ref = refs.pop(0)
        dy_ref = refs.pop(0)
        add_ref = refs.pop(0) if has_add else None
        dx_ref = refs.pop(0)
        dz_ref = refs.pop(0) if gated else None
        dg_ref = refs.pop(0)
        i = pl.program_id(1)
        xv = x_ref[...]
        if gated:
            zz = z_ref[...]
            yz = xv * _silu(zz)
        else:
            yz = xv
        r = lax.rsqrt(_gmean(yz * yz, sub) + EPS)
        xh = yz * r
        dy = dy_ref[...].astype(F32)
        dyg = dy * g_ref[...]
        d_yz = r * (dyg - xh * _gmean(dyg * xh, sub))
        if gated:
            dx_ref[...] = (d_yz * _silu(zz)).astype(dx_dtype)
            dz_ref[...] = (d_yz * xv * _dsilu(zz)).astype(dx_dtype)
        elif has_add:
            dx_ref[...] = (d_yz + add_ref[...]).astype(dx_dtype)
        else:
            dx_ref[...] = d_yz.astype(dx_dtype)
        part = jnp.sum(dy * xh, axis=0, keepdims=True)

        @pl.when(i == 0)
        def _():
            dg_ref[...] = part

        @pl.when(i > 0)
        def _():
            dg_ref[...] += part

    tile = pl.BlockSpec((tr, gw), lambda j, i: (i, j))
    specs = [pl.BlockSpec((tr, gw), lambda j, i: (i, x_col0 + j))]
    args = [x]
    if gated:
        specs.append(pl.BlockSpec((tr, gw), lambda j, i: (i, z_col0 + j)))
        args.append(z)
    specs += [pl.BlockSpec((1, gw), lambda j, i: (0, j)), tile]
    args += [g, dy]
    if has_add:
        specs.append(tile)
        args.append(add)
    width = gw * ncol
    out_shape = [jax.ShapeDtypeStruct((rows, width), dx_dtype)]
    out_specs = [tile]
    if gated:
        out_shape.append(jax.ShapeDtypeStruct((rows, width), dx_dtype))
        out_specs.append(tile)
    out_shape.append(jax.ShapeDtypeStruct((1, width), F32))
    out_specs.append(pl.BlockSpec((1, gw), lambda j, i: (0, j)))
    return pl.pallas_call(
        body, name=name, grid=(ncol, rows // tr), in_specs=specs, out_specs=out_specs, out_shape=out_shape,
        compiler_params=_cp(("parallel", "arbitrary")),
    )(*args)


HALO = 8


def _conv_rows(tc):
    return 16 * 8 * LANES // tc


def _conv_fwd(u, w8, b, *, kw, width, name, u_col0=0, mul_col0=None, out_dtype=F32):
    rows = u.shape[0]
    ts = _pick(rows, 512, 8)
    tc = _pick(width, 512)
    gated = mul_col0 is not None
    c0 = u_col0 // tc
    m0 = (mul_col0 // tc) if gated else 0
    assert u_col0 % tc == 0 and (not gated or mul_col0 % tc == 0)

    def body(*refs):
        if gated:
            cur_ref, halo_ref, mul_ref, w_ref, b_ref, o_ref, ext = refs
        else:
            cur_ref, halo_ref, w_ref, b_ref, o_ref, ext = refs
        i = pl.program_id(0)
        ext[pl.ds(0, HALO), :] = jnp.where(i == 0, 0.0, halo_ref[...])
        ext[pl.ds(HALO, ts), :] = cur_ref[...]
        bias = b_ref[...]
        taps = [w_ref[k:k + 1, :] for k in range(kw)]
        rb = _conv_rows(tc)
        for r0 in range(0, ts, rb):
            pre = bias + taps[0] * ext[pl.ds(r0 + HALO - (kw - 1), rb), :]
            for k in range(1, kw):
                pre = pre + taps[k] * ext[pl.ds(r0 + HALO - (kw - 1) + k, rb), :]
            act = _silu(pre)
            if gated:
                act = act * mul_ref[pl.ds(r0, rb), :]
            o_ref[pl.ds(r0, rb), :] = act.astype(out_dtype)

    hb = ts // HALO
    specs = [pl.BlockSpec((ts, tc), lambda i, j: (i, c0 + j)),
             pl.BlockSpec((HALO, tc), lambda i, j: (jnp.maximum(i * hb - 1, 0), c0 + j))]
    args = [u, u]
    if gated:
        specs.append(pl.BlockSpec((ts, tc), lambda i, j: (i, m0 + j)))
        args.append(u)
    specs += [pl.BlockSpec((8, tc), lambda i, j: (0, j)), pl.BlockSpec((1, tc), lambda i, j: (0, j))]
    args += [w8, b]
    return pl.pallas_call(
        body, name=name, grid=(rows // ts, width // tc), in_specs=specs,
        out_specs=pl.BlockSpec((ts, tc), lambda i, j: (i, j)),
        out_shape=jax.ShapeDtypeStruct((rows, width), out_dtype),
        scratch_shapes=[pltpu.VMEM((ts + HALO, tc), F32)],
        compiler_params=_cp(("parallel", "parallel")),
    )(*args)


def _conv_bwd(u, w8, b, dact, *, kw, width, name, u_col0=0, mul_col0=None, du_dtype=BF16):
    rows = u.shape[0]
    ts = _pick(rows, 512, 8)
    tc = _pick(width, 512)
    gated = mul_col0 is not None
    c0 = u_col0 // tc
    m0 = (mul_col0 // tc) if gated else 0
    nt = rows // ts
    hb = ts // HALO

    def body(*refs):
        refs = list(refs)
        cur_ref, halo_ref = refs.pop(0), refs.pop(0)
        mul_ref = refs.pop(0) if gated else None
        w_ref, b_ref, da_ref = refs.pop(0), refs.pop(0), refs.pop(0)
        du_ref = refs.pop(0)
        dmul_ref = refs.pop(0) if gated else None
        dwb_ref, ext_u, ext_d = refs
        t = pl.program_id(1)
        ti = nt - 1 - t
        ext_u[pl.ds(0, HALO), :] = jnp.where(ti == 0, 0.0, halo_ref[...])
        ext_u[pl.ds(HALO, ts), :] = cur_ref[...]

        @pl.when(t == 0)
        def _():
            ext_d[pl.ds(ts, HALO), :] = jnp.zeros((HALO, tc), F32)
            dwb_ref[...] = jnp.zeros((8, tc), F32)

        bias = b_ref[...]
        taps = [w_ref[k:k + 1, :] for k in range(kw)]
        rb = _conv_rows(tc)
        dw_acc = [jnp.zeros((1, tc), F32) for _ in range(kw)]
        db_acc = jnp.zeros((1, tc), F32)
        for r0 in reversed(range(0, ts, rb)):
            shifted = [ext_u[pl.ds(r0 + HALO - (kw - 1) + k, rb), :] for k in range(kw)]
            pre = bias + taps[0] * shifted[0]
            for k in range(1, kw):
                pre = pre + taps[k] * shifted[k]
            sg = _sigmoid(pre)
            dsilu = sg * (1.0 + pre * (1.0 - sg))
            da = da_ref[pl.ds(r0, rb), :].astype(F32)
            if gated:
                dmul_ref[pl.ds(r0, rb), :] = (da * (pre * sg)).astype(du_dtype)
                dgp = da * mul_ref[pl.ds(r0, rb), :] * dsilu
            else:
                dgp = da * dsilu
            ext_d[pl.ds(r0, rb), :] = dgp
            du = taps[kw - 1] * dgp
            for k in range(kw - 1):
                du = du + taps[k] * ext_d[pl.ds(r0 + kw - 1 - k, rb), :]
            du_ref[pl.ds(r0, rb), :] = du.astype(du_dtype)
            for k in range(kw):
                dw_acc[k] = dw_acc[k] + jnp.sum(dgp * shifted[k], axis=0, keepdims=True)
            db_acc = db_acc + jnp.sum(dgp, axis=0, keepdims=True)
        for k in range(kw):
            dwb_ref[k:k + 1, :] += dw_acc[k]
        dwb_ref[7:8, :] += db_acc
        ext_d[pl.ds(ts, HALO), :] = ext_d[pl.ds(0, HALO), :]

    specs = [pl.BlockSpec((ts, tc), lambda j, t: (nt - 1 - t, c0 + j)),
             pl.BlockSpec((HALO, tc), lambda j, t: (jnp.maximum((nt - 1 - t) * hb - 1, 0), c0 + j))]
    args = [u, u]
    if gated:
        specs.append(pl.BlockSpec((ts, tc), lambda j, t: (nt - 1 - t, m0 + j)))
        args.append(u)
    tile = pl.BlockSpec((ts, tc), lambda j, t: (nt - 1 - t, j))
    specs += [pl.BlockSpec((8, tc), lambda j, t: (0, j)), pl.BlockSpec((1, tc), lambda j, t: (0, j)), tile]
    args += [w8, b, dact]
    out_shape = [jax.ShapeDtypeStruct((rows, width), du_dtype)]
    out_specs = [tile]
    if gated:
        out_shape.append(jax.ShapeDtypeStruct((rows, width), du_dtype))
        out_specs.append(tile)
    out_shape.append(jax.ShapeDtypeStruct((8, width), F32))
    out_specs.append(pl.BlockSpec((8, tc), lambda j, t: (0, j)))
    return pl.pallas_call(
        body, name=name, grid=(width // tc, nt), in_specs=specs, out_specs=out_specs, out_shape=out_shape,
        scratch_shapes=[pltpu.VMEM((ts + HALO, tc), F32), pltpu.VMEM((ts + HALO, tc), F32)],
        compiler_params=_cp(("parallel", "arbitrary")),
    )(*args)


GW = SSD_HPG * SSD_HD


def _ssd_common(x, bm, cm, dt_raw, dt_raw_t, bias_r, bias_c, alog_r, alog_c):
    row = lax.broadcasted_iota(jnp.int32, (CHUNK, CHUNK), 0)
    col = lax.broadcasted_iota(jnp.int32, (CHUNK, CHUNK), 1)
    causal = row >= col
    tril = causal.astype(F32)
    triu = (row <= col).astype(F32)
    spread = _group_matrix(GW, SSD_HD, transpose=True)
    dt = _softplus(dt_raw + bias_r)
    dt_t = _softplus(dt_raw_t + bias_c)
    a_r = -jnp.exp(alog_r)
    a_c = -jnp.exp(alog_c)
    acs = _dotf(tril, dt * a_r, onehot="a", pieces=3)
    acs_t = _dotf(dt_t * a_c, triu, pieces=3)
    last = acs[CHUNK - 1:CHUNK, :]
    ds = jnp.exp(last - acs)
    cd = jnp.exp(last)
    c = dict(causal=causal, tril=tril, triu=triu, spread=spread, dt=dt, a_r=a_r, acs=acs, acs_t=acs_t, ds=ds, cd=cd)
    c["eb"] = _dotf(jnp.exp(acs), spread)
    c["dsb"] = _dotf(ds, spread)
    c["cdb"] = _dotf(cd, spread)
    c["dtb"] = _dotf(dt, spread)
    c["xdt"] = x * c["dtb"]
    c["cb"] = _dotb(cm, bm, NT)
    return c


def _ssd_lam(c, r):
    diff = c["acs"][:, r:r + 1] - c["acs_t"][r:r + 1, :]
    return jnp.exp(jnp.where(c["causal"], diff, -jnp.inf))


def _ssd_specs(nc, rev):
    def ci(t):
        return (nc - 1 - t) if rev else t
    xs = pl.BlockSpec((CHUNK, GW), lambda g, t: (ci(t), g))
    bs = pl.BlockSpec((CHUNK, SSD_N), lambda g, t: (ci(t), SSD_DI // SSD_N + g))
    cs = pl.BlockSpec((CHUNK, SSD_N), lambda g, t: (ci(t), SSD_DI // SSD_N + SSD_G + g))
    dts = pl.BlockSpec((1, CHUNK, 8), lambda g, t: (g, ci(t), 0))
    dtts = pl.BlockSpec((1, 8, CHUNK), lambda g, t: (g, 0, ci(t)))
    pr = pl.BlockSpec((1, 1, 8), lambda g, t: (g, 0, 0))
    pc = pl.BlockSpec((1, 8, 1), lambda g, t: (g, 0, 0))
    hs = pl.BlockSpec((1, 1, SSD_N, GW), lambda g, t: (ci(t), g, 0, 0))
    return xs, bs, cs, dts, dtts, pr, pc, hs


def _ssd_fwd(xbc, dtg, dtg_t, bias_r, bias_c, alog_r, alog_c, d_r, *, name):
    s = xbc.shape[0]
    nc = s // CHUNK
    xs, bs, cs, dts, dtts, pr, pc, hs = _ssd_specs(nc, False)

    def body(x_ref, b_ref, c_ref, dt_ref, dtt_ref, br_ref, bc_ref, ar_ref, ac_ref, d_ref, y_ref, hp_ref, h_sc):
        t = pl.program_id(1)

        @pl.when(t == 0)
        def _():
            h_sc[...] = jnp.zeros_like(h_sc)

        x, bm, cm = x_ref[...], b_ref[...], c_ref[...]
        c = _ssd_common(x, bm, cm, dt_ref[0], dtt_ref[0], br_ref[0], bc_ref[0], ar_ref[0], ac_ref[0])
        h = h_sc[...]
        hp_ref[0, 0] = h
        xdt = c["xdt"]
        pieces = []
        for r in range(SSD_HPG):
            m = c["cb"] * _ssd_lam(c, r)
            pieces.append(_dotb(m, xdt[:, r * SSD_HD:(r + 1) * SSD_HD]))
        y = jnp.concatenate(pieces, axis=1) + c["eb"] * _dotb(cm, h) + x * _dotf(d_ref[0], c["spread"])
        y_ref[...] = y
        h_sc[...] = h * c["cdb"] + _dotb(bm, xdt * c["dsb"], TN)

    return pl.pallas_call(
        body, name=name, grid=(SSD_G, nc),
        in_specs=[xs, bs, cs, dts, dtts, pr, pc, pr, pc, pr],
        out_specs=[xs, hs],
        out_shape=[jax.ShapeDtypeStruct((s, SSD_DI), F32), jax.ShapeDtypeStruct((nc, SSD_G, SSD_N, GW), F32)],
        scratch_shapes=[pltpu.VMEM((SSD_N, GW), F32)],
        compiler_params=_cp(("parallel", "arbitrary")),
    )(xbc, xbc, xbc, dtg, dtg_t, bias_r, bias_c, alog_r, alog_c, d_r)


def _ssd_bwd(xbc, dtg, dtg_t, bias_r, bias_c, alog_r, alog_c, d_r, hprev, dy, *, name):
    s = xbc.shape[0]
    nc = s // CHUNK
    xs, bs, cs, dts, dtts, pr, pc, hs = _ssd_specs(nc, True)
    gsum = functools.partial(_group_matrix, GW, SSD_HD)

    def body(x_ref, b_ref, c_ref, dt_ref, dtt_ref, br_ref, bc_ref, ar_ref, ac_ref, d_ref, hp_ref, dy_ref,
             dx_ref, db_ref, dc_ref, ddt_ref, dbias_ref, dalog_ref, dd_ref, dh_sc):
        t = pl.program_id(1)

        @pl.when(t == 0)
        def _():
            dh_sc[...] = jnp.zeros_like(dh_sc)
            dbias_ref[...] = jnp.zeros_like(dbias_ref)
            dalog_ref[...] = jnp.zeros_like(dalog_ref)
            dd_ref[...] = jnp.zeros_like(dd_ref)

        x, bm, cm = x_ref[...], b_ref[...], c_ref[...]
        c = _ssd_common(x, bm, cm, dt_ref[0], dtt_ref[0], br_ref[0], bc_ref[0], ar_ref[0], ac_ref[0])
        lanesum = gsum()
        h = hp_ref[0, 0]
        dh = dh_sc[...]
        dy = dy_ref[...]
        xdt, dsb = c["xdt"], c["dsb"]
        skip = _dotf(d_ref[0], c["spread"])
        dd_ref[0] += jnp.sum(_dotf(dy * x, lanesum), axis=0, keepdims=True)
        dacs = _dotf(dy * (c["eb"] * _dotb(cm, h)), lanesum)
        edy = c["eb"] * dy
        dcm = _dotb(edy, h, NT)
        dh_prev = _dotb(cm, edy, TN)
        bdh = _dotb(bm, dh)
        dxdt = dsb * bdh
        dbm = _dotb(dsb * xdt, dh, NT)
        t1 = _dotf(xdt * bdh, lanesum) * c["ds"]
        dacs = dacs - t1
        dlast = jnp.sum(t1, axis=0, keepdims=True) + jnp.sum(_dotf(dh * h, lanesum), axis=0, keepdims=True) * c["cd"]
        dcb = jnp.zeros((CHUNK, CHUNK), F32)
        pieces = []
        ones8 = jnp.ones((CHUNK, 8), F32)
        head = lax.broadcasted_iota(jnp.int32, (1, 8), 1)
        for r in range(SSD_HPG):
            sl = slice(r * SSD_HD, (r + 1) * SSD_HD)
            lam = _ssd_lam(c, r)
            m = c["cb"] * lam
            dm = _dotb(dy[:, sl], xdt[:, sl], NT)
            dcb = dcb + dm * lam
            gm = dm * m
            dacs = dacs + (jnp.sum(gm, axis=1, keepdims=True) - _dotf(gm, ones8, TN, pieces=3)) * (head == r).astype(F32)
            pieces.append(_dotb(m, dy[:, sl], TN))
        dxdt = dxdt + jnp.concatenate(pieces, axis=1)
        dcm = dcm + _dotb(dcb, bm)
        dbm = dbm + _dotb(dcb, cm, TN)
        dx_ref[...] = dy * skip + dxdt * c["dtb"]
        db_ref[...] = dbm
        dc_ref[...] = dcm
        rowid = lax.broadcasted_iota(jnp.int32, (CHUNK, 8), 0)
        dacs = dacs + jnp.where(rowid == CHUNK - 1, dlast, 0.0)
        dda = _dotf(c["triu"], dacs, onehot="a", pieces=3)
        ddt = _dotf(dxdt * x, lanesum) + dda * c["a_r"]
        ddt_raw = ddt * _sigmoid(dt_ref[0] + br_ref[0])
        ddt_ref[0] = ddt_raw
        dbias_ref[0] += jnp.sum(ddt_raw, axis=0, keepdims=True)
        dalog_ref[0] += jnp.sum(dda * c["dt"], axis=0, keepdims=True) * c["a_r"]
        dh_sc[...] = dh_prev + dh * c["cdb"]

    ci = lambda t: nc - 1 - t
    nspec = pl.BlockSpec((CHUNK, SSD_N), lambda g, t: (ci(t), g))
    return pl.pallas_call(
        body, name=name, grid=(SSD_G, nc),
        in_specs=[xs, bs, cs, dts, dtts, pr, pc, pr, pc, pr, hs, xs],
        out_specs=[xs, nspec, nspec, dts, pr, pr, pr],
        out_shape=[jax.ShapeDtypeStruct((s, SSD_DI), F32), jax.ShapeDtypeStruct((s, SSD_G * SSD_N), F32),
                   jax.ShapeDtypeStruct((s, SSD_G * SSD_N), F32), jax.ShapeDtypeStruct((SSD_G, s, 8), F32),
                   jax.ShapeDtypeStruct((SSD_G, 1, 8), F32), jax.ShapeDtypeStruct((SSD_G, 1, 8), F32),
                   jax.ShapeDtypeStruct((SSD_G, 1, 8), F32)],
        scratch_shapes=[pltpu.VMEM((SSD_N, GW), F32)],
        compiler_params=_cp(("parallel", "arbitrary")),
    )(xbc, xbc, xbc, dtg, dtg_t, bias_r, bias_c, alog_r, alog_c, d_r, hprev, dy)


FOX_PAIRS = FOX_H // 2
FOX_SCALE = FOX_HD ** -0.5
NEG_INF = -jnp.inf


def _fgate_fwd(f_t, b_c, *, name):
    hh, s = f_t.shape
    tb = _pick(s, 512)
    nb = s // tb

    def body(f_ref, b_ref, o_ref, carry):
        t = pl.program_id(0)

        @pl.when(t == 0)
        def _():
            carry[...] = jnp.zeros_like(carry)

        lf = -_softplus(-(f_ref[...] + b_ref[...]))
        row = lax.broadcasted_iota(jnp.int32, (tb, tb), 0)
        col = lax.broadcasted_iota(jnp.int32, (tb, tb), 1)
        cum = _dotf(lf, (row <= col).astype(F32), pieces=3) + carry[:, 0:1]
        o_ref[...] = cum
        carry[:, 0:1] = cum[:, tb - 1:tb]

    return pl.pallas_call(
        body, name=name, grid=(nb,),
        in_specs=[pl.BlockSpec((hh, tb), lambda t: (0, t)), pl.BlockSpec((hh, 1), lambda t: (0, 0))],
        out_specs=pl.BlockSpec((hh, tb), lambda t: (0, t)),
        out_shape=jax.ShapeDtypeStruct((hh, s), F32),
        scratch_shapes=[pltpu.VMEM((hh, LANES), F32)],
        compiler_params=_cp(("arbitrary",)),
    )(f_t, b_c)


def _fgate_bwd(dcum_q_t, dcum_k_t, f_t, b_c, *, name):
    hh, s = f_t.shape
    tb = _pick(s, 512)
    nb = s // tb

    def body(dq_ref, d_ref, f_ref, b_ref, df_ref, db_ref, carry):
        t = pl.program_id(0)

        @pl.when(t == 0)
        def _():
            carry[...] = jnp.zeros_like(carry)
            db_ref[...] = jnp.zeros_like(db_ref)

        d = d_ref[...] + dq_ref[...]
        row = lax.broadcasted_iota(jnp.int32, (tb, tb), 0)
        col = lax.broadcasted_iota(jnp.int32, (tb, tb), 1)
        rev = _dotf(d, (row >= col).astype(F32), pieces=3) + carry[:, 0:1]
        df = rev * _sigmoid(-(f_ref[...] + b_ref[...]))
        df_ref[...] = df
        db_ref[...] += jnp.sum(df, axis=1, keepdims=True)
        carry[:, 0:1] = rev[:, 0:1]

    blk = pl.BlockSpec((hh, tb), lambda t: (0, nb - 1 - t))
    return pl.pallas_call(
        body, name=name, grid=(nb,),
        in_specs=[blk, blk, blk, pl.BlockSpec((hh, 1), lambda t: (0, 0))],
        out_specs=[blk, pl.BlockSpec((hh, 1), lambda t: (0, 0))],
        out_shape=[jax.ShapeDtypeStruct((hh, s), F32), jax.ShapeDtypeStruct((hh, 1), F32)],
        scratch_shapes=[pltpu.VMEM((hh, LANES), F32)],
        compiler_params=_cp(("arbitrary",)),
    )(dcum_q_t, dcum_k_t, f_t, b_c)


def _fox_tile(s):
    return min(512, max(s // 2, 8))


def _tri_tables(nq, kv_major):
    if kv_major:
        pairs = [(i, j) for j in range(nq) for i in range(j, nq)]
    else:
        pairs = [(i, j) for i in range(nq) for j in range(i + 1)]
    return (jnp.asarray([p[0] for p in pairs], jnp.int32), jnp.asarray([p[1] for p in pairs], jnp.int32))


def _lane_tile(col, width):
    return col if width == LANES else jnp.tile(col, (1, width // LANES))


def _flash_fwd(qs, kn, qkvg, ck, *, name):
    s = qs.shape[0]
    tt = _fox_tile(s)
    nq = s // tt
    itab, jtab = _tri_tables(nq, kv_major=False)
    v0 = 2 * FOX_D // LANES

    def body(itab_ref, jtab_ref, q_ref, k_ref, v_ref, ck_ref, o_ref, lse_ref, m_sc, l_sc, acc_sc):
        t = pl.program_id(1)
        i, j = itab_ref[t], jtab_ref[t]

        @pl.when(j == 0)
        def _():
            m_sc[...] = jnp.full_like(m_sc, NEG_INF)
            l_sc[...] = jnp.zeros_like(l_sc)
            acc_sc[...] = jnp.zeros_like(acc_sc)

        low = lax.broadcasted_iota(jnp.int32, (tt, LANES), 1) < FOX_HD

        def step(diagonal):
            q2, k2 = q_ref[...], k_ref[...]
            v2 = v_ref[...].astype(BF16)
            alphas, outs = [], []
            for hh in range(2):
                qh = jnp.where(low if hh == 0 else jnp.logical_not(low), q2, jnp.zeros_like(q2))
                sc = lax.dot_general(qh, k2, NT, preferred_element_type=F32) - ck_ref[0][hh:hh + 1, :]
                if diagonal:
                    row = lax.broadcasted_iota(jnp.int32, sc.shape, 0)
                    col = lax.broadcasted_iota(jnp.int32, sc.shape, 1)
                    sc = jnp.where(row >= col, sc, NEG_INF)
                m_prev = m_sc[hh]
                m_new = jnp.maximum(m_prev, jnp.max(sc, axis=1, keepdims=True))
                alpha = jnp.exp(m_prev - m_new)
                p = jnp.exp(sc - _lane_tile(m_new, tt))
                l_sc[hh] = alpha * l_sc[hh] + jnp.sum(p, axis=1, keepdims=True)
                m_sc[hh] = m_new
                alphas.append(alpha)
                outs.append(lax.dot_general(p.astype(BF16), v2, NN, preferred_element_type=F32))
            acc_sc[...] = jnp.where(low, alphas[0], alphas[1]) * acc_sc[...] + jnp.where(low, outs[0], outs[1])

        @pl.when(j < i)
        def _():
            step(False)

        @pl.when(j == i)
        def _():
            step(True)
            o_ref[...] = acc_sc[...] / jnp.where(low, l_sc[0], l_sc[1])
            lse_ref[0] = jnp.concatenate([m_sc[hh][:, 0:1] + jnp.log(l_sc[hh][:, 0:1]) for hh in range(2)], axis=1)

    return pl.pallas_call(
        body, name=name,
        grid_spec=pltpu.PrefetchScalarGridSpec(
            num_scalar_prefetch=2, grid=(FOX_PAIRS, itab.shape[0]),
            in_specs=[pl.BlockSpec((tt, LANES), lambda p, t, it, jt: (it[t], p)),
                      pl.BlockSpec((tt, LANES), lambda p, t, it, jt: (jt[t], p)),
                      pl.BlockSpec((tt, LANES), lambda p, t, it, jt: (jt[t], v0 + p)),
                      pl.BlockSpec((1, 2, tt), lambda p, t, it, jt: (p, 0, jt[t]))],
            out_specs=[pl.BlockSpec((tt, LANES), lambda p, t, it, jt: (it[t], p)),
                       pl.BlockSpec((1, tt, 2), lambda p, t, it, jt: (p, it[t], 0))],
            scratch_shapes=[pltpu.VMEM((2, tt, LANES), F32), pltpu.VMEM((2, tt, LANES), F32), pltpu.VMEM((tt, LANES), F32)]),
        out_shape=[jax.ShapeDtypeStruct((s, FOX_D), F32), jax.ShapeDtypeStruct((FOX_PAIRS, s, 2), F32)],
        compiler_params=_cp(("parallel", "arbitrary")),
    )(itab, jtab, qs, kn, qkvg, ck)


def _flash_bwd(qs, kn, qkvg, do, lse, delta, ck, *, name):
    s = qs.shape[0]
    tt = _fox_tile(s)
    nq = s // tt
    itab, jtab = _tri_tables(nq, kv_major=True)
    nsteps = itab.shape[0]
    v0 = 2 * FOX_D // LANES

    def body(itab_ref, jtab_ref, q_ref, k_ref, v_ref, do_ref, lse_ref, dl_ref, ck_ref,
             dq_ref, dk_ref, dv_ref, dcq_ref, dck_ref, dq_sc, rs_sc, dk_sc, dv_sc, dc_sc):
        t = pl.program_id(1)
        i, j = itab_ref[t], jtab_ref[t]

        @pl.when(t == 0)
        def _():
            dq_sc[...] = jnp.zeros_like(dq_sc)
            rs_sc[...] = jnp.zeros_like(rs_sc)

        @pl.when(i == j)
        def _():
            dk_sc[...] = jnp.zeros_like(dk_sc)
            dv_sc[...] = jnp.zeros_like(dv_sc)
            dc_sc[...] = jnp.zeros_like(dc_sc)

        low = lax.broadcasted_iota(jnp.int32, (tt, LANES), 1) < FOX_HD
        rows = pl.ds(pl.multiple_of(i * tt, tt), tt)

        def step(diagonal):
            q2, k2 = q_ref[...], k_ref[...]
            v2 = v_ref[...].astype(BF16)
            do2 = do_ref[...].astype(BF16)
            dqs, dks, dvs = [], [], []
            for hh in range(2):
                sel = low if hh == 0 else jnp.logical_not(low)
                qh = jnp.where(sel, q2, jnp.zeros_like(q2))
                doh = jnp.where(sel, do2, jnp.zeros_like(do2))
                sc = lax.dot_general(qh, k2, NT, preferred_element_type=F32) - ck_ref[0][hh:hh + 1, :]
                if diagonal:
                    row = lax.broadcasted_iota(jnp.int32, sc.shape, 0)
                    col = lax.broadcasted_iota(jnp.int32, sc.shape, 1)
                    sc = jnp.where(row >= col, sc, NEG_INF)
                lse_b = jnp.broadcast_to(lse_ref[0][:, hh:hh + 1], (tt, LANES))
                dl_b = jnp.broadcast_to(dl_ref[0][:, hh:hh + 1], (tt, LANES))
                p = jnp.exp(sc - _lane_tile(lse_b, tt))
                dp = lax.dot_general(doh, v2, NT, preferred_element_type=F32)
                ds = p * (dp - _lane_tile(dl_b, tt))
                pb, dsb = p.astype(BF16), ds.astype(BF16)
                dvs.append(lax.dot_general(pb, do2, TN, preferred_element_type=F32))
                dks.append(lax.dot_general(dsb, q2, TN, preferred_element_type=F32))
                dqs.append(lax.dot_general(dsb, k2, NN, preferred_element_type=F32))
                rs_sc[hh, rows, :] += jnp.sum(ds, axis=1, keepdims=True)
                dc_sc[hh] += jnp.sum(ds, axis=0, keepdims=True)
            dv_sc[...] += jnp.where(low, dvs[0], dvs[1])
            dk_sc[...] += jnp.where(low, dks[0], dks[1])
            dq_sc[rows, :] += jnp.where(low, dqs[0], dqs[1])

        @pl.when(j < i)
        def _():
            step(False)

        @pl.when(j == i)
        def _():
            step(True)

        @pl.when(i == nq - 1)
        def _():
            dk_ref[...] = dk_sc[...]
            dv_ref[...] = dv_sc[...]
            dck_ref[0] = -jnp.concatenate([dc_sc[hh] for hh in range(2)], axis=0)

        @pl.when(t == nsteps - 1)
        def _():
            dq_ref[...] = dq_sc[...] * FOX_SCALE
            dcq_ref[0] = jnp.concatenate([rs_sc[hh] for hh in range(2)], axis=1)

    qside = pl.BlockSpec((tt, LANES), lambda p, t, it, jt: (it[t], p))
    kside = pl.BlockSpec((tt, LANES), lambda p, t, it, jt: (jt[t], p))
    stat = pl.BlockSpec((1, tt, 2), lambda p, t, it, jt: (p, it[t], 0))
    ckspec = pl.BlockSpec((1, 2, tt), lambda p, t, it, jt: (p, 0, jt[t]))
    return pl.pallas_call(
        body, name=name,
        grid_spec=pltpu.PrefetchScalarGridSpec(
            num_scalar_prefetch=2, grid=(FOX_PAIRS, nsteps),
            in_specs=[qside, kside, pl.BlockSpec((tt, LANES), lambda p, t, it, jt: (jt[t], v0 + p)), qside, stat, stat, ckspec],
            out_specs=[pl.BlockSpec((s, LANES), lambda p, t, it, jt: (0, p)), kside, kside,
                       pl.BlockSpec((1, s, 2), lambda p, t, it, jt: (p, 0, 0)), ckspec],
            scratch_shapes=[pltpu.VMEM((s, LANES), F32), pltpu.VMEM((2, s, 1), F32), pltpu.VMEM((tt, LANES), F32),
                            pltpu.VMEM((tt, LANES), F32), pltpu.VMEM((2, 1, tt), F32)]),
        out_shape=[jax.ShapeDtypeStruct((s, FOX_D), F32), jax.ShapeDtypeStruct((s, FOX_D), F32),
                   jax.ShapeDtypeStruct((s, FOX_D), F32), jax.ShapeDtypeStruct((FOX_PAIRS, s, 2), F32),
                   jax.ShapeDtypeStruct((FOX_PAIRS, 2, s), F32)],
        compiler_params=_cp(("parallel", "arbitrary")),
    )(itab, jtab, qs, kn, qkvg, do, lse, delta, ck)


def _ogate_fwd(o, qkvg, *, name):
    s = o.shape[0]
    tr = _pick(s, 512, 8)

    def body(o_ref, g_ref, out_ref):
        out_ref[...] = (o_ref[...] * _sigmoid(g_ref[...])).astype(BF16)

    tile = pl.BlockSpec((tr, FOX_D), lambda i: (i, 0))
    return pl.pallas_call(
        body, name=name, grid=(s // tr,), in_specs=[tile, pl.BlockSpec((tr, FOX_D), lambda i: (i, 3))],
        out_specs=tile, out_shape=jax.ShapeDtypeStruct((s, FOX_D), BF16), compiler_params=_cp(("parallel",)),
    )(o, qkvg)


def _ogate_bwd(dog, o, qkvg, *, name):
    s = o.shape[0]
    tr = _pick(s, 512, 8)

    def body(dog_ref, o_ref, g_ref, do_ref, dg_ref, dl_ref):
        sg = _sigmoid(g_ref[...])
        ov = o_ref[...]
        dog_v = dog_ref[...]
        do = dog_v * sg
        do_ref[...] = do
        dg_ref[...] = (dog_v * ov * sg * (1.0 - sg)).astype(BF16)
        dl_ref[...] = _dotf(do * ov, _group_matrix(FOX_D, FOX_HD))

    tile = pl.BlockSpec((tr, FOX_D), lambda i: (i, 0))
    return pl.pallas_call(
        body, name=name, grid=(s // tr,), in_specs=[tile, tile, pl.BlockSpec((tr, FOX_D), lambda i: (i, 3))],
        out_specs=[tile, tile, pl.BlockSpec((tr, FOX_H), lambda i: (i, 0))],
        out_shape=[jax.ShapeDtypeStruct((s, FOX_D), F32), jax.ShapeDtypeStruct((s, FOX_D), BF16),
                   jax.ShapeDtypeStruct((s, FOX_H), F32)],
        compiler_params=_cp(("parallel",)),
    )(dog, o, qkvg)


def _loss_head(h, g, target, *, name):
    s, d = h.shape
    tr = _pick(s, 512, 8)

    def body(h_ref, g_ref, t_ref, loss_ref, dh_ref, dg_ref):
        i = pl.program_id(0)
        x = h_ref[...]
        gv = g_ref[...]
        r = lax.rsqrt(jnp.mean(x * x, axis=-1, keepdims=True) + EPS)
        xh = x * r
        err = xh * gv - t_ref[...]
        part = 0.5 * jnp.sum(jnp.sum(err * err, axis=1, keepdims=True) * (1.0 / d), axis=0, keepdims=True)
        dy = err * (1.0 / d)
        dyg = dy * gv
        dh_ref[...] = r * (dyg - xh * jnp.mean(dyg * xh, axis=-1, keepdims=True))
        dgp = jnp.sum(dy * xh, axis=0, keepdims=True)

        @pl.when(i == 0)
        def _():
            loss_ref[...] = jnp.zeros_like(loss_ref) + part
            dg_ref[...] = dgp

        @pl.when(i > 0)
        def _():
            loss_ref[...] += part
            dg_ref[...] += dgp

    tile = pl.BlockSpec((tr, d), lambda i: (i, 0))
    vec = pl.BlockSpec((1, d), lambda i: (0, 0))
    return pl.pallas_call(
        body, name=name, grid=(s // tr,), in_specs=[tile, vec, tile],
        out_specs=[pl.BlockSpec((1, LANES), lambda i: (0, 0)), tile, vec],
        out_shape=[jax.ShapeDtypeStruct((1, LANES), F32), jax.ShapeDtypeStruct((s, d), F32),
                   jax.ShapeDtypeStruct((1, d), F32)],
        compiler_params=_cp(("arbitrary",)),
    )(h, g, target)


def _adamw(w, g, m, v, *, name):
    rows, cols = w.shape
    tr = _pick(rows, 256, 8)
    c1 = 1.0 - ADAM_B1 ** ADAM_STEP
    c2 = 1.0 - ADAM_B2 ** ADAM_STEP

    def body(w_ref, g_ref, m_ref, v_ref, d_ref, nm_ref, nv_ref):
        gv = g_ref[...]
        nm = ADAM_B1 * m_ref[...] + (1.0 - ADAM_B1) * gv
        nv = ADAM_B2 * v_ref[...] + (1.0 - ADAM_B2) * (gv * gv)
        d_ref[...] = -ADAM_LR * ((nm / c1) / (jnp.sqrt(nv / c2) + ADAM_EPS) + ADAM_WD * w_ref[...])
        nm_ref[...] = nm
        nv_ref[...] = nv

    tile = pl.BlockSpec((tr, cols), lambda i: (i, 0))
    shp = jax.ShapeDtypeStruct((rows, cols), F32)
    return pl.pallas_call(
        body, name=name, grid=(rows // tr,), in_specs=[tile] * 4, out_specs=[tile] * 3, out_shape=[shp] * 3,
        compiler_params=_cp(("parallel",)),
    )(w, g, m, v)


ANY = pl.BlockSpec(memory_space=pl.ANY)
N_DEV = 8


def _coords():
    return lax.axis_index("x"), lax.axis_index("y"), lax.axis_index("c")


def _other_chips(x, y):
    return [(1 - x, y), (x, 1 - y), (1 - x, 1 - y)]


def _allgather_small(buf, *, name, with_sum):
    rows = buf.shape[0]

    def body(*refs):
        if with_sum:
            x_ref, out_ref, sum_ref, send_sems, recv_sems = refs
        else:
            x_ref, out_ref, send_sems, recv_sems = refs
        x, y, c = _coords()
        me = 4 * x + 2 * y + c
        out_ref[me] = x_ref[...]
        copies = []
        for rel in range(1, N_DEV):
            px = (1 - x) if rel & 4 else x
            py = (1 - y) if rel & 2 else y
            pc = (1 - c) if rel & 1 else c
            cp = pltpu.make_async_remote_copy(
                src_ref=x_ref, dst_ref=out_ref.at[me], send_sem=send_sems.at[rel - 1], recv_sem=recv_sems.at[rel - 1],
                device_id=(px, py, pc), device_id_type=MESH)
            cp.start()
            copies.append(cp)
        for cp in copies:
            cp.wait()
        if with_sum:
            acc = out_ref[0]
            for k in range(1, N_DEV):
                acc = acc + out_ref[k]
            sum_ref[...] = acc

    slots = jax.ShapeDtypeStruct((N_DEV, rows, LANES), F32)
    vm = pl.BlockSpec(memory_space=pltpu.VMEM)
    out_shape = [slots, jax.ShapeDtypeStruct((rows, LANES), F32)] if with_sum else [slots]
    return pl.pallas_call(
        body, name=name, in_specs=[vm], out_specs=[vm] * len(out_shape), out_shape=out_shape,
        scratch_shapes=[pltpu.SemaphoreType.DMA((N_DEV - 1,)), pltpu.SemaphoreType.DMA((N_DEV - 1,))],
    )(buf)


def _allgather_chips(shards, *, name):
    n = len(shards)

    def body(*refs):
        ins, outs = refs[:n], refs[n:2 * n]
        send_sems, recv_sems = refs[2 * n:]
        x, y, c = _coords()
        k = 2 * x + y
        chips = _other_chips(x, y)
        sibling = (x, y, 1 - c)
        sends = []
        for t in range(n):
            for j, (px, py) in enumerate(chips):
                cp = pltpu.make_async_remote_copy(
                    src_ref=ins[t].at[:, c], dst_ref=outs[t].at[:, k, c], send_sem=send_sems.at[6 * t + j],
                    recv_sem=recv_sems.at[6 * t + j], device_id=(px, py, c), device_id_type=MESH)
                cp.start()
                sends.append(cp)
        for t in range(n):
            for j, (px, py) in enumerate(chips):
                kj = 2 * px + py
                pltpu.make_async_remote_copy(
                    src_ref=ins[t].at[:, c], dst_ref=outs[t].at[:, kj, c], send_sem=send_sems.at[6 * t + j],
                    recv_sem=recv_sems.at[6 * t + j], device_id=(px, py, c), device_id_type=MESH).wait_recv()
                cp = pltpu.make_async_remote_copy(
                    src_ref=outs[t].at[:, kj, c], dst_ref=outs[t].at[:, kj, c], send_sem=send_sems.at[6 * t + 3 + j],
                    recv_sem=recv_sems.at[6 * t + 3 + j], device_id=sibling, device_id_type=MESH)
                cp.start()
                sends.append(cp)
        for t in range(n):
            for j, (px, py) in enumerate(chips):
                kj = 2 * px + py
                pltpu.make_async_remote_copy(
                    src_ref=outs[t].at[:, kj, 1 - c], dst_ref=outs[t].at[:, kj, 1 - c], send_sem=send_sems.at[6 * t + 3 + j],
                    recv_sem=recv_sems.at[6 * t + 3 + j], device_id=sibling, device_id_type=MESH).wait_recv()
        for cp in sends:
            cp.wait_send()

    return pl.pallas_call(
        body, name=name, in_specs=[ANY] * n, out_specs=[ANY] * n,
        out_shape=[jax.ShapeDtypeStruct((s.shape[0], 4) + s.shape[1:], s.dtype) for s in shards],
        scratch_shapes=[pltpu.SemaphoreType.DMA((6 * n,)), pltpu.SemaphoreType.DMA((6 * n,))],
    )(*shards)


def _sibling_swap(arrs, *, name, other_half):
    n = len(arrs)

    def body(*refs):
        ins, outs = refs[:n], refs[n:2 * n]
        send_sems, recv_sems = refs[2 * n:]
        x, y, c = _coords()
        copies = []
        for t in range(n):
            cp = pltpu.make_async_remote_copy(
                src_ref=ins[t].at[:, 1 - c] if other_half else ins[t], dst_ref=outs[t], send_sem=send_sems.at[t],
                recv_sem=recv_sems.at[t], device_id=(x, y, 1 - c), device_id_type=MESH)
            cp.start()
            copies.append(cp)
        for cp in copies:
            cp.wait()

    return pl.pallas_call(
        body, name=name, in_specs=[ANY] * n, out_specs=[ANY] * n,
        out_shape=[jax.ShapeDtypeStruct(a.shape[:1] + a.shape[2:] if other_half else a.shape, a.dtype) for a in arrs],
        scratch_shapes=[pltpu.SemaphoreType.DMA((n,)), pltpu.SemaphoreType.DMA((n,))],
    )(*arrs)


def _chip_exchange(arrs, *, name):
    n = len(arrs)

    def body(*refs):
        ins, outs = refs[:n], refs[n:2 * n]
        send_sems, recv_sems = refs[2 * n:]
        x, y, c = _coords()
        copies = []
        for t in range(n):
            for j, (px, py) in enumerate(_other_chips(x, y)):
                cp = pltpu.make_async_remote_copy(
                    src_ref=ins[t].at[2 * px + py], dst_ref=outs[t].at[j], send_sem=send_sems.at[3 * t + j],
                    recv_sem=recv_sems.at[3 * t + j], device_id=(px, py, c), device_id_type=MESH)
                cp.start()
                copies.append(cp)
        for cp in copies:
            cp.wait()

    return pl.pallas_call(
        body, name=name, in_specs=[ANY] * n, out_specs=[ANY] * n,
        out_shape=[jax.ShapeDtypeStruct((3,) + a.shape[1:], a.dtype) for a in arrs],
        scratch_shapes=[pltpu.SemaphoreType.DMA((3 * n,)), pltpu.SemaphoreType.DMA((3 * n,))],
    )(*arrs)


def _pair_add(g, recv, sel, *, name):
    _, _, m, cols = g.shape
    tr = _pick(m, 256, 16)

    def body(sel_ref, g_ref, r_ref, out_ref, low_ref):
        acc = g_ref[0, 0].astype(F32) + r_ref[0].astype(F32)
        out_ref[0] = acc
        low_ref[0] = acc.astype(BF16)

    tile = pl.BlockSpec((1, tr, cols), lambda q, i, sel_ref: (q, i, 0))
    return pl.pallas_call(
        body, name=name,
        grid_spec=pltpu.PrefetchScalarGridSpec(
            num_scalar_prefetch=1, grid=(4, m // tr),
            in_specs=[pl.BlockSpec((1, 1, tr, cols), lambda q, i, sel_ref: (q, sel_ref[0], i, 0)), tile],
            out_specs=[tile, tile]),
        out_shape=[jax.ShapeDtypeStruct((4, m, cols), F32), jax.ShapeDtypeStruct((4, m, cols), BF16)],
        compiler_params=_cp(("parallel", "parallel")),
    )(sel, g, recv)


def _add_selected(stack, others, sel, *, name):
    _, m, cols = stack.shape
    q = others.shape[0]
    tr = _pick(m, 256, 16)

    def body(sel_ref, s_ref, o_ref, out_ref):
        acc = s_ref[0].astype(F32)
        for i in range(q):
            acc = acc + o_ref[i].astype(F32)
        out_ref[...] = acc

    return pl.pallas_call(
        body, name=name,
        grid_spec=pltpu.PrefetchScalarGridSpec(
            num_scalar_prefetch=1, grid=(m // tr,),
            in_specs=[pl.BlockSpec((1, tr, cols), lambda i, sel_ref: (sel_ref[0], i, 0)),
                      pl.BlockSpec((q, tr, cols), lambda i, sel_ref: (0, i, 0))],
            out_specs=pl.BlockSpec((tr, cols), lambda i, sel_ref: (i, 0))),
        out_shape=jax.ShapeDtypeStruct((m, cols), F32),
        compiler_params=_cp(("parallel",)),
    )(sel, stack, others)


def _reduce_scatter(grads, c, k):
    csel, ksel = jnp.reshape(c, (1,)).astype(jnp.int32), jnp.reshape(k, (1,)).astype(jnp.int32)
    from_sib = _sibling_swap(grads, name="rs_pair_swap", other_half=True)
    sums = [_pair_add(g, r, csel, name=f"rs_pair_add{t}") for t, (g, r) in enumerate(zip(grads, from_sib))]
    from_chips = _chip_exchange([low for _, low in sums], name="rs_chip_exchange")
    return [_add_selected(s, r, ksel, name=f"rs_chip_add{t}") for t, ((s, _), r) in enumerate(zip(sums, from_chips))]


BIG = ("ssd_w_in", "ssd_w_out", "fox_w_in", "fox_w_out", "ffn_w_up", "ffn_w_down")
COL_SHARDED = ("ssd_w_in", "fox_w_in", "ffn_w_up")
SMALL = (("mix_norm_g", (4, 1024)), ("ffn_norm_g", (4, 1024)), ("ssd_conv_w", (2, 4, 3072)), ("ssd_conv_b", (2, 3072)),
         ("ssd_dt_bias", (2, 32)), ("ssd_a_log", (2, 32)), ("ssd_d", (2, 32)), ("ssd_norm_g", (2, 2048)),
         ("fox_b_f", (2, 16)), ("fox_q_norm_g", (2, 64)), ("fox_k_norm_g", (2, 64)), ("ffn_conv_w", (4, 3, 2816)),
         ("ffn_conv_b", (4, 2816)), ("final_norm_g", (1024,)), ("loss", (1,)))
NAMES = ("mix_norm_g", "ffn_norm_g", "ssd_w_in", "ssd_conv_w", "ssd_conv_b", "ssd_dt_bias", "ssd_a_log", "ssd_d",
         "ssd_norm_g", "ssd_w_out", "fox_w_in", "fox_b_f", "fox_q_norm_g", "fox_k_norm_g", "fox_w_out", "ffn_w_up",
         "ffn_conv_w", "ffn_conv_b", "ffn_w_down", "final_norm_g")


def _pack(parts):
    flat = jnp.concatenate([jnp.reshape(p, (-1,)).astype(F32) for p in parts])
    rows = -(-flat.shape[0] // (8 * LANES)) * 8
    return jnp.pad(flat, (0, rows * LANES - flat.shape[0])).reshape(rows, LANES)


def _unpack(buf, shapes):
    flat = buf.reshape(-1)
    out, off = [], 0
    for shp in shapes:
        size = 1
        for d in shp:
            size *= d
        out.append(flat[off:off + size].reshape(shp))
        off += size
    return out


def _pad_lanes(a):
    return jnp.pad(a, ((0, 0), (0, LANES - a.shape[1])))


def _pad8(w):
    return jnp.pad(w, ((0, 8 - w.shape[0]), (0, 0)))


def _ssd_forward(h, p, name):
    s = h.shape[0]
    hn = _rms_fwd(h, p["mix_g"], gw=D_MODEL, ncol=1, name=f"{name}_norm")
    zx = _matmul(hn, p["w_zx"], mode="nn", name=f"{name}_proj")
    dtp = _matmul(hn, p["w_dt"], mode="nn", name=f"{name}_proj_dt")
    xbc = _conv_fwd(zx, p["conv_w8"], p["conv_b"], kw=SSD_K, width=SSD_CONV_DIM, u_col0=SSD_DI, name=f"{name}_conv")
    dt3 = dtp[:, :SSD_H].reshape(s, SSD_G, SSD_HPG)
    dtg, dtg_t = jnp.transpose(dt3, (1, 0, 2)), jnp.transpose(dt3, (1, 2, 0))
    sp = (p["bias_r"], p["bias_c"], p["alog_r"], p["alog_c"], p["d_r"])
    y, hprev = _ssd_fwd(xbc, dtg, dtg_t, *sp, name=f"{name}_scan")
    y2 = _rms_fwd(y, p["norm_g"], gw=SSD_DI // SSD_G, ncol=SSD_G, z=zx, name=f"{name}_gnorm")
    out = _matmul(y2, p["w_out"], mode="nn", add=h, name=f"{name}_out")
    return out, dict(h=h, hn=hn, zx=zx, xbc=xbc, dtg=dtg, dtg_t=dtg_t, y=y, hprev=hprev, y2=y2)


def _ssd_backward(dh1, p, a, name):
    s = dh1.shape[0]
    g = {}
    dy2 = _matmul(dh1, p["w_out"], mode="nt", name=f"{name}_out_dx")
    g["w_out"] = _matmul(a["y2"], dh1, mode="tn", out_dtype=BF16, name=f"{name}_out_dw")
    dy, dz, g["norm_g"] = _rms_bwd(a["y"], p["norm_g"], dy2, gw=SSD_DI // SSD_G, ncol=SSD_G, z=a["zx"], name=f"{name}_gnorm_b")
    sp = (p["bias_r"], p["bias_c"], p["alog_r"], p["alog_c"], p["d_r"])
    dx, dbm, dcm, ddt, g["dt_bias"], g["a_log"], g["d"] = _ssd_bwd(
        a["xbc"], a["dtg"], a["dtg_t"], *sp, a["hprev"], dy, name=f"{name}_scan_b")
    dact = jnp.concatenate([dx, dbm, dcm], axis=1)
    dxbc, dwb = _conv_bwd(a["zx"], p["conv_w8"], p["conv_b"], dact, kw=SSD_K, width=SSD_CONV_DIM, u_col0=SSD_DI,
                          name=f"{name}_conv_b")
    g["conv_w"], g["conv_b"] = dwb[:SSD_K], dwb[7]
    dzx = jnp.concatenate([dz.astype(BF16), dxbc], axis=1)
    ddtp = _pad_lanes(jnp.transpose(ddt, (1, 0, 2)).reshape(s, SSD_H))
    dhn = _matmul(dzx, p["w_zx"], mode="nt", name=f"{name}_proj_dx")
    dhn = _matmul(ddtp, p["w_dt"], mode="nt", add=dhn, name=f"{name}_proj_dt_dx")
    dw_zx = _matmul(a["hn"], dzx, mode="tn", out_dtype=BF16, name=f"{name}_proj_dw")
    dw_dt = _matmul(a["hn"], ddtp, mode="tn", out_dtype=BF16, name=f"{name}_proj_dt_dw")
    g["w_in"] = jnp.concatenate([dw_zx, dw_dt[:, :SSD_H]], axis=1)
    dh, g["mix_g"] = _rms_bwd(a["h"], p["mix_g"], dhn, gw=D_MODEL, ncol=1, add=dh1, name=f"{name}_norm_b")
    return dh, g


def _fox_forward(h, p, name):
    s = h.shape[0]
    hn = _rms_fwd(h, p["mix_g"], gw=D_MODEL, ncol=1, name=f"{name}_norm")
    qkvg = _matmul(hn, p["w_qkvg"], mode="nn", name=f"{name}_proj")
    fp = _matmul(hn, p["w_f"], mode="nn", name=f"{name}_proj_f")
    qs = _rms_fwd(qkvg, p["gq"] * FOX_SCALE, gw=FOX_D, ncol=1, x_col0=0, sub=FOX_HD, name=f"{name}_qnorm")
    kn = _rms_fwd(qkvg, p["gk"], gw=FOX_D, ncol=1, x_col0=1, sub=FOX_HD, name=f"{name}_knorm")
    f_t = jnp.transpose(fp[:, :FOX_H])
    cum_t = _fgate_fwd(f_t, p["b_f"], name=f"{name}_fgate")
    ck = cum_t.reshape(FOX_PAIRS, 2, s)
    o, lse = _flash_fwd(qs, kn, qkvg, ck, name=f"{name}_attn")
    og = _ogate_fwd(o, qkvg, name=f"{name}_ogate")
    out = _matmul(og, p["w_out"], mode="nn", add=h, name=f"{name}_out")
    return out, dict(h=h, hn=hn, qkvg=qkvg, qs=qs, kn=kn, f_t=f_t, ck=ck, o=o, lse=lse, og=og)


def _fox_backward(dh1, p, a, name):
    s = dh1.shape[0]
    g = {}
    dog = _matmul(dh1, p["w_out"], mode="nt", name=f"{name}_out_dx")
    g["w_out"] = _matmul(a["og"], dh1, mode="tn", out_dtype=BF16, name=f"{name}_out_dw")
    do, dgate, delta = _ogate_bwd(dog, a["o"], a["qkvg"], name=f"{name}_ogate_b")
    dl = jnp.transpose(delta.reshape(s, FOX_PAIRS, 2), (1, 0, 2))
    dq, dk, dv, dcq, dck = _flash_bwd(a["qs"], a["kn"], a["qkvg"], do, a["lse"], dl, a["ck"], name=f"{name}_attn_b")
    dq_raw, dgq = _rms_bwd(a["qkvg"], p["gq"], dq, gw=FOX_D, ncol=1, x_col0=0, sub=FOX_HD, dx_dtype=BF16, name=f"{name}_qnorm_b")
    dk_raw, dgk = _rms_bwd(a["qkvg"], p["gk"], dk, gw=FOX_D, ncol=1, x_col0=1, sub=FOX_HD, dx_dtype=BF16, name=f"{name}_knorm_b")
    g["gq"] = dgq.reshape(FOX_H, FOX_HD).sum(axis=0)
    g["gk"] = dgk.reshape(FOX_H, FOX_HD).sum(axis=0)
    dcq_t = jnp.transpose(dcq, (0, 2, 1)).reshape(FOX_H, s)
    df_t, dbf = _fgate_bwd(dcq_t, dck.reshape(FOX_H, s), a["f_t"], p["b_f"], name=f"{name}_fgate_b")
    g["b_f"] = dbf[:, 0]
    dproj = jnp.concatenate([dq_raw, dk_raw, dv.astype(BF16), dgate], axis=1)
    dfp = _pad_lanes(jnp.transpose(df_t))
    dhn = _matmul(dproj, p["w_qkvg"], mode="nt", name=f"{name}_proj_dx")
    dhn = _matmul(dfp, p["w_f"], mode="nt", add=dhn, name=f"{name}_proj_f_dx")
    dw_qkvg = _matmul(a["hn"], dproj, mode="tn", out_dtype=BF16, name=f"{name}_proj_dw")
    dw_f = _matmul(a["hn"], dfp, mode="tn", out_dtype=BF16, name=f"{name}_proj_f_dw")
    g["w_in"] = jnp.concatenate([dw_qkvg, dw_f[:, :FOX_H]], axis=1)
    dh, g["mix_g"] = _rms_bwd(a["h"], p["mix_g"], dhn, gw=D_MODEL, ncol=1, add=dh1, name=f"{name}_norm_b")
    return dh, g


def _ffn_forward(h, p, name):
    hn = _rms_fwd(h, p["ffn_g"], gw=D_MODEL, ncol=1, name=f"{name}_norm")
    u = _matmul(hn, p["w_up"], mode="nn", name=f"{name}_up")
    act = _conv_fwd(u, p["conv_w8"], p["conv_b"], kw=FFN_K, width=D_FF, u_col0=0, mul_col0=D_FF, out_dtype=BF16,
                    name=f"{name}_glu")
    out = _matmul(act, p["w_down"], mode="nn", add=h, name=f"{name}_down")
    return out, dict(h=h, hn=hn, u=u, act=act)


def _ffn_backward(dh2, p, a, name):
    g = {}
    dact = _matmul(dh2, p["w_down"], mode="nt", name=f"{name}_down_dx")
    g["w_down"] = _matmul(a["act"], dh2, mode="tn", out_dtype=BF16, name=f"{name}_down_dw")
    du1, du2, dwb = _conv_bwd(a["u"], p["conv_w8"], p["conv_b"], dact, kw=FFN_K, width=D_FF, u_col0=0, mul_col0=D_FF,
                              name=f"{name}_glu_b")
    g["conv_w"], g["conv_b"] = dwb[:FFN_K], dwb[7]
    du = jnp.concatenate([du1, du2], axis=1)
    dhn = _matmul(du, p["w_up"], mode="nt", name=f"{name}_up_dx")
    g["w_up"] = _matmul(a["hn"], du, mode="tn", out_dtype=BF16, name=f"{name}_up_dw")
    dh, g["ffn_g"] = _rms_bwd(a["h"], p["ffn_g"], dhn, gw=D_MODEL, ncol=1, add=dh2, name=f"{name}_norm_b")
    return dh, g


def _to_slabs(dw, col_sharded):
    rows, cols = dw.shape
    if col_sharded:
        return jnp.transpose(dw.reshape(rows, 4, cols // 4), (1, 0, 2)).reshape(4, 2, rows // 2, cols // 4)
    return dw.reshape(4, 2, rows // 8, cols)


def kernel(x, mix_norm_g, ffn_norm_g, ssd_w_in, ssd_conv_w, ssd_conv_b, ssd_dt_bias, ssd_a_log, ssd_d, ssd_norm_g, ssd_w_out, fox_w_in, fox_b_f, fox_q_norm_g, fox_k_norm_g, fox_w_out, ffn_w_up, ffn_conv_w, ffn_conv_b, ffn_w_down, final_norm_g, loss_target, m_mix_norm_g, m_ffn_norm_g, m_ssd_w_in, m_ssd_conv_w, m_ssd_conv_b, m_ssd_dt_bias, m_ssd_a_log, m_ssd_d, m_ssd_norm_g, m_ssd_w_out, m_fox_w_in, m_fox_b_f, m_fox_q_norm_g, m_fox_k_norm_g, m_fox_w_out, m_ffn_w_up, m_ffn_conv_w, m_ffn_conv_b, m_ffn_w_down, m_final_norm_g, v_mix_norm_g, v_ffn_norm_g, v_ssd_w_in, v_ssd_conv_w, v_ssd_conv_b, v_ssd_dt_bias, v_ssd_a_log, v_ssd_d, v_ssd_norm_g, v_ssd_w_out, v_fox_w_in, v_fox_b_f, v_fox_q_norm_g, v_fox_k_norm_g, v_fox_w_out, v_ffn_w_up, v_ffn_conv_w, v_ffn_conv_b, v_ffn_w_down, v_final_norm_g):
    w = dict(mix_norm_g=mix_norm_g, ffn_norm_g=ffn_norm_g, ssd_w_in=ssd_w_in, ssd_conv_w=ssd_conv_w, ssd_conv_b=ssd_conv_b,
             ssd_dt_bias=ssd_dt_bias, ssd_a_log=ssd_a_log, ssd_d=ssd_d, ssd_norm_g=ssd_norm_g, ssd_w_out=ssd_w_out,
             fox_w_in=fox_w_in, fox_b_f=fox_b_f, fox_q_norm_g=fox_q_norm_g, fox_k_norm_g=fox_k_norm_g, fox_w_out=fox_w_out,
             ffn_w_up=ffn_w_up, ffn_conv_w=ffn_conv_w, ffn_conv_b=ffn_conv_b, ffn_w_down=ffn_w_down, final_norm_g=final_norm_g)
    m_in = dict(zip(NAMES, (m_mix_norm_g, m_ffn_norm_g, m_ssd_w_in, m_ssd_conv_w, m_ssd_conv_b, m_ssd_dt_bias, m_ssd_a_log,
                            m_ssd_d, m_ssd_norm_g, m_ssd_w_out, m_fox_w_in, m_fox_b_f, m_fox_q_norm_g, m_fox_k_norm_g,
                            m_fox_w_out, m_ffn_w_up, m_ffn_conv_w, m_ffn_conv_b, m_ffn_w_down, m_final_norm_g)))
    v_in = dict(zip(NAMES, (v_mix_norm_g, v_ffn_norm_g, v_ssd_w_in, v_ssd_conv_w, v_ssd_conv_b, v_ssd_dt_bias, v_ssd_a_log,
                            v_ssd_d, v_ssd_norm_g, v_ssd_w_out, v_fox_w_in, v_fox_b_f, v_fox_q_norm_g, v_fox_k_norm_g,
                            v_fox_w_out, v_ffn_w_up, v_ffn_conv_w, v_ffn_conv_b, v_ffn_w_down, v_final_norm_g)))
    cx, cy, cc = _coords()
    chip = 2 * cx + cy
    h = x[0]
    target = loss_target[0]

    conv_shapes = [ssd_conv_w.shape, ffn_conv_w.shape]
    slots = _allgather_small(_pack([ssd_conv_w, ffn_conv_w]), name="gather_conv_w", with_sum=False)[0]
    per_chip = [_unpack(slots[2 * q], conv_shapes) for q in range(4)]
    ssd_conv_full = jnp.concatenate([pc[0] for pc in per_chip], axis=2)
    ffn_conv_full = jnp.concatenate([pc[1] for pc in per_chip], axis=2)
    shards = [w[n].astype(BF16) for n in BIG]
    shards = [s.reshape((s.shape[0], 2, s.shape[1] // 2, s.shape[2])) for s in shards]
    gathered = _allgather_chips(shards, name="gather_weights")
    full = {}
    for n, own, gth in zip(BIG, shards, gathered):
        gth = lax.dynamic_update_slice(gth, own[:, None], (0, chip, 0, 0, 0))
        nl, _, _, half, cols = gth.shape
        if n in COL_SHARDED:
            full[n] = jnp.transpose(gth.reshape(nl, 4, 2 * half, cols), (0, 2, 1, 3)).reshape(nl, 2 * half, 4 * cols)
        else:
            full[n] = gth.reshape(nl, 8 * half, cols)

    def ssd_params(j, i):
        w_in = full["ssd_w_in"][j]
        g3 = lambda v: v.reshape(SSD_G, 1, SSD_HPG)
        g3c = lambda v: v.reshape(SSD_G, SSD_HPG, 1)
        return dict(mix_g=mix_norm_g[i][None], w_zx=w_in[:, :SSD_ZX], w_dt=_pad_lanes(w_in[:, SSD_ZX:]),
                    conv_w8=_pad8(ssd_conv_full[j]), conv_b=ssd_conv_b[j][None], bias_r=g3(ssd_dt_bias[j]),
                    bias_c=g3c(ssd_dt_bias[j]), alog_r=g3(ssd_a_log[j]), alog_c=g3c(ssd_a_log[j]), d_r=g3(ssd_d[j]),
                    norm_g=ssd_norm_g[j][None], w_out=full["ssd_w_out"][j])

    def fox_params(j, i):
        w_in = full["fox_w_in"][j]
        return dict(mix_g=mix_norm_g[i][None], w_qkvg=w_in[:, :4 * FOX_D], w_f=_pad_lanes(w_in[:, 4 * FOX_D:]),
                    gq=jnp.tile(fox_q_norm_g[j], FOX_H)[None], gk=jnp.tile(fox_k_norm_g[j], FOX_H)[None],
                    b_f=fox_b_f[j][:, None], w_out=full["fox_w_out"][j])

    def ffn_params(i):
        return dict(ffn_g=ffn_norm_g[i][None], w_up=full["ffn_w_up"][i], conv_w8=_pad8(ffn_conv_full[i]),
                    conv_b=ffn_conv_b[i][None], w_down=full["ffn_w_down"][i])

    mix_p, ffn_p, mix_a, ffn_a = [], [], [], []
    for i in range(DEPTH):
        j = i // 2
        if i % 2 == 0:
            mix_p.append(ssd_params(j, i))
            h, act = _ssd_forward(h, mix_p[i], f"ssd{j}")
        else:
            mix_p.append(fox_params(j, i))
            h, act = _fox_forward(h, mix_p[i], f"fox{j}")
        mix_a.append(act)
        ffn_p.append(ffn_params(i))
        h, act = _ffn_forward(h, ffn_p[i], f"ffn{i}")
        ffn_a.append(act)
    loss_part, dh, d_final_g = _loss_head(h, final_norm_g[None], target, name="loss_head")

    mix_g, ffn_g = [None] * DEPTH, [None] * DEPTH
    for i in reversed(range(DEPTH)):
        j = i // 2
        dh, ffn_g[i] = _ffn_backward(dh, ffn_p[i], ffn_a[i], f"ffn{i}")
        if i % 2 == 0:
            dh, mix_g[i] = _ssd_backward(dh, mix_p[i], mix_a[i], f"ssd{j}")
        else:
            dh, mix_g[i] = _fox_backward(dh, mix_p[i], mix_a[i], f"fox{j}")
    grad_x = dh[None]
    ssd_g, fox_g = [mix_g[0], mix_g[2]], [mix_g[1], mix_g[3]]

    per_layer = dict(
        ssd_w_in=[g["w_in"] for g in ssd_g], ssd_w_out=[g["w_out"] for g in ssd_g],
        fox_w_in=[g["w_in"] for g in fox_g], fox_w_out=[g["w_out"] for g in fox_g],
        ffn_w_up=[g["w_up"] for g in ffn_g], ffn_w_down=[g["w_down"] for g in ffn_g])
    slabs = [_to_slabs(dw, n in COL_SHARDED) for n in BIG for dw in per_layer[n]]
    finals = _reduce_scatter(slabs, cc, chip)
    others = _sibling_swap(finals, name="rs_result_swap", other_half=False)
    grads, pos = {}, 0
    for n in BIG:
        nl = len(per_layer[n])
        mine, theirs = jnp.stack(finals[pos:pos + nl]), jnp.stack(others[pos:pos + nl])
        halves = jnp.stack([jnp.where(cc == 0, mine, theirs), jnp.where(cc == 0, theirs, mine)], axis=1)
        grads[n] = halves.reshape(w[n].shape)
        pos += nl
    small = dict(
        mix_norm_g=jnp.concatenate([g["mix_g"] for g in mix_g], axis=0),
        ffn_norm_g=jnp.concatenate([g["ffn_g"] for g in ffn_g], axis=0),
        ssd_conv_w=jnp.stack([g["conv_w"] for g in ssd_g]), ssd_conv_b=jnp.stack([g["conv_b"] for g in ssd_g]),
        ssd_dt_bias=jnp.stack([g["dt_bias"].reshape(SSD_H) for g in ssd_g]),
        ssd_a_log=jnp.stack([g["a_log"].reshape(SSD_H) for g in ssd_g]),
        ssd_d=jnp.stack([g["d"].reshape(SSD_H) for g in ssd_g]),
        ssd_norm_g=jnp.concatenate([g["norm_g"] for g in ssd_g], axis=0),
        fox_b_f=jnp.stack([g["b_f"] for g in fox_g]), fox_q_norm_g=jnp.stack([g["gq"] for g in fox_g]),
        fox_k_norm_g=jnp.stack([g["gk"] for g in fox_g]),
        ffn_conv_w=jnp.stack([g["conv_w"] for g in ffn_g]), ffn_conv_b=jnp.stack([g["conv_b"] for g in ffn_g]),
        final_norm_g=d_final_g[0], loss=loss_part[0, :1])
    _, total = _allgather_small(_pack([small[n] for n, _ in SMALL]), name="reduce_small", with_sum=True)
    for (n, shp), val in zip(SMALL, _unpack(total, [shp for _, shp in SMALL])):
        grads[n] = val
    loss = grads.pop("loss")[0]
    grads["ssd_conv_w"] = lax.dynamic_slice_in_dim(grads["ssd_conv_w"], chip * ssd_conv_w.shape[2], ssd_conv_w.shape[2], axis=2)
    grads["ffn_conv_w"] = lax.dynamic_slice_in_dim(grads["ffn_conv_w"], chip * ffn_conv_w.shape[2], ffn_conv_w.shape[2], axis=2)

    deltas, new_m, new_v = {}, {}, {}
    for n in NAMES:
        shp = w[n].shape
        two_d = (1, shp[0]) if len(shp) == 1 else (-1, shp[-1])
        r2 = lambda a: a.reshape(two_d)
        d, nm, nv = _adamw(r2(w[n]), r2(grads[n]), r2(m_in[n]), r2(v_in[n]), name=f"adamw_{n}")
        deltas[n], new_m[n], new_v[n] = d.reshape(shp), nm.reshape(shp), nv.reshape(shp)
    return (loss, grad_x, *[grads[n] for n in NAMES], *[deltas[n] for n in NAMES], *[new_m[n] for n in NAMES],
            *[new_v[n] for n in NAMES])
```

```python
import functools

import jax
import jax.numpy as jnp
from jax import lax
from jax.experimental import pallas as pl
from jax.experimental.pallas import tpu as pltpu

F32 = jnp.float32
BF16 = jnp.bfloat16
HI = lax.Precision.HIGHEST
MESH = pl.DeviceIdType.MESH

D_MODEL = 1024
DEPTH = 4
EPS = 1e-6
SSD_DI = 2048
SSD_HD = 64
SSD_G = 4
SSD_HPG = 8
SSD_N = 128
SSD_K = 4
CHUNK = 128
SSD_CONV_DIM = 3072
SSD_ZX = SSD_DI + SSD_CONV_DIM
SSD_H = 32
FOX_HD = 64
FOX_H = 16
FOX_D = 1024
D_FF = 2816
FFN_K = 3
LANES = 128
VMEM_LIMIT = 56 * 1024 * 1024

ADAM_LR = 0.001
ADAM_B1 = 0.9
ADAM_B2 = 0.999
ADAM_EPS = 1e-08
ADAM_WD = 0.01
ADAM_STEP = 10

NN = (((1,), (0,)), ((), ()))
NT = (((1,), (1,)), ((), ()))
TN = (((0,), (0,)), ((), ()))


def _pick(n, cap, mult=LANES):
    best = None
    for t in range(mult, min(n, cap) + 1, mult):
        if n % t == 0:
            best = t
    return best if best is not None else n


def _cp(sem):
    return pltpu.CompilerParams(dimension_semantics=sem, vmem_limit_bytes=VMEM_LIMIT)


def _sigmoid(x):
    return jax.nn.sigmoid(x)


def _silu(x):
    return x * _sigmoid(x)


def _dsilu(x):
    s = _sigmoid(x)
    return s * (1.0 + x * (1.0 - s))


def _softplus(x):
    e = jnp.exp(-jnp.abs(x))
    u = 1.0 + e
    l1p = jnp.where(u == 1.0, e, jnp.log(u) * (e / (u - 1.0)))
    return jnp.maximum(x, 0.0) + l1p


def _dotf(a, b, dn=NN, *, onehot="b", pieces=2):
    x, e = (a, b) if onehot == "b" else (b, a)
    e = e.astype(BF16)
    acc = None
    for n in range(pieces):
        hi = x.astype(BF16)
        part = lax.dot_general(hi, e, dn, preferred_element_type=F32) if onehot == "b" else \
            lax.dot_general(e, hi, dn, preferred_element_type=F32)
        acc = part if acc is None else acc + part
        if n + 1 < pieces:
            x = x - hi.astype(F32)
    return acc


def _dotb(a, b, dn=NN):
    return lax.dot_general(a.astype(BF16), b.astype(BF16), dn, preferred_element_type=F32)


def _group_matrix(width, sub, transpose=False):
    ng = width // sub
    shape = (ng, width) if transpose else (width, ng)
    lane = lax.broadcasted_iota(jnp.int32, shape, 1 if transpose else 0)
    grp = lax.broadcasted_iota(jnp.int32, shape, 0 if transpose else 1)
    return (lane // sub == grp).astype(F32)


def _gmean(v, sub):
    width = v.shape[-1]
    if sub == width:
        return jnp.mean(v, axis=-1, keepdims=True)
    s = _dotf(v, _group_matrix(width, sub))
    return _dotf(s, _group_matrix(width, sub, transpose=True)) * (1.0 / sub)


def _matmul(a, b, *, mode, name, out_dtype=F32, add=None):
    if mode == "nn":
        (m, k), (k2, n) = a.shape, b.shape
    elif mode == "nt":
        (m, k), (n, k2) = a.shape, b.shape
    else:
        (k, m), (k2, n) = a.shape, b.shape
    assert k == k2, (a.shape, b.shape, mode)
    tm, tn, tk = _pick(m, 1024), _pick(n, 1536), _pick(k, 1536)
    nk = k // tk
    dn = {"nn": NN, "nt": NT, "tn": TN}[mode]
    has_add = add is not None

    def body(*refs):
        if has_add:
            a_ref, b_ref, add_ref, o_ref, acc_ref = refs
        else:
            a_ref, b_ref, o_ref, acc_ref = refs
            add_ref = None
        kk = pl.program_id(2)
        part = _dotb(a_ref[...], b_ref[...], dn)

        def finish(r):
            if has_add:
                r = r + add_ref[...]
            o_ref[...] = r.astype(out_dtype)

        if nk == 1:
            finish(part)
        else:
            @pl.when(kk == 0)
            def _():
                acc_ref[...] = part

            @pl.when(kk > 0)
            def _():
                acc_ref[...] += part

            @pl.when(kk == nk - 1)
            def _():
                finish(acc_ref[...])

    if mode == "nn":
        a_spec = pl.BlockSpec((tm, tk), lambda i, j, q: (i, q))
        b_spec = pl.BlockSpec((tk, tn), lambda i, j, q: (q, j))
    elif mode == "nt":
        a_spec = pl.BlockSpec((tm, tk), lambda i, j, q: (i, q))
        b_spec = pl.BlockSpec((tn, tk), lambda i, j, q: (j, q))
    else:
        a_spec = pl.BlockSpec((tk, tm), lambda i, j, q: (q, i))
        b_spec = pl.BlockSpec((tk, tn), lambda i, j, q: (q, j))
    o_spec = pl.BlockSpec((tm, tn), lambda i, j, q: (i, j))
    in_specs = [a_spec, b_spec] + ([o_spec] if has_add else [])
    args = (a, b) + ((add,) if has_add else ())
    return pl.pallas_call(
        body, name=name, grid=(m // tm, n // tn, nk), in_specs=in_specs, out_specs=o_spec,
        out_shape=jax.ShapeDtypeStruct((m, n), out_dtype),
        scratch_shapes=[pltpu.VMEM((tm, tn) if nk > 1 else (8, LANES), F32)],
        compiler_params=_cp(("parallel", "parallel", "arbitrary")),
    )(*args)


def _rms_fwd(x, g, *, gw, ncol, name, x_col0=0, sub=None, z=None, z_col0=0, out_dtype=BF16):
    rows = x.shape[0]
    tr = _pick(rows, 512, 8)
    sub = gw if sub is None else sub
    gated = z is not None

    def body(*refs):
        if gated:
            x_ref, z_ref, g_ref, o_ref = refs
            xv = x_ref[...] * _silu(z_ref[...])
        else:
            x_ref, g_ref, o_ref = refs
            xv = x_ref[...]
        r = lax.rsqrt(_gmean(xv * xv, sub) + EPS)
        o_ref[...] = (xv * r * g_ref[...]).astype(out_dtype)

    specs = [pl.BlockSpec((tr, gw), lambda j, i: (i, x_col0 + j))]
    args = [x]
    if gated:
        specs.append(pl.BlockSpec((tr, gw), lambda j, i: (i, z_col0 + j)))
        args.append(z)
    specs.append(pl.BlockSpec((1, gw), lambda j, i: (0, j)))
    args.append(g)
    return pl.pallas_call(
        body, name=name, grid=(ncol, rows // tr), in_specs=specs,
        out_specs=pl.BlockSpec((tr, gw), lambda j, i: (i, j)),
        out_shape=jax.ShapeDtypeStruct((rows, gw * ncol), out_dtype),
        compiler_params=_cp(("parallel", "parallel")),
    )(*args)


def _rms_bwd(x, g, dy, *, gw, ncol, name, x_col0=0, sub=None, z=None, z_col0=0, add=None, dx_dtype=F32):
    rows = x.shape[0]
    tr = _pick(rows, 512, 8)
    sub = gw if sub is None else sub
    gated = z is not None
    has_add = add is not None

    def body(*refs):
        refs = list(refs)
        x_ref = refs.pop(0)
        z_ref = refs.pop(0) if gated else None
        g_ref = refs.pop(0)
        dy_ref = refs.pop(0)
        add_ref = refs.pop(0) if has_add else None
        dx_ref = refs.pop(0)
        dz_ref = refs.pop(0) if gated else None
        dg_ref = refs.pop(0)
        i = pl.program_id(1)
        xv = x_ref[...]
        if gated:
            zz = z_ref[...]
            yz = xv * _silu(zz)
        else:
            yz = xv
        r = lax.rsqrt(_gmean(yz * yz, sub) + EPS)
        xh = yz * r
        dy = dy_ref[...].astype(F32)
        dyg = dy * g_ref[...]
        d_yz = r * (dyg - xh * _gmean(dyg * xh, sub))
        if gated:
            dx_ref[...] = (d_yz * _silu(zz)).astype(dx_dtype)
            dz_ref[...] = (d_yz * xv * _dsilu(zz)).astype(dx_dtype)
        elif has_add:
            dx_ref[...] = (d_yz + add_ref[...]).astype(dx_dtype)
        else:
            dx_ref[...] = d_yz.astype(dx_dtype)
        part = jnp.sum(dy * xh, axis=0, keepdims=True)

        @pl.when(i == 0)
        def _():
            dg_ref[...] = part

        @pl.when(i > 0)
        def _():
            dg_ref[...] += part

    tile = pl.BlockSpec((tr, gw), lambda j, i: (i, j))
    specs = [pl.BlockSpec((tr, gw), lambda j, i: (i, x_col0 + j))]
    args = [x]
    if gated:
        specs.append(pl.BlockSpec((tr, gw), lambda j, i: (i, z_col0 + j)))
        args.append(z)
    specs += [pl.BlockSpec((1, gw), lambda j, i: (0, j)), tile]
    args += [g, dy]
    if has_add:
        specs.append(tile)
        args.append(add)
    width = gw * ncol
    out_shape = [jax.ShapeDtypeStruct((rows, width), dx_dtype)]
    out_specs = [tile]
    if gated:
        out_shape.append(jax.ShapeDtypeStruct((rows, width), dx_dtype))
        out_specs.append(tile)
    out_shape.append(jax.ShapeDtypeStruct((1, width), F32))
    out_specs.append(pl.BlockSpec((1, gw), lambda j, i: (0, j)))
    return pl.pallas_call(
        body, name=name, grid=(ncol, rows // tr), in_specs=specs, out_specs=out_specs, out_shape=out_shape,
        compiler_params=_cp(("parallel", "arbitrary")),
    )(*args)


HALO = 8


def _conv_rows(tc):
    return 16 * 8 * LANES // tc


def _conv_fwd(u, w8, b, *, kw, width, name, u_col0=0, mul_col0=None, out_dtype=F32, rider=None):
    rows = u.shape[0]
    ts = _pick(rows, 512, 8)
    tc = _pick(width, 512)
    gated = mul_col0 is not None
    c0 = u_col0 // tc
    m0 = (mul_col0 // tc) if gated else 0
    assert u_col0 % tc == 0 and (not gated or mul_col0 % tc == 0)

    def body(*refs):
        if gated:
            cur_ref, halo_ref, mul_ref, w_ref, b_ref, o_ref, ext = refs
        else:
            cur_ref, halo_ref, w_ref, b_ref, o_ref, ext = refs
        i = pl.program_id(0)
        ext[pl.ds(0, HALO), :] = jnp.where(i == 0, 0.0, halo_ref[...])
        ext[pl.ds(HALO, ts), :] = cur_ref[...]
        bias = b_ref[...]
        taps = [w_ref[k:k + 1, :] for k in range(kw)]
        rb = _conv_rows(tc)
        for r0 in range(0, ts, rb):
            pre = bias + taps[0] * ext[pl.ds(r0 + HALO - (kw - 1), rb), :]
            for k in range(1, kw):
                pre = pre + taps[k] * ext[pl.ds(r0 + HALO - (kw - 1) + k, rb), :]
            act = _silu(pre)
            if gated:
                act = act * mul_ref[pl.ds(r0, rb), :]
            o_ref[pl.ds(r0, rb), :] = act.astype(out_dtype)

    hb = ts // HALO
    specs = [pl.BlockSpec((ts, tc), lambda i, j: (i, c0 + j)),
             pl.BlockSpec((HALO, tc), lambda i, j: (jnp.maximum(i * hb - 1, 0), c0 + j))]
    args = [u, u]
    if gated:
        specs.append(pl.BlockSpec((ts, tc), lambda i, j: (i, m0 + j)))
        args.append(u)
    specs += [pl.BlockSpec((8, tc), lambda i, j: (0, j)), pl.BlockSpec((1, tc), lambda i, j: (0, j))]
    args += [w8, b]
    outs, riding = _call(
        body, name=name, grid=(rows // ts, width // tc), in_specs=specs,
        out_specs=[pl.BlockSpec((ts, tc), lambda i, j: (i, j))],
        out_shape=[jax.ShapeDtypeStruct((rows, width), out_dtype)],
        scratch_shapes=[pltpu.VMEM((ts + HALO, tc), F32)], sem=("parallel", "parallel"), rider=rider, args=args)
    return outs + [riding]


def _conv_bwd(u, w8, b, dact, *, kw, width, name, u_col0=0, mul_col0=None, du_dtype=BF16, rider=None):
    rows = u.shape[0]
    ts = _pick(rows, 512, 8)
    tc = _pick(width, 512)
    gated = mul_col0 is not None
    c0 = u_col0 // tc
    m0 = (mul_col0 // tc) if gated else 0
    nt = rows // ts
    hb = ts // HALO

    def body(*refs):
        refs = list(refs)
        cur_ref, halo_ref = refs.pop(0), refs.pop(0)
        mul_ref = refs.pop(0) if gated else None
        w_ref, b_ref, da_ref = refs.pop(0), refs.pop(0), refs.pop(0)
        du_ref = refs.pop(0)
        dmul_ref = refs.pop(0) if gated else None
        dwb_ref, ext_u, ext_d = refs
        t = pl.program_id(1)
        ti = nt - 1 - t
        ext_u[pl.ds(0, HALO), :] = jnp.where(ti == 0, 0.0, halo_ref[...])
        ext_u[pl.ds(HALO, ts), :] = cur_ref[...]

        @pl.when(t == 0)
        def _():
            ext_d[pl.ds(ts, HALO), :] = jnp.zeros((HALO, tc), F32)
            dwb_ref[...] = jnp.zeros((8, tc), F32)

        bias = b_ref[...]
        taps = [w_ref[k:k + 1, :] for k in range(kw)]
        rb = _conv_rows(tc)
        dw_acc = [jnp.zeros((1, tc), F32) for _ in range(kw)]
        db_acc = jnp.zeros((1, tc), F32)
        for r0 in reversed(range(0, ts, rb)):
            shifted = [ext_u[pl.ds(r0 + HALO - (kw - 1) + k, rb), :] for k in range(kw)]
            pre = bias + taps[0] * shifted[0]
            for k in range(1, kw):
                pre = pre + taps[k] * shifted[k]
            sg = _sigmoid(pre)
            dsilu = sg * (1.0 + pre * (1.0 - sg))
            da = da_ref[pl.ds(r0, rb), :].astype(F32)
            if gated:
                dmul_ref[pl.ds(r0, rb), :] = (da * (pre * sg)).astype(du_dtype)
                dgp = da * mul_ref[pl.ds(r0, rb), :] * dsilu
            else:
                dgp = da * dsilu
            ext_d[pl.ds(r0, rb), :] = dgp
            du = taps[kw - 1] * dgp
            for k in range(kw - 1):
                du = du + taps[k] * ext_d[pl.ds(r0 + kw - 1 - k, rb), :]
            du_ref[pl.ds(r0, rb), :] = du.astype(du_dtype)
            for k in range(kw):
                dw_acc[k] = dw_acc[k] + jnp.sum(dgp * shifted[k], axis=0, keepdims=True)
            db_acc = db_acc + jnp.sum(dgp, axis=0, keepdims=True)
        for k in range(kw):
            dwb_ref[k:k + 1, :] += dw_acc[k]
        dwb_ref[7:8, :] += db_acc
        ext_d[pl.ds(ts, HALO), :] = ext_d[pl.ds(0, HALO), :]

    specs = [pl.BlockSpec((ts, tc), lambda j, t: (nt - 1 - t, c0 + j)),
             pl.BlockSpec((HALO, tc), lambda j, t: (jnp.maximum((nt - 1 - t) * hb - 1, 0), c0 + j))]
    args = [u, u]
    if gated:
        specs.append(pl.BlockSpec((ts, tc), lambda j, t: (nt - 1 - t, m0 + j)))
        args.append(u)
    tile = pl.BlockSpec((ts, tc), lambda j, t: (nt - 1 - t, j))
    specs += [pl.BlockSpec((8, tc), lambda j, t: (0, j)), pl.BlockSpec((1, tc), lambda j, t: (0, j)), tile]
    args += [w8, b, dact]
    out_shape = [jax.ShapeDtypeStruct((rows, width), du_dtype)]
    out_specs = [tile]
    if gated:
        out_shape.append(jax.ShapeDtypeStruct((rows, width), du_dtype))
        out_specs.append(tile)
    out_shape.append(jax.ShapeDtypeStruct((8, width), F32))
    out_specs.append(pl.BlockSpec((8, tc), lambda j, t: (0, j)))
    outs, riding = _call(
        body, name=name, grid=(width // tc, nt), in_specs=specs, out_specs=out_specs, out_shape=out_shape,
        scratch_shapes=[pltpu.VMEM((ts + HALO, tc), F32), pltpu.VMEM((ts + HALO, tc), F32)],
        sem=("parallel", "arbitrary"), rider=rider, args=args)
    return outs + [riding]


GW = SSD_HPG * SSD_HD


def _ssd_common(x, bm, cm, dt_raw, dt_raw_t, bias_r, bias_c, alog_r, alog_c):
    row = lax.broadcasted_iota(jnp.int32, (CHUNK, CHUNK), 0)
    col = lax.broadcasted_iota(jnp.int32, (CHUNK, CHUNK), 1)
    causal = row >= col
    tril = causal.astype(F32)
    triu = (row <= col).astype(F32)
    spread = _group_matrix(GW, SSD_HD, transpose=True)
    dt = _softplus(dt_raw + bias_r)
    dt_t = _softplus(dt_raw_t + bias_c)
    a_r = -jnp.exp(alog_r)
    a_c = -jnp.exp(alog_c)
    acs = _dotf(tril, dt * a_r, onehot="a", pieces=3)
    acs_t = _dotf(dt_t * a_c, triu, pieces=3)
    last = acs[CHUNK - 1:CHUNK, :]
    ds = jnp.exp(last - acs)
    cd = jnp.exp(last)
    c = dict(causal=causal, tril=tril, triu=triu, spread=spread, dt=dt, a_r=a_r, acs=acs, acs_t=acs_t, ds=ds, cd=cd)
    c["eb"] = _dotf(jnp.exp(acs), spread)
    c["dsb"] = _dotf(ds, spread)
    c["cdb"] = _dotf(cd, spread)
    c["dtb"] = _dotf(dt, spread)
    c["xdt"] = x * c["dtb"]
    c["cb"] = _dotb(cm, bm, NT)
    return c


def _ssd_lam(c, r):
    diff = c["acs"][:, r:r + 1] - c["acs_t"][r:r + 1, :]
    return jnp.exp(jnp.where(c["causal"], diff, -jnp.inf))


def _ssd_specs(nc, rev):
    def ci(t):
        return (nc - 1 - t) if rev else t
    xs = pl.BlockSpec((CHUNK, GW), lambda g, t: (ci(t), g))
    bs = pl.BlockSpec((CHUNK, SSD_N), lambda g, t: (ci(t), SSD_DI // SSD_N + g))
    cs = pl.BlockSpec((CHUNK, SSD_N), lambda g, t: (ci(t), SSD_DI // SSD_N + SSD_G + g))
    dts = pl.BlockSpec((1, CHUNK, 8), lambda g, t: (g, ci(t), 0))
    dtts = pl.BlockSpec((1, 8, CHUNK), lambda g, t: (g, 0, ci(t)))
    pr = pl.BlockSpec((1, 1, 8), lambda g, t: (g, 0, 0))
    pc = pl.BlockSpec((1, 8, 1), lambda g, t: (g, 0, 0))
    hs = pl.BlockSpec((1, 1, SSD_N, GW), lambda g, t: (ci(t), g, 0, 0))
    return xs, bs, cs, dts, dtts, pr, pc, hs


def _ssd_fwd(xbc, dtg, dtg_t, bias_r, bias_c, alog_r, alog_c, d_r, *, name, rider=None):
    s = xbc.shape[0]
    nc = s // CHUNK
    xs, bs, cs, dts, dtts, pr, pc, hs = _ssd_specs(nc, False)

    def body(x_ref, b_ref, c_ref, dt_ref, dtt_ref, br_ref, bc_ref, ar_ref, ac_ref, d_ref, y_ref, hp_ref, h_sc):
        t = pl.program_id(1)

        @pl.when(t == 0)
        def _():
            h_sc[...] = jnp.zeros_like(h_sc)

        x, bm, cm = x_ref[...], b_ref[...], c_ref[...]
        c = _ssd_common(x, bm, cm, dt_ref[0], dtt_ref[0], br_ref[0], bc_ref[0], ar_ref[0], ac_ref[0])
        h = h_sc[...]
        hp_ref[0, 0] = h
        xdt = c["xdt"]
        pieces = []
        for r in range(SSD_HPG):
            m = c["cb"] * _ssd_lam(c, r)
            pieces.append(_dotb(m, xdt[:, r * SSD_HD:(r + 1) * SSD_HD]))
        y = jnp.concatenate(pieces, axis=1) + c["eb"] * _dotb(cm, h) + x * _dotf(d_ref[0], c["spread"])
        y_ref[...] = y
        h_sc[...] = h * c["cdb"] + _dotb(bm, xdt * c["dsb"], TN)

    outs, riding = _call(
        body, name=name, grid=(SSD_G, nc),
        in_specs=[xs, bs, cs, dts, dtts, pr, pc, pr, pc, pr],
        out_specs=[xs, hs],
        out_shape=[jax.ShapeDtypeStruct((s, SSD_DI), F32), jax.ShapeDtypeStruct((nc, SSD_G, SSD_N, GW), F32)],
        scratch_shapes=[pltpu.VMEM((SSD_N, GW), F32)], sem=("parallel", "arbitrary"), rider=rider,
        args=(xbc, xbc, xbc, dtg, dtg_t, bias_r, bias_c, alog_r, alog_c, d_r))
    return outs + [riding]


def _ssd_bwd(xbc, dtg, dtg_t, bias_r, bias_c, alog_r, alog_c, d_r, hprev, dy, *, name, rider=None):
    s = xbc.shape[0]
    nc = s // CHUNK
    xs, bs, cs, dts, dtts, pr, pc, hs = _ssd_specs(nc, True)
    gsum = functools.partial(_group_matrix, GW, SSD_HD)

    def body(x_ref, b_ref, c_ref, dt_ref, dtt_ref, br_ref, bc_ref, ar_ref, ac_ref, d_ref, hp_ref, dy_ref,
             dx_ref, db_ref, dc_ref, ddt_ref, dbias_ref, dalog_ref, dd_ref, dh_sc):
        t = pl.program_id(1)

        @pl.when(t == 0)
        def _():
            dh_sc[...] = jnp.zeros_like(dh_sc)
            dbias_ref[...] = jnp.zeros_like(dbias_ref)
            dalog_ref[...] = jnp.zeros_like(dalog_ref)
            dd_ref[...] = jnp.zeros_like(dd_ref)

        x, bm, cm = x_ref[...], b_ref[...], c_ref[...]
        c = _ssd_common(x, bm, cm, dt_ref[0], dtt_ref[0], br_ref[0], bc_ref[0], ar_ref[0], ac_ref[0])
        lanesum = gsum()
        h = hp_ref[0, 0]
        dh = dh_sc[...]
        dy = dy_ref[...]
        xdt, dsb = c["xdt"], c["dsb"]
        skip = _dotf(d_ref[0], c["spread"])
        dd_ref[0] += jnp.sum(_dotf(dy * x, lanesum), axis=0, keepdims=True)
        dacs = _dotf(dy * (c["eb"] * _dotb(cm, h)), lanesum)
        edy = c["eb"] * dy
        dcm = _dotb(edy, h, NT)
        dh_prev = _dotb(cm, edy, TN)
        bdh = _dotb(bm, dh)
        dxdt = dsb * bdh
        dbm = _dotb(dsb * xdt, dh, NT)
        t1 = _dotf(xdt * bdh, lanesum) * c["ds"]
        dacs = dacs - t1
        dlast = jnp.sum(t1, axis=0, keepdims=True) + jnp.sum(_dotf(dh * h, lanesum), axis=0, keepdims=True) * c["cd"]
        dcb = jnp.zeros((CHUNK, CHUNK), F32)
        pieces = []
        ones8 = jnp.ones((CHUNK, 8), F32)
        head = lax.broadcasted_iota(jnp.int32, (1, 8), 1)
        for r in range(SSD_HPG):
            sl = slice(r * SSD_HD, (r + 1) * SSD_HD)
            lam = _ssd_lam(c, r)
            m = c["cb"] * lam
            dm = _dotb(dy[:, sl], xdt[:, sl], NT)
            dcb = dcb + dm * lam
            gm = dm * m
            dacs = dacs + (jnp.sum(gm, axis=1, keepdims=True) - _dotf(gm, ones8, TN, pieces=3)) * (head == r).astype(F32)
            pieces.append(_dotb(m, dy[:, sl], TN))
        dxdt = dxdt + jnp.concatenate(pieces, axis=1)
        dcm = dcm + _dotb(dcb, bm)
        dbm = dbm + _dotb(dcb, cm, TN)
        dx_ref[...] = dy * skip + dxdt * c["dtb"]
        db_ref[...] = dbm
        dc_ref[...] = dcm
        rowid = lax.broadcasted_iota(jnp.int32, (CHUNK, 8), 0)
        dacs = dacs + jnp.where(rowid == CHUNK - 1, dlast, 0.0)
        dda = _dotf(c["triu"], dacs, onehot="a", pieces=3)
        ddt = _dotf(dxdt * x, lanesum) + dda * c["a_r"]
        ddt_raw = ddt * _sigmoid(dt_ref[0] + br_ref[0])
        ddt_ref[0] = ddt_raw
        dbias_ref[0] += jnp.sum(ddt_raw, axis=0, keepdims=True)
        dalog_ref[0] += jnp.sum(dda * c["dt"], axis=0, keepdims=True) * c["a_r"]
        dh_sc[...] = dh_prev + dh * c["cdb"]

    ci = lambda t: nc - 1 - t
    nspec = pl.BlockSpec((CHUNK, SSD_N), lambda g, t: (ci(t), g))
    outs, riding = _call(
        body, name=name, grid=(SSD_G, nc),
        in_specs=[xs, bs, cs, dts, dtts, pr, pc, pr, pc, pr, hs, xs],
        out_specs=[xs, nspec, nspec, dts, pr, pr, pr],
        out_shape=[jax.ShapeDtypeStruct((s, SSD_DI), F32), jax.ShapeDtypeStruct((s, SSD_G * SSD_N), F32),
                   jax.ShapeDtypeStruct((s, SSD_G * SSD_N), F32), jax.ShapeDtypeStruct((SSD_G, s, 8), F32),
                   jax.ShapeDtypeStruct((SSD_G, 1, 8), F32), jax.ShapeDtypeStruct((SSD_G, 1, 8), F32),
                   jax.ShapeDtypeStruct((SSD_G, 1, 8), F32)],
        scratch_shapes=[pltpu.VMEM((SSD_N, GW), F32)], sem=("parallel", "arbitrary"), rider=rider,
        args=(xbc, xbc, xbc, dtg, dtg_t, bias_r, bias_c, alog_r, alog_c, d_r, hprev, dy))
    return outs + [riding]


FOX_PAIRS = FOX_H // 2
FOX_SCALE = FOX_HD ** -0.5
NEG_INF = -jnp.inf


def _fgate_fwd(f_t, b_c, *, name):
    hh, s = f_t.shape
    tb = _pick(s, 512)
    nb = s // tb

    def body(f_ref, b_ref, o_ref, carry):
        t = pl.program_id(0)

        @pl.when(t == 0)
        def _():
            carry[...] = jnp.zeros_like(carry)

        lf = -_softplus(-(f_ref[...] + b_ref[...]))
        row = lax.broadcasted_iota(jnp.int32, (tb, tb), 0)
        col = lax.broadcasted_iota(jnp.int32, (tb, tb), 1)
        cum = _dotf(lf, (row <= col).astype(F32), pieces=3) + carry[:, 0:1]
        o_ref[...] = cum
        carry[:, 0:1] = cum[:, tb - 1:tb]

    return pl.pallas_call(
        body, name=name, grid=(nb,),
        in_specs=[pl.BlockSpec((hh, tb), lambda t: (0, t)), pl.BlockSpec((hh, 1), lambda t: (0, 0))],
        out_specs=pl.BlockSpec((hh, tb), lambda t: (0, t)),
        out_shape=jax.ShapeDtypeStruct((hh, s), F32),
        scratch_shapes=[pltpu.VMEM((hh, LANES), F32)],
        compiler_params=_cp(("arbitrary",)),
    )(f_t, b_c)


def _fgate_bwd(dcum_q_t, dcum_k_t, f_t, b_c, *, name):
    hh, s = f_t.shape
    tb = _pick(s, 512)
    nb = s // tb

    def body(dq_ref, d_ref, f_ref, b_ref, df_ref, db_ref, carry):
        t = pl.program_id(0)

        @pl.when(t == 0)
        def _():
            carry[...] = jnp.zeros_like(carry)
            db_ref[...] = jnp.zeros_like(db_ref)

        d = d_ref[...] + dq_ref[...]
        row = lax.broadcasted_iota(jnp.int32, (tb, tb), 0)
        col = lax.broadcasted_iota(jnp.int32, (tb, tb), 1)
        rev = _dotf(d, (row >= col).astype(F32), pieces=3) + carry[:, 0:1]
        df = rev * _sigmoid(-(f_ref[...] + b_ref[...]))
        df_ref[...] = df
        db_ref[...] += jnp.sum(df, axis=1, keepdims=True)
        carry[:, 0:1] = rev[:, 0:1]

    blk = pl.BlockSpec((hh, tb), lambda t: (0, nb - 1 - t))
    return pl.pallas_call(
        body, name=name, grid=(nb,),
        in_specs=[blk, blk, blk, pl.BlockSpec((hh, 1), lambda t: (0, 0))],
        out_specs=[blk, pl.BlockSpec((hh, 1), lambda t: (0, 0))],
        out_shape=[jax.ShapeDtypeStruct((hh, s), F32), jax.ShapeDtypeStruct((hh, 1), F32)],
        scratch_shapes=[pltpu.VMEM((hh, LANES), F32)],
        compiler_params=_cp(("arbitrary",)),
    )(dcum_q_t, dcum_k_t, f_t, b_c)


def _fox_tile(s):
    return min(512, max(s // 2, 8))


def _tri_tables(nq, kv_major):
    if kv_major:
        pairs = [(i, j) for j in range(nq) for i in range(j, nq)]
    else:
        pairs = [(i, j) for i in range(nq) for j in range(i + 1)]
    return (jnp.asarray([p[0] for p in pairs], jnp.int32), jnp.asarray([p[1] for p in pairs], jnp.int32))


def _lane_tile(col, width):
    return col if width == LANES else jnp.tile(col, (1, width // LANES))


def _flash_fwd(qs, kn, qkvg, ck, *, name, rider=None):
    s = qs.shape[0]
    tt = _fox_tile(s)
    nq = s // tt
    itab, jtab = _tri_tables(nq, kv_major=False)
    v0 = 2 * FOX_D // LANES

    def body(itab_ref, jtab_ref, q_ref, k_ref, v_ref, ck_ref, o_ref, lse_ref, m_sc, l_sc, acc_sc):
        t = pl.program_id(1)
        i, j = itab_ref[t], jtab_ref[t]

        @pl.when(j == 0)
        def _():
            m_sc[...] = jnp.full_like(m_sc, NEG_INF)
            l_sc[...] = jnp.zeros_like(l_sc)
            acc_sc[...] = jnp.zeros_like(acc_sc)

        low = lax.broadcasted_iota(jnp.int32, (tt, LANES), 1) < FOX_HD

        def step(diagonal):
            q2, k2 = q_ref[...], k_ref[...]
            v2 = v_ref[...].astype(BF16)
            alphas, outs = [], []
            for hh in range(2):
                qh = jnp.where(low if hh == 0 else jnp.logical_not(low), q2, jnp.zeros_like(q2))
                sc = lax.dot_general(qh, k2, NT, preferred_element_type=F32) - ck_ref[0][hh:hh + 1, :]
                if diagonal:
                    row = lax.broadcasted_iota(jnp.int32, sc.shape, 0)
                    col = lax.broadcasted_iota(jnp.int32, sc.shape, 1)
                    sc = jnp.where(row >= col, sc, NEG_INF)
                m_prev = m_sc[hh]
                m_new = jnp.maximum(m_prev, jnp.max(sc, axis=1, keepdims=True))
                alpha = jnp.exp(m_prev - m_new)
                p = jnp.exp(sc - _lane_tile(m_new, tt))
                l_sc[hh] = alpha * l_sc[hh] + jnp.sum(p, axis=1, keepdims=True)
                m_sc[hh] = m_new
                alphas.append(alpha)
                outs.append(lax.dot_general(p.astype(BF16), v2, NN, preferred_element_type=F32))
            acc_sc[...] = jnp.where(low, alphas[0], alphas[1]) * acc_sc[...] + jnp.where(low, outs[0], outs[1])

        @pl.when(j < i)
        def _():
            step(False)

        @pl.when(j == i)
        def _():
            step(True)
            o_ref[...] = acc_sc[...] / jnp.where(low, l_sc[0], l_sc[1])
            lse_ref[0] = jnp.concatenate([m_sc[hh][:, 0:1] + jnp.log(l_sc[hh][:, 0:1]) for hh in range(2)], axis=1)

    outs, riding = _call(
        body, name=name, grid=(FOX_PAIRS, int(itab.shape[0])), prefetch=(itab, jtab),
        in_specs=[pl.BlockSpec((tt, LANES), lambda p, t, it, jt: (it[t], p)),
                  pl.BlockSpec((tt, LANES), lambda p, t, it, jt: (jt[t], p)),
                  pl.BlockSpec((tt, LANES), lambda p, t, it, jt: (jt[t], v0 + p)),
                  pl.BlockSpec((1, 2, tt), lambda p, t, it, jt: (p, 0, jt[t]))],
        out_specs=[pl.BlockSpec((tt, LANES), lambda p, t, it, jt: (it[t], p)),
                   pl.BlockSpec((1, tt, 2), lambda p, t, it, jt: (p, it[t], 0))],
        scratch_shapes=[pltpu.VMEM((2, tt, LANES), F32), pltpu.VMEM((2, tt, LANES), F32), pltpu.VMEM((tt, LANES), F32)],
        out_shape=[jax.ShapeDtypeStruct((s, FOX_D), F32), jax.ShapeDtypeStruct((FOX_PAIRS, s, 2), F32)],
        sem=("parallel", "arbitrary"), rider=rider, args=(qs, kn, qkvg, ck))
    return outs + [riding]


def _flash_bwd(qs, kn, qkvg, do, lse, delta, ck, *, name, rider=None):
    s = qs.shape[0]
    tt = _fox_tile(s)
    nq = s // tt
    itab, jtab = _tri_tables(nq, kv_major=True)
    nsteps = itab.shape[0]
    v0 = 2 * FOX_D // LANES

    def body(itab_ref, jtab_ref, q_ref, k_ref, v_ref, do_ref, lse_ref, dl_ref, ck_ref,
             dq_ref, dk_ref, dv_ref, dcq_ref, dck_ref, dq_sc, rs_sc, dk_sc, dv_sc, dc_sc):
        t = pl.program_id(1)
        i, j = itab_ref[t], jtab_ref[t]

        @pl.when(t == 0)
        def _():
            dq_sc[...] = jnp.zeros_like(dq_sc)
            rs_sc[...] = jnp.zeros_like(rs_sc)

        @pl.when(i == j)
        def _():
            dk_sc[...] = jnp.zeros_like(dk_sc)
            dv_sc[...] = jnp.zeros_like(dv_sc)
            dc_sc[...] = jnp.zeros_like(dc_sc)

        low = lax.broadcasted_iota(jnp.int32, (tt, LANES), 1) < FOX_HD
        rows = pl.ds(pl.multiple_of(i * tt, tt), tt)

        def step(diagonal):
            q2, k2 = q_ref[...], k_ref[...]
            v2 = v_ref[...].astype(BF16)
            do2 = do_ref[...].astype(BF16)
            dqs, dks, dvs = [], [], []
            for hh in range(2):
                sel = low if hh == 0 else jnp.logical_not(low)
                qh = jnp.where(sel, q2, jnp.zeros_like(q2))
                doh = jnp.where(sel, do2, jnp.zeros_like(do2))
                sc = lax.dot_general(qh, k2, NT, preferred_element_type=F32) - ck_ref[0][hh:hh + 1, :]
                if diagonal:
                    row = lax.broadcasted_iota(jnp.int32, sc.shape, 0)
                    col = lax.broadcasted_iota(jnp.int32, sc.shape, 1)
                    sc = jnp.where(row >= col, sc, NEG_INF)
                lse_b = jnp.broadcast_to(lse_ref[0][:, hh:hh + 1], (tt, LANES))
                dl_b = jnp.broadcast_to(dl_ref[0][:, hh:hh + 1], (tt, LANES))
                p = jnp.exp(sc - _lane_tile(lse_b, tt))
                dp = lax.dot_general(doh, v2, NT, preferred_element_type=F32)
                ds = p * (dp - _lane_tile(dl_b, tt))
                pb, dsb = p.astype(BF16), ds.astype(BF16)
                dvs.append(lax.dot_general(pb, do2, TN, preferred_element_type=F32))
                dks.append(lax.dot_general(dsb, q2, TN, preferred_element_type=F32))
                dqs.append(lax.dot_general(dsb, k2, NN, preferred_element_type=F32))
                rs_sc[hh, rows, :] += jnp.sum(ds, axis=1, keepdims=True)
                dc_sc[hh] += jnp.sum(ds, axis=0, keepdims=True)
            dv_sc[...] += jnp.where(low, dvs[0], dvs[1])
            dk_sc[...] += jnp.where(low, dks[0], dks[1])
            dq_sc[rows, :] += jnp.where(low, dqs[0], dqs[1])

        @pl.when(j < i)
        def _():
            step(False)

        @pl.when(j == i)
        def _():
            step(True)

        @pl.when(i == nq - 1)
        def _():
            dk_ref[...] = dk_sc[...]
            dv_ref[...] = dv_sc[...]
            dck_ref[0] = -jnp.concatenate([dc_sc[hh] for hh in range(2)], axis=0)

        @pl.when(t == nsteps - 1)
        def _():
            dq_ref[...] = dq_sc[...] * FOX_SCALE
            dcq_ref[0] = jnp.concatenate([rs_sc[hh] for hh in range(2)], axis=1)

    qside = pl.BlockSpec((tt, LANES), lambda p, t, it, jt: (it[t], p))
    kside = pl.BlockSpec((tt, LANES), lambda p, t, it, jt: (jt[t], p))
    stat = pl.BlockSpec((1, tt, 2), lambda p, t, it, jt: (p, it[t], 0))
    ckspec = pl.BlockSpec((1, 2, tt), lambda p, t, it, jt: (p, 0, jt[t]))
    outs, riding = _call(
        body, name=name, grid=(FOX_PAIRS, nsteps), prefetch=(itab, jtab),
        in_specs=[qside, kside, pl.BlockSpec((tt, LANES), lambda p, t, it, jt: (jt[t], v0 + p)), qside, stat, stat, ckspec],
        out_specs=[pl.BlockSpec((s, LANES), lambda p, t, it, jt: (0, p)), kside, kside,
                   pl.BlockSpec((1, s, 2), lambda p, t, it, jt: (p, 0, 0)), ckspec],
        scratch_shapes=[pltpu.VMEM((s, LANES), F32), pltpu.VMEM((2, s, 1), F32), pltpu.VMEM((tt, LANES), F32),
                        pltpu.VMEM((tt, LANES), F32), pltpu.VMEM((2, 1, tt), F32)],
        out_shape=[jax.ShapeDtypeStruct((s, FOX_D), F32), jax.ShapeDtypeStruct((s, FOX_D), F32),
                   jax.ShapeDtypeStruct((s, FOX_D), F32), jax.ShapeDtypeStruct((FOX_PAIRS, s, 2), F32),
                   jax.ShapeDtypeStruct((FOX_PAIRS, 2, s), F32)],
        sem=("parallel", "arbitrary"), rider=rider, args=(qs, kn, qkvg, do, lse, delta, ck))
    return outs + [riding]


def _ogate_fwd(o, qkvg, *, name):
    s = o.shape[0]
    tr = _pick(s, 512, 8)

    def body(o_ref, g_ref, out_ref):
        out_ref[...] = (o_ref[...] * _sigmoid(g_ref[...])).astype(BF16)

    tile = pl.BlockSpec((tr, FOX_D), lambda i: (i, 0))
    return pl.pallas_call(
        body, name=name, grid=(s // tr,), in_specs=[tile, pl.BlockSpec((tr, FOX_D), lambda i: (i, 3))],
        out_specs=tile, out_shape=jax.ShapeDtypeStruct((s, FOX_D), BF16), compiler_params=_cp(("parallel",)),
    )(o, qkvg)


def _ogate_bwd(dog, o, qkvg, *, name):
    s = o.shape[0]
    tr = _pick(s, 512, 8)

    def body(dog_ref, o_ref, g_ref, do_ref, dg_ref, dl_ref):
        sg = _sigmoid(g_ref[...])
        ov = o_ref[...]
        dog_v = dog_ref[...]
        do = dog_v * sg
        do_ref[...] = do
        dg_ref[...] = (dog_v * ov * sg * (1.0 - sg)).astype(BF16)
        dl_ref[...] = _dotf(do * ov, _group_matrix(FOX_D, FOX_HD))

    tile = pl.BlockSpec((tr, FOX_D), lambda i: (i, 0))
    return pl.pallas_call(
        body, name=name, grid=(s // tr,), in_specs=[tile, tile, pl.BlockSpec((tr, FOX_D), lambda i: (i, 3))],
        out_specs=[tile, tile, pl.BlockSpec((tr, FOX_H), lambda i: (i, 0))],
        out_shape=[jax.ShapeDtypeStruct((s, FOX_D), F32), jax.ShapeDtypeStruct((s, FOX_D), BF16),
                   jax.ShapeDtypeStruct((s, FOX_H), F32)],
        compiler_params=_cp(("parallel",)),
    )(dog, o, qkvg)


def _loss_head(h, g, target, *, name):
    s, d = h.shape
    tr = _pick(s, 512, 8)

    def body(h_ref, g_ref, t_ref, loss_ref, dh_ref, dg_ref):
        i = pl.program_id(0)
        x = h_ref[...]
        gv = g_ref[...]
        r = lax.rsqrt(jnp.mean(x * x, axis=-1, keepdims=True) + EPS)
        xh = x * r
        err = xh * gv - t_ref[...]
        part = 0.5 * jnp.sum(jnp.sum(err * err, axis=1, keepdims=True) * (1.0 / d), axis=0, keepdims=True)
        dy = err * (1.0 / d)
        dyg = dy * gv
        dh_ref[...] = r * (dyg - xh * jnp.mean(dyg * xh, axis=-1, keepdims=True))
        dgp = jnp.sum(dy * xh, axis=0, keepdims=True)

        @pl.when(i == 0)
        def _():
            loss_ref[...] = jnp.zeros_like(loss_ref) + part
            dg_ref[...] = dgp

        @pl.when(i > 0)
        def _():
            loss_ref[...] += part
            dg_ref[...] += dgp

    tile = pl.BlockSpec((tr, d), lambda i: (i, 0))
    vec = pl.BlockSpec((1, d), lambda i: (0, 0))
    return pl.pallas_call(
        body, name=name, grid=(s // tr,), in_specs=[tile, vec, tile],
        out_specs=[pl.BlockSpec((1, LANES), lambda i: (0, 0)), tile, vec],
        out_shape=[jax.ShapeDtypeStruct((1, LANES), F32), jax.ShapeDtypeStruct((s, d), F32),
                   jax.ShapeDtypeStruct((1, d), F32)],
        compiler_params=_cp(("arbitrary",)),
    )(h, g, target)


def _adamw(w, g, m, v, *, name):
    rows, cols = w.shape
    tr = _pick(rows, 256, 8)
    c1 = 1.0 - ADAM_B1 ** ADAM_STEP
    c2 = 1.0 - ADAM_B2 ** ADAM_STEP

    def body(w_ref, g_ref, m_ref, v_ref, d_ref, nm_ref, nv_ref):
        gv = g_ref[...]
        nm = ADAM_B1 * m_ref[...] + (1.0 - ADAM_B1) * gv
        nv = ADAM_B2 * v_ref[...] + (1.0 - ADAM_B2) * (gv * gv)
        d_ref[...] = -ADAM_LR * ((nm / c1) / (jnp.sqrt(nv / c2) + ADAM_EPS) + ADAM_WD * w_ref[...])
        nm_ref[...] = nm
        nv_ref[...] = nv

    tile = pl.BlockSpec((tr, cols), lambda i: (i, 0))
    shp = jax.ShapeDtypeStruct((rows, cols), F32)
    return pl.pallas_call(
        body, name=name, grid=(rows // tr,), in_specs=[tile] * 4, out_specs=[tile] * 3, out_shape=[shp] * 3,
        compiler_params=_cp(("parallel",)),
    )(w, g, m, v)


ANY = pl.BlockSpec(memory_space=pl.ANY)
N_DEV = 8


def _coords():
    return lax.axis_index("x"), lax.axis_index("y"), lax.axis_index("c")


def _other_chips(x, y):
    return [(1 - x, y), (x, 1 - y), (1 - x, 1 - y)]


def _allgather_small(buf, *, name, with_sum):
    rows = buf.shape[0]

    def body(*refs):
        if with_sum:
            x_ref, out_ref, sum_ref, send_sems, recv_sems = refs
        else:
            x_ref, out_ref, send_sems, recv_sems = refs
        x, y, c = _coords()
        me = 4 * x + 2 * y + c
        out_ref[me] = x_ref[...]
        copies = []
        for rel in range(1, N_DEV):
            px = (1 - x) if rel & 4 else x
            py = (1 - y) if rel & 2 else y
            pc = (1 - c) if rel & 1 else c
            cp = pltpu.make_async_remote_copy(
                src_ref=x_ref, dst_ref=out_ref.at[me], send_sem=send_sems.at[rel - 1], recv_sem=recv_sems.at[rel - 1],
                device_id=(px, py, pc), device_id_type=MESH)
            cp.start()
            copies.append(cp)
        for cp in copies:
            cp.wait()
        if with_sum:
            acc = out_ref[0]
            for k in range(1, N_DEV):
                acc = acc + out_ref[k]
            sum_ref[...] = acc

    slots = jax.ShapeDtypeStruct((N_DEV, rows, LANES), F32)
    vm = pl.BlockSpec(memory_space=pltpu.VMEM)
    out_shape = [slots, jax.ShapeDtypeStruct((rows, LANES), F32)] if with_sum else [slots]
    return pl.pallas_call(
        body, name=name, in_specs=[vm], out_specs=[vm] * len(out_shape), out_shape=out_shape,
        scratch_shapes=[pltpu.SemaphoreType.DMA((N_DEV - 1,)), pltpu.SemaphoreType.DMA((N_DEV - 1,))],
    )(buf)


class _Gather:
    per_array = 6

    def __init__(self, arrays):
        self.arrays = list(arrays)

    def out_shapes(self):
        return [jax.ShapeDtypeStruct((4,) + a.shape, a.dtype) for a in self.arrays]

    @staticmethod
    def _ici(ins, outs, send_sems, recv_sems, t, j, px, py, c, slot):
        return pltpu.make_async_remote_copy(
            src_ref=ins[t].at[c], dst_ref=outs[t].at[slot, c], send_sem=send_sems.at[6 * t + j],
            recv_sem=recv_sems.at[6 * t + j], device_id=(px, py, c), device_id_type=MESH)

    @staticmethod
    def _d2d(outs, send_sems, recv_sems, t, j, kj, half, sibling):
        return pltpu.make_async_remote_copy(
            src_ref=outs[t].at[kj, half], dst_ref=outs[t].at[kj, half], send_sem=send_sems.at[6 * t + 3 + j],
            recv_sem=recv_sems.at[6 * t + 3 + j], device_id=sibling, device_id_type=MESH)

    def start(self, ins, outs, send_sems, recv_sems):
        x, y, c = _coords()
        for t in range(len(ins)):
            for j, (px, py) in enumerate(_other_chips(x, y)):
                self._ici(ins, outs, send_sems, recv_sems, t, j, px, py, c, 2 * x + y).start()

    def finish(self, ins, outs, send_sems, recv_sems):
        x, y, c = _coords()
        chips = _other_chips(x, y)
        sibling = (x, y, 1 - c)
        started = []
        for t in range(len(ins)):
            for j, (px, py) in enumerate(chips):
                ici = self._ici(ins, outs, send_sems, recv_sems, t, j, px, py, c, 2 * px + py)
                ici.wait_recv()
                fwd = self._d2d(outs, send_sems, recv_sems, t, j, 2 * px + py, c, sibling)
                fwd.start()
                started += [ici, fwd]
        for t in range(len(ins)):
            for j, (px, py) in enumerate(chips):
                self._d2d(outs, send_sems, recv_sems, t, j, 2 * px + py, 1 - c, sibling).wait_recv()
        for cp in started:
            cp.wait_send()


class _Exchange:
    per_array = 7

    def __init__(self, arrays):
        self.arrays = list(arrays)

    def out_shapes(self):
        return [jax.ShapeDtypeStruct((7,) + a.shape[2:], a.dtype) for a in self.arrays]

    @staticmethod
    def _copies(ins, outs, send_sems, recv_sems):
        x, y, c = _coords()
        for t in range(len(ins)):
            for rel in range(1, N_DEV):
                px = (1 - x) if rel & 4 else x
                py = (1 - y) if rel & 2 else y
                pc = (1 - c) if rel & 1 else c
                yield pltpu.make_async_remote_copy(
                    src_ref=ins[t].at[2 * px + py, pc], dst_ref=outs[t].at[rel - 1], send_sem=send_sems.at[7 * t + rel - 1],
                    recv_sem=recv_sems.at[7 * t + rel - 1], device_id=(px, py, pc), device_id_type=MESH)

    def start(self, ins, outs, send_sems, recv_sems):
        for cp in self._copies(ins, outs, send_sems, recv_sems):
            cp.start()

    def finish(self, ins, outs, send_sems, recv_sems):
        for cp in self._copies(ins, outs, send_sems, recv_sems):
            cp.wait()


def _call(body, *, name, grid, in_specs, out_specs, out_shape, scratch_shapes, args, sem, rider=None, prefetch=()):
    n_in, n_out, n_pre = len(in_specs), len(out_specs), len(prefetch)
    n_c = len(rider.arrays) if rider is not None else 0

    def wrapped(*refs):
        pre, rest = refs[:n_pre], refs[n_pre:]
        ins, cins = rest[:n_in], rest[n_in:n_in + n_c]
        outs = rest[n_in + n_c:n_in + n_c + n_out]
        couts = rest[n_in + n_c + n_out:n_in + 2 * n_c + n_out]
        scratch = rest[n_in + 2 * n_c + n_out:]
        if rider is None:
            body(*pre, *ins, *outs, *scratch)
            return
        send_sems, recv_sems = scratch[-2:]
        ids = [pl.program_id(a) for a in range(len(grid))]
        first = functools.reduce(jnp.logical_and, [i == 0 for i in ids])
        last = functools.reduce(jnp.logical_and, [i == g - 1 for i, g in zip(ids, grid)])

        @pl.when(first)
        def _():
            rider.start(cins, couts, send_sems, recv_sems)

        body(*pre, *ins, *outs, *scratch[:-2])

        @pl.when(last)
        def _():
            rider.finish(cins, couts, send_sems, recv_sems)

    if rider is not None:
        nsem = rider.per_array * n_c
        in_specs = list(in_specs) + [ANY] * n_c
        out_specs = list(out_specs) + [ANY] * n_c
        out_shape = list(out_shape) + rider.out_shapes()
        scratch_shapes = list(scratch_shapes) + [pltpu.SemaphoreType.DMA((nsem,)), pltpu.SemaphoreType.DMA((nsem,))]
        args = list(args) + rider.arrays
        sem = ("arbitrary",) * len(grid)
    if n_pre:
        res = pl.pallas_call(
            wrapped, name=name, out_shape=out_shape, compiler_params=_cp(sem),
            grid_spec=pltpu.PrefetchScalarGridSpec(num_scalar_prefetch=n_pre, grid=grid, in_specs=in_specs,
                                                   out_specs=out_specs, scratch_shapes=scratch_shapes),
        )(*prefetch, *args)
    else:
        res = pl.pallas_call(
            wrapped, name=name, grid=grid, in_specs=in_specs, out_specs=out_specs, out_shape=out_shape,
            scratch_shapes=scratch_shapes, compiler_params=_cp(sem),
        )(*args)
    return list(res[:n_out]), list(res[n_out:])


def _run_rider(rider, *, name):
    n = len(rider.arrays)

    def body(*refs):
        ins, outs = refs[:n], refs[n:2 * n]
        send_sems, recv_sems = refs[2 * n:]
        rider.start(ins, outs, send_sems, recv_sems)
        rider.finish(ins, outs, send_sems, recv_sems)

    nsem = rider.per_array * n
    return pl.pallas_call(
        body, name=name, in_specs=[ANY] * n, out_specs=[ANY] * n, out_shape=rider.out_shapes(),
        scratch_shapes=[pltpu.SemaphoreType.DMA((nsem,)), pltpu.SemaphoreType.DMA((nsem,))],
    )(*rider.arrays)


def _sibling_swap(arrs, *, name):
    n = len(arrs)

    def body(*refs):
        ins, outs = refs[:n], refs[n:2 * n]
        send_sems, recv_sems = refs[2 * n:]
        x, y, c = _coords()
        copies = []
        for t in range(n):
            cp = pltpu.make_async_remote_copy(
                src_ref=ins[t], dst_ref=outs[t], send_sem=send_sems.at[t], recv_sem=recv_sems.at[t],
                device_id=(x, y, 1 - c), device_id_type=MESH)
            cp.start()
            copies.append(cp)
        for cp in copies:
            cp.wait()

    return pl.pallas_call(
        body, name=name, in_specs=[ANY] * n, out_specs=[ANY] * n,
        out_shape=[jax.ShapeDtypeStruct(a.shape, a.dtype) for a in arrs],
        scratch_shapes=[pltpu.SemaphoreType.DMA((n,)), pltpu.SemaphoreType.DMA((n,))],
    )(*arrs)


def _add_selected(stack, others, sel, *, name):
    _, m, cols = stack.shape
    q = others.shape[0]
    tr = _pick(m, 256, 16)

    def body(sel_ref, s_ref, o_ref, out_ref):
        acc = s_ref[0].astype(F32)
        for i in range(q):
            acc = acc + o_ref[i].astype(F32)
        out_ref[...] = acc

    return pl.pallas_call(
        body, name=name,
        grid_spec=pltpu.PrefetchScalarGridSpec(
            num_scalar_prefetch=1, grid=(m // tr,),
            in_specs=[pl.BlockSpec((1, tr, cols), lambda i, sel_ref: (sel_ref[0], i, 0)),
                      pl.BlockSpec((q, tr, cols), lambda i, sel_ref: (0, i, 0))],
            out_specs=pl.BlockSpec((tr, cols), lambda i, sel_ref: (i, 0))),
        out_shape=jax.ShapeDtypeStruct((m, cols), F32),
        compiler_params=_cp(("parallel",)),
    )(sel, stack, others)


BIG = ("ssd_w_in", "ssd_w_out", "fox_w_in", "fox_w_out", "ffn_w_up", "ffn_w_down")
COL_SHARDED = ("ssd_w_in", "fox_w_in", "ffn_w_up")
SMALL = (("mix_norm_g", (4, 1024)), ("ffn_norm_g", (4, 1024)), ("ssd_conv_w", (2, 4, 3072)), ("ssd_conv_b", (2, 3072)),
         ("ssd_dt_bias", (2, 32)), ("ssd_a_log", (2, 32)), ("ssd_d", (2, 32)), ("ssd_norm_g", (2, 2048)),
         ("fox_b_f", (2, 16)), ("fox_q_norm_g", (2, 64)), ("fox_k_norm_g", (2, 64)), ("ffn_conv_w", (4, 3, 2816)),
         ("ffn_conv_b", (4, 2816)), ("final_norm_g", (1024,)), ("loss", (1,)))
NAMES = ("mix_norm_g", "ffn_norm_g", "ssd_w_in", "ssd_conv_w", "ssd_conv_b", "ssd_dt_bias", "ssd_a_log", "ssd_d",
         "ssd_norm_g", "ssd_w_out", "fox_w_in", "fox_b_f", "fox_q_norm_g", "fox_k_norm_g", "fox_w_out", "ffn_w_up",
         "ffn_conv_w", "ffn_conv_b", "ffn_w_down", "final_norm_g")


def _pack(parts):
    flat = jnp.concatenate([jnp.reshape(p, (-1,)).astype(F32) for p in parts])
    rows = -(-flat.shape[0] // (8 * LANES)) * 8
    return jnp.pad(flat, (0, rows * LANES - flat.shape[0])).reshape(rows, LANES)


def _unpack(buf, shapes):
    flat = buf.reshape(-1)
    out, off = [], 0
    for shp in shapes:
        size = 1
        for d in shp:
            size *= d
        out.append(flat[off:off + size].reshape(shp))
        off += size
    return out


def _pad_lanes(a):
    return jnp.pad(a, ((0, 0), (0, LANES - a.shape[1])))


def _pad8(w):
    return jnp.pad(w, ((0, 8 - w.shape[0]), (0, 0)))


def _ssd_forward(h, p, name, rider=None):
    s = h.shape[0]
    hn = _rms_fwd(h, p["mix_g"], gw=D_MODEL, ncol=1, name=f"{name}_norm")
    zx = _matmul(hn, p["w_zx"], mode="nn", name=f"{name}_proj")
    dtp = _matmul(hn, p["w_dt"], mode="nn", name=f"{name}_proj_dt")
    xbc, _ = _conv_fwd(zx, p["conv_w8"], p["conv_b"], kw=SSD_K, width=SSD_CONV_DIM, u_col0=SSD_DI, name=f"{name}_conv")
    dt3 = dtp[:, :SSD_H].reshape(s, SSD_G, SSD_HPG)
    dtg, dtg_t = jnp.transpose(dt3, (1, 0, 2)), jnp.transpose(dt3, (1, 2, 0))
    sp = (p["bias_r"], p["bias_c"], p["alog_r"], p["alog_c"], p["d_r"])
    y, hprev, riding = _ssd_fwd(xbc, dtg, dtg_t, *sp, name=f"{name}_scan", rider=rider)
    y2 = _rms_fwd(y, p["norm_g"], gw=SSD_DI // SSD_G, ncol=SSD_G, z=zx, name=f"{name}_gnorm")
    out = _matmul(y2, p["w_out"], mode="nn", add=h, name=f"{name}_out")
    return out, dict(h=h, hn=hn, zx=zx, xbc=xbc, dtg=dtg, dtg_t=dtg_t, y=y, hprev=hprev, y2=y2), riding


def _ssd_backward(dh1, p, a, name, rider=None):
    s = dh1.shape[0]
    g = {}
    dy2 = _matmul(dh1, p["w_out"], mode="nt", name=f"{name}_out_dx")
    g["w_out"] = _matmul(a["y2"], dh1, mode="tn", out_dtype=BF16, name=f"{name}_out_dw")
    dy, dz, g["norm_g"] = _rms_bwd(a["y"], p["norm_g"], dy2, gw=SSD_DI // SSD_G, ncol=SSD_G, z=a["zx"], name=f"{name}_gnorm_b")
    sp = (p["bias_r"], p["bias_c"], p["alog_r"], p["alog_c"], p["d_r"])
    dx, dbm, dcm, ddt, g["dt_bias"], g["a_log"], g["d"], riding = _ssd_bwd(
        a["xbc"], a["dtg"], a["dtg_t"], *sp, a["hprev"], dy, name=f"{name}_scan_b", rider=rider)
    dact = jnp.concatenate([dx, dbm, dcm], axis=1)
    dxbc, dwb, _ = _conv_bwd(a["zx"], p["conv_w8"], p["conv_b"], dact, kw=SSD_K, width=SSD_CONV_DIM, u_col0=SSD_DI,
                             name=f"{name}_conv_b")
    g["conv_w"], g["conv_b"] = dwb[:SSD_K], dwb[7]
    dzx = jnp.concatenate([dz.astype(BF16), dxbc], axis=1)
    ddtp = _pad_lanes(jnp.transpose(ddt, (1, 0, 2)).reshape(s, SSD_H))
    dhn = _matmul(dzx, p["w_zx"], mode="nt", name=f"{name}_proj_dx")
    dhn = _matmul(ddtp, p["w_dt"], mode="nt", add=dhn, name=f"{name}_proj_dt_dx")
    dw_zx = _matmul(a["hn"], dzx, mode="tn", out_dtype=BF16, name=f"{name}_proj_dw")
    dw_dt = _matmul(a["hn"], ddtp, mode="tn", out_dtype=BF16, name=f"{name}_proj_dt_dw")
    g["w_in"] = jnp.concatenate([dw_zx, dw_dt[:, :SSD_H]], axis=1)
    dh, g["mix_g"] = _rms_bwd(a["h"], p["mix_g"], dhn, gw=D_MODEL, ncol=1, add=dh1, name=f"{name}_norm_b")
    return dh, g, riding


def _fox_forward(h, p, name, rider=None):
    s = h.shape[0]
    hn = _rms_fwd(h, p["mix_g"], gw=D_MODEL, ncol=1, name=f"{name}_norm")
    qkvg = _matmul(hn, p["w_qkvg"], mode="nn", name=f"{name}_proj")
    fp = _matmul(hn, p["w_f"], mode="nn", name=f"{name}_proj_f")
    qs = _rms_fwd(qkvg, p["gq"] * FOX_SCALE, gw=FOX_D, ncol=1, x_col0=0, sub=FOX_HD, name=f"{name}_qnorm")
    kn = _rms_fwd(qkvg, p["gk"], gw=FOX_D, ncol=1, x_col0=1, sub=FOX_HD, name=f"{name}_knorm")
    f_t = jnp.transpose(fp[:, :FOX_H])
    cum_t = _fgate_fwd(f_t, p["b_f"], name=f"{name}_fgate")
    ck = cum_t.reshape(FOX_PAIRS, 2, s)
    o, lse, riding = _flash_fwd(qs, kn, qkvg, ck, name=f"{name}_attn", rider=rider)
    og = _ogate_fwd(o, qkvg, name=f"{name}_ogate")
    out = _matmul(og, p["w_out"], mode="nn", add=h, name=f"{name}_out")
    return out, dict(h=h, hn=hn, qkvg=qkvg, qs=qs, kn=kn, f_t=f_t, ck=ck, o=o, lse=lse, og=og), riding


def _fox_backward(dh1, p, a, name, rider=None):
    s = dh1.shape[0]
    g = {}
    dog = _matmul(dh1, p["w_out"], mode="nt", name=f"{name}_out_dx")
    g["w_out"] = _matmul(a["og"], dh1, mode="tn", out_dtype=BF16, name=f"{name}_out_dw")
    do, dgate, delta = _ogate_bwd(dog, a["o"], a["qkvg"], name=f"{name}_ogate_b")
    dl = jnp.transpose(delta.reshape(s, FOX_PAIRS, 2), (1, 0, 2))
    dq, dk, dv, dcq, dck, riding = _flash_bwd(a["qs"], a["kn"], a["qkvg"], do, a["lse"], dl, a["ck"], name=f"{name}_attn_b",
                                              rider=rider)
    dq_raw, dgq = _rms_bwd(a["qkvg"], p["gq"], dq, gw=FOX_D, ncol=1, x_col0=0, sub=FOX_HD, dx_dtype=BF16, name=f"{name}_qnorm_b")
    dk_raw, dgk = _rms_bwd(a["qkvg"], p["gk"], dk, gw=FOX_D, ncol=1, x_col0=1, sub=FOX_HD, dx_dtype=BF16, name=f"{name}_knorm_b")
    g["gq"] = dgq.reshape(FOX_H, FOX_HD).sum(axis=0)
    g["gk"] = dgk.reshape(FOX_H, FOX_HD).sum(axis=0)
    dcq_t = jnp.transpose(dcq, (0, 2, 1)).reshape(FOX_H, s)
    df_t, dbf = _fgate_bwd(dcq_t, dck.reshape(FOX_H, s), a["f_t"], p["b_f"], name=f"{name}_fgate_b")
    g["b_f"] = dbf[:, 0]
    dproj = jnp.concatenate([dq_raw, dk_raw, dv.astype(BF16), dgate], axis=1)
    dfp = _pad_lanes(jnp.transpose(df_t))
    dhn = _matmul(dproj, p["w_qkvg"], mode="nt", name=f"{name}_proj_dx")
    dhn = _matmul(dfp, p["w_f"], mode="nt", add=dhn, name=f"{name}_proj_f_dx")
    dw_qkvg = _matmul(a["hn"], dproj, mode="tn", out_dtype=BF16, name=f"{name}_proj_dw")
    dw_f = _matmul(a["hn"], dfp, mode="tn", out_dtype=BF16, name=f"{name}_proj_f_dw")
    g["w_in"] = jnp.concatenate([dw_qkvg, dw_f[:, :FOX_H]], axis=1)
    dh, g["mix_g"] = _rms_bwd(a["h"], p["mix_g"], dhn, gw=D_MODEL, ncol=1, add=dh1, name=f"{name}_norm_b")
    return dh, g, riding


def _ffn_forward(h, p, name, rider=None):
    hn = _rms_fwd(h, p["ffn_g"], gw=D_MODEL, ncol=1, name=f"{name}_norm")
    u = _matmul(hn, p["w_up"], mode="nn", name=f"{name}_up")
    act, riding = _conv_fwd(u, p["conv_w8"], p["conv_b"], kw=FFN_K, width=D_FF, u_col0=0, mul_col0=D_FF, out_dtype=BF16,
                            name=f"{name}_glu", rider=rider)
    out = _matmul(act, p["w_down"], mode="nn", add=h, name=f"{name}_down")
    return out, dict(h=h, hn=hn, u=u, act=act), riding


def _ffn_backward(dh2, p, a, name, rider=None):
    g = {}
    dact = _matmul(dh2, p["w_down"], mode="nt", name=f"{name}_down_dx")
    g["w_down"] = _matmul(a["act"], dh2, mode="tn", out_dtype=BF16, name=f"{name}_down_dw")
    du1, du2, dwb, riding = _conv_bwd(a["u"], p["conv_w8"], p["conv_b"], dact, kw=FFN_K, width=D_FF, u_col0=0,
                                      mul_col0=D_FF, name=f"{name}_glu_b", rider=rider)
    g["conv_w"], g["conv_b"] = dwb[:FFN_K], dwb[7]
    du = jnp.concatenate([du1, du2], axis=1)
    dhn = _matmul(du, p["w_up"], mode="nt", name=f"{name}_up_dx")
    g["w_up"] = _matmul(a["hn"], du, mode="tn", out_dtype=BF16, name=f"{name}_up_dw")
    dh, g["ffn_g"] = _rms_bwd(a["h"], p["ffn_g"], dhn, gw=D_MODEL, ncol=1, add=dh2, name=f"{name}_norm_b")
    return dh, g, riding


def _to_slabs(dw, col_sharded):
    rows, cols = dw.shape
    if col_sharded:
        return jnp.transpose(dw.reshape(rows, 4, cols // 4), (1, 0, 2)).reshape(4, 2, rows // 2, cols // 4)
    return dw.reshape(4, 2, rows // 8, cols)


def kernel(x, mix_norm_g, ffn_norm_g, ssd_w_in, ssd_conv_w, ssd_conv_b, ssd_dt_bias, ssd_a_log, ssd_d, ssd_norm_g, ssd_w_out, fox_w_in, fox_b_f, fox_q_norm_g, fox_k_norm_g, fox_w_out, ffn_w_up, ffn_conv_w, ffn_conv_b, ffn_w_down, final_norm_g, loss_target, m_mix_norm_g, m_ffn_norm_g, m_ssd_w_in, m_ssd_conv_w, m_ssd_conv_b, m_ssd_dt_bias, m_ssd_a_log, m_ssd_d, m_ssd_norm_g, m_ssd_w_out, m_fox_w_in, m_fox_b_f, m_fox_q_norm_g, m_fox_k_norm_g, m_fox_w_out, m_ffn_w_up, m_ffn_conv_w, m_ffn_conv_b, m_ffn_w_down, m_final_norm_g, v_mix_norm_g, v_ffn_norm_g, v_ssd_w_in, v_ssd_conv_w, v_ssd_conv_b, v_ssd_dt_bias, v_ssd_a_log, v_ssd_d, v_ssd_norm_g, v_ssd_w_out, v_fox_w_in, v_fox_b_f, v_fox_q_norm_g, v_fox_k_norm_g, v_fox_w_out, v_ffn_w_up, v_ffn_conv_w, v_ffn_conv_b, v_ffn_w_down, v_final_norm_g):
    w = dict(mix_norm_g=mix_norm_g, ffn_norm_g=ffn_norm_g, ssd_w_in=ssd_w_in, ssd_conv_w=ssd_conv_w, ssd_conv_b=ssd_conv_b,
             ssd_dt_bias=ssd_dt_bias, ssd_a_log=ssd_a_log, ssd_d=ssd_d, ssd_norm_g=ssd_norm_g, ssd_w_out=ssd_w_out,
             fox_w_in=fox_w_in, fox_b_f=fox_b_f, fox_q_norm_g=fox_q_norm_g, fox_k_norm_g=fox_k_norm_g, fox_w_out=fox_w_out,
             ffn_w_up=ffn_w_up, ffn_conv_w=ffn_conv_w, ffn_conv_b=ffn_conv_b, ffn_w_down=ffn_w_down, final_norm_g=final_norm_g)
    m_in = dict(zip(NAMES, (m_mix_norm_g, m_ffn_norm_g, m_ssd_w_in, m_ssd_conv_w, m_ssd_conv_b, m_ssd_dt_bias, m_ssd_a_log,
                            m_ssd_d, m_ssd_norm_g, m_ssd_w_out, m_fox_w_in, m_fox_b_f, m_fox_q_norm_g, m_fox_k_norm_g,
                            m_fox_w_out, m_ffn_w_up, m_ffn_conv_w, m_ffn_conv_b, m_ffn_w_down, m_final_norm_g)))
    v_in = dict(zip(NAMES, (v_mix_norm_g, v_ffn_norm_g, v_ssd_w_in, v_ssd_conv_w, v_ssd_conv_b, v_ssd_dt_bias, v_ssd_a_log,
                            v_ssd_d, v_ssd_norm_g, v_ssd_w_out, v_fox_w_in, v_fox_b_f, v_fox_q_norm_g, v_fox_k_norm_g,
                            v_fox_w_out, v_ffn_w_up, v_ffn_conv_w, v_ffn_conv_b, v_ffn_w_down, v_final_norm_g)))
    cx, cy, cc = _coords()
    chip = 2 * cx + cy
    h = x[0]
    target = loss_target[0]

    conv_shapes = [ssd_conv_w.shape, ffn_conv_w.shape]
    slots = _allgather_small(_pack([ssd_conv_w, ffn_conv_w]), name="gather_conv_w", with_sum=False)[0]
    per_chip = [_unpack(slots[2 * q], conv_shapes) for q in range(4)]
    ssd_conv_full = jnp.concatenate([pc[0] for pc in per_chip], axis=2)
    ffn_conv_full = jnp.concatenate([pc[1] for pc in per_chip], axis=2)
    low = {n: w[n].astype(BF16) for n in BIG}
    sub_weights = dict(ssd=("ssd_w_in", "ssd_w_out"), fox=("fox_w_in", "fox_w_out"), ffn=("ffn_w_up", "ffn_w_down"))

    def shards_of(kind, idx):
        return [low[n][idx].reshape(2, low[n].shape[1] // 2, low[n].shape[2]) for n in sub_weights[kind]]

    def assemble(kind, idx, gathered):
        full = []
        for n, own, gth in zip(sub_weights[kind], shards_of(kind, idx), gathered):
            gth = lax.dynamic_update_slice(gth, own[None], (chip, 0, 0, 0))
            _, _, half, cols = gth.shape
            if n in COL_SHARDED:
                full.append(jnp.transpose(gth.reshape(4, 2 * half, cols), (1, 0, 2)).reshape(2 * half, 4 * cols))
            else:
                full.append(gth.reshape(8 * half, cols))
        return full

    def ssd_params(j, i, weights):
        w_in, w_out = weights
        g3 = lambda v: v.reshape(SSD_G, 1, SSD_HPG)
        g3c = lambda v: v.reshape(SSD_G, SSD_HPG, 1)
        return dict(mix_g=mix_norm_g[i][None], w_zx=w_in[:, :SSD_ZX], w_dt=_pad_lanes(w_in[:, SSD_ZX:]),
                    conv_w8=_pad8(ssd_conv_full[j]), conv_b=ssd_conv_b[j][None], bias_r=g3(ssd_dt_bias[j]),
                    bias_c=g3c(ssd_dt_bias[j]), alog_r=g3(ssd_a_log[j]), alog_c=g3c(ssd_a_log[j]), d_r=g3(ssd_d[j]),
                    norm_g=ssd_norm_g[j][None], w_out=w_out)

    def fox_params(j, i, weights):
        w_in, w_out = weights
        return dict(mix_g=mix_norm_g[i][None], w_qkvg=w_in[:, :4 * FOX_D], w_f=_pad_lanes(w_in[:, 4 * FOX_D:]),
                    gq=jnp.tile(fox_q_norm_g[j], FOX_H)[None], gk=jnp.tile(fox_k_norm_g[j], FOX_H)[None],
                    b_f=fox_b_f[j][:, None], w_out=w_out)

    def ffn_params(i, weights):
        w_up, w_down = weights
        return dict(ffn_g=ffn_norm_g[i][None], w_up=w_up, conv_w8=_pad8(ffn_conv_full[i]), conv_b=ffn_conv_b[i][None],
                    w_down=w_down)

    order = [("ssd", 0), ("ffn", 0), ("fox", 0), ("ffn", 1), ("ssd", 1), ("ffn", 2), ("fox", 1), ("ffn", 3)]
    fetch = {("ssd", 0): [("ffn", 0)], ("ffn", 0): [("fox", 0)], ("fox", 0): [("ffn", 1), ("ssd", 1), ("ffn", 2)],
             ("ssd", 1): [("fox", 1)], ("fox", 1): [("ffn", 3)]}
    ready = {("ssd", 0): assemble("ssd", 0, _run_rider(_Gather(shards_of("ssd", 0)), name="gather_first"))}
    params, acts = {}, {}
    forward = dict(ssd=_ssd_forward, fox=_fox_forward, ffn=_ffn_forward)
    for kind, idx in order:
        if kind == "ssd":
            params[kind, idx] = ssd_params(idx, 2 * idx, ready.pop((kind, idx)))
        elif kind == "fox":
            params[kind, idx] = fox_params(idx, 2 * idx + 1, ready.pop((kind, idx)))
        else:
            params[kind, idx] = ffn_params(idx, ready.pop((kind, idx)))
        wanted = fetch.get((kind, idx), [])
        rider = _Gather([s for sub in wanted for s in shards_of(*sub)]) if wanted else None
        h, acts[kind, idx], riding = forward[kind](h, params[kind, idx], f"{kind}{idx}", rider=rider)
        for q, sub in enumerate(wanted):
            ready[sub] = assemble(*sub, riding[2 * q:2 * q + 2])
    loss_part, dh, d_final_g = _loss_head(h, final_norm_g[None], target, name="loss_head")

    backward = dict(ssd=_ssd_backward, fox=_fox_backward, ffn=_ffn_backward)
    grad_keys = dict(ssd=("w_in", "w_out"), fox=("w_in", "w_out"), ffn=("w_up", "w_down"))
    sub_g, slabs, received = {}, {}, {}
    pending = None
    for kind, idx in reversed(order):
        rider = _Exchange(slabs[pending]) if pending is not None else None
        dh, sub_g[kind, idx], riding = backward[kind](dh, params[kind, idx], acts[kind, idx], f"{kind}{idx}", rider=rider)
        if pending is not None:
            received[pending] = riding
        slabs[kind, idx] = [_to_slabs(sub_g[kind, idx][key], n in COL_SHARDED)
                            for key, n in zip(grad_keys[kind], sub_weights[kind])]
        pending = (kind, idx)
    received[pending] = _run_rider(_Exchange(slabs[pending]), name="rs_last_exchange")
    grad_x = dh[None]
    ssd_g, fox_g = [sub_g["ssd", 0], sub_g["ssd", 1]], [sub_g["fox", 0], sub_g["fox", 1]]
    mix_g = [ssd_g[0], fox_g[0], ssd_g[1], fox_g[1]]
    ffn_g = [sub_g["ffn", i] for i in range(DEPTH)]

    me = jnp.reshape(2 * chip + cc, (1,)).astype(jnp.int32)
    finals = {}
    for sub in order:
        for q, (slab, got) in enumerate(zip(slabs[sub], received[sub])):
            _, _, m, cols = slab.shape
            finals[sub, q] = _add_selected(slab.reshape(8, m, cols), got, me, name=f"rs_add_{sub[0]}{sub[1]}_{q}")
    keys = list(finals)
    others = dict(zip(keys, _sibling_swap([finals[key] for key in keys], name="rs_result_swap")))
    grads = {}
    for kind, names in sub_weights.items():
        for q, n in enumerate(names):
            subs = [sub for sub in sorted(set(order)) if sub[0] == kind]
            mine = jnp.stack([finals[sub, q] for sub in subs])
            theirs = jnp.stack([others[sub, q] for sub in subs])
            halves = jnp.stack([jnp.where(cc == 0, mine, theirs), jnp.where(cc == 0, theirs, mine)], axis=1)
            grads[n] = halves.reshape(w[n].shape)
    small = dict(
        mix_norm_g=jnp.concatenate([g["mix_g"] for g in mix_g], axis=0),
        ffn_norm_g=jnp.concatenate([g["ffn_g"] for g in ffn_g], axis=0),
        ssd_conv_w=jnp.stack([g["conv_w"] for g in ssd_g]), ssd_conv_b=jnp.stack([g["conv_b"] for g in ssd_g]),
        ssd_dt_bias=jnp.stack([g["dt_bias"].reshape(SSD_H) for g in ssd_g]),
        ssd_a_log=jnp.stack([g["a_log"].reshape(SSD_H) for g in ssd_g]),
        ssd_d=jnp.stack([g["d"].reshape(SSD_H) for g in ssd_g]),
        ssd_norm_g=jnp.concatenate([g["norm_g"] for g in ssd_g], axis=0),
        fox_b_f=jnp.stack([g["b_f"] for g in fox_g]), fox_q_norm_g=jnp.stack([g["gq"] for g in fox_g]),
        fox_k_norm_g=jnp.stack([g["gk"] for g in fox_g]),
        ffn_conv_w=jnp.stack([g["conv_w"] for g in ffn_g]), ffn_conv_b=jnp.stack([g["conv_b"] for g in ffn_g]),
        final_norm_g=d_final_g[0], loss=loss_part[0, :1])
    _, total = _allgather_small(_pack([small[n] for n, _ in SMALL]), name="reduce_small", with_sum=True)
    for (n, shp), val in zip(SMALL, _unpack(total, [shp for _, shp in SMALL])):
        grads[n] = val
    loss = grads.pop("loss")[0]
    grads["ssd_conv_w"] = lax.dynamic_slice_in_dim(grads["ssd_conv_w"], chip * ssd_conv_w.shape[2], ssd_conv_w.shape[2], axis=2)
    grads["ffn_conv_w"] = lax.dynamic_slice_in_dim(grads["ffn_conv_w"], chip * ffn_conv_w.shape[2], ffn_conv_w.shape[2], axis=2)

    deltas, new_m, new_v = {}, {}, {}
    for n in NAMES:
        shp = w[n].shape
        two_d = (1, shp[0]) if len(shp) == 1 else (-1, shp[-1])
        r2 = lambda a: a.reshape(two_d)
        d, nm, nv = _adamw(r2(w[n]), r2(grads[n]), r2(m_in[n]), r2(v_in[n]), name=f"adamw_{n}")
        deltas[n], new_m[n], new_v[n] = d.reshape(shp), nm.reshape(shp), nv.reshape(shp)
    return (loss, grad_x, *[grads[n] for n in NAMES], *[deltas[n] for n in NAMES], *[new_m[n] for n in NAMES],
            *[new_v[n] for n in NAMES])
```

```python
import functools

import jax
import jax.numpy as jnp
from jax import lax
from jax.experimental import pallas as pl
from jax.experimental.pallas import tpu as pltpu

F32 = jnp.float32
BF16 = jnp.bfloat16
HI = lax.Precision.HIGHEST
MESH = pl.DeviceIdType.MESH

D_MODEL = 1024
DEPTH = 4
EPS = 1e-6
SSD_DI = 2048
SSD_HD = 64
SSD_G = 4
SSD_HPG = 8
SSD_N = 128
SSD_K = 4
CHUNK = 128
SSD_CONV_DIM = 3072
SSD_ZX = SSD_DI + SSD_CONV_DIM
SSD_H = 32
FOX_HD = 64
FOX_H = 16
FOX_D = 1024
D_FF = 2816
FFN_K = 3
LANES = 128
VMEM_LIMIT = 56 * 1024 * 1024

ADAM_LR = 0.001
ADAM_B1 = 0.9
ADAM_B2 = 0.999
ADAM_EPS = 1e-08
ADAM_WD = 0.01
ADAM_STEP = 10

NN = (((1,), (0,)), ((), ()))
NT = (((1,), (1,)), ((), ()))
TN = (((0,), (0,)), ((), ()))


def _pick(n, cap, mult=LANES):
    best = None
    for t in range(mult, min(n, cap) + 1, mult):
        if n % t == 0:
            best = t
    return best if best is not None else n


def _cp(sem):
    return pltpu.CompilerParams(dimension_semantics=sem, vmem_limit_bytes=VMEM_LIMIT)


def _sigmoid(x):
    return jax.nn.sigmoid(x)


def _silu(x):
    return x * _sigmoid(x)


def _dsilu(x):
    s = _sigmoid(x)
    return s * (1.0 + x * (1.0 - s))


def _softplus(x):
    e = jnp.exp(-jnp.abs(x))
    u = 1.0 + e
    l1p = jnp.where(u == 1.0, e, jnp.log(u) * (e / (u - 1.0)))
    return jnp.maximum(x, 0.0) + l1p


def _dotf(a, b, dn=NN, *, onehot="b", pieces=2):
    x, e = (a, b) if onehot == "b" else (b, a)
    e = e.astype(BF16)
    acc = None
    for n in range(pieces):
        hi = x.astype(BF16)
        part = lax.dot_general(hi, e, dn, preferred_element_type=F32) if onehot == "b" else \
            lax.dot_general(e, hi, dn, preferred_element_type=F32)
        acc = part if acc is None else acc + part
        if n + 1 < pieces:
            x = x - hi.astype(F32)
    return acc


def _dotb(a, b, dn=NN):
    return lax.dot_general(a.astype(BF16), b.astype(BF16), dn, preferred_element_type=F32)


def _group_matrix(width, sub, transpose=False):
    ng = width // sub
    shape = (ng, width) if transpose else (width, ng)
    lane = lax.broadcasted_iota(jnp.int32, shape, 1 if transpose else 0)
    grp = lax.broadcasted_iota(jnp.int32, shape, 0 if transpose else 1)
    return (lane // sub == grp).astype(F32)


def _gmean(v, sub):
    width = v.shape[-1]
    if sub == width:
        return jnp.mean(v, axis=-1, keepdims=True)
    s = _dotf(v, _group_matrix(width, sub))
    return _dotf(s, _group_matrix(width, sub, transpose=True)) * (1.0 / sub)


def _matmul(a, b, *, mode, name, out_dtype=F32, add=None):
    if mode == "nn":
        (m, k), (k2, n) = a.shape, b.shape
    elif mode == "nt":
        (m, k), (n, k2) = a.shape, b.shape
    else:
        (k, m), (k2, n) = a.shape, b.shape
    assert k == k2, (a.shape, b.shape, mode)
    tm, tn, tk = _pick(m, 1024), _pick(n, 1536), _pick(k, 1536)
    nk = k // tk
    dn = {"nn": NN, "nt": NT, "tn": TN}[mode]
    has_add = add is not None

    def body(*refs):
        if has_add:
            a_ref, b_ref, add_ref, o_ref, acc_ref = refs
        else:
            a_ref, b_ref, o_ref, acc_ref = refs
            add_ref = None
        kk = pl.program_id(2)
        part = _dotb(a_ref[...], b_ref[...], dn)

        def finish(r):
            if has_add:
                r = r + add_ref[...]
            o_ref[...] = r.astype(out_dtype)

        if nk == 1:
            finish(part)
        else:
            @pl.when(kk == 0)
            def _():
                acc_ref[...] = part

            @pl.when(kk > 0)
            def _():
                acc_ref[...] += part

            @pl.when(kk == nk - 1)
            def _():
                finish(acc_ref[...])

    if mode == "nn":
        a_spec = pl.BlockSpec((tm, tk), lambda i, j, q: (i, q))
        b_spec = pl.BlockSpec((tk, tn), lambda i, j, q: (q, j))
    elif mode == "nt":
        a_spec = pl.BlockSpec((tm, tk), lambda i, j, q: (i, q))
        b_spec = pl.BlockSpec((tn, tk), lambda i, j, q: (j, q))
    else:
        a_spec = pl.BlockSpec((tk, tm), lambda i, j, q: (q, i))
        b_spec = pl.BlockSpec((tk, tn), lambda i, j, q: (q, j))
    o_spec = pl.BlockSpec((tm, tn), lambda i, j, q: (i, j))
    in_specs = [a_spec, b_spec] + ([o_spec] if has_add else [])
    args = (a, b) + ((add,) if has_add else ())
    return pl.pallas_call(
        body, name=name, grid=(m // tm, n // tn, nk), in_specs=in_specs, out_specs=o_spec,
        out_shape=jax.ShapeDtypeStruct((m, n), out_dtype),
        scratch_shapes=[pltpu.VMEM((tm, tn) if nk > 1 else (8, LANES), F32)],
        compiler_params=_cp(("parallel", "parallel", "arbitrary")),
    )(*args)


def _rms_fwd(x, g, *, gw, ncol, name, x_col0=0, sub=None, z=None, z_col0=0, out_dtype=BF16):
    rows = x.shape[0]
    tr = _pick(rows, 512, 8)
    sub = gw if sub is None else sub
    gated = z is not None

    def body(*refs):
        if gated:
            x_ref, z_ref, g_ref, o_ref = refs
            xv = x_ref[...] * _silu(z_ref[...])
        else:
            x_ref, g_ref, o_ref = refs
            xv = x_ref[...]
        r = lax.rsqrt(_gmean(xv * xv, sub) + EPS)
        o_ref[...] = (xv * r * g_ref[...]).astype(out_dtype)

    specs = [pl.BlockSpec((tr, gw), lambda j, i: (i, x_col0 + j))]
    args = [x]
    if gated:
        specs.append(pl.BlockSpec((tr, gw), lambda j, i: (i, z_col0 + j)))
        args.append(z)
    specs.append(pl.BlockSpec((1, gw), lambda j, i: (0, j)))
    args.append(g)
    return pl.pallas_call(
        body, name=name, grid=(ncol, rows // tr), in_specs=specs,
        out_specs=pl.BlockSpec((tr, gw), lambda j, i: (i, j)),
        out_shape=jax.ShapeDtypeStruct((rows, gw * ncol), out_dtype),
        compiler_params=_cp(("parallel", "parallel")),
    )(*args)


def _rms_bwd(x, g, dy, *, gw, ncol, name, x_col0=0, sub=None, z=None, z_col0=0, add=None, dx_dtype=F32):
    rows = x.shape[0]
    tr = _pick(rows, 512, 8)
    sub = gw if sub is None else sub
    gated = z is not None
    has_add = add is not None

    def body(*refs):
        refs = list(refs)
        x_ref = refs.pop(0)
        z_ref = refs.pop(0) if gated else None
        g_ref = refs.pop(0)
        dy_ref = refs.pop(0)
        add_ref = refs.pop(0) if has_add else None
        dx_ref = refs.pop(0)
        dz_ref = refs.pop(0) if gated else None
        dg_ref = refs.pop(0)
        i = pl.program_id(1)
        xv = x_ref[...]
        if gated:
            zz = z_ref[...]
            yz = xv * _silu(zz)
        else:
            yz = xv
        r = lax.rsqrt(_gmean(yz * yz, sub) + EPS)
        xh = yz * r
        dy = dy_ref[...].astype(F32)
        dyg = dy * g_ref[...]
        d_yz = r * (dyg - xh * _gmean(dyg * xh, sub))
        if gated:
            dx_ref[...] = (d_yz * _silu(zz)).astype(dx_dtype)
            dz_ref[...] = (d_yz * xv * _dsilu(zz)).astype(dx_dtype)
        elif has_add:
            dx_ref[...] = (d_yz + add_ref[...]).astype(dx_dtype)
        else:
            dx_ref[...] = d_yz.astype(dx_dtype)
        part = jnp.sum(dy * xh, axis=0, keepdims=True)

        @pl.when(i == 0)
        def _():
            dg_ref[...] = part

        @pl.when(i > 0)
        def _():
            dg_ref[...] += part

    tile = pl.BlockSpec((tr, gw), lambda j, i: (i, j))
    specs = [pl.BlockSpec((tr, gw), lambda j, i: (i, x_col0 + j))]
    args = [x]
    if gated:
        specs.append(pl.BlockSpec((tr, gw), lambda j, i: (i, z_col0 + j)))
        args.append(z)
    specs += [pl.BlockSpec((1, gw), lambda j, i: (0, j)), tile]
    args += [g, dy]
    if has_add:
        specs.append(tile)
        args.append(add)
    width = gw * ncol
    out_shape = [jax.ShapeDtypeStruct((rows, width), dx_dtype)]
    out_specs = [tile]
    if gated:
        out_shape.append(jax.ShapeDtypeStruct((rows, width), dx_dtype))
        out_specs.append(tile)
    out_shape.append(jax.ShapeDtypeStruct((1, width), F32))
    out_specs.append(pl.BlockSpec((1, gw), lambda j, i: (0, j)))
    return pl.pallas_call(
        body, name=name, grid=(ncol, rows // tr), in_specs=specs, out_specs=out_specs, out_shape=out_shape,
        compiler_params=_cp(("parallel", "arbitrary")),
    )(*args)


HALO = 8


def _conv_rows(tc):
    return 16 * 8 * LANES // tc


def _conv_fwd(u, w8, b, *, kw, width, name, u_col0=0, mul_col0=None, out_dtype=F32, rider=None):
    rows = u.shape[0]
    ts = _pick(rows, 512, 8)
    tc = _pick(width, 512)
    gated = mul_col0 is not None
    c0 = u_col0 // tc
    m0 = (mul_col0 // tc) if gated else 0
    assert u_col0 % tc == 0 and (not gated or mul_col0 % tc == 0)

    def body(*refs):
        if gated:
            cur_ref, halo_ref, mul_ref, w_ref, b_ref, o_ref, ext = refs
        else:
            cur_ref, halo_ref, w_ref, b_ref, o_ref, ext = refs
        i = pl.program_id(0)
        ext[pl.ds(0, HALO), :] = jnp.where(i == 0, 0.0, halo_ref[...])
        ext[pl.ds(HALO, ts), :] = cur_ref[...]
        bias = b_ref[...]
        taps = [w_ref[k:k + 1, :] for k in range(kw)]
        rb = _conv_rows(tc)
        for r0 in range(0, ts, rb):
            pre = bias + taps[0] * ext[pl.ds(r0 + HALO - (kw - 1), rb), :]
            for k in range(1, kw):
                pre = pre + taps[k] * ext[pl.ds(r0 + HALO - (kw - 1) + k, rb), :]
            act = _silu(pre)
            if gated:
                act = act * mul_ref[pl.ds(r0, rb), :]
            o_ref[pl.ds(r0, rb), :] = act.astype(out_dtype)

    hb = ts // HALO
    specs = [pl.BlockSpec((ts, tc), lambda i, j: (i, c0 + j)),
             pl.BlockSpec((HALO, tc), lambda i, j: (jnp.maximum(i * hb - 1, 0), c0 + j))]
    args = [u, u]
    if gated:
        specs.append(pl.BlockSpec((ts, tc), lambda i, j: (i, m0 + j)))
        args.append(u)
    specs += [pl.BlockSpec((8, tc), lambda i, j: (0, j)), pl.BlockSpec((1, tc), lambda i, j: (0, j))]
    args += [w8, b]
    outs, riding = _call(
        body, name=name, grid=(rows // ts, width // tc), in_specs=specs,
        out_specs=[pl.BlockSpec((ts, tc), lambda i, j: (i, j))],
        out_shape=[jax.ShapeDtypeStruct((rows, width), out_dtype)],
        scratch_shapes=[pltpu.VMEM((ts + HALO, tc), F32)], sem=("parallel", "parallel"), rider=rider, args=args)
    return outs + [riding]


def _conv_bwd(u, w8, b, dact, *, kw, width, name, u_col0=0, mul_col0=None, du_dtype=BF16, rider=None):
    rows = u.shape[0]
    ts = _pick(rows, 512, 8)
    tc = _pick(width, 512)
    gated = mul_col0 is not None
    c0 = u_col0 // tc
    m0 = (mul_col0 // tc) if gated else 0
    nt = rows // ts
    hb = ts // HALO

    def body(*refs):
        refs = list(refs)
        cur_ref, halo_ref = refs.pop(0), refs.pop(0)
        mul_ref = refs.pop(0) if gated else None
        w_ref, b_ref, da_ref = refs.pop(0), refs.pop(0), refs.pop(0)
        du_ref = refs.pop(0)
        dmul_ref = refs.pop(0) if gated else None
        dwb_ref, ext_u, ext_d = refs
        t = pl.program_id(1)
        ti = nt - 1 - t
        ext_u[pl.ds(0, HALO), :] = jnp.where(ti == 0, 0.0, halo_ref[...])
        ext_u[pl.ds(HALO, ts), :] = cur_ref[...]

        @pl.when(t == 0)
        def _():
            ext_d[pl.ds(ts, HALO), :] = jnp.zeros((HALO, tc), F32)
            dwb_ref[...] = jnp.zeros((8, tc), F32)

        bias = b_ref[...]
        taps = [w_ref[k:k + 1, :] for k in range(kw)]
        rb = _conv_rows(tc)
        dw_acc = [jnp.zeros((1, tc), F32) for _ in range(kw)]
        db_acc = jnp.zeros((1, tc), F32)
        for r0 in reversed(range(0, ts, rb)):
            shifted = [ext_u[pl.ds(r0 + HALO - (kw - 1) + k, rb), :] for k in range(kw)]
            pre = bias + taps[0] * shifted[0]
            for k in range(1, kw):
                pre = pre + taps[k] * shifted[k]
            sg = _sigmoid(pre)
            dsilu = sg * (1.0 + pre * (1.0 - sg))
            da = da_ref[pl.ds(r0, rb), :].astype(F32)
            if gated:
                dmul_ref[pl.ds(r0, rb), :] = (da * (pre * sg)).astype(du_dtype)
                dgp = da * mul_ref[pl.ds(r0, rb), :] * dsilu
            else:
                dgp = da * dsilu
            ext_d[pl.ds(r0, rb), :] = dgp
            du = taps[kw - 1] * dgp
            for k in range(kw - 1):
                du = du + taps[k] * ext_d[pl.ds(r0 + kw - 1 - k, rb), :]
            du_ref[pl.ds(r0, rb), :] = du.astype(du_dtype)
            for k in range(kw):
                dw_acc[k] = dw_acc[k] + jnp.sum(dgp * shifted[k], axis=0, keepdims=True)
            db_acc = db_acc + jnp.sum(dgp, axis=0, keepdims=True)
        for k in range(kw):
            dwb_ref[k:k + 1, :] += dw_acc[k]
        dwb_ref[7:8, :] += db_acc
        ext_d[pl.ds(ts, HALO), :] = ext_d[pl.ds(0, HALO), :]

    specs = [pl.BlockSpec((ts, tc), lambda j, t: (nt - 1 - t, c0 + j)),
             pl.BlockSpec((HALO, tc), lambda j, t: (jnp.maximum((nt - 1 - t) * hb - 1, 0), c0 + j))]
    args = [u, u]
    if gated:
        specs.append(pl.BlockSpec((ts, tc), lambda j, t: (nt - 1 - t, m0 + j)))
        args.append(u)
    tile = pl.BlockSpec((ts, tc), lambda j, t: (nt - 1 - t, j))
    specs += [pl.BlockSpec((8, tc), lambda j, t: (0, j)), pl.BlockSpec((1, tc), lambda j, t: (0, j)), tile]
    args += [w8, b, dact]
    out_shape = [jax.ShapeDtypeStruct((rows, width), du_dtype)]
    out_specs = [tile]
    if gated:
        out_shape.append(jax.ShapeDtypeStruct((rows, width), du_dtype))
        out_specs.append(tile)
    out_shape.append(jax.ShapeDtypeStruct((8, width), F32))
    out_specs.append(pl.BlockSpec((8, tc), lambda j, t: (0, j)))
    outs, riding = _call(
        body, name=name, grid=(width // tc, nt), in_specs=specs, out_specs=out_specs, out_shape=out_shape,
        scratch_shapes=[pltpu.VMEM((ts + HALO, tc), F32), pltpu.VMEM((ts + HALO, tc), F32)],
        sem=("parallel", "arbitrary"), rider=rider, args=args)
    return outs + [riding]


GW = SSD_HPG * SSD_HD


def _ssd_common(x, bm, cm, dt_raw, dt_raw_t, bias_r, bias_c, alog_r, alog_c):
    row = lax.broadcasted_iota(jnp.int32, (CHUNK, CHUNK), 0)
    col = lax.broadcasted_iota(jnp.int32, (CHUNK, CHUNK), 1)
    causal = row >= col
    tril = causal.astype(F32)
    triu = (row <= col).astype(F32)
    spread = _group_matrix(GW, SSD_HD, transpose=True)
    dt = _softplus(dt_raw + bias_r)
    dt_t = _softplus(dt_raw_t + bias_c)
    a_r = -jnp.exp(alog_r)
    a_c = -jnp.exp(alog_c)
    acs = _dotf(tril, dt * a_r, onehot="a", pieces=3)
    acs_t = _dotf(dt_t * a_c, triu, pieces=3)
    last = acs[CHUNK - 1:CHUNK, :]
    ds = jnp.exp(last - acs)
    cd = jnp.exp(last)
    c = dict(causal=causal, tril=tril, triu=triu, spread=spread, dt=dt, a_r=a_r, acs=acs, acs_t=acs_t, ds=ds, cd=cd)
    c["eb"] = _dotf(jnp.exp(acs), spread)
    c["dsb"] = _dotf(ds, spread)
    c["cdb"] = _dotf(cd, spread)
    c["dtb"] = _dotf(dt, spread)
    c["xdt"] = x * c["dtb"]
    c["cb"] = _dotb(cm, bm, NT)
    return c


def _ssd_lam(c, r):
    diff = c["acs"][:, r:r + 1] - c["acs_t"][r:r + 1, :]
    return jnp.exp(jnp.where(c["causal"], diff, -jnp.inf))


def _ssd_specs(nc, rev):
    def ci(t):
        return (nc - 1 - t) if rev else t
    xs = pl.BlockSpec((CHUNK, GW), lambda g, t: (ci(t), g))
    bs = pl.BlockSpec((CHUNK, SSD_N), lambda g, t: (ci(t), SSD_DI // SSD_N + g))
    cs = pl.BlockSpec((CHUNK, SSD_N), lambda g, t: (ci(t), SSD_DI // SSD_N + SSD_G + g))
    dts = pl.BlockSpec((1, CHUNK, 8), lambda g, t: (g, ci(t), 0))
    dtts = pl.BlockSpec((1, 8, CHUNK), lambda g, t: (g, 0, ci(t)))
    pr = pl.BlockSpec((1, 1, 8), lambda g, t: (g, 0, 0))
    pc = pl.BlockSpec((1, 8, 1), lambda g, t: (g, 0, 0))
    hs = pl.BlockSpec((1, 1, SSD_N, GW), lambda g, t: (ci(t), g, 0, 0))
    return xs, bs, cs, dts, dtts, pr, pc, hs


def _ssd_fwd(xbc, dtg, dtg_t, bias_r, bias_c, alog_r, alog_c, d_r, *, name, rider=None):
    s = xbc.shape[0]
    nc = s // CHUNK
    xs, bs, cs, dts, dtts, pr, pc, hs = _ssd_specs(nc, False)

    def body(x_ref, b_ref, c_ref, dt_ref, dtt_ref, br_ref, bc_ref, ar_ref, ac_ref, d_ref, y_ref, hp_ref, h_sc):
        t = pl.program_id(1)

        @pl.when(t == 0)
        def _():
            h_sc[...] = jnp.zeros_like(h_sc)

        x, bm, cm = x_ref[...], b_ref[...], c_ref[...]
        c = _ssd_common(x, bm, cm, dt_ref[0], dtt_ref[0], br_ref[0], bc_ref[0], ar_ref[0], ac_ref[0])
        h = h_sc[...]
        hp_ref[0, 0] = h
        xdt = c["xdt"]
        pieces = []
        for r in range(SSD_HPG):
            m = c["cb"] * _ssd_lam(c, r)
            pieces.append(_dotb(m, xdt[:, r * SSD_HD:(r + 1) * SSD_HD]))
        y = jnp.concatenate(pieces, axis=1) + c["eb"] * _dotb(cm, h) + x * _dotf(d_ref[0], c["spread"])
        y_ref[...] = y
        h_sc[...] = h * c["cdb"] + _dotb(bm, xdt * c["dsb"], TN)

    outs, riding = _call(
        body, name=name, grid=(SSD_G, nc),
        in_specs=[xs, bs, cs, dts, dtts, pr, pc, pr, pc, pr],
        out_specs=[xs, hs],
        out_shape=[jax.ShapeDtypeStruct((s, SSD_DI), F32), jax.ShapeDtypeStruct((nc, SSD_G, SSD_N, GW), F32)],
        scratch_shapes=[pltpu.VMEM((SSD_N, GW), F32)], sem=("parallel", "arbitrary"), rider=rider,
        args=(xbc, xbc, xbc, dtg, dtg_t, bias_r, bias_c, alog_r, alog_c, d_r))
    return outs + [riding]


def _ssd_bwd(xbc, dtg, dtg_t, bias_r, bias_c, alog_r, alog_c, d_r, hprev, dy, *, name, rider=None):
    s = xbc.shape[0]
    nc = s // CHUNK
    xs, bs, cs, dts, dtts, pr, pc, hs = _ssd_specs(nc, True)
    gsum = functools.partial(_group_matrix, GW, SSD_HD)

    def body(x_ref, b_ref, c_ref, dt_ref, dtt_ref, br_ref, bc_ref, ar_ref, ac_ref, d_ref, hp_ref, dy_ref,
             dx_ref, db_ref, dc_ref, ddt_ref, dbias_ref, dalog_ref, dd_ref, dh_sc):
        t = pl.program_id(1)

        @pl.when(t == 0)
        def _():
            dh_sc[...] = jnp.zeros_like(dh_sc)
            dbias_ref[...] = jnp.zeros_like(dbias_ref)
            dalog_ref[...] = jnp.zeros_like(dalog_ref)
            dd_ref[...] = jnp.zeros_like(dd_ref)

        x, bm, cm = x_ref[...], b_ref[...], c_ref[...]
        c = _ssd_common(x, bm, cm, dt_ref[0], dtt_ref[0], br_ref[0], bc_ref[0], ar_ref[0], ac_ref[0])
        lanesum = gsum()
        h = hp_ref[0, 0]
        dh = dh_sc[...]
        dy = dy_ref[...]
        xdt, dsb = c["xdt"], c["dsb"]
        skip = _dotf(d_ref[0], c["spread"])
        dd_ref[0] += jnp.sum(_dotf(dy * x, lanesum), axis=0, keepdims=True)
        dacs = _dotf(dy * (c["eb"] * _dotb(cm, h)), lanesum)
        edy = c["eb"] * dy
        dcm = _dotb(edy, h, NT)
        dh_prev = _dotb(cm, edy, TN)
        bdh = _dotb(bm, dh)
        dxdt = dsb * bdh
        dbm = _dotb(dsb * xdt, dh, NT)
        t1 = _dotf(xdt * bdh, lanesum) * c["ds"]
        dacs = dacs - t1
        dlast = jnp.sum(t1, axis=0, keepdims=True) + jnp.sum(_dotf(dh * h, lanesum), axis=0, keepdims=True) * c["cd"]
        dcb = jnp.zeros((CHUNK, CHUNK), F32)
        pieces = []
        ones8 = jnp.ones((CHUNK, 8), F32)
        head = lax.broadcasted_iota(jnp.int32, (1, 8), 1)
        for r in range(SSD_HPG):
            sl = slice(r * SSD_HD, (r + 1) * SSD_HD)
            lam = _ssd_lam(c, r)
            m = c["cb"] * lam
            dm = _dotb(dy[:, sl], xdt[:, sl], NT)
            dcb = dcb + dm * lam
            gm = dm * m
            dacs = dacs + (jnp.sum(gm, axis=1, keepdims=True) - _dotf(gm, ones8, TN, pieces=3)) * (head == r).astype(F32)
            pieces.append(_dotb(m, dy[:, sl], TN))
        dxdt = dxdt + jnp.concatenate(pieces, axis=1)
        dcm = dcm + _dotb(dcb, bm)
        dbm = dbm + _dotb(dcb, cm, TN)
        dx_ref[...] = dy * skip + dxdt * c["dtb"]
        db_ref[...] = dbm
        dc_ref[...] = dcm
        rowid = lax.broadcasted_iota(jnp.int32, (CHUNK, 8), 0)
        dacs = dacs + jnp.where(rowid == CHUNK - 1, dlast, 0.0)
        dda = _dotf(c["triu"], dacs, onehot="a", pieces=3)
        ddt = _dotf(dxdt * x, lanesum) + dda * c["a_r"]
        ddt_raw = ddt * _sigmoid(dt_ref[0] + br_ref[0])
        ddt_ref[0] = ddt_raw
        dbias_ref[0] += jnp.sum(ddt_raw, axis=0, keepdims=True)
        dalog_ref[0] += jnp.sum(dda * c["dt"], axis=0, keepdims=True) * c["a_r"]
        dh_sc[...] = dh_prev + dh * c["cdb"]

    ci = lambda t: nc - 1 - t
    nspec = pl.BlockSpec((CHUNK, SSD_N), lambda g, t: (ci(t), g))
    outs, riding = _call(
        body, name=name, grid=(SSD_G, nc),
        in_specs=[xs, bs, cs, dts, dtts, pr, pc, pr, pc, pr, hs, xs],
        out_specs=[xs, nspec, nspec, dts, pr, pr, pr],
        out_shape=[jax.ShapeDtypeStruct((s, SSD_DI), F32), jax.ShapeDtypeStruct((s, SSD_G * SSD_N), F32),
                   jax.ShapeDtypeStruct((s, SSD_G * SSD_N), F32), jax.ShapeDtypeStruct((SSD_G, s, 8), F32),
                   jax.ShapeDtypeStruct((SSD_G, 1, 8), F32), jax.ShapeDtypeStruct((SSD_G, 1, 8), F32),
                   jax.ShapeDtypeStruct((SSD_G, 1, 8), F32)],
        scratch_shapes=[pltpu.VMEM((SSD_N, GW), F32)], sem=("parallel", "arbitrary"), rider=rider,
        args=(xbc, xbc, xbc, dtg, dtg_t, bias_r, bias_c, alog_r, alog_c, d_r, hprev, dy))
    return outs + [riding]


FOX_PAIRS = FOX_H // 2
FOX_SCALE = FOX_HD ** -0.5
NEG_INF = -jnp.inf


def _fgate_fwd(f_t, b_c, *, name):
    hh, s = f_t.shape
    tb = _pick(s, 512)
    nb = s // tb

    def body(f_ref, b_ref, o_ref, carry):
        t = pl.program_id(0)

        @pl.when(t == 0)
        def _():
            carry[...] = jnp.zeros_like(carry)

        lf = -_softplus(-(f_ref[...] + b_ref[...]))
        row = lax.broadcasted_iota(jnp.int32, (tb, tb), 0)
        col = lax.broadcasted_iota(jnp.int32, (tb, tb), 1)
        cum = _dotf(lf, (row <= col).astype(F32), pieces=3) + carry[:, 0:1]
        o_ref[...] = cum
        carry[:, 0:1] = cum[:, tb - 1:tb]

    return pl.pallas_call(
        body, name=name, grid=(nb,),
        in_specs=[pl.BlockSpec((hh, tb), lambda t: (0, t)), pl.BlockSpec((hh, 1), lambda t: (0, 0))],
        out_specs=pl.BlockSpec((hh, tb), lambda t: (0, t)),
        out_shape=jax.ShapeDtypeStruct((hh, s), F32),
        scratch_shapes=[pltpu.VMEM((hh, LANES), F32)],
        compiler_params=_cp(("arbitrary",)),
    )(f_t, b_c)


def _fgate_bwd(dcum_q_t, dcum_k_t, f_t, b_c, *, name):
    hh, s = f_t.shape
    tb = _pick(s, 512)
    nb = s // tb

    def body(dq_ref, d_ref, f_ref, b_ref, df_ref, db_ref, carry):
        t = pl.program_id(0)

        @pl.when(t == 0)
        def _():
            carry[...] = jnp.zeros_like(carry)
            db_ref[...] = jnp.zeros_like(db_ref)

        d = d_ref[...] + dq_ref[...]
        row = lax.broadcasted_iota(jnp.int32, (tb, tb), 0)
        col = lax.broadcasted_iota(jnp.int32, (tb, tb), 1)
        rev = _dotf(d, (row >= col).astype(F32), pieces=3) + carry[:, 0:1]
        df = rev * _sigmoid(-(f_ref[...] + b_ref[...]))
        df_ref[...] = df
        db_ref[...] += jnp.sum(df, axis=1, keepdims=True)
        carry[:, 0:1] = rev[:, 0:1]

    blk = pl.BlockSpec((hh, tb), lambda t: (0, nb - 1 - t))
    return pl.pallas_call(
        body, name=name, grid=(nb,),
        in_specs=[blk, blk, blk, pl.BlockSpec((hh, 1), lambda t: (0, 0))],
        out_specs=[blk, pl.BlockSpec((hh, 1), lambda t: (0, 0))],
        out_shape=[jax.ShapeDtypeStruct((hh, s), F32), jax.ShapeDtypeStruct((hh, 1), F32)],
        scratch_shapes=[pltpu.VMEM((hh, LANES), F32)],
        compiler_params=_cp(("arbitrary",)),
    )(dcum_q_t, dcum_k_t, f_t, b_c)


def _fox_tile(s):
    return min(512, max(s // 2, 8))


def _tri_tables(nq, kv_major):
    if kv_major:
        pairs = [(i, j) for j in range(nq) for i in range(j, nq)]
    else:
        pairs = [(i, j) for i in range(nq) for j in range(i + 1)]
    return (jnp.asarray([p[0] for p in pairs], jnp.int32), jnp.asarray([p[1] for p in pairs], jnp.int32))


def _lane_tile(col, width):
    return col if width == LANES else jnp.tile(col, (1, width // LANES))


def _flash_fwd(qs, kn, qkvg, ck, *, name, rider=None):
    s = qs.shape[0]
    tt = _fox_tile(s)
    nq = s // tt
    itab, jtab = _tri_tables(nq, kv_major=False)
    v0 = 2 * FOX_D // LANES

    def body(itab_ref, jtab_ref, q_ref, k_ref, v_ref, ck_ref, o_ref, lse_ref, m_sc, l_sc, acc_sc):
        t = pl.program_id(1)
        i, j = itab_ref[t], jtab_ref[t]

        @pl.when(j == 0)
        def _():
            m_sc[...] = jnp.full_like(m_sc, NEG_INF)
            l_sc[...] = jnp.zeros_like(l_sc)
            acc_sc[...] = jnp.zeros_like(acc_sc)

        low = lax.broadcasted_iota(jnp.int32, (tt, LANES), 1) < FOX_HD

        def step(diagonal):
            q2, k2 = q_ref[...], k_ref[...]
            v2 = v_ref[...].astype(BF16)
            alphas, outs = [], []
            for hh in range(2):
                qh = jnp.where(low if hh == 0 else jnp.logical_not(low), q2, jnp.zeros_like(q2))
                sc = lax.dot_general(qh, k2, NT, preferred_element_type=F32) - ck_ref[0][hh:hh + 1, :]
                if diagonal:
                    row = lax.broadcasted_iota(jnp.int32, sc.shape, 0)
                    col = lax.broadcasted_iota(jnp.int32, sc.shape, 1)
                    sc = jnp.where(row >= col, sc, NEG_INF)
                m_prev = m_sc[hh]
                m_new = jnp.maximum(m_prev, jnp.max(sc, axis=1, keepdims=True))
                alpha = jnp.exp(m_prev - m_new)
                p = jnp.exp(sc - _lane_tile(m_new, tt))
                l_sc[hh] = alpha * l_sc[hh] + jnp.sum(p, axis=1, keepdims=True)
                m_sc[hh] = m_new
                alphas.append(alpha)
                outs.append(lax.dot_general(p.astype(BF16), v2, NN, preferred_element_type=F32))
            acc_sc[...] = jnp.where(low, alphas[0], alphas[1]) * acc_sc[...] + jnp.where(low, outs[0], outs[1])

        @pl.when(j < i)
        def _():
            step(False)

        @pl.when(j == i)
        def _():
            step(True)
            o_ref[...] = acc_sc[...] / jnp.where(low, l_sc[0], l_sc[1])
            lse_ref[0] = jnp.concatenate([m_sc[hh][:, 0:1] + jnp.log(l_sc[hh][:, 0:1]) for hh in range(2)], axis=1)

    outs, riding = _call(
        body, name=name, grid=(FOX_PAIRS, int(itab.shape[0])), prefetch=(itab, jtab),
        in_specs=[pl.BlockSpec((tt, LANES), lambda p, t, it, jt: (it[t], p)),
                  pl.BlockSpec((tt, LANES), lambda p, t, it, jt: (jt[t], p)),
                  pl.BlockSpec((tt, LANES), lambda p, t, it, jt: (jt[t], v0 + p)),
                  pl.BlockSpec((1, 2, tt), lambda p, t, it, jt: (p, 0, jt[t]))],
        out_specs=[pl.BlockSpec((tt, LANES), lambda p, t, it, jt: (it[t], p)),
                   pl.BlockSpec((1, tt, 2), lambda p, t, it, jt: (p, it[t], 0))],
        scratch_shapes=[pltpu.VMEM((2, tt, LANES), F32), pltpu.VMEM((2, tt, LANES), F32), pltpu.VMEM((tt, LANES), F32)],
        out_shape=[jax.ShapeDtypeStruct((s, FOX_D), F32), jax.ShapeDtypeStruct((FOX_PAIRS, s, 2), F32)],
        sem=("parallel", "arbitrary"), rider=rider, args=(qs, kn, qkvg, ck))
    return outs + [riding]


def _flash_bwd(qs, kn, qkvg, do, lse_t, delta_t, ck_c, *, name, rider=None):
    s = qs.shape[0]
    tt = _fox_tile(s)
    nq = s // tt
    nl = tt // LANES
    itab, jtab = _tri_tables(nq, kv_major=True)
    nsteps = itab.shape[0]
    v0 = 2 * FOX_D // LANES

    def body(itab_ref, jtab_ref, q_ref, k_ref, v_ref, do_ref, lse_ref, dl_ref, ck_ref,
             dq_ref, dk_ref, dv_ref, dcq_ref, dck_ref, dqt_sc, rs_sc, dk_sc, dv_sc, dc_sc, kt_sc, ckb_sc):
        t = pl.program_id(1)
        i, j = itab_ref[t], jtab_ref[t]

        @pl.when(t == 0)
        def _():
            dqt_sc[...] = jnp.zeros_like(dqt_sc)
            rs_sc[...] = jnp.zeros_like(rs_sc)

        @pl.when(i == j)
        def _():
            dk_sc[...] = jnp.zeros_like(dk_sc)
            dv_sc[...] = jnp.zeros_like(dv_sc)
            dc_sc[...] = jnp.zeros_like(dc_sc)
            kt_sc[...] = k_ref[...].astype(F32).T.astype(BF16)
            for hh in range(2):
                ckb_sc[hh] = jnp.broadcast_to(ck_ref[0][:, hh:hh + 1], (tt, LANES))

        low = lax.broadcasted_iota(jnp.int32, (tt, LANES), 1) < FOX_HD
        top = lax.broadcasted_iota(jnp.int32, (LANES, tt), 0) < FOX_HD

        def step(diagonal):
            q2, k2, kt = q_ref[...], k_ref[...], kt_sc[...]
            v2 = v_ref[...].astype(BF16)
            do2 = do_ref[...].astype(BF16)
            dqs, dks, dvs = [], [], []
            for hh in range(2):
                sel = low if hh == 0 else jnp.logical_not(low)
                qh = jnp.where(sel, q2, jnp.zeros_like(q2))
                doh = jnp.where(sel, do2, jnp.zeros_like(do2))
                st = lax.dot_general(k2, qh, NT, preferred_element_type=F32)
                st = st - _lane_tile(ckb_sc[hh], tt) - lse_ref[0][hh:hh + 1, :]
                if diagonal:
                    key = lax.broadcasted_iota(jnp.int32, st.shape, 0)
                    qry = lax.broadcasted_iota(jnp.int32, st.shape, 1)
                    st = jnp.where(qry >= key, st, NEG_INF)
                pt = jnp.exp(st)
                dpt = lax.dot_general(v2, doh, NT, preferred_element_type=F32)
                dst = pt * (dpt - dl_ref[0][hh:hh + 1, :])
                ptb, dstb = pt.astype(BF16), dst.astype(BF16)
                dvs.append(lax.dot_general(ptb, do2, NN, preferred_element_type=F32))
                dks.append(lax.dot_general(dstb, q2, NN, preferred_element_type=F32))
                dqs.append(lax.dot_general(kt, dstb, NN, preferred_element_type=F32))
                rs_sc[hh, i] += jnp.sum(dst, axis=0, keepdims=True)
                part = dst[:, 0:LANES]
                for b in range(1, nl):
                    part = part + dst[:, b * LANES:(b + 1) * LANES]
                dc_sc[hh] += part
            dv_sc[...] += jnp.where(low, dvs[0], dvs[1])
            dk_sc[...] += jnp.where(low, dks[0], dks[1])
            dqt_sc[i] += jnp.where(top, dqs[0], dqs[1])

        @pl.when(j < i)
        def _():
            step(False)

        @pl.when(j == i)
        def _():
            step(True)

        @pl.when(i == nq - 1)
        def _():
            dk_ref[...] = dk_sc[...]
            dv_ref[...] = dv_sc[...]
            dck_ref[0] = -jnp.concatenate([jnp.sum(dc_sc[hh], axis=1, keepdims=True) for hh in range(2)], axis=1)

        @pl.when(t == nsteps - 1)
        def _():
            for b in range(nq):
                dq_ref[pl.ds(b * tt, tt), :] = dqt_sc[b].T * FOX_SCALE
                dcq_ref[0, :, pl.ds(b * tt, tt)] = jnp.concatenate([rs_sc[hh, b] for hh in range(2)], axis=0)

    qside = pl.BlockSpec((tt, LANES), lambda p, t, it, jt: (it[t], p))
    kside = pl.BlockSpec((tt, LANES), lambda p, t, it, jt: (jt[t], p))
    qstat = pl.BlockSpec((1, 2, tt), lambda p, t, it, jt: (p, 0, it[t]))
    kstat = pl.BlockSpec((1, tt, 2), lambda p, t, it, jt: (p, jt[t], 0))
    outs, riding = _call(
        body, name=name, grid=(FOX_PAIRS, nsteps), prefetch=(itab, jtab),
        in_specs=[qside, kside, pl.BlockSpec((tt, LANES), lambda p, t, it, jt: (jt[t], v0 + p)), qside, qstat, qstat, kstat],
        out_specs=[pl.BlockSpec((s, LANES), lambda p, t, it, jt: (0, p)), kside, kside,
                   pl.BlockSpec((1, 2, s), lambda p, t, it, jt: (p, 0, 0)), kstat],
        scratch_shapes=[pltpu.VMEM((nq, LANES, tt), F32), pltpu.VMEM((2, nq, 1, tt), F32), pltpu.VMEM((tt, LANES), F32),
                        pltpu.VMEM((tt, LANES), F32), pltpu.VMEM((2, tt, LANES), F32), pltpu.VMEM((LANES, tt), BF16),
                        pltpu.VMEM((2, tt, LANES), F32)],
        out_shape=[jax.ShapeDtypeStruct((s, FOX_D), F32), jax.ShapeDtypeStruct((s, FOX_D), F32),
                   jax.ShapeDtypeStruct((s, FOX_D), F32), jax.ShapeDtypeStruct((FOX_PAIRS, 2, s), F32),
                   jax.ShapeDtypeStruct((FOX_PAIRS, s, 2), F32)],
        sem=("parallel", "arbitrary"), rider=rider, args=(qs, kn, qkvg, do, lse_t, delta_t, ck_c))
    return outs + [riding]


def _ogate_fwd(o, qkvg, *, name):
    s = o.shape[0]
    tr = _pick(s, 512, 8)

    def body(o_ref, g_ref, out_ref):
        out_ref[...] = (o_ref[...] * _sigmoid(g_ref[...])).astype(BF16)

    tile = pl.BlockSpec((tr, FOX_D), lambda i: (i, 0))
    return pl.pallas_call(
        body, name=name, grid=(s // tr,), in_specs=[tile, pl.BlockSpec((tr, FOX_D), lambda i: (i, 3))],
        out_specs=tile, out_shape=jax.ShapeDtypeStruct((s, FOX_D), BF16), compiler_params=_cp(("parallel",)),
    )(o, qkvg)


def _ogate_bwd(dog, o, qkvg, *, name):
    s = o.shape[0]
    tr = _pick(s, 512, 8)

    def body(dog_ref, o_ref, g_ref, do_ref, dg_ref, dl_ref):
        sg = _sigmoid(g_ref[...])
        ov = o_ref[...]
        dog_v = dog_ref[...]
        do = dog_v * sg
        do_ref[...] = do
        dg_ref[...] = (dog_v * ov * sg * (1.0 - sg)).astype(BF16)
        dl_ref[...] = _dotf(do * ov, _group_matrix(FOX_D, FOX_HD))

    tile = pl.BlockSpec((tr, FOX_D), lambda i: (i, 0))
    return pl.pallas_call(
        body, name=name, grid=(s // tr,), in_specs=[tile, tile, pl.BlockSpec((tr, FOX_D), lambda i: (i, 3))],
        out_specs=[tile, tile, pl.BlockSpec((tr, FOX_H), lambda i: (i, 0))],
        out_shape=[jax.ShapeDtypeStruct((s, FOX_D), F32), jax.ShapeDtypeStruct((s, FOX_D), BF16),
                   jax.ShapeDtypeStruct((s, FOX_H), F32)],
        compiler_params=_cp(("parallel",)),
    )(dog, o, qkvg)


def _loss_head(h, g, target, *, name):
    s, d = h.shape
    tr = _pick(s, 512, 8)

    def body(h_ref, g_ref, t_ref, loss_ref, dh_ref, dg_ref):
        i = pl.program_id(0)
        x = h_ref[...]
        gv = g_ref[...]
        r = lax.rsqrt(jnp.mean(x * x, axis=-1, keepdims=True) + EPS)
        xh = x * r
        err = xh * gv - t_ref[...]
        part = 0.5 * jnp.sum(jnp.sum(err * err, axis=1, keepdims=True) * (1.0 / d), axis=0, keepdims=True)
        dy = err * (1.0 / d)
        dyg = dy * gv
        dh_ref[...] = r * (dyg - xh * jnp.mean(dyg * xh, axis=-1, keepdims=True))
        dgp = jnp.sum(dy * xh, axis=0, keepdims=True)

        @pl.when(i == 0)
        def _():
            loss_ref[...] = jnp.zeros_like(loss_ref) + part
            dg_ref[...] = dgp

        @pl.when(i > 0)
        def _():
            loss_ref[...] += part
            dg_ref[...] += dgp

    tile = pl.BlockSpec((tr, d), lambda i: (i, 0))
    vec = pl.BlockSpec((1, d), lambda i: (0, 0))
    return pl.pallas_call(
        body, name=name, grid=(s // tr,), in_specs=[tile, vec, tile],
        out_specs=[pl.BlockSpec((1, LANES), lambda i: (0, 0)), tile, vec],
        out_shape=[jax.ShapeDtypeStruct((1, LANES), F32), jax.ShapeDtypeStruct((s, d), F32),
                   jax.ShapeDtypeStruct((1, d), F32)],
        compiler_params=_cp(("arbitrary",)),
    )(h, g, target)


def _adamw(w, g, m, v, *, name):
    rows, cols = w.shape
    tr = _pick(rows, 256, 8)
    c1 = 1.0 - ADAM_B1 ** ADAM_STEP
    c2 = 1.0 - ADAM_B2 ** ADAM_STEP

    def body(w_ref, g_ref, m_ref, v_ref, d_ref, nm_ref, nv_ref):
        gv = g_ref[...]
        nm = ADAM_B1 * m_ref[...] + (1.0 - ADAM_B1) * gv
        nv = ADAM_B2 * v_ref[...] + (1.0 - ADAM_B2) * (gv * gv)
        d_ref[...] = -ADAM_LR * ((nm / c1) / (jnp.sqrt(nv / c2) + ADAM_EPS) + ADAM_WD * w_ref[...])
        nm_ref[...] = nm
        nv_ref[...] = nv

    tile = pl.BlockSpec((tr, cols), lambda i: (i, 0))
    shp = jax.ShapeDtypeStruct((rows, cols), F32)
    return pl.pallas_call(
        body, name=name, grid=(rows // tr,), in_specs=[tile] * 4, out_specs=[tile] * 3, out_shape=[shp] * 3,
        compiler_params=_cp(("parallel",)),
    )(w, g, m, v)


ANY = pl.BlockSpec(memory_space=pl.ANY)
N_DEV = 8


def _coords():
    return lax.axis_index("x"), lax.axis_index("y"), lax.axis_index("c")


def _other_chips(x, y):
    return [(1 - x, y), (x, 1 - y), (1 - x, 1 - y)]


def _allgather_small(buf, *, name, with_sum):
    rows = buf.shape[0]

    def body(*refs):
        if with_sum:
            x_ref, out_ref, sum_ref, send_sems, recv_sems = refs
        else:
            x_ref, out_ref, send_sems, recv_sems = refs
        x, y, c = _coords()
        me = 4 * x + 2 * y + c
        out_ref[me] = x_ref[...]
        copies = []
        for rel in range(1, N_DEV):
            px = (1 - x) if rel & 4 else x
            py = (1 - y) if rel & 2 else y
            pc = (1 - c) if rel & 1 else c
            cp = pltpu.make_async_remote_copy(
                src_ref=x_ref, dst_ref=out_ref.at[me], send_sem=send_sems.at[rel - 1], recv_sem=recv_sems.at[rel - 1],
                device_id=(px, py, pc), device_id_type=MESH)
            cp.start()
            copies.append(cp)
        for cp in copies:
            cp.wait()
        if with_sum:
            acc = out_ref[0]
            for k in range(1, N_DEV):
                acc = acc + out_ref[k]
            sum_ref[...] = acc

    slots = jax.ShapeDtypeStruct((N_DEV, rows, LANES), F32)
    vm = pl.BlockSpec(memory_space=pltpu.VMEM)
    out_shape = [slots, jax.ShapeDtypeStruct((rows, LANES), F32)] if with_sum else [slots]
    return pl.pallas_call(
        body, name=name, in_specs=[vm], out_specs=[vm] * len(out_shape), out_shape=out_shape,
        scratch_shapes=[pltpu.SemaphoreType.DMA((N_DEV - 1,)), pltpu.SemaphoreType.DMA((N_DEV - 1,))],
    )(buf)


class _Gather:
    per_array = 6

    def __init__(self, arrays):
        self.arrays = list(arrays)

    def out_shapes(self):
        return [jax.ShapeDtypeStruct((4,) + a.shape, a.dtype) for a in self.arrays]

    @staticmethod
    def _ici(ins, outs, send_sems, recv_sems, t, j, px, py, c, slot):
        return pltpu.make_async_remote_copy(
            src_ref=ins[t].at[c], dst_ref=outs[t].at[slot, c], send_sem=send_sems.at[6 * t + j],
            recv_sem=recv_sems.at[6 * t + j], device_id=(px, py, c), device_id_type=MESH)

    @staticmethod
    def _d2d(outs, send_sems, recv_sems, t, j, kj, half, sibling):
        return pltpu.make_async_remote_copy(
            src_ref=outs[t].at[kj, half], dst_ref=outs[t].at[kj, half], send_sem=send_sems.at[6 * t + 3 + j],
            recv_sem=recv_sems.at[6 * t + 3 + j], device_id=sibling, device_id_type=MESH)

    def start(self, ins, outs, send_sems, recv_sems):
        x, y, c = _coords()
        for t in range(len(ins)):
            for j, (px, py) in enumerate(_other_chips(x, y)):
                self._ici(ins, outs, send_sems, recv_sems, t, j, px, py, c, 2 * x + y).start()

    def finish(self, ins, outs, send_sems, recv_sems):
        x, y, c = _coords()
        chips = _other_chips(x, y)
        sibling = (x, y, 1 - c)
        started = []
        for t in range(len(ins)):
            for j, (px, py) in enumerate(chips):
                ici = self._ici(ins, outs, send_sems, recv_sems, t, j, px, py, c, 2 * px + py)
                ici.wait_recv()
                fwd = self._d2d(outs, send_sems, recv_sems, t, j, 2 * px + py, c, sibling)
                fwd.start()
                started += [ici, fwd]
        for t in range(len(ins)):
            for j, (px, py) in enumerate(chips):
                self._d2d(outs, send_sems, recv_sems, t, j, 2 * px + py, 1 - c, sibling).wait_recv()
        for cp in started:
            cp.wait_send()


class _Exchange:
    per_array = 7

    def __init__(self, arrays):
        self.arrays = list(arrays)

    def out_shapes(self):
        return [jax.ShapeDtypeStruct((7,) + a.shape[2:], a.dtype) for a in self.arrays]

    @staticmethod
    def _copies(ins, outs, send_sems, recv_sems):
        x, y, c = _coords()
        for t in range(len(ins)):
            for rel in range(1, N_DEV):
                px = (1 - x) if rel & 4 else x
                py = (1 - y) if rel & 2 else y
                pc = (1 - c) if rel & 1 else c
                yield pltpu.make_async_remote_copy(
                    src_ref=ins[t].at[2 * px + py, pc], dst_ref=outs[t].at[rel - 1], send_sem=send_sems.at[7 * t + rel - 1],
                    recv_sem=recv_sems.at[7 * t + rel - 1], device_id=(px, py, pc), device_id_type=MESH)

    def start(self, ins, outs, send_sems, recv_sems):
        for cp in self._copies(ins, outs, send_sems, recv_sems):
            cp.start()

    def finish(self, ins, outs, send_sems, recv_sems):
        for cp in self._copies(ins, outs, send_sems, recv_sems):
            cp.wait()


def _call(body, *, name, grid, in_specs, out_specs, out_shape, scratch_shapes, args, sem, rider=None, prefetch=()):
    n_in, n_out, n_pre = len(in_specs), len(out_specs), len(prefetch)
    n_c = len(rider.arrays) if rider is not None else 0

    def wrapped(*refs):
        pre, rest = refs[:n_pre], refs[n_pre:]
        ins, cins = rest[:n_in], rest[n_in:n_in + n_c]
        outs = rest[n_in + n_c:n_in + n_c + n_out]
        couts = rest[n_in + n_c + n_out:n_in + 2 * n_c + n_out]
        scratch = rest[n_in + 2 * n_c + n_out:]
        if rider is None:
            body(*pre, *ins, *outs, *scratch)
            return
        send_sems, recv_sems = scratch[-2:]
        ids = [pl.program_id(a) for a in range(len(grid))]
        first = functools.reduce(jnp.logical_and, [i == 0 for i in ids])
        last = functools.reduce(jnp.logical_and, [i == g - 1 for i, g in zip(ids, grid)])

        @pl.when(first)
        def _():
            rider.start(cins, couts, send_sems, recv_sems)

        body(*pre, *ins, *outs, *scratch[:-2])

        @pl.when(last)
        def _():
            rider.finish(cins, couts, send_sems, recv_sems)

    if rider is not None:
        nsem = rider.per_array * n_c
        in_specs = list(in_specs) + [ANY] * n_c
        out_specs = list(out_specs) + [ANY] * n_c
        out_shape = list(out_shape) + rider.out_shapes()
        scratch_shapes = list(scratch_shapes) + [pltpu.SemaphoreType.DMA((nsem,)), pltpu.SemaphoreType.DMA((nsem,))]
        args = list(args) + rider.arrays
        sem = ("arbitrary",) * len(grid)
    if n_pre:
        res = pl.pallas_call(
            wrapped, name=name, out_shape=out_shape, compiler_params=_cp(sem),
            grid_spec=pltpu.PrefetchScalarGridSpec(num_scalar_prefetch=n_pre, grid=grid, in_specs=in_specs,
                                                   out_specs=out_specs, scratch_shapes=scratch_shapes),
        )(*prefetch, *args)
    else:
        res = pl.pallas_call(
            wrapped, name=name, grid=grid, in_specs=in_specs, out_specs=out_specs, out_shape=out_shape,
            scratch_shapes=scratch_shapes, compiler_params=_cp(sem),
        )(*args)
    return list(res[:n_out]), list(res[n_out:])


def _run_rider(rider, *, name):
    n = len(rider.arrays)

    def body(*refs):
        ins, outs = refs[:n], refs[n:2 * n]
        send_sems, recv_sems = refs[2 * n:]
        rider.start(ins, outs, send_sems, recv_sems)
        rider.finish(ins, outs, send_sems, recv_sems)

    nsem = rider.per_array * n
    return pl.pallas_call(
        body, name=name, in_specs=[ANY] * n, out_specs=[ANY] * n, out_shape=rider.out_shapes(),
        scratch_shapes=[pltpu.SemaphoreType.DMA((nsem,)), pltpu.SemaphoreType.DMA((nsem,))],
    )(*rider.arrays)


def _sibling_swap(arrs, *, name):
    n = len(arrs)

    def body(*refs):
        ins, outs = refs[:n], refs[n:2 * n]
        send_sems, recv_sems = refs[2 * n:]
        x, y, c = _coords()
        copies = []
        for t in range(n):
            cp = pltpu.make_async_remote_copy(
                src_ref=ins[t], dst_ref=outs[t], send_sem=send_sems.at[t], recv_sem=recv_sems.at[t],
                device_id=(x, y, 1 - c), device_id_type=MESH)
            cp.start()
            copies.append(cp)
        for cp in copies:
            cp.wait()

    return pl.pallas_call(
        body, name=name, in_specs=[ANY] * n, out_specs=[ANY] * n,
        out_shape=[jax.ShapeDtypeStruct(a.shape, a.dtype) for a in arrs],
        scratch_shapes=[pltpu.SemaphoreType.DMA((n,)), pltpu.SemaphoreType.DMA((n,))],
    )(*arrs)


def _add_selected(stack, others, sel, *, name):
    _, m, cols = stack.shape
    q = others.shape[0]
    tr = _pick(m, 256, 16)

    def body(sel_ref, s_ref, o_ref, out_ref):
        acc = s_ref[0].astype(F32)
        for i in range(q):
            acc = acc + o_ref[i].astype(F32)
        out_ref[...] = acc

    return pl.pallas_call(
        body, name=name,
        grid_spec=pltpu.PrefetchScalarGridSpec(
            num_scalar_prefetch=1, grid=(m // tr,),
            in_specs=[pl.BlockSpec((1, tr, cols), lambda i, sel_ref: (sel_ref[0], i, 0)),
                      pl.BlockSpec((q, tr, cols), lambda i, sel_ref: (0, i, 0))],
            out_specs=pl.BlockSpec((tr, cols), lambda i, sel_ref: (i, 0))),
        out_shape=jax.ShapeDtypeStruct((m, cols), F32),
        compiler_params=_cp(("parallel",)),
    )(sel, stack, others)


BIG = ("ssd_w_in", "ssd_w_out", "fox_w_in", "fox_w_out", "ffn_w_up", "ffn_w_down")
COL_SHARDED = ("ssd_w_in", "fox_w_in", "ffn_w_up")
SMALL = (("mix_norm_g", (4, 1024)), ("ffn_norm_g", (4, 1024)), ("ssd_conv_w", (2, 4, 3072)), ("ssd_conv_b", (2, 3072)),
         ("ssd_dt_bias", (2, 32)), ("ssd_a_log", (2, 32)), ("ssd_d", (2, 32)), ("ssd_norm_g", (2, 2048)),
         ("fox_b_f", (2, 16)), ("fox_q_norm_g", (2, 64)), ("fox_k_norm_g", (2, 64)), ("ffn_conv_w", (4, 3, 2816)),
         ("ffn_conv_b", (4, 2816)), ("final_norm_g", (1024,)), ("loss", (1,)))
NAMES = ("mix_norm_g", "ffn_norm_g", "ssd_w_in", "ssd_conv_w", "ssd_conv_b", "ssd_dt_bias", "ssd_a_log", "ssd_d",
         "ssd_norm_g", "ssd_w_out", "fox_w_in", "fox_b_f", "fox_q_norm_g", "fox_k_norm_g", "fox_w_out", "ffn_w_up",
         "ffn_conv_w", "ffn_conv_b", "ffn_w_down", "final_norm_g")


def _pack(parts):
    flat = jnp.concatenate([jnp.reshape(p, (-1,)).astype(F32) for p in parts])
    rows = -(-flat.shape[0] // (8 * LANES)) * 8
    return jnp.pad(flat, (0, rows * LANES - flat.shape[0])).reshape(rows, LANES)


def _unpack(buf, shapes):
    flat = buf.reshape(-1)
    out, off = [], 0
    for shp in shapes:
        size = 1
        for d in shp:
            size *= d
        out.append(flat[off:off + size].reshape(shp))
        off += size
    return out


def _pad_lanes(a):
    return jnp.pad(a, ((0, 0), (0, LANES - a.shape[1])))


def _pad8(w):
    return jnp.pad(w, ((0, 8 - w.shape[0]), (0, 0)))


def _ssd_forward(h, p, name, rider=None):
    s = h.shape[0]
    hn = _rms_fwd(h, p["mix_g"], gw=D_MODEL, ncol=1, name=f"{name}_norm")
    zx = _matmul(hn, p["w_zx"], mode="nn", name=f"{name}_proj")
    dtp = _matmul(hn, p["w_dt"], mode="nn", name=f"{name}_proj_dt")
    xbc, _ = _conv_fwd(zx, p["conv_w8"], p["conv_b"], kw=SSD_K, width=SSD_CONV_DIM, u_col0=SSD_DI, name=f"{name}_conv")
    dt3 = dtp[:, :SSD_H].reshape(s, SSD_G, SSD_HPG)
    dtg, dtg_t = jnp.transpose(dt3, (1, 0, 2)), jnp.transpose(dt3, (1, 2, 0))
    sp = (p["bias_r"], p["bias_c"], p["alog_r"], p["alog_c"], p["d_r"])
    y, hprev, riding = _ssd_fwd(xbc, dtg, dtg_t, *sp, name=f"{name}_scan", rider=rider)
    y2 = _rms_fwd(y, p["norm_g"], gw=SSD_DI // SSD_G, ncol=SSD_G, z=zx, name=f"{name}_gnorm")
    out = _matmul(y2, p["w_out"], mode="nn", add=h, name=f"{name}_out")
    return out, dict(h=h, hn=hn, zx=zx, xbc=xbc, dtg=dtg, dtg_t=dtg_t, y=y, hprev=hprev, y2=y2), riding


def _ssd_backward(dh1, p, a, name, rider=None):
    s = dh1.shape[0]
    g = {}
    dy2 = _matmul(dh1, p["w_out"], mode="nt", name=f"{name}_out_dx")
    g["w_out"] = _matmul(a["y2"], dh1, mode="tn", out_dtype=BF16, name=f"{name}_out_dw")
    dy, dz, g["norm_g"] = _rms_bwd(a["y"], p["norm_g"], dy2, gw=SSD_DI // SSD_G, ncol=SSD_G, z=a["zx"], name=f"{name}_gnorm_b")
    sp = (p["bias_r"], p["bias_c"], p["alog_r"], p["alog_c"], p["d_r"])
    dx, dbm, dcm, ddt, g["dt_bias"], g["a_log"], g["d"], riding = _ssd_bwd(
        a["xbc"], a["dtg"], a["dtg_t"], *sp, a["hprev"], dy, name=f"{name}_scan_b", rider=rider)
    dact = jnp.concatenate([dx, dbm, dcm], axis=1)
    dxbc, dwb, _ = _conv_bwd(a["zx"], p["conv_w8"], p["conv_b"], dact, kw=SSD_K, width=SSD_CONV_DIM, u_col0=SSD_DI,
                             name=f"{name}_conv_b")
    g["conv_w"], g["conv_b"] = dwb[:SSD_K], dwb[7]
    dzx = jnp.concatenate([dz.astype(BF16), dxbc], axis=1)
    ddtp = _pad_lanes(jnp.transpose(ddt, (1, 0, 2)).reshape(s, SSD_H))
    dhn = _matmul(dzx, p["w_zx"], mode="nt", name=f"{name}_proj_dx")
    dhn = _matmul(ddtp, p["w_dt"], mode="nt", add=dhn, name=f"{name}_proj_dt_dx")
    dw_zx = _matmul(a["hn"], dzx, mode="tn", out_dtype=BF16, name=f"{name}_proj_dw")
    dw_dt = _matmul(a["hn"], ddtp, mode="tn", out_dtype=BF16, name=f"{name}_proj_dt_dw")
    g["w_in"] = jnp.concatenate([dw_zx, dw_dt[:, :SSD_H]], axis=1)
    dh, g["mix_g"] = _rms_bwd(a["h"], p["mix_g"], dhn, gw=D_MODEL, ncol=1, add=dh1, name=f"{name}_norm_b")
    return dh, g, riding


def _fox_forward(h, p, name, rider=None):
    s = h.shape[0]
    hn = _rms_fwd(h, p["mix_g"], gw=D_MODEL, ncol=1, name=f"{name}_norm")
    qkvg = _matmul(hn, p["w_qkvg"], mode="nn", name=f"{name}_proj")
    fp = _matmul(hn, p["w_f"], mode="nn", name=f"{name}_proj_f")
    qs = _rms_fwd(qkvg, p["gq"] * FOX_SCALE, gw=FOX_D, ncol=1, x_col0=0, sub=FOX_HD, name=f"{name}_qnorm")
    kn = _rms_fwd(qkvg, p["gk"], gw=FOX_D, ncol=1, x_col0=1, sub=FOX_HD, name=f"{name}_knorm")
    f_t = jnp.transpose(fp[:, :FOX_H])
    cum_t = _fgate_fwd(f_t, p["b_f"], name=f"{name}_fgate")
    ck = cum_t.reshape(FOX_PAIRS, 2, s)
    o, lse, riding = _flash_fwd(qs, kn, qkvg, ck, name=f"{name}_attn", rider=rider)
    og = _ogate_fwd(o, qkvg, name=f"{name}_ogate")
    out = _matmul(og, p["w_out"], mode="nn", add=h, name=f"{name}_out")
    return out, dict(h=h, hn=hn, qkvg=qkvg, qs=qs, kn=kn, f_t=f_t, ck=ck, o=o, lse=lse, og=og), riding


def _fox_backward(dh1, p, a, name, rider=None):
    s = dh1.shape[0]
    g = {}
    dog = _matmul(dh1, p["w_out"], mode="nt", name=f"{name}_out_dx")
    g["w_out"] = _matmul(a["og"], dh1, mode="tn", out_dtype=BF16, name=f"{name}_out_dw")
    do, dgate, delta = _ogate_bwd(dog, a["o"], a["qkvg"], name=f"{name}_ogate_b")
    swap = lambda v: jnp.transpose(v, (0, 2, 1))
    dl_t = jnp.transpose(delta.reshape(s, FOX_PAIRS, 2), (1, 2, 0))
    dq, dk, dv, dcq, dck, riding = _flash_bwd(a["qs"], a["kn"], a["qkvg"], do, swap(a["lse"]), dl_t, swap(a["ck"]),
                                              name=f"{name}_attn_b", rider=rider)
    dq_raw, dgq = _rms_bwd(a["qkvg"], p["gq"], dq, gw=FOX_D, ncol=1, x_col0=0, sub=FOX_HD, dx_dtype=BF16, name=f"{name}_qnorm_b")
    dk_raw, dgk = _rms_bwd(a["qkvg"], p["gk"], dk, gw=FOX_D, ncol=1, x_col0=1, sub=FOX_HD, dx_dtype=BF16, name=f"{name}_knorm_b")
    g["gq"] = dgq.reshape(FOX_H, FOX_HD).sum(axis=0)
    g["gk"] = dgk.reshape(FOX_H, FOX_HD).sum(axis=0)
    df_t, dbf = _fgate_bwd(dcq.reshape(FOX_H, s), swap(dck).reshape(FOX_H, s), a["f_t"], p["b_f"], name=f"{name}_fgate_b")
    g["b_f"] = dbf[:, 0]
    dproj = jnp.concatenate([dq_raw, dk_raw, dv.astype(BF16), dgate], axis=1)
    dfp = _pad_lanes(jnp.transpose(df_t))
    dhn = _matmul(dproj, p["w_qkvg"], mode="nt", name=f"{name}_proj_dx")
    dhn = _matmul(dfp, p["w_f"], mode="nt", add=dhn, name=f"{name}_proj_f_dx")
    dw_qkvg = _matmul(a["hn"], dproj, mode="tn", out_dtype=BF16, name=f"{name}_proj_dw")
    dw_f = _matmul(a["hn"], dfp, mode="tn", out_dtype=BF16, name=f"{name}_proj_f_dw")
    g["w_in"] = jnp.concatenate([dw_qkvg, dw_f[:, :FOX_H]], axis=1)
    dh, g["mix_g"] = _rms_bwd(a["h"], p["mix_g"], dhn, gw=D_MODEL, ncol=1, add=dh1, name=f"{name}_norm_b")
    return dh, g, riding


def _ffn_forward(h, p, name, rider=None):
    hn = _rms_fwd(h, p["ffn_g"], gw=D_MODEL, ncol=1, name=f"{name}_norm")
    u = _matmul(hn, p["w_up"], mode="nn", name=f"{name}_up")
    act, riding = _conv_fwd(u, p["conv_w8"], p["conv_b"], kw=FFN_K, width=D_FF, u_col0=0, mul_col0=D_FF, out_dtype=BF16,
                            name=f"{name}_glu", rider=rider)
    out = _matmul(act, p["w_down"], mode="nn", add=h, name=f"{name}_down")
    return out, dict(h=h, hn=hn, u=u, act=act), riding


def _ffn_backward(dh2, p, a, name, rider=None):
    g = {}
    dact = _matmul(dh2, p["w_down"], mode="nt", name=f"{name}_down_dx")
    g["w_down"] = _matmul(a["act"], dh2, mode="tn", out_dtype=BF16, name=f"{name}_down_dw")
    du1, du2, dwb, riding = _conv_bwd(a["u"], p["conv_w8"], p["conv_b"], dact, kw=FFN_K, width=D_FF, u_col0=0,
                                      mul_col0=D_FF, name=f"{name}_glu_b", rider=rider)
    g["conv_w"], g["conv_b"] = dwb[:FFN_K], dwb[7]
    du = jnp.concatenate([du1, du2], axis=1)
    dhn = _matmul(du, p["w_up"], mode="nt", name=f"{name}_up_dx")
    g["w_up"] = _matmul(a["hn"], du, mode="tn", out_dtype=BF16, name=f"{name}_up_dw")
    dh, g["ffn_g"] = _rms_bwd(a["h"], p["ffn_g"], dhn, gw=D_MODEL, ncol=1, add=dh2, name=f"{name}_norm_b")
    return dh, g, riding


def _to_slabs(dw, col_sharded):
    rows, cols = dw.shape
    if col_sharded:
        return jnp.transpose(dw.reshape(rows, 4, cols // 4), (1, 0, 2)).reshape(4, 2, rows // 2, cols // 4)
    return dw.reshape(4, 2, rows // 8, cols)


def kernel(x, mix_norm_g, ffn_norm_g, ssd_w_in, ssd_conv_w, ssd_conv_b, ssd_dt_bias, ssd_a_log, ssd_d, ssd_norm_g, ssd_w_out, fox_w_in, fox_b_f, fox_q_norm_g, fox_k_norm_g, fox_w_out, ffn_w_up, ffn_conv_w, ffn_conv_b, ffn_w_down, final_norm_g, loss_target, m_mix_norm_g, m_ffn_norm_g, m_ssd_w_in, m_ssd_conv_w, m_ssd_conv_b, m_ssd_dt_bias, m_ssd_a_log, m_ssd_d, m_ssd_norm_g, m_ssd_w_out, m_fox_w_in, m_fox_b_f, m_fox_q_norm_g, m_fox_k_norm_g, m_fox_w_out, m_ffn_w_up, m_ffn_conv_w, m_ffn_conv_b, m_ffn_w_down, m_final_norm_g, v_mix_norm_g, v_ffn_norm_g, v_ssd_w_in, v_ssd_conv_w, v_ssd_conv_b, v_ssd_dt_bias, v_ssd_a_log, v_ssd_d, v_ssd_norm_g, v_ssd_w_out, v_fox_w_in, v_fox_b_f, v_fox_q_norm_g, v_fox_k_norm_g, v_fox_w_out, v_ffn_w_up, v_ffn_conv_w, v_ffn_conv_b, v_ffn_w_down, v_final_norm_g):
    w = dict(mix_norm_g=mix_norm_g, ffn_norm_g=ffn_norm_g, ssd_w_in=ssd_w_in, ssd_conv_w=ssd_conv_w, ssd_conv_b=ssd_conv_b,
             ssd_dt_bias=ssd_dt_bias, ssd_a_log=ssd_a_log, ssd_d=ssd_d, ssd_norm_g=ssd_norm_g, ssd_w_out=ssd_w_out,
             fox_w_in=fox_w_in, fox_b_f=fox_b_f, fox_q_norm_g=fox_q_norm_g, fox_k_norm_g=fox_k_norm_g, fox_w_out=fox_w_out,
             ffn_w_up=ffn_w_up, ffn_conv_w=ffn_conv_w, ffn_conv_b=ffn_conv_b, ffn_w_down=ffn_w_down, final_norm_g=final_norm_g)
    m_in = dict(zip(NAMES, (m_mix_norm_g, m_ffn_norm_g, m_ssd_w_in, m_ssd_conv_w, m_ssd_conv_b, m_ssd_dt_bias, m_ssd_a_log,
                            m_ssd_d, m_ssd_norm_g, m_ssd_w_out, m_fox_w_in, m_fox_b_f, m_fox_q_norm_g, m_fox_k_norm_g,
                            m_fox_w_out, m_ffn_w_up, m_ffn_conv_w, m_ffn_conv_b, m_ffn_w_down, m_final_norm_g)))
    v_in = dict(zip(NAMES, (v_mix_norm_g, v_ffn_norm_g, v_ssd_w_in, v_ssd_conv_w, v_ssd_conv_b, v_ssd_dt_bias, v_ssd_a_log,
                            v_ssd_d, v_ssd_norm_g, v_ssd_w_out, v_fox_w_in, v_fox_b_f, v_fox_q_norm_g, v_fox_k_norm_g,
                            v_fox_w_out, v_ffn_w_up, v_ffn_conv_w, v_ffn_conv_b, v_ffn_w_down, v_final_norm_g)))
    cx, cy, cc = _coords()
    chip = 2 * cx + cy
    h = x[0]
    target = loss_target[0]

    conv_shapes = [ssd_conv_w.shape, ffn_conv_w.shape]
    slots = _allgather_small(_pack([ssd_conv_w, ffn_conv_w]), name="gather_conv_w", with_sum=False)[0]
    per_chip = [_unpack(slots[2 * q], conv_shapes) for q in range(4)]
    ssd_conv_full = jnp.concatenate([pc[0] for pc in per_chip], axis=2)
    ffn_conv_full = jnp.concatenate([pc[1] for pc in per_chip], axis=2)
    low = {n: w[n].astype(BF16) for n in BIG}
    sub_weights = dict(ssd=("ssd_w_in", "ssd_w_out"), fox=("fox_w_in", "fox_w_out"), ffn=("ffn_w_up", "ffn_w_down"))

    def shards_of(kind, idx):
        return [low[n][idx].reshape(2, low[n].shape[1] // 2, low[n].shape[2]) for n in sub_weights[kind]]

    def assemble(kind, idx, gathered):
        full = []
        for n, own, gth in zip(sub_weights[kind], shards_of(kind, idx), gathered):
            gth = lax.dynamic_update_slice(gth, own[None], (chip, 0, 0, 0))
            _, _, half, cols = gth.shape
            if n in COL_SHARDED:
                full.append(jnp.transpose(gth.reshape(4, 2 * half, cols), (1, 0, 2)).reshape(2 * half, 4 * cols))
            else:
                full.append(gth.reshape(8 * half, cols))
        return full

    def ssd_params(j, i, weights):
        w_in, w_out = weights
        g3 = lambda v: v.reshape(SSD_G, 1, SSD_HPG)
        g3c = lambda v: v.reshape(SSD_G, SSD_HPG, 1)
        return dict(mix_g=mix_norm_g[i][None], w_zx=w_in[:, :SSD_ZX], w_dt=_pad_lanes(w_in[:, SSD_ZX:]),
                    conv_w8=_pad8(ssd_conv_full[j]), conv_b=ssd_conv_b[j][None], bias_r=g3(ssd_dt_bias[j]),
                    bias_c=g3c(ssd_dt_bias[j]), alog_r=g3(ssd_a_log[j]), alog_c=g3c(ssd_a_log[j]), d_r=g3(ssd_d[j]),
                    norm_g=ssd_norm_g[j][None], w_out=w_out)

    def fox_params(j, i, weights):
        w_in, w_out = weights
        return dict(mix_g=mix_norm_g[i][None], w_qkvg=w_in[:, :4 * FOX_D], w_f=_pad_lanes(w_in[:, 4 * FOX_D:]),
                    gq=jnp.tile(fox_q_norm_g[j], FOX_H)[None], gk=jnp.tile(fox_k_norm_g[j], FOX_H)[None],
                    b_f=fox_b_f[j][:, None], w_out=w_out)

    def ffn_params(i, weights):
        w_up, w_down = weights
        return dict(ffn_g=ffn_norm_g[i][None], w_up=w_up, conv_w8=_pad8(ffn_conv_full[i]), conv_b=ffn_conv_b[i][None],
                    w_down=w_down)

    order = [("ssd", 0), ("ffn", 0), ("fox", 0), ("ffn", 1), ("ssd", 1), ("ffn", 2), ("fox", 1), ("ffn", 3)]
    fetch = {("ssd", 0): [("ffn", 0)], ("ffn", 0): [("fox", 0)], ("fox", 0): [("ffn", 1), ("ssd", 1), ("ffn", 2)],
             ("ssd", 1): [("fox", 1)], ("fox", 1): [("ffn", 3)]}
    ready = {("ssd", 0): assemble("ssd", 0, _run_rider(_Gather(shards_of("ssd", 0)), name="gather_first"))}
    params, acts = {}, {}
    forward = dict(ssd=_ssd_forward, fox=_fox_forward, ffn=_ffn_forward)
    for kind, idx in order:
        if kind == "ssd":
            params[kind, idx] = ssd_params(idx, 2 * idx, ready.pop((kind, idx)))
        elif kind == "fox":
            params[kind, idx] = fox_params(idx, 2 * idx + 1, ready.pop((kind, idx)))
        else:
            params[kind, idx] = ffn_params(idx, ready.pop((kind, idx)))
        wanted = fetch.get((kind, idx), [])
        rider = _Gather([s for sub in wanted for s in shards_of(*sub)]) if wanted else None
        h, acts[kind, idx], riding = forward[kind](h, params[kind, idx], f"{kind}{idx}", rider=rider)
        for q, sub in enumerate(wanted):
            ready[sub] = assemble(*sub, riding[2 * q:2 * q + 2])
    loss_part, dh, d_final_g = _loss_head(h, final_norm_g[None], target, name="loss_head")

    backward = dict(ssd=_ssd_backward, fox=_fox_backward, ffn=_ffn_backward)
    grad_keys = dict(ssd=("w_in", "w_out"), fox=("w_in", "w_out"), ffn=("w_up", "w_down"))
    sub_g, slabs, received = {}, {}, {}
    waiting = []
    for kind, idx in reversed(order):
        riders = waiting if (kind != "ffn" and waiting) else []
        rider = _Exchange([s for sub in riders for s in slabs[sub]]) if riders else None
        dh, sub_g[kind, idx], riding = backward[kind](dh, params[kind, idx], acts[kind, idx], f"{kind}{idx}", rider=rider)
        for q, sub in enumerate(riders):
            received[sub] = riding[2 * q:2 * q + 2]
        waiting = [sub for sub in waiting if sub not in riders]
        slabs[kind, idx] = [_to_slabs(sub_g[kind, idx][key], n in COL_SHARDED)
                            for key, n in zip(grad_keys[kind], sub_weights[kind])]
        waiting.append((kind, idx))
    last = _run_rider(_Exchange([s for sub in waiting for s in slabs[sub]]), name="rs_last_exchange")
    for q, sub in enumerate(waiting):
        received[sub] = last[2 * q:2 * q + 2]
    grad_x = dh[None]
    ssd_g, fox_g = [sub_g["ssd", 0], sub_g["ssd", 1]], [sub_g["fox", 0], sub_g["fox", 1]]
    mix_g = [ssd_g[0], fox_g[0], ssd_g[1], fox_g[1]]
    ffn_g = [sub_g["ffn", i] for i in range(DEPTH)]

    me = jnp.reshape(2 * chip + cc, (1,)).astype(jnp.int32)
    finals = {}
    for sub in order:
        for q, (slab, got) in enumerate(zip(slabs[sub], received[sub])):
            _, _, m, cols = slab.shape
            finals[sub, q] = _add_selected(slab.reshape(8, m, cols), got, me, name=f"rs_add_{sub[0]}{sub[1]}_{q}")
    keys = list(finals)
    others = dict(zip(keys, _sibling_swap([finals[key] for key in keys], name="rs_result_swap")))
    grads = {}
    for kind, names in sub_weights.items():
        for q, n in enumerate(names):
            subs = [sub for sub in sorted(set(order)) if sub[0] == kind]
            mine = jnp.stack([finals[sub, q] for sub in subs])
            theirs = jnp.stack([others[sub, q] for sub in subs])
            halves = jnp.stack([jnp.where(cc == 0, mine, theirs), jnp.where(cc == 0, theirs, mine)], axis=1)
            grads[n] = halves.reshape(w[n].shape)
    small = dict(
        mix_norm_g=jnp.concatenate([g["mix_g"] for g in mix_g], axis=0),
        ffn_norm_g=jnp.concatenate([g["ffn_g"] for g in ffn_g], axis=0),
        ssd_conv_w=jnp.stack([g["conv_w"] for g in ssd_g]), ssd_conv_b=jnp.stack([g["conv_b"] for g in ssd_g]),
        ssd_dt_bias=jnp.stack([g["dt_bias"].reshape(SSD_H) for g in ssd_g]),
        ssd_a_log=jnp.stack([g["a_log"].reshape(SSD_H) for g in ssd_g]),
        ssd_d=jnp.stack([g["d"].reshape(SSD_H) for g in ssd_g]),
        ssd_norm_g=jnp.concatenate([g["norm_g"] for g in ssd_g], axis=0),
        fox_b_f=jnp.stack([g["b_f"] for g in fox_g]), fox_q_norm_g=jnp.stack([g["gq"] for g in fox_g]),
        fox_k_norm_g=jnp.stack([g["gk"] for g in fox_g]),
        ffn_conv_w=jnp.stack([g["conv_w"] for g in ffn_g]), ffn_conv_b=jnp.stack([g["conv_b"] for g in ffn_g]),
        final_norm_g=d_final_g[0], loss=loss_part[0, :1])
    _, total = _allgather_small(_pack([small[n] for n, _ in SMALL]), name="reduce_small", with_sum=True)
    for (n, shp), val in zip(SMALL, _unpack(total, [shp for _, shp in SMALL])):
        grads[n] = val
    loss = grads.pop("loss")[0]
    grads["ssd_conv_w"] = lax.dynamic_slice_in_dim(grads["ssd_conv_w"], chip * ssd_conv_w.shape[2], ssd_conv_w.shape[2], axis=2)
    grads["ffn_conv_w"] = lax.dynamic_slice_in_dim(grads["ffn_conv_w"], chip * ffn_conv_w.shape[2], ffn_conv_w.shape[2], axis=2)

    deltas, new_m, new_v = {}, {}, {}
    for n in NAMES:
        shp = w[n].shape
        two_d = (1, shp[0]) if len(shp) == 1 else (-1, shp[-1])
        r2 = lambda a: a.reshape(two_d)
        d, nm, nv = _adamw(r2(w[n]), r2(grads[n]), r2(m_in[n]), r2(v_in[n]), name=f"adamw_{n}")
        deltas[n], new_m[n], new_v[n] = d.reshape(shp), nm.reshape(shp), nv.reshape(shp)
    return (loss, grad_x, *[grads[n] for n in NAMES], *[deltas[n] for n in NAMES], *[new_m[n] for n in NAMES],
            *[new_v[n] for n in NAMES])
```

```python
import functools

import jax
import jax.numpy as jnp
from jax import lax
from jax.experimental import pallas as pl
from jax.experimental.pallas import tpu as pltpu

F32 = jnp.float32
BF16 = jnp.bfloat16
HI = lax.Precision.HIGHEST
MESH = pl.DeviceIdType.MESH

D_MODEL = 1024
DEPTH = 4
EPS = 1e-6
SSD_DI = 2048
SSD_HD = 64
SSD_G = 4
SSD_HPG = 8
SSD_N = 128
SSD_K = 4
CHUNK = 128
SSD_CONV_DIM = 3072
SSD_ZX = SSD_DI + SSD_CONV_DIM
SSD_H = 32
FOX_HD = 64
FOX_H = 16
FOX_D = 1024
D_FF = 2816
FFN_K = 3
LANES = 128
VMEM_LIMIT = 56 * 1024 * 1024

ADAM_LR = 0.001
ADAM_B1 = 0.9
ADAM_B2 = 0.999
ADAM_EPS = 1e-08
ADAM_WD = 0.01
ADAM_STEP = 10

NN = (((1,), (0,)), ((), ()))
NT = (((1,), (1,)), ((), ()))
TN = (((0,), (0,)), ((), ()))


def _pick(n, cap, mult=LANES):
    best = None
    for t in range(mult, min(n, cap) + 1, mult):
        if n % t == 0:
            best = t
    return best if best is not None else n


def _cp(sem):
    return pltpu.CompilerParams(dimension_semantics=sem, vmem_limit_bytes=VMEM_LIMIT)


def _sigmoid(x):
    return jax.nn.sigmoid(x)


def _silu(x):
    return x * _sigmoid(x)


def _dsilu(x):
    s = _sigmoid(x)
    return s * (1.0 + x * (1.0 - s))


def _softplus(x):
    e = jnp.exp(-jnp.abs(x))
    u = 1.0 + e
    l1p = jnp.where(u == 1.0, e, jnp.log(u) * (e / (u - 1.0)))
    return jnp.maximum(x, 0.0) + l1p


def _dotf(a, b, dn=NN, *, onehot="b", pieces=2):
    x, e = (a, b) if onehot == "b" else (b, a)
    e = e.astype(BF16)
    acc = None
    for n in range(pieces):
        hi = x.astype(BF16)
        part = lax.dot_general(hi, e, dn, preferred_element_type=F32) if onehot == "b" else \
            lax.dot_general(e, hi, dn, preferred_element_type=F32)
        acc = part if acc is None else acc + part
        if n + 1 < pieces:
            x = x - hi.astype(F32)
    return acc


def _dotb(a, b, dn=NN):
    return lax.dot_general(a.astype(BF16), b.astype(BF16), dn, preferred_element_type=F32)


def _group_matrix(width, sub, transpose=False):
    ng = width // sub
    shape = (ng, width) if transpose else (width, ng)
    lane = lax.broadcasted_iota(jnp.int32, shape, 1 if transpose else 0)
    grp = lax.broadcasted_iota(jnp.int32, shape, 0 if transpose else 1)
    return (lane // sub == grp).astype(F32)


def _gmean(v, sub):
    width = v.shape[-1]
    if sub == width:
        return jnp.mean(v, axis=-1, keepdims=True)
    s = _dotf(v, _group_matrix(width, sub))
    return _dotf(s, _group_matrix(width, sub, transpose=True)) * (1.0 / sub)


def _matmul(a, b, *, mode, name, out_dtype=F32, add=None):
    a_planes = a.shape[0] if (mode == "nt" and a.ndim == 3) else 0
    b_planes = b.shape[0] if (mode == "tn" and b.ndim == 3) else 0
    a2 = (a.shape[1], a.shape[0] * a.shape[2]) if a_planes else a.shape
    b2 = (b.shape[1], b.shape[0] * b.shape[2]) if b_planes else b.shape
    if mode == "nn":
        (m, k), (k2, n) = a2, b2
    elif mode == "nt":
        (m, k), (n, k2) = a2, b2
    else:
        (k, m), (k2, n) = a2, b2
    assert k == k2, (a.shape, b.shape, mode)
    tm, tn = _pick(m, 1536), _pick(n // b_planes if b_planes else n, 1536)
    tk = _pick(k // a_planes if a_planes else k, 1536)
    nk = k // tk
    dn = {"nn": NN, "nt": NT, "tn": TN}[mode]
    has_add = add is not None

    def body(*refs):
        if has_add:
            a_ref, b_ref, add_ref, o_ref, acc_ref = refs
        else:
            a_ref, b_ref, o_ref, acc_ref = refs
            add_ref = None
        kk = pl.program_id(2)
        part = _dotb(a_ref[0] if a_planes else a_ref[...], b_ref[0] if b_planes else b_ref[...], dn)

        def finish(r):
            if has_add:
                r = r + add_ref[...]
            o_ref[...] = r.astype(out_dtype)

        if nk == 1:
            finish(part)
        else:
            @pl.when(kk == 0)
            def _():
                acc_ref[...] = part

            @pl.when(kk > 0)
            def _():
                acc_ref[...] += part

            @pl.when(kk == nk - 1)
            def _():
                finish(acc_ref[...])

    if mode == "nn":
        a_spec = pl.BlockSpec((tm, tk), lambda i, j, q: (i, q))
        b_spec = pl.BlockSpec((tk, tn), lambda i, j, q: (q, j))
    elif mode == "nt":
        per = (k // a_planes) // tk if a_planes else 0
        a_spec = (pl.BlockSpec((1, tm, tk), lambda i, j, q: (q // per, i, q % per)) if a_planes
                  else pl.BlockSpec((tm, tk), lambda i, j, q: (i, q)))
        b_spec = pl.BlockSpec((tn, tk), lambda i, j, q: (j, q))
    else:
        per = (n // b_planes) // tn if b_planes else 0
        a_spec = pl.BlockSpec((tk, tm), lambda i, j, q: (q, i))
        b_spec = (pl.BlockSpec((1, tk, tn), lambda i, j, q: (j // per, q, j % per)) if b_planes
                  else pl.BlockSpec((tk, tn), lambda i, j, q: (q, j)))
    o_spec = pl.BlockSpec((tm, tn), lambda i, j, q: (i, j))
    in_specs = [a_spec, b_spec] + ([o_spec] if has_add else [])
    args = (a, b) + ((add,) if has_add else ())
    return pl.pallas_call(
        body, name=name, grid=(m // tm, n // tn, nk), in_specs=in_specs, out_specs=o_spec,
        out_shape=jax.ShapeDtypeStruct((m, n), out_dtype),
        scratch_shapes=[pltpu.VMEM((tm, tn) if nk > 1 else (8, LANES), F32)],
        compiler_params=_cp(("parallel", "parallel", "arbitrary")),
    )(*args)


def _rms_fwd(x, g, *, gw, ncol, name, x_col0=0, sub=None, z=None, z_col0=0, out_dtype=BF16):
    rows = x.shape[0]
    tr = _pick(rows, 512, 8)
    sub = gw if sub is None else sub
    gated = z is not None

    def body(*refs):
        if gated:
            x_ref, z_ref, g_ref, o_ref = refs
            xv = x_ref[...] * _silu(z_ref[...])
        else:
            x_ref, g_ref, o_ref = refs
            xv = x_ref[...]
        r = lax.rsqrt(_gmean(xv * xv, sub) + EPS)
        o_ref[...] = (xv * r * g_ref[...]).astype(out_dtype)

    specs = [pl.BlockSpec((tr, gw), lambda j, i: (i, x_col0 + j))]
    args = [x]
    if gated:
        specs.append(pl.BlockSpec((tr, gw), lambda j, i: (i, z_col0 + j)))
        args.append(z)
    specs.append(pl.BlockSpec((1, gw), lambda j, i: (0, j)))
    args.append(g)
    return pl.pallas_call(
        body, name=name, grid=(ncol, rows // tr), in_specs=specs,
        out_specs=pl.BlockSpec((tr, gw), lambda j, i: (i, j)),
        out_shape=jax.ShapeDtypeStruct((rows, gw * ncol), out_dtype),
        compiler_params=_cp(("parallel", "parallel")),
    )(*args)


def _rms_bwd(x, g, dy, *, gw, ncol, name, x_col0=0, sub=None, z=None, z_col0=0, add=None, dx_dtype=F32):
    rows = x.shape[0]
    tr = _pick(rows, 512, 8)
    sub = gw if sub is None else sub
    gated = z is not None
    has_add = add is not None

    def body(*refs):
        refs = list(refs)
        x_ref = refs.pop(0)
        z_ref = refs.pop(0) if gated else None
        g_ref = refs.pop(0)
        dy_ref = refs.pop(0)
        add_ref = refs.pop(0) if has_add else None
        dx_ref = refs.pop(0)
        dz_ref = refs.pop(0) if gated else None
        dg_ref = refs.pop(0)
        i = pl.program_id(1)
        xv = x_ref[...]
        if gated:
            zz = z_ref[...]
            yz = xv * _silu(zz)
        else:
            yz = xv
        r = lax.rsqrt(_gmean(yz * yz, sub) + EPS)
        xh = yz * r
        dy = dy_ref[...].astype(F32)
        dyg = dy * g_ref[...]
        d_yz = r * (dyg - xh * _gmean(dyg * xh, sub))
        if gated:
            dx_ref[...] = (d_yz * _silu(zz)).astype(dx_dtype)
            dz_ref[...] = (d_yz * xv * _dsilu(zz)).astype(dx_dtype)
        elif has_add:
            dx_ref[...] = (d_yz + add_ref[...]).astype(dx_dtype)
        else:
            dx_ref[...] = d_yz.astype(dx_dtype)
        part = jnp.sum(dy * xh, axis=0, keepdims=True)

        @pl.when(i == 0)
        def _():
            dg_ref[...] = part

        @pl.when(i > 0)
        def _():
            dg_ref[...] += part

    tile = pl.BlockSpec((tr, gw), lambda j, i: (i, j))
    specs = [pl.BlockSpec((tr, gw), lambda j, i: (i, x_col0 + j))]
    args = [x]
    if gated:
        specs.append(pl.BlockSpec((tr, gw), lambda j, i: (i, z_col0 + j)))
        args.append(z)
    specs += [pl.BlockSpec((1, gw), lambda j, i: (0, j)), tile]
    args += [g, dy]
    if has_add:
        specs.append(tile)
        args.append(add)
    width = gw * ncol
    out_shape = [jax.ShapeDtypeStruct((rows, width), dx_dtype)]
    out_specs = [tile]
    if gated:
        out_shape.append(jax.ShapeDtypeStruct((rows, width), dx_dtype))
        out_specs.append(tile)
    out_shape.append(jax.ShapeDtypeStruct((1, width), F32))
    out_specs.append(pl.BlockSpec((1, gw), lambda j, i: (0, j)))
    return pl.pallas_call(
        body, name=name, grid=(ncol, rows // tr), in_specs=specs, out_specs=out_specs, out_shape=out_shape,
        compiler_params=_cp(("parallel", "arbitrary")),
    )(*args)


HALO = 8


def _conv_rows(tc):
    return 16 * 8 * LANES // tc


def _conv_fwd(u, w8, b, *, kw, width, name, u_col0=0, mul_col0=None, out_dtype=F32, rider=None):
    rows = u.shape[0]
    ts = _pick(rows, 512, 8)
    tc = _pick(width, 512)
    gated = mul_col0 is not None
    c0 = u_col0 // tc
    m0 = (mul_col0 // tc) if gated else 0
    assert u_col0 % tc == 0 and (not gated or mul_col0 % tc == 0)

    def body(*refs):
        if gated:
            cur_ref, halo_ref, mul_ref, w_ref, b_ref, o_ref, ext = refs
        else:
            cur_ref, halo_ref, w_ref, b_ref, o_ref, ext = refs
        i = pl.program_id(0)
        ext[pl.ds(0, HALO), :] = jnp.where(i == 0, 0.0, halo_ref[...])
        ext[pl.ds(HALO, ts), :] = cur_ref[...]
        bias = b_ref[...]
        taps = [w_ref[k:k + 1, :] for k in range(kw)]
        rb = _conv_rows(tc)
        for r0 in range(0, ts, rb):
            pre = bias + taps[0] * ext[pl.ds(r0 + HALO - (kw - 1), rb), :]
            for k in range(1, kw):
                pre = pre + taps[k] * ext[pl.ds(r0 + HALO - (kw - 1) + k, rb), :]
            act = _silu(pre)
            if gated:
                act = act * mul_ref[pl.ds(r0, rb), :]
            o_ref[pl.ds(r0, rb), :] = act.astype(out_dtype)

    hb = ts // HALO
    specs = [pl.BlockSpec((ts, tc), lambda i, j: (i, c0 + j)),
             pl.BlockSpec((HALO, tc), lambda i, j: (jnp.maximum(i * hb - 1, 0), c0 + j))]
    args = [u, u]
    if gated:
        specs.append(pl.BlockSpec((ts, tc), lambda i, j: (i, m0 + j)))
        args.append(u)
    specs += [pl.BlockSpec((8, tc), lambda i, j: (0, j)), pl.BlockSpec((1, tc), lambda i, j: (0, j))]
    args += [w8, b]
    outs, riding = _call(
        body, name=name, grid=(rows // ts, width // tc), in_specs=specs,
        out_specs=[pl.BlockSpec((ts, tc), lambda i, j: (i, j))],
        out_shape=[jax.ShapeDtypeStruct((rows, width), out_dtype)],
        scratch_shapes=[pltpu.VMEM((ts + HALO, tc), F32)], sem=("parallel", "parallel"), rider=rider, args=args)
    return outs + [riding]


def _conv_bwd(u, w8, b, dact, *, kw, width, name, u_col0=0, mul_col0=None, du_dtype=BF16, rider=None):
    rows = u.shape[0]
    ts = _pick(rows, 512, 8)
    tc = _pick(width, 512)
    gated = mul_col0 is not None
    c0 = u_col0 // tc
    m0 = (mul_col0 // tc) if gated else 0
    nt = rows // ts
    hb = ts // HALO

    def body(*refs):
        refs = list(refs)
        cur_ref, halo_ref = refs.pop(0), refs.pop(0)
        mul_ref = refs.pop(0) if gated else None
        w_ref, b_ref, da_ref = refs.pop(0), refs.pop(0), refs.pop(0)
        du_ref = refs.pop(0)
        dwb_ref, ext_u, ext_d = refs
        t = pl.program_id(1)
        ti = nt - 1 - t
        ext_u[pl.ds(0, HALO), :] = jnp.where(ti == 0, 0.0, halo_ref[...])
        ext_u[pl.ds(HALO, ts), :] = cur_ref[...]

        @pl.when(t == 0)
        def _():
            ext_d[pl.ds(ts, HALO), :] = jnp.zeros((HALO, tc), F32)
            dwb_ref[...] = jnp.zeros((8, tc), F32)

        bias = b_ref[...]
        taps = [w_ref[k:k + 1, :] for k in range(kw)]
        rb = _conv_rows(tc)
        dw_acc = [jnp.zeros((1, tc), F32) for _ in range(kw)]
        db_acc = jnp.zeros((1, tc), F32)
        for r0 in reversed(range(0, ts, rb)):
            shifted = [ext_u[pl.ds(r0 + HALO - (kw - 1) + k, rb), :] for k in range(kw)]
            pre = bias + taps[0] * shifted[0]
            for k in range(1, kw):
                pre = pre + taps[k] * shifted[k]
            sg = _sigmoid(pre)
            dsilu = sg * (1.0 + pre * (1.0 - sg))
            da = da_ref[pl.ds(r0, rb), :].astype(F32)
            if gated:
                du_ref[1, pl.ds(r0, rb), :] = (da * (pre * sg)).astype(du_dtype)
                dgp = da * mul_ref[pl.ds(r0, rb), :] * dsilu
            else:
                dgp = da * dsilu
            ext_d[pl.ds(r0, rb), :] = dgp
            du = taps[kw - 1] * dgp
            for k in range(kw - 1):
                du = du + taps[k] * ext_d[pl.ds(r0 + kw - 1 - k, rb), :]
            if gated:
                du_ref[0, pl.ds(r0, rb), :] = du.astype(du_dtype)
            else:
                du_ref[pl.ds(r0, rb), :] = du.astype(du_dtype)
            for k in range(kw):
                dw_acc[k] = dw_acc[k] + jnp.sum(dgp * shifted[k], axis=0, keepdims=True)
            db_acc = db_acc + jnp.sum(dgp, axis=0, keepdims=True)
        for k in range(kw):
            dwb_ref[k:k + 1, :] += dw_acc[k]
        dwb_ref[7:8, :] += db_acc
        ext_d[pl.ds(ts, HALO), :] = ext_d[pl.ds(0, HALO), :]

    specs = [pl.BlockSpec((ts, tc), lambda j, t: (nt - 1 - t, c0 + j)),
             pl.BlockSpec((HALO, tc), lambda j, t: (jnp.maximum((nt - 1 - t) * hb - 1, 0), c0 + j))]
    args = [u, u]
    if gated:
        specs.append(pl.BlockSpec((ts, tc), lambda j, t: (nt - 1 - t, m0 + j)))
        args.append(u)
    tile = pl.BlockSpec((ts, tc), lambda j, t: (nt - 1 - t, j))
    specs += [pl.BlockSpec((8, tc), lambda j, t: (0, j)), pl.BlockSpec((1, tc), lambda j, t: (0, j)), tile]
    args += [w8, b, dact]
    if gated:
        out_shape = [jax.ShapeDtypeStruct((2, rows, width), du_dtype)]
        out_specs = [pl.BlockSpec((2, ts, tc), lambda j, t: (0, nt - 1 - t, j))]
    else:
        out_shape = [jax.ShapeDtypeStruct((rows, width), du_dtype)]
        out_specs = [tile]
    out_shape.append(jax.ShapeDtypeStruct((8, width), F32))
    out_specs.append(pl.BlockSpec((8, tc), lambda j, t: (0, j)))
    outs, riding = _call(
        body, name=name, grid=(width // tc, nt), in_specs=specs, out_specs=out_specs, out_shape=out_shape,
        scratch_shapes=[pltpu.VMEM((ts + HALO, tc), F32), pltpu.VMEM((ts + HALO, tc), F32)],
        sem=("parallel", "arbitrary"), rider=rider, args=args)
    return outs + [riding]


GW = SSD_HPG * SSD_HD


def _ssd_common(x, bm, cm, dt_raw, dt_raw_t, bias_r, bias_c, alog_r, alog_c):
    row = lax.broadcasted_iota(jnp.int32, (CHUNK, CHUNK), 0)
    col = lax.broadcasted_iota(jnp.int32, (CHUNK, CHUNK), 1)
    causal = row >= col
    tril = causal.astype(F32)
    triu = (row <= col).astype(F32)
    spread = _group_matrix(GW, SSD_HD, transpose=True)
    dt = _softplus(dt_raw + bias_r)
    dt_t = _softplus(dt_raw_t + bias_c)
    a_r = -jnp.exp(alog_r)
    a_c = -jnp.exp(alog_c)
    acs = _dotf(tril, dt * a_r, onehot="a", pieces=3)
    acs_t = _dotf(dt_t * a_c, triu, pieces=3)
    last = acs[CHUNK - 1:CHUNK, :]
    ds = jnp.exp(last - acs)
    cd = jnp.exp(last)
    c = dict(causal=causal, tril=tril, triu=triu, spread=spread, dt=dt, a_r=a_r, acs=acs, acs_t=acs_t, ds=ds, cd=cd)
    c["eb"] = _dotf(jnp.exp(acs), spread)
    c["dsb"] = _dotf(ds, spread)
    c["cdb"] = _dotf(cd, spread)
    c["dtb"] = _dotf(dt, spread)
    c["xdt"] = x * c["dtb"]
    c["cb"] = _dotb(cm, bm, NT)
    return c


def _ssd_lam(c, r):
    diff = c["acs"][:, r:r + 1] - c["acs_t"][r:r + 1, :]
    return jnp.exp(jnp.where(c["causal"], diff, -jnp.inf))


def _ssd_specs(nc, rev):
    def ci(t):
        return (nc - 1 - t) if rev else t
    xs = pl.BlockSpec((CHUNK, GW), lambda g, t: (ci(t), g))
    bs = pl.BlockSpec((CHUNK, SSD_N), lambda g, t: (ci(t), SSD_DI // SSD_N + g))
    cs = pl.BlockSpec((CHUNK, SSD_N), lambda g, t: (ci(t), SSD_DI // SSD_N + SSD_G + g))
    dts = pl.BlockSpec((1, CHUNK, 8), lambda g, t: (g, ci(t), 0))
    dtts = pl.BlockSpec((1, 8, CHUNK), lambda g, t: (g, 0, ci(t)))
    pr = pl.BlockSpec((1, 1, 8), lambda g, t: (g, 0, 0))
    pc = pl.BlockSpec((1, 8, 1), lambda g, t: (g, 0, 0))
    hs = pl.BlockSpec((1, 1, SSD_N, GW), lambda g, t: (ci(t), g, 0, 0))
    return xs, bs, cs, dts, dtts, pr, pc, hs


def _ssd_fwd(xbc, dtg, dtg_t, bias_r, bias_c, alog_r, alog_c, d_r, *, name, rider=None):
    s = xbc.shape[0]
    nc = s // CHUNK
    xs, bs, cs, dts, dtts, pr, pc, hs = _ssd_specs(nc, False)

    def body(x_ref, b_ref, c_ref, dt_ref, dtt_ref, br_ref, bc_ref, ar_ref, ac_ref, d_ref, y_ref, hp_ref, h_sc):
        t = pl.program_id(1)

        @pl.when(t == 0)
        def _():
            h_sc[...] = jnp.zeros_like(h_sc)

        x, bm, cm = x_ref[...], b_ref[...], c_ref[...]
        c = _ssd_common(x, bm, cm, dt_ref[0], dtt_ref[0], br_ref[0], bc_ref[0], ar_ref[0], ac_ref[0])
        h = h_sc[...]
        hp_ref[0, 0] = h
        xdt = c["xdt"]
        pieces = []
        for r in range(SSD_HPG):
            m = c["cb"] * _ssd_lam(c, r)
            pieces.append(_dotb(m, xdt[:, r * SSD_HD:(r + 1) * SSD_HD]))
        y = jnp.concatenate(pieces, axis=1) + c["eb"] * _dotb(cm, h) + x * _dotf(d_ref[0], c["spread"])
        y_ref[...] = y
        h_sc[...] = h * c["cdb"] + _dotb(bm, xdt * c["dsb"], TN)

    outs, riding = _call(
        body, name=name, grid=(SSD_G, nc),
        in_specs=[xs, bs, cs, dts, dtts, pr, pc, pr, pc, pr],
        out_specs=[xs, hs],
        out_shape=[jax.ShapeDtypeStruct((s, SSD_DI), F32), jax.ShapeDtypeStruct((nc, SSD_G, SSD_N, GW), F32)],
        scratch_shapes=[pltpu.VMEM((SSD_N, GW), F32)], sem=("parallel", "arbitrary"), rider=rider,
        args=(xbc, xbc, xbc, dtg, dtg_t, bias_r, bias_c, alog_r, alog_c, d_r))
    return outs + [riding]


def _ssd_bwd(xbc, dtg, dtg_t, bias_r, bias_c, alog_r, alog_c, d_r, hprev, dy, *, name, rider=None):
    s = xbc.shape[0]
    nc = s // CHUNK
    xs, bs, cs, dts, dtts, pr, pc, hs = _ssd_specs(nc, True)
    gsum = functools.partial(_group_matrix, GW, SSD_HD)

    def body(x_ref, b_ref, c_ref, dt_ref, dtt_ref, br_ref, bc_ref, ar_ref, ac_ref, d_ref, hp_ref, dy_ref,
             dx_ref, db_ref, dc_ref, ddt_ref, dbias_ref, dalog_ref, dd_ref, dh_sc):
        t = pl.program_id(1)

        @pl.when(t == 0)
        def _():
            dh_sc[...] = jnp.zeros_like(dh_sc)
            dbias_ref[...] = jnp.zeros_like(dbias_ref)
            dalog_ref[...] = jnp.zeros_like(dalog_ref)
            dd_ref[...] = jnp.zeros_like(dd_ref)

        x, bm, cm = x_ref[...], b_ref[...], c_ref[...]
        c = _ssd_common(x, bm, cm, dt_ref[0], dtt_ref[0], br_ref[0], bc_ref[0], ar_ref[0], ac_ref[0])
        lanesum = gsum()
        h = hp_ref[0, 0]
        dh = dh_sc[...]
        dy = dy_ref[...]
        xdt, dsb = c["xdt"], c["dsb"]
        skip = _dotf(d_ref[0], c["spread"])
        dd_ref[0] += jnp.sum(_dotf(dy * x, lanesum), axis=0, keepdims=True)
        dacs = _dotf(dy * (c["eb"] * _dotb(cm, h)), lanesum)
        edy = c["eb"] * dy
        dcm = _dotb(edy, h, NT)
        dh_prev = _dotb(cm, edy, TN)
        bdh = _dotb(bm, dh)
        dxdt = dsb * bdh
        dbm = _dotb(dsb * xdt, dh, NT)
        t1 = _dotf(xdt * bdh, lanesum) * c["ds"]
        dacs = dacs - t1
        dlast = jnp.sum(t1, axis=0, keepdims=True) + jnp.sum(_dotf(dh * h, lanesum), axis=0, keepdims=True) * c["cd"]
        dcb = jnp.zeros((CHUNK, CHUNK), F32)
        pieces = []
        ones8 = jnp.ones((CHUNK, 8), F32)
        head = lax.broadcasted_iota(jnp.int32, (1, 8), 1)
        for r in range(SSD_HPG):
            sl = slice(r * SSD_HD, (r + 1) * SSD_HD)
            lam = _ssd_lam(c, r)
            m = c["cb"] * lam
            dm = _dotb(dy[:, sl], xdt[:, sl], NT)
            dcb = dcb + dm * lam
            gm = dm * m
            dacs = dacs + (jnp.sum(gm, axis=1, keepdims=True) - _dotf(gm, ones8, TN, pieces=3)) * (head == r).astype(F32)
            pieces.append(_dotb(m, dy[:, sl], TN))
        dxdt = dxdt + jnp.concatenate(pieces, axis=1)
        dcm = dcm + _dotb(dcb, bm)
        dbm = dbm + _dotb(dcb, cm, TN)
        dx_ref[...] = dy * skip + dxdt * c["dtb"]
        db_ref[...] = dbm
        dc_ref[...] = dcm
        rowid = lax.broadcasted_iota(jnp.int32, (CHUNK, 8), 0)
        dacs = dacs + jnp.where(rowid == CHUNK - 1, dlast, 0.0)
        dda = _dotf(c["triu"], dacs, onehot="a", pieces=3)
        ddt = _dotf(dxdt * x, lanesum) + dda * c["a_r"]
        ddt_raw = ddt * _sigmoid(dt_ref[0] + br_ref[0])
        ddt_ref[0] = ddt_raw
        dbias_ref[0] += jnp.sum(ddt_raw, axis=0, keepdims=True)
        dalog_ref[0] += jnp.sum(dda * c["dt"], axis=0, keepdims=True) * c["a_r"]
        dh_sc[...] = dh_prev + dh * c["cdb"]

    ci = lambda t: nc - 1 - t
    nspec = pl.BlockSpec((CHUNK, SSD_N), lambda g, t: (ci(t), g))
    outs, riding = _call(
        body, name=name, grid=(SSD_G, nc),
        in_specs=[xs, bs, cs, dts, dtts, pr, pc, pr, pc, pr, hs, xs],
        out_specs=[xs, nspec, nspec, dts, pr, pr, pr],
        out_shape=[jax.ShapeDtypeStruct((s, SSD_DI), F32), jax.ShapeDtypeStruct((s, SSD_G * SSD_N), F32),
                   jax.ShapeDtypeStruct((s, SSD_G * SSD_N), F32), jax.ShapeDtypeStruct((SSD_G, s, 8), F32),
                   jax.ShapeDtypeStruct((SSD_G, 1, 8), F32), jax.ShapeDtypeStruct((SSD_G, 1, 8), F32),
                   jax.ShapeDtypeStruct((SSD_G, 1, 8), F32)],
        scratch_shapes=[pltpu.VMEM((SSD_N, GW), F32)], sem=("parallel", "arbitrary"), rider=rider,
        args=(xbc, xbc, xbc, dtg, dtg_t, bias_r, bias_c, alog_r, alog_c, d_r, hprev, dy))
    return outs + [riding]


FOX_PAIRS = FOX_H // 2
FOX_SCALE = FOX_HD ** -0.5
NEG_INF = -jnp.inf


def _fgate_fwd(f_t, b_c, *, name):
    hh, s = f_t.shape
    tb = _pick(s, 512)
    nb = s // tb

    def body(f_ref, b_ref, o_ref, carry):
        t = pl.program_id(0)

        @pl.when(t == 0)
        def _():
            carry[...] = jnp.zeros_like(carry)

        lf = -_softplus(-(f_ref[...] + b_ref[...]))
        row = lax.broadcasted_iota(jnp.int32, (tb, tb), 0)
        col = lax.broadcasted_iota(jnp.int32, (tb, tb), 1)
        cum = _dotf(lf, (row <= col).astype(F32), pieces=3) + carry[:, 0:1]
        o_ref[...] = cum
        carry[:, 0:1] = cum[:, tb - 1:tb]

    return pl.pallas_call(
        body, name=name, grid=(nb,),
        in_specs=[pl.BlockSpec((hh, tb), lambda t: (0, t)), pl.BlockSpec((hh, 1), lambda t: (0, 0))],
        out_specs=pl.BlockSpec((hh, tb), lambda t: (0, t)),
        out_shape=jax.ShapeDtypeStruct((hh, s), F32),
        scratch_shapes=[pltpu.VMEM((hh, LANES), F32)],
        compiler_params=_cp(("arbitrary",)),
    )(f_t, b_c)


def _fgate_bwd(dcum_q_t, dcum_k_t, f_t, b_c, *, name):
    hh, s = f_t.shape
    tb = _pick(s, 512)
    nb = s // tb

    def body(dq_ref, d_ref, f_ref, b_ref, df_ref, db_ref, carry):
        t = pl.program_id(0)

        @pl.when(t == 0)
        def _():
            carry[...] = jnp.zeros_like(carry)
            db_ref[...] = jnp.zeros_like(db_ref)

        d = d_ref[...] + dq_ref[...]
        row = lax.broadcasted_iota(jnp.int32, (tb, tb), 0)
        col = lax.broadcasted_iota(jnp.int32, (tb, tb), 1)
        rev = _dotf(d, (row >= col).astype(F32), pieces=3) + carry[:, 0:1]
        df = rev * _sigmoid(-(f_ref[...] + b_ref[...]))
        df_ref[...] = df
        db_ref[...] += jnp.sum(df, axis=1, keepdims=True)
        carry[:, 0:1] = rev[:, 0:1]

    blk = pl.BlockSpec((hh, tb), lambda t: (0, nb - 1 - t))
    return pl.pallas_call(
        body, name=name, grid=(nb,),
        in_specs=[blk, blk, blk, pl.BlockSpec((hh, 1), lambda t: (0, 0))],
        out_specs=[blk, pl.BlockSpec((hh, 1), lambda t: (0, 0))],
        out_shape=[jax.ShapeDtypeStruct((hh, s), F32), jax.ShapeDtypeStruct((hh, 1), F32)],
        scratch_shapes=[pltpu.VMEM((hh, LANES), F32)],
        compiler_params=_cp(("arbitrary",)),
    )(dcum_q_t, dcum_k_t, f_t, b_c)


def _fox_tile(s):
    return min(512, max(s // 2, 8))


def _tri_tables(nq, kv_major):
    if kv_major:
        pairs = [(i, j) for j in range(nq) for i in range(j, nq)]
    else:
        pairs = [(i, j) for i in range(nq) for j in range(i + 1)]
    return (jnp.asarray([p[0] for p in pairs], jnp.int32), jnp.asarray([p[1] for p in pairs], jnp.int32))


def _lane_tile(col, width):
    return col if width == LANES else jnp.tile(col, (1, width // LANES))


def _flash_fwd(qs, kn, qkvg, ck, *, name, rider=None):
    s = qs.shape[0]
    tt = _fox_tile(s)
    nq = s // tt
    itab, jtab = _tri_tables(nq, kv_major=False)
    v0 = 2 * FOX_D // LANES

    def body(itab_ref, jtab_ref, q_ref, k_ref, v_ref, ck_ref, o_ref, lse_ref, m_sc, l_sc, acc_sc):
        t = pl.program_id(1)
        i, j = itab_ref[t], jtab_ref[t]

        @pl.when(j == 0)
        def _():
            m_sc[...] = jnp.full_like(m_sc, NEG_INF)
            l_sc[...] = jnp.zeros_like(l_sc)
            acc_sc[...] = jnp.zeros_like(acc_sc)

        low = lax.broadcasted_iota(jnp.int32, (tt, LANES), 1) < FOX_HD

        def step(diagonal):
            q2, k2 = q_ref[...], k_ref[...]
            v2 = v_ref[...].astype(BF16)
            alphas, outs = [], []
            for hh in range(2):
                qh = jnp.where(low if hh == 0 else jnp.logical_not(low), q2, jnp.zeros_like(q2))
                sc = lax.dot_general(qh, k2, NT, preferred_element_type=F32) - ck_ref[0][hh:hh + 1, :]
                if diagonal:
                    row = lax.broadcasted_iota(jnp.int32, sc.shape, 0)
                    col = lax.broadcasted_iota(jnp.int32, sc.shape, 1)
                    sc = jnp.where(row >= col, sc, NEG_INF)
                m_prev = m_sc[hh]
                m_new = jnp.maximum(m_prev, jnp.max(sc, axis=1, keepdims=True))
                alpha = jnp.exp(m_prev - m_new)
                p = jnp.exp(sc - _lane_tile(m_new, tt))
                l_sc[hh] = alpha * l_sc[hh] + jnp.sum(p, axis=1, keepdims=True)
                m_sc[hh] = m_new
                alphas.append(alpha)
                outs.append(lax.dot_general(p.astype(BF16), v2, NN, preferred_element_type=F32))
            acc_sc[...] = jnp.where(low, alphas[0], alphas[1]) * acc_sc[...] + jnp.where(low, outs[0], outs[1])

        @pl.when(j < i)
        def _():
            step(False)

        @pl.when(j == i)
        def _():
            step(True)
            o_ref[...] = acc_sc[...] / jnp.where(low, l_sc[0], l_sc[1])
            lse_ref[0] = jnp.concatenate([m_sc[hh][:, 0:1] + jnp.log(l_sc[hh][:, 0:1]) for hh in range(2)], axis=1)

    outs, riding = _call(
        body, name=name, grid=(FOX_PAIRS, int(itab.shape[0])), prefetch=(itab, jtab),
        in_specs=[pl.BlockSpec((tt, LANES), lambda p, t, it, jt: (it[t], p)),
                  pl.BlockSpec((tt, LANES), lambda p, t, it, jt: (jt[t], p)),
                  pl.BlockSpec((tt, LANES), lambda p, t, it, jt: (jt[t], v0 + p)),
                  pl.BlockSpec((1, 2, tt), lambda p, t, it, jt: (p, 0, jt[t]))],
        out_specs=[pl.BlockSpec((tt, LANES), lambda p, t, it, jt: (it[t], p)),
                   pl.BlockSpec((1, tt, 2), lambda p, t, it, jt: (p, it[t], 0))],
        scratch_shapes=[pltpu.VMEM((2, tt, LANES), F32), pltpu.VMEM((2, tt, LANES), F32), pltpu.VMEM((tt, LANES), F32)],
        out_shape=[jax.ShapeDtypeStruct((s, FOX_D), F32), jax.ShapeDtypeStruct((FOX_PAIRS, s, 2), F32)],
        sem=("parallel", "arbitrary"), rider=rider, args=(qs, kn, qkvg, ck))
    return outs + [riding]


def _flash_bwd(qs, kn, qkvg, do, lse_t, delta_t, ck_c, *, name, rider=None):
    s = qs.shape[0]
    tt = _fox_tile(s)
    nq = s // tt
    nl = tt // LANES
    itab, jtab = _tri_tables(nq, kv_major=True)
    nsteps = itab.shape[0]
    v0 = 2 * FOX_D // LANES

    def body(itab_ref, jtab_ref, q_ref, k_ref, v_ref, do_ref, lse_ref, dl_ref, ck_ref,
             dq_ref, dk_ref, dv_ref, dcq_ref, dck_ref, dqt_sc, rs_sc, dk_sc, dv_sc, dc_sc, kt_sc, ckb_sc):
        t = pl.program_id(1)
        i, j = itab_ref[t], jtab_ref[t]

        @pl.when(t == 0)
        def _():
            dqt_sc[...] = jnp.zeros_like(dqt_sc)
            rs_sc[...] = jnp.zeros_like(rs_sc)

        @pl.when(i == j)
        def _():
            dk_sc[...] = jnp.zeros_like(dk_sc)
            dv_sc[...] = jnp.zeros_like(dv_sc)
            dc_sc[...] = jnp.zeros_like(dc_sc)
            kt_sc[...] = k_ref[...].astype(F32).T.astype(BF16)
            for hh in range(2):
                ckb_sc[hh] = jnp.broadcast_to(ck_ref[0][:, hh:hh + 1], (tt, LANES))

        low = lax.broadcasted_iota(jnp.int32, (tt, LANES), 1) < FOX_HD
        top = lax.broadcasted_iota(jnp.int32, (LANES, tt), 0) < FOX_HD

        def step(diagonal):
            q2, k2, kt = q_ref[...], k_ref[...], kt_sc[...]
            v2 = v_ref[...].astype(BF16)
            do2 = do_ref[...].astype(BF16)
            dqs, dks, dvs = [], [], []
            for hh in range(2):
                sel = low if hh == 0 else jnp.logical_not(low)
                qh = jnp.where(sel, q2, jnp.zeros_like(q2))
                doh = jnp.where(sel, do2, jnp.zeros_like(do2))
                st = lax.dot_general(k2, qh, NT, preferred_element_type=F32)
                st = st - _lane_tile(ckb_sc[hh], tt) - lse_ref[0][hh:hh + 1, :]
                if diagonal:
                    key = lax.broadcasted_iota(jnp.int32, st.shape, 0)
                    qry = lax.broadcasted_iota(jnp.int32, st.shape, 1)
                    st = jnp.where(qry >= key, st, NEG_INF)
                pt = jnp.exp(st)
                dpt = lax.dot_general(v2, doh, NT, preferred_element_type=F32)
                dst = pt * (dpt - dl_ref[0][hh:hh + 1, :])
                ptb, dstb = pt.astype(BF16), dst.astype(BF16)
                dvs.append(lax.dot_general(ptb, do2, NN, preferred_element_type=F32))
                dks.append(lax.dot_general(dstb, q2, NN, preferred_element_type=F32))
                dqs.append(lax.dot_general(kt, dstb, NN, preferred_element_type=F32))
                rs_sc[hh, i] += jnp.sum(dst, axis=0, keepdims=True)
                part = dst[:, 0:LANES]
                for b in range(1, nl):
                    part = part + dst[:, b * LANES:(b + 1) * LANES]
                dc_sc[hh] += part
            dv_sc[...] += jnp.where(low, dvs[0], dvs[1])
            dk_sc[...] += jnp.where(low, dks[0], dks[1])
            dqt_sc[i] += jnp.where(top, dqs[0], dqs[1])

        @pl.when(j < i)
        def _():
            step(False)

        @pl.when(j == i)
        def _():
            step(True)

        @pl.when(i == nq - 1)
        def _():
            dk_ref[...] = dk_sc[...]
            dv_ref[...] = dv_sc[...]
            dck_ref[0] = -jnp.concatenate([jnp.sum(dc_sc[hh], axis=1, keepdims=True) for hh in range(2)], axis=1)

        @pl.when(t == nsteps - 1)
        def _():
            for b in range(nq):
                dq_ref[pl.ds(b * tt, tt), :] = dqt_sc[b].T * FOX_SCALE
                dcq_ref[0, :, pl.ds(b * tt, tt)] = jnp.concatenate([rs_sc[hh, b] for hh in range(2)], axis=0)

    qside = pl.BlockSpec((tt, LANES), lambda p, t, it, jt: (it[t], p))
    kside = pl.BlockSpec((tt, LANES), lambda p, t, it, jt: (jt[t], p))
    qstat = pl.BlockSpec((1, 2, tt), lambda p, t, it, jt: (p, 0, it[t]))
    kstat = pl.BlockSpec((1, tt, 2), lambda p, t, it, jt: (p, jt[t], 0))
    outs, riding = _call(
        body, name=name, grid=(FOX_PAIRS, nsteps), prefetch=(itab, jtab),
        in_specs=[qside, kside, pl.BlockSpec((tt, LANES), lambda p, t, it, jt: (jt[t], v0 + p)), qside, qstat, qstat, kstat],
        out_specs=[pl.BlockSpec((s, LANES), lambda p, t, it, jt: (0, p)), kside, kside,
                   pl.BlockSpec((1, 2, s), lambda p, t, it, jt: (p, 0, 0)), kstat],
        scratch_shapes=[pltpu.VMEM((nq, LANES, tt), F32), pltpu.VMEM((2, nq, 1, tt), F32), pltpu.VMEM((tt, LANES), F32),
                        pltpu.VMEM((tt, LANES), F32), pltpu.VMEM((2, tt, LANES), F32), pltpu.VMEM((LANES, tt), BF16),
                        pltpu.VMEM((2, tt, LANES), F32)],
        out_shape=[jax.ShapeDtypeStruct((s, FOX_D), F32), jax.ShapeDtypeStruct((s, FOX_D), F32),
                   jax.ShapeDtypeStruct((s, FOX_D), F32), jax.ShapeDtypeStruct((FOX_PAIRS, 2, s), F32),
                   jax.ShapeDtypeStruct((FOX_PAIRS, s, 2), F32)],
        sem=("parallel", "arbitrary"), rider=rider, args=(qs, kn, qkvg, do, lse_t, delta_t, ck_c))
    return outs + [riding]


def _ogate_fwd(o, qkvg, *, name):
    s = o.shape[0]
    tr = _pick(s, 512, 8)

    def body(o_ref, g_ref, out_ref):
        out_ref[...] = (o_ref[...] * _sigmoid(g_ref[...])).astype(BF16)

    tile = pl.BlockSpec((tr, FOX_D), lambda i: (i, 0))
    return pl.pallas_call(
        body, name=name, grid=(s // tr,), in_specs=[tile, pl.BlockSpec((tr, FOX_D), lambda i: (i, 3))],
        out_specs=tile, out_shape=jax.ShapeDtypeStruct((s, FOX_D), BF16), compiler_params=_cp(("parallel",)),
    )(o, qkvg)


def _ogate_bwd(dog, o, qkvg, *, name):
    s = o.shape[0]
    tr = _pick(s, 512, 8)

    def body(dog_ref, o_ref, g_ref, do_ref, dg_ref, dl_ref):
        sg = _sigmoid(g_ref[...])
        ov = o_ref[...]
        dog_v = dog_ref[...]
        do = dog_v * sg
        do_ref[...] = do
        dg_ref[...] = (dog_v * ov * sg * (1.0 - sg)).astype(BF16)
        dl_ref[...] = _dotf(do * ov, _group_matrix(FOX_D, FOX_HD))

    tile = pl.BlockSpec((tr, FOX_D), lambda i: (i, 0))
    return pl.pallas_call(
        body, name=name, grid=(s // tr,), in_specs=[tile, tile, pl.BlockSpec((tr, FOX_D), lambda i: (i, 3))],
        out_specs=[tile, tile, pl.BlockSpec((tr, FOX_H), lambda i: (i, 0))],
        out_shape=[jax.ShapeDtypeStruct((s, FOX_D), F32), jax.ShapeDtypeStruct((s, FOX_D), BF16),
                   jax.ShapeDtypeStruct((s, FOX_H), F32)],
        compiler_params=_cp(("parallel",)),
    )(dog, o, qkvg)


def _loss_head(h, g, target, *, name):
    s, d = h.shape
    tr = _pick(s, 512, 8)

    def body(h_ref, g_ref, t_ref, loss_ref, dh_ref, dg_ref):
        i = pl.program_id(0)
        x = h_ref[...]
        gv = g_ref[...]
        r = lax.rsqrt(jnp.mean(x * x, axis=-1, keepdims=True) + EPS)
        xh = x * r
        err = xh * gv - t_ref[...]
        part = 0.5 * jnp.sum(jnp.sum(err * err, axis=1, keepdims=True) * (1.0 / d), axis=0, keepdims=True)
        dy = err * (1.0 / d)
        dyg = dy * gv
        dh_ref[...] = r * (dyg - xh * jnp.mean(dyg * xh, axis=-1, keepdims=True))
        dgp = jnp.sum(dy * xh, axis=0, keepdims=True)

        @pl.when(i == 0)
        def _():
            loss_ref[...] = jnp.zeros_like(loss_ref) + part
            dg_ref[...] = dgp

        @pl.when(i > 0)
        def _():
            loss_ref[...] += part
            dg_ref[...] += dgp

    tile = pl.BlockSpec((tr, d), lambda i: (i, 0))
    vec = pl.BlockSpec((1, d), lambda i: (0, 0))
    return pl.pallas_call(
        body, name=name, grid=(s // tr,), in_specs=[tile, vec, tile],
        out_specs=[pl.BlockSpec((1, LANES), lambda i: (0, 0)), tile, vec],
        out_shape=[jax.ShapeDtypeStruct((1, LANES), F32), jax.ShapeDtypeStruct((s, d), F32),
                   jax.ShapeDtypeStruct((1, d), F32)],
        compiler_params=_cp(("arbitrary",)),
    )(h, g, target)


def _adamw(w, g, m, v, *, name):
    rows, cols = w.shape
    tr = _pick(rows, 256, 8)
    c1 = 1.0 - ADAM_B1 ** ADAM_STEP
    c2 = 1.0 - ADAM_B2 ** ADAM_STEP

    def body(w_ref, g_ref, m_ref, v_ref, d_ref, nm_ref, nv_ref):
        gv = g_ref[...]
        nm = ADAM_B1 * m_ref[...] + (1.0 - ADAM_B1) * gv
        nv = ADAM_B2 * v_ref[...] + (1.0 - ADAM_B2) * (gv * gv)
        d_ref[...] = -ADAM_LR * ((nm / c1) / (jnp.sqrt(nv / c2) + ADAM_EPS) + ADAM_WD * w_ref[...])
        nm_ref[...] = nm
        nv_ref[...] = nv

    tile = pl.BlockSpec((tr, cols), lambda i: (i, 0))
    shp = jax.ShapeDtypeStruct((rows, cols), F32)
    return pl.pallas_call(
        body, name=name, grid=(rows // tr,), in_specs=[tile] * 4, out_specs=[tile] * 3, out_shape=[shp] * 3,
        compiler_params=_cp(("parallel",)),
    )(w, g, m, v)


ANY = pl.BlockSpec(memory_space=pl.ANY)
N_DEV = 8


def _coords():
    return lax.axis_index("x"), lax.axis_index("y"), lax.axis_index("c")


def _other_chips(x, y):
    return [(1 - x, y), (x, 1 - y), (1 - x, 1 - y)]


def _allgather_small(buf, *, name, with_sum):
    rows = buf.shape[0]

    def body(*refs):
        if with_sum:
            x_ref, out_ref, sum_ref, send_sems, recv_sems = refs
        else:
            x_ref, out_ref, send_sems, recv_sems = refs
        x, y, c = _coords()
        me = 4 * x + 2 * y + c
        out_ref[me] = x_ref[...]
        copies = []
        for rel in range(1, N_DEV):
            px = (1 - x) if rel & 4 else x
            py = (1 - y) if rel & 2 else y
            pc = (1 - c) if rel & 1 else c
            cp = pltpu.make_async_remote_copy(
                src_ref=x_ref, dst_ref=out_ref.at[me], send_sem=send_sems.at[rel - 1], recv_sem=recv_sems.at[rel - 1],
                device_id=(px, py, pc), device_id_type=MESH)
            cp.start()
            copies.append(cp)
        for cp in copies:
            cp.wait()
        if with_sum:
            acc = out_ref[0]
            for k in range(1, N_DEV):
                acc = acc + out_ref[k]
            sum_ref[...] = acc

    slots = jax.ShapeDtypeStruct((N_DEV, rows, LANES), F32)
    vm = pl.BlockSpec(memory_space=pltpu.VMEM)
    out_shape = [slots, jax.ShapeDtypeStruct((rows, LANES), F32)] if with_sum else [slots]
    return pl.pallas_call(
        body, name=name, in_specs=[vm], out_specs=[vm] * len(out_shape), out_shape=out_shape,
        scratch_shapes=[pltpu.SemaphoreType.DMA((N_DEV - 1,)), pltpu.SemaphoreType.DMA((N_DEV - 1,))],
    )(buf)


class _Gather:
    per_array = 6

    def __init__(self, arrays):
        self.arrays = list(arrays)

    def out_shapes(self):
        return [jax.ShapeDtypeStruct((4,) + a.shape, a.dtype) for a in self.arrays]

    @staticmethod
    def _ici(ins, outs, send_sems, recv_sems, t, j, px, py, c, slot):
        return pltpu.make_async_remote_copy(
            src_ref=ins[t].at[c], dst_ref=outs[t].at[slot, c], send_sem=send_sems.at[6 * t + j],
            recv_sem=recv_sems.at[6 * t + j], device_id=(px, py, c), device_id_type=MESH)

    @staticmethod
    def _d2d(outs, send_sems, recv_sems, t, j, kj, half, sibling):
        return pltpu.make_async_remote_copy(
            src_ref=outs[t].at[kj, half], dst_ref=outs[t].at[kj, half], send_sem=send_sems.at[6 * t + 3 + j],
            recv_sem=recv_sems.at[6 * t + 3 + j], device_id=sibling, device_id_type=MESH)

    def start(self, ins, outs, send_sems, recv_sems):
        x, y, c = _coords()
        for t in range(len(ins)):
            for j, (px, py) in enumerate(_other_chips(x, y)):
                self._ici(ins, outs, send_sems, recv_sems, t, j, px, py, c, 2 * x + y).start()

    def finish(self, ins, outs, send_sems, recv_sems):
        x, y, c = _coords()
        chips = _other_chips(x, y)
        sibling = (x, y, 1 - c)
        started = []
        for t in range(len(ins)):
            for j, (px, py) in enumerate(chips):
                ici = self._ici(ins, outs, send_sems, recv_sems, t, j, px, py, c, 2 * px + py)
                ici.wait_recv()
                fwd = self._d2d(outs, send_sems, recv_sems, t, j, 2 * px + py, c, sibling)
                fwd.start()
                started += [ici, fwd]
        for t in range(len(ins)):
            for j, (px, py) in enumerate(chips):
                self._d2d(outs, send_sems, recv_sems, t, j, 2 * px + py, 1 - c, sibling).wait_recv()
        for cp in started:
            cp.wait_send()


class _Exchange:
    per_array = 7

    def __init__(self, arrays):
        self.arrays = list(arrays)

    def out_shapes(self):
        return [jax.ShapeDtypeStruct((7,) + a.shape[2:], a.dtype) for a in self.arrays]

    @staticmethod
    def _copies(ins, outs, send_sems, recv_sems):
        x, y, c = _coords()
        for t in range(len(ins)):
            for rel in range(1, N_DEV):
                px = (1 - x) if rel & 4 else x
                py = (1 - y) if rel & 2 else y
                pc = (1 - c) if rel & 1 else c
                yield pltpu.make_async_remote_copy(
                    src_ref=ins[t].at[2 * px + py, pc], dst_ref=outs[t].at[rel - 1], send_sem=send_sems.at[7 * t + rel - 1],
                    recv_sem=recv_sems.at[7 * t + rel - 1], device_id=(px, py, pc), device_id_type=MESH)

    def start(self, ins, outs, send_sems, recv_sems):
        for cp in self._copies(ins, outs, send_sems, recv_sems):
            cp.start()

    def finish(self, ins, outs, send_sems, recv_sems):
        for cp in self._copies(ins, outs, send_sems, recv_sems):
            cp.wait()


def _call(body, *, name, grid, in_specs, out_specs, out_shape, scratch_shapes, args, sem, rider=None, prefetch=()):
    n_in, n_out, n_pre = len(in_specs), len(out_specs), len(prefetch)
    n_c = len(rider.arrays) if rider is not None else 0

    def wrapped(*refs):
        pre, rest = refs[:n_pre], refs[n_pre:]
        ins, cins = rest[:n_in], rest[n_in:n_in + n_c]
        outs = rest[n_in + n_c:n_in + n_c + n_out]
        couts = rest[n_in + n_c + n_out:n_in + 2 * n_c + n_out]
        scratch = rest[n_in + 2 * n_c + n_out:]
        if rider is None:
            body(*pre, *ins, *outs, *scratch)
            return
        send_sems, recv_sems = scratch[-2:]
        ids = [pl.program_id(a) for a in range(len(grid))]
        first = functools.reduce(jnp.logical_and, [i == 0 for i in ids])
        last = functools.reduce(jnp.logical_and, [i == g - 1 for i, g in zip(ids, grid)])

        @pl.when(first)
        def _():
            rider.start(cins, couts, send_sems, recv_sems)

        body(*pre, *ins, *outs, *scratch[:-2])

        @pl.when(last)
        def _():
            rider.finish(cins, couts, send_sems, recv_sems)

    if rider is not None:
        nsem = rider.per_array * n_c
        in_specs = list(in_specs) + [ANY] * n_c
        out_specs = list(out_specs) + [ANY] * n_c
        out_shape = list(out_shape) + rider.out_shapes()
        scratch_shapes = list(scratch_shapes) + [pltpu.SemaphoreType.DMA((nsem,)), pltpu.SemaphoreType.DMA((nsem,))]
        args = list(args) + rider.arrays
        sem = ("arbitrary",) * len(grid)
    if n_pre:
        res = pl.pallas_call(
            wrapped, name=name, out_shape=out_shape, compiler_params=_cp(sem),
            grid_spec=pltpu.PrefetchScalarGridSpec(num_scalar_prefetch=n_pre, grid=grid, in_specs=in_specs,
                                                   out_specs=out_specs, scratch_shapes=scratch_shapes),
        )(*prefetch, *args)
    else:
        res = pl.pallas_call(
            wrapped, name=name, grid=grid, in_specs=in_specs, out_specs=out_specs, out_shape=out_shape,
            scratch_shapes=scratch_shapes, compiler_params=_cp(sem),
        )(*args)
    return list(res[:n_out]), list(res[n_out:])


def _run_rider(rider, *, name):
    n = len(rider.arrays)

    def body(*refs):
        ins, outs = refs[:n], refs[n:2 * n]
        send_sems, recv_sems = refs[2 * n:]
        rider.start(ins, outs, send_sems, recv_sems)
        rider.finish(ins, outs, send_sems, recv_sems)

    nsem = rider.per_array * n
    return pl.pallas_call(
        body, name=name, in_specs=[ANY] * n, out_specs=[ANY] * n, out_shape=rider.out_shapes(),
        scratch_shapes=[pltpu.SemaphoreType.DMA((nsem,)), pltpu.SemaphoreType.DMA((nsem,))],
    )(*rider.arrays)


def _sibling_swap(arrs, *, name):
    n = len(arrs)

    def body(*refs):
        ins, outs = refs[:n], refs[n:2 * n]
        send_sems, recv_sems = refs[2 * n:]
        x, y, c = _coords()
        copies = []
        for t in range(n):
            cp = pltpu.make_async_remote_copy(
                src_ref=ins[t], dst_ref=outs[t], send_sem=send_sems.at[t], recv_sem=recv_sems.at[t],
                device_id=(x, y, 1 - c), device_id_type=MESH)
            cp.start()
            copies.append(cp)
        for cp in copies:
            cp.wait()

    return pl.pallas_call(
        body, name=name, in_specs=[ANY] * n, out_specs=[ANY] * n,
        out_shape=[jax.ShapeDtypeStruct(a.shape, a.dtype) for a in arrs],
        scratch_shapes=[pltpu.SemaphoreType.DMA((n,)), pltpu.SemaphoreType.DMA((n,))],
    )(*arrs)


def _add_selected(stack, others, sel, *, name):
    _, m, cols = stack.shape
    q = others.shape[0]
    tr = _pick(m, 256, 16)

    def body(sel_ref, s_ref, o_ref, out_ref):
        acc = s_ref[0].astype(F32)
        for i in range(q):
            acc = acc + o_ref[i].astype(F32)
        out_ref[...] = acc

    return pl.pallas_call(
        body, name=name,
        grid_spec=pltpu.PrefetchScalarGridSpec(
            num_scalar_prefetch=1, grid=(m // tr,),
            in_specs=[pl.BlockSpec((1, tr, cols), lambda i, sel_ref: (sel_ref[0], i, 0)),
                      pl.BlockSpec((q, tr, cols), lambda i, sel_ref: (0, i, 0))],
            out_specs=pl.BlockSpec((tr, cols), lambda i, sel_ref: (i, 0))),
        out_shape=jax.ShapeDtypeStruct((m, cols), F32),
        compiler_params=_cp(("parallel",)),
    )(sel, stack, others)


BIG = ("ssd_w_in", "ssd_w_out", "fox_w_in", "fox_w_out", "ffn_w_up", "ffn_w_down")
COL_SHARDED = ("ssd_w_in", "fox_w_in", "ffn_w_up")
SMALL = (("mix_norm_g", (4, 1024)), ("ffn_norm_g", (4, 1024)), ("ssd_conv_w", (2, 4, 3072)), ("ssd_conv_b", (2, 3072)),
         ("ssd_dt_bias", (2, 32)), ("ssd_a_log", (2, 32)), ("ssd_d", (2, 32)), ("ssd_norm_g", (2, 2048)),
         ("fox_b_f", (2, 16)), ("fox_q_norm_g", (2, 64)), ("fox_k_norm_g", (2, 64)), ("ffn_conv_w", (4, 3, 2816)),
         ("ffn_conv_b", (4, 2816)), ("final_norm_g", (1024,)), ("loss", (1,)))
NAMES = ("mix_norm_g", "ffn_norm_g", "ssd_w_in", "ssd_conv_w", "ssd_conv_b", "ssd_dt_bias", "ssd_a_log", "ssd_d",
         "ssd_norm_g", "ssd_w_out", "fox_w_in", "fox_b_f", "fox_q_norm_g", "fox_k_norm_g", "fox_w_out", "ffn_w_up",
         "ffn_conv_w", "ffn_conv_b", "ffn_w_down", "final_norm_g")


def _pack(parts):
    flat = jnp.concatenate([jnp.reshape(p, (-1,)).astype(F32) for p in parts])
    rows = -(-flat.shape[0] // (8 * LANES)) * 8
    return jnp.pad(flat, (0, rows * LANES - flat.shape[0])).reshape(rows, LANES)


def _unpack(buf, shapes):
    flat = buf.reshape(-1)
    out, off = [], 0
    for shp in shapes:
        size = 1
        for d in shp:
            size *= d
        out.append(flat[off:off + size].reshape(shp))
        off += size
    return out


def _pad_lanes(a):
    return jnp.pad(a, ((0, 0), (0, LANES - a.shape[1])))


def _pad8(w):
    return jnp.pad(w, ((0, 8 - w.shape[0]), (0, 0)))


def _ssd_forward(h, p, name, rider=None):
    s = h.shape[0]
    hn = _rms_fwd(h, p["mix_g"], gw=D_MODEL, ncol=1, name=f"{name}_norm")
    zx = _matmul(hn, p["w_zx"], mode="nn", name=f"{name}_proj")
    dtp = _matmul(hn, p["w_dt"], mode="nn", name=f"{name}_proj_dt")
    xbc, _ = _conv_fwd(zx, p["conv_w8"], p["conv_b"], kw=SSD_K, width=SSD_CONV_DIM, u_col0=SSD_DI, name=f"{name}_conv")
    dt3 = dtp[:, :SSD_H].reshape(s, SSD_G, SSD_HPG)
    dtg, dtg_t = jnp.transpose(dt3, (1, 0, 2)), jnp.transpose(dt3, (1, 2, 0))
    sp = (p["bias_r"], p["bias_c"], p["alog_r"], p["alog_c"], p["d_r"])
    y, hprev, riding = _ssd_fwd(xbc, dtg, dtg_t, *sp, name=f"{name}_scan", rider=rider)
    y2 = _rms_fwd(y, p["norm_g"], gw=SSD_DI // SSD_G, ncol=SSD_G, z=zx, name=f"{name}_gnorm")
    out = _matmul(y2, p["w_out"], mode="nn", add=h, name=f"{name}_out")
    return out, dict(h=h, hn=hn, zx=zx, xbc=xbc, dtg=dtg, dtg_t=dtg_t, y=y, hprev=hprev, y2=y2), riding


def _ssd_backward(dh1, p, a, name, ride=()):
    s = dh1.shape[0]
    g = {}
    dy2 = _matmul(dh1, p["w_out"], mode="nt", name=f"{name}_out_dx")
    g["w_out"] = _matmul(a["y2"], dh1, mode="tn", out_dtype=BF16, name=f"{name}_out_dw")
    rider = _Exchange(list(ride) + [_to_slabs(g["w_out"], False)])
    dy, dz, g["norm_g"] = _rms_bwd(a["y"], p["norm_g"], dy2, gw=SSD_DI // SSD_G, ncol=SSD_G, z=a["zx"], name=f"{name}_gnorm_b")
    sp = (p["bias_r"], p["bias_c"], p["alog_r"], p["alog_c"], p["d_r"])
    dx, dbm, dcm, ddt, g["dt_bias"], g["a_log"], g["d"], riding = _ssd_bwd(
        a["xbc"], a["dtg"], a["dtg_t"], *sp, a["hprev"], dy, name=f"{name}_scan_b", rider=rider)
    dact = jnp.concatenate([dx, dbm, dcm], axis=1)
    dxbc, dwb, _ = _conv_bwd(a["zx"], p["conv_w8"], p["conv_b"], dact, kw=SSD_K, width=SSD_CONV_DIM, u_col0=SSD_DI,
                             name=f"{name}_conv_b")
    g["conv_w"], g["conv_b"] = dwb[:SSD_K], dwb[7]
    dzx = jnp.concatenate([dz.astype(BF16), dxbc], axis=1)
    ddtp = _pad_lanes(jnp.transpose(ddt, (1, 0, 2)).reshape(s, SSD_H))
    dhn = _matmul(dzx, p["w_zx"], mode="nt", name=f"{name}_proj_dx")
    dhn = _matmul(ddtp, p["w_dt"], mode="nt", add=dhn, name=f"{name}_proj_dt_dx")
    dw_zx = _matmul(a["hn"], dzx, mode="tn", out_dtype=BF16, name=f"{name}_proj_dw")
    dw_dt = _matmul(a["hn"], ddtp, mode="tn", out_dtype=BF16, name=f"{name}_proj_dt_dw")
    g["w_in"] = jnp.concatenate([dw_zx, dw_dt[:, :SSD_H]], axis=1)
    dh, g["mix_g"] = _rms_bwd(a["h"], p["mix_g"], dhn, gw=D_MODEL, ncol=1, add=dh1, name=f"{name}_norm_b")
    g["w_out_received"] = riding[-1]
    return dh, g, riding[:-1]


def _fox_forward(h, p, name, rider=None):
    s = h.shape[0]
    hn = _rms_fwd(h, p["mix_g"], gw=D_MODEL, ncol=1, name=f"{name}_norm")
    qkvg = _matmul(hn, p["w_qkvg"], mode="nn", name=f"{name}_proj")
    fp = _matmul(hn, p["w_f"], mode="nn", name=f"{name}_proj_f")
    qs = _rms_fwd(qkvg, p["gq"] * FOX_SCALE, gw=FOX_D, ncol=1, x_col0=0, sub=FOX_HD, name=f"{name}_qnorm")
    kn = _rms_fwd(qkvg, p["gk"], gw=FOX_D, ncol=1, x_col0=1, sub=FOX_HD, name=f"{name}_knorm")
    f_t = jnp.transpose(fp[:, :FOX_H])
    cum_t = _fgate_fwd(f_t, p["b_f"], name=f"{name}_fgate")
    ck = cum_t.reshape(FOX_PAIRS, 2, s)
    o, lse, riding = _flash_fwd(qs, kn, qkvg, ck, name=f"{name}_attn", rider=rider)
    og = _ogate_fwd(o, qkvg, name=f"{name}_ogate")
    out = _matmul(og, p["w_out"], mode="nn", add=h, name=f"{name}_out")
    return out, dict(h=h, hn=hn, qkvg=qkvg, qs=qs, kn=kn, f_t=f_t, ck=ck, o=o, lse=lse, og=og), riding


def _fox_backward(dh1, p, a, name, ride=()):
    s = dh1.shape[0]
    g = {}
    dog = _matmul(dh1, p["w_out"], mode="nt", name=f"{name}_out_dx")
    g["w_out"] = _matmul(a["og"], dh1, mode="tn", out_dtype=BF16, name=f"{name}_out_dw")
    rider = _Exchange(list(ride) + [_to_slabs(g["w_out"], False)])
    do, dgate, delta = _ogate_bwd(dog, a["o"], a["qkvg"], name=f"{name}_ogate_b")
    swap = lambda v: jnp.transpose(v, (0, 2, 1))
    dl_t = jnp.transpose(delta.reshape(s, FOX_PAIRS, 2), (1, 2, 0))
    dq, dk, dv, dcq, dck, riding = _flash_bwd(a["qs"], a["kn"], a["qkvg"], do, swap(a["lse"]), dl_t, swap(a["ck"]),
                                              name=f"{name}_attn_b", rider=rider)
    dq_raw, dgq = _rms_bwd(a["qkvg"], p["gq"], dq, gw=FOX_D, ncol=1, x_col0=0, sub=FOX_HD, dx_dtype=BF16, name=f"{name}_qnorm_b")
    dk_raw, dgk = _rms_bwd(a["qkvg"], p["gk"], dk, gw=FOX_D, ncol=1, x_col0=1, sub=FOX_HD, dx_dtype=BF16, name=f"{name}_knorm_b")
    g["gq"] = dgq.reshape(FOX_H, FOX_HD).sum(axis=0)
    g["gk"] = dgk.reshape(FOX_H, FOX_HD).sum(axis=0)
    df_t, dbf = _fgate_bwd(dcq.reshape(FOX_H, s), swap(dck).reshape(FOX_H, s), a["f_t"], p["b_f"], name=f"{name}_fgate_b")
    g["b_f"] = dbf[:, 0]
    dproj = jnp.concatenate([dq_raw, dk_raw, dv.astype(BF16), dgate], axis=1)
    dfp = _pad_lanes(jnp.transpose(df_t))
    dhn = _matmul(dproj, p["w_qkvg"], mode="nt", name=f"{name}_proj_dx")
    dhn = _matmul(dfp, p["w_f"], mode="nt", add=dhn, name=f"{name}_proj_f_dx")
    dw_qkvg = _matmul(a["hn"], dproj, mode="tn", out_dtype=BF16, name=f"{name}_proj_dw")
    dw_f = _matmul(a["hn"], dfp, mode="tn", out_dtype=BF16, name=f"{name}_proj_f_dw")
    g["w_in"] = jnp.concatenate([dw_qkvg, dw_f[:, :FOX_H]], axis=1)
    dh, g["mix_g"] = _rms_bwd(a["h"], p["mix_g"], dhn, gw=D_MODEL, ncol=1, add=dh1, name=f"{name}_norm_b")
    g["w_out_received"] = riding[-1]
    return dh, g, riding[:-1]


def _ffn_forward(h, p, name, rider=None):
    hn = _rms_fwd(h, p["ffn_g"], gw=D_MODEL, ncol=1, name=f"{name}_norm")
    u = _matmul(hn, p["w_up"], mode="nn", name=f"{name}_up")
    act, riding = _conv_fwd(u, p["conv_w8"], p["conv_b"], kw=FFN_K, width=D_FF, u_col0=0, mul_col0=D_FF, out_dtype=BF16,
                            name=f"{name}_glu", rider=rider)
    out = _matmul(act, p["w_down"], mode="nn", add=h, name=f"{name}_down")
    return out, dict(h=h, hn=hn, u=u, act=act), riding


def _ffn_backward(dh2, p, a, name):
    g = {}
    dact = _matmul(dh2, p["w_down"], mode="nt", name=f"{name}_down_dx")
    g["w_down"] = _matmul(a["act"], dh2, mode="tn", out_dtype=BF16, name=f"{name}_down_dw")
    du, dwb, _ = _conv_bwd(a["u"], p["conv_w8"], p["conv_b"], dact, kw=FFN_K, width=D_FF, u_col0=0,
                           mul_col0=D_FF, name=f"{name}_glu_b")
    g["conv_w"], g["conv_b"] = dwb[:FFN_K], dwb[7]
    dhn = _matmul(du, p["w_up"], mode="nt", name=f"{name}_up_dx")
    g["w_up"] = _matmul(a["hn"], du, mode="tn", out_dtype=BF16, name=f"{name}_up_dw")
    dh, g["ffn_g"] = _rms_bwd(a["h"], p["ffn_g"], dhn, gw=D_MODEL, ncol=1, add=dh2, name=f"{name}_norm_b")
    return dh, g


def _to_slabs(dw, col_sharded):
    rows, cols = dw.shape
    if col_sharded:
        return jnp.transpose(dw.reshape(rows, 4, cols // 4), (1, 0, 2)).reshape(4, 2, rows // 2, cols // 4)
    return dw.reshape(4, 2, rows // 8, cols)


def kernel(x, mix_norm_g, ffn_norm_g, ssd_w_in, ssd_conv_w, ssd_conv_b, ssd_dt_bias, ssd_a_log, ssd_d, ssd_norm_g, ssd_w_out, fox_w_in, fox_b_f, fox_q_norm_g, fox_k_norm_g, fox_w_out, ffn_w_up, ffn_conv_w, ffn_conv_b, ffn_w_down, final_norm_g, loss_target, m_mix_norm_g, m_ffn_norm_g, m_ssd_w_in, m_ssd_conv_w, m_ssd_conv_b, m_ssd_dt_bias, m_ssd_a_log, m_ssd_d, m_ssd_norm_g, m_ssd_w_out, m_fox_w_in, m_fox_b_f, m_fox_q_norm_g, m_fox_k_norm_g, m_fox_w_out, m_ffn_w_up, m_ffn_conv_w, m_ffn_conv_b, m_ffn_w_down, m_final_norm_g, v_mix_norm_g, v_ffn_norm_g, v_ssd_w_in, v_ssd_conv_w, v_ssd_conv_b, v_ssd_dt_bias, v_ssd_a_log, v_ssd_d, v_ssd_norm_g, v_ssd_w_out, v_fox_w_in, v_fox_b_f, v_fox_q_norm_g, v_fox_k_norm_g, v_fox_w_out, v_ffn_w_up, v_ffn_conv_w, v_ffn_conv_b, v_ffn_w_down, v_final_norm_g):
    w = dict(mix_norm_g=mix_norm_g, ffn_norm_g=ffn_norm_g, ssd_w_in=ssd_w_in, ssd_conv_w=ssd_conv_w, ssd_conv_b=ssd_conv_b,
             ssd_dt_bias=ssd_dt_bias, ssd_a_log=ssd_a_log, ssd_d=ssd_d, ssd_norm_g=ssd_norm_g, ssd_w_out=ssd_w_out,
             fox_w_in=fox_w_in, fox_b_f=fox_b_f, fox_q_norm_g=fox_q_norm_g, fox_k_norm_g=fox_k_norm_g, fox_w_out=fox_w_out,
             ffn_w_up=ffn_w_up, ffn_conv_w=ffn_conv_w, ffn_conv_b=ffn_conv_b, ffn_w_down=ffn_w_down, final_norm_g=final_norm_g)
    m_in = dict(zip(NAMES, (m_mix_norm_g, m_ffn_norm_g, m_ssd_w_in, m_ssd_conv_w, m_ssd_conv_b, m_ssd_dt_bias, m_ssd_a_log,
                            m_ssd_d, m_ssd_norm_g, m_ssd_w_out, m_fox_w_in, m_fox_b_f, m_fox_q_norm_g, m_fox_k_norm_g,
                            m_fox_w_out, m_ffn_w_up, m_ffn_conv_w, m_ffn_conv_b, m_ffn_w_down, m_final_norm_g)))
    v_in = dict(zip(NAMES, (v_mix_norm_g, v_ffn_norm_g, v_ssd_w_in, v_ssd_conv_w, v_ssd_conv_b, v_ssd_dt_bias, v_ssd_a_log,
                            v_ssd_d, v_ssd_norm_g, v_ssd_w_out, v_fox_w_in, v_fox_b_f, v_fox_q_norm_g, v_fox_k_norm_g,
                            v_fox_w_out, v_ffn_w_up, v_ffn_conv_w, v_ffn_conv_b, v_ffn_w_down, v_final_norm_g)))
    cx, cy, cc = _coords()
    chip = 2 * cx + cy
    h = x[0]
    target = loss_target[0]

    conv_shapes = [ssd_conv_w.shape, ffn_conv_w.shape]
    slots = _allgather_small(_pack([ssd_conv_w, ffn_conv_w]), name="gather_conv_w", with_sum=False)[0]
    per_chip = [_unpack(slots[2 * q], conv_shapes) for q in range(4)]
    ssd_conv_full = jnp.concatenate([pc[0] for pc in per_chip], axis=2)
    ffn_conv_full = jnp.concatenate([pc[1] for pc in per_chip], axis=2)
    low = {n: w[n].astype(BF16) for n in BIG}
    sub_weights = dict(ssd=("ssd_w_in", "ssd_w_out"), fox=("fox_w_in", "fox_w_out"), ffn=("ffn_w_up", "ffn_w_down"))

    def shards_of(kind, idx):
        return [low[n][idx].reshape(2, low[n].shape[1] // 2, low[n].shape[2]) for n in sub_weights[kind]]

    def assemble(kind, idx, gathered):
        full = []
        for n, own, gth in zip(sub_weights[kind], shards_of(kind, idx), gathered):
            gth = lax.dynamic_update_slice(gth, own[None], (chip, 0, 0, 0))
            _, _, half, cols = gth.shape
            if n in COL_SHARDED:
                full.append(jnp.transpose(gth.reshape(4, 2 * half, cols), (1, 0, 2)).reshape(2 * half, 4 * cols))
            else:
                full.append(gth.reshape(8 * half, cols))
        return full

    def ssd_params(j, i, weights):
        w_in, w_out = weights
        g3 = lambda v: v.reshape(SSD_G, 1, SSD_HPG)
        g3c = lambda v: v.reshape(SSD_G, SSD_HPG, 1)
        return dict(mix_g=mix_norm_g[i][None], w_zx=w_in[:, :SSD_ZX], w_dt=_pad_lanes(w_in[:, SSD_ZX:]),
                    conv_w8=_pad8(ssd_conv_full[j]), conv_b=ssd_conv_b[j][None], bias_r=g3(ssd_dt_bias[j]),
                    bias_c=g3c(ssd_dt_bias[j]), alog_r=g3(ssd_a_log[j]), alog_c=g3c(ssd_a_log[j]), d_r=g3(ssd_d[j]),
                    norm_g=ssd_norm_g[j][None], w_out=w_out)

    def fox_params(j, i, weights):
        w_in, w_out = weights
        return dict(mix_g=mix_norm_g[i][None], w_qkvg=w_in[:, :4 * FOX_D], w_f=_pad_lanes(w_in[:, 4 * FOX_D:]),
                    gq=jnp.tile(fox_q_norm_g[j], FOX_H)[None], gk=jnp.tile(fox_k_norm_g[j], FOX_H)[None],
                    b_f=fox_b_f[j][:, None], w_out=w_out)

    def ffn_params(i, weights):
        w_up, w_down = weights
        return dict(ffn_g=ffn_norm_g[i][None], w_up=w_up, conv_w8=_pad8(ffn_conv_full[i]), conv_b=ffn_conv_b[i][None],
                    w_down=w_down)

    order = [("ssd", 0), ("ffn", 0), ("fox", 0), ("ffn", 1), ("ssd", 1), ("ffn", 2), ("fox", 1), ("ffn", 3)]
    fetch = {("ssd", 0): [("ffn", 0)], ("ffn", 0): [("fox", 0)], ("fox", 0): [("ffn", 1), ("ssd", 1), ("ffn", 2)],
             ("ssd", 1): [("fox", 1)], ("fox", 1): [("ffn", 3)]}
    ready = {("ssd", 0): assemble("ssd", 0, _run_rider(_Gather(shards_of("ssd", 0)), name="gather_first"))}
    params, acts = {}, {}
    forward = dict(ssd=_ssd_forward, fox=_fox_forward, ffn=_ffn_forward)
    for kind, idx in order:
        if kind == "ssd":
            params[kind, idx] = ssd_params(idx, 2 * idx, ready.pop((kind, idx)))
        elif kind == "fox":
            params[kind, idx] = fox_params(idx, 2 * idx + 1, ready.pop((kind, idx)))
        else:
            params[kind, idx] = ffn_params(idx, ready.pop((kind, idx)))
        wanted = fetch.get((kind, idx), [])
        rider = _Gather([s for sub in wanted for s in shards_of(*sub)]) if wanted else None
        h, acts[kind, idx], riding = forward[kind](h, params[kind, idx], f"{kind}{idx}", rider=rider)
        for q, sub in enumerate(wanted):
            ready[sub] = assemble(*sub, riding[2 * q:2 * q + 2])
    loss_part, dh, d_final_g = _loss_head(h, final_norm_g[None], target, name="loss_head")

    backward = dict(ssd=_ssd_backward, fox=_fox_backward, ffn=_ffn_backward)
    grad_keys = dict(ssd=("w_in", "w_out"), fox=("w_in", "w_out"), ffn=("w_up", "w_down"))
    sub_g, slabs, received = {}, {}, {}
    waiting = []
    for sub in reversed(order):
        kind = sub[0]
        if kind == "ffn":
            dh, sub_g[sub] = _ffn_backward(dh, params[sub], acts[sub], f"{kind}{sub[1]}")
        else:
            dh, sub_g[sub], got = backward[kind](dh, params[sub], acts[sub], f"{kind}{sub[1]}",
                                                 ride=[slabs[key] for key in waiting])
            received.update(zip(waiting, got))
            waiting = []
        for q, (key, n) in enumerate(zip(grad_keys[kind], sub_weights[kind])):
            slabs[sub, q] = _to_slabs(sub_g[sub][key], n in COL_SHARDED)
            if kind != "ffn" and q == 1:
                received[sub, q] = sub_g[sub]["w_out_received"]
            else:
                waiting.append((sub, q))
    received.update(zip(waiting, _run_rider(_Exchange([slabs[key] for key in waiting]), name="rs_last_exchange")))
    grad_x = dh[None]
    ssd_g, fox_g = [sub_g["ssd", 0], sub_g["ssd", 1]], [sub_g["fox", 0], sub_g["fox", 1]]
    mix_g = [ssd_g[0], fox_g[0], ssd_g[1], fox_g[1]]
    ffn_g = [sub_g["ffn", i] for i in range(DEPTH)]

    me = jnp.reshape(2 * chip + cc, (1,)).astype(jnp.int32)
    finals = {}
    for sub in order:
        for q in range(2):
            _, _, m, cols = slabs[sub, q].shape
            finals[sub, q] = _add_selected(slabs[sub, q].reshape(8, m, cols), received[sub, q], me,
                                           name=f"rs_add_{sub[0]}{sub[1]}_{q}")
    keys = list(finals)
    others = dict(zip(keys, _sibling_swap([finals[key] for key in keys], name="rs_result_swap")))
    grads = {}
    for kind, names in sub_weights.items():
        for q, n in enumerate(names):
            subs = [sub for sub in sorted(set(order)) if sub[0] == kind]
            mine = jnp.stack([finals[sub, q] for sub in subs])
            theirs = jnp.stack([others[sub, q] for sub in subs])
            halves = jnp.stack([jnp.where(cc == 0, mine, theirs), jnp.where(cc == 0, theirs, mine)], axis=1)
            grads[n] = halves.reshape(w[n].shape)
    small = dict(
        mix_norm_g=jnp.concatenate([g["mix_g"] for g in mix_g], axis=0),
        ffn_norm_g=jnp.concatenate([g["ffn_g"] for g in ffn_g], axis=0),
        ssd_conv_w=jnp.stack([g["conv_w"] for g in ssd_g]), ssd_conv_b=jnp.stack([g["conv_b"] for g in ssd_g]),
        ssd_dt_bias=jnp.stack([g["dt_bias"].reshape(SSD_H) for g in ssd_g]),
        ssd_a_log=jnp.stack([g["a_log"].reshape(SSD_H) for g in ssd_g]),
        ssd_d=jnp.stack([g["d"].reshape(SSD_H) for g in ssd_g]),
        ssd_norm_g=jnp.concatenate([g["norm_g"] for g in ssd_g], axis=0),
        fox_b_f=jnp.stack([g["b_f"] for g in fox_g]), fox_q_norm_g=jnp.stack([g["gq"] for g in fox_g]),
        fox_k_norm_g=jnp.stack([g["gk"] for g in fox_g]),
        ffn_conv_w=jnp.stack([g["conv_w"] for g in ffn_g]), ffn_conv_b=jnp.stack([g["conv_b"] for g in ffn_g]),
        final_norm_g=d_final_g[0], loss=loss_part[0, :1])
    _, total = _allgather_small(_pack([small[n] for n, _ in SMALL]), name="reduce_small", with_sum=True)
    for (n, shp), val in zip(SMALL, _unpack(total, [shp for _, shp in SMALL])):
        grads[n] = val
    loss = grads.pop("loss")[0]
    grads["ssd_conv_w"] = lax.dynamic_slice_in_dim(grads["ssd_conv_w"], chip * ssd_conv_w.shape[2], ssd_conv_w.shape[2], axis=2)
    grads["ffn_conv_w"] = lax.dynamic_slice_in_dim(grads["ffn_conv_w"], chip * ffn_conv_w.shape[2], ffn_conv_w.shape[2], axis=2)

    deltas, new_m, new_v = {}, {}, {}
    for n in NAMES:
        shp = w[n].shape
        two_d = (1, shp[0]) if len(shp) == 1 else (-1, shp[-1])
        r2 = lambda a: a.reshape(two_d)
        d, nm, nv = _adamw(r2(w[n]), r2(grads[n]), r2(m_in[n]), r2(v_in[n]), name=f"adamw_{n}")
        deltas[n], new_m[n], new_v[n] = d.reshape(shp), nm.reshape(shp), nv.reshape(shp)
    return (loss, grad_x, *[grads[n] for n in NAMES], *[deltas[n] for n in NAMES], *[new_m[n] for n in NAMES],
            *[new_v[n] for n in NAMES])
```

```python
import functools

import jax
import jax.numpy as jnp
from jax import lax
from jax.experimental import pallas as pl
from jax.experimental.pallas import tpu as pltpu

F32 = jnp.float32
BF16 = jnp.bfloat16
HI = lax.Precision.HIGHEST
MESH = pl.DeviceIdType.MESH

D_MODEL = 1024
DEPTH = 4
EPS = 1e-6
SSD_DI = 2048
SSD_HD = 64
SSD_G = 4
SSD_HPG = 8
SSD_N = 128
SSD_K = 4
CHUNK = 128
SSD_CONV_DIM = 3072
SSD_ZX = SSD_DI + SSD_CONV_DIM
SSD_H = 32
FOX_HD = 64
FOX_H = 16
FOX_D = 1024
D_FF = 2816
FFN_K = 3
LANES = 128
VMEM_LIMIT = 56 * 1024 * 1024

ADAM_LR = 0.001
ADAM_B1 = 0.9
ADAM_B2 = 0.999
ADAM_EPS = 1e-08
ADAM_WD = 0.01
ADAM_STEP = 10

NN = (((1,), (0,)), ((), ()))
NT = (((1,), (1,)), ((), ()))
TN = (((0,), (0,)), ((), ()))


def _pick(n, cap, mult=LANES):
    best = None
    for t in range(mult, min(n, cap) + 1, mult):
        if n % t == 0:
            best = t
    return best if best is not None else n


def _cp(sem):
    return pltpu.CompilerParams(dimension_semantics=sem, vmem_limit_bytes=VMEM_LIMIT)


def _sigmoid(x):
    return jax.nn.sigmoid(x)


def _silu(x):
    return x * _sigmoid(x)


def _dsilu(x):
    s = _sigmoid(x)
    return s * (1.0 + x * (1.0 - s))


def _softplus(x):
    e = jnp.exp(-jnp.abs(x))
    u = 1.0 + e
    l1p = jnp.where(u == 1.0, e, jnp.log(u) * (e / (u - 1.0)))
    return jnp.maximum(x, 0.0) + l1p


def _dotf(a, b, dn=NN, *, onehot="b", pieces=2):
    x, e = (a, b) if onehot == "b" else (b, a)
    e = e.astype(BF16)
    acc = None
    for n in range(pieces):
        hi = x.astype(BF16)
        part = lax.dot_general(hi, e, dn, preferred_element_type=F32) if onehot == "b" else \
            lax.dot_general(e, hi, dn, preferred_element_type=F32)
        acc = part if acc is None else acc + part
        if n + 1 < pieces:
            x = x - hi.astype(F32)
    return acc


def _dotb(a, b, dn=NN):
    return lax.dot_general(a.astype(BF16), b.astype(BF16), dn, preferred_element_type=F32)


def _group_matrix(width, sub, transpose=False):
    ng = width // sub
    shape = (ng, width) if transpose else (width, ng)
    lane = lax.broadcasted_iota(jnp.int32, shape, 1 if transpose else 0)
    grp = lax.broadcasted_iota(jnp.int32, shape, 0 if transpose else 1)
    return (lane // sub == grp).astype(F32)


def _gmean(v, sub):
    width = v.shape[-1]
    if sub == width:
        return jnp.mean(v, axis=-1, keepdims=True)
    s = _dotf(v, _group_matrix(width, sub))
    return _dotf(s, _group_matrix(width, sub, transpose=True)) * (1.0 / sub)


def _matmul(a, b, *, mode, name, out_dtype=F32, add=None):
    a_planes = a.shape[0] if (mode == "nt" and a.ndim == 3) else 0
    b_planes = b.shape[0] if (mode == "tn" and b.ndim == 3) else 0
    a2 = (a.shape[1], a.shape[0] * a.shape[2]) if a_planes else a.shape
    b2 = (b.shape[1], b.shape[0] * b.shape[2]) if b_planes else b.shape
    if mode == "nn":
        (m, k), (k2, n) = a2, b2
    elif mode == "nt":
        (m, k), (n, k2) = a2, b2
    else:
        (k, m), (k2, n) = a2, b2
    assert k == k2, (a.shape, b.shape, mode)
    tm, tn = _pick(m, 1536), _pick(n // b_planes if b_planes else n, 1536)
    tk = _pick(k // a_planes if a_planes else k, 1536)
    nk = k // tk
    dn = {"nn": NN, "nt": NT, "tn": TN}[mode]
    has_add = add is not None

    def body(*refs):
        if has_add:
            a_ref, b_ref, add_ref, o_ref, acc_ref = refs
        else:
            a_ref, b_ref, o_ref, acc_ref = refs
            add_ref = None
        kk = pl.program_id(2)
        part = _dotb(a_ref[0] if a_planes else a_ref[...], b_ref[0] if b_planes else b_ref[...], dn)

        def finish(r):
            if has_add:
                r = r + add_ref[...]
            o_ref[...] = r.astype(out_dtype)

        if nk == 1:
            finish(part)
        else:
            @pl.when(kk == 0)
            def _():
                acc_ref[...] = part

            @pl.when(kk > 0)
            def _():
                acc_ref[...] += part

            @pl.when(kk == nk - 1)
            def _():
                finish(acc_ref[...])

    if mode == "nn":
        a_spec = pl.BlockSpec((tm, tk), lambda i, j, q: (i, q))
        b_spec = pl.BlockSpec((tk, tn), lambda i, j, q: (q, j))
    elif mode == "nt":
        per = (k // a_planes) // tk if a_planes else 0
        a_spec = (pl.BlockSpec((1, tm, tk), lambda i, j, q: (q // per, i, q % per)) if a_planes
                  else pl.BlockSpec((tm, tk), lambda i, j, q: (i, q)))
        b_spec = pl.BlockSpec((tn, tk), lambda i, j, q: (j, q))
    else:
        per = (n // b_planes) // tn if b_planes else 0
        a_spec = pl.BlockSpec((tk, tm), lambda i, j, q: (q, i))
        b_spec = (pl.BlockSpec((1, tk, tn), lambda i, j, q: (j // per, q, j % per)) if b_planes
                  else pl.BlockSpec((tk, tn), lambda i, j, q: (q, j)))
    o_spec = pl.BlockSpec((tm, tn), lambda i, j, q: (i, j))
    in_specs = [a_spec, b_spec] + ([o_spec] if has_add else [])
    args = (a, b) + ((add,) if has_add else ())
    return pl.pallas_call(
        body, name=name, grid=(m // tm, n // tn, nk), in_specs=in_specs, out_specs=o_spec,
        out_shape=jax.ShapeDtypeStruct((m, n), out_dtype),
        scratch_shapes=[pltpu.VMEM((tm, tn) if nk > 1 else (8, LANES), F32)],
        compiler_params=_cp(("parallel", "parallel", "arbitrary")),
    )(*args)


def _rms_fwd(x, g, *, gw, ncol, name, x_col0=0, sub=None, z=None, z_col0=0, out_dtype=BF16):
    rows = x.shape[0]
    tr = _pick(rows, 512, 8)
    sub = gw if sub is None else sub
    gated = z is not None

    def body(*refs):
        if gated:
            x_ref, z_ref, g_ref, o_ref = refs
            xv = x_ref[...] * _silu(z_ref[...])
        else:
            x_ref, g_ref, o_ref = refs
            xv = x_ref[...]
        r = lax.rsqrt(_gmean(xv * xv, sub) + EPS)
        o_ref[...] = (xv * r * g_ref[...]).astype(out_dtype)

    specs = [pl.BlockSpec((tr, gw), lambda j, i: (i, x_col0 + j))]
    args = [x]
    if gated:
        specs.append(pl.BlockSpec((tr, gw), lambda j, i: (i, z_col0 + j)))
        args.append(z)
    specs.append(pl.BlockSpec((1, gw), lambda j, i: (0, j)))
    args.append(g)
    return pl.pallas_call(
        body, name=name, grid=(ncol, rows // tr), in_specs=specs,
        out_specs=pl.BlockSpec((tr, gw), lambda j, i: (i, j)),
        out_shape=jax.ShapeDtypeStruct((rows, gw * ncol), out_dtype),
        compiler_params=_cp(("parallel", "parallel")),
    )(*args)


def _rms_bwd(x, g, dy, *, gw, ncol, name, x_col0=0, sub=None, z=None, z_col0=0, add=None, dx_dtype=F32):
    rows = x.shape[0]
    tr = _pick(rows, 512, 8)
    sub = gw if sub is None else sub
    gated = z is not None
    has_add = add is not None

    def body(*refs):
        refs = list(refs)
        x_ref = refs.pop(0)
        z_ref = refs.pop(0) if gated else None
        g_ref = refs.pop(0)
        dy_ref = refs.pop(0)
        add_ref = refs.pop(0) if has_add else None
        dx_ref = refs.pop(0)
        dz_ref = refs.pop(0) if gated else None
        dg_ref = refs.pop(0)
        i = pl.program_id(1)
        xv = x_ref[...]
        if gated:
            zz = z_ref[...]
            yz = xv * _silu(zz)
        else:
            yz = xv
        r = lax.rsqrt(_gmean(yz * yz, sub) + EPS)
        xh = yz * r
        dy = dy_ref[...].astype(F32)
        dyg = dy * g_ref[...]
        d_yz = r * (dyg - xh * _gmean(dyg * xh, sub))
        if gated:
            dx_ref[...] = (d_yz * _silu(zz)).astype(dx_dtype)
            dz_ref[...] = (d_yz * xv * _dsilu(zz)).astype(dx_dtype)
        elif has_add:
            dx_ref[...] = (d_yz + add_ref[...]).astype(dx_dtype)
        else:
            dx_ref[...] = d_yz.astype(dx_dtype)
        part = jnp.sum(dy * xh, axis=0, keepdims=True)

        @pl.when(i == 0)
        def _():
            dg_ref[...] = part

        @pl.when(i > 0)
        def _():
            dg_ref[...] += part

    tile = pl.BlockSpec((tr, gw), lambda j, i: (i, j))
    specs = [pl.BlockSpec((tr, gw), lambda j, i: (i, x_col0 + j))]
    args = [x]
    if gated:
        specs.append(pl.BlockSpec((tr, gw), lambda j, i: (i, z_col0 + j)))
        args.append(z)
    specs += [pl.BlockSpec((1, gw), lambda j, i: (0, j)), tile]
    args += [g, dy]
    if has_add:
        specs.append(tile)
        args.append(add)
    width = gw * ncol
    out_shape = [jax.ShapeDtypeStruct((rows, width), dx_dtype)]
    out_specs = [tile]
    if gated:
        out_shape.append(jax.ShapeDtypeStruct((rows, width), dx_dtype))
        out_specs.append(tile)
    out_shape.append(jax.ShapeDtypeStruct((1, width), F32))
    out_specs.append(pl.BlockSpec((1, gw), lambda j, i: (0, j)))
    return pl.pallas_call(
        body, name=name, grid=(ncol, rows // tr), in_specs=specs, out_specs=out_specs, out_shape=out_shape,
        compiler_params=_cp(("parallel", "arbitrary")),
    )(*args)


HALO = 8


def _conv_rows(tc):
    return 16 * 8 * LANES // tc


def _conv_fwd(u, w8, b, *, kw, width, name, u_col0=0, mul_col0=None, out_dtype=F32, rider=None):
    rows = u.shape[0]
    ts = _pick(rows, 512, 8)
    tc = _pick(width, 512)
    gated = mul_col0 is not None
    c0 = u_col0 // tc
    m0 = (mul_col0 // tc) if gated else 0
    assert u_col0 % tc == 0 and (not gated or mul_col0 % tc == 0)

    def body(*refs):
        if gated:
            cur_ref, halo_ref, mul_ref, w_ref, b_ref, o_ref, ext = refs
        else:
            cur_ref, halo_ref, w_ref, b_ref, o_ref, ext = refs
        i = pl.program_id(0)
        ext[pl.ds(0, HALO), :] = jnp.where(i == 0, 0.0, halo_ref[...])
        ext[pl.ds(HALO, ts), :] = cur_ref[...]
        bias = b_ref[...]
        taps = [w_ref[k:k + 1, :] for k in range(kw)]
        rb = _conv_rows(tc)
        for r0 in range(0, ts, rb):
            pre = bias + taps[0] * ext[pl.ds(r0 + HALO - (kw - 1), rb), :]
            for k in range(1, kw):
                pre = pre + taps[k] * ext[pl.ds(r0 + HALO - (kw - 1) + k, rb), :]
            act = _silu(pre)
            if gated:
                act = act * mul_ref[pl.ds(r0, rb), :]
            o_ref[pl.ds(r0, rb), :] = act.astype(out_dtype)

    hb = ts // HALO
    specs = [pl.BlockSpec((ts, tc), lambda i, j: (i, c0 + j)),
             pl.BlockSpec((HALO, tc), lambda i, j: (jnp.maximum(i * hb - 1, 0), c0 + j))]
    args = [u, u]
    if gated:
        specs.append(pl.BlockSpec((ts, tc), lambda i, j: (i, m0 + j)))
        args.append(u)
    specs += [pl.BlockSpec((8, tc), lambda i, j: (0, j)), pl.BlockSpec((1, tc), lambda i, j: (0, j))]
    args += [w8, b]
    outs, riding = _call(
        body, name=name, grid=(rows // ts, width // tc), in_specs=specs,
        out_specs=[pl.BlockSpec((ts, tc), lambda i, j: (i, j))],
        out_shape=[jax.ShapeDtypeStruct((rows, width), out_dtype)],
        scratch_shapes=[pltpu.VMEM((ts + HALO, tc), F32)], sem=("parallel", "parallel"), rider=rider, args=args)
    return outs + [riding]


def _conv_bwd(u, w8, b, dact, *, kw, width, name, u_col0=0, mul_col0=None, du_dtype=BF16, rider=None):
    rows = u.shape[0]
    ts = _pick(rows, 512, 8)
    tc = _pick(width, 512)
    gated = mul_col0 is not None
    c0 = u_col0 // tc
    m0 = (mul_col0 // tc) if gated else 0
    nt = rows // ts
    hb = ts // HALO

    def body(*refs):
        refs = list(refs)
        cur_ref, halo_ref = refs.pop(0), refs.pop(0)
        mul_ref = refs.pop(0) if gated else None
        w_ref, b_ref, da_ref = refs.pop(0), refs.pop(0), refs.pop(0)
        du_ref = refs.pop(0)
        dwb_ref, ext_u, ext_d = refs
        t = pl.program_id(1)
        ti = nt - 1 - t
        ext_u[pl.ds(0, HALO), :] = jnp.where(ti == 0, 0.0, halo_ref[...])
        ext_u[pl.ds(HALO, ts), :] = cur_ref[...]

        @pl.when(t == 0)
        def _():
            ext_d[pl.ds(ts, HALO), :] = jnp.zeros((HALO, tc), F32)
            dwb_ref[...] = jnp.zeros((8, tc), F32)

        bias = b_ref[...]
        taps = [w_ref[k:k + 1, :] for k in range(kw)]
        rb = _conv_rows(tc)
        dw_acc = [jnp.zeros((1, tc), F32) for _ in range(kw)]
        db_acc = jnp.zeros((1, tc), F32)
        for r0 in reversed(range(0, ts, rb)):
            shifted = [ext_u[pl.ds(r0 + HALO - (kw - 1) + k, rb), :] for k in range(kw)]
            pre = bias + taps[0] * shifted[0]
            for k in range(1, kw):
                pre = pre + taps[k] * shifted[k]
            sg = _sigmoid(pre)
            dsilu = sg * (1.0 + pre * (1.0 - sg))
            da = da_ref[pl.ds(r0, rb), :].astype(F32)
            if gated:
                du_ref[1, pl.ds(r0, rb), :] = (da * (pre * sg)).astype(du_dtype)
                dgp = da * mul_ref[pl.ds(r0, rb), :] * dsilu
            else:
                dgp = da * dsilu
            ext_d[pl.ds(r0, rb), :] = dgp
            du = taps[kw - 1] * dgp
            for k in range(kw - 1):
                du = du + taps[k] * ext_d[pl.ds(r0 + kw - 1 - k, rb), :]
            if gated:
                du_ref[0, pl.ds(r0, rb), :] = du.astype(du_dtype)
            else:
                du_ref[pl.ds(r0, rb), :] = du.astype(du_dtype)
            for k in range(kw):
                dw_acc[k] = dw_acc[k] + jnp.sum(dgp * shifted[k], axis=0, keepdims=True)
            db_acc = db_acc + jnp.sum(dgp, axis=0, keepdims=True)
        for k in range(kw):
            dwb_ref[k:k + 1, :] += dw_acc[k]
        dwb_ref[7:8, :] += db_acc
        ext_d[pl.ds(ts, HALO), :] = ext_d[pl.ds(0, HALO), :]

    specs = [pl.BlockSpec((ts, tc), lambda j, t: (nt - 1 - t, c0 + j)),
             pl.BlockSpec((HALO, tc), lambda j, t: (jnp.maximum((nt - 1 - t) * hb - 1, 0), c0 + j))]
    args = [u, u]
    if gated:
        specs.append(pl.BlockSpec((ts, tc), lambda j, t: (nt - 1 - t, m0 + j)))
        args.append(u)
    tile = pl.BlockSpec((ts, tc), lambda j, t: (nt - 1 - t, j))
    specs += [pl.BlockSpec((8, tc), lambda j, t: (0, j)), pl.BlockSpec((1, tc), lambda j, t: (0, j)), tile]
    args += [w8, b, dact]
    if gated:
        out_shape = [jax.ShapeDtypeStruct((2, rows, width), du_dtype)]
        out_specs = [pl.BlockSpec((2, ts, tc), lambda j, t: (0, nt - 1 - t, j))]
    else:
        out_shape = [jax.ShapeDtypeStruct((rows, width), du_dtype)]
        out_specs = [tile]
    out_shape.append(jax.ShapeDtypeStruct((8, width), F32))
    out_specs.append(pl.BlockSpec((8, tc), lambda j, t: (0, j)))
    outs, riding = _call(
        body, name=name, grid=(width // tc, nt), in_specs=specs, out_specs=out_specs, out_shape=out_shape,
        scratch_shapes=[pltpu.VMEM((ts + HALO, tc), F32), pltpu.VMEM((ts + HALO, tc), F32)],
        sem=("parallel", "arbitrary"), rider=rider, args=args)
    return outs + [riding]


GW = SSD_HPG * SSD_HD


def _ssd_common(x, bm, cm, dt_raw, dt_raw_t, bias_r, bias_c, alog_r, alog_c):
    row = lax.broadcasted_iota(jnp.int32, (CHUNK, CHUNK), 0)
    col = lax.broadcasted_iota(jnp.int32, (CHUNK, CHUNK), 1)
    causal = row >= col
    tril = causal.astype(F32)
    triu = (row <= col).astype(F32)
    spread = _group_matrix(GW, SSD_HD, transpose=True)
    dt = _softplus(dt_raw + bias_r)
    dt_t = _softplus(dt_raw_t + bias_c)
    a_r = -jnp.exp(alog_r)
    a_c = -jnp.exp(alog_c)
    acs = _dotf(tril, dt * a_r, onehot="a", pieces=3)
    acs_t = _dotf(dt_t * a_c, triu, pieces=3)
    last = acs[CHUNK - 1:CHUNK, :]
    ds = jnp.exp(last - acs)
    cd = jnp.exp(last)
    c = dict(causal=causal, tril=tril, triu=triu, spread=spread, dt=dt, a_r=a_r, acs=acs, acs_t=acs_t, ds=ds, cd=cd)
    c["eb"] = _dotf(jnp.exp(acs), spread)
    c["dsb"] = _dotf(ds, spread)
    c["cdb"] = _dotf(cd, spread)
    c["dtb"] = _dotf(dt, spread)
    c["xdt"] = x * c["dtb"]
    c["cb"] = _dotb(cm, bm, NT)
    return c


def _ssd_lam(c, r):
    diff = c["acs"][:, r:r + 1] - c["acs_t"][r:r + 1, :]
    return jnp.exp(jnp.where(c["causal"], diff, -jnp.inf))


GP = 2


def _ssd_specs(nc, rev):
    def ci(t):
        return (nc - 1 - t) if rev else t
    xs = pl.BlockSpec((CHUNK, GP * GW), lambda g, t: (ci(t), g))
    bs = pl.BlockSpec((CHUNK, GP * SSD_N), lambda g, t: (ci(t), SSD_DI // (GP * SSD_N) + g))
    cs = pl.BlockSpec((CHUNK, GP * SSD_N), lambda g, t: (ci(t), (SSD_DI // SSD_N + SSD_G) // GP + g))
    dts = pl.BlockSpec((GP, CHUNK, 8), lambda g, t: (g, ci(t), 0))
    dtts = pl.BlockSpec((GP, 8, CHUNK), lambda g, t: (g, 0, ci(t)))
    pr = pl.BlockSpec((GP, 1, 8), lambda g, t: (g, 0, 0))
    pc = pl.BlockSpec((GP, 8, 1), lambda g, t: (g, 0, 0))
    hs = pl.BlockSpec((1, GP, SSD_N, GW), lambda g, t: (ci(t), g, 0, 0))
    return xs, bs, cs, dts, dtts, pr, pc, hs


def _ssd_fwd(xbc, dtg, dtg_t, bias_r, bias_c, alog_r, alog_c, d_r, *, name, rider=None):
    s = xbc.shape[0]
    nc = s // CHUNK
    xs, bs, cs, dts, dtts, pr, pc, hs = _ssd_specs(nc, False)

    def body(x_ref, b_ref, c_ref, dt_ref, dtt_ref, br_ref, bc_ref, ar_ref, ac_ref, d_ref, y_ref, hp_ref, h_sc):
        t = pl.program_id(1)

        @pl.when(t == 0)
        def _():
            h_sc[...] = jnp.zeros_like(h_sc)

        for gg in range(GP):
            wide, narrow = slice(gg * GW, (gg + 1) * GW), slice(gg * SSD_N, (gg + 1) * SSD_N)
            x, bm, cm = x_ref[:, wide], b_ref[:, narrow], c_ref[:, narrow]
            c = _ssd_common(x, bm, cm, dt_ref[gg], dtt_ref[gg], br_ref[gg], bc_ref[gg], ar_ref[gg], ac_ref[gg])
            h = h_sc[gg]
            hp_ref[0, gg] = h
            xdt = c["xdt"]
            pieces = []
            for r in range(SSD_HPG):
                m = c["cb"] * _ssd_lam(c, r)
                pieces.append(_dotb(m, xdt[:, r * SSD_HD:(r + 1) * SSD_HD]))
            y = jnp.concatenate(pieces, axis=1) + c["eb"] * _dotb(cm, h) + x * _dotf(d_ref[gg], c["spread"])
            y_ref[:, wide] = y
            h_sc[gg] = h * c["cdb"] + _dotb(bm, xdt * c["dsb"], TN)

    outs, riding = _call(
        body, name=name, grid=(SSD_G // GP, nc),
        in_specs=[xs, bs, cs, dts, dtts, pr, pc, pr, pc, pr],
        out_specs=[xs, hs],
        out_shape=[jax.ShapeDtypeStruct((s, SSD_DI), F32), jax.ShapeDtypeStruct((nc, SSD_G, SSD_N, GW), F32)],
        scratch_shapes=[pltpu.VMEM((GP, SSD_N, GW), F32)], sem=("parallel", "arbitrary"), rider=rider,
        args=(xbc, xbc, xbc, dtg, dtg_t, bias_r, bias_c, alog_r, alog_c, d_r))
    return outs + [riding]


def _ssd_bwd(xbc, dtg, dtg_t, bias_r, bias_c, alog_r, alog_c, d_r, hprev, dy, *, name, rider=None):
    s = xbc.shape[0]
    nc = s // CHUNK
    xs, bs, cs, dts, dtts, pr, pc, hs = _ssd_specs(nc, True)
    gsum = functools.partial(_group_matrix, GW, SSD_HD)

    def body(x_ref, b_ref, c_ref, dt_ref, dtt_ref, br_ref, bc_ref, ar_ref, ac_ref, d_ref, hp_ref, dy_ref,
             dx_ref, db_ref, dc_ref, ddt_ref, dbias_ref, dalog_ref, dd_ref, dh_sc):
        t = pl.program_id(1)

        @pl.when(t == 0)
        def _():
            dh_sc[...] = jnp.zeros_like(dh_sc)
            dbias_ref[...] = jnp.zeros_like(dbias_ref)
            dalog_ref[...] = jnp.zeros_like(dalog_ref)
            dd_ref[...] = jnp.zeros_like(dd_ref)

        for gg in range(GP):
            wide, narrow = slice(gg * GW, (gg + 1) * GW), slice(gg * SSD_N, (gg + 1) * SSD_N)
            x, bm, cm = x_ref[:, wide], b_ref[:, narrow], c_ref[:, narrow]
            c = _ssd_common(x, bm, cm, dt_ref[gg], dtt_ref[gg], br_ref[gg], bc_ref[gg], ar_ref[gg], ac_ref[gg])
            lanesum = gsum()
            h = hp_ref[0, gg]
            dh = dh_sc[gg]
            dy = dy_ref[:, wide]
            xdt, dsb = c["xdt"], c["dsb"]
            skip = _dotf(d_ref[gg], c["spread"])
            dd_ref[gg] += jnp.sum(_dotf(dy * x, lanesum), axis=0, keepdims=True)
            dacs = _dotf(dy * (c["eb"] * _dotb(cm, h)), lanesum)
            edy = c["eb"] * dy
            dcm = _dotb(edy, h, NT)
            dh_prev = _dotb(cm, edy, TN)
            bdh = _dotb(bm, dh)
            dxdt = dsb * bdh
            dbm = _dotb(dsb * xdt, dh, NT)
            t1 = _dotf(xdt * bdh, lanesum) * c["ds"]
            dacs = dacs - t1
            dlast = (jnp.sum(t1, axis=0, keepdims=True)
                     + jnp.sum(_dotf(dh * h, lanesum), axis=0, keepdims=True) * c["cd"])
            dcb = jnp.zeros((CHUNK, CHUNK), F32)
            pieces = []
            ones8 = jnp.ones((CHUNK, 8), F32)
            head = lax.broadcasted_iota(jnp.int32, (1, 8), 1)
            for r in range(SSD_HPG):
                sl = slice(r * SSD_HD, (r + 1) * SSD_HD)
                lam = _ssd_lam(c, r)
                m = c["cb"] * lam
                dm = _dotb(dy[:, sl], xdt[:, sl], NT)
                dcb = dcb + dm * lam
                gm = dm * m
                dacs = dacs + ((jnp.sum(gm, axis=1, keepdims=True) - _dotf(gm, ones8, TN, pieces=3))
                               * (head == r).astype(F32))
                pieces.append(_dotb(m, dy[:, sl], TN))
            dxdt = dxdt + jnp.concatenate(pieces, axis=1)
            dcm = dcm + _dotb(dcb, bm)
            dbm = dbm + _dotb(dcb, cm, TN)
            dx_ref[:, wide] = dy * skip + dxdt * c["dtb"]
            db_ref[:, narrow] = dbm
            dc_ref[:, narrow] = dcm
            rowid = lax.broadcasted_iota(jnp.int32, (CHUNK, 8), 0)
            dacs = dacs + jnp.where(rowid == CHUNK - 1, dlast, 0.0)
            dda = _dotf(c["triu"], dacs, onehot="a", pieces=3)
            ddt = _dotf(dxdt * x, lanesum) + dda * c["a_r"]
            ddt_raw = ddt * _sigmoid(dt_ref[gg] + br_ref[gg])
            ddt_ref[gg] = ddt_raw
            dbias_ref[gg] += jnp.sum(ddt_raw, axis=0, keepdims=True)
            dalog_ref[gg] += jnp.sum(dda * c["dt"], axis=0, keepdims=True) * c["a_r"]
            dh_sc[gg] = dh_prev + dh * c["cdb"]

    ci = lambda t: nc - 1 - t
    nspec = pl.BlockSpec((CHUNK, GP * SSD_N), lambda g, t: (ci(t), g))
    outs, riding = _call(
        body, name=name, grid=(SSD_G // GP, nc),
        in_specs=[xs, bs, cs, dts, dtts, pr, pc, pr, pc, pr, hs, xs],
        out_specs=[xs, nspec, nspec, dts, pr, pr, pr],
        out_shape=[jax.ShapeDtypeStruct((s, SSD_DI), F32), jax.ShapeDtypeStruct((s, SSD_G * SSD_N), F32),
                   jax.ShapeDtypeStruct((s, SSD_G * SSD_N), F32), jax.ShapeDtypeStruct((SSD_G, s, 8), F32),
                   jax.ShapeDtypeStruct((SSD_G, 1, 8), F32), jax.ShapeDtypeStruct((SSD_G, 1, 8), F32),
                   jax.ShapeDtypeStruct((SSD_G, 1, 8), F32)],
        scratch_shapes=[pltpu.VMEM((GP, SSD_N, GW), F32)], sem=("parallel", "arbitrary"), rider=rider,
        args=(xbc, xbc, xbc, dtg, dtg_t, bias_r, bias_c, alog_r, alog_c, d_r, hprev, dy))
    return outs + [riding]


FOX_PAIRS = FOX_H // 2
FOX_SCALE = FOX_HD ** -0.5
NEG_INF = -jnp.inf


def _fgate_fwd(f_t, b_c, *, name):
    hh, s = f_t.shape
    tb = _pick(s, 512)
    nb = s // tb

    def body(f_ref, b_ref, o_ref, carry):
        t = pl.program_id(0)

        @pl.when(t == 0)
        def _():
            carry[...] = jnp.zeros_like(carry)

        lf = -_softplus(-(f_ref[...] + b_ref[...]))
        row = lax.broadcasted_iota(jnp.int32, (tb, tb), 0)
        col = lax.broadcasted_iota(jnp.int32, (tb, tb), 1)
        cum = _dotf(lf, (row <= col).astype(F32), pieces=3) + carry[:, 0:1]
        o_ref[...] = cum
        carry[:, 0:1] = cum[:, tb - 1:tb]

    return pl.pallas_call(
        body, name=name, grid=(nb,),
        in_specs=[pl.BlockSpec((hh, tb), lambda t: (0, t)), pl.BlockSpec((hh, 1), lambda t: (0, 0))],
        out_specs=pl.BlockSpec((hh, tb), lambda t: (0, t)),
        out_shape=jax.ShapeDtypeStruct((hh, s), F32),
        scratch_shapes=[pltpu.VMEM((hh, LANES), F32)],
        compiler_params=_cp(("arbitrary",)),
    )(f_t, b_c)


def _fgate_bwd(dcum_q_t, dcum_k_t, f_t, b_c, *, name):
    hh, s = f_t.shape
    tb = _pick(s, 512)
    nb = s // tb

    def body(dq_ref, d_ref, f_ref, b_ref, df_ref, db_ref, carry):
        t = pl.program_id(0)

        @pl.when(t == 0)
        def _():
            carry[...] = jnp.zeros_like(carry)
            db_ref[...] = jnp.zeros_like(db_ref)

        d = d_ref[...] + dq_ref[...]
        row = lax.broadcasted_iota(jnp.int32, (tb, tb), 0)
        col = lax.broadcasted_iota(jnp.int32, (tb, tb), 1)
        rev = _dotf(d, (row >= col).astype(F32), pieces=3) + carry[:, 0:1]
        df = rev * _sigmoid(-(f_ref[...] + b_ref[...]))
        df_ref[...] = df
        db_ref[...] += jnp.sum(df, axis=1, keepdims=True)
        carry[:, 0:1] = rev[:, 0:1]

    blk = pl.BlockSpec((hh, tb), lambda t: (0, nb - 1 - t))
    return pl.pallas_call(
        body, name=name, grid=(nb,),
        in_specs=[blk, blk, blk, pl.BlockSpec((hh, 1), lambda t: (0, 0))],
        out_specs=[blk, pl.BlockSpec((hh, 1), lambda t: (0, 0))],
        out_shape=[jax.ShapeDtypeStruct((hh, s), F32), jax.ShapeDtypeStruct((hh, 1), F32)],
        scratch_shapes=[pltpu.VMEM((hh, LANES), F32)],
        compiler_params=_cp(("arbitrary",)),
    )(dcum_q_t, dcum_k_t, f_t, b_c)


def _fox_tile(s):
    return min(512, max(s // 2, 8))


def _tri_tables(nq, kv_major):
    if kv_major:
        pairs = [(i, j) for j in range(nq) for i in range(j, nq)]
    else:
        pairs = [(i, j) for i in range(nq) for j in range(i + 1)]
    return (jnp.asarray([p[0] for p in pairs], jnp.int32), jnp.asarray([p[1] for p in pairs], jnp.int32))


def _lane_tile(col, width):
    return col if width == LANES else jnp.tile(col, (1, width // LANES))


def _flash_fwd(qs, kn, qkvg, ck, *, name, rider=None):
    s = qs.shape[0]
    tt = _fox_tile(s)
    nq = s // tt
    itab, jtab = _tri_tables(nq, kv_major=False)
    v0 = 2 * FOX_D // LANES

    def body(itab_ref, jtab_ref, q_ref, k_ref, v_ref, ck_ref, o_ref, lse_ref, m_sc, l_sc, acc_sc):
        t = pl.program_id(1)
        i, j = itab_ref[t], jtab_ref[t]

        @pl.when(j == 0)
        def _():
            m_sc[...] = jnp.full_like(m_sc, NEG_INF)
            l_sc[...] = jnp.zeros_like(l_sc)
            acc_sc[...] = jnp.zeros_like(acc_sc)

        low = lax.broadcasted_iota(jnp.int32, (tt, LANES), 1) < FOX_HD

        def step(diagonal):
            q2, k2 = q_ref[...], k_ref[...]
            v2 = v_ref[...].astype(BF16)
            alphas, outs = [], []
            for hh in range(2):
                qh = jnp.where(low if hh == 0 else jnp.logical_not(low), q2, jnp.zeros_like(q2))
                sc = lax.dot_general(qh, k2, NT, preferred_element_type=F32) - ck_ref[0][hh:hh + 1, :]
                if diagonal:
                    row = lax.broadcasted_iota(jnp.int32, sc.shape, 0)
                    col = lax.broadcasted_iota(jnp.int32, sc.shape, 1)
                    sc = jnp.where(row >= col, sc, NEG_INF)
                m_prev = m_sc[hh]
                m_new = jnp.maximum(m_prev, jnp.max(sc, axis=1, keepdims=True))
                alpha = jnp.exp(m_prev - m_new)
                p = jnp.exp(sc - _lane_tile(m_new, tt))
                l_sc[hh] = alpha * l_sc[hh] + jnp.sum(p, axis=1, keepdims=True)
                m_sc[hh] = m_new
                alphas.append(alpha)
                outs.append(lax.dot_general(p.astype(BF16), v2, NN, preferred_element_type=F32))
            acc_sc[...] = jnp.where(low, alphas[0], alphas[1]) * acc_sc[...] + jnp.where(low, outs[0], outs[1])

        @pl.when(j < i)
        def _():
            step(False)

        @pl.when(j == i)
        def _():
            step(True)
            o_ref[...] = acc_sc[...] / jnp.where(low, l_sc[0], l_sc[1])
            lse_ref[0] = jnp.concatenate([m_sc[hh][:, 0:1] + jnp.log(l_sc[hh][:, 0:1]) for hh in range(2)], axis=1)

    outs, riding = _call(
        body, name=name, grid=(FOX_PAIRS, int(itab.shape[0])), prefetch=(itab, jtab),
        in_specs=[pl.BlockSpec((tt, LANES), lambda p, t, it, jt: (it[t], p)),
                  pl.BlockSpec((tt, LANES), lambda p, t, it, jt: (jt[t], p)),
                  pl.BlockSpec((tt, LANES), lambda p, t, it, jt: (jt[t], v0 + p)),
                  pl.BlockSpec((1, 2, tt), lambda p, t, it, jt: (p, 0, jt[t]))],
        out_specs=[pl.BlockSpec((tt, LANES), lambda p, t, it, jt: (it[t], p)),
                   pl.BlockSpec((1, tt, 2), lambda p, t, it, jt: (p, it[t], 0))],
        scratch_shapes=[pltpu.VMEM((2, tt, LANES), F32), pltpu.VMEM((2, tt, LANES), F32), pltpu.VMEM((tt, LANES), F32)],
        out_shape=[jax.ShapeDtypeStruct((s, FOX_D), F32), jax.ShapeDtypeStruct((FOX_PAIRS, s, 2), F32)],
        sem=("parallel", "arbitrary"), rider=rider, args=(qs, kn, qkvg, ck))
    return outs + [riding]


def _flash_bwd(qs, kn, qkvg, do, lse_t, delta_t, ck_c, *, name, rider=None):
    s = qs.shape[0]
    tt = _fox_tile(s)
    nq = s // tt
    nl = tt // LANES
    itab, jtab = _tri_tables(nq, kv_major=True)
    nsteps = itab.shape[0]
    v0 = 2 * FOX_D // LANES

    def body(itab_ref, jtab_ref, q_ref, k_ref, v_ref, do_ref, lse_ref, dl_ref, ck_ref,
             dq_ref, dk_ref, dv_ref, dcq_ref, dck_ref, dqt_sc, rs_sc, dk_sc, dv_sc, dc_sc, kt_sc, ckb_sc):
        t = pl.program_id(1)
        i, j = itab_ref[t], jtab_ref[t]

        @pl.when(t == 0)
        def _():
            dqt_sc[...] = jnp.zeros_like(dqt_sc)
            rs_sc[...] = jnp.zeros_like(rs_sc)

        @pl.when(i == j)
        def _():
            dk_sc[...] = jnp.zeros_like(dk_sc)
            dv_sc[...] = jnp.zeros_like(dv_sc)
            dc_sc[...] = jnp.zeros_like(dc_sc)
            kt_sc[...] = k_ref[...].astype(F32).T.astype(BF16)
            for hh in range(2):
                ckb_sc[hh] = jnp.broadcast_to(ck_ref[0][:, hh:hh + 1], (tt, LANES))

        low = lax.broadcasted_iota(jnp.int32, (tt, LANES), 1) < FOX_HD
        top = lax.broadcasted_iota(jnp.int32, (LANES, tt), 0) < FOX_HD

        def step(diagonal):
            q2, k2, kt = q_ref[...], k_ref[...], kt_sc[...]
            v2 = v_ref[...].astype(BF16)
            do2 = do_ref[...].astype(BF16)
            dqs, dks, dvs = [], [], []
            for hh in range(2):
                sel = low if hh == 0 else jnp.logical_not(low)
                qh = jnp.where(sel, q2, jnp.zeros_like(q2))
                doh = jnp.where(sel, do2, jnp.zeros_like(do2))
                st = lax.dot_general(k2, qh, NT, preferred_element_type=F32)
                st = st - _lane_tile(ckb_sc[hh], tt) - lse_ref[0][hh:hh + 1, :]
                if diagonal:
                    key = lax.broadcasted_iota(jnp.int32, st.shape, 0)
                    qry = lax.broadcasted_iota(jnp.int32, st.shape, 1)
                    st = jnp.where(qry >= key, st, NEG_INF)
                pt = jnp.exp(st)
                dpt = lax.dot_general(v2, doh, NT, preferred_element_type=F32)
                dst = pt * (dpt - dl_ref[0][hh:hh + 1, :])
                ptb, dstb = pt.astype(BF16), dst.astype(BF16)
                dvs.append(lax.dot_general(ptb, do2, NN, preferred_element_type=F32))
                dks.append(lax.dot_general(dstb, q2, NN, preferred_element_type=F32))
                dqs.append(lax.dot_general(kt, dstb, NN, preferred_element_type=F32))
                rs_sc[hh, i] += jnp.sum(dst, axis=0, keepdims=True)
                part = dst[:, 0:LANES]
                for b in range(1, nl):
                    part = part + dst[:, b * LANES:(b + 1) * LANES]
                dc_sc[hh] += part
            dv_sc[...] += jnp.where(low, dvs[0], dvs[1])
            dk_sc[...] += jnp.where(low, dks[0], dks[1])
            dqt_sc[i] += jnp.where(top, dqs[0], dqs[1])

        @pl.when(j < i)
        def _():
            step(False)

        @pl.when(j == i)
        def _():
            step(True)

        @pl.when(i == nq - 1)
        def _():
            dk_ref[...] = dk_sc[...]
            dv_ref[...] = dv_sc[...]
            dck_ref[0] = -jnp.concatenate([jnp.sum(dc_sc[hh], axis=1, keepdims=True) for hh in range(2)], axis=1)

        @pl.when(t == nsteps - 1)
        def _():
            for b in range(nq):
                dq_ref[pl.ds(b * tt, tt), :] = dqt_sc[b].T * FOX_SCALE
                dcq_ref[0, :, pl.ds(b * tt, tt)] = jnp.concatenate([rs_sc[hh, b] for hh in range(2)], axis=0)

    qside = pl.BlockSpec((tt, LANES), lambda p, t, it, jt: (it[t], p))
    kside = pl.BlockSpec((tt, LANES), lambda p, t, it, jt: (jt[t], p))
    qstat = pl.BlockSpec((1, 2, tt), lambda p, t, it, jt: (p, 0, it[t]))
    kstat = pl.BlockSpec((1, tt, 2), lambda p, t, it, jt: (p, jt[t], 0))
    outs, riding = _call(
        body, name=name, grid=(FOX_PAIRS, nsteps), prefetch=(itab, jtab),
        in_specs=[qside, kside, pl.BlockSpec((tt, LANES), lambda p, t, it, jt: (jt[t], v0 + p)), qside, qstat, qstat, kstat],
        out_specs=[pl.BlockSpec((s, LANES), lambda p, t, it, jt: (0, p)), kside, kside,
                   pl.BlockSpec((1, 2, s), lambda p, t, it, jt: (p, 0, 0)), kstat],
        scratch_shapes=[pltpu.VMEM((nq, LANES, tt), F32), pltpu.VMEM((2, nq, 1, tt), F32), pltpu.VMEM((tt, LANES), F32),
                        pltpu.VMEM((tt, LANES), F32), pltpu.VMEM((2, tt, LANES), F32), pltpu.VMEM((LANES, tt), BF16),
                        pltpu.VMEM((2, tt, LANES), F32)],
        out_shape=[jax.ShapeDtypeStruct((s, FOX_D), F32), jax.ShapeDtypeStruct((s, FOX_D), F32),
                   jax.ShapeDtypeStruct((s, FOX_D), F32), jax.ShapeDtypeStruct((FOX_PAIRS, 2, s), F32),
                   jax.ShapeDtypeStruct((FOX_PAIRS, s, 2), F32)],
        sem=("parallel", "arbitrary"), rider=rider, args=(qs, kn, qkvg, do, lse_t, delta_t, ck_c))
    return outs + [riding]


def _ogate_fwd(o, qkvg, *, name):
    s = o.shape[0]
    tr = _pick(s, 512, 8)

    def body(o_ref, g_ref, out_ref):
        out_ref[...] = (o_ref[...] * _sigmoid(g_ref[...])).astype(BF16)

    tile = pl.BlockSpec((tr, FOX_D), lambda i: (i, 0))
    return pl.pallas_call(
        body, name=name, grid=(s // tr,), in_specs=[tile, pl.BlockSpec((tr, FOX_D), lambda i: (i, 3))],
        out_specs=tile, out_shape=jax.ShapeDtypeStruct((s, FOX_D), BF16), compiler_params=_cp(("parallel",)),
    )(o, qkvg)


def _ogate_bwd(dog, o, qkvg, *, name):
    s = o.shape[0]
    tr = _pick(s, 512, 8)

    def body(dog_ref, o_ref, g_ref, do_ref, dg_ref, dl_ref):
        sg = _sigmoid(g_ref[...])
        ov = o_ref[...]
        dog_v = dog_ref[...]
        do = dog_v * sg
        do_ref[...] = do
        dg_ref[...] = (dog_v * ov * sg * (1.0 - sg)).astype(BF16)
        dl_ref[...] = _dotf(do * ov, _group_matrix(FOX_D, FOX_HD))

    tile = pl.BlockSpec((tr, FOX_D), lambda i: (i, 0))
    return pl.pallas_call(
        body, name=name, grid=(s // tr,), in_specs=[tile, tile, pl.BlockSpec((tr, FOX_D), lambda i: (i, 3))],
        out_specs=[tile, tile, pl.BlockSpec((tr, FOX_H), lambda i: (i, 0))],
        out_shape=[jax.ShapeDtypeStruct((s, FOX_D), F32), jax.ShapeDtypeStruct((s, FOX_D), BF16),
                   jax.ShapeDtypeStruct((s, FOX_H), F32)],
        compiler_params=_cp(("parallel",)),
    )(dog, o, qkvg)


def _loss_head(h, g, target, *, name):
    s, d = h.shape
    tr = _pick(s, 512, 8)

    def body(h_ref, g_ref, t_ref, loss_ref, dh_ref, dg_ref):
        i = pl.program_id(0)
        x = h_ref[...]
        gv = g_ref[...]
        r = lax.rsqrt(jnp.mean(x * x, axis=-1, keepdims=True) + EPS)
        xh = x * r
        err = xh * gv - t_ref[...]
        part = 0.5 * jnp.sum(jnp.sum(err * err, axis=1, keepdims=True) * (1.0 / d), axis=0, keepdims=True)
        dy = err * (1.0 / d)
        dyg = dy * gv
        dh_ref[...] = r * (dyg - xh * jnp.mean(dyg * xh, axis=-1, keepdims=True))
        dgp = jnp.sum(dy * xh, axis=0, keepdims=True)

        @pl.when(i == 0)
        def _():
            loss_ref[...] = jnp.zeros_like(loss_ref) + part
            dg_ref[...] = dgp

        @pl.when(i > 0)
        def _():
            loss_ref[...] += part
            dg_ref[...] += dgp

    tile = pl.BlockSpec((tr, d), lambda i: (i, 0))
    vec = pl.BlockSpec((1, d), lambda i: (0, 0))
    return pl.pallas_call(
        body, name=name, grid=(s // tr,), in_specs=[tile, vec, tile],
        out_specs=[pl.BlockSpec((1, LANES), lambda i: (0, 0)), tile, vec],
        out_shape=[jax.ShapeDtypeStruct((1, LANES), F32), jax.ShapeDtypeStruct((s, d), F32),
                   jax.ShapeDtypeStruct((1, d), F32)],
        compiler_params=_cp(("arbitrary",)),
    )(h, g, target)


def _adamw(w, g, m, v, *, name):
    rows, cols = w.shape
    tr = _pick(rows, 256, 8)
    c1 = 1.0 - ADAM_B1 ** ADAM_STEP
    c2 = 1.0 - ADAM_B2 ** ADAM_STEP

    def body(w_ref, g_ref, m_ref, v_ref, d_ref, nm_ref, nv_ref):
        gv = g_ref[...]
        nm = ADAM_B1 * m_ref[...] + (1.0 - ADAM_B1) * gv
        nv = ADAM_B2 * v_ref[...] + (1.0 - ADAM_B2) * (gv * gv)
        d_ref[...] = -ADAM_LR * ((nm / c1) / (jnp.sqrt(nv / c2) + ADAM_EPS) + ADAM_WD * w_ref[...])
        nm_ref[...] = nm
        nv_ref[...] = nv

    tile = pl.BlockSpec((tr, cols), lambda i: (i, 0))
    shp = jax.ShapeDtypeStruct((rows, cols), F32)
    return pl.pallas_call(
        body, name=name, grid=(rows // tr,), in_specs=[tile] * 4, out_specs=[tile] * 3, out_shape=[shp] * 3,
        compiler_params=_cp(("parallel",)),
    )(w, g, m, v)


ANY = pl.BlockSpec(memory_space=pl.ANY)
N_DEV = 8


def _coords():
    return lax.axis_index("x"), lax.axis_index("y"), lax.axis_index("c")


def _other_chips(x, y):
    return [(1 - x, y), (x, 1 - y), (1 - x, 1 - y)]


def _allgather_small(buf, *, name, with_sum):
    rows = buf.shape[0]

    def body(*refs):
        if with_sum:
            x_ref, out_ref, sum_ref, send_sems, recv_sems = refs
        else:
            x_ref, out_ref, send_sems, recv_sems = refs
        x, y, c = _coords()
        me = 4 * x + 2 * y + c
        out_ref[me] = x_ref[...]
        copies = []
        for rel in range(1, N_DEV):
            px = (1 - x) if rel & 4 else x
            py = (1 - y) if rel & 2 else y
            pc = (1 - c) if rel & 1 else c
            cp = pltpu.make_async_remote_copy(
                src_ref=x_ref, dst_ref=out_ref.at[me], send_sem=send_sems.at[rel - 1], recv_sem=recv_sems.at[rel - 1],
                device_id=(px, py, pc), device_id_type=MESH)
            cp.start()
            copies.append(cp)
        for cp in copies:
            cp.wait()
        if with_sum:
            acc = out_ref[0]
            for k in range(1, N_DEV):
                acc = acc + out_ref[k]
            sum_ref[...] = acc

    slots = jax.ShapeDtypeStruct((N_DEV, rows, LANES), F32)
    vm = pl.BlockSpec(memory_space=pltpu.VMEM)
    out_shape = [slots, jax.ShapeDtypeStruct((rows, LANES), F32)] if with_sum else [slots]
    return pl.pallas_call(
        body, name=name, in_specs=[vm], out_specs=[vm] * len(out_shape), out_shape=out_shape,
        scratch_shapes=[pltpu.SemaphoreType.DMA((N_DEV - 1,)), pltpu.SemaphoreType.DMA((N_DEV - 1,))],
    )(buf)


class _Gather:
    per_array = 6

    def __init__(self, arrays):
        self.arrays = list(arrays)

    def out_shapes(self):
        return [jax.ShapeDtypeStruct((4,) + a.shape, a.dtype) for a in self.arrays]

    @staticmethod
    def _ici(ins, outs, send_sems, recv_sems, t, j, px, py, c, slot):
        return pltpu.make_async_remote_copy(
            src_ref=ins[t].at[c], dst_ref=outs[t].at[slot, c], send_sem=send_sems.at[6 * t + j],
            recv_sem=recv_sems.at[6 * t + j], device_id=(px, py, c), device_id_type=MESH)

    @staticmethod
    def _d2d(outs, send_sems, recv_sems, t, j, kj, half, sibling):
        return pltpu.make_async_remote_copy(
            src_ref=outs[t].at[kj, half], dst_ref=outs[t].at[kj, half], send_sem=send_sems.at[6 * t + 3 + j],
            recv_sem=recv_sems.at[6 * t + 3 + j], device_id=sibling, device_id_type=MESH)

    def start(self, ins, outs, send_sems, recv_sems):
        x, y, c = _coords()
        for t in range(len(ins)):
            for j, (px, py) in enumerate(_other_chips(x, y)):
                self._ici(ins, outs, send_sems, recv_sems, t, j, px, py, c, 2 * x + y).start()

    def finish(self, ins, outs, send_sems, recv_sems):
        x, y, c = _coords()
        chips = _other_chips(x, y)
        sibling = (x, y, 1 - c)
        started = []
        for t in range(len(ins)):
            for j, (px, py) in enumerate(chips):
                ici = self._ici(ins, outs, send_sems, recv_sems, t, j, px, py, c, 2 * px + py)
                ici.wait_recv()
                fwd = self._d2d(outs, send_sems, recv_sems, t, j, 2 * px + py, c, sibling)
                fwd.start()
                started += [ici, fwd]
        for t in range(len(ins)):
            for j, (px, py) in enumerate(chips):
                self._d2d(outs, send_sems, recv_sems, t, j, 2 * px + py, 1 - c, sibling).wait_recv()
        for cp in started:
            cp.wait_send()


class _Exchange:
    per_array = 7

    def __init__(self, arrays):
        self.arrays = list(arrays)

    def out_shapes(self):
        return [jax.ShapeDtypeStruct((7,) + a.shape[2:], a.dtype) for a in self.arrays]

    @staticmethod
    def _copies(ins, outs, send_sems, recv_sems):
        x, y, c = _coords()
        for t in range(len(ins)):
            for rel in range(1, N_DEV):
                px = (1 - x) if rel & 4 else x
                py = (1 - y) if rel & 2 else y
                pc = (1 - c) if rel & 1 else c
                yield pltpu.make_async_remote_copy(
                    src_ref=ins[t].at[2 * px + py, pc], dst_ref=outs[t].at[rel - 1], send_sem=send_sems.at[7 * t + rel - 1],
                    recv_sem=recv_sems.at[7 * t + rel - 1], device_id=(px, py, pc), device_id_type=MESH)

    def start(self, ins, outs, send_sems, recv_sems):
        for cp in self._copies(ins, outs, send_sems, recv_sems):
            cp.start()

    def finish(self, ins, outs, send_sems, recv_sems):
        for cp in self._copies(ins, outs, send_sems, recv_sems):
            cp.wait()


def _call(body, *, name, grid, in_specs, out_specs, out_shape, scratch_shapes, args, sem, rider=None, prefetch=()):
    n_in, n_out, n_pre = len(in_specs), len(out_specs), len(prefetch)
    n_c = len(rider.arrays) if rider is not None else 0

    def wrapped(*refs):
        pre, rest = refs[:n_pre], refs[n_pre:]
        ins, cins = rest[:n_in], rest[n_in:n_in + n_c]
        outs = rest[n_in + n_c:n_in + n_c + n_out]
        couts = rest[n_in + n_c + n_out:n_in + 2 * n_c + n_out]
        scratch = rest[n_in + 2 * n_c + n_out:]
        if rider is None:
            body(*pre, *ins, *outs, *scratch)
            return
        send_sems, recv_sems = scratch[-2:]
        ids = [pl.program_id(a) for a in range(len(grid))]
        first = functools.reduce(jnp.logical_and, [i == 0 for i in ids])
        last = functools.reduce(jnp.logical_and, [i == g - 1 for i, g in zip(ids, grid)])

        @pl.when(first)
        def _():
            rider.start(cins, couts, send_sems, recv_sems)

        body(*pre, *ins, *outs, *scratch[:-2])

        @pl.when(last)
        def _():
            rider.finish(cins, couts, send_sems, recv_sems)

    if rider is not None:
        nsem = rider.per_array * n_c
        in_specs = list(in_specs) + [ANY] * n_c
        out_specs = list(out_specs) + [ANY] * n_c
        out_shape = list(out_shape) + rider.out_shapes()
        scratch_shapes = list(scratch_shapes) + [pltpu.SemaphoreType.DMA((nsem,)), pltpu.SemaphoreType.DMA((nsem,))]
        args = list(args) + rider.arrays
        sem = ("arbitrary",) * len(grid)
    if n_pre:
        res = pl.pallas_call(
            wrapped, name=name, out_shape=out_shape, compiler_params=_cp(sem),
            grid_spec=pltpu.PrefetchScalarGridSpec(num_scalar_prefetch=n_pre, grid=grid, in_specs=in_specs,
                                                   out_specs=out_specs, scratch_shapes=scratch_shapes),
        )(*prefetch, *args)
    else:
        res = pl.pallas_call(
            wrapped, name=name, grid=grid, in_specs=in_specs, out_specs=out_specs, out_shape=out_shape,
            scratch_shapes=scratch_shapes, compiler_params=_cp(sem),
        )(*args)
    return list(res[:n_out]), list(res[n_out:])


def _run_rider(rider, *, name):
    n = len(rider.arrays)

    def body(*refs):
        ins, outs = refs[:n], refs[n:2 * n]
        send_sems, recv_sems = refs[2 * n:]
        rider.start(ins, outs, send_sems, recv_sems)
        rider.finish(ins, outs, send_sems, recv_sems)

    nsem = rider.per_array * n
    return pl.pallas_call(
        body, name=name, in_specs=[ANY] * n, out_specs=[ANY] * n, out_shape=rider.out_shapes(),
        scratch_shapes=[pltpu.SemaphoreType.DMA((nsem,)), pltpu.SemaphoreType.DMA((nsem,))],
    )(*rider.arrays)


def _sibling_swap(arrs, *, name):
    n = len(arrs)

    def body(*refs):
        ins, outs = refs[:n], refs[n:2 * n]
        send_sems, recv_sems = refs[2 * n:]
        x, y, c = _coords()
        copies = []
        for t in range(n):
            cp = pltpu.make_async_remote_copy(
                src_ref=ins[t], dst_ref=outs[t], send_sem=send_sems.at[t], recv_sem=recv_sems.at[t],
                device_id=(x, y, 1 - c), device_id_type=MESH)
            cp.start()
            copies.append(cp)
        for cp in copies:
            cp.wait()

    return pl.pallas_call(
        body, name=name, in_specs=[ANY] * n, out_specs=[ANY] * n,
        out_shape=[jax.ShapeDtypeStruct(a.shape, a.dtype) for a in arrs],
        scratch_shapes=[pltpu.SemaphoreType.DMA((n,)), pltpu.SemaphoreType.DMA((n,))],
    )(*arrs)


def _add_selected(stack, others, sel, *, name):
    _, m, cols = stack.shape
    q = others.shape[0]
    tr = _pick(m, 256, 16)

    def body(sel_ref, s_ref, o_ref, out_ref):
        acc = s_ref[0].astype(F32)
        for i in range(q):
            acc = acc + o_ref[i].astype(F32)
        out_ref[...] = acc

    return pl.pallas_call(
        body, name=name,
        grid_spec=pltpu.PrefetchScalarGridSpec(
            num_scalar_prefetch=1, grid=(m // tr,),
            in_specs=[pl.BlockSpec((1, tr, cols), lambda i, sel_ref: (sel_ref[0], i, 0)),
                      pl.BlockSpec((q, tr, cols), lambda i, sel_ref: (0, i, 0))],
            out_specs=pl.BlockSpec((tr, cols), lambda i, sel_ref: (i, 0))),
        out_shape=jax.ShapeDtypeStruct((m, cols), F32),
        compiler_params=_cp(("parallel",)),
    )(sel, stack, others)


BIG = ("ssd_w_in", "ssd_w_out", "fox_w_in", "fox_w_out", "ffn_w_up", "ffn_w_down")
COL_SHARDED = ("ssd_w_in", "fox_w_in", "ffn_w_up")
SMALL = (("mix_norm_g", (4, 1024)), ("ffn_norm_g", (4, 1024)), ("ssd_conv_w", (2, 4, 3072)), ("ssd_conv_b", (2, 3072)),
         ("ssd_dt_bias", (2, 32)), ("ssd_a_log", (2, 32)), ("ssd_d", (2, 32)), ("ssd_norm_g", (2, 2048)),
         ("fox_b_f", (2, 16)), ("fox_q_norm_g", (2, 64)), ("fox_k_norm_g", (2, 64)), ("ffn_conv_w", (4, 3, 2816)),
         ("ffn_conv_b", (4, 2816)), ("final_norm_g", (1024,)), ("loss", (1,)))
NAMES = ("mix_norm_g", "ffn_norm_g", "ssd_w_in", "ssd_conv_w", "ssd_conv_b", "ssd_dt_bias", "ssd_a_log", "ssd_d",
         "ssd_norm_g", "ssd_w_out", "fox_w_in", "fox_b_f", "fox_q_norm_g", "fox_k_norm_g", "fox_w_out", "ffn_w_up",
         "ffn_conv_w", "ffn_conv_b", "ffn_w_down", "final_norm_g")


def _pack(parts):
    flat = jnp.concatenate([jnp.reshape(p, (-1,)).astype(F32) for p in parts])
    rows = -(-flat.shape[0] // (8 * LANES)) * 8
    return jnp.pad(flat, (0, rows * LANES - flat.shape[0])).reshape(rows, LANES)


def _unpack(buf, shapes):
    flat = buf.reshape(-1)
    out, off = [], 0
    for shp in shapes:
        size = 1
        for d in shp:
            size *= d
        out.append(flat[off:off + size].reshape(shp))
        off += size
    return out


def _pad_lanes(a):
    return jnp.pad(a, ((0, 0), (0, LANES - a.shape[1])))


def _pad8(w):
    return jnp.pad(w, ((0, 8 - w.shape[0]), (0, 0)))


def _ssd_forward(h, p, name, rider=None):
    s = h.shape[0]
    hn = _rms_fwd(h, p["mix_g"], gw=D_MODEL, ncol=1, name=f"{name}_norm")
    zx = _matmul(hn, p["w_zx"], mode="nn", name=f"{name}_proj")
    dtp = _matmul(hn, p["w_dt"], mode="nn", name=f"{name}_proj_dt")
    xbc, _ = _conv_fwd(zx, p["conv_w8"], p["conv_b"], kw=SSD_K, width=SSD_CONV_DIM, u_col0=SSD_DI, name=f"{name}_conv")
    dt3 = dtp[:, :SSD_H].reshape(s, SSD_G, SSD_HPG)
    dtg, dtg_t = jnp.transpose(dt3, (1, 0, 2)), jnp.transpose(dt3, (1, 2, 0))
    sp = (p["bias_r"], p["bias_c"], p["alog_r"], p["alog_c"], p["d_r"])
    y, hprev, riding = _ssd_fwd(xbc, dtg, dtg_t, *sp, name=f"{name}_scan", rider=rider)
    y2 = _rms_fwd(y, p["norm_g"], gw=SSD_DI // SSD_G, ncol=SSD_G, z=zx, name=f"{name}_gnorm")
    out = _matmul(y2, p["w_out"], mode="nn", add=h, name=f"{name}_out")
    return out, dict(h=h, hn=hn, zx=zx, xbc=xbc, dtg=dtg, dtg_t=dtg_t, y=y, hprev=hprev, y2=y2), riding


def _ssd_backward(dh1, p, a, name, ride=()):
    s = dh1.shape[0]
    g = {}
    dy2 = _matmul(dh1, p["w_out"], mode="nt", name=f"{name}_out_dx")
    g["w_out"] = _matmul(a["y2"], dh1, mode="tn", out_dtype=BF16, name=f"{name}_out_dw")
    rider = _Exchange(list(ride) + [_to_slabs(g["w_out"], False)])
    dy, dz, g["norm_g"] = _rms_bwd(a["y"], p["norm_g"], dy2, gw=SSD_DI // SSD_G, ncol=SSD_G, z=a["zx"], name=f"{name}_gnorm_b")
    sp = (p["bias_r"], p["bias_c"], p["alog_r"], p["alog_c"], p["d_r"])
    dx, dbm, dcm, ddt, g["dt_bias"], g["a_log"], g["d"], riding = _ssd_bwd(
        a["xbc"], a["dtg"], a["dtg_t"], *sp, a["hprev"], dy, name=f"{name}_scan_b", rider=rider)
    dact = jnp.concatenate([dx, dbm, dcm], axis=1)
    dxbc, dwb, _ = _conv_bwd(a["zx"], p["conv_w8"], p["conv_b"], dact, kw=SSD_K, width=SSD_CONV_DIM, u_col0=SSD_DI,
                             name=f"{name}_conv_b")
    g["conv_w"], g["conv_b"] = dwb[:SSD_K], dwb[7]
    dzx = jnp.concatenate([dz.astype(BF16), dxbc], axis=1)
    ddtp = _pad_lanes(jnp.transpose(ddt, (1, 0, 2)).reshape(s, SSD_H))
    dhn = _matmul(dzx, p["w_zx"], mode="nt", name=f"{name}_proj_dx")
    dhn = _matmul(ddtp, p["w_dt"], mode="nt", add=dhn, name=f"{name}_proj_dt_dx")
    dw_zx = _matmul(a["hn"], dzx, mode="tn", out_dtype=BF16, name=f"{name}_proj_dw")
    dw_dt = _matmul(a["hn"], ddtp, mode="tn", out_dtype=BF16, name=f"{name}_proj_dt_dw")
    g["w_in"] = jnp.concatenate([dw_zx, dw_dt[:, :SSD_H]], axis=1)
    dh, g["mix_g"] = _rms_bwd(a["h"], p["mix_g"], dhn, gw=D_MODEL, ncol=1, add=dh1, name=f"{name}_norm_b")
    g["w_out_received"] = riding[-1]
    return dh, g, riding[:-1]


def _fox_forward(h, p, name, rider=None):
    s = h.shape[0]
    hn = _rms_fwd(h, p["mix_g"], gw=D_MODEL, ncol=1, name=f"{name}_norm")
    qkvg = _matmul(hn, p["w_qkvg"], mode="nn", name=f"{name}_proj")
    fp = _matmul(hn, p["w_f"], mode="nn", name=f"{name}_proj_f")
    qs = _rms_fwd(qkvg, p["gq"] * FOX_SCALE, gw=FOX_D, ncol=1, x_col0=0, sub=FOX_HD, name=f"{name}_qnorm")
    kn = _rms_fwd(qkvg, p["gk"], gw=FOX_D, ncol=1, x_col0=1, sub=FOX_HD, name=f"{name}_knorm")
    f_t = jnp.transpose(fp[:, :FOX_H])
    cum_t = _fgate_fwd(f_t, p["b_f"], name=f"{name}_fgate")
    ck = cum_t.reshape(FOX_PAIRS, 2, s)
    o, lse, riding = _flash_fwd(qs, kn, qkvg, ck, name=f"{name}_attn", rider=rider)
    og = _ogate_fwd(o, qkvg, name=f"{name}_ogate")
    out = _matmul(og, p["w_out"], mode="nn", add=h, name=f"{name}_out")
    return out, dict(h=h, hn=hn, qkvg=qkvg, qs=qs, kn=kn, f_t=f_t, ck=ck, o=o, lse=lse, og=og), riding


def _fox_backward(dh1, p, a, name, ride=()):
    s = dh1.shape[0]
    g = {}
    dog = _matmul(dh1, p["w_out"], mode="nt", name=f"{name}_out_dx")
    g["w_out"] = _matmul(a["og"], dh1, mode="tn", out_dtype=BF16, name=f"{name}_out_dw")
    rider = _Exchange(list(ride) + [_to_slabs(g["w_out"], False)])
    do, dgate, delta = _ogate_bwd(dog, a["o"], a["qkvg"], name=f"{name}_ogate_b")
    swap = lambda v: jnp.transpose(v, (0, 2, 1))
    dl_t = jnp.transpose(delta.reshape(s, FOX_PAIRS, 2), (1, 2, 0))
    dq, dk, dv, dcq, dck, riding = _flash_bwd(a["qs"], a["kn"], a["qkvg"], do, swap(a["lse"]), dl_t, swap(a["ck"]),
                                              name=f"{name}_attn_b", rider=rider)
    dq_raw, dgq = _rms_bwd(a["qkvg"], p["gq"], dq, gw=FOX_D, ncol=1, x_col0=0, sub=FOX_HD, dx_dtype=BF16, name=f"{name}_qnorm_b")
    dk_raw, dgk = _rms_bwd(a["qkvg"], p["gk"], dk, gw=FOX_D, ncol=1, x_col0=1, sub=FOX_HD, dx_dtype=BF16, name=f"{name}_knorm_b")
    g["gq"] = dgq.reshape(FOX_H, FOX_HD).sum(axis=0)
    g["gk"] = dgk.reshape(FOX_H, FOX_HD).sum(axis=0)
    df_t, dbf = _fgate_bwd(dcq.reshape(FOX_H, s), swap(dck).reshape(FOX_H, s), a["f_t"], p["b_f"], name=f"{name}_fgate_b")
    g["b_f"] = dbf[:, 0]
    dproj = jnp.concatenate([dq_raw, dk_raw, dv.astype(BF16), dgate], axis=1)
    dfp = _pad_lanes(jnp.transpose(df_t))
    dhn = _matmul(dproj, p["w_qkvg"], mode="nt", name=f"{name}_proj_dx")
    dhn = _matmul(dfp, p["w_f"], mode="nt", add=dhn, name=f"{name}_proj_f_dx")
    dw_qkvg = _matmul(a["hn"], dproj, mode="tn", out_dtype=BF16, name=f"{name}_proj_dw")
    dw_f = _matmul(a["hn"], dfp, mode="tn", out_dtype=BF16, name=f"{name}_proj_f_dw")
    g["w_in"] = jnp.concatenate([dw_qkvg, dw_f[:, :FOX_H]], axis=1)
    dh, g["mix_g"] = _rms_bwd(a["h"], p["mix_g"], dhn, gw=D_MODEL, ncol=1, add=dh1, name=f"{name}_norm_b")
    g["w_out_received"] = riding[-1]
    return dh, g, riding[:-1]


def _ffn_forward(h, p, name, rider=None):
    hn = _rms_fwd(h, p["ffn_g"], gw=D_MODEL, ncol=1, name=f"{name}_norm")
    u = _matmul(hn, p["w_up"], mode="nn", name=f"{name}_up")
    act, riding = _conv_fwd(u, p["conv_w8"], p["conv_b"], kw=FFN_K, width=D_FF, u_col0=0, mul_col0=D_FF, out_dtype=BF16,
                            name=f"{name}_glu", rider=rider)
    out = _matmul(act, p["w_down"], mode="nn", add=h, name=f"{name}_down")
    return out, dict(h=h, hn=hn, u=u, act=act), riding


def _ffn_backward(dh2, p, a, name):
    g = {}
    dact = _matmul(dh2, p["w_down"], mode="nt", name=f"{name}_down_dx")
    g["w_down"] = _matmul(a["act"], dh2, mode="tn", out_dtype=BF16, name=f"{name}_down_dw")
    du, dwb, _ = _conv_bwd(a["u"], p["conv_w8"], p["conv_b"], dact, kw=FFN_K, width=D_FF, u_col0=0,
                           mul_col0=D_FF, name=f"{name}_glu_b")
    g["conv_w"], g["conv_b"] = dwb[:FFN_K], dwb[7]
    dhn = _matmul(du, p["w_up"], mode="nt", name=f"{name}_up_dx")
    g["w_up"] = _matmul(a["hn"], du, mode="tn", out_dtype=BF16, name=f"{name}_up_dw")
    dh, g["ffn_g"] = _rms_bwd(a["h"], p["ffn_g"], dhn, gw=D_MODEL, ncol=1, add=dh2, name=f"{name}_norm_b")
    return dh, g


def _to_slabs(dw, col_sharded):
    rows, cols = dw.shape
    if col_sharded:
        return jnp.transpose(dw.reshape(rows, 4, cols // 4), (1, 0, 2)).reshape(4, 2, rows // 2, cols // 4)
    return dw.reshape(4, 2, rows // 8, cols)


def kernel(x, mix_norm_g, ffn_norm_g, ssd_w_in, ssd_conv_w, ssd_conv_b, ssd_dt_bias, ssd_a_log, ssd_d, ssd_norm_g, ssd_w_out, fox_w_in, fox_b_f, fox_q_norm_g, fox_k_norm_g, fox_w_out, ffn_w_up, ffn_conv_w, ffn_conv_b, ffn_w_down, final_norm_g, loss_target, m_mix_norm_g, m_ffn_norm_g, m_ssd_w_in, m_ssd_conv_w, m_ssd_conv_b, m_ssd_dt_bias, m_ssd_a_log, m_ssd_d, m_ssd_norm_g, m_ssd_w_out, m_fox_w_in, m_fox_b_f, m_fox_q_norm_g, m_fox_k_norm_g, m_fox_w_out, m_ffn_w_up, m_ffn_conv_w, m_ffn_conv_b, m_ffn_w_down, m_final_norm_g, v_mix_norm_g, v_ffn_norm_g, v_ssd_w_in, v_ssd_conv_w, v_ssd_conv_b, v_ssd_dt_bias, v_ssd_a_log, v_ssd_d, v_ssd_norm_g, v_ssd_w_out, v_fox_w_in, v_fox_b_f, v_fox_q_norm_g, v_fox_k_norm_g, v_fox_w_out, v_ffn_w_up, v_ffn_conv_w, v_ffn_conv_b, v_ffn_w_down, v_final_norm_g):
    w = dict(mix_norm_g=mix_norm_g, ffn_norm_g=ffn_norm_g, ssd_w_in=ssd_w_in, ssd_conv_w=ssd_conv_w, ssd_conv_b=ssd_conv_b,
             ssd_dt_bias=ssd_dt_bias, ssd_a_log=ssd_a_log, ssd_d=ssd_d, ssd_norm_g=ssd_norm_g, ssd_w_out=ssd_w_out,
             fox_w_in=fox_w_in, fox_b_f=fox_b_f, fox_q_norm_g=fox_q_norm_g, fox_k_norm_g=fox_k_norm_g, fox_w_out=fox_w_out,
             ffn_w_up=ffn_w_up, ffn_conv_w=ffn_conv_w, ffn_conv_b=ffn_conv_b, ffn_w_down=ffn_w_down, final_norm_g=final_norm_g)
    m_in = dict(zip(NAMES, (m_mix_norm_g, m_ffn_norm_g, m_ssd_w_in, m_ssd_conv_w, m_ssd_conv_b, m_ssd_dt_bias, m_ssd_a_log,
                            m_ssd_d, m_ssd_norm_g, m_ssd_w_out, m_fox_w_in, m_fox_b_f, m_fox_q_norm_g, m_fox_k_norm_g,
                            m_fox_w_out, m_ffn_w_up, m_ffn_conv_w, m_ffn_conv_b, m_ffn_w_down, m_final_norm_g)))
    v_in = dict(zip(NAMES, (v_mix_norm_g, v_ffn_norm_g, v_ssd_w_in, v_ssd_conv_w, v_ssd_conv_b, v_ssd_dt_bias, v_ssd_a_log,
                            v_ssd_d, v_ssd_norm_g, v_ssd_w_out, v_fox_w_in, v_fox_b_f, v_fox_q_norm_g, v_fox_k_norm_g,
                            v_fox_w_out, v_ffn_w_up, v_ffn_conv_w, v_ffn_conv_b, v_ffn_w_down, v_final_norm_g)))
    cx, cy, cc = _coords()
    chip = 2 * cx + cy
    h = x[0]
    target = loss_target[0]

    conv_shapes = [ssd_conv_w.shape, ffn_conv_w.shape]
    slots = _allgather_small(_pack([ssd_conv_w, ffn_conv_w]), name="gather_conv_w", with_sum=False)[0]
    per_chip = [_unpack(slots[2 * q], conv_shapes) for q in range(4)]
    ssd_conv_full = jnp.concatenate([pc[0] for pc in per_chip], axis=2)
    ffn_conv_full = jnp.concatenate([pc[1] for pc in per_chip], axis=2)
    low = {n: w[n].astype(BF16) for n in BIG}
    sub_weights = dict(ssd=("ssd_w_in", "ssd_w_out"), fox=("fox_w_in", "fox_w_out"), ffn=("ffn_w_up", "ffn_w_down"))

    def shards_of(kind, idx):
        return [low[n][idx].reshape(2, low[n].shape[1] // 2, low[n].shape[2]) for n in sub_weights[kind]]

    def assemble(kind, idx, gathered):
        full = []
        for n, own, gth in zip(sub_weights[kind], shards_of(kind, idx), gathered):
            gth = lax.dynamic_update_slice(gth, own[None], (chip, 0, 0, 0))
            _, _, half, cols = gth.shape
            if n in COL_SHARDED:
                full.append(jnp.transpose(gth.reshape(4, 2 * half, cols), (1, 0, 2)).reshape(2 * half, 4 * cols))
            else:
                full.append(gth.reshape(8 * half, cols))
        return full

    def ssd_params(j, i, weights):
        w_in, w_out = weights
        g3 = lambda v: v.reshape(SSD_G, 1, SSD_HPG)
        g3c = lambda v: v.reshape(SSD_G, SSD_HPG, 1)
        return dict(mix_g=mix_norm_g[i][None], w_zx=w_in[:, :SSD_ZX], w_dt=_pad_lanes(w_in[:, SSD_ZX:]),
                    conv_w8=_pad8(ssd_conv_full[j]), conv_b=ssd_conv_b[j][None], bias_r=g3(ssd_dt_bias[j]),
                    bias_c=g3c(ssd_dt_bias[j]), alog_r=g3(ssd_a_log[j]), alog_c=g3c(ssd_a_log[j]), d_r=g3(ssd_d[j]),
                    norm_g=ssd_norm_g[j][None], w_out=w_out)

    def fox_params(j, i, weights):
        w_in, w_out = weights
        return dict(mix_g=mix_norm_g[i][None], w_qkvg=w_in[:, :4 * FOX_D], w_f=_pad_lanes(w_in[:, 4 * FOX_D:]),
                    gq=jnp.tile(fox_q_norm_g[j], FOX_H)[None], gk=jnp.tile(fox_k_norm_g[j], FOX_H)[None],
                    b_f=fox_b_f[j][:, None], w_out=w_out)

    def ffn_params(i, weights):
        w_up, w_down = weights
        return dict(ffn_g=ffn_norm_g[i][None], w_up=w_up, conv_w8=_pad8(ffn_conv_full[i]), conv_b=ffn_conv_b[i][None],
                    w_down=w_down)

    order = [("ssd", 0), ("ffn", 0), ("fox", 0), ("ffn", 1), ("ssd", 1), ("ffn", 2), ("fox", 1), ("ffn", 3)]
    fetch = {("ssd", 0): [("ffn", 0)], ("ffn", 0): [("fox", 0)], ("fox", 0): [("ffn", 1), ("ssd", 1), ("ffn", 2)],
             ("ssd", 1): [("fox", 1)], ("fox", 1): [("ffn", 3)]}
    ready = {("ssd", 0): assemble("ssd", 0, _run_rider(_Gather(shards_of("ssd", 0)), name="gather_first"))}
    params, acts = {}, {}
    forward = dict(ssd=_ssd_forward, fox=_fox_forward, ffn=_ffn_forward)
    for kind, idx in order:
        if kind == "ssd":
            params[kind, idx] = ssd_params(idx, 2 * idx, ready.pop((kind, idx)))
        elif kind == "fox":
            params[kind, idx] = fox_params(idx, 2 * idx + 1, ready.pop((kind, idx)))
        else:
            params[kind, idx] = ffn_params(idx, ready.pop((kind, idx)))
        wanted = fetch.get((kind, idx), [])
        rider = _Gather([s for sub in wanted for s in shards_of(*sub)]) if wanted else None
        h, acts[kind, idx], riding = forward[kind](h, params[kind, idx], f"{kind}{idx}", rider=rider)
        for q, sub in enumerate(wanted):
            ready[sub] = assemble(*sub, riding[2 * q:2 * q + 2])
    loss_part, dh, d_final_g = _loss_head(h, final_norm_g[None], target, name="loss_head")

    backward = dict(ssd=_ssd_backward, fox=_fox_backward, ffn=_ffn_backward)
    grad_keys = dict(ssd=("w_in", "w_out"), fox=("w_in", "w_out"), ffn=("w_up", "w_down"))
    sub_g, slabs, received = {}, {}, {}
    waiting = []
    for sub in reversed(order):
        kind = sub[0]
        if kind == "ffn":
            dh, sub_g[sub] = _ffn_backward(dh, params[sub], acts[sub], f"{kind}{sub[1]}")
        else:
            dh, sub_g[sub], got = backward[kind](dh, params[sub], acts[sub], f"{kind}{sub[1]}",
                                                 ride=[slabs[key] for key in waiting])
            received.update(zip(waiting, got))
            waiting = []
        for q, (key, n) in enumerate(zip(grad_keys[kind], sub_weights[kind])):
            slabs[sub, q] = _to_slabs(sub_g[sub][key], n in COL_SHARDED)
            if kind != "ffn" and q == 1:
                received[sub, q] = sub_g[sub]["w_out_received"]
            else:
                waiting.append((sub, q))
    received.update(zip(waiting, _run_rider(_Exchange([slabs[key] for key in waiting]), name="rs_last_exchange")))
    grad_x = dh[None]
    ssd_g, fox_g = [sub_g["ssd", 0], sub_g["ssd", 1]], [sub_g["fox", 0], sub_g["fox", 1]]
    mix_g = [ssd_g[0], fox_g[0], ssd_g[1], fox_g[1]]
    ffn_g = [sub_g["ffn", i] for i in range(DEPTH)]

    me = jnp.reshape(2 * chip + cc, (1,)).astype(jnp.int32)
    finals = {}
    for sub in order:
        for q in range(2):
            _, _, m, cols = slabs[sub, q].shape
            finals[sub, q] = _add_selected(slabs[sub, q].reshape(8, m, cols), received[sub, q], me,
                                           name=f"rs_add_{sub[0]}{sub[1]}_{q}")
    keys = list(finals)
    others = dict(zip(keys, _sibling_swap([finals[key] for key in keys], name="rs_result_swap")))
    grads = {}
    for kind, names in sub_weights.items():
        for q, n in enumerate(names):
            subs = [sub for sub in sorted(set(order)) if sub[0] == kind]
            mine = jnp.stack([finals[sub, q] for sub in subs])
            theirs = jnp.stack([others[sub, q] for sub in subs])
            halves = jnp.stack([jnp.where(cc == 0, mine, theirs), jnp.where(cc == 0, theirs, mine)], axis=1)
            grads[n] = halves.reshape(w[n].shape)
    small = dict(
        mix_norm_g=jnp.concatenate([g["mix_g"] for g in mix_g], axis=0),
        ffn_norm_g=jnp.concatenate([g["ffn_g"] for g in ffn_g], axis=0),
        ssd_conv_w=jnp.stack([g["conv_w"] for g in ssd_g]), ssd_conv_b=jnp.stack([g["conv_b"] for g in ssd_g]),
        ssd_dt_bias=jnp.stack([g["dt_bias"].reshape(SSD_H) for g in ssd_g]),
        ssd_a_log=jnp.stack([g["a_log"].reshape(SSD_H) for g in ssd_g]),
        ssd_d=jnp.stack([g["d"].reshape(SSD_H) for g in ssd_g]),
        ssd_norm_g=jnp.concatenate([g["norm_g"] for g in ssd_g], axis=0),
        fox_b_f=jnp.stack([g["b_f"] for g in fox_g]), fox_q_norm_g=jnp.stack([g["gq"] for g in fox_g]),
        fox_k_norm_g=jnp.stack([g["gk"] for g in fox_g]),
        ffn_conv_w=jnp.stack([g["conv_w"] for g in ffn_g]), ffn_conv_b=jnp.stack([g["conv_b"] for g in ffn_g]),
        final_norm_g=d_final_g[0], loss=loss_part[0, :1])
    _, total = _allgather_small(_pack([small[n] for n, _ in SMALL]), name="reduce_small", with_sum=True)
    for (n, shp), val in zip(SMALL, _unpack(total, [shp for _, shp in SMALL])):
        grads[n] = val
    loss = grads.pop("loss")[0]
    grads["ssd_conv_w"] = lax.dynamic_slice_in_dim(grads["ssd_conv_w"], chip * ssd_conv_w.shape[2], ssd_conv_w.shape[2], axis=2)
    grads["ffn_conv_w"] = lax.dynamic_slice_in_dim(grads["ffn_conv_w"], chip * ffn_conv_w.shape[2], ffn_conv_w.shape[2], axis=2)

    deltas, new_m, new_v = {}, {}, {}
    for n in NAMES:
        shp = w[n].shape
        two_d = (1, shp[0]) if len(shp) == 1 else (-1, shp[-1])
        r2 = lambda a: a.reshape(two_d)
        d, nm, nv = _adamw(r2(w[n]), r2(grads[n]), r2(m_in[n]), r2(v_in[n]), name=f"adamw_{n}")
        deltas[n], new_m[n], new_v[n] = d.reshape(shp), nm.reshape(shp), nv.reshape(shp)
    return (loss, grad_x, *[grads[n] for n in NAMES], *[deltas[n] for n in NAMES], *[new_m[n] for n in NAMES],
            *[new_v[n] for n in NAMES])
```

```python
import functools

import jax
import jax.numpy as jnp
from jax import lax
from jax.experimental import pallas as pl
from jax.experimental.pallas import tpu as pltpu

F32 = jnp.float32
BF16 = jnp.bfloat16
HI = lax.Precision.HIGHEST
MESH = pl.DeviceIdType.MESH

D_MODEL = 1024
DEPTH = 4
EPS = 1e-6
SSD_DI = 2048
SSD_HD = 64
SSD_G = 4
SSD_HPG = 8
SSD_N = 128
SSD_K = 4
CHUNK = 128
SSD_CONV_DIM = 3072
SSD_ZX = SSD_DI + SSD_CONV_DIM
SSD_H = 32
FOX_HD = 64
FOX_H = 16
FOX_D = 1024
D_FF = 2816
FFN_K = 3
LANES = 128
VMEM_LIMIT = 56 * 1024 * 1024

ADAM_LR = 0.001
ADAM_B1 = 0.9
ADAM_B2 = 0.999
ADAM_EPS = 1e-08
ADAM_WD = 0.01
ADAM_STEP = 10

NN = (((1,), (0,)), ((), ()))
NT = (((1,), (1,)), ((), ()))
TN = (((0,), (0,)), ((), ()))


def _pick(n, cap, mult=LANES):
    best = None
    for t in range(mult, min(n, cap) + 1, mult):
        if n % t == 0:
            best = t
    return best if best is not None else n


def _cp(sem):
    return pltpu.CompilerParams(dimension_semantics=sem, vmem_limit_bytes=VMEM_LIMIT)


def _sigmoid(x):
    return jax.nn.sigmoid(x)


def _silu(x):
    return x * _sigmoid(x)


def _dsilu(x):
    s = _sigmoid(x)
    return s * (1.0 + x * (1.0 - s))


def _softplus(x):
    e = jnp.exp(-jnp.abs(x))
    u = 1.0 + e
    l1p = jnp.where(u == 1.0, e, jnp.log(u) * (e / (u - 1.0)))
    return jnp.maximum(x, 0.0) + l1p


def _dotf(a, b, dn=NN, *, onehot="b", pieces=2):
    x, e = (a, b) if onehot == "b" else (b, a)
    e = e.astype(BF16)
    acc = None
    for n in range(pieces):
        hi = x.astype(BF16)
        part = lax.dot_general(hi, e, dn, preferred_element_type=F32) if onehot == "b" else \
            lax.dot_general(e, hi, dn, preferred_element_type=F32)
        acc = part if acc is None else acc + part
        if n + 1 < pieces:
            x = x - hi.astype(F32)
    return acc


def _dotb(a, b, dn=NN):
    return lax.dot_general(a.astype(BF16), b.astype(BF16), dn, preferred_element_type=F32)


def _group_matrix(width, sub, transpose=False):
    ng = width // sub
    shape = (ng, width) if transpose else (width, ng)
    lane = lax.broadcasted_iota(jnp.int32, shape, 1 if transpose else 0)
    grp = lax.broadcasted_iota(jnp.int32, shape, 0 if transpose else 1)
    return (lane // sub == grp).astype(F32)


def _gmean(v, sub):
    width = v.shape[-1]
    if sub == width:
        return jnp.mean(v, axis=-1, keepdims=True)
    s = _dotf(v, _group_matrix(width, sub))
    return _dotf(s, _group_matrix(width, sub, transpose=True)) * (1.0 / sub)


def _matmul(a, b, *, mode, name, out_dtype=F32, add=None):
    a_planes = a.shape[0] if (mode == "nt" and a.ndim == 3) else 0
    b_planes = b.shape[0] if (mode == "tn" and b.ndim == 3) else 0
    a2 = (a.shape[1], a.shape[0] * a.shape[2]) if a_planes else a.shape
    b2 = (b.shape[1], b.shape[0] * b.shape[2]) if b_planes else b.shape
    if mode == "nn":
        (m, k), (k2, n) = a2, b2
    elif mode == "nt":
        (m, k), (n, k2) = a2, b2
    else:
        (k, m), (k2, n) = a2, b2
    assert k == k2, (a.shape, b.shape, mode)
    tm, tn = _pick(m, 1536), _pick(n // b_planes if b_planes else n, 1536)
    tk = _pick(k // a_planes if a_planes else k, 1536)
    nk = k // tk
    dn = {"nn": NN, "nt": NT, "tn": TN}[mode]
    has_add = add is not None

    def body(*refs):
        if has_add:
            a_ref, b_ref, add_ref, o_ref, acc_ref = refs
        else:
            a_ref, b_ref, o_ref, acc_ref = refs
            add_ref = None
        kk = pl.program_id(2)
        part = _dotb(a_ref[0] if a_planes else a_ref[...], b_ref[0] if b_planes else b_ref[...], dn)

        def finish(r):
            if has_add:
                r = r + add_ref[...]
            o_ref[...] = r.astype(out_dtype)

        if nk == 1:
            finish(part)
        else:
            @pl.when(kk == 0)
            def _():
                acc_ref[...] = part

            @pl.when(kk > 0)
            def _():
                acc_ref[...] += part

            @pl.when(kk == nk - 1)
            def _():
                finish(acc_ref[...])

    if mode == "nn":
        a_spec = pl.BlockSpec((tm, tk), lambda i, j, q: (i, q))
        b_spec = pl.BlockSpec((tk, tn), lambda i, j, q: (q, j))
    elif mode == "nt":
        per = (k // a_planes) // tk if a_planes else 0
        a_spec = (pl.BlockSpec((1, tm, tk), lambda i, j, q: (q // per, i, q % per)) if a_planes
                  else pl.BlockSpec((tm, tk), lambda i, j, q: (i, q)))
        b_spec = pl.BlockSpec((tn, tk), lambda i, j, q: (j, q))
    else:
        per = (n // b_planes) // tn if b_planes else 0
        a_spec = pl.BlockSpec((tk, tm), lambda i, j, q: (q, i))
        b_spec = (pl.BlockSpec((1, tk, tn), lambda i, j, q: (j // per, q, j % per)) if b_planes
                  else pl.BlockSpec((tk, tn), lambda i, j, q: (q, j)))
    o_spec = pl.BlockSpec((tm, tn), lambda i, j, q: (i, j))
    in_specs = [a_spec, b_spec] + ([o_spec] if has_add else [])
    args = (a, b) + ((add,) if has_add else ())
    return pl.pallas_call(
        body, name=name, grid=(m // tm, n // tn, nk), in_specs=in_specs, out_specs=o_spec,
        out_shape=jax.ShapeDtypeStruct((m, n), out_dtype),
        scratch_shapes=[pltpu.VMEM((tm, tn) if nk > 1 else (8, LANES), F32)],
        compiler_params=_cp(("parallel", "parallel", "arbitrary")),
    )(*args)


def _rms_fwd(x, g, *, gw, ncol, name, x_col0=0, sub=None, z=None, z_col0=0, out_dtype=BF16):
    rows = x.shape[0]
    tr = _pick(rows, 512, 8)
    sub = gw if sub is None else sub
    gated = z is not None

    def body(*refs):
        if gated:
            x_ref, z_ref, g_ref, o_ref = refs
            xv = x_ref[...] * _silu(z_ref[...])
        else:
            x_ref, g_ref, o_ref = refs
            xv = x_ref[...]
        r = lax.rsqrt(_gmean(xv * xv, sub) + EPS)
        o_ref[...] = (xv * r * g_ref[...]).astype(out_dtype)

    specs = [pl.BlockSpec((tr, gw), lambda j, i: (i, x_col0 + j))]
    args = [x]
    if gated:
        specs.append(pl.BlockSpec((tr, gw), lambda j, i: (i, z_col0 + j)))
        args.append(z)
    specs.append(pl.BlockSpec((1, gw), lambda j, i: (0, j)))
    args.append(g)
    return pl.pallas_call(
        body, name=name, grid=(ncol, rows // tr), in_specs=specs,
        out_specs=pl.BlockSpec((tr, gw), lambda j, i: (i, j)),
        out_shape=jax.ShapeDtypeStruct((rows, gw * ncol), out_dtype),
        compiler_params=_cp(("parallel", "parallel")),
    )(*args)


def _rms_bwd(x, g, dy, *, gw, ncol, name, x_col0=0, sub=None, z=None, z_col0=0, add=None, dx_dtype=F32):
    rows = x.shape[0]
    tr = _pick(rows, 512, 8)
    sub = gw if sub is None else sub
    gated = z is not None
    has_add = add is not None

    def body(*refs):
        refs = list(refs)
        x_ref = refs.pop(0)
        z_ref = refs.pop(0) if gated else None
        g_ref = refs.pop(0)
        dy_ref = refs.pop(0)
        add_ref = refs.pop(0) if has_add else None
        dx_ref = refs.pop(0)
        dz_ref = refs.pop(0) if gated else None
        dg_ref = refs.pop(0)
        i = pl.program_id(1)
        xv = x_ref[...]
        if gated:
            zz = z_ref[...]
            yz = xv * _silu(zz)
        else:
            yz = xv
        r = lax.rsqrt(_gmean(yz * yz, sub) + EPS)
        xh = yz * r
        dy = dy_ref[...].astype(F32)
        dyg = dy * g_ref[...]
        d_yz = r * (dyg - xh * _gmean(dyg * xh, sub))
        if gated:
            dx_ref[...] = (d_yz * _silu(zz)).astype(dx_dtype)
            dz_ref[...] = (d_yz * xv * _dsilu(zz)).astype(dx_dtype)
        elif has_add:
            dx_ref[...] = (d_yz + add_ref[...]).astype(dx_dtype)
        else:
            dx_ref[...] = d_yz.astype(dx_dtype)
        part = jnp.sum(dy * xh, axis=0, keepdims=True)

        @pl.when(i == 0)
        def _():
            dg_ref[...] = part

        @pl.when(i > 0)
        def _():
            dg_ref[...] += part

    tile = pl.BlockSpec((tr, gw), lambda j, i: (i, j))
    specs = [pl.BlockSpec((tr, gw), lambda j, i: (i, x_col0 + j))]
    args = [x]
    if gated:
        specs.append(pl.BlockSpec((tr, gw), lambda j, i: (i, z_col0 + j)))
        args.append(z)
    specs += [pl.BlockSpec((1, gw), lambda j, i: (0, j)), tile]
    args += [g, dy]
    if has_add:
        specs.append(tile)
        args.append(add)
    width = gw * ncol
    out_shape = [jax.ShapeDtypeStruct((rows, width), dx_dtype)]
    out_specs = [tile]
    if gated:
        out_shape.append(jax.ShapeDtypeStruct((rows, width), dx_dtype))
        out_specs.append(tile)
    out_shape.append(jax.ShapeDtypeStruct((1, width), F32))
    out_specs.append(pl.BlockSpec((1, gw), lambda j, i: (0, j)))
    return pl.pallas_call(
        body, name=name, grid=(ncol, rows // tr), in_specs=specs, out_specs=out_specs, out_shape=out_shape,
        compiler_params=_cp(("parallel", "arbitrary")),
    )(*args)


HALO = 8


def _conv_rows(tc):
    return 16 * 8 * LANES // tc


def _conv_fwd(u, w8, b, *, kw, width, name, u_col0=0, mul_col0=None, out_dtype=F32, rider=None):
    rows = u.shape[0]
    ts = _pick(rows, 512, 8)
    tc = _pick(width, 512)
    gated = mul_col0 is not None
    c0 = u_col0 // tc
    m0 = (mul_col0 // tc) if gated else 0
    assert u_col0 % tc == 0 and (not gated or mul_col0 % tc == 0)

    def body(*refs):
        if gated:
            cur_ref, halo_ref, mul_ref, w_ref, b_ref, o_ref, ext = refs
        else:
            cur_ref, halo_ref, w_ref, b_ref, o_ref, ext = refs
        i = pl.program_id(0)
        ext[pl.ds(0, HALO), :] = jnp.where(i == 0, 0.0, halo_ref[...])
        ext[pl.ds(HALO, ts), :] = cur_ref[...]
        bias = b_ref[...]
        taps = [w_ref[k:k + 1, :] for k in range(kw)]
        rb = _conv_rows(tc)
        for r0 in range(0, ts, rb):
            pre = bias + taps[0] * ext[pl.ds(r0 + HALO - (kw - 1), rb), :]
            for k in range(1, kw):
                pre = pre + taps[k] * ext[pl.ds(r0 + HALO - (kw - 1) + k, rb), :]
            act = _silu(pre)
            if gated:
                act = act * mul_ref[pl.ds(r0, rb), :]
            o_ref[pl.ds(r0, rb), :] = act.astype(out_dtype)

    hb = ts // HALO
    specs = [pl.BlockSpec((ts, tc), lambda i, j: (i, c0 + j)),
             pl.BlockSpec((HALO, tc), lambda i, j: (jnp.maximum(i * hb - 1, 0), c0 + j))]
    args = [u, u]
    if gated:
        specs.append(pl.BlockSpec((ts, tc), lambda i, j: (i, m0 + j)))
        args.append(u)
    specs += [pl.BlockSpec((8, tc), lambda i, j: (0, j)), pl.BlockSpec((1, tc), lambda i, j: (0, j))]
    args += [w8, b]
    outs, riding = _call(
        body, name=name, grid=(rows // ts, width // tc), in_specs=specs,
        out_specs=[pl.BlockSpec((ts, tc), lambda i, j: (i, j))],
        out_shape=[jax.ShapeDtypeStruct((rows, width), out_dtype)],
        scratch_shapes=[pltpu.VMEM((ts + HALO, tc), F32)], sem=("parallel", "parallel"), rider=rider, args=args)
    return outs + [riding]


def _conv_bwd(u, w8, b, dact, *, kw, width, name, u_col0=0, mul_col0=None, du_dtype=BF16, rider=None):
    rows = u.shape[0]
    ts = _pick(rows, 512, 8)
    tc = _pick(width, 512)
    gated = mul_col0 is not None
    c0 = u_col0 // tc
    m0 = (mul_col0 // tc) if gated else 0
    nt = rows // ts
    hb = ts // HALO

    def body(*refs):
        refs = list(refs)
        cur_ref, halo_ref = refs.pop(0), refs.pop(0)
        mul_ref = refs.pop(0) if gated else None
        w_ref, b_ref, da_ref = refs.pop(0), refs.pop(0), refs.pop(0)
        du_ref = refs.pop(0)
        dwb_ref, ext_u, ext_d = refs
        t = pl.program_id(1)
        ti = nt - 1 - t
        ext_u[pl.ds(0, HALO), :] = jnp.where(ti == 0, 0.0, halo_ref[...])
        ext_u[pl.ds(HALO, ts), :] = cur_ref[...]

        @pl.when(t == 0)
        def _():
            ext_d[pl.ds(ts, HALO), :] = jnp.zeros((HALO, tc), F32)
            dwb_ref[...] = jnp.zeros((8, tc), F32)

        bias = b_ref[...]
        taps = [w_ref[k:k + 1, :] for k in range(kw)]
        rb = _conv_rows(tc)
        dw_acc = [jnp.zeros((1, tc), F32) for _ in range(kw)]
        db_acc = jnp.zeros((1, tc), F32)
        for r0 in reversed(range(0, ts, rb)):
            shifted = [ext_u[pl.ds(r0 + HALO - (kw - 1) + k, rb), :] for k in range(kw)]
            pre = bias + taps[0] * shifted[0]
            for k in range(1, kw):
                pre = pre + taps[k] * shifted[k]
            sg = _sigmoid(pre)
            dsilu = sg * (1.0 + pre * (1.0 - sg))
            da = da_ref[pl.ds(r0, rb), :].astype(F32)
            if gated:
                du_ref[1, pl.ds(r0, rb), :] = (da * (pre * sg)).astype(du_dtype)
                dgp = da * mul_ref[pl.ds(r0, rb), :] * dsilu
            else:
                dgp = da * dsilu
            ext_d[pl.ds(r0, rb), :] = dgp
            du = taps[kw - 1] * dgp
            for k in range(kw - 1):
                du = du + taps[k] * ext_d[pl.ds(r0 + kw - 1 - k, rb), :]
            if gated:
                du_ref[0, pl.ds(r0, rb), :] = du.astype(du_dtype)
            else:
                du_ref[pl.ds(r0, rb), :] = du.astype(du_dtype)
            for k in range(kw):
                dw_acc[k] = dw_acc[k] + jnp.sum(dgp * shifted[k], axis=0, keepdims=True)
            db_acc = db_acc + jnp.sum(dgp, axis=0, keepdims=True)
        for k in range(kw):
            dwb_ref[k:k + 1, :] += dw_acc[k]
        dwb_ref[7:8, :] += db_acc
        ext_d[pl.ds(ts, HALO), :] = ext_d[pl.ds(0, HALO), :]

    specs = [pl.BlockSpec((ts, tc), lambda j, t: (nt - 1 - t, c0 + j)),
             pl.BlockSpec((HALO, tc), lambda j, t: (jnp.maximum((nt - 1 - t) * hb - 1, 0), c0 + j))]
    args = [u, u]
    if gated:
        specs.append(pl.BlockSpec((ts, tc), lambda j, t: (nt - 1 - t, m0 + j)))
        args.append(u)
    tile = pl.BlockSpec((ts, tc), lambda j, t: (nt - 1 - t, j))
    specs += [pl.BlockSpec((8, tc), lambda j, t: (0, j)), pl.BlockSpec((1, tc), lambda j, t: (0, j)), tile]
    args += [w8, b, dact]
    if gated:
        out_shape = [jax.ShapeDtypeStruct((2, rows, width), du_dtype)]
        out_specs = [pl.BlockSpec((2, ts, tc), lambda j, t: (0, nt - 1 - t, j))]
    else:
        out_shape = [jax.ShapeDtypeStruct((rows, width), du_dtype)]
        out_specs = [tile]
    out_shape.append(jax.ShapeDtypeStruct((8, width), F32))
    out_specs.append(pl.BlockSpec((8, tc), lambda j, t: (0, j)))
    outs, riding = _call(
        body, name=name, grid=(width // tc, nt), in_specs=specs, out_specs=out_specs, out_shape=out_shape,
        scratch_shapes=[pltpu.VMEM((ts + HALO, tc), F32), pltpu.VMEM((ts + HALO, tc), F32)],
        sem=("parallel", "arbitrary"), rider=rider, args=args)
    return outs + [riding]


GW = SSD_HPG * SSD_HD


def _ssd_common(x, bm, cm, dt_raw, dt_raw_t, bias_r, bias_c, alog_r, alog_c):
    row = lax.broadcasted_iota(jnp.int32, (CHUNK, CHUNK), 0)
    col = lax.broadcasted_iota(jnp.int32, (CHUNK, CHUNK), 1)
    causal = row >= col
    tril = causal.astype(F32)
    triu = (row <= col).astype(F32)
    spread = _group_matrix(GW, SSD_HD, transpose=True)
    dt = _softplus(dt_raw + bias_r)
    dt_t = _softplus(dt_raw_t + bias_c)
    a_r = -jnp.exp(alog_r)
    a_c = -jnp.exp(alog_c)
    acs = _dotf(tril, dt * a_r, onehot="a", pieces=3)
    acs_t = _dotf(dt_t * a_c, triu, pieces=3)
    last = acs[CHUNK - 1:CHUNK, :]
    ds = jnp.exp(last - acs)
    cd = jnp.exp(last)
    c = dict(causal=causal, tril=tril, triu=triu, spread=spread, dt=dt, a_r=a_r, acs=acs, acs_t=acs_t, ds=ds, cd=cd)
    c["eb"] = _dotf(jnp.exp(acs), spread)
    c["dsb"] = _dotf(ds, spread)
    c["cdb"] = _dotf(cd, spread)
    c["dtb"] = _dotf(dt, spread)
    c["xdt"] = x * c["dtb"]
    c["cb"] = _dotb(cm, bm, NT)
    return c


def _ssd_lam(c, r):
    diff = c["acs"][:, r:r + 1] - c["acs_t"][r:r + 1, :]
    return jnp.exp(jnp.where(c["causal"], diff, -jnp.inf))


GP = 2


def _ssd_specs(nc, rev):
    def ci(t):
        return (nc - 1 - t) if rev else t
    xs = pl.BlockSpec((CHUNK, GP * GW), lambda g, t: (ci(t), g))
    bs = pl.BlockSpec((CHUNK, GP * SSD_N), lambda g, t: (ci(t), SSD_DI // (GP * SSD_N) + g))
    cs = pl.BlockSpec((CHUNK, GP * SSD_N), lambda g, t: (ci(t), (SSD_DI // SSD_N + SSD_G) // GP + g))
    dts = pl.BlockSpec((GP, CHUNK, 8), lambda g, t: (g, ci(t), 0))
    dtts = pl.BlockSpec((GP, 8, CHUNK), lambda g, t: (g, 0, ci(t)))
    pr = pl.BlockSpec((GP, 1, 8), lambda g, t: (g, 0, 0))
    pc = pl.BlockSpec((GP, 8, 1), lambda g, t: (g, 0, 0))
    hs = pl.BlockSpec((1, GP, SSD_N, GW), lambda g, t: (ci(t), g, 0, 0))
    return xs, bs, cs, dts, dtts, pr, pc, hs


def _ssd_fwd(xbc, dtg, dtg_t, bias_r, bias_c, alog_r, alog_c, d_r, *, name, rider=None):
    s = xbc.shape[0]
    nc = s // CHUNK
    xs, bs, cs, dts, dtts, pr, pc, hs = _ssd_specs(nc, False)

    def body(x_ref, b_ref, c_ref, dt_ref, dtt_ref, br_ref, bc_ref, ar_ref, ac_ref, d_ref, y_ref, hp_ref, h_sc):
        t = pl.program_id(1)

        @pl.when(t == 0)
        def _():
            h_sc[...] = jnp.zeros_like(h_sc)

        for gg in range(GP):
            wide, narrow = slice(gg * GW, (gg + 1) * GW), slice(gg * SSD_N, (gg + 1) * SSD_N)
            x, bm, cm = x_ref[:, wide], b_ref[:, narrow], c_ref[:, narrow]
            c = _ssd_common(x, bm, cm, dt_ref[gg], dtt_ref[gg], br_ref[gg], bc_ref[gg], ar_ref[gg], ac_ref[gg])
            h = h_sc[gg]
            hp_ref[0, gg] = h
            xdt = c["xdt"]
            pieces = []
            for r in range(SSD_HPG):
                m = c["cb"] * _ssd_lam(c, r)
                pieces.append(_dotb(m, xdt[:, r * SSD_HD:(r + 1) * SSD_HD]))
            y = jnp.concatenate(pieces, axis=1) + c["eb"] * _dotb(cm, h) + x * _dotf(d_ref[gg], c["spread"])
            y_ref[:, wide] = y
            h_sc[gg] = h * c["cdb"] + _dotb(bm, xdt * c["dsb"], TN)

    outs, riding = _call(
        body, name=name, grid=(SSD_G // GP, nc),
        in_specs=[xs, bs, cs, dts, dtts, pr, pc, pr, pc, pr],
        out_specs=[xs, hs],
        out_shape=[jax.ShapeDtypeStruct((s, SSD_DI), F32), jax.ShapeDtypeStruct((nc, SSD_G, SSD_N, GW), F32)],
        scratch_shapes=[pltpu.VMEM((GP, SSD_N, GW), F32)], sem=("parallel", "arbitrary"), rider=rider,
        args=(xbc, xbc, xbc, dtg, dtg_t, bias_r, bias_c, alog_r, alog_c, d_r))
    return outs + [riding]


def _ssd_bwd(xbc, dtg, dtg_t, bias_r, bias_c, alog_r, alog_c, d_r, hprev, dy, *, name, rider=None):
    s = xbc.shape[0]
    nc = s // CHUNK
    xs, bs, cs, dts, dtts, pr, pc, hs = _ssd_specs(nc, True)
    gsum = functools.partial(_group_matrix, GW, SSD_HD)

    def body(x_ref, b_ref, c_ref, dt_ref, dtt_ref, br_ref, bc_ref, ar_ref, ac_ref, d_ref, hp_ref, dy_ref,
             dx_ref, db_ref, dc_ref, ddt_ref, dbias_ref, dalog_ref, dd_ref, dh_sc):
        t = pl.program_id(1)

        @pl.when(t == 0)
        def _():
            dh_sc[...] = jnp.zeros_like(dh_sc)
            dbias_ref[...] = jnp.zeros_like(dbias_ref)
            dalog_ref[...] = jnp.zeros_like(dalog_ref)
            dd_ref[...] = jnp.zeros_like(dd_ref)

        for gg in range(GP):
            wide, narrow = slice(gg * GW, (gg + 1) * GW), slice(gg * SSD_N, (gg + 1) * SSD_N)
            x, bm, cm = x_ref[:, wide], b_ref[:, narrow], c_ref[:, narrow]
            c = _ssd_common(x, bm, cm, dt_ref[gg], dtt_ref[gg], br_ref[gg], bc_ref[gg], ar_ref[gg], ac_ref[gg])
            lanesum = gsum()
            h = hp_ref[0, gg]
            dh = dh_sc[gg]
            dy = dy_ref[:, wide]
            xdt, dsb = c["xdt"], c["dsb"]
            skip = _dotf(d_ref[gg], c["spread"])
            dd_ref[gg] += jnp.sum(_dotf(dy * x, lanesum), axis=0, keepdims=True)
            dacs = _dotf(dy * (c["eb"] * _dotb(cm, h)), lanesum)
            edy = c["eb"] * dy
            dcm = _dotb(edy, h, NT)
            dh_prev = _dotb(cm, edy, TN)
            bdh = _dotb(bm, dh)
            dxdt = dsb * bdh
            dbm = _dotb(dsb * xdt, dh, NT)
            t1 = _dotf(xdt * bdh, lanesum) * c["ds"]
            dacs = dacs - t1
            dlast = (jnp.sum(t1, axis=0, keepdims=True)
                     + jnp.sum(_dotf(dh * h, lanesum), axis=0, keepdims=True) * c["cd"])
            dcb = jnp.zeros((CHUNK, CHUNK), F32)
            pieces = []
            ones8 = jnp.ones((CHUNK, 8), F32)
            head = lax.broadcasted_iota(jnp.int32, (1, 8), 1)
            for r in range(SSD_HPG):
                sl = slice(r * SSD_HD, (r + 1) * SSD_HD)
                lam = _ssd_lam(c, r)
                m = c["cb"] * lam
                dm = _dotb(dy[:, sl], xdt[:, sl], NT)
                dcb = dcb + dm * lam
                gm = dm * m
                dacs = dacs + ((jnp.sum(gm, axis=1, keepdims=True) - _dotf(gm, ones8, TN, pieces=3))
                               * (head == r).astype(F32))
                pieces.append(_dotb(m, dy[:, sl], TN))
            dxdt = dxdt + jnp.concatenate(pieces, axis=1)
            dcm = dcm + _dotb(dcb, bm)
            dbm = dbm + _dotb(dcb, cm, TN)
            dx_ref[:, wide] = dy * skip + dxdt * c["dtb"]
            db_ref[:, narrow] = dbm
            dc_ref[:, narrow] = dcm
            rowid = lax.broadcasted_iota(jnp.int32, (CHUNK, 8), 0)
            dacs = dacs + jnp.where(rowid == CHUNK - 1, dlast, 0.0)
            dda = _dotf(c["triu"], dacs, onehot="a", pieces=3)
            ddt = _dotf(dxdt * x, lanesum) + dda * c["a_r"]
            ddt_raw = ddt * _sigmoid(dt_ref[gg] + br_ref[gg])
            ddt_ref[gg] = ddt_raw
            dbias_ref[gg] += jnp.sum(ddt_raw, axis=0, keepdims=True)
            dalog_ref[gg] += jnp.sum(dda * c["dt"], axis=0, keepdims=True) * c["a_r"]
            dh_sc[gg] = dh_prev + dh * c["cdb"]

    ci = lambda t: nc - 1 - t
    nspec = pl.BlockSpec((CHUNK, GP * SSD_N), lambda g, t: (ci(t), g))
    outs, riding = _call(
        body, name=name, grid=(SSD_G // GP, nc),
        in_specs=[xs, bs, cs, dts, dtts, pr, pc, pr, pc, pr, hs, xs],
        out_specs=[xs, nspec, nspec, dts, pr, pr, pr],
        out_shape=[jax.ShapeDtypeStruct((s, SSD_DI), F32), jax.ShapeDtypeStruct((s, SSD_G * SSD_N), F32),
                   jax.ShapeDtypeStruct((s, SSD_G * SSD_N), F32), jax.ShapeDtypeStruct((SSD_G, s, 8), F32),
                   jax.ShapeDtypeStruct((SSD_G, 1, 8), F32), jax.ShapeDtypeStruct((SSD_G, 1, 8), F32),
                   jax.ShapeDtypeStruct((SSD_G, 1, 8), F32)],
        scratch_shapes=[pltpu.VMEM((GP, SSD_N, GW), F32)], sem=("parallel", "arbitrary"), rider=rider,
        args=(xbc, xbc, xbc, dtg, dtg_t, bias_r, bias_c, alog_r, alog_c, d_r, hprev, dy))
    return outs + [riding]


FOX_PAIRS = FOX_H // 2
FOX_SCALE = FOX_HD ** -0.5
NEG_INF = -jnp.inf


def _fgate_fwd(f_t, b_c, *, name):
    hh, s = f_t.shape
    tb = _pick(s, 512)
    nb = s // tb

    def body(f_ref, b_ref, o_ref, carry):
        t = pl.program_id(0)

        @pl.when(t == 0)
        def _():
            carry[...] = jnp.zeros_like(carry)

        lf = -_softplus(-(f_ref[...] + b_ref[...]))
        row = lax.broadcasted_iota(jnp.int32, (tb, tb), 0)
        col = lax.broadcasted_iota(jnp.int32, (tb, tb), 1)
        cum = _dotf(lf, (row <= col).astype(F32), pieces=3) + carry[:, 0:1]
        o_ref[...] = cum
        carry[:, 0:1] = cum[:, tb - 1:tb]

    return pl.pallas_call(
        body, name=name, grid=(nb,),
        in_specs=[pl.BlockSpec((hh, tb), lambda t: (0, t)), pl.BlockSpec((hh, 1), lambda t: (0, 0))],
        out_specs=pl.BlockSpec((hh, tb), lambda t: (0, t)),
        out_shape=jax.ShapeDtypeStruct((hh, s), F32),
        scratch_shapes=[pltpu.VMEM((hh, LANES), F32)],
        compiler_params=_cp(("arbitrary",)),
    )(f_t, b_c)


def _fgate_bwd(dcum_q_t, dcum_k_t, f_t, b_c, *, name):
    hh, s = f_t.shape
    tb = _pick(s, 512)
    nb = s // tb

    def body(dq_ref, d_ref, f_ref, b_ref, df_ref, db_ref, carry):
        t = pl.program_id(0)

        @pl.when(t == 0)
        def _():
            carry[...] = jnp.zeros_like(carry)
            db_ref[...] = jnp.zeros_like(db_ref)

        d = d_ref[...] + dq_ref[...]
        row = lax.broadcasted_iota(jnp.int32, (tb, tb), 0)
        col = lax.broadcasted_iota(jnp.int32, (tb, tb), 1)
        rev = _dotf(d, (row >= col).astype(F32), pieces=3) + carry[:, 0:1]
        df = rev * _sigmoid(-(f_ref[...] + b_ref[...]))
        df_ref[...] = df
        db_ref[...] += jnp.sum(df, axis=1, keepdims=True)
        carry[:, 0:1] = rev[:, 0:1]

    blk = pl.BlockSpec((hh, tb), lambda t: (0, nb - 1 - t))
    return pl.pallas_call(
        body, name=name, grid=(nb,),
        in_specs=[blk, blk, blk, pl.BlockSpec((hh, 1), lambda t: (0, 0))],
        out_specs=[blk, pl.BlockSpec((hh, 1), lambda t: (0, 0))],
        out_shape=[jax.ShapeDtypeStruct((hh, s), F32), jax.ShapeDtypeStruct((hh, 1), F32)],
        scratch_shapes=[pltpu.VMEM((hh, LANES), F32)],
        compiler_params=_cp(("arbitrary",)),
    )(dcum_q_t, dcum_k_t, f_t, b_c)


def _fox_tile(s):
    return min(512, max(s // 2, 8))


def _tri_tables(nq, kv_major):
    if kv_major:
        pairs = [(i, j) for j in range(nq) for i in range(j, nq)]
    else:
        pairs = [(i, j) for i in range(nq) for j in range(i + 1)]
    return (jnp.asarray([p[0] for p in pairs], jnp.int32), jnp.asarray([p[1] for p in pairs], jnp.int32))


def _lane_tile(col, width):
    return col if width == LANES else jnp.tile(col, (1, width // LANES))


def _flash_fwd(qs, kn, qkvg, ck, *, name, rider=None):
    s = qs.shape[0]
    tt = _fox_tile(s)
    nq = s // tt
    itab, jtab = _tri_tables(nq, kv_major=False)
    v0 = 2 * FOX_D // LANES

    def body(itab_ref, jtab_ref, q_ref, k_ref, v_ref, ck_ref, o_ref, lse_ref, m_sc, l_sc, acc_sc):
        t = pl.program_id(1)
        i, j = itab_ref[t], jtab_ref[t]

        @pl.when(j == 0)
        def _():
            m_sc[...] = jnp.full_like(m_sc, NEG_INF)
            l_sc[...] = jnp.zeros_like(l_sc)
            acc_sc[...] = jnp.zeros_like(acc_sc)

        low = lax.broadcasted_iota(jnp.int32, (tt, LANES), 1) < FOX_HD

        def step(diagonal):
            q2, k2 = q_ref[...], k_ref[...]
            v2 = v_ref[...].astype(BF16)
            alphas, outs = [], []
            for hh in range(2):
                qh = jnp.where(low if hh == 0 else jnp.logical_not(low), q2, jnp.zeros_like(q2))
                sc = lax.dot_general(qh, k2, NT, preferred_element_type=F32) - ck_ref[0][hh:hh + 1, :]
                if diagonal:
                    row = lax.broadcasted_iota(jnp.int32, sc.shape, 0)
                    col = lax.broadcasted_iota(jnp.int32, sc.shape, 1)
                    sc = jnp.where(row >= col, sc, NEG_INF)
                m_prev = m_sc[hh]
                m_new = jnp.maximum(m_prev, jnp.max(sc, axis=1, keepdims=True))
                alpha = jnp.exp(m_prev - m_new)
                p = jnp.exp(sc - _lane_tile(m_new, tt))
                l_sc[hh] = alpha * l_sc[hh] + jnp.sum(p, axis=1, keepdims=True)
                m_sc[hh] = m_new
                alphas.append(alpha)
                outs.append(lax.dot_general(p.astype(BF16), v2, NN, preferred_element_type=F32))
            acc_sc[...] = jnp.where(low, alphas[0], alphas[1]) * acc_sc[...] + jnp.where(low, outs[0], outs[1])

        @pl.when(j < i)
        def _():
            step(False)

        @pl.when(j == i)
        def _():
            step(True)
            o_ref[...] = acc_sc[...] / jnp.where(low, l_sc[0], l_sc[1])
            lse_ref[0] = jnp.concatenate([m_sc[hh][:, 0:1] + jnp.log(l_sc[hh][:, 0:1]) for hh in range(2)], axis=1)

    outs, riding = _call(
        body, name=name, grid=(FOX_PAIRS, int(itab.shape[0])), prefetch=(itab, jtab),
        in_specs=[pl.BlockSpec((tt, LANES), lambda p, t, it, jt: (it[t], p)),
                  pl.BlockSpec((tt, LANES), lambda p, t, it, jt: (jt[t], p)),
                  pl.BlockSpec((tt, LANES), lambda p, t, it, jt: (jt[t], v0 + p)),
                  pl.BlockSpec((1, 2, tt), lambda p, t, it, jt: (p, 0, jt[t]))],
        out_specs=[pl.BlockSpec((tt, LANES), lambda p, t, it, jt: (it[t], p)),
                   pl.BlockSpec((1, tt, 2), lambda p, t, it, jt: (p, it[t], 0))],
        scratch_shapes=[pltpu.VMEM((2, tt, LANES), F32), pltpu.VMEM((2, tt, LANES), F32), pltpu.VMEM((tt, LANES), F32)],
        out_shape=[jax.ShapeDtypeStruct((s, FOX_D), F32), jax.ShapeDtypeStruct((FOX_PAIRS, s, 2), F32)],
        sem=("parallel", "arbitrary"), rider=rider, args=(qs, kn, qkvg, ck))
    return outs + [riding]


def _flash_bwd(qs, kn, qkvg, do, lse_t, delta_t, ck_c, *, name, rider=None):
    s = qs.shape[0]
    tt = _fox_tile(s)
    nq = s // tt
    nl = tt // LANES
    itab, jtab = _tri_tables(nq, kv_major=True)
    nsteps = itab.shape[0]
    v0 = 2 * FOX_D // LANES

    def body(itab_ref, jtab_ref, q_ref, k_ref, v_ref, do_ref, lse_ref, dl_ref, ck_ref,
             dq_ref, dk_ref, dv_ref, dcq_ref, dck_ref, dqt_sc, rs_sc, dk_sc, dv_sc, dc_sc, kt_sc, ckb_sc):
        t = pl.program_id(1)
        i, j = itab_ref[t], jtab_ref[t]

        @pl.when(t == 0)
        def _():
            dqt_sc[...] = jnp.zeros_like(dqt_sc)
            rs_sc[...] = jnp.zeros_like(rs_sc)

        @pl.when(i == j)
        def _():
            dk_sc[...] = jnp.zeros_like(dk_sc)
            dv_sc[...] = jnp.zeros_like(dv_sc)
            dc_sc[...] = jnp.zeros_like(dc_sc)
            kt_sc[...] = k_ref[...].astype(F32).T.astype(BF16)
            for hh in range(2):
                ckb_sc[hh] = jnp.broadcast_to(ck_ref[0][:, hh:hh + 1], (tt, LANES))

        low = lax.broadcasted_iota(jnp.int32, (tt, LANES), 1) < FOX_HD
        top = lax.broadcasted_iota(jnp.int32, (LANES, tt), 0) < FOX_HD

        def step(diagonal):
            q2, k2, kt = q_ref[...], k_ref[...], kt_sc[...]
            v2 = v_ref[...].astype(BF16)
            do2 = do_ref[...].astype(BF16)
            dqs, dks, dvs = [], [], []
            for hh in range(2):
                sel = low if hh == 0 else jnp.logical_not(low)
                qh = jnp.where(sel, q2, jnp.zeros_like(q2))
                doh = jnp.where(sel, do2, jnp.zeros_like(do2))
                st = lax.dot_general(k2, qh, NT, preferred_element_type=F32)
                st = st - _lane_tile(ckb_sc[hh], tt) - lse_ref[0][hh:hh + 1, :]
                if diagonal:
                    key = lax.broadcasted_iota(jnp.int32, st.shape, 0)
                    qry = lax.broadcasted_iota(jnp.int32, st.shape, 1)
                    st = jnp.where(qry >= key, st, NEG_INF)
                pt = jnp.exp(st)
                dpt = lax.dot_general(v2, doh, NT, preferred_element_type=F32)
                dst = pt * (dpt - dl_ref[0][hh:hh + 1, :])
                ptb, dstb = pt.astype(BF16), dst.astype(BF16)
                dvs.append(lax.dot_general(ptb, do2, NN, preferred_element_type=F32))
                dks.append(lax.dot_general(dstb, q2, NN, preferred_element_type=F32))
                dqs.append(lax.dot_general(kt, dstb, NN, preferred_element_type=F32))
                rs_sc[hh, i] += jnp.sum(dst, axis=0, keepdims=True)
                part = dst[:, 0:LANES]
                for b in range(1, nl):
                    part = part + dst[:, b * LANES:(b + 1) * LANES]
                dc_sc[hh] += part
            dv_sc[...] += jnp.where(low, dvs[0], dvs[1])
            dk_sc[...] += jnp.where(low, dks[0], dks[1])
            dqt_sc[i] += jnp.where(top, dqs[0], dqs[1])

        @pl.when(j < i)
        def _():
            step(False)

        @pl.when(j == i)
        def _():
            step(True)

        @pl.when(i == nq - 1)
        def _():
            dk_ref[...] = dk_sc[...]
            dv_ref[...] = dv_sc[...]
            dck_ref[0] = -jnp.concatenate([jnp.sum(dc_sc[hh], axis=1, keepdims=True) for hh in range(2)], axis=1)

        @pl.when(t == nsteps - 1)
        def _():
            for b in range(nq):
                dq_ref[pl.ds(b * tt, tt), :] = dqt_sc[b].T * FOX_SCALE
                dcq_ref[0, :, pl.ds(b * tt, tt)] = jnp.concatenate([rs_sc[hh, b] for hh in range(2)], axis=0)

    qside = pl.BlockSpec((tt, LANES), lambda p, t, it, jt: (it[t], p))
    kside = pl.BlockSpec((tt, LANES), lambda p, t, it, jt: (jt[t], p))
    qstat = pl.BlockSpec((1, 2, tt), lambda p, t, it, jt: (p, 0, it[t]))
    kstat = pl.BlockSpec((1, tt, 2), lambda p, t, it, jt: (p, jt[t], 0))
    outs, riding = _call(
        body, name=name, grid=(FOX_PAIRS, nsteps), prefetch=(itab, jtab),
        in_specs=[qside, kside, pl.BlockSpec((tt, LANES), lambda p, t, it, jt: (jt[t], v0 + p)), qside, qstat, qstat, kstat],
        out_specs=[pl.BlockSpec((s, LANES), lambda p, t, it, jt: (0, p)), kside, kside,
                   pl.BlockSpec((1, 2, s), lambda p, t, it, jt: (p, 0, 0)), kstat],
        scratch_shapes=[pltpu.VMEM((nq, LANES, tt), F32), pltpu.VMEM((2, nq, 1, tt), F32), pltpu.VMEM((tt, LANES), F32),
                        pltpu.VMEM((tt, LANES), F32), pltpu.VMEM((2, tt, LANES), F32), pltpu.VMEM((LANES, tt), BF16),
                        pltpu.VMEM((2, tt, LANES), F32)],
        out_shape=[jax.ShapeDtypeStruct((s, FOX_D), F32), jax.ShapeDtypeStruct((s, FOX_D), F32),
                   jax.ShapeDtypeStruct((s, FOX_D), F32), jax.ShapeDtypeStruct((FOX_PAIRS, 2, s), F32),
                   jax.ShapeDtypeStruct((FOX_PAIRS, s, 2), F32)],
        sem=("parallel", "arbitrary"), rider=rider, args=(qs, kn, qkvg, do, lse_t, delta_t, ck_c))
    return outs + [riding]


def _ogate_fwd(o, qkvg, *, name):
    s = o.shape[0]
    tr = _pick(s, 512, 8)

    def body(o_ref, g_ref, out_ref):
        out_ref[...] = (o_ref[...] * _sigmoid(g_ref[...])).astype(BF16)

    tile = pl.BlockSpec((tr, FOX_D), lambda i: (i, 0))
    return pl.pallas_call(
        body, name=name, grid=(s // tr,), in_specs=[tile, pl.BlockSpec((tr, FOX_D), lambda i: (i, 3))],
        out_specs=tile, out_shape=jax.ShapeDtypeStruct((s, FOX_D), BF16), compiler_params=_cp(("parallel",)),
    )(o, qkvg)


def _ogate_bwd(dog, o, qkvg, *, name):
    s = o.shape[0]
    tr = _pick(s, 512, 8)

    def body(dog_ref, o_ref, g_ref, do_ref, dg_ref, dl_ref):
        sg = _sigmoid(g_ref[...])
        ov = o_ref[...]
        dog_v = dog_ref[...]
        do = dog_v * sg
        do_ref[...] = do
        dg_ref[...] = (dog_v * ov * sg * (1.0 - sg)).astype(BF16)
        dl_ref[...] = _dotf(do * ov, _group_matrix(FOX_D, FOX_HD))

    tile = pl.BlockSpec((tr, FOX_D), lambda i: (i, 0))
    return pl.pallas_call(
        body, name=name, grid=(s // tr,), in_specs=[tile, tile, pl.BlockSpec((tr, FOX_D), lambda i: (i, 3))],
        out_specs=[tile, tile, pl.BlockSpec((tr, FOX_H), lambda i: (i, 0))],
        out_shape=[jax.ShapeDtypeStruct((s, FOX_D), F32), jax.ShapeDtypeStruct((s, FOX_D), BF16),
                   jax.ShapeDtypeStruct((s, FOX_H), F32)],
        compiler_params=_cp(("parallel",)),
    )(dog, o, qkvg)


def _loss_head(h, g, target, *, name):
    s, d = h.shape
    tr = _pick(s, 512, 8)

    def body(h_ref, g_ref, t_ref, loss_ref, dh_ref, dg_ref):
        i = pl.program_id(0)
        x = h_ref[...]
        gv = g_ref[...]
        r = lax.rsqrt(jnp.mean(x * x, axis=-1, keepdims=True) + EPS)
        xh = x * r
        err = xh * gv - t_ref[...]
        part = 0.5 * jnp.sum(jnp.sum(err * err, axis=1, keepdims=True) * (1.0 / d), axis=0, keepdims=True)
        dy = err * (1.0 / d)
        dyg = dy * gv
        dh_ref[...] = r * (dyg - xh * jnp.mean(dyg * xh, axis=-1, keepdims=True))
        dgp = jnp.sum(dy * xh, axis=0, keepdims=True)

        @pl.when(i == 0)
        def _():
            loss_ref[...] = jnp.zeros_like(loss_ref) + part
            dg_ref[...] = dgp

        @pl.when(i > 0)
        def _():
            loss_ref[...] += part
            dg_ref[...] += dgp

    tile = pl.BlockSpec((tr, d), lambda i: (i, 0))
    vec = pl.BlockSpec((1, d), lambda i: (0, 0))
    return pl.pallas_call(
        body, name=name, grid=(s // tr,), in_specs=[tile, vec, tile],
        out_specs=[pl.BlockSpec((1, LANES), lambda i: (0, 0)), tile, vec],
        out_shape=[jax.ShapeDtypeStruct((1, LANES), F32), jax.ShapeDtypeStruct((s, d), F32),
                   jax.ShapeDtypeStruct((1, d), F32)],
        compiler_params=_cp(("arbitrary",)),
    )(h, g, target)


def _adamw(w, g, m, v, *, name):
    rows, cols = w.shape
    tr = _pick(rows, 256, 8)
    c1 = 1.0 - ADAM_B1 ** ADAM_STEP
    c2 = 1.0 - ADAM_B2 ** ADAM_STEP

    def body(w_ref, g_ref, m_ref, v_ref, d_ref, nm_ref, nv_ref):
        gv = g_ref[...]
        nm = ADAM_B1 * m_ref[...] + (1.0 - ADAM_B1) * gv
        nv = ADAM_B2 * v_ref[...] + (1.0 - ADAM_B2) * (gv * gv)
        d_ref[...] = -ADAM_LR * ((nm / c1) / (jnp.sqrt(nv / c2) + ADAM_EPS) + ADAM_WD * w_ref[...])
        nm_ref[...] = nm
        nv_ref[...] = nv

    tile = pl.BlockSpec((tr, cols), lambda i: (i, 0))
    shp = jax.ShapeDtypeStruct((rows, cols), F32)
    return pl.pallas_call(
        body, name=name, grid=(rows // tr,), in_specs=[tile] * 4, out_specs=[tile] * 3, out_shape=[shp] * 3,
        compiler_params=_cp(("parallel",)),
    )(w, g, m, v)


ANY = pl.BlockSpec(memory_space=pl.ANY)
N_DEV = 8


def _coords():
    return lax.axis_index("x"), lax.axis_index("y"), lax.axis_index("c")


def _other_chips(x, y):
    return [(1 - x, y), (x, 1 - y), (1 - x, 1 - y)]


def _allgather_small(buf, *, name, with_sum):
    rows = buf.shape[0]

    def body(*refs):
        if with_sum:
            x_ref, out_ref, sum_ref, send_sems, recv_sems = refs
        else:
            x_ref, out_ref, send_sems, recv_sems = refs
        x, y, c = _coords()
        me = 4 * x + 2 * y + c
        out_ref[me] = x_ref[...]
        copies = []
        for rel in range(1, N_DEV):
            px = (1 - x) if rel & 4 else x
            py = (1 - y) if rel & 2 else y
            pc = (1 - c) if rel & 1 else c
            cp = pltpu.make_async_remote_copy(
                src_ref=x_ref, dst_ref=out_ref.at[me], send_sem=send_sems.at[rel - 1], recv_sem=recv_sems.at[rel - 1],
                device_id=(px, py, pc), device_id_type=MESH)
            cp.start()
            copies.append(cp)
        for cp in copies:
            cp.wait()
        if with_sum:
            acc = out_ref[0]
            for k in range(1, N_DEV):
                acc = acc + out_ref[k]
            sum_ref[...] = acc

    slots = jax.ShapeDtypeStruct((N_DEV, rows, LANES), F32)
    vm = pl.BlockSpec(memory_space=pltpu.VMEM)
    out_shape = [slots, jax.ShapeDtypeStruct((rows, LANES), F32)] if with_sum else [slots]
    return pl.pallas_call(
        body, name=name, in_specs=[vm], out_specs=[vm] * len(out_shape), out_shape=out_shape,
        scratch_shapes=[pltpu.SemaphoreType.DMA((N_DEV - 1,)), pltpu.SemaphoreType.DMA((N_DEV - 1,))],
    )(buf)


class _Gather:
    per_array = 6

    def __init__(self, arrays):
        self.arrays = list(arrays)

    def out_shapes(self):
        return [jax.ShapeDtypeStruct((4,) + a.shape, a.dtype) for a in self.arrays]

    @staticmethod
    def _ici(ins, outs, send_sems, recv_sems, t, j, px, py, c, slot):
        return pltpu.make_async_remote_copy(
            src_ref=ins[t].at[c], dst_ref=outs[t].at[slot, c], send_sem=send_sems.at[6 * t + j],
            recv_sem=recv_sems.at[6 * t + j], device_id=(px, py, c), device_id_type=MESH)

    @staticmethod
    def _d2d(outs, send_sems, recv_sems, t, j, kj, half, sibling):
        return pltpu.make_async_remote_copy(
            src_ref=outs[t].at[kj, half], dst_ref=outs[t].at[kj, half], send_sem=send_sems.at[6 * t + 3 + j],
            recv_sem=recv_sems.at[6 * t + 3 + j], device_id=sibling, device_id_type=MESH)

    def start(self, ins, outs, send_sems, recv_sems):
        x, y, c = _coords()
        for t in range(len(ins)):
            for j, (px, py) in enumerate(_other_chips(x, y)):
                self._ici(ins, outs, send_sems, recv_sems, t, j, px, py, c, 2 * x + y).start()

    def finish(self, ins, outs, send_sems, recv_sems):
        x, y, c = _coords()
        chips = _other_chips(x, y)
        sibling = (x, y, 1 - c)
        started = []
        for t in range(len(ins)):
            for j, (px, py) in enumerate(chips):
                ici = self._ici(ins, outs, send_sems, recv_sems, t, j, px, py, c, 2 * px + py)
                ici.wait_recv()
                fwd = self._d2d(outs, send_sems, recv_sems, t, j, 2 * px + py, c, sibling)
                fwd.start()
                started += [ici, fwd]
        for t in range(len(ins)):
            for j, (px, py) in enumerate(chips):
                self._d2d(outs, send_sems, recv_sems, t, j, 2 * px + py, 1 - c, sibling).wait_recv()
        for cp in started:
            cp.wait_send()


class _Exchange:
    per_array = 7

    def __init__(self, arrays):
        self.arrays = list(arrays)

    def out_shapes(self):
        return [jax.ShapeDtypeStruct((7,) + a.shape[2:], a.dtype) for a in self.arrays]

    @staticmethod
    def _copies(ins, outs, send_sems, recv_sems):
        x, y, c = _coords()
        for t in range(len(ins)):
            for rel in range(1, N_DEV):
                px = (1 - x) if rel & 4 else x
                py = (1 - y) if rel & 2 else y
                pc = (1 - c) if rel & 1 else c
                yield pltpu.make_async_remote_copy(
                    src_ref=ins[t].at[2 * px + py, pc], dst_ref=outs[t].at[rel - 1], send_sem=send_sems.at[7 * t + rel - 1],
                    recv_sem=recv_sems.at[7 * t + rel - 1], device_id=(px, py, pc), device_id_type=MESH)

    def start(self, ins, outs, send_sems, recv_sems):
        for cp in self._copies(ins, outs, send_sems, recv_sems):
            cp.start()

    def finish(self, ins, outs, send_sems, recv_sems):
        for cp in self._copies(ins, outs, send_sems, recv_sems):
            cp.wait()


def _call(body, *, name, grid, in_specs, out_specs, out_shape, scratch_shapes, args, sem, rider=None, prefetch=()):
    n_in, n_out, n_pre = len(in_specs), len(out_specs), len(prefetch)
    n_c = len(rider.arrays) if rider is not None else 0

    def wrapped(*refs):
        pre, rest = refs[:n_pre], refs[n_pre:]
        ins, cins = rest[:n_in], rest[n_in:n_in + n_c]
        outs = rest[n_in + n_c:n_in + n_c + n_out]
        couts = rest[n_in + n_c + n_out:n_in + 2 * n_c + n_out]
        scratch = rest[n_in + 2 * n_c + n_out:]
        if rider is None:
            body(*pre, *ins, *outs, *scratch)
            return
        send_sems, recv_sems = scratch[-2:]
        ids = [pl.program_id(a) for a in range(len(grid))]
        first = functools.reduce(jnp.logical_and, [i == 0 for i in ids])
        last = functools.reduce(jnp.logical_and, [i == g - 1 for i, g in zip(ids, grid)])

        @pl.when(first)
        def _():
            rider.start(cins, couts, send_sems, recv_sems)

        body(*pre, *ins, *outs, *scratch[:-2])

        @pl.when(last)
        def _():
            rider.finish(cins, couts, send_sems, recv_sems)

    if rider is not None:
        nsem = rider.per_array * n_c
        in_specs = list(in_specs) + [ANY] * n_c
        out_specs = list(out_specs) + [ANY] * n_c
        out_shape = list(out_shape) + rider.out_shapes()
        scratch_shapes = list(scratch_shapes) + [pltpu.SemaphoreType.DMA((nsem,)), pltpu.SemaphoreType.DMA((nsem,))]
        args = list(args) + rider.arrays
        sem = ("arbitrary",) * len(grid)
    if n_pre:
        res = pl.pallas_call(
            wrapped, name=name, out_shape=out_shape, compiler_params=_cp(sem),
            grid_spec=pltpu.PrefetchScalarGridSpec(num_scalar_prefetch=n_pre, grid=grid, in_specs=in_specs,
                                                   out_specs=out_specs, scratch_shapes=scratch_shapes),
        )(*prefetch, *args)
    else:
        res = pl.pallas_call(
            wrapped, name=name, grid=grid, in_specs=in_specs, out_specs=out_specs, out_shape=out_shape,
            scratch_shapes=scratch_shapes, compiler_params=_cp(sem),
        )(*args)
    return list(res[:n_out]), list(res[n_out:])


def _run_rider(rider, *, name):
    n = len(rider.arrays)

    def body(*refs):
        ins, outs = refs[:n], refs[n:2 * n]
        send_sems, recv_sems = refs[2 * n:]
        rider.start(ins, outs, send_sems, recv_sems)
        rider.finish(ins, outs, send_sems, recv_sems)

    nsem = rider.per_array * n
    return pl.pallas_call(
        body, name=name, in_specs=[ANY] * n, out_specs=[ANY] * n, out_shape=rider.out_shapes(),
        scratch_shapes=[pltpu.SemaphoreType.DMA((nsem,)), pltpu.SemaphoreType.DMA((nsem,))],
    )(*rider.arrays)


def _sibling_swap(arrs, *, name):
    n = len(arrs)

    def body(*refs):
        ins, outs = refs[:n], refs[n:2 * n]
        send_sems, recv_sems = refs[2 * n:]
        x, y, c = _coords()
        copies = []
        for t in range(n):
            cp = pltpu.make_async_remote_copy(
                src_ref=ins[t], dst_ref=outs[t], send_sem=send_sems.at[t], recv_sem=recv_sems.at[t],
                device_id=(x, y, 1 - c), device_id_type=MESH)
            cp.start()
            copies.append(cp)
        for cp in copies:
            cp.wait()

    return pl.pallas_call(
        body, name=name, in_specs=[ANY] * n, out_specs=[ANY] * n,
        out_shape=[jax.ShapeDtypeStruct(a.shape, a.dtype) for a in arrs],
        scratch_shapes=[pltpu.SemaphoreType.DMA((n,)), pltpu.SemaphoreType.DMA((n,))],
    )(*arrs)


def _add_selected(stack, others, sel, *, name):
    _, m, cols = stack.shape
    q = others.shape[0]
    tr = _pick(m, 256, 16)

    def body(sel_ref, s_ref, o_ref, out_ref):
        acc = s_ref[0].astype(F32)
        for i in range(q):
            acc = acc + o_ref[i].astype(F32)
        out_ref[...] = acc

    return pl.pallas_call(
        body, name=name,
        grid_spec=pltpu.PrefetchScalarGridSpec(
            num_scalar_prefetch=1, grid=(m // tr,),
            in_specs=[pl.BlockSpec((1, tr, cols), lambda i, sel_ref: (sel_ref[0], i, 0)),
                      pl.BlockSpec((q, tr, cols), lambda i, sel_ref: (0, i, 0))],
            out_specs=pl.BlockSpec((tr, cols), lambda i, sel_ref: (i, 0))),
        out_shape=jax.ShapeDtypeStruct((m, cols), F32),
        compiler_params=_cp(("parallel",)),
    )(sel, stack, others)


BIG = ("ssd_w_in", "ssd_w_out", "fox_w_in", "fox_w_out", "ffn_w_up", "ffn_w_down")
COL_SHARDED = ("ssd_w_in", "fox_w_in", "ffn_w_up")
SMALL = (("mix_norm_g", (4, 1024)), ("ffn_norm_g", (4, 1024)), ("ssd_conv_w", (2, 4, 3072)), ("ssd_conv_b", (2, 3072)),
         ("ssd_dt_bias", (2, 32)), ("ssd_a_log", (2, 32)), ("ssd_d", (2, 32)), ("ssd_norm_g", (2, 2048)),
         ("fox_b_f", (2, 16)), ("fox_q_norm_g", (2, 64)), ("fox_k_norm_g", (2, 64)), ("ffn_conv_w", (4, 3, 2816)),
         ("ffn_conv_b", (4, 2816)), ("final_norm_g", (1024,)), ("loss", (1,)))
NAMES = ("mix_norm_g", "ffn_norm_g", "ssd_w_in", "ssd_conv_w", "ssd_conv_b", "ssd_dt_bias", "ssd_a_log", "ssd_d",
         "ssd_norm_g", "ssd_w_out", "fox_w_in", "fox_b_f", "fox_q_norm_g", "fox_k_norm_g", "fox_w_out", "ffn_w_up",
         "ffn_conv_w", "ffn_conv_b", "ffn_w_down", "final_norm_g")


def _pack(parts):
    flat = jnp.concatenate([jnp.reshape(p, (-1,)).astype(F32) for p in parts])
    rows = -(-flat.shape[0] // (8 * LANES)) * 8
    return jnp.pad(flat, (0, rows * LANES - flat.shape[0])).reshape(rows, LANES)


def _unpack(buf, shapes):
    flat = buf.reshape(-1)
    out, off = [], 0
    for shp in shapes:
        size = 1
        for d in shp:
            size *= d
        out.append(flat[off:off + size].reshape(shp))
        off += size
    return out


def _pad_lanes(a):
    return jnp.pad(a, ((0, 0), (0, LANES - a.shape[1])))


def _pad8(w):
    return jnp.pad(w, ((0, 8 - w.shape[0]), (0, 0)))


def _ssd_forward(h, p, name, rider=None):
    s = h.shape[0]
    hn = _rms_fwd(h, p["mix_g"], gw=D_MODEL, ncol=1, name=f"{name}_norm")
    zx = _matmul(hn, p["w_zx"], mode="nn", name=f"{name}_proj")
    dtp = _matmul(hn, p["w_dt"], mode="nn", name=f"{name}_proj_dt")
    xbc, _ = _conv_fwd(zx, p["conv_w8"], p["conv_b"], kw=SSD_K, width=SSD_CONV_DIM, u_col0=SSD_DI, name=f"{name}_conv")
    dt3 = dtp[:, :SSD_H].reshape(s, SSD_G, SSD_HPG)
    dtg, dtg_t = jnp.transpose(dt3, (1, 0, 2)), jnp.transpose(dt3, (1, 2, 0))
    sp = (p["bias_r"], p["bias_c"], p["alog_r"], p["alog_c"], p["d_r"])
    y, hprev, riding = _ssd_fwd(xbc, dtg, dtg_t, *sp, name=f"{name}_scan", rider=rider)
    y2 = _rms_fwd(y, p["norm_g"], gw=SSD_DI // SSD_G, ncol=SSD_G, z=zx, name=f"{name}_gnorm")
    out = _matmul(y2, p["w_out"], mode="nn", add=h, name=f"{name}_out")
    return out, dict(h=h, hn=hn, zx=zx, xbc=xbc, dtg=dtg, dtg_t=dtg_t, y=y, hprev=hprev, y2=y2), riding


def _ssd_backward(dh1, p, a, name, ride=()):
    s = dh1.shape[0]
    g = {}
    dy2 = _matmul(dh1, p["w_out"], mode="nt", name=f"{name}_out_dx")
    g["w_out"] = _matmul(a["y2"], dh1, mode="tn", out_dtype=BF16, name=f"{name}_out_dw")
    rider = _Exchange(list(ride) + [_to_slabs(g["w_out"], False)])
    dy, dz, g["norm_g"] = _rms_bwd(a["y"], p["norm_g"], dy2, gw=SSD_DI // SSD_G, ncol=SSD_G, z=a["zx"], name=f"{name}_gnorm_b")
    sp = (p["bias_r"], p["bias_c"], p["alog_r"], p["alog_c"], p["d_r"])
    dx, dbm, dcm, ddt, g["dt_bias"], g["a_log"], g["d"], riding = _ssd_bwd(
        a["xbc"], a["dtg"], a["dtg_t"], *sp, a["hprev"], dy, name=f"{name}_scan_b", rider=rider)
    dact = jnp.concatenate([dx, dbm, dcm], axis=1)
    dxbc, dwb, _ = _conv_bwd(a["zx"], p["conv_w8"], p["conv_b"], dact, kw=SSD_K, width=SSD_CONV_DIM, u_col0=SSD_DI,
                             name=f"{name}_conv_b")
    g["conv_w"], g["conv_b"] = dwb[:SSD_K], dwb[7]
    dzx = jnp.concatenate([dz.astype(BF16), dxbc], axis=1)
    ddtp = _pad_lanes(jnp.transpose(ddt, (1, 0, 2)).reshape(s, SSD_H))
    dhn = _matmul(dzx, p["w_zx"], mode="nt", name=f"{name}_proj_dx")
    dhn = _matmul(ddtp, p["w_dt"], mode="nt", add=dhn, name=f"{name}_proj_dt_dx")
    dw_zx = _matmul(a["hn"], dzx, mode="tn", out_dtype=BF16, name=f"{name}_proj_dw")
    dw_dt = _matmul(a["hn"], ddtp, mode="tn", out_dtype=BF16, name=f"{name}_proj_dt_dw")
    g["w_in"] = jnp.concatenate([dw_zx, dw_dt[:, :SSD_H]], axis=1)
    dh, g["mix_g"] = _rms_bwd(a["h"], p["mix_g"], dhn, gw=D_MODEL, ncol=1, add=dh1, name=f"{name}_norm_b")
    g["w_out_received"] = riding[-1]
    return dh, g, riding[:-1]


def _fox_forward(h, p, name, rider=None):
    s = h.shape[0]
    hn = _rms_fwd(h, p["mix_g"], gw=D_MODEL, ncol=1, name=f"{name}_norm")
    qkvg = _matmul(hn, p["w_qkvg"], mode="nn", name=f"{name}_proj")
    fp = _matmul(hn, p["w_f"], mode="nn", name=f"{name}_proj_f")
    qs = _rms_fwd(qkvg, p["gq"] * FOX_SCALE, gw=FOX_D, ncol=1, x_col0=0, sub=FOX_HD, name=f"{name}_qnorm")
    kn = _rms_fwd(qkvg, p["gk"], gw=FOX_D, ncol=1, x_col0=1, sub=FOX_HD, name=f"{name}_knorm")
    f_t = jnp.transpose(fp[:, :FOX_H])
    cum_t = _fgate_fwd(f_t, p["b_f"], name=f"{name}_fgate")
    ck = cum_t.reshape(FOX_PAIRS, 2, s)
    o, lse, riding = _flash_fwd(qs, kn, qkvg, ck, name=f"{name}_attn", rider=rider)
    og = _ogate_fwd(o, qkvg, name=f"{name}_ogate")
    out = _matmul(og, p["w_out"], mode="nn", add=h, name=f"{name}_out")
    return out, dict(h=h, hn=hn, qkvg=qkvg, qs=qs, kn=kn, f_t=f_t, ck=ck, o=o, lse=lse, og=og), riding


def _fox_backward(dh1, p, a, name, ride=()):
    s = dh1.shape[0]
    g = {}
    dog = _matmul(dh1, p["w_out"], mode="nt", name=f"{name}_out_dx")
    g["w_out"] = _matmul(a["og"], dh1, mode="tn", out_dtype=BF16, name=f"{name}_out_dw")
    rider = _Exchange(list(ride) + [_to_slabs(g["w_out"], False)])
    do, dgate, delta = _ogate_bwd(dog, a["o"], a["qkvg"], name=f"{name}_ogate_b")
    swap = lambda v: jnp.transpose(v, (0, 2, 1))
    dl_t = jnp.transpose(delta.reshape(s, FOX_PAIRS, 2), (1, 2, 0))
    dq, dk, dv, dcq, dck, riding = _flash_bwd(a["qs"], a["kn"], a["qkvg"], do, swap(a["lse"]), dl_t, swap(a["ck"]),
                                              name=f"{name}_attn_b", rider=rider)
    dq_raw, dgq = _rms_bwd(a["qkvg"], p["gq"], dq, gw=FOX_D, ncol=1, x_col0=0, sub=FOX_HD, dx_dtype=BF16, name=f"{name}_qnorm_b")
    dk_raw, dgk = _rms_bwd(a["qkvg"], p["gk"], dk, gw=FOX_D, ncol=1, x_col0=1, sub=FOX_HD, dx_dtype=BF16, name=f"{name}_knorm_b")
    g["gq"] = dgq.reshape(FOX_H, FOX_HD).sum(axis=0)
    g["gk"] = dgk.reshape(FOX_H, FOX_HD).sum(axis=0)
    df_t, dbf = _fgate_bwd(dcq.reshape(FOX_H, s), swap(dck).reshape(FOX_H, s), a["f_t"], p["b_f"], name=f"{name}_fgate_b")
    g["b_f"] = dbf[:, 0]
    dproj = jnp.concatenate([dq_raw, dk_raw, dv.astype(BF16), dgate], axis=1)
    dfp = _pad_lanes(jnp.transpose(df_t))
    dhn = _matmul(dproj, p["w_qkvg"], mode="nt", name=f"{name}_proj_dx")
    dhn = _matmul(dfp, p["w_f"], mode="nt", add=dhn, name=f"{name}_proj_f_dx")
    dw_qkvg = _matmul(a["hn"], dproj, mode="tn", out_dtype=BF16, name=f"{name}_proj_dw")
    dw_f = _matmul(a["hn"], dfp, mode="tn", out_dtype=BF16, name=f"{name}_proj_f_dw")
    g["w_in"] = jnp.concatenate([dw_qkvg, dw_f[:, :FOX_H]], axis=1)
    dh, g["mix_g"] = _rms_bwd(a["h"], p["mix_g"], dhn, gw=D_MODEL, ncol=1, add=dh1, name=f"{name}_norm_b")
    g["w_out_received"] = riding[-1]
    return dh, g, riding[:-1]


def _ffn_forward(h, p, name, rider=None):
    hn = _rms_fwd(h, p["ffn_g"], gw=D_MODEL, ncol=1, name=f"{name}_norm")
    u = _matmul(hn, p["w_up"], mode="nn", name=f"{name}_up")
    act, riding = _conv_fwd(u, p["conv_w8"], p["conv_b"], kw=FFN_K, width=D_FF, u_col0=0, mul_col0=D_FF, out_dtype=BF16,
                            name=f"{name}_glu", rider=rider)
    out = _matmul(act, p["w_down"], mode="nn", add=h, name=f"{name}_down")
    return out, dict(h=h, hn=hn, u=u, act=act), riding


def _ffn_backward(dh2, p, a, name, ride=()):
    g = {}
    dact = _matmul(dh2, p["w_down"], mode="nt", name=f"{name}_down_dx")
    g["w_down"] = _matmul(a["act"], dh2, mode="tn", out_dtype=BF16, name=f"{name}_down_dw")
    du, dwb, riding = _conv_bwd(a["u"], p["conv_w8"], p["conv_b"], dact, kw=FFN_K, width=D_FF, u_col0=0, mul_col0=D_FF,
                                name=f"{name}_glu_b", rider=_Exchange(list(ride)) if ride else None)
    g["conv_w"], g["conv_b"] = dwb[:FFN_K], dwb[7]
    dhn = _matmul(du, p["w_up"], mode="nt", name=f"{name}_up_dx")
    g["w_up"] = _matmul(a["hn"], du, mode="tn", out_dtype=BF16, name=f"{name}_up_dw")
    dh, g["ffn_g"] = _rms_bwd(a["h"], p["ffn_g"], dhn, gw=D_MODEL, ncol=1, add=dh2, name=f"{name}_norm_b")
    return dh, g, riding


def _to_slabs(dw, col_sharded):
    rows, cols = dw.shape
    if col_sharded:
        return jnp.transpose(dw.reshape(rows, 4, cols // 4), (1, 0, 2)).reshape(4, 2, rows // 2, cols // 4)
    return dw.reshape(4, 2, rows // 8, cols)


def kernel(x, mix_norm_g, ffn_norm_g, ssd_w_in, ssd_conv_w, ssd_conv_b, ssd_dt_bias, ssd_a_log, ssd_d, ssd_norm_g, ssd_w_out, fox_w_in, fox_b_f, fox_q_norm_g, fox_k_norm_g, fox_w_out, ffn_w_up, ffn_conv_w, ffn_conv_b, ffn_w_down, final_norm_g, loss_target, m_mix_norm_g, m_ffn_norm_g, m_ssd_w_in, m_ssd_conv_w, m_ssd_conv_b, m_ssd_dt_bias, m_ssd_a_log, m_ssd_d, m_ssd_norm_g, m_ssd_w_out, m_fox_w_in, m_fox_b_f, m_fox_q_norm_g, m_fox_k_norm_g, m_fox_w_out, m_ffn_w_up, m_ffn_conv_w, m_ffn_conv_b, m_ffn_w_down, m_final_norm_g, v_mix_norm_g, v_ffn_norm_g, v_ssd_w_in, v_ssd_conv_w, v_ssd_conv_b, v_ssd_dt_bias, v_ssd_a_log, v_ssd_d, v_ssd_norm_g, v_ssd_w_out, v_fox_w_in, v_fox_b_f, v_fox_q_norm_g, v_fox_k_norm_g, v_fox_w_out, v_ffn_w_up, v_ffn_conv_w, v_ffn_conv_b, v_ffn_w_down, v_final_norm_g):
    w = dict(mix_norm_g=mix_norm_g, ffn_norm_g=ffn_norm_g, ssd_w_in=ssd_w_in, ssd_conv_w=ssd_conv_w, ssd_conv_b=ssd_conv_b,
             ssd_dt_bias=ssd_dt_bias, ssd_a_log=ssd_a_log, ssd_d=ssd_d, ssd_norm_g=ssd_norm_g, ssd_w_out=ssd_w_out,
             fox_w_in=fox_w_in, fox_b_f=fox_b_f, fox_q_norm_g=fox_q_norm_g, fox_k_norm_g=fox_k_norm_g, fox_w_out=fox_w_out,
             ffn_w_up=ffn_w_up, ffn_conv_w=ffn_conv_w, ffn_conv_b=ffn_conv_b, ffn_w_down=ffn_w_down, final_norm_g=final_norm_g)
    m_in = dict(zip(NAMES, (m_mix_norm_g, m_ffn_norm_g, m_ssd_w_in, m_ssd_conv_w, m_ssd_conv_b, m_ssd_dt_bias, m_ssd_a_log,
                            m_ssd_d, m_ssd_norm_g, m_ssd_w_out, m_fox_w_in, m_fox_b_f, m_fox_q_norm_g, m_fox_k_norm_g,
                            m_fox_w_out, m_ffn_w_up, m_ffn_conv_w, m_ffn_conv_b, m_ffn_w_down, m_final_norm_g)))
    v_in = dict(zip(NAMES, (v_mix_norm_g, v_ffn_norm_g, v_ssd_w_in, v_ssd_conv_w, v_ssd_conv_b, v_ssd_dt_bias, v_ssd_a_log,
                            v_ssd_d, v_ssd_norm_g, v_ssd_w_out, v_fox_w_in, v_fox_b_f, v_fox_q_norm_g, v_fox_k_norm_g,
                            v_fox_w_out, v_ffn_w_up, v_ffn_conv_w, v_ffn_conv_b, v_ffn_w_down, v_final_norm_g)))
    cx, cy, cc = _coords()
    chip = 2 * cx + cy
    h = x[0]
    target = loss_target[0]

    conv_shapes = [ssd_conv_w.shape, ffn_conv_w.shape]
    slots = _allgather_small(_pack([ssd_conv_w, ffn_conv_w]), name="gather_conv_w", with_sum=False)[0]
    per_chip = [_unpack(slots[2 * q], conv_shapes) for q in range(4)]
    ssd_conv_full = jnp.concatenate([pc[0] for pc in per_chip], axis=2)
    ffn_conv_full = jnp.concatenate([pc[1] for pc in per_chip], axis=2)
    low = {n: w[n].astype(BF16) for n in BIG}
    sub_weights = dict(ssd=("ssd_w_in", "ssd_w_out"), fox=("fox_w_in", "fox_w_out"), ffn=("ffn_w_up", "ffn_w_down"))

    def shards_of(kind, idx):
        return [low[n][idx].reshape(2, low[n].shape[1] // 2, low[n].shape[2]) for n in sub_weights[kind]]

    def assemble(kind, idx, gathered):
        full = []
        for n, own, gth in zip(sub_weights[kind], shards_of(kind, idx), gathered):
            gth = lax.dynamic_update_slice(gth, own[None], (chip, 0, 0, 0))
            _, _, half, cols = gth.shape
            if n in COL_SHARDED:
                full.append(jnp.transpose(gth.reshape(4, 2 * half, cols), (1, 0, 2)).reshape(2 * half, 4 * cols))
            else:
                full.append(gth.reshape(8 * half, cols))
        return full

    def ssd_params(j, i, weights):
        w_in, w_out = weights
        g3 = lambda v: v.reshape(SSD_G, 1, SSD_HPG)
        g3c = lambda v: v.reshape(SSD_G, SSD_HPG, 1)
        return dict(mix_g=mix_norm_g[i][None], w_zx=w_in[:, :SSD_ZX], w_dt=_pad_lanes(w_in[:, SSD_ZX:]),
                    conv_w8=_pad8(ssd_conv_full[j]), conv_b=ssd_conv_b[j][None], bias_r=g3(ssd_dt_bias[j]),
                    bias_c=g3c(ssd_dt_bias[j]), alog_r=g3(ssd_a_log[j]), alog_c=g3c(ssd_a_log[j]), d_r=g3(ssd_d[j]),
                    norm_g=ssd_norm_g[j][None], w_out=w_out)

    def fox_params(j, i, weights):
        w_in, w_out = weights
        return dict(mix_g=mix_norm_g[i][None], w_qkvg=w_in[:, :4 * FOX_D], w_f=_pad_lanes(w_in[:, 4 * FOX_D:]),
                    gq=jnp.tile(fox_q_norm_g[j], FOX_H)[None], gk=jnp.tile(fox_k_norm_g[j], FOX_H)[None],
                    b_f=fox_b_f[j][:, None], w_out=w_out)

    def ffn_params(i, weights):
        w_up, w_down = weights
        return dict(ffn_g=ffn_norm_g[i][None], w_up=w_up, conv_w8=_pad8(ffn_conv_full[i]), conv_b=ffn_conv_b[i][None],
                    w_down=w_down)

    order = [("ssd", 0), ("ffn", 0), ("fox", 0), ("ffn", 1), ("ssd", 1), ("ffn", 2), ("fox", 1), ("ffn", 3)]
    fetch = {("ssd", 0): [("ffn", 0)], ("ffn", 0): [("fox", 0)], ("fox", 0): [("ffn", 1), ("ssd", 1), ("ffn", 2)],
             ("ssd", 1): [("fox", 1)], ("fox", 1): [("ffn", 3)]}
    ready = {("ssd", 0): assemble("ssd", 0, _run_rider(_Gather(shards_of("ssd", 0)), name="gather_first"))}
    params, acts = {}, {}
    forward = dict(ssd=_ssd_forward, fox=_fox_forward, ffn=_ffn_forward)
    for kind, idx in order:
        if kind == "ssd":
            params[kind, idx] = ssd_params(idx, 2 * idx, ready.pop((kind, idx)))
        elif kind == "fox":
            params[kind, idx] = fox_params(idx, 2 * idx + 1, ready.pop((kind, idx)))
        else:
            params[kind, idx] = ffn_params(idx, ready.pop((kind, idx)))
        wanted = fetch.get((kind, idx), [])
        rider = _Gather([s for sub in wanted for s in shards_of(*sub)]) if wanted else None
        h, acts[kind, idx], riding = forward[kind](h, params[kind, idx], f"{kind}{idx}", rider=rider)
        for q, sub in enumerate(wanted):
            ready[sub] = assemble(*sub, riding[2 * q:2 * q + 2])
    loss_part, dh, d_final_g = _loss_head(h, final_norm_g[None], target, name="loss_head")

    backward = dict(ssd=_ssd_backward, fox=_fox_backward, ffn=_ffn_backward)
    grad_keys = dict(ssd=("w_in", "w_out"), fox=("w_in", "w_out"), ffn=("w_up", "w_down"))
    sub_g, slabs, received = {}, {}, {}
    waiting = []
    for sub in reversed(order):
        kind = sub[0]
        dh, sub_g[sub], got = backward[kind](dh, params[sub], acts[sub], f"{kind}{sub[1]}",
                                             ride=[slabs[key] for key in waiting])
        received.update(zip(waiting, got))
        waiting = []
        for q, (key, n) in enumerate(zip(grad_keys[kind], sub_weights[kind])):
            slabs[sub, q] = _to_slabs(sub_g[sub][key], n in COL_SHARDED)
            if kind != "ffn" and q == 1:
                received[sub, q] = sub_g[sub]["w_out_received"]
            else:
                waiting.append((sub, q))
    received.update(zip(waiting, _run_rider(_Exchange([slabs[key] for key in waiting]), name="rs_last_exchange")))
    grad_x = dh[None]
    ssd_g, fox_g = [sub_g["ssd", 0], sub_g["ssd", 1]], [sub_g["fox", 0], sub_g["fox", 1]]
    mix_g = [ssd_g[0], fox_g[0], ssd_g[1], fox_g[1]]
    ffn_g = [sub_g["ffn", i] for i in range(DEPTH)]

    me = jnp.reshape(2 * chip + cc, (1,)).astype(jnp.int32)
    finals = {}
    for sub in order:
        for q in range(2):
            _, _, m, cols = slabs[sub, q].shape
            finals[sub, q] = _add_selected(slabs[sub, q].reshape(8, m, cols), received[sub, q], me,
                                           name=f"rs_add_{sub[0]}{sub[1]}_{q}")
    keys = list(finals)
    others = dict(zip(keys, _sibling_swap([finals[key] for key in keys], name="rs_result_swap")))
    grads = {}
    for kind, names in sub_weights.items():
        for q, n in enumerate(names):
            subs = [sub for sub in sorted(set(order)) if sub[0] == kind]
            mine = jnp.stack([finals[sub, q] for sub in subs])
            theirs = jnp.stack([others[sub, q] for sub in subs])
            halves = jnp.stack([jnp.where(cc == 0, mine, theirs), jnp.where(cc == 0, theirs, mine)], axis=1)
            grads[n] = halves.reshape(w[n].shape)
    small = dict(
        mix_norm_g=jnp.concatenate([g["mix_g"] for g in mix_g], axis=0),
        ffn_norm_g=jnp.concatenate([g["ffn_g"] for g in ffn_g], axis=0),
        ssd_conv_w=jnp.stack([g["conv_w"] for g in ssd_g]), ssd_conv_b=jnp.stack([g["conv_b"] for g in ssd_g]),
        ssd_dt_bias=jnp.stack([g["dt_bias"].reshape(SSD_H) for g in ssd_g]),
        ssd_a_log=jnp.stack([g["a_log"].reshape(SSD_H) for g in ssd_g]),
        ssd_d=jnp.stack([g["d"].reshape(SSD_H) for g in ssd_g]),
        ssd_norm_g=jnp.concatenate([g["norm_g"] for g in ssd_g], axis=0),
        fox_b_f=jnp.stack([g["b_f"] for g in fox_g]), fox_q_norm_g=jnp.stack([g["gq"] for g in fox_g]),
        fox_k_norm_g=jnp.stack([g["gk"] for g in fox_g]),
        ffn_conv_w=jnp.stack([g["conv_w"] for g in ffn_g]), ffn_conv_b=jnp.stack([g["conv_b"] for g in ffn_g]),
        final_norm_g=d_final_g[0], loss=loss_part[0, :1])
    _, total = _allgather_small(_pack([small[n] for n, _ in SMALL]), name="reduce_small", with_sum=True)
    for (n, shp), val in zip(SMALL, _unpack(total, [shp for _, shp in SMALL])):
        grads[n] = val
    loss = grads.pop("loss")[0]
    grads["ssd_conv_w"] = lax.dynamic_slice_in_dim(grads["ssd_conv_w"], chip * ssd_conv_w.shape[2], ssd_conv_w.shape[2], axis=2)
    grads["ffn_conv_w"] = lax.dynamic_slice_in_dim(grads["ffn_conv_w"], chip * ffn_conv_w.shape[2], ffn_conv_w.shape[2], axis=2)

    deltas, new_m, new_v = {}, {}, {}
    for n in NAMES:
        shp = w[n].shape
        two_d = (1, shp[0]) if len(shp) == 1 else (-1, shp[-1])
        r2 = lambda a: a.reshape(two_d)
        d, nm, nv = _adamw(r2(w[n]), r2(grads[n]), r2(m_in[n]), r2(v_in[n]), name=f"adamw_{n}")
        deltas[n], new_m[n], new_v[n] = d.reshape(shp), nm.reshape(shp), nv.reshape(shp)
    return (loss, grad_x, *[grads[n] for n in NAMES], *[deltas[n] for n in NAMES], *[new_m[n] for n in NAMES],
            *[new_v[n] for n in NAMES])
```

```python
import functools

import jax
import jax.numpy as jnp
from jax import lax
from jax.experimental import pallas as pl
from jax.experimental.pallas import tpu as pltpu

F32 = jnp.float32
BF16 = jnp.bfloat16
HI = lax.Precision.HIGHEST
MESH = pl.DeviceIdType.MESH

D_MODEL = 1024
DEPTH = 4
EPS = 1e-6
SSD_DI = 2048
SSD_HD = 64
SSD_G = 4
SSD_HPG = 8
SSD_N = 128
SSD_K = 4
CHUNK = 128
SSD_CONV_DIM = 3072
SSD_ZX = SSD_DI + SSD_CONV_DIM
SSD_H = 32
FOX_HD = 64
FOX_H = 16
FOX_D = 1024
D_FF = 2816
FFN_K = 3
LANES = 128
VMEM_LIMIT = 56 * 1024 * 1024

ADAM_LR = 0.001
ADAM_B1 = 0.9
ADAM_B2 = 0.999
ADAM_EPS = 1e-08
ADAM_WD = 0.01
ADAM_STEP = 10

NN = (((1,), (0,)), ((), ()))
NT = (((1,), (1,)), ((), ()))
TN = (((0,), (0,)), ((), ()))


def _pick(n, cap, mult=LANES):
    best = None
    for t in range(mult, min(n, cap) + 1, mult):
        if n % t == 0:
            best = t
    return best if best is not None else n


def _cp(sem):
    return pltpu.CompilerParams(dimension_semantics=sem, vmem_limit_bytes=VMEM_LIMIT)


def _sigmoid(x):
    return jax.nn.sigmoid(x)


def _silu(x):
    return x * _sigmoid(x)


def _dsilu(x):
    s = _sigmoid(x)
    return s * (1.0 + x * (1.0 - s))


def _softplus(x):
    e = jnp.exp(-jnp.abs(x))
    u = 1.0 + e
    l1p = jnp.where(u == 1.0, e, jnp.log(u) * (e / (u - 1.0)))
    return jnp.maximum(x, 0.0) + l1p


def _dotf(a, b, dn=NN, *, onehot="b", pieces=2):
    x, e = (a, b) if onehot == "b" else (b, a)
    e = e.astype(BF16)
    acc = None
    for n in range(pieces):
        hi = x.astype(BF16)
        part = lax.dot_general(hi, e, dn, preferred_element_type=F32) if onehot == "b" else \
            lax.dot_general(e, hi, dn, preferred_element_type=F32)
        acc = part if acc is None else acc + part
        if n + 1 < pieces:
            x = x - hi.astype(F32)
    return acc


def _dotb(a, b, dn=NN):
    return lax.dot_general(a.astype(BF16), b.astype(BF16), dn, preferred_element_type=F32)


def _group_matrix(width, sub, transpose=False):
    ng = width // sub
    shape = (ng, width) if transpose else (width, ng)
    lane = lax.broadcasted_iota(jnp.int32, shape, 1 if transpose else 0)
    grp = lax.broadcasted_iota(jnp.int32, shape, 0 if transpose else 1)
    return (lane // sub == grp).astype(F32)


def _gmean(v, sub):
    width = v.shape[-1]
    if sub == width:
        return jnp.mean(v, axis=-1, keepdims=True)
    assert LANES % sub == 0 and width % LANES == 0
    lane = lax.broadcasted_iota(jnp.int32, (v.shape[0], LANES), 1)
    tiles = []
    for b in range(width // LANES):
        t = v[:, b * LANES:(b + 1) * LANES]
        step = 1
        while step < sub:
            lower = pltpu.roll(t, step, axis=1)
            upper = pltpu.roll(t, LANES - step, axis=1)
            t = t + jnp.where((lane & step) != 0, lower, upper)
            step *= 2
        tiles.append(t)
    return jnp.concatenate(tiles, axis=1) * (1.0 / sub)


def _matmul(a, b, *, mode, name, out_dtype=F32, add=None):
    a_planes = a.shape[0] if (mode == "nt" and a.ndim == 3) else 0
    b_planes = b.shape[0] if (mode == "tn" and b.ndim == 3) else 0
    a2 = (a.shape[1], a.shape[0] * a.shape[2]) if a_planes else a.shape
    b2 = (b.shape[1], b.shape[0] * b.shape[2]) if b_planes else b.shape
    if mode == "nn":
        (m, k), (k2, n) = a2, b2
    elif mode == "nt":
        (m, k), (n, k2) = a2, b2
    else:
        (k, m), (k2, n) = a2, b2
    assert k == k2, (a.shape, b.shape, mode)
    tm, tn = _pick(m, 1536), _pick(n // b_planes if b_planes else n, 1536)
    tk = _pick(k // a_planes if a_planes else k, 1536)
    nk = k // tk
    dn = {"nn": NN, "nt": NT, "tn": TN}[mode]
    has_add = add is not None

    def body(*refs):
        if has_add:
            a_ref, b_ref, add_ref, o_ref, acc_ref = refs
        else:
            a_ref, b_ref, o_ref, acc_ref = refs
            add_ref = None
        kk = pl.program_id(2)
        part = _dotb(a_ref[0] if a_planes else a_ref[...], b_ref[0] if b_planes else b_ref[...], dn)

        def finish(r):
            if has_add:
                r = r + add_ref[...]
            o_ref[...] = r.astype(out_dtype)

        if nk == 1:
            finish(part)
        else:
            @pl.when(kk == 0)
            def _():
                acc_ref[...] = part

            @pl.when(kk > 0)
            def _():
                acc_ref[...] += part

            @pl.when(kk == nk - 1)
            def _():
                finish(acc_ref[...])

    if mode == "nn":
        a_spec = pl.BlockSpec((tm, tk), lambda i, j, q: (i, q))
        b_spec = pl.BlockSpec((tk, tn), lambda i, j, q: (q, j))
    elif mode == "nt":
        per = (k // a_planes) // tk if a_planes else 0
        a_spec = (pl.BlockSpec((1, tm, tk), lambda i, j, q: (q // per, i, q % per)) if a_planes
                  else pl.BlockSpec((tm, tk), lambda i, j, q: (i, q)))
        b_spec = pl.BlockSpec((tn, tk), lambda i, j, q: (j, q))
    else:
        per = (n // b_planes) // tn if b_planes else 0
        a_spec = pl.BlockSpec((tk, tm), lambda i, j, q: (q, i))
        b_spec = (pl.BlockSpec((1, tk, tn), lambda i, j, q: (j // per, q, j % per)) if b_planes
                  else pl.BlockSpec((tk, tn), lambda i, j, q: (q, j)))
    o_spec = pl.BlockSpec((tm, tn), lambda i, j, q: (i, j))
    in_specs = [a_spec, b_spec] + ([o_spec] if has_add else [])
    args = (a, b) + ((add,) if has_add else ())
    return pl.pallas_call(
        body, name=name, grid=(m // tm, n // tn, nk), in_specs=in_specs, out_specs=o_spec,
        out_shape=jax.ShapeDtypeStruct((m, n), out_dtype),
        scratch_shapes=[pltpu.VMEM((tm, tn) if nk > 1 else (8, LANES), F32)],
        compiler_params=_cp(("parallel", "parallel", "arbitrary")),
    )(*args)


def _rms_fwd(x, g, *, gw, ncol, name, x_col0=0, sub=None, z=None, z_col0=0, out_dtype=BF16):
    rows = x.shape[0]
    tr = _pick(rows, 512, 8)
    sub = gw if sub is None else sub
    gated = z is not None

    def body(*refs):
        if gated:
            x_ref, z_ref, g_ref, o_ref = refs
            xv = x_ref[...] * _silu(z_ref[...])
        else:
            x_ref, g_ref, o_ref = refs
            xv = x_ref[...]
        r = lax.rsqrt(_gmean(xv * xv, sub) + EPS)
        o_ref[...] = (xv * r * g_ref[...]).astype(out_dtype)

    specs = [pl.BlockSpec((tr, gw), lambda j, i: (i, x_col0 + j))]
    args = [x]
    if gated:
        specs.append(pl.BlockSpec((tr, gw), lambda j, i: (i, z_col0 + j)))
        args.append(z)
    specs.append(pl.BlockSpec((1, gw), lambda j, i: (0, j)))
    args.append(g)
    return pl.pallas_call(
        body, name=name, grid=(ncol, rows // tr), in_specs=specs,
        out_specs=pl.BlockSpec((tr, gw), lambda j, i: (i, j)),
        out_shape=jax.ShapeDtypeStruct((rows, gw * ncol), out_dtype),
        compiler_params=_cp(("parallel", "parallel")),
    )(*args)


def _rms_bwd(x, g, dy, *, gw, ncol, name, x_col0=0, sub=None, z=None, z_col0=0, add=None, dx_dtype=F32):
    rows = x.shape[0]
    tr = _pick(rows, 512, 8)
    sub = gw if sub is None else sub
    gated = z is not None
    has_add = add is not None

    def body(*refs):
        refs = list(refs)
        x_ref = refs.pop(0)
        z_ref = refs.pop(0) if gated else None
        g_ref = refs.pop(0)
        dy_ref = refs.pop(0)
        add_ref = refs.pop(0) if has_add else None
        dx_ref = refs.pop(0)
        dz_ref = refs.pop(0) if gated else None
        dg_ref = refs.pop(0)
        i = pl.program_id(1)
        xv = x_ref[...]
        if gated:
            zz = z_ref[...]
            yz = xv * _silu(zz)
        else:
            yz = xv
        r = lax.rsqrt(_gmean(yz * yz, sub) + EPS)
        xh = yz * r
        dy = dy_ref[...].astype(F32)
        dyg = dy * g_ref[...]
        d_yz = r * (dyg - xh * _gmean(dyg * xh, sub))
        if gated:
            dx_ref[...] = (d_yz * _silu(zz)).astype(dx_dtype)
            dz_ref[...] = (d_yz * xv * _dsilu(zz)).astype(dx_dtype)
        elif has_add:
            dx_ref[...] = (d_yz + add_ref[...]).astype(dx_dtype)
        else:
            dx_ref[...] = d_yz.astype(dx_dtype)
        part = jnp.sum(dy * xh, axis=0, keepdims=True)

        @pl.when(i == 0)
        def _():
            dg_ref[...] = part

        @pl.when(i > 0)
        def _():
            dg_ref[...] += part

    tile = pl.BlockSpec((tr, gw), lambda j, i: (i, j))
    specs = [pl.BlockSpec((tr, gw), lambda j, i: (i, x_col0 + j))]
    args = [x]
    if gated:
        specs.append(pl.BlockSpec((tr, gw), lambda j, i: (i, z_col0 + j)))
        args.append(z)
    specs += [pl.BlockSpec((1, gw), lambda j, i: (0, j)), tile]
    args += [g, dy]
    if has_add:
        specs.append(tile)
        args.append(add)
    width = gw * ncol
    out_shape = [jax.ShapeDtypeStruct((rows, width), dx_dtype)]
    out_specs = [tile]
    if gated:
        out_shape.append(jax.ShapeDtypeStruct((rows, width), dx_dtype))
        out_specs.append(tile)
    out_shape.append(jax.ShapeDtypeStruct((1, width), F32))
    out_specs.append(pl.BlockSpec((1, gw), lambda j, i: (0, j)))
    return pl.pallas_call(
        body, name=name, grid=(ncol, rows // tr), in_specs=specs, out_specs=out_specs, out_shape=out_shape,
        compiler_params=_cp(("parallel", "arbitrary")),
    )(*args)


HALO = 8


def _conv_rows(tc):
    return 16 * 8 * LANES // tc


def _conv_fwd(u, w8, b, *, kw, width, name, u_col0=0, mul_col0=None, out_dtype=F32, rider=None):
    rows = u.shape[0]
    ts = _pick(rows, 512, 8)
    tc = _pick(width, 512)
    gated = mul_col0 is not None
    c0 = u_col0 // tc
    m0 = (mul_col0 // tc) if gated else 0
    assert u_col0 % tc == 0 and (not gated or mul_col0 % tc == 0)

    def body(*refs):
        if gated:
            cur_ref, halo_ref, mul_ref, w_ref, b_ref, o_ref, ext = refs
        else:
            cur_ref, halo_ref, w_ref, b_ref, o_ref, ext = refs
        i = pl.program_id(0)
        ext[pl.ds(0, HALO), :] = jnp.where(i == 0, 0.0, halo_ref[...])
        ext[pl.ds(HALO, ts), :] = cur_ref[...]
        bias = b_ref[...]
        taps = [w_ref[k:k + 1, :] for k in range(kw)]
        rb = _conv_rows(tc)
        for r0 in range(0, ts, rb):
            pre = bias + taps[0] * ext[pl.ds(r0 + HALO - (kw - 1), rb), :]
            for k in range(1, kw):
                pre = pre + taps[k] * ext[pl.ds(r0 + HALO - (kw - 1) + k, rb), :]
            act = _silu(pre)
            if gated:
                act = act * mul_ref[pl.ds(r0, rb), :]
            o_ref[pl.ds(r0, rb), :] = act.astype(out_dtype)

    hb = ts // HALO
    specs = [pl.BlockSpec((ts, tc), lambda i, j: (i, c0 + j)),
             pl.BlockSpec((HALO, tc), lambda i, j: (jnp.maximum(i * hb - 1, 0), c0 + j))]
    args = [u, u]
    if gated:
        specs.append(pl.BlockSpec((ts, tc), lambda i, j: (i, m0 + j)))
        args.append(u)
    specs += [pl.BlockSpec((8, tc), lambda i, j: (0, j)), pl.BlockSpec((1, tc), lambda i, j: (0, j))]
    args += [w8, b]
    outs, riding = _call(
        body, name=name, grid=(rows // ts, width // tc), in_specs=specs,
        out_specs=[pl.BlockSpec((ts, tc), lambda i, j: (i, j))],
        out_shape=[jax.ShapeDtypeStruct((rows, width), out_dtype)],
        scratch_shapes=[pltpu.VMEM((ts + HALO, tc), F32)], sem=("parallel", "parallel"), rider=rider, args=args)
    return outs + [riding]


def _conv_bwd(u, w8, b, dact, *, kw, width, name, u_col0=0, mul_col0=None, du_dtype=BF16, rider=None):
    rows = u.shape[0]
    ts = _pick(rows, 512, 8)
    tc = _pick(width, 512)
    pieces = list(dact) if isinstance(dact, (list, tuple)) else [dact]
    firsts, seen = [], 0
    for piece in pieces:
        assert piece.shape[1] % tc == 0, (piece.shape, tc)
        firsts.append(seen // tc)
        seen += piece.shape[1]
    assert seen == width
    gated = mul_col0 is not None
    c0 = u_col0 // tc
    m0 = (mul_col0 // tc) if gated else 0
    nt = rows // ts
    hb = ts // HALO

    def body(*refs):
        refs = list(refs)
        cur_ref, halo_ref = refs.pop(0), refs.pop(0)
        mul_ref = refs.pop(0) if gated else None
        w_ref, b_ref = refs.pop(0), refs.pop(0)
        da_refs = [refs.pop(0) for _ in pieces]
        col_tile = pl.program_id(0)
        du_ref = refs.pop(0)
        dwb_ref, ext_u, ext_d = refs
        t = pl.program_id(1)
        ti = nt - 1 - t
        ext_u[pl.ds(0, HALO), :] = jnp.where(ti == 0, 0.0, halo_ref[...])
        ext_u[pl.ds(HALO, ts), :] = cur_ref[...]

        @pl.when(t == 0)
        def _():
            ext_d[pl.ds(ts, HALO), :] = jnp.zeros((HALO, tc), F32)
            dwb_ref[...] = jnp.zeros((8, tc), F32)

        bias = b_ref[...]
        taps = [w_ref[k:k + 1, :] for k in range(kw)]
        rb = _conv_rows(tc)
        dw_acc = [jnp.zeros((1, tc), F32) for _ in range(kw)]
        db_acc = jnp.zeros((1, tc), F32)
        for r0 in reversed(range(0, ts, rb)):
            shifted = [ext_u[pl.ds(r0 + HALO - (kw - 1) + k, rb), :] for k in range(kw)]
            pre = bias + taps[0] * shifted[0]
            for k in range(1, kw):
                pre = pre + taps[k] * shifted[k]
            sg = _sigmoid(pre)
            dsilu = sg * (1.0 + pre * (1.0 - sg))
            da = da_refs[0][pl.ds(r0, rb), :].astype(F32)
            for first, ref in zip(firsts[1:], da_refs[1:]):
                da = jnp.where(col_tile >= first, ref[pl.ds(r0, rb), :].astype(F32), da)
            if gated:
                du_ref[1, pl.ds(r0, rb), :] = (da * (pre * sg)).astype(du_dtype)
                dgp = da * mul_ref[pl.ds(r0, rb), :] * dsilu
            else:
                dgp = da * dsilu
            ext_d[pl.ds(r0, rb), :] = dgp
            du = taps[kw - 1] * dgp
            for k in range(kw - 1):
                du = du + taps[k] * ext_d[pl.ds(r0 + kw - 1 - k, rb), :]
            if gated:
                du_ref[0, pl.ds(r0, rb), :] = du.astype(du_dtype)
            else:
                du_ref[pl.ds(r0, rb), :] = du.astype(du_dtype)
            for k in range(kw):
                dw_acc[k] = dw_acc[k] + jnp.sum(dgp * shifted[k], axis=0, keepdims=True)
            db_acc = db_acc + jnp.sum(dgp, axis=0, keepdims=True)
        for k in range(kw):
            dwb_ref[k:k + 1, :] += dw_acc[k]
        dwb_ref[7:8, :] += db_acc
        ext_d[pl.ds(ts, HALO), :] = ext_d[pl.ds(0, HALO), :]

    specs = [pl.BlockSpec((ts, tc), lambda j, t: (nt - 1 - t, c0 + j)),
             pl.BlockSpec((HALO, tc), lambda j, t: (jnp.maximum((nt - 1 - t) * hb - 1, 0), c0 + j))]
    args = [u, u]
    if gated:
        specs.append(pl.BlockSpec((ts, tc), lambda j, t: (nt - 1 - t, m0 + j)))
        args.append(u)
    tile = pl.BlockSpec((ts, tc), lambda j, t: (nt - 1 - t, j))
    specs += [pl.BlockSpec((8, tc), lambda j, t: (0, j)), pl.BlockSpec((1, tc), lambda j, t: (0, j))]
    args += [w8, b]
    for first, piece in zip(firsts, pieces):
        count = piece.shape[1] // tc

        def piece_map(j, t, first=first, count=count):
            mine = jnp.logical_and(j >= first, j < first + count)
            return (jnp.where(mine, nt - 1 - t, 0), jnp.clip(j - first, 0, count - 1))

        specs.append(pl.BlockSpec((ts, tc), piece_map))
        args.append(piece)
    if gated:
        out_shape = [jax.ShapeDtypeStruct((2, rows, width), du_dtype)]
        out_specs = [pl.BlockSpec((2, ts, tc), lambda j, t: (0, nt - 1 - t, j))]
    else:
        out_shape = [jax.ShapeDtypeStruct((rows, width), du_dtype)]
        out_specs = [tile]
    out_shape.append(jax.ShapeDtypeStruct((8, width), F32))
    out_specs.append(pl.BlockSpec((8, tc), lambda j, t: (0, j)))
    outs, riding = _call(
        body, name=name, grid=(width // tc, nt), in_specs=specs, out_specs=out_specs, out_shape=out_shape,
        scratch_shapes=[pltpu.VMEM((ts + HALO, tc), F32), pltpu.VMEM((ts + HALO, tc), F32)],
        sem=("parallel", "arbitrary"), rider=rider, args=args)
    return outs + [riding]


GW = SSD_HPG * SSD_HD


def _ssd_common(x, bm, cm, dt_raw, dt_raw_t, bias_r, bias_c, alog_r, alog_c):
    row = lax.broadcasted_iota(jnp.int32, (CHUNK, CHUNK), 0)
    col = lax.broadcasted_iota(jnp.int32, (CHUNK, CHUNK), 1)
    causal = row >= col
    tril = causal.astype(F32)
    triu = (row <= col).astype(F32)
    spread = _group_matrix(GW, SSD_HD, transpose=True)
    dt = _softplus(dt_raw + bias_r)
    dt_t = _softplus(dt_raw_t + bias_c)
    a_r = -jnp.exp(alog_r)
    a_c = -jnp.exp(alog_c)
    acs = _dotf(tril, dt * a_r, onehot="a", pieces=3)
    acs_t = _dotf(dt_t * a_c, triu, pieces=3)
    last = acs[CHUNK - 1:CHUNK, :]
    ds = jnp.exp(last - acs)
    cd = jnp.exp(last)
    c = dict(causal=causal, tril=tril, triu=triu, spread=spread, dt=dt, a_r=a_r, acs=acs, acs_t=acs_t, ds=ds, cd=cd)
    c["eb"] = _dotf(jnp.exp(acs), spread)
    c["dsb"] = _dotf(ds, spread)
    c["cdb"] = _dotf(cd, spread)
    c["dtb"] = _dotf(dt, spread)
    c["xdt"] = x * c["dtb"]
    c["cb"] = _dotb(cm, bm, NT)
    return c


def _ssd_lam(c, r):
    diff = c["acs"][:, r:r + 1] - c["acs_t"][r:r + 1, :]
    return jnp.exp(jnp.where(c["causal"], diff, -jnp.inf))


GP = 2


def _ssd_specs(nc, rev):
    def ci(t):
        return (nc - 1 - t) if rev else t
    xs = pl.BlockSpec((CHUNK, GP * GW), lambda g, t: (ci(t), g))
    bs = pl.BlockSpec((CHUNK, GP * SSD_N), lambda g, t: (ci(t), SSD_DI // (GP * SSD_N) + g))
    cs = pl.BlockSpec((CHUNK, GP * SSD_N), lambda g, t: (ci(t), (SSD_DI // SSD_N + SSD_G) // GP + g))
    dts = pl.BlockSpec((GP, CHUNK, 8), lambda g, t: (g, ci(t), 0))
    dtts = pl.BlockSpec((GP, 8, CHUNK), lambda g, t: (g, 0, ci(t)))
    pr = pl.BlockSpec((GP, 1, 8), lambda g, t: (g, 0, 0))
    pc = pl.BlockSpec((GP, 8, 1), lambda g, t: (g, 0, 0))
    hs = pl.BlockSpec((1, GP, SSD_N, GW), lambda g, t: (ci(t), g, 0, 0))
    return xs, bs, cs, dts, dtts, pr, pc, hs


def _ssd_fwd(xbc, dtg, dtg_t, bias_r, bias_c, alog_r, alog_c, d_r, *, name, rider=None):
    s = xbc.shape[0]
    nc = s // CHUNK
    xs, bs, cs, dts, dtts, pr, pc, hs = _ssd_specs(nc, False)

    def body(x_ref, b_ref, c_ref, dt_ref, dtt_ref, br_ref, bc_ref, ar_ref, ac_ref, d_ref, y_ref, hp_ref, h_sc):
        t = pl.program_id(1)

        @pl.when(t == 0)
        def _():
            h_sc[...] = jnp.zeros_like(h_sc)

        for gg in range(GP):
            wide, narrow = slice(gg * GW, (gg + 1) * GW), slice(gg * SSD_N, (gg + 1) * SSD_N)
            x, bm, cm = x_ref[:, wide], b_ref[:, narrow], c_ref[:, narrow]
            c = _ssd_common(x, bm, cm, dt_ref[gg], dtt_ref[gg], br_ref[gg], bc_ref[gg], ar_ref[gg], ac_ref[gg])
            h = h_sc[gg]
            hp_ref[0, gg] = h
            xdt = c["xdt"]
            pieces = []
            for r in range(SSD_HPG):
                m = c["cb"] * _ssd_lam(c, r)
                pieces.append(_dotb(m, xdt[:, r * SSD_HD:(r + 1) * SSD_HD]))
            y = jnp.concatenate(pieces, axis=1) + c["eb"] * _dotb(cm, h) + x * _dotf(d_ref[gg], c["spread"])
            y_ref[:, wide] = y
            h_sc[gg] = h * c["cdb"] + _dotb(bm, xdt * c["dsb"], TN)

    outs, riding = _call(
        body, name=name, grid=(SSD_G // GP, nc),
        in_specs=[xs, bs, cs, dts, dtts, pr, pc, pr, pc, pr],
        out_specs=[xs, hs],
        out_shape=[jax.ShapeDtypeStruct((s, SSD_DI), F32), jax.ShapeDtypeStruct((nc, SSD_G, SSD_N, GW), F32)],
        scratch_shapes=[pltpu.VMEM((GP, SSD_N, GW), F32)], sem=("parallel", "arbitrary"), rider=rider,
        args=(xbc, xbc, xbc, dtg, dtg_t, bias_r, bias_c, alog_r, alog_c, d_r))
    return outs + [riding]


def _ssd_bwd(xbc, dtg, dtg_t, bias_r, bias_c, alog_r, alog_c, d_r, hprev, dy, *, name, rider=None):
    s = xbc.shape[0]
    nc = s // CHUNK
    xs, bs, cs, dts, dtts, pr, pc, hs = _ssd_specs(nc, True)
    gsum = functools.partial(_group_matrix, GW, SSD_HD)

    def body(x_ref, b_ref, c_ref, dt_ref, dtt_ref, br_ref, bc_ref, ar_ref, ac_ref, d_ref, hp_ref, dy_ref,
             dx_ref, db_ref, dc_ref, ddt_ref, dbias_ref, dalog_ref, dd_ref, dh_sc):
        t = pl.program_id(1)

        @pl.when(t == 0)
        def _():
            dh_sc[...] = jnp.zeros_like(dh_sc)
            dbias_ref[...] = jnp.zeros_like(dbias_ref)
            dalog_ref[...] = jnp.zeros_like(dalog_ref)
            dd_ref[...] = jnp.zeros_like(dd_ref)

        for gg in range(GP):
            wide, narrow = slice(gg * GW, (gg + 1) * GW), slice(gg * SSD_N, (gg + 1) * SSD_N)
            x, bm, cm = x_ref[:, wide], b_ref[:, narrow], c_ref[:, narrow]
            c = _ssd_common(x, bm, cm, dt_ref[gg], dtt_ref[gg], br_ref[gg], bc_ref[gg], ar_ref[gg], ac_ref[gg])
            lanesum = gsum()
            h = hp_ref[0, gg]
            dh = dh_sc[gg]
            dy = dy_ref[:, wide]
            xdt, dsb = c["xdt"], c["dsb"]
            skip = _dotf(d_ref[gg], c["spread"])
            dd_ref[gg] += jnp.sum(_dotf(dy * x, lanesum), axis=0, keepdims=True)
            dacs = _dotf(dy * (c["eb"] * _dotb(cm, h)), lanesum)
            edy = c["eb"] * dy
            dcm = _dotb(edy, h, NT)
            dh_prev = _dotb(cm, edy, TN)
            bdh = _dotb(bm, dh)
            dxdt = dsb * bdh
            dbm = _dotb(dsb * xdt, dh, NT)
            t1 = _dotf(xdt * bdh, lanesum) * c["ds"]
            dacs = dacs - t1
            dlast = (jnp.sum(t1, axis=0, keepdims=True)
                     + jnp.sum(_dotf(dh * h, lanesum), axis=0, keepdims=True) * c["cd"])
            dcb = jnp.zeros((CHUNK, CHUNK), F32)
            pieces = []
            ones8 = jnp.ones((CHUNK, 8), F32)
            head = lax.broadcasted_iota(jnp.int32, (1, 8), 1)
            for r in range(SSD_HPG):
                sl = slice(r * SSD_HD, (r + 1) * SSD_HD)
                lam = _ssd_lam(c, r)
                m = c["cb"] * lam
                dm = _dotb(dy[:, sl], xdt[:, sl], NT)
                dcb = dcb + dm * lam
                gm = dm * m
                dacs = dacs + ((jnp.sum(gm, axis=1, keepdims=True) - _dotf(gm, ones8, TN, pieces=3))
                               * (head == r).astype(F32))
                pieces.append(_dotb(m, dy[:, sl], TN))
            dxdt = dxdt + jnp.concatenate(pieces, axis=1)
            dcm = dcm + _dotb(dcb, bm)
            dbm = dbm + _dotb(dcb, cm, TN)
            dx_ref[:, wide] = dy * skip + dxdt * c["dtb"]
            db_ref[:, narrow] = dbm
            dc_ref[:, narrow] = dcm
            rowid = lax.broadcasted_iota(jnp.int32, (CHUNK, 8), 0)
            dacs = dacs + jnp.where(rowid == CHUNK - 1, dlast, 0.0)
            dda = _dotf(c["triu"], dacs, onehot="a", pieces=3)
            ddt = _dotf(dxdt * x, lanesum) + dda * c["a_r"]
            ddt_raw = ddt * _sigmoid(dt_ref[gg] + br_ref[gg])
            ddt_ref[gg] = ddt_raw
            dbias_ref[gg] += jnp.sum(ddt_raw, axis=0, keepdims=True)
            dalog_ref[gg] += jnp.sum(dda * c["dt"], axis=0, keepdims=True) * c["a_r"]
            dh_sc[gg] = dh_prev + dh * c["cdb"]

    ci = lambda t: nc - 1 - t
    nspec = pl.BlockSpec((CHUNK, GP * SSD_N), lambda g, t: (ci(t), g))
    outs, riding = _call(
        body, name=name, grid=(SSD_G // GP, nc),
        in_specs=[xs, bs, cs, dts, dtts, pr, pc, pr, pc, pr, hs, xs],
        out_specs=[xs, nspec, nspec, dts, pr, pr, pr],
        out_shape=[jax.ShapeDtypeStruct((s, SSD_DI), F32), jax.ShapeDtypeStruct((s, SSD_G * SSD_N), F32),
                   jax.ShapeDtypeStruct((s, SSD_G * SSD_N), F32), jax.ShapeDtypeStruct((SSD_G, s, 8), F32),
                   jax.ShapeDtypeStruct((SSD_G, 1, 8), F32), jax.ShapeDtypeStruct((SSD_G, 1, 8), F32),
                   jax.ShapeDtypeStruct((SSD_G, 1, 8), F32)],
        scratch_shapes=[pltpu.VMEM((GP, SSD_N, GW), F32)], sem=("parallel", "arbitrary"), rider=rider,
        args=(xbc, xbc, xbc, dtg, dtg_t, bias_r, bias_c, alog_r, alog_c, d_r, hprev, dy))
    return outs + [riding]


FOX_PAIRS = FOX_H // 2
FOX_SCALE = FOX_HD ** -0.5
NEG_INF = -jnp.inf


def _fgate_fwd(f_t, b_c, *, name):
    hh, s = f_t.shape
    tb = _pick(s, 512)
    nb = s // tb

    def body(f_ref, b_ref, o_ref, carry):
        t = pl.program_id(0)

        @pl.when(t == 0)
        def _():
            carry[...] = jnp.zeros_like(carry)

        lf = -_softplus(-(f_ref[...] + b_ref[...]))
        row = lax.broadcasted_iota(jnp.int32, (tb, tb), 0)
        col = lax.broadcasted_iota(jnp.int32, (tb, tb), 1)
        cum = _dotf(lf, (row <= col).astype(F32), pieces=3) + carry[:, 0:1]
        o_ref[...] = cum
        carry[:, 0:1] = cum[:, tb - 1:tb]

    return pl.pallas_call(
        body, name=name, grid=(nb,),
        in_specs=[pl.BlockSpec((hh, tb), lambda t: (0, t)), pl.BlockSpec((hh, 1), lambda t: (0, 0))],
        out_specs=pl.BlockSpec((hh, tb), lambda t: (0, t)),
        out_shape=jax.ShapeDtypeStruct((hh, s), F32),
        scratch_shapes=[pltpu.VMEM((hh, LANES), F32)],
        compiler_params=_cp(("arbitrary",)),
    )(f_t, b_c)


def _fgate_bwd(dcum_q_t, dcum_k_t, f_t, b_c, *, name):
    hh, s = f_t.shape
    tb = _pick(s, 512)
    nb = s // tb

    def body(dq_ref, d_ref, f_ref, b_ref, df_ref, db_ref, carry):
        t = pl.program_id(0)

        @pl.when(t == 0)
        def _():
            carry[...] = jnp.zeros_like(carry)
            db_ref[...] = jnp.zeros_like(db_ref)

        d = d_ref[...] + dq_ref[...]
        row = lax.broadcasted_iota(jnp.int32, (tb, tb), 0)
        col = lax.broadcasted_iota(jnp.int32, (tb, tb), 1)
        rev = _dotf(d, (row >= col).astype(F32), pieces=3) + carry[:, 0:1]
        df = rev * _sigmoid(-(f_ref[...] + b_ref[...]))
        df_ref[...] = df
        db_ref[...] += jnp.sum(df, axis=1, keepdims=True)
        carry[:, 0:1] = rev[:, 0:1]

    blk = pl.BlockSpec((hh, tb), lambda t: (0, nb - 1 - t))
    return pl.pallas_call(
        body, name=name, grid=(nb,),
        in_specs=[blk, blk, blk, pl.BlockSpec((hh, 1), lambda t: (0, 0))],
        out_specs=[blk, pl.BlockSpec((hh, 1), lambda t: (0, 0))],
        out_shape=[jax.ShapeDtypeStruct((hh, s), F32), jax.ShapeDtypeStruct((hh, 1), F32)],
        scratch_shapes=[pltpu.VMEM((hh, LANES), F32)],
        compiler_params=_cp(("arbitrary",)),
    )(dcum_q_t, dcum_k_t, f_t, b_c)


def _fox_tile(s):
    return min(512, max(s // 2, 8))


def _tri_tables(nq, kv_major):
    if kv_major:
        pairs = [(i, j) for j in range(nq) for i in range(j, nq)]
    else:
        pairs = [(i, j) for i in range(nq) for j in range(i + 1)]
    return (jnp.asarray([p[0] for p in pairs], jnp.int32), jnp.asarray([p[1] for p in pairs], jnp.int32))


def _lane_tile(col, width):
    return col if width == LANES else jnp.tile(col, (1, width // LANES))


def _flash_fwd(qs, kn, qkvg, ck, *, name, rider=None):
    s = qs.shape[0]
    tt = _fox_tile(s)
    nq = s // tt
    itab, jtab = _tri_tables(nq, kv_major=False)
    v0 = 2 * FOX_D // LANES

    def body(itab_ref, jtab_ref, q_ref, k_ref, v_ref, ck_ref, o_ref, lse_ref, m_sc, l_sc, acc_sc):
        t = pl.program_id(1)
        i, j = itab_ref[t], jtab_ref[t]

        @pl.when(j == 0)
        def _():
            m_sc[...] = jnp.full_like(m_sc, NEG_INF)
            l_sc[...] = jnp.zeros_like(l_sc)
            acc_sc[...] = jnp.zeros_like(acc_sc)

        low = lax.broadcasted_iota(jnp.int32, (tt, LANES), 1) < FOX_HD

        def step(diagonal):
            q2, k2 = q_ref[...], k_ref[...]
            v2 = v_ref[...].astype(BF16)
            alphas, outs = [], []
            for hh in range(2):
                qh = jnp.where(low if hh == 0 else jnp.logical_not(low), q2, jnp.zeros_like(q2))
                sc = lax.dot_general(qh, k2, NT, preferred_element_type=F32) - ck_ref[0][hh:hh + 1, :]
                if diagonal:
                    row = lax.broadcasted_iota(jnp.int32, sc.shape, 0)
                    col = lax.broadcasted_iota(jnp.int32, sc.shape, 1)
                    sc = jnp.where(row >= col, sc, NEG_INF)
                m_prev = m_sc[hh]
                m_new = jnp.maximum(m_prev, jnp.max(sc, axis=1, keepdims=True))
                alpha = jnp.exp(m_prev - m_new)
                p = jnp.exp(sc - _lane_tile(m_new, tt))
                l_sc[hh] = alpha * l_sc[hh] + jnp.sum(p, axis=1, keepdims=True)
                m_sc[hh] = m_new
                alphas.append(alpha)
                outs.append(lax.dot_general(p.astype(BF16), v2, NN, preferred_element_type=F32))
            acc_sc[...] = jnp.where(low, alphas[0], alphas[1]) * acc_sc[...] + jnp.where(low, outs[0], outs[1])

        @pl.when(j < i)
        def _():
            step(False)

        @pl.when(j == i)
        def _():
            step(True)
            o_ref[...] = acc_sc[...] / jnp.where(low, l_sc[0], l_sc[1])
            lse_ref[0] = jnp.concatenate([m_sc[hh][:, 0:1] + jnp.log(l_sc[hh][:, 0:1]) for hh in range(2)], axis=1)

    outs, riding = _call(
        body, name=name, grid=(FOX_PAIRS, int(itab.shape[0])), prefetch=(itab, jtab),
        in_specs=[pl.BlockSpec((tt, LANES), lambda p, t, it, jt: (it[t], p)),
                  pl.BlockSpec((tt, LANES), lambda p, t, it, jt: (jt[t], p)),
                  pl.BlockSpec((tt, LANES), lambda p, t, it, jt: (jt[t], v0 + p)),
                  pl.BlockSpec((1, 2, tt), lambda p, t, it, jt: (p, 0, jt[t]))],
        out_specs=[pl.BlockSpec((tt, LANES), lambda p, t, it, jt: (it[t], p)),
                   pl.BlockSpec((1, tt, 2), lambda p, t, it, jt: (p, it[t], 0))],
        scratch_shapes=[pltpu.VMEM((2, tt, LANES), F32), pltpu.VMEM((2, tt, LANES), F32), pltpu.VMEM((tt, LANES), F32)],
        out_shape=[jax.ShapeDtypeStruct((s, FOX_D), F32), jax.ShapeDtypeStruct((FOX_PAIRS, s, 2), F32)],
        sem=("parallel", "arbitrary"), rider=rider, args=(qs, kn, qkvg, ck))
    return outs + [riding]


def _flash_bwd(qs, kn, qkvg, do, lse_t, delta_t, ck_c, *, name, rider=None):
    s = qs.shape[0]
    tt = _fox_tile(s)
    nq = s // tt
    nl = tt // LANES
    itab, jtab = _tri_tables(nq, kv_major=True)
    nsteps = itab.shape[0]
    v0 = 2 * FOX_D // LANES

    def body(itab_ref, jtab_ref, q_ref, k_ref, v_ref, do_ref, lse_ref, dl_ref, ck_ref,
             dq_ref, dk_ref, dv_ref, dcq_ref, dck_ref, dqt_sc, rs_sc, dk_sc, dv_sc, dc_sc, kt_sc, ckb_sc):
        t = pl.program_id(1)
        i, j = itab_ref[t], jtab_ref[t]

        @pl.when(t == 0)
        def _():
            dqt_sc[...] = jnp.zeros_like(dqt_sc)
            rs_sc[...] = jnp.zeros_like(rs_sc)

        @pl.when(i == j)
        def _():
            dk_sc[...] = jnp.zeros_like(dk_sc)
            dv_sc[...] = jnp.zeros_like(dv_sc)
            dc_sc[...] = jnp.zeros_like(dc_sc)
            kt_sc[...] = k_ref[...].astype(F32).T.astype(BF16)
            for hh in range(2):
                ckb_sc[hh] = jnp.broadcast_to(ck_ref[0][:, hh:hh + 1], (tt, LANES))

        low = lax.broadcasted_iota(jnp.int32, (tt, LANES), 1) < FOX_HD
        top = lax.broadcasted_iota(jnp.int32, (LANES, tt), 0) < FOX_HD

        def step(diagonal):
            q2, k2, kt = q_ref[...], k_ref[...], kt_sc[...]
            v2 = v_ref[...].astype(BF16)
            do2 = do_ref[...].astype(BF16)
            dqs, dks, dvs = [], [], []
            for hh in range(2):
                sel = low if hh == 0 else jnp.logical_not(low)
                qh = jnp.where(sel, q2, jnp.zeros_like(q2))
                doh = jnp.where(sel, do2, jnp.zeros_like(do2))
                st = lax.dot_general(k2, qh, NT, preferred_element_type=F32)
                st = st - _lane_tile(ckb_sc[hh], tt) - lse_ref[0][hh:hh + 1, :]
                if diagonal:
                    key = lax.broadcasted_iota(jnp.int32, st.shape, 0)
                    qry = lax.broadcasted_iota(jnp.int32, st.shape, 1)
                    st = jnp.where(qry >= key, st, NEG_INF)
                pt = jnp.exp(st)
                dpt = lax.dot_general(v2, doh, NT, preferred_element_type=F32)
                dst = pt * (dpt - dl_ref[0][hh:hh + 1, :])
                ptb, dstb = pt.astype(BF16), dst.astype(BF16)
                dvs.append(lax.dot_general(ptb, do2, NN, preferred_element_type=F32))
                dks.append(lax.dot_general(dstb, q2, NN, preferred_element_type=F32))
                dqs.append(lax.dot_general(kt, dstb, NN, preferred_element_type=F32))
                rs_sc[hh, i] += jnp.sum(dst, axis=0, keepdims=True)
                part = dst[:, 0:LANES]
                for b in range(1, nl):
                    part = part + dst[:, b * LANES:(b + 1) * LANES]
                dc_sc[hh] += part
            dv_sc[...] += jnp.where(low, dvs[0], dvs[1])
            dk_sc[...] += jnp.where(low, dks[0], dks[1])
            dqt_sc[i] += jnp.where(top, dqs[0], dqs[1])

        @pl.when(j < i)
        def _():
            step(False)

        @pl.when(j == i)
        def _():
            step(True)

        @pl.when(i == nq - 1)
        def _():
            dk_ref[...] = dk_sc[...]
            dv_ref[...] = dv_sc[...]
            dck_ref[0] = -jnp.concatenate([jnp.sum(dc_sc[hh], axis=1, keepdims=True) for hh in range(2)], axis=1)

        @pl.when(t == nsteps - 1)
        def _():
            for b in range(nq):
                dq_ref[pl.ds(b * tt, tt), :] = dqt_sc[b].T * FOX_SCALE
                dcq_ref[0, :, pl.ds(b * tt, tt)] = jnp.concatenate([rs_sc[hh, b] for hh in range(2)], axis=0)

    qside = pl.BlockSpec((tt, LANES), lambda p, t, it, jt: (it[t], p))
    kside = pl.BlockSpec((tt, LANES), lambda p, t, it, jt: (jt[t], p))
    qstat = pl.BlockSpec((1, 2, tt), lambda p, t, it, jt: (p, 0, it[t]))
    kstat = pl.BlockSpec((1, tt, 2), lambda p, t, it, jt: (p, jt[t], 0))
    outs, riding = _call(
        body, name=name, grid=(FOX_PAIRS, nsteps), prefetch=(itab, jtab),
        in_specs=[qside, kside, pl.BlockSpec((tt, LANES), lambda p, t, it, jt: (jt[t], v0 + p)), qside, qstat, qstat, kstat],
        out_specs=[pl.BlockSpec((s, LANES), lambda p, t, it, jt: (0, p)), kside, kside,
                   pl.BlockSpec((1, 2, s), lambda p, t, it, jt: (p, 0, 0)), kstat],
        scratch_shapes=[pltpu.VMEM((nq, LANES, tt), F32), pltpu.VMEM((2, nq, 1, tt), F32), pltpu.VMEM((tt, LANES), F32),
                        pltpu.VMEM((tt, LANES), F32), pltpu.VMEM((2, tt, LANES), F32), pltpu.VMEM((LANES, tt), BF16),
                        pltpu.VMEM((2, tt, LANES), F32)],
        out_shape=[jax.ShapeDtypeStruct((s, FOX_D), F32), jax.ShapeDtypeStruct((s, FOX_D), F32),
                   jax.ShapeDtypeStruct((s, FOX_D), F32), jax.ShapeDtypeStruct((FOX_PAIRS, 2, s), F32),
                   jax.ShapeDtypeStruct((FOX_PAIRS, s, 2), F32)],
        sem=("parallel", "arbitrary"), rider=rider, args=(qs, kn, qkvg, do, lse_t, delta_t, ck_c))
    return outs + [riding]


def _ogate_fwd(o, qkvg, *, name):
    s = o.shape[0]
    tr = _pick(s, 512, 8)

    def body(o_ref, g_ref, out_ref):
        out_ref[...] = (o_ref[...] * _sigmoid(g_ref[...])).astype(BF16)

    tile = pl.BlockSpec((tr, FOX_D), lambda i: (i, 0))
    return pl.pallas_call(
        body, name=name, grid=(s // tr,), in_specs=[tile, pl.BlockSpec((tr, FOX_D), lambda i: (i, 3))],
        out_specs=tile, out_shape=jax.ShapeDtypeStruct((s, FOX_D), BF16), compiler_params=_cp(("parallel",)),
    )(o, qkvg)


def _ogate_bwd(dog, o, qkvg, *, name):
    s = o.shape[0]
    tr = _pick(s, 512, 8)

    def body(dog_ref, o_ref, g_ref, do_ref, dg_ref, dl_ref):
        sg = _sigmoid(g_ref[...])
        ov = o_ref[...]
        dog_v = dog_ref[...]
        do = dog_v * sg
        do_ref[...] = do
        dg_ref[...] = (dog_v * ov * sg * (1.0 - sg)).astype(BF16)
        dl_ref[...] = _dotf(do * ov, _group_matrix(FOX_D, FOX_HD))

    tile = pl.BlockSpec((tr, FOX_D), lambda i: (i, 0))
    return pl.pallas_call(
        body, name=name, grid=(s // tr,), in_specs=[tile, tile, pl.BlockSpec((tr, FOX_D), lambda i: (i, 3))],
        out_specs=[tile, tile, pl.BlockSpec((tr, FOX_H), lambda i: (i, 0))],
        out_shape=[jax.ShapeDtypeStruct((s, FOX_D), F32), jax.ShapeDtypeStruct((s, FOX_D), BF16),
                   jax.ShapeDtypeStruct((s, FOX_H), F32)],
        compiler_params=_cp(("parallel",)),
    )(dog, o, qkvg)


def _loss_head(h, g, target, *, name):
    s, d = h.shape
    tr = _pick(s, 512, 8)

    def body(h_ref, g_ref, t_ref, loss_ref, dh_ref, dg_ref):
        i = pl.program_id(0)
        x = h_ref[...]
        gv = g_ref[...]
        r = lax.rsqrt(jnp.mean(x * x, axis=-1, keepdims=True) + EPS)
        xh = x * r
        err = xh * gv - t_ref[...]
        part = 0.5 * jnp.sum(jnp.sum(err * err, axis=1, keepdims=True) * (1.0 / d), axis=0, keepdims=True)
        dy = err * (1.0 / d)
        dyg = dy * gv
        dh_ref[...] = r * (dyg - xh * jnp.mean(dyg * xh, axis=-1, keepdims=True))
        dgp = jnp.sum(dy * xh, axis=0, keepdims=True)

        @pl.when(i == 0)
        def _():
            loss_ref[...] = jnp.zeros_like(loss_ref) + part
            dg_ref[...] = dgp

        @pl.when(i > 0)
        def _():
            loss_ref[...] += part
            dg_ref[...] += dgp

    tile = pl.BlockSpec((tr, d), lambda i: (i, 0))
    vec = pl.BlockSpec((1, d), lambda i: (0, 0))
    return pl.pallas_call(
        body, name=name, grid=(s // tr,), in_specs=[tile, vec, tile],
        out_specs=[pl.BlockSpec((1, LANES), lambda i: (0, 0)), tile, vec],
        out_shape=[jax.ShapeDtypeStruct((1, LANES), F32), jax.ShapeDtypeStruct((s, d), F32),
                   jax.ShapeDtypeStruct((1, d), F32)],
        compiler_params=_cp(("arbitrary",)),
    )(h, g, target)


def _adamw(w, g, m, v, *, name):
    rows, cols = w.shape
    tr = _pick(rows, 256, 8)
    c1 = 1.0 - ADAM_B1 ** ADAM_STEP
    c2 = 1.0 - ADAM_B2 ** ADAM_STEP

    def body(w_ref, g_ref, m_ref, v_ref, d_ref, nm_ref, nv_ref):
        gv = g_ref[...]
        nm = ADAM_B1 * m_ref[...] + (1.0 - ADAM_B1) * gv
        nv = ADAM_B2 * v_ref[...] + (1.0 - ADAM_B2) * (gv * gv)
        d_ref[...] = -ADAM_LR * ((nm / c1) / (jnp.sqrt(nv / c2) + ADAM_EPS) + ADAM_WD * w_ref[...])
        nm_ref[...] = nm
        nv_ref[...] = nv

    tile = pl.BlockSpec((tr, cols), lambda i: (i, 0))
    shp = jax.ShapeDtypeStruct((rows, cols), F32)
    return pl.pallas_call(
        body, name=name, grid=(rows // tr,), in_specs=[tile] * 4, out_specs=[tile] * 3, out_shape=[shp] * 3,
        compiler_params=_cp(("parallel",)),
    )(w, g, m, v)


ANY = pl.BlockSpec(memory_space=pl.ANY)
N_DEV = 8


def _coords():
    return lax.axis_index("x"), lax.axis_index("y"), lax.axis_index("c")


def _other_chips(x, y):
    return [(1 - x, y), (x, 1 - y), (1 - x, 1 - y)]


def _allgather_small(buf, *, name, with_sum):
    rows = buf.shape[0]

    def body(*refs):
        if with_sum:
            x_ref, out_ref, sum_ref, send_sems, recv_sems = refs
        else:
            x_ref, out_ref, send_sems, recv_sems = refs
        x, y, c = _coords()
        me = 4 * x + 2 * y + c
        out_ref[me] = x_ref[...]
        copies = []
        for rel in range(1, N_DEV):
            px = (1 - x) if rel & 4 else x
            py = (1 - y) if rel & 2 else y
            pc = (1 - c) if rel & 1 else c
            cp = pltpu.make_async_remote_copy(
                src_ref=x_ref, dst_ref=out_ref.at[me], send_sem=send_sems.at[rel - 1], recv_sem=recv_sems.at[rel - 1],
                device_id=(px, py, pc), device_id_type=MESH)
            cp.start()
            copies.append(cp)
        for cp in copies:
            cp.wait()
        if with_sum:
            acc = out_ref[0]
            for k in range(1, N_DEV):
                acc = acc + out_ref[k]
            sum_ref[...] = acc

    slots = jax.ShapeDtypeStruct((N_DEV, rows, LANES), F32)
    vm = pl.BlockSpec(memory_space=pltpu.VMEM)
    out_shape = [slots, jax.ShapeDtypeStruct((rows, LANES), F32)] if with_sum else [slots]
    return pl.pallas_call(
        body, name=name, in_specs=[vm], out_specs=[vm] * len(out_shape), out_shape=out_shape,
        scratch_shapes=[pltpu.SemaphoreType.DMA((N_DEV - 1,)), pltpu.SemaphoreType.DMA((N_DEV - 1,))],
    )(buf)


class _Gather:
    per_array = 6

    def __init__(self, arrays):
        self.arrays = list(arrays)

    def out_shapes(self):
        return [jax.ShapeDtypeStruct((4,) + a.shape, a.dtype) for a in self.arrays]

    @staticmethod
    def _ici(ins, outs, send_sems, recv_sems, t, j, px, py, c, slot):
        return pltpu.make_async_remote_copy(
            src_ref=ins[t].at[c], dst_ref=outs[t].at[slot, c], send_sem=send_sems.at[6 * t + j],
            recv_sem=recv_sems.at[6 * t + j], device_id=(px, py, c), device_id_type=MESH)

    @staticmethod
    def _d2d(outs, send_sems, recv_sems, t, j, kj, half, sibling):
        return pltpu.make_async_remote_copy(
            src_ref=outs[t].at[kj, half], dst_ref=outs[t].at[kj, half], send_sem=send_sems.at[6 * t + 3 + j],
            recv_sem=recv_sems.at[6 * t + 3 + j], device_id=sibling, device_id_type=MESH)

    def start(self, ins, outs, send_sems, recv_sems):
        x, y, c = _coords()
        for t in range(len(ins)):
            for j, (px, py) in enumerate(_other_chips(x, y)):
                self._ici(ins, outs, send_sems, recv_sems, t, j, px, py, c, 2 * x + y).start()

    def finish(self, ins, outs, send_sems, recv_sems):
        x, y, c = _coords()
        chips = _other_chips(x, y)
        sibling = (x, y, 1 - c)
        started = []
        for t in range(len(ins)):
            for j, (px, py) in enumerate(chips):
                ici = self._ici(ins, outs, send_sems, recv_sems, t, j, px, py, c, 2 * px + py)
                ici.wait_recv()
                fwd = self._d2d(outs, send_sems, recv_sems, t, j, 2 * px + py, c, sibling)
                fwd.start()
                started += [ici, fwd]
        for t in range(len(ins)):
            for j, (px, py) in enumerate(chips):
                self._d2d(outs, send_sems, recv_sems, t, j, 2 * px + py, 1 - c, sibling).wait_recv()
        for cp in started:
            cp.wait_send()


class _Exchange:
    per_array = 7

    def __init__(self, arrays):
        self.arrays = list(arrays)

    def out_shapes(self):
        return [jax.ShapeDtypeStruct((7,) + a.shape[2:], a.dtype) for a in self.arrays]

    @staticmethod
    def _copies(ins, outs, send_sems, recv_sems):
        x, y, c = _coords()
        for t in range(len(ins)):
            for rel in range(1, N_DEV):
                px = (1 - x) if rel & 4 else x
                py = (1 - y) if rel & 2 else y
                pc = (1 - c) if rel & 1 else c
                yield pltpu.make_async_remote_copy(
                    src_ref=ins[t].at[2 * px + py, pc], dst_ref=outs[t].at[rel - 1], send_sem=send_sems.at[7 * t + rel - 1],
                    recv_sem=recv_sems.at[7 * t + rel - 1], device_id=(px, py, pc), device_id_type=MESH)

    def start(self, ins, outs, send_sems, recv_sems):
        for cp in self._copies(ins, outs, send_sems, recv_sems):
            cp.start()

    def finish(self, ins, outs, send_sems, recv_sems):
        for cp in self._copies(ins, outs, send_sems, recv_sems):
            cp.wait()


def _call(body, *, name, grid, in_specs, out_specs, out_shape, scratch_shapes, args, sem, rider=None, prefetch=()):
    n_in, n_out, n_pre = len(in_specs), len(out_specs), len(prefetch)
    n_c = len(rider.arrays) if rider is not None else 0

    def wrapped(*refs):
        pre, rest = refs[:n_pre], refs[n_pre:]
        ins, cins = rest[:n_in], rest[n_in:n_in + n_c]
        outs = rest[n_in + n_c:n_in + n_c + n_out]
        couts = rest[n_in + n_c + n_out:n_in + 2 * n_c + n_out]
        scratch = rest[n_in + 2 * n_c + n_out:]
        if rider is None:
            body(*pre, *ins, *outs, *scratch)
            return
        send_sems, recv_sems = scratch[-2:]
        ids = [pl.program_id(a) for a in range(len(grid))]
        first = functools.reduce(jnp.logical_and, [i == 0 for i in ids])
        last = functools.reduce(jnp.logical_and, [i == g - 1 for i, g in zip(ids, grid)])

        @pl.when(first)
        def _():
            rider.start(cins, couts, send_sems, recv_sems)

        body(*pre, *ins, *outs, *scratch[:-2])

        @pl.when(last)
        def _():
            rider.finish(cins, couts, send_sems, recv_sems)

    if rider is not None:
        nsem = rider.per_array * n_c
        in_specs = list(in_specs) + [ANY] * n_c
        out_specs = list(out_specs) + [ANY] * n_c
        out_shape = list(out_shape) + rider.out_shapes()
        scratch_shapes = list(scratch_shapes) + [pltpu.SemaphoreType.DMA((nsem,)), pltpu.SemaphoreType.DMA((nsem,))]
        args = list(args) + rider.arrays
        sem = ("arbitrary",) * len(grid)
    if n_pre:
        res = pl.pallas_call(
            wrapped, name=name, out_shape=out_shape, compiler_params=_cp(sem),
            grid_spec=pltpu.PrefetchScalarGridSpec(num_scalar_prefetch=n_pre, grid=grid, in_specs=in_specs,
                                                   out_specs=out_specs, scratch_shapes=scratch_shapes),
        )(*prefetch, *args)
    else:
        res = pl.pallas_call(
            wrapped, name=name, grid=grid, in_specs=in_specs, out_specs=out_specs, out_shape=out_shape,
            scratch_shapes=scratch_shapes, compiler_params=_cp(sem),
        )(*args)
    return list(res[:n_out]), list(res[n_out:])


def _run_rider(rider, *, name):
    n = len(rider.arrays)

    def body(*refs):
        ins, outs = refs[:n], refs[n:2 * n]
        send_sems, recv_sems = refs[2 * n:]
        rider.start(ins, outs, send_sems, recv_sems)
        rider.finish(ins, outs, send_sems, recv_sems)

    nsem = rider.per_array * n
    return pl.pallas_call(
        body, name=name, in_specs=[ANY] * n, out_specs=[ANY] * n, out_shape=rider.out_shapes(),
        scratch_shapes=[pltpu.SemaphoreType.DMA((nsem,)), pltpu.SemaphoreType.DMA((nsem,))],
    )(*rider.arrays)


def _sibling_swap(arrs, *, name):
    n = len(arrs)

    def body(*refs):
        ins, outs = refs[:n], refs[n:2 * n]
        send_sems, recv_sems = refs[2 * n:]
        x, y, c = _coords()
        copies = []
        for t in range(n):
            cp = pltpu.make_async_remote_copy(
                src_ref=ins[t], dst_ref=outs[t], send_sem=send_sems.at[t], recv_sem=recv_sems.at[t],
                device_id=(x, y, 1 - c), device_id_type=MESH)
            cp.start()
            copies.append(cp)
        for cp in copies:
            cp.wait()

    return pl.pallas_call(
        body, name=name, in_specs=[ANY] * n, out_specs=[ANY] * n,
        out_shape=[jax.ShapeDtypeStruct(a.shape, a.dtype) for a in arrs],
        scratch_shapes=[pltpu.SemaphoreType.DMA((n,)), pltpu.SemaphoreType.DMA((n,))],
    )(*arrs)


def _add_selected(stack, others, sel, *, name):
    _, m, cols = stack.shape
    q = others.shape[0]
    tr = _pick(m, 256, 16)

    def body(sel_ref, s_ref, o_ref, out_ref):
        acc = s_ref[0].astype(F32)
        for i in range(q):
            acc = acc + o_ref[i].astype(F32)
        out_ref[...] = acc

    return pl.pallas_call(
        body, name=name,
        grid_spec=pltpu.PrefetchScalarGridSpec(
            num_scalar_prefetch=1, grid=(m // tr,),
            in_specs=[pl.BlockSpec((1, tr, cols), lambda i, sel_ref: (sel_ref[0], i, 0)),
                      pl.BlockSpec((q, tr, cols), lambda i, sel_ref: (0, i, 0))],
            out_specs=pl.BlockSpec((tr, cols), lambda i, sel_ref: (i, 0))),
        out_shape=jax.ShapeDtypeStruct((m, cols), F32),
        compiler_params=_cp(("parallel",)),
    )(sel, stack, others)


BIG = ("ssd_w_in", "ssd_w_out", "fox_w_in", "fox_w_out", "ffn_w_up", "ffn_w_down")
COL_SHARDED = ("ssd_w_in", "fox_w_in", "ffn_w_up")
SMALL = (("mix_norm_g", (4, 1024)), ("ffn_norm_g", (4, 1024)), ("ssd_conv_w", (2, 4, 3072)), ("ssd_conv_b", (2, 3072)),
         ("ssd_dt_bias", (2, 32)), ("ssd_a_log", (2, 32)), ("ssd_d", (2, 32)), ("ssd_norm_g", (2, 2048)),
         ("fox_b_f", (2, 16)), ("fox_q_norm_g", (2, 64)), ("fox_k_norm_g", (2, 64)), ("ffn_conv_w", (4, 3, 2816)),
         ("ffn_conv_b", (4, 2816)), ("final_norm_g", (1024,)), ("loss", (1,)))
NAMES = ("mix_norm_g", "ffn_norm_g", "ssd_w_in", "ssd_conv_w", "ssd_conv_b", "ssd_dt_bias", "ssd_a_log", "ssd_d",
         "ssd_norm_g", "ssd_w_out", "fox_w_in", "fox_b_f", "fox_q_norm_g", "fox_k_norm_g", "fox_w_out", "ffn_w_up",
         "ffn_conv_w", "ffn_conv_b", "ffn_w_down", "final_norm_g")


def _pack(parts):
    flat = jnp.concatenate([jnp.reshape(p, (-1,)).astype(F32) for p in parts])
    rows = -(-flat.shape[0] // (8 * LANES)) * 8
    return jnp.pad(flat, (0, rows * LANES - flat.shape[0])).reshape(rows, LANES)


def _unpack(buf, shapes):
    flat = buf.reshape(-1)
    out, off = [], 0
    for shp in shapes:
        size = 1
        for d in shp:
            size *= d
        out.append(flat[off:off + size].reshape(shp))
        off += size
    return out


def _pad_lanes(a):
    return jnp.pad(a, ((0, 0), (0, LANES - a.shape[1])))


def _pad8(w):
    return jnp.pad(w, ((0, 8 - w.shape[0]), (0, 0)))


def _ssd_forward(h, p, name, rider=None):
    s = h.shape[0]
    hn = _rms_fwd(h, p["mix_g"], gw=D_MODEL, ncol=1, name=f"{name}_norm")
    zx = _matmul(hn, p["w_zx"], mode="nn", name=f"{name}_proj")
    dtp = _matmul(hn, p["w_dt"], mode="nn", name=f"{name}_proj_dt")
    xbc, _ = _conv_fwd(zx, p["conv_w8"], p["conv_b"], kw=SSD_K, width=SSD_CONV_DIM, u_col0=SSD_DI, name=f"{name}_conv")
    dt3 = dtp[:, :SSD_H].reshape(s, SSD_G, SSD_HPG)
    dtg, dtg_t = jnp.transpose(dt3, (1, 0, 2)), jnp.transpose(dt3, (1, 2, 0))
    sp = (p["bias_r"], p["bias_c"], p["alog_r"], p["alog_c"], p["d_r"])
    y, hprev, riding = _ssd_fwd(xbc, dtg, dtg_t, *sp, name=f"{name}_scan", rider=rider)
    y2 = _rms_fwd(y, p["norm_g"], gw=SSD_DI // SSD_G, ncol=SSD_G, z=zx, name=f"{name}_gnorm")
    out = _matmul(y2, p["w_out"], mode="nn", add=h, name=f"{name}_out")
    return out, dict(h=h, hn=hn, zx=zx, xbc=xbc, dtg=dtg, dtg_t=dtg_t, y=y, hprev=hprev, y2=y2), riding


def _ssd_backward(dh1, p, a, name, ride=()):
    s = dh1.shape[0]
    g = {}
    dy2 = _matmul(dh1, p["w_out"], mode="nt", name=f"{name}_out_dx")
    g["w_out"] = _matmul(a["y2"], dh1, mode="tn", out_dtype=BF16, name=f"{name}_out_dw")
    rider = _Exchange(list(ride) + [_to_slabs(g["w_out"], False)])
    dy, dz, g["norm_g"] = _rms_bwd(a["y"], p["norm_g"], dy2, gw=SSD_DI // SSD_G, ncol=SSD_G, z=a["zx"], name=f"{name}_gnorm_b")
    sp = (p["bias_r"], p["bias_c"], p["alog_r"], p["alog_c"], p["d_r"])
    dx, dbm, dcm, ddt, g["dt_bias"], g["a_log"], g["d"], riding = _ssd_bwd(
        a["xbc"], a["dtg"], a["dtg_t"], *sp, a["hprev"], dy, name=f"{name}_scan_b", rider=rider)
    dxbc, dwb, _ = _conv_bwd(a["zx"], p["conv_w8"], p["conv_b"], [dx, dbm, dcm], kw=SSD_K, width=SSD_CONV_DIM,
                             u_col0=SSD_DI, name=f"{name}_conv_b")
    g["conv_w"], g["conv_b"] = dwb[:SSD_K], dwb[7]
    dzx = jnp.concatenate([dz.astype(BF16), dxbc], axis=1)
    ddtp = _pad_lanes(jnp.transpose(ddt, (1, 0, 2)).reshape(s, SSD_H))
    dhn = _matmul(dzx, p["w_zx"], mode="nt", name=f"{name}_proj_dx")
    dhn = _matmul(ddtp, p["w_dt"], mode="nt", add=dhn, name=f"{name}_proj_dt_dx")
    dw_zx = _matmul(a["hn"], dzx, mode="tn", out_dtype=BF16, name=f"{name}_proj_dw")
    dw_dt = _matmul(a["hn"], ddtp, mode="tn", out_dtype=BF16, name=f"{name}_proj_dt_dw")
    g["w_in"] = jnp.concatenate([dw_zx, dw_dt[:, :SSD_H]], axis=1)
    dh, g["mix_g"] = _rms_bwd(a["h"], p["mix_g"], dhn, gw=D_MODEL, ncol=1, add=dh1, name=f"{name}_norm_b")
    g["w_out_received"] = riding[-1]
    return dh, g, riding[:-1]


def _fox_forward(h, p, name, rider=None):
    s = h.shape[0]
    hn = _rms_fwd(h, p["mix_g"], gw=D_MODEL, ncol=1, name=f"{name}_norm")
    qkvg = _matmul(hn, p["w_qkvg"], mode="nn", name=f"{name}_proj")
    fp = _matmul(hn, p["w_f"], mode="nn", name=f"{name}_proj_f")
    qs = _rms_fwd(qkvg, p["gq"] * FOX_SCALE, gw=FOX_D, ncol=1, x_col0=0, sub=FOX_HD, name=f"{name}_qnorm")
    kn = _rms_fwd(qkvg, p["gk"], gw=FOX_D, ncol=1, x_col0=1, sub=FOX_HD, name=f"{name}_knorm")
    f_t = jnp.transpose(fp[:, :FOX_H])
    cum_t = _fgate_fwd(f_t, p["b_f"], name=f"{name}_fgate")
    ck = cum_t.reshape(FOX_PAIRS, 2, s)
    o, lse, riding = _flash_fwd(qs, kn, qkvg, ck, name=f"{name}_attn", rider=rider)
    og = _ogate_fwd(o, qkvg, name=f"{name}_ogate")
    out = _matmul(og, p["w_out"], mode="nn", add=h, name=f"{name}_out")
    return out, dict(h=h, hn=hn, qkvg=qkvg, qs=qs, kn=kn, f_t=f_t, ck=ck, o=o, lse=lse, og=og), riding


def _fox_backward(dh1, p, a, name, ride=()):
    s = dh1.shape[0]
    g = {}
    dog = _matmul(dh1, p["w_out"], mode="nt", name=f"{name}_out_dx")
    g["w_out"] = _matmul(a["og"], dh1, mode="tn", out_dtype=BF16, name=f"{name}_out_dw")
    rider = _Exchange(list(ride) + [_to_slabs(g["w_out"], False)])
    do, dgate, delta = _ogate_bwd(dog, a["o"], a["qkvg"], name=f"{name}_ogate_b")
    swap = lambda v: jnp.transpose(v, (0, 2, 1))
    dl_t = jnp.transpose(delta.reshape(s, FOX_PAIRS, 2), (1, 2, 0))
    dq, dk, dv, dcq, dck, riding = _flash_bwd(a["qs"], a["kn"], a["qkvg"], do, swap(a["lse"]), dl_t, swap(a["ck"]),
                                              name=f"{name}_attn_b", rider=rider)
    dq_raw, dgq = _rms_bwd(a["qkvg"], p["gq"], dq, gw=FOX_D, ncol=1, x_col0=0, sub=FOX_HD, dx_dtype=BF16, name=f"{name}_qnorm_b")
    dk_raw, dgk = _rms_bwd(a["qkvg"], p["gk"], dk, gw=FOX_D, ncol=1, x_col0=1, sub=FOX_HD, dx_dtype=BF16, name=f"{name}_knorm_b")
    g["gq"] = dgq.reshape(FOX_H, FOX_HD).sum(axis=0)
    g["gk"] = dgk.reshape(FOX_H, FOX_HD).sum(axis=0)
    df_t, dbf = _fgate_bwd(dcq.reshape(FOX_H, s), swap(dck).reshape(FOX_H, s), a["f_t"], p["b_f"], name=f"{name}_fgate_b")
    g["b_f"] = dbf[:, 0]
    dproj = jnp.concatenate([dq_raw, dk_raw, dv.astype(BF16), dgate], axis=1)
    dfp = _pad_lanes(jnp.transpose(df_t))
    dhn = _matmul(dproj, p["w_qkvg"], mode="nt", name=f"{name}_proj_dx")
    dhn = _matmul(dfp, p["w_f"], mode="nt", add=dhn, name=f"{name}_proj_f_dx")
    dw_qkvg = _matmul(a["hn"], dproj, mode="tn", out_dtype=BF16, name=f"{name}_proj_dw")
    dw_f = _matmul(a["hn"], dfp, mode="tn", out_dtype=BF16, name=f"{name}_proj_f_dw")
    g["w_in"] = jnp.concatenate([dw_qkvg, dw_f[:, :FOX_H]], axis=1)
    dh, g["mix_g"] = _rms_bwd(a["h"], p["mix_g"], dhn, gw=D_MODEL, ncol=1, add=dh1, name=f"{name}_norm_b")
    g["w_out_received"] = riding[-1]
    return dh, g, riding[:-1]


def _ffn_forward(h, p, name, rider=None):
    hn = _rms_fwd(h, p["ffn_g"], gw=D_MODEL, ncol=1, name=f"{name}_norm")
    u = _matmul(hn, p["w_up"], mode="nn", name=f"{name}_up")
    act, riding = _conv_fwd(u, p["conv_w8"], p["conv_b"], kw=FFN_K, width=D_FF, u_col0=0, mul_col0=D_FF, out_dtype=BF16,
                            name=f"{name}_glu", rider=rider)
    out = _matmul(act, p["w_down"], mode="nn", add=h, name=f"{name}_down")
    return out, dict(h=h, hn=hn, u=u, act=act), riding


def _ffn_backward(dh2, p, a, name, ride=()):
    g = {}
    dact = _matmul(dh2, p["w_down"], mode="nt", name=f"{name}_down_dx")
    g["w_down"] = _matmul(a["act"], dh2, mode="tn", out_dtype=BF16, name=f"{name}_down_dw")
    du, dwb, riding = _conv_bwd(a["u"], p["conv_w8"], p["conv_b"], dact, kw=FFN_K, width=D_FF, u_col0=0, mul_col0=D_FF,
                                name=f"{name}_glu_b", rider=_Exchange(list(ride)) if ride else None)
    g["conv_w"], g["conv_b"] = dwb[:FFN_K], dwb[7]
    dhn = _matmul(du, p["w_up"], mode="nt", name=f"{name}_up_dx")
    g["w_up"] = _matmul(a["hn"], du, mode="tn", out_dtype=BF16, name=f"{name}_up_dw")
    dh, g["ffn_g"] = _rms_bwd(a["h"], p["ffn_g"], dhn, gw=D_MODEL, ncol=1, add=dh2, name=f"{name}_norm_b")
    return dh, g, riding


def _to_slabs(dw, col_sharded):
    rows, cols = dw.shape
    if col_sharded:
        return jnp.transpose(dw.reshape(rows, 4, cols // 4), (1, 0, 2)).reshape(4, 2, rows // 2, cols // 4)
    return dw.reshape(4, 2, rows // 8, cols)


def kernel(x, mix_norm_g, ffn_norm_g, ssd_w_in, ssd_conv_w, ssd_conv_b, ssd_dt_bias, ssd_a_log, ssd_d, ssd_norm_g, ssd_w_out, fox_w_in, fox_b_f, fox_q_norm_g, fox_k_norm_g, fox_w_out, ffn_w_up, ffn_conv_w, ffn_conv_b, ffn_w_down, final_norm_g, loss_target, m_mix_norm_g, m_ffn_norm_g, m_ssd_w_in, m_ssd_conv_w, m_ssd_conv_b, m_ssd_dt_bias, m_ssd_a_log, m_ssd_d, m_ssd_norm_g, m_ssd_w_out, m_fox_w_in, m_fox_b_f, m_fox_q_norm_g, m_fox_k_norm_g, m_fox_w_out, m_ffn_w_up, m_ffn_conv_w, m_ffn_conv_b, m_ffn_w_down, m_final_norm_g, v_mix_norm_g, v_ffn_norm_g, v_ssd_w_in, v_ssd_conv_w, v_ssd_conv_b, v_ssd_dt_bias, v_ssd_a_log, v_ssd_d, v_ssd_norm_g, v_ssd_w_out, v_fox_w_in, v_fox_b_f, v_fox_q_norm_g, v_fox_k_norm_g, v_fox_w_out, v_ffn_w_up, v_ffn_conv_w, v_ffn_conv_b, v_ffn_w_down, v_final_norm_g):
    w = dict(mix_norm_g=mix_norm_g, ffn_norm_g=ffn_norm_g, ssd_w_in=ssd_w_in, ssd_conv_w=ssd_conv_w, ssd_conv_b=ssd_conv_b,
             ssd_dt_bias=ssd_dt_bias, ssd_a_log=ssd_a_log, ssd_d=ssd_d, ssd_norm_g=ssd_norm_g, ssd_w_out=ssd_w_out,
             fox_w_in=fox_w_in, fox_b_f=fox_b_f, fox_q_norm_g=fox_q_norm_g, fox_k_norm_g=fox_k_norm_g, fox_w_out=fox_w_out,
             ffn_w_up=ffn_w_up, ffn_conv_w=ffn_conv_w, ffn_conv_b=ffn_conv_b, ffn_w_down=ffn_w_down, final_norm_g=final_norm_g)
    m_in = dict(zip(NAMES, (m_mix_norm_g, m_ffn_norm_g, m_ssd_w_in, m_ssd_conv_w, m_ssd_conv_b, m_ssd_dt_bias, m_ssd_a_log,
                            m_ssd_d, m_ssd_norm_g, m_ssd_w_out, m_fox_w_in, m_fox_b_f, m_fox_q_norm_g, m_fox_k_norm_g,
                            m_fox_w_out, m_ffn_w_up, m_ffn_conv_w, m_ffn_conv_b, m_ffn_w_down, m_final_norm_g)))
    v_in = dict(zip(NAMES, (v_mix_norm_g, v_ffn_norm_g, v_ssd_w_in, v_ssd_conv_w, v_ssd_conv_b, v_ssd_dt_bias, v_ssd_a_log,
                            v_ssd_d, v_ssd_norm_g, v_ssd_w_out, v_fox_w_in, v_fox_b_f, v_fox_q_norm_g, v_fox_k_norm_g,
                            v_fox_w_out, v_ffn_w_up, v_ffn_conv_w, v_ffn_conv_b, v_ffn_w_down, v_final_norm_g)))
    cx, cy, cc = _coords()
    chip = 2 * cx + cy
    h = x[0]
    target = loss_target[0]

    conv_shapes = [ssd_conv_w.shape, ffn_conv_w.shape]
    slots = _allgather_small(_pack([ssd_conv_w, ffn_conv_w]), name="gather_conv_w", with_sum=False)[0]
    per_chip = [_unpack(slots[2 * q], conv_shapes) for q in range(4)]
    ssd_conv_full = jnp.concatenate([pc[0] for pc in per_chip], axis=2)
    ffn_conv_full = jnp.concatenate([pc[1] for pc in per_chip], axis=2)
    low = {n: w[n].astype(BF16) for n in BIG}
    sub_weights = dict(ssd=("ssd_w_in", "ssd_w_out"), fox=("fox_w_in", "fox_w_out"), ffn=("ffn_w_up", "ffn_w_down"))

    def shards_of(kind, idx):
        return [low[n][idx].reshape(2, low[n].shape[1] // 2, low[n].shape[2]) for n in sub_weights[kind]]

    def assemble(kind, idx, gathered):
        full = []
        for n, own, gth in zip(sub_weights[kind], shards_of(kind, idx), gathered):
            gth = lax.dynamic_update_slice(gth, own[None], (chip, 0, 0, 0))
            _, _, half, cols = gth.shape
            if n in COL_SHARDED:
                full.append(jnp.transpose(gth.reshape(4, 2 * half, cols), (1, 0, 2)).reshape(2 * half, 4 * cols))
            else:
                full.append(gth.reshape(8 * half, cols))
        return full

    def ssd_params(j, i, weights):
        w_in, w_out = weights
        g3 = lambda v: v.reshape(SSD_G, 1, SSD_HPG)
        g3c = lambda v: v.reshape(SSD_G, SSD_HPG, 1)
        return dict(mix_g=mix_norm_g[i][None], w_zx=w_in[:, :SSD_ZX], w_dt=_pad_lanes(w_in[:, SSD_ZX:]),
                    conv_w8=_pad8(ssd_conv_full[j]), conv_b=ssd_conv_b[j][None], bias_r=g3(ssd_dt_bias[j]),
                    bias_c=g3c(ssd_dt_bias[j]), alog_r=g3(ssd_a_log[j]), alog_c=g3c(ssd_a_log[j]), d_r=g3(ssd_d[j]),
                    norm_g=ssd_norm_g[j][None], w_out=w_out)

    def fox_params(j, i, weights):
        w_in, w_out = weights
        return dict(mix_g=mix_norm_g[i][None], w_qkvg=w_in[:, :4 * FOX_D], w_f=_pad_lanes(w_in[:, 4 * FOX_D:]),
                    gq=jnp.tile(fox_q_norm_g[j], FOX_H)[None], gk=jnp.tile(fox_k_norm_g[j], FOX_H)[None],
                    b_f=fox_b_f[j][:, None], w_out=w_out)

    def ffn_params(i, weights):
        w_up, w_down = weights
        return dict(ffn_g=ffn_norm_g[i][None], w_up=w_up, conv_w8=_pad8(ffn_conv_full[i]), conv_b=ffn_conv_b[i][None],
                    w_down=w_down)

    order = [("ssd", 0), ("ffn", 0), ("fox", 0), ("ffn", 1), ("ssd", 1), ("ffn", 2), ("fox", 1), ("ffn", 3)]
    fetch = {("ssd", 0): [("ffn", 0)], ("ffn", 0): [("fox", 0)], ("fox", 0): [("ffn", 1), ("ssd", 1), ("ffn", 2)],
             ("ssd", 1): [("fox", 1)], ("fox", 1): [("ffn", 3)]}
    ready = {("ssd", 0): assemble("ssd", 0, _run_rider(_Gather(shards_of("ssd", 0)), name="gather_first"))}
    params, acts = {}, {}
    forward = dict(ssd=_ssd_forward, fox=_fox_forward, ffn=_ffn_forward)
    for kind, idx in order:
        if kind == "ssd":
            params[kind, idx] = ssd_params(idx, 2 * idx, ready.pop((kind, idx)))
        elif kind == "fox":
            params[kind, idx] = fox_params(idx, 2 * idx + 1, ready.pop((kind, idx)))
        else:
            params[kind, idx] = ffn_params(idx, ready.pop((kind, idx)))
        wanted = fetch.get((kind, idx), [])
        rider = _Gather([s for sub in wanted for s in shards_of(*sub)]) if wanted else None
        h, acts[kind, idx], riding = forward[kind](h, params[kind, idx], f"{kind}{idx}", rider=rider)
        for q, sub in enumerate(wanted):
            ready[sub] = assemble(*sub, riding[2 * q:2 * q + 2])
    loss_part, dh, d_final_g = _loss_head(h, final_norm_g[None], target, name="loss_head")

    backward = dict(ssd=_ssd_backward, fox=_fox_backward, ffn=_ffn_backward)
    grad_keys = dict(ssd=("w_in", "w_out"), fox=("w_in", "w_out"), ffn=("w_up", "w_down"))
    sub_g, slabs, received = {}, {}, {}
    waiting = []
    for sub in reversed(order):
        kind = sub[0]
        dh, sub_g[sub], got = backward[kind](dh, params[sub], acts[sub], f"{kind}{sub[1]}",
                                             ride=[slabs[key] for key in waiting])
        received.update(zip(waiting, got))
        waiting = []
        for q, (key, n) in enumerate(zip(grad_keys[kind], sub_weights[kind])):
            slabs[sub, q] = _to_slabs(sub_g[sub][key], n in COL_SHARDED)
            if kind != "ffn" and q == 1:
                received[sub, q] = sub_g[sub]["w_out_received"]
            else:
                waiting.append((sub, q))
    received.update(zip(waiting, _run_rider(_Exchange([slabs[key] for key in waiting]), name="rs_last_exchange")))
    grad_x = dh[None]
    ssd_g, fox_g = [sub_g["ssd", 0], sub_g["ssd", 1]], [sub_g["fox", 0], sub_g["fox", 1]]
    mix_g = [ssd_g[0], fox_g[0], ssd_g[1], fox_g[1]]
    ffn_g = [sub_g["ffn", i] for i in range(DEPTH)]

    me = jnp.reshape(2 * chip + cc, (1,)).astype(jnp.int32)
    finals = {}
    for sub in order:
        for q in range(2):
            _, _, m, cols = slabs[sub, q].shape
            finals[sub, q] = _add_selected(slabs[sub, q].reshape(8, m, cols), received[sub, q], me,
                                           name=f"rs_add_{sub[0]}{sub[1]}_{q}")
    keys = list(finals)
    others = dict(zip(keys, _sibling_swap([finals[key] for key in keys], name="rs_result_swap")))
    grads = {}
    for kind, names in sub_weights.items():
        for q, n in enumerate(names):
            subs = [sub for sub in sorted(set(order)) if sub[0] == kind]
            mine = jnp.stack([finals[sub, q] for sub in subs])
            theirs = jnp.stack([others[sub, q] for sub in subs])
            halves = jnp.stack([jnp.where(cc == 0, mine, theirs), jnp.where(cc == 0, theirs, mine)], axis=1)
            grads[n] = halves.reshape(w[n].shape)
    small = dict(
        mix_norm_g=jnp.concatenate([g["mix_g"] for g in mix_g], axis=0),
        ffn_norm_g=jnp.concatenate([g["ffn_g"] for g in ffn_g], axis=0),
        ssd_conv_w=jnp.stack([g["conv_w"] for g in ssd_g]), ssd_conv_b=jnp.stack([g["conv_b"] for g in ssd_g]),
        ssd_dt_bias=jnp.stack([g["dt_bias"].reshape(SSD_H) for g in ssd_g]),
        ssd_a_log=jnp.stack([g["a_log"].reshape(SSD_H) for g in ssd_g]),
        ssd_d=jnp.stack([g["d"].reshape(SSD_H) for g in ssd_g]),
        ssd_norm_g=jnp.concatenate([g["norm_g"] for g in ssd_g], axis=0),
        fox_b_f=jnp.stack([g["b_f"] for g in fox_g]), fox_q_norm_g=jnp.stack([g["gq"] for g in fox_g]),
        fox_k_norm_g=jnp.stack([g["gk"] for g in fox_g]),
        ffn_conv_w=jnp.stack([g["conv_w"] for g in ffn_g]), ffn_conv_b=jnp.stack([g["conv_b"] for g in ffn_g]),
        final_norm_g=d_final_g[0], loss=loss_part[0, :1])
    _, total = _allgather_small(_pack([small[n] for n, _ in SMALL]), name="reduce_small", with_sum=True)
    for (n, shp), val in zip(SMALL, _unpack(total, [shp for _, shp in SMALL])):
        grads[n] = val
    loss = grads.pop("loss")[0]
    grads["ssd_conv_w"] = lax.dynamic_slice_in_dim(grads["ssd_conv_w"], chip * ssd_conv_w.shape[2], ssd_conv_w.shape[2], axis=2)
    grads["ffn_conv_w"] = lax.dynamic_slice_in_dim(grads["ffn_conv_w"], chip * ffn_conv_w.shape[2], ffn_conv_w.shape[2], axis=2)

    deltas, new_m, new_v = {}, {}, {}
    for n in NAMES:
        shp = w[n].shape
        two_d = (1, shp[0]) if len(shp) == 1 else (-1, shp[-1])
        r2 = lambda a: a.reshape(two_d)
        d, nm, nv = _adamw(r2(w[n]), r2(grads[n]), r2(m_in[n]), r2(v_in[n]), name=f"adamw_{n}")
        deltas[n], new_m[n], new_v[n] = d.reshape(shp), nm.reshape(shp), nv.reshape(shp)
    return (loss, grad_x, *[grads[n] for n in NAMES], *[deltas[n] for n in NAMES], *[new_m[n] for n in NAMES],
            *[new_v[n] for n in NAMES])
```

```python
import functools

import jax
import jax.numpy as jnp
from jax import lax
from jax.experimental import pallas as pl
from jax.experimental.pallas import tpu as pltpu

F32 = jnp.float32
BF16 = jnp.bfloat16
HI = lax.Precision.HIGHEST
MESH = pl.DeviceIdType.MESH

D_MODEL = 1024
DEPTH = 4
EPS = 1e-6
SSD_DI = 2048
SSD_HD = 64
SSD_G = 4
SSD_HPG = 8
SSD_N = 128
SSD_K = 4
CHUNK = 128
SSD_CONV_DIM = 3072
SSD_ZX = SSD_DI + SSD_CONV_DIM
SSD_H = 32
FOX_HD = 64
FOX_H = 16
FOX_D = 1024
D_FF = 2816
FFN_K = 3
LANES = 128
VMEM_LIMIT = 56 * 1024 * 1024

ADAM_LR = 0.001
ADAM_B1 = 0.9
ADAM_B2 = 0.999
ADAM_EPS = 1e-08
ADAM_WD = 0.01
ADAM_STEP = 10

NN = (((1,), (0,)), ((), ()))
NT = (((1,), (1,)), ((), ()))
TN = (((0,), (0,)), ((), ()))


def _pick(n, cap, mult=LANES):
    best = None
    for t in range(mult, min(n, cap) + 1, mult):
        if n % t == 0:
            best = t
    return best if best is not None else n


def _cp(sem):
    return pltpu.CompilerParams(dimension_semantics=sem, vmem_limit_bytes=VMEM_LIMIT)


def _sigmoid(x):
    return jax.nn.sigmoid(x)


def _silu(x):
    return x * _sigmoid(x)


def _dsilu(x):
    s = _sigmoid(x)
    return s * (1.0 + x * (1.0 - s))


def _softplus(x):
    e = jnp.exp(-jnp.abs(x))
    u = 1.0 + e
    l1p = jnp.where(u == 1.0, e, jnp.log(u) * (e / (u - 1.0)))
    return jnp.maximum(x, 0.0) + l1p


def _dotf(a, b, dn=NN, *, onehot="b", pieces=2):
    x, e = (a, b) if onehot == "b" else (b, a)
    e = e.astype(BF16)
    acc = None
    for n in range(pieces):
        hi = x.astype(BF16)
        part = lax.dot_general(hi, e, dn, preferred_element_type=F32) if onehot == "b" else \
            lax.dot_general(e, hi, dn, preferred_element_type=F32)
        acc = part if acc is None else acc + part
        if n + 1 < pieces:
            x = x - hi.astype(F32)
    return acc


def _dotb(a, b, dn=NN):
    return lax.dot_general(a.astype(BF16), b.astype(BF16), dn, preferred_element_type=F32)


def _group_matrix(width, sub, transpose=False):
    ng = width // sub
    shape = (ng, width) if transpose else (width, ng)
    lane = lax.broadcasted_iota(jnp.int32, shape, 1 if transpose else 0)
    grp = lax.broadcasted_iota(jnp.int32, shape, 0 if transpose else 1)
    return (lane // sub == grp).astype(F32)


def _gmean(v, sub):
    width = v.shape[-1]
    if sub == width:
        return jnp.mean(v, axis=-1, keepdims=True)
    s = _dotf(v, _group_matrix(width, sub))
    return _dotf(s, _group_matrix(width, sub, transpose=True)) * (1.0 / sub)


def _matmul(a, b, *, mode, name, out_dtype=F32, add=None):
    a_planes = a.shape[0] if (mode == "nt" and a.ndim == 3) else 0
    b_planes = b.shape[0] if (mode == "tn" and b.ndim == 3) else 0
    a2 = (a.shape[1], a.shape[0] * a.shape[2]) if a_planes else a.shape
    b2 = (b.shape[1], b.shape[0] * b.shape[2]) if b_planes else b.shape
    if mode == "nn":
        (m, k), (k2, n) = a2, b2
    elif mode == "nt":
        (m, k), (n, k2) = a2, b2
    else:
        (k, m), (k2, n) = a2, b2
    assert k == k2, (a.shape, b.shape, mode)
    tm, tn = _pick(m, 1536), _pick(n // b_planes if b_planes else n, 1536)
    tk = _pick(k // a_planes if a_planes else k, 1536)
    nk = k // tk
    dn = {"nn": NN, "nt": NT, "tn": TN}[mode]
    has_add = add is not None

    def body(*refs):
        if has_add:
            a_ref, b_ref, add_ref, o_ref, acc_ref = refs
        else:
            a_ref, b_ref, o_ref, acc_ref = refs
            add_ref = None
        kk = pl.program_id(2)
        part = _dotb(a_ref[0] if a_planes else a_ref[...], b_ref[0] if b_planes else b_ref[...], dn)

        def finish(r):
            if has_add:
                r = r + add_ref[...]
            o_ref[...] = r.astype(out_dtype)

        if nk == 1:
            finish(part)
        else:
            @pl.when(kk == 0)
            def _():
                acc_ref[...] = part

            @pl.when(kk > 0)
            def _():
                acc_ref[...] += part

            @pl.when(kk == nk - 1)
            def _():
                finish(acc_ref[...])

    if mode == "nn":
        a_spec = pl.BlockSpec((tm, tk), lambda i, j, q: (i, q))
        b_spec = pl.BlockSpec((tk, tn), lambda i, j, q: (q, j))
    elif mode == "nt":
        per = (k // a_planes) // tk if a_planes else 0
        a_spec = (pl.BlockSpec((1, tm, tk), lambda i, j, q: (q // per, i, q % per)) if a_planes
                  else pl.BlockSpec((tm, tk), lambda i, j, q: (i, q)))
        b_spec = pl.BlockSpec((tn, tk), lambda i, j, q: (j, q))
    else:
        per = (n // b_planes) // tn if b_planes else 0
        a_spec = pl.BlockSpec((tk, tm), lambda i, j, q: (q, i))
        b_spec = (pl.BlockSpec((1, tk, tn), lambda i, j, q: (j // per, q, j % per)) if b_planes
                  else pl.BlockSpec((tk, tn), lambda i, j, q: (q, j)))
    o_spec = pl.BlockSpec((tm, tn), lambda i, j, q: (i, j))
    in_specs = [a_spec, b_spec] + ([o_spec] if has_add else [])
    args = (a, b) + ((add,) if has_add else ())
    return pl.pallas_call(
        body, name=name, grid=(m // tm, n // tn, nk), in_specs=in_specs, out_specs=o_spec,
        out_shape=jax.ShapeDtypeStruct((m, n), out_dtype),
        scratch_shapes=[pltpu.VMEM((tm, tn) if nk > 1 else (8, LANES), F32)],
        compiler_params=_cp(("parallel", "parallel", "arbitrary")),
    )(*args)


def _rms_fwd(x, g, *, gw, ncol, name, x_col0=0, sub=None, z=None, z_col0=0, out_dtype=BF16):
    rows = x.shape[0]
    tr = _pick(rows, 512, 8)
    sub = gw if sub is None else sub
    gated = z is not None

    def body(*refs):
        if gated:
            x_ref, z_ref, g_ref, o_ref = refs
            xv = x_ref[...] * _silu(z_ref[...])
        else:
            x_ref, g_ref, o_ref = refs
            xv = x_ref[...]
        r = lax.rsqrt(_gmean(xv * xv, sub) + EPS)
        o_ref[...] = (xv * r * g_ref[...]).astype(out_dtype)

    specs = [pl.BlockSpec((tr, gw), lambda j, i: (i, x_col0 + j))]
    args = [x]
    if gated:
        specs.append(pl.BlockSpec((tr, gw), lambda j, i: (i, z_col0 + j)))
        args.append(z)
    specs.append(pl.BlockSpec((1, gw), lambda j, i: (0, j)))
    args.append(g)
    return pl.pallas_call(
        body, name=name, grid=(ncol, rows // tr), in_specs=specs,
        out_specs=pl.BlockSpec((tr, gw), lambda j, i: (i, j)),
        out_shape=jax.ShapeDtypeStruct((rows, gw * ncol), out_dtype),
        compiler_params=_cp(("parallel", "parallel")),
    )(*args)


def _rms_bwd(x, g, dy, *, gw, ncol, name, x_col0=0, sub=None, z=None, z_col0=0, add=None, dx_dtype=F32):
    rows = x.shape[0]
    tr = _pick(rows, 512, 8)
    sub = gw if sub is None else sub
    gated = z is not None
    has_add = add is not None

    def body(*refs):
        refs = list(refs)
        x_ref = refs.pop(0)
        z_ref = refs.pop(0) if gated else None
        g_ref = refs.pop(0)
        dy_ref = refs.pop(0)
        add_ref = refs.pop(0) if has_add else None
        dx_ref = refs.pop(0)
        dz_ref = refs.pop(0) if gated else None
        dg_ref = refs.pop(0)
        i = pl.program_id(1)
        xv = x_ref[...]
        if gated:
            zz = z_ref[...]
            yz = xv * _silu(zz)
        else:
            yz = xv
        r = lax.rsqrt(_gmean(yz * yz, sub) + EPS)
        xh = yz * r
        dy = dy_ref[...].astype(F32)
        dyg = dy * g_ref[...]
        d_yz = r * (dyg - xh * _gmean(dyg * xh, sub))
        if gated:
            dx_ref[...] = (d_yz * _silu(zz)).astype(dx_dtype)
            dz_ref[...] = (d_yz * xv * _dsilu(zz)).astype(dx_dtype)
        elif has_add:
            dx_ref[...] = (d_yz + add_ref[...]).astype(dx_dtype)
        else:
            dx_ref[...] = d_yz.astype(dx_dtype)
        part = jnp.sum(dy * xh, axis=0, keepdims=True)

        @pl.when(i == 0)
        def _():
            dg_ref[...] = part

        @pl.when(i > 0)
        def _():
            dg_ref[...] += part

    tile = pl.BlockSpec((tr, gw), lambda j, i: (i, j))
    specs = [pl.BlockSpec((tr, gw), lambda j, i: (i, x_col0 + j))]
    args = [x]
    if gated:
        specs.append(pl.BlockSpec((tr, gw), lambda j, i: (i, z_col0 + j)))
        args.append(z)
    specs += [pl.BlockSpec((1, gw), lambda j, i: (0, j)), tile]
    args += [g, dy]
    if has_add:
        specs.append(tile)
        args.append(add)
    width = gw * ncol
    out_shape = [jax.ShapeDtypeStruct((rows, width), dx_dtype)]
    out_specs = [tile]
    if gated:
        out_shape.append(jax.ShapeDtypeStruct((rows, width), dx_dtype))
        out_specs.append(tile)
    out_shape.append(jax.ShapeDtypeStruct((1, width), F32))
    out_specs.append(pl.BlockSpec((1, gw), lambda j, i: (0, j)))
    return pl.pallas_call(
        body, name=name, grid=(ncol, rows // tr), in_specs=specs, out_specs=out_specs, out_shape=out_shape,
        compiler_params=_cp(("parallel", "arbitrary")),
    )(*args)


HALO = 8


def _conv_rows(tc):
    return 16 * 8 * LANES // tc


def _conv_fwd(u, w8, b, *, kw, width, name, u_col0=0, mul_col0=None, out_dtype=F32, rider=None):
    rows = u.shape[0]
    ts = _pick(rows, 512, 8)
    tc = _pick(width, 512)
    gated = mul_col0 is not None
    c0 = u_col0 // tc
    m0 = (mul_col0 // tc) if gated else 0
    assert u_col0 % tc == 0 and (not gated or mul_col0 % tc == 0)

    def body(*refs):
        if gated:
            cur_ref, halo_ref, mul_ref, w_ref, b_ref, o_ref, ext = refs
        else:
            cur_ref, halo_ref, w_ref, b_ref, o_ref, ext = refs
        i = pl.program_id(0)
        ext[pl.ds(0, HALO), :] = jnp.where(i == 0, 0.0, halo_ref[...])
        ext[pl.ds(HALO, ts), :] = cur_ref[...]
        bias = b_ref[...]
        taps = [w_ref[k:k + 1, :] for k in range(kw)]
        rb = _conv_rows(tc)
        for r0 in range(0, ts, rb):
            pre = bias + taps[0] * ext[pl.ds(r0 + HALO - (kw - 1), rb), :]
            for k in range(1, kw):
                pre = pre + taps[k] * ext[pl.ds(r0 + HALO - (kw - 1) + k, rb), :]
            act = _silu(pre)
            if gated:
                act = act * mul_ref[pl.ds(r0, rb), :]
            o_ref[pl.ds(r0, rb), :] = act.astype(out_dtype)

    hb = ts // HALO
    specs = [pl.BlockSpec((ts, tc), lambda i, j: (i, c0 + j)),
             pl.BlockSpec((HALO, tc), lambda i, j: (jnp.maximum(i * hb - 1, 0), c0 + j))]
    args = [u, u]
    if gated:
        specs.append(pl.BlockSpec((ts, tc), lambda i, j: (i, m0 + j)))
        args.append(u)
    specs += [pl.BlockSpec((8, tc), lambda i, j: (0, j)), pl.BlockSpec((1, tc), lambda i, j: (0, j))]
    args += [w8, b]
    outs, riding = _call(
        body, name=name, grid=(rows // ts, width // tc), in_specs=specs,
        out_specs=[pl.BlockSpec((ts, tc), lambda i, j: (i, j))],
        out_shape=[jax.ShapeDtypeStruct((rows, width), out_dtype)],
        scratch_shapes=[pltpu.VMEM((ts + HALO, tc), F32)], sem=("parallel", "parallel"), rider=rider, args=args)
    return outs + [riding]


def _conv_bwd(u, w8, b, dact, *, kw, width, name, u_col0=0, mul_col0=None, du_dtype=BF16, rider=None):
    rows = u.shape[0]
    ts = _pick(rows, 512, 8)
    tc = _pick(width, 512)
    pieces = list(dact) if isinstance(dact, (list, tuple)) else [dact]
    firsts, seen = [], 0
    for piece in pieces:
        assert piece.shape[1] % tc == 0, (piece.shape, tc)
        firsts.append(seen // tc)
        seen += piece.shape[1]
    assert seen == width
    gated = mul_col0 is not None
    c0 = u_col0 // tc
    m0 = (mul_col0 // tc) if gated else 0
    nt = rows // ts
    hb = ts // HALO

    def body(*refs):
        refs = list(refs)
        cur_ref, halo_ref = refs.pop(0), refs.pop(0)
        mul_ref = refs.pop(0) if gated else None
        w_ref, b_ref = refs.pop(0), refs.pop(0)
        da_refs = [refs.pop(0) for _ in pieces]
        col_tile = pl.program_id(0)
        du_ref = refs.pop(0)
        dwb_ref, ext_u, ext_d = refs
        t = pl.program_id(1)
        ti = nt - 1 - t
        ext_u[pl.ds(0, HALO), :] = jnp.where(ti == 0, 0.0, halo_ref[...])
        ext_u[pl.ds(HALO, ts), :] = cur_ref[...]

        @pl.when(t == 0)
        def _():
            ext_d[pl.ds(ts, HALO), :] = jnp.zeros((HALO, tc), F32)
            dwb_ref[...] = jnp.zeros((8, tc), F32)

        bias = b_ref[...]
        taps = [w_ref[k:k + 1, :] for k in range(kw)]
        rb = _conv_rows(tc)
        dw_acc = [jnp.zeros((1, tc), F32) for _ in range(kw)]
        db_acc = jnp.zeros((1, tc), F32)
        for r0 in reversed(range(0, ts, rb)):
            shifted = [ext_u[pl.ds(r0 + HALO - (kw - 1) + k, rb), :] for k in range(kw)]
            pre = bias + taps[0] * shifted[0]
            for k in range(1, kw):
                pre = pre + taps[k] * shifted[k]
            sg = _sigmoid(pre)
            dsilu = sg * (1.0 + pre * (1.0 - sg))
            da = da_refs[0][pl.ds(r0, rb), :].astype(F32)
            for first, ref in zip(firsts[1:], da_refs[1:]):
                da = jnp.where(col_tile >= first, ref[pl.ds(r0, rb), :].astype(F32), da)
            if gated:
                du_ref[1, pl.ds(r0, rb), :] = (da * (pre * sg)).astype(du_dtype)
                dgp = da * mul_ref[pl.ds(r0, rb), :] * dsilu
            else:
                dgp = da * dsilu
            ext_d[pl.ds(r0, rb), :] = dgp
            du = taps[kw - 1] * dgp
            for k in range(kw - 1):
                du = du + taps[k] * ext_d[pl.ds(r0 + kw - 1 - k, rb), :]
            if gated:
                du_ref[0, pl.ds(r0, rb), :] = du.astype(du_dtype)
            else:
                du_ref[pl.ds(r0, rb), :] = du.astype(du_dtype)
            for k in range(kw):
                dw_acc[k] = dw_acc[k] + jnp.sum(dgp * shifted[k], axis=0, keepdims=True)
            db_acc = db_acc + jnp.sum(dgp, axis=0, keepdims=True)
        for k in range(kw):
            dwb_ref[k:k + 1, :] += dw_acc[k]
        dwb_ref[7:8, :] += db_acc
        ext_d[pl.ds(ts, HALO), :] = ext_d[pl.ds(0, HALO), :]

    specs = [pl.BlockSpec((ts, tc), lambda j, t: (nt - 1 - t, c0 + j)),
             pl.BlockSpec((HALO, tc), lambda j, t: (jnp.maximum((nt - 1 - t) * hb - 1, 0), c0 + j))]
    args = [u, u]
    if gated:
        specs.append(pl.BlockSpec((ts, tc), lambda j, t: (nt - 1 - t, m0 + j)))
        args.append(u)
    tile = pl.BlockSpec((ts, tc), lambda j, t: (nt - 1 - t, j))
    specs += [pl.BlockSpec((8, tc), lambda j, t: (0, j)), pl.BlockSpec((1, tc), lambda j, t: (0, j))]
    args += [w8, b]
    for first, piece in zip(firsts, pieces):
        count = piece.shape[1] // tc

        def piece_map(j, t, first=first, count=count):
            mine = jnp.logical_and(j >= first, j < first + count)
            return (jnp.where(mine, nt - 1 - t, 0), jnp.clip(j - first, 0, count - 1))

        specs.append(pl.BlockSpec((ts, tc), piece_map))
        args.append(piece)
    if gated:
        out_shape = [jax.ShapeDtypeStruct((2, rows, width), du_dtype)]
        out_specs = [pl.BlockSpec((2, ts, tc), lambda j, t: (0, nt - 1 - t, j))]
    else:
        out_shape = [jax.ShapeDtypeStruct((rows, width), du_dtype)]
        out_specs = [tile]
    out_shape.append(jax.ShapeDtypeStruct((8, width), F32))
    out_specs.append(pl.BlockSpec((8, tc), lambda j, t: (0, j)))
    outs, riding = _call(
        body, name=name, grid=(width // tc, nt), in_specs=specs, out_specs=out_specs, out_shape=out_shape,
        scratch_shapes=[pltpu.VMEM((ts + HALO, tc), F32), pltpu.VMEM((ts + HALO, tc), F32)],
        sem=("parallel", "arbitrary"), rider=rider, args=args)
    return outs + [riding]


GW = SSD_HPG * SSD_HD


def _ssd_common(x, bm, cm, dt_raw, dt_raw_t, bias_r, bias_c, alog_r, alog_c):
    row = lax.broadcasted_iota(jnp.int32, (CHUNK, CHUNK), 0)
    col = lax.broadcasted_iota(jnp.int32, (CHUNK, CHUNK), 1)
    causal = row >= col
    tril = causal.astype(F32)
    triu = (row <= col).astype(F32)
    spread = _group_matrix(GW, SSD_HD, transpose=True)
    dt = _softplus(dt_raw + bias_r)
    dt_t = _softplus(dt_raw_t + bias_c)
    a_r = -jnp.exp(alog_r)
    a_c = -jnp.exp(alog_c)
    acs = _dotf(tril, dt * a_r, onehot="a", pieces=3)
    acs_t = _dotf(dt_t * a_c, triu, pieces=3)
    last = acs[CHUNK - 1:CHUNK, :]
    ds = jnp.exp(last - acs)
    cd = jnp.exp(last)
    c = dict(causal=causal, tril=tril, triu=triu, spread=spread, dt=dt, a_r=a_r, acs=acs, acs_t=acs_t, ds=ds, cd=cd)
    c["eb"] = _dotf(jnp.exp(acs), spread)
    c["dsb"] = _dotf(ds, spread)
    c["cdb"] = _dotf(cd, spread)
    c["dtb"] = _dotf(dt, spread)
    c["xdt"] = x * c["dtb"]
    c["cb"] = _dotb(cm, bm, NT)
    return c


def _ssd_lam(c, r):
    diff = c["acs"][:, r:r + 1] - c["acs_t"][r:r + 1, :]
    return jnp.exp(jnp.where(c["causal"], diff, -jnp.inf))


GP = 2


def _ssd_specs(nc, rev):
    def ci(t):
        return (nc - 1 - t) if rev else t
    xs = pl.BlockSpec((CHUNK, GP * GW), lambda g, t: (ci(t), g))
    bs = pl.BlockSpec((CHUNK, GP * SSD_N), lambda g, t: (ci(t), SSD_DI // (GP * SSD_N) + g))
    cs = pl.BlockSpec((CHUNK, GP * SSD_N), lambda g, t: (ci(t), (SSD_DI // SSD_N + SSD_G) // GP + g))
    dts = pl.BlockSpec((GP, CHUNK, 8), lambda g, t: (g, ci(t), 0))
    dtts = pl.BlockSpec((GP, 8, CHUNK), lambda g, t: (g, 0, ci(t)))
    pr = pl.BlockSpec((GP, 1, 8), lambda g, t: (g, 0, 0))
    pc = pl.BlockSpec((GP, 8, 1), lambda g, t: (g, 0, 0))
    hs = pl.BlockSpec((1, GP, SSD_N, GW), lambda g, t: (ci(t), g, 0, 0))
    return xs, bs, cs, dts, dtts, pr, pc, hs


def _ssd_fwd(xbc, dtg, dtg_t, bias_r, bias_c, alog_r, alog_c, d_r, *, name, rider=None):
    s = xbc.shape[0]
    nc = s // CHUNK
    xs, bs, cs, dts, dtts, pr, pc, hs = _ssd_specs(nc, False)

    def body(x_ref, b_ref, c_ref, dt_ref, dtt_ref, br_ref, bc_ref, ar_ref, ac_ref, d_ref, y_ref, hp_ref, h_sc):
        t = pl.program_id(1)

        @pl.when(t == 0)
        def _():
            h_sc[...] = jnp.zeros_like(h_sc)

        for gg in range(GP):
            wide, narrow = slice(gg * GW, (gg + 1) * GW), slice(gg * SSD_N, (gg + 1) * SSD_N)
            x, bm, cm = x_ref[:, wide], b_ref[:, narrow], c_ref[:, narrow]
            c = _ssd_common(x, bm, cm, dt_ref[gg], dtt_ref[gg], br_ref[gg], bc_ref[gg], ar_ref[gg], ac_ref[gg])
            h = h_sc[gg]
            hp_ref[0, gg] = h
            xdt = c["xdt"]
            pieces = []
            for r in range(SSD_HPG):
                m = c["cb"] * _ssd_lam(c, r)
                pieces.append(_dotb(m, xdt[:, r * SSD_HD:(r + 1) * SSD_HD]))
            y = jnp.concatenate(pieces, axis=1) + c["eb"] * _dotb(cm, h) + x * _dotf(d_ref[gg], c["spread"])
            y_ref[:, wide] = y
            h_sc[gg] = h * c["cdb"] + _dotb(bm, xdt * c["dsb"], TN)

    outs, riding = _call(
        body, name=name, grid=(SSD_G // GP, nc),
        in_specs=[xs, bs, cs, dts, dtts, pr, pc, pr, pc, pr],
        out_specs=[xs, hs],
        out_shape=[jax.ShapeDtypeStruct((s, SSD_DI), F32), jax.ShapeDtypeStruct((nc, SSD_G, SSD_N, GW), F32)],
        scratch_shapes=[pltpu.VMEM((GP, SSD_N, GW), F32)], sem=("parallel", "arbitrary"), rider=rider,
        args=(xbc, xbc, xbc, dtg, dtg_t, bias_r, bias_c, alog_r, alog_c, d_r))
    return outs + [riding]


def _ssd_bwd(xbc, dtg, dtg_t, bias_r, bias_c, alog_r, alog_c, d_r, hprev, dy, *, name, rider=None):
    s = xbc.shape[0]
    nc = s // CHUNK
    xs, bs, cs, dts, dtts, pr, pc, hs = _ssd_specs(nc, True)
    gsum = functools.partial(_group_matrix, GW, SSD_HD)

    def body(x_ref, b_ref, c_ref, dt_ref, dtt_ref, br_ref, bc_ref, ar_ref, ac_ref, d_ref, hp_ref, dy_ref,
             dx_ref, db_ref, dc_ref, ddt_ref, dbias_ref, dalog_ref, dd_ref, dh_sc):
        t = pl.program_id(1)

        @pl.when(t == 0)
        def _():
            dh_sc[...] = jnp.zeros_like(dh_sc)
            dbias_ref[...] = jnp.zeros_like(dbias_ref)
            dalog_ref[...] = jnp.zeros_like(dalog_ref)
            dd_ref[...] = jnp.zeros_like(dd_ref)

        for gg in range(GP):
            wide, narrow = slice(gg * GW, (gg + 1) * GW), slice(gg * SSD_N, (gg + 1) * SSD_N)
            x, bm, cm = x_ref[:, wide], b_ref[:, narrow], c_ref[:, narrow]
            c = _ssd_common(x, bm, cm, dt_ref[gg], dtt_ref[gg], br_ref[gg], bc_ref[gg], ar_ref[gg], ac_ref[gg])
            lanesum = gsum()
            h = hp_ref[0, gg]
            dh = dh_sc[gg]
            dy = dy_ref[:, wide]
            xdt, dsb = c["xdt"], c["dsb"]
            skip = _dotf(d_ref[gg], c["spread"])
            dd_ref[gg] += jnp.sum(_dotf(dy * x, lanesum), axis=0, keepdims=True)
            dacs = _dotf(dy * (c["eb"] * _dotb(cm, h)), lanesum)
            edy = c["eb"] * dy
            dcm = _dotb(edy, h, NT)
            dh_prev = _dotb(cm, edy, TN)
            bdh = _dotb(bm, dh)
            dxdt = dsb * bdh
            dbm = _dotb(dsb * xdt, dh, NT)
            t1 = _dotf(xdt * bdh, lanesum) * c["ds"]
            dacs = dacs - t1
            dlast = (jnp.sum(t1, axis=0, keepdims=True)
                     + jnp.sum(_dotf(dh * h, lanesum), axis=0, keepdims=True) * c["cd"])
            dcb = jnp.zeros((CHUNK, CHUNK), F32)
            pieces = []
            ones8 = jnp.ones((CHUNK, 8), F32)
            head = lax.broadcasted_iota(jnp.int32, (1, 8), 1)
            for r in range(SSD_HPG):
                sl = slice(r * SSD_HD, (r + 1) * SSD_HD)
                lam = _ssd_lam(c, r)
                m = c["cb"] * lam
                dm = _dotb(dy[:, sl], xdt[:, sl], NT)
                dcb = dcb + dm * lam
                gm = dm * m
                dacs = dacs + ((jnp.sum(gm, axis=1, keepdims=True) - _dotf(gm, ones8, TN, pieces=3))
                               * (head == r).astype(F32))
                pieces.append(_dotb(m, dy[:, sl], TN))
            dxdt = dxdt + jnp.concatenate(pieces, axis=1)
            dcm = dcm + _dotb(dcb, bm)
            dbm = dbm + _dotb(dcb, cm, TN)
            dx_ref[:, wide] = dy * skip + dxdt * c["dtb"]
            db_ref[:, narrow] = dbm
            dc_ref[:, narrow] = dcm
            rowid = lax.broadcasted_iota(jnp.int32, (CHUNK, 8), 0)
            dacs = dacs + jnp.where(rowid == CHUNK - 1, dlast, 0.0)
            dda = _dotf(c["triu"], dacs, onehot="a", pieces=3)
            ddt = _dotf(dxdt * x, lanesum) + dda * c["a_r"]
            ddt_raw = ddt * _sigmoid(dt_ref[gg] + br_ref[gg])
            ddt_ref[gg] = ddt_raw
            dbias_ref[gg] += jnp.sum(ddt_raw, axis=0, keepdims=True)
            dalog_ref[gg] += jnp.sum(dda * c["dt"], axis=0, keepdims=True) * c["a_r"]
            dh_sc[gg] = dh_prev + dh * c["cdb"]

    ci = lambda t: nc - 1 - t
    nspec = pl.BlockSpec((CHUNK, GP * SSD_N), lambda g, t: (ci(t), g))
    outs, riding = _call(
        body, name=name, grid=(SSD_G // GP, nc),
        in_specs=[xs, bs, cs, dts, dtts, pr, pc, pr, pc, pr, hs, xs],
        out_specs=[xs, nspec, nspec, dts, pr, pr, pr],
        out_shape=[jax.ShapeDtypeStruct((s, SSD_DI), F32), jax.ShapeDtypeStruct((s, SSD_G * SSD_N), F32),
                   jax.ShapeDtypeStruct((s, SSD_G * SSD_N), F32), jax.ShapeDtypeStruct((SSD_G, s, 8), F32),
                   jax.ShapeDtypeStruct((SSD_G, 1, 8), F32), jax.ShapeDtypeStruct((SSD_G, 1, 8), F32),
                   jax.ShapeDtypeStruct((SSD_G, 1, 8), F32)],
        scratch_shapes=[pltpu.VMEM((GP, SSD_N, GW), F32)], sem=("parallel", "arbitrary"), rider=rider,
        args=(xbc, xbc, xbc, dtg, dtg_t, bias_r, bias_c, alog_r, alog_c, d_r, hprev, dy))
    return outs + [riding]


FOX_PAIRS = FOX_H // 2
FOX_SCALE = FOX_HD ** -0.5
NEG_INF = -jnp.inf


def _fgate_fwd(f_t, b_c, *, name):
    hh, s = f_t.shape
    tb = _pick(s, 512)
    nb = s // tb

    def body(f_ref, b_ref, o_ref, carry):
        t = pl.program_id(0)

        @pl.when(t == 0)
        def _():
            carry[...] = jnp.zeros_like(carry)

        lf = -_softplus(-(f_ref[...] + b_ref[...]))
        row = lax.broadcasted_iota(jnp.int32, (tb, tb), 0)
        col = lax.broadcasted_iota(jnp.int32, (tb, tb), 1)
        cum = _dotf(lf, (row <= col).astype(F32), pieces=3) + carry[:, 0:1]
        o_ref[...] = cum
        carry[:, 0:1] = cum[:, tb - 1:tb]

    return pl.pallas_call(
        body, name=name, grid=(nb,),
        in_specs=[pl.BlockSpec((hh, tb), lambda t: (0, t)), pl.BlockSpec((hh, 1), lambda t: (0, 0))],
        out_specs=pl.BlockSpec((hh, tb), lambda t: (0, t)),
        out_shape=jax.ShapeDtypeStruct((hh, s), F32),
        scratch_shapes=[pltpu.VMEM((hh, LANES), F32)],
        compiler_params=_cp(("arbitrary",)),
    )(f_t, b_c)


def _fgate_bwd(dcum_q_t, dcum_k_t, f_t, b_c, *, name):
    hh, s = f_t.shape
    tb = _pick(s, 512)
    nb = s // tb

    def body(dq_ref, d_ref, f_ref, b_ref, df_ref, db_ref, carry):
        t = pl.program_id(0)

        @pl.when(t == 0)
        def _():
            carry[...] = jnp.zeros_like(carry)
            db_ref[...] = jnp.zeros_like(db_ref)

        d = d_ref[...] + dq_ref[...]
        row = lax.broadcasted_iota(jnp.int32, (tb, tb), 0)
        col = lax.broadcasted_iota(jnp.int32, (tb, tb), 1)
        rev = _dotf(d, (row >= col).astype(F32), pieces=3) + carry[:, 0:1]
        df = rev * _sigmoid(-(f_ref[...] + b_ref[...]))
        df_ref[...] = df
        db_ref[...] += jnp.sum(df, axis=1, keepdims=True)
        carry[:, 0:1] = rev[:, 0:1]

    blk = pl.BlockSpec((hh, tb), lambda t: (0, nb - 1 - t))
    return pl.pallas_call(
        body, name=name, grid=(nb,),
        in_specs=[blk, blk, blk, pl.BlockSpec((hh, 1), lambda t: (0, 0))],
        out_specs=[blk, pl.BlockSpec((hh, 1), lambda t: (0, 0))],
        out_shape=[jax.ShapeDtypeStruct((hh, s), F32), jax.ShapeDtypeStruct((hh, 1), F32)],
        scratch_shapes=[pltpu.VMEM((hh, LANES), F32)],
        compiler_params=_cp(("arbitrary",)),
    )(dcum_q_t, dcum_k_t, f_t, b_c)


def _fox_tile(s):
    return min(512, max(s // 2, 8))


def _tri_tables(nq, kv_major):
    if kv_major:
        pairs = [(i, j) for j in range(nq) for i in range(j, nq)]
    else:
        pairs = [(i, j) for i in range(nq) for j in range(i + 1)]
    return (jnp.asarray([p[0] for p in pairs], jnp.int32), jnp.asarray([p[1] for p in pairs], jnp.int32))


def _lane_tile(col, width):
    return col if width == LANES else jnp.tile(col, (1, width // LANES))


def _flash_fwd(qs, kn, qkvg, ck, *, name, rider=None):
    s = qs.shape[0]
    tt = _fox_tile(s)
    nq = s // tt
    itab, jtab = _tri_tables(nq, kv_major=False)
    v0 = 2 * FOX_D // LANES

    def body(itab_ref, jtab_ref, q_ref, k_ref, v_ref, ck_ref, o_ref, lse_ref, m_sc, l_sc, acc_sc):
        t = pl.program_id(1)
        i, j = itab_ref[t], jtab_ref[t]

        @pl.when(j == 0)
        def _():
            m_sc[...] = jnp.full_like(m_sc, NEG_INF)
            l_sc[...] = jnp.zeros_like(l_sc)
            acc_sc[...] = jnp.zeros_like(acc_sc)

        low = lax.broadcasted_iota(jnp.int32, (tt, LANES), 1) < FOX_HD

        def step(diagonal):
            q2, k2 = q_ref[...], k_ref[...]
            v2 = v_ref[...].astype(BF16)
            alphas, outs = [], []
            for hh in range(2):
                qh = jnp.where(low if hh == 0 else jnp.logical_not(low), q2, jnp.zeros_like(q2))
                sc = lax.dot_general(qh, k2, NT, preferred_element_type=F32) - ck_ref[0][hh:hh + 1, :]
                if diagonal:
                    row = lax.broadcasted_iota(jnp.int32, sc.shape, 0)
                    col = lax.broadcasted_iota(jnp.int32, sc.shape, 1)
                    sc = jnp.where(row >= col, sc, NEG_INF)
                m_prev = m_sc[hh]
                m_new = jnp.maximum(m_prev, jnp.max(sc, axis=1, keepdims=True))
                alpha = jnp.exp(m_prev - m_new)
                p = jnp.exp(sc - _lane_tile(m_new, tt))
                l_sc[hh] = alpha * l_sc[hh] + jnp.sum(p, axis=1, keepdims=True)
                m_sc[hh] = m_new
                alphas.append(alpha)
                outs.append(lax.dot_general(p.astype(BF16), v2, NN, preferred_element_type=F32))
            acc_sc[...] = jnp.where(low, alphas[0], alphas[1]) * acc_sc[...] + jnp.where(low, outs[0], outs[1])

        @pl.when(j < i)
        def _():
            step(False)

        @pl.when(j == i)
        def _():
            step(True)
            o_ref[...] = acc_sc[...] / jnp.where(low, l_sc[0], l_sc[1])
            lse_ref[0] = jnp.concatenate([m_sc[hh][:, 0:1] + jnp.log(l_sc[hh][:, 0:1]) for hh in range(2)], axis=1)

    outs, riding = _call(
        body, name=name, grid=(FOX_PAIRS, int(itab.shape[0])), prefetch=(itab, jtab),
        in_specs=[pl.BlockSpec((tt, LANES), lambda p, t, it, jt: (it[t], p)),
                  pl.BlockSpec((tt, LANES), lambda p, t, it, jt: (jt[t], p)),
                  pl.BlockSpec((tt, LANES), lambda p, t, it, jt: (jt[t], v0 + p)),
                  pl.BlockSpec((1, 2, tt), lambda p, t, it, jt: (p, 0, jt[t]))],
        out_specs=[pl.BlockSpec((tt, LANES), lambda p, t, it, jt: (it[t], p)),
                   pl.BlockSpec((1, tt, 2), lambda p, t, it, jt: (p, it[t], 0))],
        scratch_shapes=[pltpu.VMEM((2, tt, LANES), F32), pltpu.VMEM((2, tt, LANES), F32), pltpu.VMEM((tt, LANES), F32)],
        out_shape=[jax.ShapeDtypeStruct((s, FOX_D), F32), jax.ShapeDtypeStruct((FOX_PAIRS, s, 2), F32)],
        sem=("parallel", "arbitrary"), rider=rider, args=(qs, kn, qkvg, ck))
    return outs + [riding]


def _flash_bwd(qs, kn, qkvg, do, lse_t, delta_t, ck_c, *, name, rider=None):
    s = qs.shape[0]
    tt = _fox_tile(s)
    nq = s // tt
    nl = tt // LANES
    itab, jtab = _tri_tables(nq, kv_major=True)
    nsteps = itab.shape[0]
    v0 = 2 * FOX_D // LANES

    def body(itab_ref, jtab_ref, q_ref, k_ref, v_ref, do_ref, lse_ref, dl_ref, ck_ref,
             dq_ref, dk_ref, dv_ref, dcq_ref, dck_ref, dqt_sc, rs_sc, dk_sc, dv_sc, dc_sc, kt_sc, ckb_sc):
        t = pl.program_id(1)
        i, j = itab_ref[t], jtab_ref[t]

        @pl.when(t == 0)
        def _():
            dqt_sc[...] = jnp.zeros_like(dqt_sc)
            rs_sc[...] = jnp.zeros_like(rs_sc)

        @pl.when(i == j)
        def _():
            dk_sc[...] = jnp.zeros_like(dk_sc)
            dv_sc[...] = jnp.zeros_like(dv_sc)
            dc_sc[...] = jnp.zeros_like(dc_sc)
            kt_sc[...] = k_ref[...].astype(F32).T.astype(BF16)
            for hh in range(2):
                ckb_sc[hh] = jnp.broadcast_to(ck_ref[0][:, hh:hh + 1], (tt, LANES))

        low = lax.broadcasted_iota(jnp.int32, (tt, LANES), 1) < FOX_HD
        top = lax.broadcasted_iota(jnp.int32, (LANES, tt), 0) < FOX_HD

        def step(diagonal):
            q2, k2, kt = q_ref[...], k_ref[...], kt_sc[...]
            v2 = v_ref[...].astype(BF16)
            do2 = do_ref[...].astype(BF16)
            dqs, dks, dvs = [], [], []
            for hh in range(2):
                sel = low if hh == 0 else jnp.logical_not(low)
                qh = jnp.where(sel, q2, jnp.zeros_like(q2))
                doh = jnp.where(sel, do2, jnp.zeros_like(do2))
                st = lax.dot_general(k2, qh, NT, preferred_element_type=F32)
                st = st - _lane_tile(ckb_sc[hh], tt) - lse_ref[0][hh:hh + 1, :]
                if diagonal:
                    key = lax.broadcasted_iota(jnp.int32, st.shape, 0)
                    qry = lax.broadcasted_iota(jnp.int32, st.shape, 1)
                    st = jnp.where(qry >= key, st, NEG_INF)
                pt = jnp.exp(st)
                dpt = lax.dot_general(v2, doh, NT, preferred_element_type=F32)
                dst = pt * (dpt - dl_ref[0][hh:hh + 1, :])
                ptb, dstb = pt.astype(BF16), dst.astype(BF16)
                dvs.append(lax.dot_general(ptb, do2, NN, preferred_element_type=F32))
                dks.append(lax.dot_general(dstb, q2, NN, preferred_element_type=F32))
                dqs.append(lax.dot_general(kt, dstb, NN, preferred_element_type=F32))
                rs_sc[hh, i] += jnp.sum(dst, axis=0, keepdims=True)
                part = dst[:, 0:LANES]
                for b in range(1, nl):
                    part = part + dst[:, b * LANES:(b + 1) * LANES]
                dc_sc[hh] += part
            dv_sc[...] += jnp.where(low, dvs[0], dvs[1])
            dk_sc[...] += jnp.where(low, dks[0], dks[1])
            dqt_sc[i] += jnp.where(top, dqs[0], dqs[1])

        @pl.when(j < i)
        def _():
            step(False)

        @pl.when(j == i)
        def _():
            step(True)

        @pl.when(i == nq - 1)
        def _():
            dk_ref[...] = dk_sc[...]
            dv_ref[...] = dv_sc[...]
            dck_ref[0] = -jnp.concatenate([jnp.sum(dc_sc[hh], axis=1, keepdims=True) for hh in range(2)], axis=1)

        @pl.when(t == nsteps - 1)
        def _():
            for b in range(nq):
                dq_ref[pl.ds(b * tt, tt), :] = dqt_sc[b].T * FOX_SCALE
                dcq_ref[0, :, pl.ds(b * tt, tt)] = jnp.concatenate([rs_sc[hh, b] for hh in range(2)], axis=0)

    qside = pl.BlockSpec((tt, LANES), lambda p, t, it, jt: (it[t], p))
    kside = pl.BlockSpec((tt, LANES), lambda p, t, it, jt: (jt[t], p))
    qstat = pl.BlockSpec((1, 2, tt), lambda p, t, it, jt: (p, 0, it[t]))
    kstat = pl.BlockSpec((1, tt, 2), lambda p, t, it, jt: (p, jt[t], 0))
    outs, riding = _call(
        body, name=name, grid=(FOX_PAIRS, nsteps), prefetch=(itab, jtab),
        in_specs=[qside, kside, pl.BlockSpec((tt, LANES), lambda p, t, it, jt: (jt[t], v0 + p)), qside, qstat, qstat, kstat],
        out_specs=[pl.BlockSpec((s, LANES), lambda p, t, it, jt: (0, p)), kside, kside,
                   pl.BlockSpec((1, 2, s), lambda p, t, it, jt: (p, 0, 0)), kstat],
        scratch_shapes=[pltpu.VMEM((nq, LANES, tt), F32), pltpu.VMEM((2, nq, 1, tt), F32), pltpu.VMEM((tt, LANES), F32),
                        pltpu.VMEM((tt, LANES), F32), pltpu.VMEM((2, tt, LANES), F32), pltpu.VMEM((LANES, tt), BF16),
                        pltpu.VMEM((2, tt, LANES), F32)],
        out_shape=[jax.ShapeDtypeStruct((s, FOX_D), F32), jax.ShapeDtypeStruct((s, FOX_D), F32),
                   jax.ShapeDtypeStruct((s, FOX_D), F32), jax.ShapeDtypeStruct((FOX_PAIRS, 2, s), F32),
                   jax.ShapeDtypeStruct((FOX_PAIRS, s, 2), F32)],
        sem=("parallel", "arbitrary"), rider=rider, args=(qs, kn, qkvg, do, lse_t, delta_t, ck_c))
    return outs + [riding]


def _ogate_fwd(o, qkvg, *, name):
    s = o.shape[0]
    tr = _pick(s, 512, 8)

    def body(o_ref, g_ref, out_ref):
        out_ref[...] = (o_ref[...] * _sigmoid(g_ref[...])).astype(BF16)

    tile = pl.BlockSpec((tr, FOX_D), lambda i: (i, 0))
    return pl.pallas_call(
        body, name=name, grid=(s // tr,), in_specs=[tile, pl.BlockSpec((tr, FOX_D), lambda i: (i, 3))],
        out_specs=tile, out_shape=jax.ShapeDtypeStruct((s, FOX_D), BF16), compiler_params=_cp(("parallel",)),
    )(o, qkvg)


def _ogate_bwd(dog, o, qkvg, *, name):
    s = o.shape[0]
    tr = _pick(s, 512, 8)

    def body(dog_ref, o_ref, g_ref, do_ref, dg_ref, dl_ref):
        sg = _sigmoid(g_ref[...])
        ov = o_ref[...]
        dog_v = dog_ref[...]
        do = dog_v * sg
        do_ref[...] = do
        dg_ref[...] = (dog_v * ov * sg * (1.0 - sg)).astype(BF16)
        dl_ref[...] = _dotf(do * ov, _group_matrix(FOX_D, FOX_HD))

    tile = pl.BlockSpec((tr, FOX_D), lambda i: (i, 0))
    return pl.pallas_call(
        body, name=name, grid=(s // tr,), in_specs=[tile, tile, pl.BlockSpec((tr, FOX_D), lambda i: (i, 3))],
        out_specs=[tile, tile, pl.BlockSpec((tr, FOX_H), lambda i: (i, 0))],
        out_shape=[jax.ShapeDtypeStruct((s, FOX_D), F32), jax.ShapeDtypeStruct((s, FOX_D), BF16),
                   jax.ShapeDtypeStruct((s, FOX_H), F32)],
        compiler_params=_cp(("parallel",)),
    )(dog, o, qkvg)


def _loss_head(h, g, target, *, name):
    s, d = h.shape
    tr = _pick(s, 512, 8)

    def body(h_ref, g_ref, t_ref, loss_ref, dh_ref, dg_ref):
        i = pl.program_id(0)
        x = h_ref[...]
        gv = g_ref[...]
        r = lax.rsqrt(jnp.mean(x * x, axis=-1, keepdims=True) + EPS)
        xh = x * r
        err = xh * gv - t_ref[...]
        part = 0.5 * jnp.sum(jnp.sum(err * err, axis=1, keepdims=True) * (1.0 / d), axis=0, keepdims=True)
        dy = err * (1.0 / d)
        dyg = dy * gv
        dh_ref[...] = r * (dyg - xh * jnp.mean(dyg * xh, axis=-1, keepdims=True))
        dgp = jnp.sum(dy * xh, axis=0, keepdims=True)

        @pl.when(i == 0)
        def _():
            loss_ref[...] = jnp.zeros_like(loss_ref) + part
            dg_ref[...] = dgp

        @pl.when(i > 0)
        def _():
            loss_ref[...] += part
            dg_ref[...] += dgp

    tile = pl.BlockSpec((tr, d), lambda i: (i, 0))
    vec = pl.BlockSpec((1, d), lambda i: (0, 0))
    return pl.pallas_call(
        body, name=name, grid=(s // tr,), in_specs=[tile, vec, tile],
        out_specs=[pl.BlockSpec((1, LANES), lambda i: (0, 0)), tile, vec],
        out_shape=[jax.ShapeDtypeStruct((1, LANES), F32), jax.ShapeDtypeStruct((s, d), F32),
                   jax.ShapeDtypeStruct((1, d), F32)],
        compiler_params=_cp(("arbitrary",)),
    )(h, g, target)


def _adamw(w, g, m, v, *, name):
    rows, cols = w.shape
    tr = _pick(rows, 256, 8)
    c1 = 1.0 - ADAM_B1 ** ADAM_STEP
    c2 = 1.0 - ADAM_B2 ** ADAM_STEP

    def body(w_ref, g_ref, m_ref, v_ref, d_ref, nm_ref, nv_ref):
        gv = g_ref[...]
        nm = ADAM_B1 * m_ref[...] + (1.0 - ADAM_B1) * gv
        nv = ADAM_B2 * v_ref[...] + (1.0 - ADAM_B2) * (gv * gv)
        d_ref[...] = -ADAM_LR * ((nm / c1) / (jnp.sqrt(nv / c2) + ADAM_EPS) + ADAM_WD * w_ref[...])
        nm_ref[...] = nm
        nv_ref[...] = nv

    tile = pl.BlockSpec((tr, cols), lambda i: (i, 0))
    shp = jax.ShapeDtypeStruct((rows, cols), F32)
    return pl.pallas_call(
        body, name=name, grid=(rows // tr,), in_specs=[tile] * 4, out_specs=[tile] * 3, out_shape=[shp] * 3,
        compiler_params=_cp(("parallel",)),
    )(w, g, m, v)


ANY = pl.BlockSpec(memory_space=pl.ANY)
N_DEV = 8


def _coords():
    return lax.axis_index("x"), lax.axis_index("y"), lax.axis_index("c")


def _other_chips(x, y):
    return [(1 - x, y), (x, 1 - y), (1 - x, 1 - y)]


def _allgather_small(buf, *, name, with_sum):
    rows = buf.shape[0]

    def body(*refs):
        if with_sum:
            x_ref, out_ref, sum_ref, send_sems, recv_sems = refs
        else:
            x_ref, out_ref, send_sems, recv_sems = refs
        x, y, c = _coords()
        me = 4 * x + 2 * y + c
        out_ref[me] = x_ref[...]
        copies = []
        for rel in range(1, N_DEV):
            px = (1 - x) if rel & 4 else x
            py = (1 - y) if rel & 2 else y
            pc = (1 - c) if rel & 1 else c
            cp = pltpu.make_async_remote_copy(
                src_ref=x_ref, dst_ref=out_ref.at[me], send_sem=send_sems.at[rel - 1], recv_sem=recv_sems.at[rel - 1],
                device_id=(px, py, pc), device_id_type=MESH)
            cp.start()
            copies.append(cp)
        for cp in copies:
            cp.wait()
        if with_sum:
            acc = out_ref[0]
            for k in range(1, N_DEV):
                acc = acc + out_ref[k]
            sum_ref[...] = acc

    slots = jax.ShapeDtypeStruct((N_DEV, rows, LANES), F32)
    vm = pl.BlockSpec(memory_space=pltpu.VMEM)
    out_shape = [slots, jax.ShapeDtypeStruct((rows, LANES), F32)] if with_sum else [slots]
    return pl.pallas_call(
        body, name=name, in_specs=[vm], out_specs=[vm] * len(out_shape), out_shape=out_shape,
        scratch_shapes=[pltpu.SemaphoreType.DMA((N_DEV - 1,)), pltpu.SemaphoreType.DMA((N_DEV - 1,))],
    )(buf)


class _Gather:
    per_array = 6

    def __init__(self, arrays):
        self.arrays = list(arrays)

    def out_shapes(self):
        return [jax.ShapeDtypeStruct((4,) + a.shape, a.dtype) for a in self.arrays]

    @staticmethod
    def _ici(ins, outs, send_sems, recv_sems, t, j, px, py, c, slot):
        return pltpu.make_async_remote_copy(
            src_ref=ins[t].at[c], dst_ref=outs[t].at[slot, c], send_sem=send_sems.at[6 * t + j],
            recv_sem=recv_sems.at[6 * t + j], device_id=(px, py, c), device_id_type=MESH)

    @staticmethod
    def _d2d(outs, send_sems, recv_sems, t, j, kj, half, sibling):
        return pltpu.make_async_remote_copy(
            src_ref=outs[t].at[kj, half], dst_ref=outs[t].at[kj, half], send_sem=send_sems.at[6 * t + 3 + j],
            recv_sem=recv_sems.at[6 * t + 3 + j], device_id=sibling, device_id_type=MESH)

    def start(self, ins, outs, send_sems, recv_sems):
        x, y, c = _coords()
        for t in range(len(ins)):
            for j, (px, py) in enumerate(_other_chips(x, y)):
                self._ici(ins, outs, send_sems, recv_sems, t, j, px, py, c, 2 * x + y).start()

    def finish(self, ins, outs, send_sems, recv_sems):
        x, y, c = _coords()
        chips = _other_chips(x, y)
        sibling = (x, y, 1 - c)
        started = []
        for t in range(len(ins)):
            for j, (px, py) in enumerate(chips):
                ici = self._ici(ins, outs, send_sems, recv_sems, t, j, px, py, c, 2 * px + py)
                ici.wait_recv()
                fwd = self._d2d(outs, send_sems, recv_sems, t, j, 2 * px + py, c, sibling)
                fwd.start()
                started += [ici, fwd]
        for t in range(len(ins)):
            for j, (px, py) in enumerate(chips):
                self._d2d(outs, send_sems, recv_sems, t, j, 2 * px + py, 1 - c, sibling).wait_recv()
        for cp in started:
            cp.wait_send()


class _Exchange:
    per_array = 7

    def __init__(self, arrays):
        self.arrays = list(arrays)

    def out_shapes(self):
        return [jax.ShapeDtypeStruct((7,) + a.shape[2:], a.dtype) for a in self.arrays]

    @staticmethod
    def _copies(ins, outs, send_sems, recv_sems):
        x, y, c = _coords()
        for t in range(len(ins)):
            for rel in range(1, N_DEV):
                px = (1 - x) if rel & 4 else x
                py = (1 - y) if rel & 2 else y
                pc = (1 - c) if rel & 1 else c
                yield pltpu.make_async_remote_copy(
                    src_ref=ins[t].at[2 * px + py, pc], dst_ref=outs[t].at[rel - 1], send_sem=send_sems.at[7 * t + rel - 1],
                    recv_sem=recv_sems.at[7 * t + rel - 1], device_id=(px, py, pc), device_id_type=MESH)

    def start(self, ins, outs, send_sems, recv_sems):
        for cp in self._copies(ins, outs, send_sems, recv_sems):
            cp.start()

    def finish(self, ins, outs, send_sems, recv_sems):
        for cp in self._copies(ins, outs, send_sems, recv_sems):
            cp.wait()


def _call(body, *, name, grid, in_specs, out_specs, out_shape, scratch_shapes, args, sem, rider=None, prefetch=()):
    n_in, n_out, n_pre = len(in_specs), len(out_specs), len(prefetch)
    n_c = len(rider.arrays) if rider is not None else 0

    def wrapped(*refs):
        pre, rest = refs[:n_pre], refs[n_pre:]
        ins, cins = rest[:n_in], rest[n_in:n_in + n_c]
        outs = rest[n_in + n_c:n_in + n_c + n_out]
        couts = rest[n_in + n_c + n_out:n_in + 2 * n_c + n_out]
        scratch = rest[n_in + 2 * n_c + n_out:]
        if rider is None:
            body(*pre, *ins, *outs, *scratch)
            return
        send_sems, recv_sems = scratch[-2:]
        ids = [pl.program_id(a) for a in range(len(grid))]
        first = functools.reduce(jnp.logical_and, [i == 0 for i in ids])
        last = functools.reduce(jnp.logical_and, [i == g - 1 for i, g in zip(ids, grid)])

        @pl.when(first)
        def _():
            rider.start(cins, couts, send_sems, recv_sems)

        body(*pre, *ins, *outs, *scratch[:-2])

        @pl.when(last)
        def _():
            rider.finish(cins, couts, send_sems, recv_sems)

    if rider is not None:
        nsem = rider.per_array * n_c
        in_specs = list(in_specs) + [ANY] * n_c
        out_specs = list(out_specs) + [ANY] * n_c
        out_shape = list(out_shape) + rider.out_shapes()
        scratch_shapes = list(scratch_shapes) + [pltpu.SemaphoreType.DMA((nsem,)), pltpu.SemaphoreType.DMA((nsem,))]
        args = list(args) + rider.arrays
        sem = ("arbitrary",) * len(grid)
    if n_pre:
        res = pl.pallas_call(
            wrapped, name=name, out_shape=out_shape, compiler_params=_cp(sem),
            grid_spec=pltpu.PrefetchScalarGridSpec(num_scalar_prefetch=n_pre, grid=grid, in_specs=in_specs,
                                                   out_specs=out_specs, scratch_shapes=scratch_shapes),
        )(*prefetch, *args)
    else:
        res = pl.pallas_call(
            wrapped, name=name, grid=grid, in_specs=in_specs, out_specs=out_specs, out_shape=out_shape,
            scratch_shapes=scratch_shapes, compiler_params=_cp(sem),
        )(*args)
    return list(res[:n_out]), list(res[n_out:])


def _run_rider(rider, *, name):
    n = len(rider.arrays)

    def body(*refs):
        ins, outs = refs[:n], refs[n:2 * n]
        send_sems, recv_sems = refs[2 * n:]
        rider.start(ins, outs, send_sems, recv_sems)
        rider.finish(ins, outs, send_sems, recv_sems)

    nsem = rider.per_array * n
    return pl.pallas_call(
        body, name=name, in_specs=[ANY] * n, out_specs=[ANY] * n, out_shape=rider.out_shapes(),
        scratch_shapes=[pltpu.SemaphoreType.DMA((nsem,)), pltpu.SemaphoreType.DMA((nsem,))],
    )(*rider.arrays)


def _sibling_swap(arrs, *, name):
    n = len(arrs)

    def body(*refs):
        ins, outs = refs[:n], refs[n:2 * n]
        send_sems, recv_sems = refs[2 * n:]
        x, y, c = _coords()
        copies = []
        for t in range(n):
            cp = pltpu.make_async_remote_copy(
                src_ref=ins[t], dst_ref=outs[t], send_sem=send_sems.at[t], recv_sem=recv_sems.at[t],
                device_id=(x, y, 1 - c), device_id_type=MESH)
            cp.start()
            copies.append(cp)
        for cp in copies:
            cp.wait()

    return pl.pallas_call(
        body, name=name, in_specs=[ANY] * n, out_specs=[ANY] * n,
        out_shape=[jax.ShapeDtypeStruct(a.shape, a.dtype) for a in arrs],
        scratch_shapes=[pltpu.SemaphoreType.DMA((n,)), pltpu.SemaphoreType.DMA((n,))],
    )(*arrs)


def _add_selected(stack, others, sel, *, name):
    _, m, cols = stack.shape
    q = others.shape[0]
    tr = _pick(m, 256, 16)

    def body(sel_ref, s_ref, o_ref, out_ref):
        acc = s_ref[0].astype(F32)
        for i in range(q):
            acc = acc + o_ref[i].astype(F32)
        out_ref[...] = acc

    return pl.pallas_call(
        body, name=name,
        grid_spec=pltpu.PrefetchScalarGridSpec(
            num_scalar_prefetch=1, grid=(m // tr,),
            in_specs=[pl.BlockSpec((1, tr, cols), lambda i, sel_ref: (sel_ref[0], i, 0)),
                      pl.BlockSpec((q, tr, cols), lambda i, sel_ref: (0, i, 0))],
            out_specs=pl.BlockSpec((tr, cols), lambda i, sel_ref: (i, 0))),
        out_shape=jax.ShapeDtypeStruct((m, cols), F32),
        compiler_params=_cp(("parallel",)),
    )(sel, stack, others)


BIG = ("ssd_w_in", "ssd_w_out", "fox_w_in", "fox_w_out", "ffn_w_up", "ffn_w_down")
COL_SHARDED = ("ssd_w_in", "fox_w_in", "ffn_w_up")
SMALL = (("mix_norm_g", (4, 1024)), ("ffn_norm_g", (4, 1024)), ("ssd_conv_w", (2, 4, 3072)), ("ssd_conv_b", (2, 3072)),
         ("ssd_dt_bias", (2, 32)), ("ssd_a_log", (2, 32)), ("ssd_d", (2, 32)), ("ssd_norm_g", (2, 2048)),
         ("fox_b_f", (2, 16)), ("fox_q_norm_g", (2, 64)), ("fox_k_norm_g", (2, 64)), ("ffn_conv_w", (4, 3, 2816)),
         ("ffn_conv_b", (4, 2816)), ("final_norm_g", (1024,)), ("loss", (1,)))
NAMES = ("mix_norm_g", "ffn_norm_g", "ssd_w_in", "ssd_conv_w", "ssd_conv_b", "ssd_dt_bias", "ssd_a_log", "ssd_d",
         "ssd_norm_g", "ssd_w_out", "fox_w_in", "fox_b_f", "fox_q_norm_g", "fox_k_norm_g", "fox_w_out", "ffn_w_up",
         "ffn_conv_w", "ffn_conv_b", "ffn_w_down", "final_norm_g")


def _pack(parts):
    flat = jnp.concatenate([jnp.reshape(p, (-1,)).astype(F32) for p in parts])
    rows = -(-flat.shape[0] // (8 * LANES)) * 8
    return jnp.pad(flat, (0, rows * LANES - flat.shape[0])).reshape(rows, LANES)


def _unpack(buf, shapes):
    flat = buf.reshape(-1)
    out, off = [], 0
    for shp in shapes:
        size = 1
        for d in shp:
            size *= d
        out.append(flat[off:off + size].reshape(shp))
        off += size
    return out


def _pad_lanes(a):
    return jnp.pad(a, ((0, 0), (0, LANES - a.shape[1])))


def _pad8(w):
    return jnp.pad(w, ((0, 8 - w.shape[0]), (0, 0)))


def _ssd_forward(h, p, name, rider=None):
    s = h.shape[0]
    hn = _rms_fwd(h, p["mix_g"], gw=D_MODEL, ncol=1, name=f"{name}_norm")
    zx = _matmul(hn, p["w_zx"], mode="nn", name=f"{name}_proj")
    dtp = _matmul(hn, p["w_dt"], mode="nn", name=f"{name}_proj_dt")
    xbc, _ = _conv_fwd(zx, p["conv_w8"], p["conv_b"], kw=SSD_K, width=SSD_CONV_DIM, u_col0=SSD_DI, name=f"{name}_conv")
    dt3 = dtp[:, :SSD_H].reshape(s, SSD_G, SSD_HPG)
    dtg, dtg_t = jnp.transpose(dt3, (1, 0, 2)), jnp.transpose(dt3, (1, 2, 0))
    sp = (p["bias_r"], p["bias_c"], p["alog_r"], p["alog_c"], p["d_r"])
    y, hprev, riding = _ssd_fwd(xbc, dtg, dtg_t, *sp, name=f"{name}_scan", rider=rider)
    y2 = _rms_fwd(y, p["norm_g"], gw=SSD_DI // SSD_G, ncol=SSD_G, z=zx, name=f"{name}_gnorm")
    out = _matmul(y2, p["w_out"], mode="nn", add=h, name=f"{name}_out")
    return out, dict(h=h, hn=hn, zx=zx, xbc=xbc, dtg=dtg, dtg_t=dtg_t, y=y, hprev=hprev, y2=y2), riding


def _ssd_backward(dh1, p, a, name, ride=()):
    s = dh1.shape[0]
    g = {}
    dy2 = _matmul(dh1, p["w_out"], mode="nt", name=f"{name}_out_dx")
    g["w_out"] = _matmul(a["y2"], dh1, mode="tn", out_dtype=BF16, name=f"{name}_out_dw")
    rider = _Exchange(list(ride) + [_to_slabs(g["w_out"], False)])
    dy, dz, g["norm_g"] = _rms_bwd(a["y"], p["norm_g"], dy2, gw=SSD_DI // SSD_G, ncol=SSD_G, z=a["zx"], name=f"{name}_gnorm_b")
    sp = (p["bias_r"], p["bias_c"], p["alog_r"], p["alog_c"], p["d_r"])
    dx, dbm, dcm, ddt, g["dt_bias"], g["a_log"], g["d"], riding = _ssd_bwd(
        a["xbc"], a["dtg"], a["dtg_t"], *sp, a["hprev"], dy, name=f"{name}_scan_b", rider=rider)
    dxbc, dwb, _ = _conv_bwd(a["zx"], p["conv_w8"], p["conv_b"], [dx, dbm, dcm], kw=SSD_K, width=SSD_CONV_DIM,
                             u_col0=SSD_DI, name=f"{name}_conv_b")
    g["conv_w"], g["conv_b"] = dwb[:SSD_K], dwb[7]
    dzx = jnp.concatenate([dz.astype(BF16), dxbc], axis=1)
    ddtp = _pad_lanes(jnp.transpose(ddt, (1, 0, 2)).reshape(s, SSD_H))
    dhn = _matmul(dzx, p["w_zx"], mode="nt", name=f"{name}_proj_dx")
    dhn = _matmul(ddtp, p["w_dt"], mode="nt", add=dhn, name=f"{name}_proj_dt_dx")
    dw_zx = _matmul(a["hn"], dzx, mode="tn", out_dtype=BF16, name=f"{name}_proj_dw")
    dw_dt = _matmul(a["hn"], ddtp, mode="tn", out_dtype=BF16, name=f"{name}_proj_dt_dw")
    g["w_in"] = jnp.concatenate([dw_zx, dw_dt[:, :SSD_H]], axis=1)
    dh, g["mix_g"] = _rms_bwd(a["h"], p["mix_g"], dhn, gw=D_MODEL, ncol=1, add=dh1, name=f"{name}_norm_b")
    g["w_out_received"] = riding[-1]
    return dh, g, riding[:-1]


def _fox_forward(h, p, name, rider=None):
    s = h.shape[0]
    hn = _rms_fwd(h, p["mix_g"], gw=D_MODEL, ncol=1, name=f"{name}_norm")
    qkvg = _matmul(hn, p["w_qkvg"], mode="nn", name=f"{name}_proj")
    fp = _matmul(hn, p["w_f"], mode="nn", name=f"{name}_proj_f")
    qs = _rms_fwd(qkvg, p["gq"] * FOX_SCALE, gw=FOX_D, ncol=1, x_col0=0, sub=FOX_HD, name=f"{name}_qnorm")
    kn = _rms_fwd(qkvg, p["gk"], gw=FOX_D, ncol=1, x_col0=1, sub=FOX_HD, name=f"{name}_knorm")
    f_t = jnp.transpose(fp[:, :FOX_H])
    cum_t = _fgate_fwd(f_t, p["b_f"], name=f"{name}_fgate")
    ck = cum_t.reshape(FOX_PAIRS, 2, s)
    o, lse, riding = _flash_fwd(qs, kn, qkvg, ck, name=f"{name}_attn", rider=rider)
    og = _ogate_fwd(o, qkvg, name=f"{name}_ogate")
    out = _matmul(og, p["w_out"], mode="nn", add=h, name=f"{name}_out")
    return out, dict(h=h, hn=hn, qkvg=qkvg, qs=qs, kn=kn, f_t=f_t, ck=ck, o=o, lse=lse, og=og), riding


def _fox_backward(dh1, p, a, name, ride=()):
    s = dh1.shape[0]
    g = {}
    dog = _matmul(dh1, p["w_out"], mode="nt", name=f"{name}_out_dx")
    g["w_out"] = _matmul(a["og"], dh1, mode="tn", out_dtype=BF16, name=f"{name}_out_dw")
    rider = _Exchange(list(ride) + [_to_slabs(g["w_out"], False)])
    do, dgate, delta = _ogate_bwd(dog, a["o"], a["qkvg"], name=f"{name}_ogate_b")
    swap = lambda v: jnp.transpose(v, (0, 2, 1))
    dl_t = jnp.transpose(delta.reshape(s, FOX_PAIRS, 2), (1, 2, 0))
    dq, dk, dv, dcq, dck, riding = _flash_bwd(a["qs"], a["kn"], a["qkvg"], do, swap(a["lse"]), dl_t, swap(a["ck"]),
                                              name=f"{name}_attn_b", rider=rider)
    dq_raw, dgq = _rms_bwd(a["qkvg"], p["gq"], dq, gw=FOX_D, ncol=1, x_col0=0, sub=FOX_HD, dx_dtype=BF16, name=f"{name}_qnorm_b")
    dk_raw, dgk = _rms_bwd(a["qkvg"], p["gk"], dk, gw=FOX_D, ncol=1, x_col0=1, sub=FOX_HD, dx_dtype=BF16, name=f"{name}_knorm_b")
    g["gq"] = dgq.reshape(FOX_H, FOX_HD).sum(axis=0)
    g["gk"] = dgk.reshape(FOX_H, FOX_HD).sum(axis=0)
    df_t, dbf = _fgate_bwd(dcq.reshape(FOX_H, s), swap(dck).reshape(FOX_H, s), a["f_t"], p["b_f"], name=f"{name}_fgate_b")
    g["b_f"] = dbf[:, 0]
    dproj = jnp.concatenate([dq_raw, dk_raw, dv.astype(BF16), dgate], axis=1)
    dfp = _pad_lanes(jnp.transpose(df_t))
    dhn = _matmul(dproj, p["w_qkvg"], mode="nt", name=f"{name}_proj_dx")
    dhn = _matmul(dfp, p["w_f"], mode="nt", add=dhn, name=f"{name}_proj_f_dx")
    dw_qkvg = _matmul(a["hn"], dproj, mode="tn", out_dtype=BF16, name=f"{name}_proj_dw")
    dw_f = _matmul(a["hn"], dfp, mode="tn", out_dtype=BF16, name=f"{name}_proj_f_dw")
    g["w_in"] = jnp.concatenate([dw_qkvg, dw_f[:, :FOX_H]], axis=1)
    dh, g["mix_g"] = _rms_bwd(a["h"], p["mix_g"], dhn, gw=D_MODEL, ncol=1, add=dh1, name=f"{name}_norm_b")
    g["w_out_received"] = riding[-1]
    return dh, g, riding[:-1]


def _ffn_forward(h, p, name, rider=None):
    hn = _rms_fwd(h, p["ffn_g"], gw=D_MODEL, ncol=1, name=f"{name}_norm")
    u = _matmul(hn, p["w_up"], mode="nn", name=f"{name}_up")
    act, riding = _conv_fwd(u, p["conv_w8"], p["conv_b"], kw=FFN_K, width=D_FF, u_col0=0, mul_col0=D_FF, out_dtype=BF16,
                            name=f"{name}_glu", rider=rider)
    out = _matmul(act, p["w_down"], mode="nn", add=h, name=f"{name}_down")
    return out, dict(h=h, hn=hn, u=u, act=act), riding


def _ffn_backward(dh2, p, a, name, ride=()):
    g = {}
    dact = _matmul(dh2, p["w_down"], mode="nt", name=f"{name}_down_dx")
    g["w_down"] = _matmul(a["act"], dh2, mode="tn", out_dtype=BF16, name=f"{name}_down_dw")
    du, dwb, riding = _conv_bwd(a["u"], p["conv_w8"], p["conv_b"], dact, kw=FFN_K, width=D_FF, u_col0=0, mul_col0=D_FF,
                                name=f"{name}_glu_b", rider=_Exchange(list(ride)) if ride else None)
    g["conv_w"], g["conv_b"] = dwb[:FFN_K], dwb[7]
    dhn = _matmul(du, p["w_up"], mode="nt", name=f"{name}_up_dx")
    g["w_up"] = _matmul(a["hn"], du, mode="tn", out_dtype=BF16, name=f"{name}_up_dw")
    dh, g["ffn_g"] = _rms_bwd(a["h"], p["ffn_g"], dhn, gw=D_MODEL, ncol=1, add=dh2, name=f"{name}_norm_b")
    return dh, g, riding


def _to_slabs(dw, col_sharded):
    rows, cols = dw.shape
    if col_sharded:
        return jnp.transpose(dw.reshape(rows, 4, cols // 4), (1, 0, 2)).reshape(4, 2, rows // 2, cols // 4)
    return dw.reshape(4, 2, rows // 8, cols)


def kernel(x, mix_norm_g, ffn_norm_g, ssd_w_in, ssd_conv_w, ssd_conv_b, ssd_dt_bias, ssd_a_log, ssd_d, ssd_norm_g, ssd_w_out, fox_w_in, fox_b_f, fox_q_norm_g, fox_k_norm_g, fox_w_out, ffn_w_up, ffn_conv_w, ffn_conv_b, ffn_w_down, final_norm_g, loss_target, m_mix_norm_g, m_ffn_norm_g, m_ssd_w_in, m_ssd_conv_w, m_ssd_conv_b, m_ssd_dt_bias, m_ssd_a_log, m_ssd_d, m_ssd_norm_g, m_ssd_w_out, m_fox_w_in, m_fox_b_f, m_fox_q_norm_g, m_fox_k_norm_g, m_fox_w_out, m_ffn_w_up, m_ffn_conv_w, m_ffn_conv_b, m_ffn_w_down, m_final_norm_g, v_mix_norm_g, v_ffn_norm_g, v_ssd_w_in, v_ssd_conv_w, v_ssd_conv_b, v_ssd_dt_bias, v_ssd_a_log, v_ssd_d, v_ssd_norm_g, v_ssd_w_out, v_fox_w_in, v_fox_b_f, v_fox_q_norm_g, v_fox_k_norm_g, v_fox_w_out, v_ffn_w_up, v_ffn_conv_w, v_ffn_conv_b, v_ffn_w_down, v_final_norm_g):
    w = dict(mix_norm_g=mix_norm_g, ffn_norm_g=ffn_norm_g, ssd_w_in=ssd_w_in, ssd_conv_w=ssd_conv_w, ssd_conv_b=ssd_conv_b,
             ssd_dt_bias=ssd_dt_bias, ssd_a_log=ssd_a_log, ssd_d=ssd_d, ssd_norm_g=ssd_norm_g, ssd_w_out=ssd_w_out,
             fox_w_in=fox_w_in, fox_b_f=fox_b_f, fox_q_norm_g=fox_q_norm_g, fox_k_norm_g=fox_k_norm_g, fox_w_out=fox_w_out,
             ffn_w_up=ffn_w_up, ffn_conv_w=ffn_conv_w, ffn_conv_b=ffn_conv_b, ffn_w_down=ffn_w_down, final_norm_g=final_norm_g)
    m_in = dict(zip(NAMES, (m_mix_norm_g, m_ffn_norm_g, m_ssd_w_in, m_ssd_conv_w, m_ssd_conv_b, m_ssd_dt_bias, m_ssd_a_log,
                            m_ssd_d, m_ssd_norm_g, m_ssd_w_out, m_fox_w_in, m_fox_b_f, m_fox_q_norm_g, m_fox_k_norm_g,
                            m_fox_w_out, m_ffn_w_up, m_ffn_conv_w, m_ffn_conv_b, m_ffn_w_down, m_final_norm_g)))
    v_in = dict(zip(NAMES, (v_mix_norm_g, v_ffn_norm_g, v_ssd_w_in, v_ssd_conv_w, v_ssd_conv_b, v_ssd_dt_bias, v_ssd_a_log,
                            v_ssd_d, v_ssd_norm_g, v_ssd_w_out, v_fox_w_in, v_fox_b_f, v_fox_q_norm_g, v_fox_k_norm_g,
                            v_fox_w_out, v_ffn_w_up, v_ffn_conv_w, v_ffn_conv_b, v_ffn_w_down, v_final_norm_g)))
    cx, cy, cc = _coords()
    chip = 2 * cx + cy
    h = x[0]
    target = loss_target[0]

    conv_shapes = [ssd_conv_w.shape, ffn_conv_w.shape]
    slots = _allgather_small(_pack([ssd_conv_w, ffn_conv_w]), name="gather_conv_w", with_sum=False)[0]
    per_chip = [_unpack(slots[2 * q], conv_shapes) for q in range(4)]
    ssd_conv_full = jnp.concatenate([pc[0] for pc in per_chip], axis=2)
    ffn_conv_full = jnp.concatenate([pc[1] for pc in per_chip], axis=2)
    low = {n: w[n].astype(BF16) for n in BIG}
    sub_weights = dict(ssd=("ssd_w_in", "ssd_w_out"), fox=("fox_w_in", "fox_w_out"), ffn=("ffn_w_up", "ffn_w_down"))

    def shards_of(kind, idx):
        return [low[n][idx].reshape(2, low[n].shape[1] // 2, low[n].shape[2]) for n in sub_weights[kind]]

    def assemble(kind, idx, gathered):
        full = []
        for n, own, gth in zip(sub_weights[kind], shards_of(kind, idx), gathered):
            gth = lax.dynamic_update_slice(gth, own[None], (chip, 0, 0, 0))
            _, _, half, cols = gth.shape
            if n in COL_SHARDED:
                full.append(jnp.transpose(gth.reshape(4, 2 * half, cols), (1, 0, 2)).reshape(2 * half, 4 * cols))
            else:
                full.append(gth.reshape(8 * half, cols))
        return full

    def ssd_params(j, i, weights):
        w_in, w_out = weights
        g3 = lambda v: v.reshape(SSD_G, 1, SSD_HPG)
        g3c = lambda v: v.reshape(SSD_G, SSD_HPG, 1)
        return dict(mix_g=mix_norm_g[i][None], w_zx=w_in[:, :SSD_ZX], w_dt=_pad_lanes(w_in[:, SSD_ZX:]),
                    conv_w8=_pad8(ssd_conv_full[j]), conv_b=ssd_conv_b[j][None], bias_r=g3(ssd_dt_bias[j]),
                    bias_c=g3c(ssd_dt_bias[j]), alog_r=g3(ssd_a_log[j]), alog_c=g3c(ssd_a_log[j]), d_r=g3(ssd_d[j]),
                    norm_g=ssd_norm_g[j][None], w_out=w_out)

    def fox_params(j, i, weights):
        w_in, w_out = weights
        return dict(mix_g=mix_norm_g[i][None], w_qkvg=w_in[:, :4 * FOX_D], w_f=_pad_lanes(w_in[:, 4 * FOX_D:]),
                    gq=jnp.tile(fox_q_norm_g[j], FOX_H)[None], gk=jnp.tile(fox_k_norm_g[j], FOX_H)[None],
                    b_f=fox_b_f[j][:, None], w_out=w_out)

    def ffn_params(i, weights):
        w_up, w_down = weights
        return dict(ffn_g=ffn_norm_g[i][None], w_up=w_up, conv_w8=_pad8(ffn_conv_full[i]), conv_b=ffn_conv_b[i][None],
                    w_down=w_down)

    order = [("ssd", 0), ("ffn", 0), ("fox", 0), ("ffn", 1), ("ssd", 1), ("ffn", 2), ("fox", 1), ("ffn", 3)]
    fetch = {("ssd", 0): [("ffn", 0)], ("ffn", 0): [("fox", 0)], ("fox", 0): [("ffn", 1), ("ssd", 1), ("ffn", 2)],
             ("ssd", 1): [("fox", 1)], ("fox", 1): [("ffn", 3)]}
    ready = {("ssd", 0): assemble("ssd", 0, _run_rider(_Gather(shards_of("ssd", 0)), name="gather_first"))}
    params, acts = {}, {}
    forward = dict(ssd=_ssd_forward, fox=_fox_forward, ffn=_ffn_forward)
    for kind, idx in order:
        if kind == "ssd":
            params[kind, idx] = ssd_params(idx, 2 * idx, ready.pop((kind, idx)))
        elif kind == "fox":
            params[kind, idx] = fox_params(idx, 2 * idx + 1, ready.pop((kind, idx)))
        else:
            params[kind, idx] = ffn_params(idx, ready.pop((kind, idx)))
        wanted = fetch.get((kind, idx), [])
        rider = _Gather([s for sub in wanted for s in shards_of(*sub)]) if wanted else None
        h, acts[kind, idx], riding = forward[kind](h, params[kind, idx], f"{kind}{idx}", rider=rider)
        for q, sub in enumerate(wanted):
            ready[sub] = assemble(*sub, riding[2 * q:2 * q + 2])
    loss_part, dh, d_final_g = _loss_head(h, final_norm_g[None], target, name="loss_head")

    backward = dict(ssd=_ssd_backward, fox=_fox_backward, ffn=_ffn_backward)
    grad_keys = dict(ssd=("w_in", "w_out"), fox=("w_in", "w_out"), ffn=("w_up", "w_down"))
    sub_g, slabs, received = {}, {}, {}
    waiting = []
    for sub in reversed(order):
        kind = sub[0]
        dh, sub_g[sub], got = backward[kind](dh, params[sub], acts[sub], f"{kind}{sub[1]}",
                                             ride=[slabs[key] for key in waiting])
        received.update(zip(waiting, got))
        waiting = []
        for q, (key, n) in enumerate(zip(grad_keys[kind], sub_weights[kind])):
            slabs[sub, q] = _to_slabs(sub_g[sub][key], n in COL_SHARDED)
            if kind != "ffn" and q == 1:
                received[sub, q] = sub_g[sub]["w_out_received"]
            else:
                waiting.append((sub, q))
    received.update(zip(waiting, _run_rider(_Exchange([slabs[key] for key in waiting]), name="rs_last_exchange")))
    grad_x = dh[None]
    ssd_g, fox_g = [sub_g["ssd", 0], sub_g["ssd", 1]], [sub_g["fox", 0], sub_g["fox", 1]]
    mix_g = [ssd_g[0], fox_g[0], ssd_g[1], fox_g[1]]
    ffn_g = [sub_g["ffn", i] for i in range(DEPTH)]

    me = jnp.reshape(2 * chip + cc, (1,)).astype(jnp.int32)
    finals = {}
    for sub in order:
        for q in range(2):
            _, _, m, cols = slabs[sub, q].shape
            finals[sub, q] = _add_selected(slabs[sub, q].reshape(8, m, cols), received[sub, q], me,
                                           name=f"rs_add_{sub[0]}{sub[1]}_{q}")
    keys = list(finals)
    others = dict(zip(keys, _sibling_swap([finals[key] for key in keys], name="rs_result_swap")))
    grads = {}
    for kind, names in sub_weights.items():
        for q, n in enumerate(names):
            subs = [sub for sub in sorted(set(order)) if sub[0] == kind]
            mine = jnp.stack([finals[sub, q] for sub in subs])
            theirs = jnp.stack([others[sub, q] for sub in subs])
            halves = jnp.stack([jnp.where(cc == 0, mine, theirs), jnp.where(cc == 0, theirs, mine)], axis=1)
            grads[n] = halves.reshape(w[n].shape)
    small = dict(
        mix_norm_g=jnp.concatenate([g["mix_g"] for g in mix_g], axis=0),
        ffn_norm_g=jnp.concatenate([g["ffn_g"] for g in ffn_g], axis=0),
        ssd_conv_w=jnp.stack([g["conv_w"] for g in ssd_g]), ssd_conv_b=jnp.stack([g["conv_b"] for g in ssd_g]),
        ssd_dt_bias=jnp.stack([g["dt_bias"].reshape(SSD_H) for g in ssd_g]),
        ssd_a_log=jnp.stack([g["a_log"].reshape(SSD_H) for g in ssd_g]),
        ssd_d=jnp.stack([g["d"].reshape(SSD_H) for g in ssd_g]),
        ssd_norm_g=jnp.concatenate([g["norm_g"] for g in ssd_g], axis=0),
        fox_b_f=jnp.stack([g["b_f"] for g in fox_g]), fox_q_norm_g=jnp.stack([g["gq"] for g in fox_g]),
        fox_k_norm_g=jnp.stack([g["gk"] for g in fox_g]),
        ffn_conv_w=jnp.stack([g["conv_w"] for g in ffn_g]), ffn_conv_b=jnp.stack([g["conv_b"] for g in ffn_g]),
        final_norm_g=d_final_g[0], loss=loss_part[0, :1])
    _, total = _allgather_small(_pack([small[n] for n, _ in SMALL]), name="reduce_small", with_sum=True)
    for (n, shp), val in zip(SMALL, _unpack(total, [shp for _, shp in SMALL])):
        grads[n] = val
    loss = grads.pop("loss")[0]
    grads["ssd_conv_w"] = lax.dynamic_slice_in_dim(grads["ssd_conv_w"], chip * ssd_conv_w.shape[2], ssd_conv_w.shape[2], axis=2)
    grads["ffn_conv_w"] = lax.dynamic_slice_in_dim(grads["ffn_conv_w"], chip * ffn_conv_w.shape[2], ffn_conv_w.shape[2], axis=2)

    deltas, new_m, new_v = {}, {}, {}
    for n in NAMES:
        shp = w[n].shape
        two_d = (1, shp[0]) if len(shp) == 1 else (-1, shp[-1])
        r2 = lambda a: a.reshape(two_d)
        d, nm, nv = _adamw(r2(w[n]), r2(grads[n]), r2(m_in[n]), r2(v_in[n]), name=f"adamw_{n}")
        deltas[n], new_m[n], new_v[n] = d.reshape(shp), nm.reshape(shp), nv.reshape(shp)
    return (loss, grad_x, *[grads[n] for n in NAMES], *[deltas[n] for n in NAMES], *[new_m[n] for n in NAMES],
            *[new_v[n] for n in NAMES])
```

```python
import functools

import jax
import jax.numpy as jnp
from jax import lax
from jax.experimental import pallas as pl
from jax.experimental.pallas import tpu as pltpu

F32 = jnp.float32
BF16 = jnp.bfloat16
HI = lax.Precision.HIGHEST
MESH = pl.DeviceIdType.MESH

D_MODEL = 1024
DEPTH = 4
EPS = 1e-6
SSD_DI = 2048
SSD_HD = 64
SSD_G = 4
SSD_HPG = 8
SSD_N = 128
SSD_K = 4
CHUNK = 128
SSD_CONV_DIM = 3072
SSD_ZX = SSD_DI + SSD_CONV_DIM
SSD_H = 32
FOX_HD = 64
FOX_H = 16
FOX_D = 1024
D_FF = 2816
FFN_K = 3
LANES = 128
VMEM_LIMIT = 56 * 1024 * 1024

ADAM_LR = 0.001
ADAM_B1 = 0.9
ADAM_B2 = 0.999
ADAM_EPS = 1e-08
ADAM_WD = 0.01
ADAM_STEP = 10

NN = (((1,), (0,)), ((), ()))
NT = (((1,), (1,)), ((), ()))
TN = (((0,), (0,)), ((), ()))


def _pick(n, cap, mult=LANES):
    best = None
    for t in range(mult, min(n, cap) + 1, mult):
        if n % t == 0:
            best = t
    return best if best is not None else n


def _cp(sem):
    return pltpu.CompilerParams(dimension_semantics=sem, vmem_limit_bytes=VMEM_LIMIT)


def _sigmoid(x):
    return jax.nn.sigmoid(x)


def _silu(x):
    return x * _sigmoid(x)


def _dsilu(x):
    s = _sigmoid(x)
    return s * (1.0 + x * (1.0 - s))


def _softplus(x):
    e = jnp.exp(-jnp.abs(x))
    u = 1.0 + e
    l1p = jnp.where(u == 1.0, e, jnp.log(u) * (e / (u - 1.0)))
    return jnp.maximum(x, 0.0) + l1p


def _dotf(a, b, dn=NN, *, onehot="b", pieces=2):
    x, e = (a, b) if onehot == "b" else (b, a)
    e = e.astype(BF16)
    acc = None
    for n in range(pieces):
        hi = x.astype(BF16)
        part = lax.dot_general(hi, e, dn, preferred_element_type=F32) if onehot == "b" else \
            lax.dot_general(e, hi, dn, preferred_element_type=F32)
        acc = part if acc is None else acc + part
        if n + 1 < pieces:
            x = x - hi.astype(F32)
    return acc


def _dotb(a, b, dn=NN):
    return lax.dot_general(a.astype(BF16), b.astype(BF16), dn, preferred_element_type=F32)


def _group_matrix(width, sub, transpose=False):
    ng = width // sub
    shape = (ng, width) if transpose else (width, ng)
    lane = lax.broadcasted_iota(jnp.int32, shape, 1 if transpose else 0)
    grp = lax.broadcasted_iota(jnp.int32, shape, 0 if transpose else 1)
    return (lane // sub == grp).astype(F32)


def _gmean(v, sub):
    width = v.shape[-1]
    if sub == width:
        return jnp.mean(v, axis=-1, keepdims=True)
    s = _dotf(v, _group_matrix(width, sub))
    return _dotf(s, _group_matrix(width, sub, transpose=True)) * (1.0 / sub)


def _matmul(a, b, *, mode, name, out_dtype=F32, add=None):
    a_planes = a.shape[0] if (mode == "nt" and a.ndim == 3) else 0
    b_planes = b.shape[0] if (mode == "tn" and b.ndim == 3) else 0
    a2 = (a.shape[1], a.shape[0] * a.shape[2]) if a_planes else a.shape
    b2 = (b.shape[1], b.shape[0] * b.shape[2]) if b_planes else b.shape
    if mode == "nn":
        (m, k), (k2, n) = a2, b2
    elif mode == "nt":
        (m, k), (n, k2) = a2, b2
    else:
        (k, m), (k2, n) = a2, b2
    assert k == k2, (a.shape, b.shape, mode)
    tm, tn = _pick(m, 1536), _pick(n // b_planes if b_planes else n, 1536)
    tk = _pick(k // a_planes if a_planes else k, 1536)
    nk = k // tk
    dn = {"nn": NN, "nt": NT, "tn": TN}[mode]
    has_add = add is not None

    def body(*refs):
        if has_add:
            a_ref, b_ref, add_ref, o_ref, acc_ref = refs
        else:
            a_ref, b_ref, o_ref, acc_ref = refs
            add_ref = None
        kk = pl.program_id(2)
        part = _dotb(a_ref[0] if a_planes else a_ref[...], b_ref[0] if b_planes else b_ref[...], dn)

        def finish(r):
            if has_add:
                r = r + add_ref[...]
            o_ref[...] = r.astype(out_dtype)

        if nk == 1:
            finish(part)
        else:
            @pl.when(kk == 0)
            def _():
                acc_ref[...] = part

            @pl.when(kk > 0)
            def _():
                acc_ref[...] += part

            @pl.when(kk == nk - 1)
            def _():
                finish(acc_ref[...])

    if mode == "nn":
        a_spec = pl.BlockSpec((tm, tk), lambda i, j, q: (i, q))
        b_spec = pl.BlockSpec((tk, tn), lambda i, j, q: (q, j))
    elif mode == "nt":
        per = (k // a_planes) // tk if a_planes else 0
        a_spec = (pl.BlockSpec((1, tm, tk), lambda i, j, q: (q // per, i, q % per)) if a_planes
                  else pl.BlockSpec((tm, tk), lambda i, j, q: (i, q)))
        b_spec = pl.BlockSpec((tn, tk), lambda i, j, q: (j, q))
    else:
        per = (n // b_planes) // tn if b_planes else 0
        a_spec = pl.BlockSpec((tk, tm), lambda i, j, q: (q, i))
        b_spec = (pl.BlockSpec((1, tk, tn), lambda i, j, q: (j // per, q, j % per)) if b_planes
                  else pl.BlockSpec((tk, tn), lambda i, j, q: (q, j)))
    o_spec = pl.BlockSpec((tm, tn), lambda i, j, q: (i, j))
    in_specs = [a_spec, b_spec] + ([o_spec] if has_add else [])
    args = (a, b) + ((add,) if has_add else ())
    return pl.pallas_call(
        body, name=name, grid=(m // tm, n // tn, nk), in_specs=in_specs, out_specs=o_spec,
        out_shape=jax.ShapeDtypeStruct((m, n), out_dtype),
        scratch_shapes=[pltpu.VMEM((tm, tn) if nk > 1 else (8, LANES), F32)],
        compiler_params=_cp(("parallel", "parallel", "arbitrary")),
    )(*args)


def _rms_fwd(x, g, *, gw, ncol, name, x_col0=0, sub=None, z=None, z_col0=0, out_dtype=BF16):
    rows = x.shape[0]
    tr = _pick(rows, 512, 8)
    sub = gw if sub is None else sub
    gated = z is not None

    def body(*refs):
        if gated:
            x_ref, z_ref, g_ref, o_ref = refs
            xv = x_ref[...] * _silu(z_ref[...])
        else:
            x_ref, g_ref, o_ref = refs
            xv = x_ref[...]
        r = lax.rsqrt(_gmean(xv * xv, sub) + EPS)
        o_ref[...] = (xv * r * g_ref[...]).astype(out_dtype)

    specs = [pl.BlockSpec((tr, gw), lambda j, i: (i, x_col0 + j))]
    args = [x]
    if gated:
        specs.append(pl.BlockSpec((tr, gw), lambda j, i: (i, z_col0 + j)))
        args.append(z)
    specs.append(pl.BlockSpec((1, gw), lambda j, i: (0, j)))
    args.append(g)
    return pl.pallas_call(
        body, name=name, grid=(ncol, rows // tr), in_specs=specs,
        out_specs=pl.BlockSpec((tr, gw), lambda j, i: (i, j)),
        out_shape=jax.ShapeDtypeStruct((rows, gw * ncol), out_dtype),
        compiler_params=_cp(("parallel", "parallel")),
    )(*args)


def _rms_bwd(x, g, dy, *, gw, ncol, name, x_col0=0, sub=None, z=None, z_col0=0, add=None, dx_dtype=F32):
    rows = x.shape[0]
    tr = _pick(rows, 512, 8)
    sub = gw if sub is None else sub
    gated = z is not None
    has_add = add is not None

    def body(*refs):
        refs = list(refs)
        x_ref = refs.pop(0)
        z_ref = refs.pop(0) if gated else None
        g_ref = refs.pop(0)
        dy_ref = refs.pop(0)
        add_ref = refs.pop(0) if has_add else None
        dx_ref = refs.pop(0)
        dz_ref = refs.pop(0) if gated else None
        dg_ref = refs.pop(0)
        i = pl.program_id(1)
        xv = x_ref[...]
        if gated:
            zz = z_ref[...]
            yz = xv * _silu(zz)
        else:
            yz = xv
        r = lax.rsqrt(_gmean(yz * yz, sub) + EPS)
        xh = yz * r
        dy = dy_ref[...].astype(F32)
        dyg = dy * g_ref[...]
        d_yz = r * (dyg - xh * _gmean(dyg * xh, sub))
        if gated:
            dx_ref[...] = (d_yz * _silu(zz)).astype(dx_dtype)
            dz_ref[...] = (d_yz * xv * _dsilu(zz)).astype(dx_dtype)
        elif has_add:
            dx_ref[...] = (d_yz + add_ref[...]).astype(dx_dtype)
        else:
            dx_ref[...] = d_yz.astype(dx_dtype)
        part = jnp.sum(dy * xh, axis=0, keepdims=True)

        @pl.when(i == 0)
        def _():
            dg_ref[...] = part

        @pl.when(i > 0)
        def _():
            dg_ref[...] += part

    tile = pl.BlockSpec((tr, gw), lambda j, i: (i, j))
    specs = [pl.BlockSpec((tr, gw), lambda j, i: (i, x_col0 + j))]
    args = [x]
    if gated:
        specs.append(pl.BlockSpec((tr, gw), lambda j, i: (i, z_col0 + j)))
        args.append(z)
    specs += [pl.BlockSpec((1, gw), lambda j, i: (0, j)), tile]
    args += [g, dy]
    if has_add:
        specs.append(tile)
        args.append(add)
    width = gw * ncol
    out_shape = [jax.ShapeDtypeStruct((rows, width), dx_dtype)]
    out_specs = [tile]
    if gated:
        out_shape.append(jax.ShapeDtypeStruct((rows, width), dx_dtype))
        out_specs.append(tile)
    out_shape.append(jax.ShapeDtypeStruct((1, width), F32))
    out_specs.append(pl.BlockSpec((1, gw), lambda j, i: (0, j)))
    return pl.pallas_call(
        body, name=name, grid=(ncol, rows // tr), in_specs=specs, out_specs=out_specs, out_shape=out_shape,
        compiler_params=_cp(("parallel", "arbitrary")),
    )(*args)


HALO = 8


def _conv_rows(tc):
    return 16 * 8 * LANES // tc


def _conv_fwd(u, w8, b, *, kw, width, name, u_col0=0, mul_col0=None, out_dtype=F32, rider=None):
    rows = u.shape[0]
    ts = _pick(rows, 1024, 8)
    tc = _pick(width, 512)
    gated = mul_col0 is not None
    c0 = u_col0 // tc
    m0 = (mul_col0 // tc) if gated else 0
    assert u_col0 % tc == 0 and (not gated or mul_col0 % tc == 0)

    def body(*refs):
        if gated:
            cur_ref, halo_ref, mul_ref, w_ref, b_ref, o_ref, ext = refs
        else:
            cur_ref, halo_ref, w_ref, b_ref, o_ref, ext = refs
        i = pl.program_id(0)
        ext[pl.ds(0, HALO), :] = jnp.where(i == 0, 0.0, halo_ref[...])
        ext[pl.ds(HALO, ts), :] = cur_ref[...]
        bias = b_ref[...]
        taps = [w_ref[k:k + 1, :] for k in range(kw)]
        rb = _conv_rows(tc)
        for r0 in range(0, ts, rb):
            pre = bias + taps[0] * ext[pl.ds(r0 + HALO - (kw - 1), rb), :]
            for k in range(1, kw):
                pre = pre + taps[k] * ext[pl.ds(r0 + HALO - (kw - 1) + k, rb), :]
            act = _silu(pre)
            if gated:
                act = act * mul_ref[pl.ds(r0, rb), :]
            o_ref[pl.ds(r0, rb), :] = act.astype(out_dtype)

    hb = ts // HALO
    specs = [pl.BlockSpec((ts, tc), lambda i, j: (i, c0 + j)),
             pl.BlockSpec((HALO, tc), lambda i, j: (jnp.maximum(i * hb - 1, 0), c0 + j))]
    args = [u, u]
    if gated:
        specs.append(pl.BlockSpec((ts, tc), lambda i, j: (i, m0 + j)))
        args.append(u)
    specs += [pl.BlockSpec((8, tc), lambda i, j: (0, j)), pl.BlockSpec((1, tc), lambda i, j: (0, j))]
    args += [w8, b]
    outs, riding = _call(
        body, name=name, grid=(rows // ts, width // tc), in_specs=specs,
        out_specs=[pl.BlockSpec((ts, tc), lambda i, j: (i, j))],
        out_shape=[jax.ShapeDtypeStruct((rows, width), out_dtype)],
        scratch_shapes=[pltpu.VMEM((ts + HALO, tc), F32)], sem=("parallel", "parallel"), rider=rider, args=args)
    return outs + [riding]


def _conv_bwd(u, w8, b, dact, *, kw, width, name, u_col0=0, mul_col0=None, du_dtype=BF16, rider=None):
    rows = u.shape[0]
    ts = _pick(rows, 1024, 8)
    tc = _pick(width, 512)
    pieces = list(dact) if isinstance(dact, (list, tuple)) else [dact]
    firsts, seen = [], 0
    for piece in pieces:
        assert piece.shape[1] % tc == 0, (piece.shape, tc)
        firsts.append(seen // tc)
        seen += piece.shape[1]
    assert seen == width
    gated = mul_col0 is not None
    c0 = u_col0 // tc
    m0 = (mul_col0 // tc) if gated else 0
    nt = rows // ts
    hb = ts // HALO

    def body(*refs):
        refs = list(refs)
        cur_ref, halo_ref = refs.pop(0), refs.pop(0)
        mul_ref = refs.pop(0) if gated else None
        w_ref, b_ref = refs.pop(0), refs.pop(0)
        da_refs = [refs.pop(0) for _ in pieces]
        col_tile = pl.program_id(0)
        du_ref = refs.pop(0)
        dwb_ref, ext_u, ext_d = refs
        t = pl.program_id(1)
        ti = nt - 1 - t
        ext_u[pl.ds(0, HALO), :] = jnp.where(ti == 0, 0.0, halo_ref[...])
        ext_u[pl.ds(HALO, ts), :] = cur_ref[...]

        @pl.when(t == 0)
        def _():
            ext_d[pl.ds(ts, HALO), :] = jnp.zeros((HALO, tc), F32)
            dwb_ref[...] = jnp.zeros((8, tc), F32)

        bias = b_ref[...]
        taps = [w_ref[k:k + 1, :] for k in range(kw)]
        rb = _conv_rows(tc)
        dw_acc = [jnp.zeros((1, tc), F32) for _ in range(kw)]
        db_acc = jnp.zeros((1, tc), F32)
        for r0 in reversed(range(0, ts, rb)):
            shifted = [ext_u[pl.ds(r0 + HALO - (kw - 1) + k, rb), :] for k in range(kw)]
            pre = bias + taps[0] * shifted[0]
            for k in range(1, kw):
                pre = pre + taps[k] * shifted[k]
            sg = _sigmoid(pre)
            dsilu = sg * (1.0 + pre * (1.0 - sg))
            da = da_refs[0][pl.ds(r0, rb), :].astype(F32)
            for first, ref in zip(firsts[1:], da_refs[1:]):
                da = jnp.where(col_tile >= first, ref[pl.ds(r0, rb), :].astype(F32), da)
            if gated:
                du_ref[1, pl.ds(r0, rb), :] = (da * (pre * sg)).astype(du_dtype)
                dgp = da * mul_ref[pl.ds(r0, rb), :] * dsilu
            else:
                dgp = da * dsilu
            ext_d[pl.ds(r0, rb), :] = dgp
            du = taps[kw - 1] * dgp
            for k in range(kw - 1):
                du = du + taps[k] * ext_d[pl.ds(r0 + kw - 1 - k, rb), :]
            if gated:
                du_ref[0, pl.ds(r0, rb), :] = du.astype(du_dtype)
            else:
                du_ref[pl.ds(r0, rb), :] = du.astype(du_dtype)
            for k in range(kw):
                dw_acc[k] = dw_acc[k] + jnp.sum(dgp * shifted[k], axis=0, keepdims=True)
            db_acc = db_acc + jnp.sum(dgp, axis=0, keepdims=True)
        for k in range(kw):
            dwb_ref[k:k + 1, :] += dw_acc[k]
        dwb_ref[7:8, :] += db_acc
        ext_d[pl.ds(ts, HALO), :] = ext_d[pl.ds(0, HALO), :]

    specs = [pl.BlockSpec((ts, tc), lambda j, t: (nt - 1 - t, c0 + j)),
             pl.BlockSpec((HALO, tc), lambda j, t: (jnp.maximum((nt - 1 - t) * hb - 1, 0), c0 + j))]
    args = [u, u]
    if gated:
        specs.append(pl.BlockSpec((ts, tc), lambda j, t: (nt - 1 - t, m0 + j)))
        args.append(u)
    tile = pl.BlockSpec((ts, tc), lambda j, t: (nt - 1 - t, j))
    specs += [pl.BlockSpec((8, tc), lambda j, t: (0, j)), pl.BlockSpec((1, tc), lambda j, t: (0, j))]
    args += [w8, b]
    for first, piece in zip(firsts, pieces):
        count = piece.shape[1] // tc

        def piece_map(j, t, first=first, count=count):
            mine = jnp.logical_and(j >= first, j < first + count)
            return (jnp.where(mine, nt - 1 - t, 0), jnp.clip(j - first, 0, count - 1))

        specs.append(pl.BlockSpec((ts, tc), piece_map))
        args.append(piece)
    if gated:
        out_shape = [jax.ShapeDtypeStruct((2, rows, width), du_dtype)]
        out_specs = [pl.BlockSpec((2, ts, tc), lambda j, t: (0, nt - 1 - t, j))]
    else:
        out_shape = [jax.ShapeDtypeStruct((rows, width), du_dtype)]
        out_specs = [tile]
    out_shape.append(jax.ShapeDtypeStruct((8, width), F32))
    out_specs.append(pl.BlockSpec((8, tc), lambda j, t: (0, j)))
    outs, riding = _call(
        body, name=name, grid=(width // tc, nt), in_specs=specs, out_specs=out_specs, out_shape=out_shape,
        scratch_shapes=[pltpu.VMEM((ts + HALO, tc), F32), pltpu.VMEM((ts + HALO, tc), F32)],
        sem=("parallel", "arbitrary"), rider=rider, args=args)
    return outs + [riding]


GW = SSD_HPG * SSD_HD


def _ssd_common(x, bm, cm, dt_raw, dt_raw_t, bias_r, bias_c, alog_r, alog_c):
    row = lax.broadcasted_iota(jnp.int32, (CHUNK, CHUNK), 0)
    col = lax.broadcasted_iota(jnp.int32, (CHUNK, CHUNK), 1)
    causal = row >= col
    tril = causal.astype(F32)
    triu = (row <= col).astype(F32)
    spread = _group_matrix(GW, SSD_HD, transpose=True)
    dt = _softplus(dt_raw + bias_r)
    dt_t = _softplus(dt_raw_t + bias_c)
    a_r = -jnp.exp(alog_r)
    a_c = -jnp.exp(alog_c)
    acs = _dotf(tril, dt * a_r, onehot="a", pieces=3)
    acs_t = _dotf(dt_t * a_c, triu, pieces=3)
    last = acs[CHUNK - 1:CHUNK, :]
    ds = jnp.exp(last - acs)
    cd = jnp.exp(last)
    c = dict(causal=causal, tril=tril, triu=triu, spread=spread, dt=dt, a_r=a_r, acs=acs, acs_t=acs_t, ds=ds, cd=cd)
    c["eb"] = _dotf(jnp.exp(acs), spread)
    c["dsb"] = _dotf(ds, spread)
    c["cdb"] = _dotf(cd, spread)
    c["dtb"] = _dotf(dt, spread)
    c["xdt"] = x * c["dtb"]
    c["cb"] = _dotb(cm, bm, NT)
    return c


def _ssd_lam(c, r):
    diff = c["acs"][:, r:r + 1] - c["acs_t"][r:r + 1, :]
    return jnp.exp(jnp.where(c["causal"], diff, -jnp.inf))


GP = 2


def _ssd_specs(nc, rev):
    def ci(t):
        return (nc - 1 - t) if rev else t
    xs = pl.BlockSpec((CHUNK, GP * GW), lambda g, t: (ci(t), g))
    bs = pl.BlockSpec((CHUNK, GP * SSD_N), lambda g, t: (ci(t), SSD_DI // (GP * SSD_N) + g))
    cs = pl.BlockSpec((CHUNK, GP * SSD_N), lambda g, t: (ci(t), (SSD_DI // SSD_N + SSD_G) // GP + g))
    dts = pl.BlockSpec((GP, CHUNK, 8), lambda g, t: (g, ci(t), 0))
    dtts = pl.BlockSpec((GP, 8, CHUNK), lambda g, t: (g, 0, ci(t)))
    pr = pl.BlockSpec((GP, 1, 8), lambda g, t: (g, 0, 0))
    pc = pl.BlockSpec((GP, 8, 1), lambda g, t: (g, 0, 0))
    hs = pl.BlockSpec((1, GP, SSD_N, GW), lambda g, t: (ci(t), g, 0, 0))
    return xs, bs, cs, dts, dtts, pr, pc, hs


def _ssd_fwd(xbc, dtg, dtg_t, bias_r, bias_c, alog_r, alog_c, d_r, *, name, rider=None):
    s = xbc.shape[0]
    nc = s // CHUNK
    xs, bs, cs, dts, dtts, pr, pc, hs = _ssd_specs(nc, False)

    def body(x_ref, b_ref, c_ref, dt_ref, dtt_ref, br_ref, bc_ref, ar_ref, ac_ref, d_ref, y_ref, hp_ref, h_sc):
        t = pl.program_id(1)

        @pl.when(t == 0)
        def _():
            h_sc[...] = jnp.zeros_like(h_sc)

        for gg in range(GP):
            wide, narrow = slice(gg * GW, (gg + 1) * GW), slice(gg * SSD_N, (gg + 1) * SSD_N)
            x, bm, cm = x_ref[:, wide], b_ref[:, narrow], c_ref[:, narrow]
            c = _ssd_common(x, bm, cm, dt_ref[gg], dtt_ref[gg], br_ref[gg], bc_ref[gg], ar_ref[gg], ac_ref[gg])
            h = h_sc[gg]
            hp_ref[0, gg] = h
            xdt = c["xdt"]
            pieces = []
            for r in range(SSD_HPG):
                m = c["cb"] * _ssd_lam(c, r)
                pieces.append(_dotb(m, xdt[:, r * SSD_HD:(r + 1) * SSD_HD]))
            y = jnp.concatenate(pieces, axis=1) + c["eb"] * _dotb(cm, h) + x * _dotf(d_ref[gg], c["spread"])
            y_ref[:, wide] = y
            h_sc[gg] = h * c["cdb"] + _dotb(bm, xdt * c["dsb"], TN)

    outs, riding = _call(
        body, name=name, grid=(SSD_G // GP, nc),
        in_specs=[xs, bs, cs, dts, dtts, pr, pc, pr, pc, pr],
        out_specs=[xs, hs],
        out_shape=[jax.ShapeDtypeStruct((s, SSD_DI), F32), jax.ShapeDtypeStruct((nc, SSD_G, SSD_N, GW), F32)],
        scratch_shapes=[pltpu.VMEM((GP, SSD_N, GW), F32)], sem=("parallel", "arbitrary"), rider=rider,
        args=(xbc, xbc, xbc, dtg, dtg_t, bias_r, bias_c, alog_r, alog_c, d_r))
    return outs + [riding]


def _ssd_bwd(xbc, dtg, dtg_t, bias_r, bias_c, alog_r, alog_c, d_r, hprev, dy, *, name, rider=None):
    s = xbc.shape[0]
    nc = s // CHUNK
    xs, bs, cs, dts, dtts, pr, pc, hs = _ssd_specs(nc, True)
    gsum = functools.partial(_group_matrix, GW, SSD_HD)

    def body(x_ref, b_ref, c_ref, dt_ref, dtt_ref, br_ref, bc_ref, ar_ref, ac_ref, d_ref, hp_ref, dy_ref,
             dx_ref, db_ref, dc_ref, ddt_ref, dbias_ref, dalog_ref, dd_ref, dh_sc):
        t = pl.program_id(1)

        @pl.when(t == 0)
        def _():
            dh_sc[...] = jnp.zeros_like(dh_sc)
            dbias_ref[...] = jnp.zeros_like(dbias_ref)
            dalog_ref[...] = jnp.zeros_like(dalog_ref)
            dd_ref[...] = jnp.zeros_like(dd_ref)

        for gg in range(GP):
            wide, narrow = slice(gg * GW, (gg + 1) * GW), slice(gg * SSD_N, (gg + 1) * SSD_N)
            x, bm, cm = x_ref[:, wide], b_ref[:, narrow], c_ref[:, narrow]
            c = _ssd_common(x, bm, cm, dt_ref[gg], dtt_ref[gg], br_ref[gg], bc_ref[gg], ar_ref[gg], ac_ref[gg])
            lanesum = gsum()
            h = hp_ref[0, gg]
            dh = dh_sc[gg]
            dy = dy_ref[:, wide]
            xdt, dsb = c["xdt"], c["dsb"]
            skip = _dotf(d_ref[gg], c["spread"])
            dd_ref[gg] += jnp.sum(_dotf(dy * x, lanesum), axis=0, keepdims=True)
            dacs = _dotf(dy * (c["eb"] * _dotb(cm, h)), lanesum)
            edy = c["eb"] * dy
            dcm = _dotb(edy, h, NT)
            dh_prev = _dotb(cm, edy, TN)
            bdh = _dotb(bm, dh)
            dxdt = dsb * bdh
            dbm = _dotb(dsb * xdt, dh, NT)
            t1 = _dotf(xdt * bdh, lanesum) * c["ds"]
            dacs = dacs - t1
            dlast = (jnp.sum(t1, axis=0, keepdims=True)
                     + jnp.sum(_dotf(dh * h, lanesum), axis=0, keepdims=True) * c["cd"])
            dcb = jnp.zeros((CHUNK, CHUNK), F32)
            pieces = []
            ones8 = jnp.ones((CHUNK, 8), F32)
            head = lax.broadcasted_iota(jnp.int32, (1, 8), 1)
            for r in range(SSD_HPG):
                sl = slice(r * SSD_HD, (r + 1) * SSD_HD)
                lam = _ssd_lam(c, r)
                m = c["cb"] * lam
                dm = _dotb(dy[:, sl], xdt[:, sl], NT)
                dcb = dcb + dm * lam
                gm = dm * m
                dacs = dacs + ((jnp.sum(gm, axis=1, keepdims=True) - _dotf(gm, ones8, TN, pieces=3))
                               * (head == r).astype(F32))
                pieces.append(_dotb(m, dy[:, sl], TN))
            dxdt = dxdt + jnp.concatenate(pieces, axis=1)
            dcm = dcm + _dotb(dcb, bm)
            dbm = dbm + _dotb(dcb, cm, TN)
            dx_ref[:, wide] = dy * skip + dxdt * c["dtb"]
            db_ref[:, narrow] = dbm
            dc_ref[:, narrow] = dcm
            rowid = lax.broadcasted_iota(jnp.int32, (CHUNK, 8), 0)
            dacs = dacs + jnp.where(rowid == CHUNK - 1, dlast, 0.0)
            dda = _dotf(c["triu"], dacs, onehot="a", pieces=3)
            ddt = _dotf(dxdt * x, lanesum) + dda * c["a_r"]
            ddt_raw = ddt * _sigmoid(dt_ref[gg] + br_ref[gg])
            ddt_ref[gg] = ddt_raw
            dbias_ref[gg] += jnp.sum(ddt_raw, axis=0, keepdims=True)
            dalog_ref[gg] += jnp.sum(dda * c["dt"], axis=0, keepdims=True) * c["a_r"]
            dh_sc[gg] = dh_prev + dh * c["cdb"]

    ci = lambda t: nc - 1 - t
    nspec = pl.BlockSpec((CHUNK, GP * SSD_N), lambda g, t: (ci(t), g))
    outs, riding = _call(
        body, name=name, grid=(SSD_G // GP, nc),
        in_specs=[xs, bs, cs, dts, dtts, pr, pc, pr, pc, pr, hs, xs],
        out_specs=[xs, nspec, nspec, dts, pr, pr, pr],
        out_shape=[jax.ShapeDtypeStruct((s, SSD_DI), F32), jax.ShapeDtypeStruct((s, SSD_G * SSD_N), F32),
                   jax.ShapeDtypeStruct((s, SSD_G * SSD_N), F32), jax.ShapeDtypeStruct((SSD_G, s, 8), F32),
                   jax.ShapeDtypeStruct((SSD_G, 1, 8), F32), jax.ShapeDtypeStruct((SSD_G, 1, 8), F32),
                   jax.ShapeDtypeStruct((SSD_G, 1, 8), F32)],
        scratch_shapes=[pltpu.VMEM((GP, SSD_N, GW), F32)], sem=("parallel", "arbitrary"), rider=rider,
        args=(xbc, xbc, xbc, dtg, dtg_t, bias_r, bias_c, alog_r, alog_c, d_r, hprev, dy))
    return outs + [riding]


FOX_PAIRS = FOX_H // 2
FOX_SCALE = FOX_HD ** -0.5
NEG_INF = -jnp.inf


def _fgate_fwd(f_t, b_c, *, name):
    hh, s = f_t.shape
    tb = _pick(s, 512)
    nb = s // tb

    def body(f_ref, b_ref, o_ref, carry):
        t = pl.program_id(0)

        @pl.when(t == 0)
        def _():
            carry[...] = jnp.zeros_like(carry)

        lf = -_softplus(-(f_ref[...] + b_ref[...]))
        row = lax.broadcasted_iota(jnp.int32, (tb, tb), 0)
        col = lax.broadcasted_iota(jnp.int32, (tb, tb), 1)
        cum = _dotf(lf, (row <= col).astype(F32), pieces=3) + carry[:, 0:1]
        o_ref[...] = cum
        carry[:, 0:1] = cum[:, tb - 1:tb]

    return pl.pallas_call(
        body, name=name, grid=(nb,),
        in_specs=[pl.BlockSpec((hh, tb), lambda t: (0, t)), pl.BlockSpec((hh, 1), lambda t: (0, 0))],
        out_specs=pl.BlockSpec((hh, tb), lambda t: (0, t)),
        out_shape=jax.ShapeDtypeStruct((hh, s), F32),
        scratch_shapes=[pltpu.VMEM((hh, LANES), F32)],
        compiler_params=_cp(("arbitrary",)),
    )(f_t, b_c)


def _fgate_bwd(dcum_q_t, dcum_k_t, f_t, b_c, *, name):
    hh, s = f_t.shape
    tb = _pick(s, 512)
    nb = s // tb

    def body(dq_ref, d_ref, f_ref, b_ref, df_ref, db_ref, carry):
        t = pl.program_id(0)

        @pl.when(t == 0)
        def _():
            carry[...] = jnp.zeros_like(carry)
            db_ref[...] = jnp.zeros_like(db_ref)

        d = d_ref[...] + dq_ref[...]
        row = lax.broadcasted_iota(jnp.int32, (tb, tb), 0)
        col = lax.broadcasted_iota(jnp.int32, (tb, tb), 1)
        rev = _dotf(d, (row >= col).astype(F32), pieces=3) + carry[:, 0:1]
        df = rev * _sigmoid(-(f_ref[...] + b_ref[...]))
        df_ref[...] = df
        db_ref[...] += jnp.sum(df, axis=1, keepdims=True)
        carry[:, 0:1] = rev[:, 0:1]

    blk = pl.BlockSpec((hh, tb), lambda t: (0, nb - 1 - t))
    return pl.pallas_call(
        body, name=name, grid=(nb,),
        in_specs=[blk, blk, blk, pl.BlockSpec((hh, 1), lambda t: (0, 0))],
        out_specs=[blk, pl.BlockSpec((hh, 1), lambda t: (0, 0))],
        out_shape=[jax.ShapeDtypeStruct((hh, s), F32), jax.ShapeDtypeStruct((hh, 1), F32)],
        scratch_shapes=[pltpu.VMEM((hh, LANES), F32)],
        compiler_params=_cp(("arbitrary",)),
    )(dcum_q_t, dcum_k_t, f_t, b_c)


def _fox_tile(s):
    return min(512, max(s // 2, 8))


def _tri_tables(nq, kv_major):
    if kv_major:
        pairs = [(i, j) for j in range(nq) for i in range(j, nq)]
    else:
        pairs = [(i, j) for i in range(nq) for j in range(i + 1)]
    return (jnp.asarray([p[0] for p in pairs], jnp.int32), jnp.asarray([p[1] for p in pairs], jnp.int32))


def _lane_tile(col, width):
    return col if width == LANES else jnp.tile(col, (1, width // LANES))


def _flash_fwd(qs, kn, qkvg, ck, *, name, rider=None):
    s = qs.shape[0]
    tt = _fox_tile(s)
    nq = s // tt
    itab, jtab = _tri_tables(nq, kv_major=False)
    v0 = 2 * FOX_D // LANES

    def body(itab_ref, jtab_ref, q_ref, k_ref, v_ref, ck_ref, o_ref, lse_ref, m_sc, l_sc, acc_sc):
        t = pl.program_id(1)
        i, j = itab_ref[t], jtab_ref[t]

        @pl.when(j == 0)
        def _():
            m_sc[...] = jnp.full_like(m_sc, NEG_INF)
            l_sc[...] = jnp.zeros_like(l_sc)
            acc_sc[...] = jnp.zeros_like(acc_sc)

        low = lax.broadcasted_iota(jnp.int32, (tt, LANES), 1) < FOX_HD

        def step(diagonal):
            q2, k2 = q_ref[...], k_ref[...]
            v2 = v_ref[...].astype(BF16)
            alphas, outs = [], []
            for hh in range(2):
                qh = jnp.where(low if hh == 0 else jnp.logical_not(low), q2, jnp.zeros_like(q2))
                sc = lax.dot_general(qh, k2, NT, preferred_element_type=F32) - ck_ref[0][hh:hh + 1, :]
                if diagonal:
                    row = lax.broadcasted_iota(jnp.int32, sc.shape, 0)
                    col = lax.broadcasted_iota(jnp.int32, sc.shape, 1)
                    sc = jnp.where(row >= col, sc, NEG_INF)
                m_prev = m_sc[hh]
                m_new = jnp.maximum(m_prev, jnp.max(sc, axis=1, keepdims=True))
                alpha = jnp.exp(m_prev - m_new)
                p = jnp.exp(sc - _lane_tile(m_new, tt))
                l_sc[hh] = alpha * l_sc[hh] + jnp.sum(p, axis=1, keepdims=True)
                m_sc[hh] = m_new
                alphas.append(alpha)
                outs.append(lax.dot_general(p.astype(BF16), v2, NN, preferred_element_type=F32))
            acc_sc[...] = jnp.where(low, alphas[0], alphas[1]) * acc_sc[...] + jnp.where(low, outs[0], outs[1])

        @pl.when(j < i)
        def _():
            step(False)

        @pl.when(j == i)
        def _():
            step(True)
            o_ref[...] = acc_sc[...] / jnp.where(low, l_sc[0], l_sc[1])
            lse_ref[0] = jnp.concatenate([m_sc[hh][:, 0:1] + jnp.log(l_sc[hh][:, 0:1]) for hh in range(2)], axis=1)

    outs, riding = _call(
        body, name=name, grid=(FOX_PAIRS, int(itab.shape[0])), prefetch=(itab, jtab),
        in_specs=[pl.BlockSpec((tt, LANES), lambda p, t, it, jt: (it[t], p)),
                  pl.BlockSpec((tt, LANES), lambda p, t, it, jt: (jt[t], p)),
                  pl.BlockSpec((tt, LANES), lambda p, t, it, jt: (jt[t], v0 + p)),
                  pl.BlockSpec((1, 2, tt), lambda p, t, it, jt: (p, 0, jt[t]))],
        out_specs=[pl.BlockSpec((tt, LANES), lambda p, t, it, jt: (it[t], p)),
                   pl.BlockSpec((1, tt, 2), lambda p, t, it, jt: (p, it[t], 0))],
        scratch_shapes=[pltpu.VMEM((2, tt, LANES), F32), pltpu.VMEM((2, tt, LANES), F32), pltpu.VMEM((tt, LANES), F32)],
        out_shape=[jax.ShapeDtypeStruct((s, FOX_D), F32), jax.ShapeDtypeStruct((FOX_PAIRS, s, 2), F32)],
        sem=("parallel", "arbitrary"), rider=rider, args=(qs, kn, qkvg, ck))
    return outs + [riding]


def _flash_bwd(qs, kn, qkvg, do, lse_t, delta_t, ck_c, *, name, rider=None):
    s = qs.shape[0]
    tt = _fox_tile(s)
    nq = s // tt
    nl = tt // LANES
    itab, jtab = _tri_tables(nq, kv_major=True)
    nsteps = itab.shape[0]
    v0 = 2 * FOX_D // LANES

    def body(itab_ref, jtab_ref, q_ref, k_ref, v_ref, do_ref, lse_ref, dl_ref, ck_ref,
             dq_ref, dk_ref, dv_ref, dcq_ref, dck_ref, dqt_sc, rs_sc, dk_sc, dv_sc, dc_sc, kt_sc, ckb_sc):
        t = pl.program_id(1)
        i, j = itab_ref[t], jtab_ref[t]

        @pl.when(t == 0)
        def _():
            dqt_sc[...] = jnp.zeros_like(dqt_sc)
            rs_sc[...] = jnp.zeros_like(rs_sc)

        @pl.when(i == j)
        def _():
            dk_sc[...] = jnp.zeros_like(dk_sc)
            dv_sc[...] = jnp.zeros_like(dv_sc)
            dc_sc[...] = jnp.zeros_like(dc_sc)
            kt_sc[...] = k_ref[...].astype(F32).T.astype(BF16)
            for hh in range(2):
                ckb_sc[hh] = jnp.broadcast_to(ck_ref[0][:, hh:hh + 1], (tt, LANES))

        low = lax.broadcasted_iota(jnp.int32, (tt, LANES), 1) < FOX_HD
        top = lax.broadcasted_iota(jnp.int32, (LANES, tt), 0) < FOX_HD

        def step(diagonal):
            q2, k2, kt = q_ref[...], k_ref[...], kt_sc[...]
            v2 = v_ref[...].astype(BF16)
            do2 = do_ref[...].astype(BF16)
            dqs, dks, dvs = [], [], []
            for hh in range(2):
                sel = low if hh == 0 else jnp.logical_not(low)
                qh = jnp.where(sel, q2, jnp.zeros_like(q2))
                doh = jnp.where(sel, do2, jnp.zeros_like(do2))
                st = lax.dot_general(k2, qh, NT, preferred_element_type=F32)
                st = st - _lane_tile(ckb_sc[hh], tt) - lse_ref[0][hh:hh + 1, :]
                if diagonal:
                    key = lax.broadcasted_iota(jnp.int32, st.shape, 0)
                    qry = lax.broadcasted_iota(jnp.int32, st.shape, 1)
                    st = jnp.where(qry >= key, st, NEG_INF)
                pt = jnp.exp(st)
                dpt = lax.dot_general(v2, doh, NT, preferred_element_type=F32)
                dst = pt * (dpt - dl_ref[0][hh:hh + 1, :])
                ptb, dstb = pt.astype(BF16), dst.astype(BF16)
                dvs.append(lax.dot_general(ptb, do2, NN, preferred_element_type=F32))
                dks.append(lax.dot_general(dstb, q2, NN, preferred_element_type=F32))
                dqs.append(lax.dot_general(kt, dstb, NN, preferred_element_type=F32))
                rs_sc[hh, i] += jnp.sum(dst, axis=0, keepdims=True)
                part = dst[:, 0:LANES]
                for b in range(1, nl):
                    part = part + dst[:, b * LANES:(b + 1) * LANES]
                dc_sc[hh] += part
            dv_sc[...] += jnp.where(low, dvs[0], dvs[1])
            dk_sc[...] += jnp.where(low, dks[0], dks[1])
            dqt_sc[i] += jnp.where(top, dqs[0], dqs[1])

        @pl.when(j < i)
        def _():
            step(False)

        @pl.when(j == i)
        def _():
            step(True)

        @pl.when(i == nq - 1)
        def _():
            dk_ref[...] = dk_sc[...]
            dv_ref[...] = dv_sc[...]
            dck_ref[0] = -jnp.concatenate([jnp.sum(dc_sc[hh], axis=1, keepdims=True) for hh in range(2)], axis=1)

        @pl.when(t == nsteps - 1)
        def _():
            for b in range(nq):
                dq_ref[pl.ds(b * tt, tt), :] = dqt_sc[b].T * FOX_SCALE
                dcq_ref[0, :, pl.ds(b * tt, tt)] = jnp.concatenate([rs_sc[hh, b] for hh in range(2)], axis=0)

    qside = pl.BlockSpec((tt, LANES), lambda p, t, it, jt: (it[t], p))
    kside = pl.BlockSpec((tt, LANES), lambda p, t, it, jt: (jt[t], p))
    qstat = pl.BlockSpec((1, 2, tt), lambda p, t, it, jt: (p, 0, it[t]))
    kstat = pl.BlockSpec((1, tt, 2), lambda p, t, it, jt: (p, jt[t], 0))
    outs, riding = _call(
        body, name=name, grid=(FOX_PAIRS, nsteps), prefetch=(itab, jtab),
        in_specs=[qside, kside, pl.BlockSpec((tt, LANES), lambda p, t, it, jt: (jt[t], v0 + p)), qside, qstat, qstat, kstat],
        out_specs=[pl.BlockSpec((s, LANES), lambda p, t, it, jt: (0, p)), kside, kside,
                   pl.BlockSpec((1, 2, s), lambda p, t, it, jt: (p, 0, 0)), kstat],
        scratch_shapes=[pltpu.VMEM((nq, LANES, tt), F32), pltpu.VMEM((2, nq, 1, tt), F32), pltpu.VMEM((tt, LANES), F32),
                        pltpu.VMEM((tt, LANES), F32), pltpu.VMEM((2, tt, LANES), F32), pltpu.VMEM((LANES, tt), BF16),
                        pltpu.VMEM((2, tt, LANES), F32)],
        out_shape=[jax.ShapeDtypeStruct((s, FOX_D), F32), jax.ShapeDtypeStruct((s, FOX_D), F32),
                   jax.ShapeDtypeStruct((s, FOX_D), F32), jax.ShapeDtypeStruct((FOX_PAIRS, 2, s), F32),
                   jax.ShapeDtypeStruct((FOX_PAIRS, s, 2), F32)],
        sem=("parallel", "arbitrary"), rider=rider, args=(qs, kn, qkvg, do, lse_t, delta_t, ck_c))
    return outs + [riding]


def _ogate_fwd(o, qkvg, *, name):
    s = o.shape[0]
    tr = _pick(s, 512, 8)

    def body(o_ref, g_ref, out_ref):
        out_ref[...] = (o_ref[...] * _sigmoid(g_ref[...])).astype(BF16)

    tile = pl.BlockSpec((tr, FOX_D), lambda i: (i, 0))
    return pl.pallas_call(
        body, name=name, grid=(s // tr,), in_specs=[tile, pl.BlockSpec((tr, FOX_D), lambda i: (i, 3))],
        out_specs=tile, out_shape=jax.ShapeDtypeStruct((s, FOX_D), BF16), compiler_params=_cp(("parallel",)),
    )(o, qkvg)


def _ogate_bwd(dog, o, qkvg, *, name):
    s = o.shape[0]
    tr = _pick(s, 512, 8)

    def body(dog_ref, o_ref, g_ref, do_ref, dg_ref, dl_ref):
        sg = _sigmoid(g_ref[...])
        ov = o_ref[...]
        dog_v = dog_ref[...]
        do = dog_v * sg
        do_ref[...] = do
        dg_ref[...] = (dog_v * ov * sg * (1.0 - sg)).astype(BF16)
        dl_ref[...] = _dotf(do * ov, _group_matrix(FOX_D, FOX_HD))

    tile = pl.BlockSpec((tr, FOX_D), lambda i: (i, 0))
    return pl.pallas_call(
        body, name=name, grid=(s // tr,), in_specs=[tile, tile, pl.BlockSpec((tr, FOX_D), lambda i: (i, 3))],
        out_specs=[tile, tile, pl.BlockSpec((tr, FOX_H), lambda i: (i, 0))],
        out_shape=[jax.ShapeDtypeStruct((s, FOX_D), F32), jax.ShapeDtypeStruct((s, FOX_D), BF16),
                   jax.ShapeDtypeStruct((s, FOX_H), F32)],
        compiler_params=_cp(("parallel",)),
    )(dog, o, qkvg)


def _loss_head(h, g, target, *, name):
    s, d = h.shape
    tr = _pick(s, 512, 8)

    def body(h_ref, g_ref, t_ref, loss_ref, dh_ref, dg_ref):
        i = pl.program_id(0)
        x = h_ref[...]
        gv = g_ref[...]
        r = lax.rsqrt(jnp.mean(x * x, axis=-1, keepdims=True) + EPS)
        xh = x * r
        err = xh * gv - t_ref[...]
        part = 0.5 * jnp.sum(jnp.sum(err * err, axis=1, keepdims=True) * (1.0 / d), axis=0, keepdims=True)
        dy = err * (1.0 / d)
        dyg = dy * gv
        dh_ref[...] = r * (dyg - xh * jnp.mean(dyg * xh, axis=-1, keepdims=True))
        dgp = jnp.sum(dy * xh, axis=0, keepdims=True)

        @pl.when(i == 0)
        def _():
            loss_ref[...] = jnp.zeros_like(loss_ref) + part
            dg_ref[...] = dgp

        @pl.when(i > 0)
        def _():
            loss_ref[...] += part
            dg_ref[...] += dgp

    tile = pl.BlockSpec((tr, d), lambda i: (i, 0))
    vec = pl.BlockSpec((1, d), lambda i: (0, 0))
    return pl.pallas_call(
        body, name=name, grid=(s // tr,), in_specs=[tile, vec, tile],
        out_specs=[pl.BlockSpec((1, LANES), lambda i: (0, 0)), tile, vec],
        out_shape=[jax.ShapeDtypeStruct((1, LANES), F32), jax.ShapeDtypeStruct((s, d), F32),
                   jax.ShapeDtypeStruct((1, d), F32)],
        compiler_params=_cp(("arbitrary",)),
    )(h, g, target)


def _adamw(w, g, m, v, *, name):
    rows, cols = w.shape
    tr = _pick(rows, 256, 8)
    c1 = 1.0 - ADAM_B1 ** ADAM_STEP
    c2 = 1.0 - ADAM_B2 ** ADAM_STEP

    def body(w_ref, g_ref, m_ref, v_ref, d_ref, nm_ref, nv_ref):
        gv = g_ref[...]
        nm = ADAM_B1 * m_ref[...] + (1.0 - ADAM_B1) * gv
        nv = ADAM_B2 * v_ref[...] + (1.0 - ADAM_B2) * (gv * gv)
        d_ref[...] = -ADAM_LR * ((nm / c1) / (jnp.sqrt(nv / c2) + ADAM_EPS) + ADAM_WD * w_ref[...])
        nm_ref[...] = nm
        nv_ref[...] = nv

    tile = pl.BlockSpec((tr, cols), lambda i: (i, 0))
    shp = jax.ShapeDtypeStruct((rows, cols), F32)
    return pl.pallas_call(
        body, name=name, grid=(rows // tr,), in_specs=[tile] * 4, out_specs=[tile] * 3, out_shape=[shp] * 3,
        compiler_params=_cp(("parallel",)),
    )(w, g, m, v)


ANY = pl.BlockSpec(memory_space=pl.ANY)
N_DEV = 8


def _coords():
    return lax.axis_index("x"), lax.axis_index("y"), lax.axis_index("c")


def _other_chips(x, y):
    return [(1 - x, y), (x, 1 - y), (1 - x, 1 - y)]


def _allgather_small(buf, *, name, with_sum):
    rows = buf.shape[0]

    def body(*refs):
        if with_sum:
            x_ref, out_ref, sum_ref, send_sems, recv_sems = refs
        else:
            x_ref, out_ref, send_sems, recv_sems = refs
        x, y, c = _coords()
        me = 4 * x + 2 * y + c
        out_ref[me] = x_ref[...]
        copies = []
        for rel in range(1, N_DEV):
            px = (1 - x) if rel & 4 else x
            py = (1 - y) if rel & 2 else y
            pc = (1 - c) if rel & 1 else c
            cp = pltpu.make_async_remote_copy(
                src_ref=x_ref, dst_ref=out_ref.at[me], send_sem=send_sems.at[rel - 1], recv_sem=recv_sems.at[rel - 1],
                device_id=(px, py, pc), device_id_type=MESH)
            cp.start()
            copies.append(cp)
        for cp in copies:
            cp.wait()
        if with_sum:
            acc = out_ref[0]
            for k in range(1, N_DEV):
                acc = acc + out_ref[k]
            sum_ref[...] = acc

    slots = jax.ShapeDtypeStruct((N_DEV, rows, LANES), F32)
    vm = pl.BlockSpec(memory_space=pltpu.VMEM)
    out_shape = [slots, jax.ShapeDtypeStruct((rows, LANES), F32)] if with_sum else [slots]
    return pl.pallas_call(
        body, name=name, in_specs=[vm], out_specs=[vm] * len(out_shape), out_shape=out_shape,
        scratch_shapes=[pltpu.SemaphoreType.DMA((N_DEV - 1,)), pltpu.SemaphoreType.DMA((N_DEV - 1,))],
    )(buf)


class _Gather:
    per_array = 6

    def __init__(self, arrays):
        self.arrays = list(arrays)

    def out_shapes(self):
        return [jax.ShapeDtypeStruct((4,) + a.shape, a.dtype) for a in self.arrays]

    @staticmethod
    def _ici(ins, outs, send_sems, recv_sems, t, j, px, py, c, slot):
        return pltpu.make_async_remote_copy(
            src_ref=ins[t].at[c], dst_ref=outs[t].at[slot, c], send_sem=send_sems.at[6 * t + j],
            recv_sem=recv_sems.at[6 * t + j], device_id=(px, py, c), device_id_type=MESH)

    @staticmethod
    def _d2d(outs, send_sems, recv_sems, t, j, kj, half, sibling):
        return pltpu.make_async_remote_copy(
            src_ref=outs[t].at[kj, half], dst_ref=outs[t].at[kj, half], send_sem=send_sems.at[6 * t + 3 + j],
            recv_sem=recv_sems.at[6 * t + 3 + j], device_id=sibling, device_id_type=MESH)

    def start(self, ins, outs, send_sems, recv_sems):
        x, y, c = _coords()
        for t in range(len(ins)):
            for j, (px, py) in enumerate(_other_chips(x, y)):
                self._ici(ins, outs, send_sems, recv_sems, t, j, px, py, c, 2 * x + y).start()

    def finish(self, ins, outs, send_sems, recv_sems):
        x, y, c = _coords()
        chips = _other_chips(x, y)
        sibling = (x, y, 1 - c)
        started = []
        for t in range(len(ins)):
            for j, (px, py) in enumerate(chips):
                ici = self._ici(ins, outs, send_sems, recv_sems, t, j, px, py, c, 2 * px + py)
                ici.wait_recv()
                fwd = self._d2d(outs, send_sems, recv_sems, t, j, 2 * px + py, c, sibling)
                fwd.start()
                started += [ici, fwd]
        for t in range(len(ins)):
            for j, (px, py) in enumerate(chips):
                self._d2d(outs, send_sems, recv_sems, t, j, 2 * px + py, 1 - c, sibling).wait_recv()
        for cp in started:
            cp.wait_send()


class _Exchange:
    per_array = 7

    def __init__(self, arrays):
        self.arrays = list(arrays)

    def out_shapes(self):
        return [jax.ShapeDtypeStruct((7,) + a.shape[2:], a.dtype) for a in self.arrays]

    @staticmethod
    def _copies(ins, outs, send_sems, recv_sems):
        x, y, c = _coords()
        for t in range(len(ins)):
            for rel in range(1, N_DEV):
                px = (1 - x) if rel & 4 else x
                py = (1 - y) if rel & 2 else y
                pc = (1 - c) if rel & 1 else c
                yield pltpu.make_async_remote_copy(
                    src_ref=ins[t].at[2 * px + py, pc], dst_ref=outs[t].at[rel - 1], send_sem=send_sems.at[7 * t + rel - 1],
                    recv_sem=recv_sems.at[7 * t + rel - 1], device_id=(px, py, pc), device_id_type=MESH)

    def start(self, ins, outs, send_sems, recv_sems):
        for cp in self._copies(ins, outs, send_sems, recv_sems):
            cp.start()

    def finish(self, ins, outs, send_sems, recv_sems):
        for cp in self._copies(ins, outs, send_sems, recv_sems):
            cp.wait()


def _call(body, *, name, grid, in_specs, out_specs, out_shape, scratch_shapes, args, sem, rider=None, prefetch=()):
    n_in, n_out, n_pre = len(in_specs), len(out_specs), len(prefetch)
    n_c = len(rider.arrays) if rider is not None else 0

    def wrapped(*refs):
        pre, rest = refs[:n_pre], refs[n_pre:]
        ins, cins = rest[:n_in], rest[n_in:n_in + n_c]
        outs = rest[n_in + n_c:n_in + n_c + n_out]
        couts = rest[n_in + n_c + n_out:n_in + 2 * n_c + n_out]
        scratch = rest[n_in + 2 * n_c + n_out:]
        if rider is None:
            body(*pre, *ins, *outs, *scratch)
            return
        send_sems, recv_sems = scratch[-2:]
        ids = [pl.program_id(a) for a in range(len(grid))]
        first = functools.reduce(jnp.logical_and, [i == 0 for i in ids])
        last = functools.reduce(jnp.logical_and, [i == g - 1 for i, g in zip(ids, grid)])

        @pl.when(first)
        def _():
            rider.start(cins, couts, send_sems, recv_sems)

        body(*pre, *ins, *outs, *scratch[:-2])

        @pl.when(last)
        def _():
            rider.finish(cins, couts, send_sems, recv_sems)

    if rider is not None:
        nsem = rider.per_array * n_c
        in_specs = list(in_specs) + [ANY] * n_c
        out_specs = list(out_specs) + [ANY] * n_c
        out_shape = list(out_shape) + rider.out_shapes()
        scratch_shapes = list(scratch_shapes) + [pltpu.SemaphoreType.DMA((nsem,)), pltpu.SemaphoreType.DMA((nsem,))]
        args = list(args) + rider.arrays
        sem = ("arbitrary",) * len(grid)
    if n_pre:
        res = pl.pallas_call(
            wrapped, name=name, out_shape=out_shape, compiler_params=_cp(sem),
            grid_spec=pltpu.PrefetchScalarGridSpec(num_scalar_prefetch=n_pre, grid=grid, in_specs=in_specs,
                                                   out_specs=out_specs, scratch_shapes=scratch_shapes),
        )(*prefetch, *args)
    else:
        res = pl.pallas_call(
            wrapped, name=name, grid=grid, in_specs=in_specs, out_specs=out_specs, out_shape=out_shape,
            scratch_shapes=scratch_shapes, compiler_params=_cp(sem),
        )(*args)
    return list(res[:n_out]), list(res[n_out:])


def _run_rider(rider, *, name):
    n = len(rider.arrays)

    def body(*refs):
        ins, outs = refs[:n], refs[n:2 * n]
        send_sems, recv_sems = refs[2 * n:]
        rider.start(ins, outs, send_sems, recv_sems)
        rider.finish(ins, outs, send_sems, recv_sems)

    nsem = rider.per_array * n
    return pl.pallas_call(
        body, name=name, in_specs=[ANY] * n, out_specs=[ANY] * n, out_shape=rider.out_shapes(),
        scratch_shapes=[pltpu.SemaphoreType.DMA((nsem,)), pltpu.SemaphoreType.DMA((nsem,))],
    )(*rider.arrays)


def _sibling_swap(arrs, *, name):
    n = len(arrs)

    def body(*refs):
        ins, outs = refs[:n], refs[n:2 * n]
        send_sems, recv_sems = refs[2 * n:]
        x, y, c = _coords()
        copies = []
        for t in range(n):
            cp = pltpu.make_async_remote_copy(
                src_ref=ins[t], dst_ref=outs[t], send_sem=send_sems.at[t], recv_sem=recv_sems.at[t],
                device_id=(x, y, 1 - c), device_id_type=MESH)
            cp.start()
            copies.append(cp)
        for cp in copies:
            cp.wait()

    return pl.pallas_call(
        body, name=name, in_specs=[ANY] * n, out_specs=[ANY] * n,
        out_shape=[jax.ShapeDtypeStruct(a.shape, a.dtype) for a in arrs],
        scratch_shapes=[pltpu.SemaphoreType.DMA((n,)), pltpu.SemaphoreType.DMA((n,))],
    )(*arrs)


def _add_selected(stack, others, sel, *, name):
    _, m, cols = stack.shape
    q = others.shape[0]
    tr = _pick(m, 256, 16)

    def body(sel_ref, s_ref, o_ref, out_ref):
        acc = s_ref[0].astype(F32)
        for i in range(q):
            acc = acc + o_ref[i].astype(F32)
        out_ref[...] = acc

    return pl.pallas_call(
        body, name=name,
        grid_spec=pltpu.PrefetchScalarGridSpec(
            num_scalar_prefetch=1, grid=(m // tr,),
            in_specs=[pl.BlockSpec((1, tr, cols), lambda i, sel_ref: (sel_ref[0], i, 0)),
                      pl.BlockSpec((q, tr, cols), lambda i, sel_ref: (0, i, 0))],
            out_specs=pl.BlockSpec((tr, cols), lambda i, sel_ref: (i, 0))),
        out_shape=jax.ShapeDtypeStruct((m, cols), F32),
        compiler_params=_cp(("parallel",)),
    )(sel, stack, others)


BIG = ("ssd_w_in", "ssd_w_out", "fox_w_in", "fox_w_out", "ffn_w_up", "ffn_w_down")
COL_SHARDED = ("ssd_w_in", "fox_w_in", "ffn_w_up")
SMALL = (("mix_norm_g", (4, 1024)), ("ffn_norm_g", (4, 1024)), ("ssd_conv_w", (2, 4, 3072)), ("ssd_conv_b", (2, 3072)),
         ("ssd_dt_bias", (2, 32)), ("ssd_a_log", (2, 32)), ("ssd_d", (2, 32)), ("ssd_norm_g", (2, 2048)),
         ("fox_b_f", (2, 16)), ("fox_q_norm_g", (2, 64)), ("fox_k_norm_g", (2, 64)), ("ffn_conv_w", (4, 3, 2816)),
         ("ffn_conv_b", (4, 2816)), ("final_norm_g", (1024,)), ("loss", (1,)))
NAMES = ("mix_norm_g", "ffn_norm_g", "ssd_w_in", "ssd_conv_w", "ssd_conv_b", "ssd_dt_bias", "ssd_a_log", "ssd_d",
         "ssd_norm_g", "ssd_w_out", "fox_w_in", "fox_b_f", "fox_q_norm_g", "fox_k_norm_g", "fox_w_out", "ffn_w_up",
         "ffn_conv_w", "ffn_conv_b", "ffn_w_down", "final_norm_g")


def _pack(parts):
    flat = jnp.concatenate([jnp.reshape(p, (-1,)).astype(F32) for p in parts])
    rows = -(-flat.shape[0] // (8 * LANES)) * 8
    return jnp.pad(flat, (0, rows * LANES - flat.shape[0])).reshape(rows, LANES)


def _unpack(buf, shapes):
    flat = buf.reshape(-1)
    out, off = [], 0
    for shp in shapes:
        size = 1
        for d in shp:
            size *= d
        out.append(flat[off:off + size].reshape(shp))
        off += size
    return out


def _pad_lanes(a):
    return jnp.pad(a, ((0, 0), (0, LANES - a.shape[1])))


def _pad8(w):
    return jnp.pad(w, ((0, 8 - w.shape[0]), (0, 0)))


def _ssd_forward(h, p, name, rider=None):
    s = h.shape[0]
    hn = _rms_fwd(h, p["mix_g"], gw=D_MODEL, ncol=1, name=f"{name}_norm")
    zx = _matmul(hn, p["w_zx"], mode="nn", name=f"{name}_proj")
    dtp = _matmul(hn, p["w_dt"], mode="nn", name=f"{name}_proj_dt")
    xbc, _ = _conv_fwd(zx, p["conv_w8"], p["conv_b"], kw=SSD_K, width=SSD_CONV_DIM, u_col0=SSD_DI, name=f"{name}_conv")
    dt3 = dtp[:, :SSD_H].reshape(s, SSD_G, SSD_HPG)
    dtg, dtg_t = jnp.transpose(dt3, (1, 0, 2)), jnp.transpose(dt3, (1, 2, 0))
    sp = (p["bias_r"], p["bias_c"], p["alog_r"], p["alog_c"], p["d_r"])
    y, hprev, riding = _ssd_fwd(xbc, dtg, dtg_t, *sp, name=f"{name}_scan", rider=rider)
    y2 = _rms_fwd(y, p["norm_g"], gw=SSD_DI // SSD_G, ncol=SSD_G, z=zx, name=f"{name}_gnorm")
    out = _matmul(y2, p["w_out"], mode="nn", add=h, name=f"{name}_out")
    return out, dict(h=h, hn=hn, zx=zx, xbc=xbc, dtg=dtg, dtg_t=dtg_t, y=y, hprev=hprev, y2=y2), riding


def _ssd_backward(dh1, p, a, name, ride=()):
    s = dh1.shape[0]
    g = {}
    dy2 = _matmul(dh1, p["w_out"], mode="nt", name=f"{name}_out_dx")
    g["w_out"] = _matmul(a["y2"], dh1, mode="tn", out_dtype=BF16, name=f"{name}_out_dw")
    rider = _Exchange(list(ride) + [_to_slabs(g["w_out"], False)])
    dy, dz, g["norm_g"] = _rms_bwd(a["y"], p["norm_g"], dy2, gw=SSD_DI // SSD_G, ncol=SSD_G, z=a["zx"], name=f"{name}_gnorm_b")
    sp = (p["bias_r"], p["bias_c"], p["alog_r"], p["alog_c"], p["d_r"])
    dx, dbm, dcm, ddt, g["dt_bias"], g["a_log"], g["d"], riding = _ssd_bwd(
        a["xbc"], a["dtg"], a["dtg_t"], *sp, a["hprev"], dy, name=f"{name}_scan_b", rider=rider)
    dxbc, dwb, _ = _conv_bwd(a["zx"], p["conv_w8"], p["conv_b"], [dx, dbm, dcm], kw=SSD_K, width=SSD_CONV_DIM,
                             u_col0=SSD_DI, name=f"{name}_conv_b")
    g["conv_w"], g["conv_b"] = dwb[:SSD_K], dwb[7]
    dzx = jnp.concatenate([dz.astype(BF16), dxbc], axis=1)
    ddtp = _pad_lanes(jnp.transpose(ddt, (1, 0, 2)).reshape(s, SSD_H))
    dhn = _matmul(dzx, p["w_zx"], mode="nt", name=f"{name}_proj_dx")
    dhn = _matmul(ddtp, p["w_dt"], mode="nt", add=dhn, name=f"{name}_proj_dt_dx")
    dw_zx = _matmul(a["hn"], dzx, mode="tn", out_dtype=BF16, name=f"{name}_proj_dw")
    dw_dt = _matmul(a["hn"], ddtp, mode="tn", out_dtype=BF16, name=f"{name}_proj_dt_dw")
    g["w_in"] = jnp.concatenate([dw_zx, dw_dt[:, :SSD_H]], axis=1)
    dh, g["mix_g"] = _rms_bwd(a["h"], p["mix_g"], dhn, gw=D_MODEL, ncol=1, add=dh1, name=f"{name}_norm_b")
    g["w_out_received"] = riding[-1]
    return dh, g, riding[:-1]


def _fox_forward(h, p, name, rider=None):
    s = h.shape[0]
    hn = _rms_fwd(h, p["mix_g"], gw=D_MODEL, ncol=1, name=f"{name}_norm")
    qkvg = _matmul(hn, p["w_qkvg"], mode="nn", name=f"{name}_proj")
    fp = _matmul(hn, p["w_f"], mode="nn", name=f"{name}_proj_f")
    qs = _rms_fwd(qkvg, p["gq"] * FOX_SCALE, gw=FOX_D, ncol=1, x_col0=0, sub=FOX_HD, name=f"{name}_qnorm")
    kn = _rms_fwd(qkvg, p["gk"], gw=FOX_D, ncol=1, x_col0=1, sub=FOX_HD, name=f"{name}_knorm")
    f_t = jnp.transpose(fp[:, :FOX_H])
    cum_t = _fgate_fwd(f_t, p["b_f"], name=f"{name}_fgate")
    ck = cum_t.reshape(FOX_PAIRS, 2, s)
    o, lse, riding = _flash_fwd(qs, kn, qkvg, ck, name=f"{name}_attn", rider=rider)
    og = _ogate_fwd(o, qkvg, name=f"{name}_ogate")
    out = _matmul(og, p["w_out"], mode="nn", add=h, name=f"{name}_out")
    return out, dict(h=h, hn=hn, qkvg=qkvg, qs=qs, kn=kn, f_t=f_t, ck=ck, o=o, lse=lse, og=og), riding


def _fox_backward(dh1, p, a, name, ride=()):
    s = dh1.shape[0]
    g = {}
    dog = _matmul(dh1, p["w_out"], mode="nt", name=f"{name}_out_dx")
    g["w_out"] = _matmul(a["og"], dh1, mode="tn", out_dtype=BF16, name=f"{name}_out_dw")
    rider = _Exchange(list(ride) + [_to_slabs(g["w_out"], False)])
    do, dgate, delta = _ogate_bwd(dog, a["o"], a["qkvg"], name=f"{name}_ogate_b")
    swap = lambda v: jnp.transpose(v, (0, 2, 1))
    dl_t = jnp.transpose(delta.reshape(s, FOX_PAIRS, 2), (1, 2, 0))
    dq, dk, dv, dcq, dck, riding = _flash_bwd(a["qs"], a["kn"], a["qkvg"], do, swap(a["lse"]), dl_t, swap(a["ck"]),
                                              name=f"{name}_attn_b", rider=rider)
    dq_raw, dgq = _rms_bwd(a["qkvg"], p["gq"], dq, gw=FOX_D, ncol=1, x_col0=0, sub=FOX_HD, dx_dtype=BF16, name=f"{name}_qnorm_b")
    dk_raw, dgk = _rms_bwd(a["qkvg"], p["gk"], dk, gw=FOX_D, ncol=1, x_col0=1, sub=FOX_HD, dx_dtype=BF16, name=f"{name}_knorm_b")
    g["gq"] = dgq.reshape(FOX_H, FOX_HD).sum(axis=0)
    g["gk"] = dgk.reshape(FOX_H, FOX_HD).sum(axis=0)
    df_t, dbf = _fgate_bwd(dcq.reshape(FOX_H, s), swap(dck).reshape(FOX_H, s), a["f_t"], p["b_f"], name=f"{name}_fgate_b")
    g["b_f"] = dbf[:, 0]
    dproj = jnp.concatenate([dq_raw, dk_raw, dv.astype(BF16), dgate], axis=1)
    dfp = _pad_lanes(jnp.transpose(df_t))
    dhn = _matmul(dproj, p["w_qkvg"], mode="nt", name=f"{name}_proj_dx")
    dhn = _matmul(dfp, p["w_f"], mode="nt", add=dhn, name=f"{name}_proj_f_dx")
    dw_qkvg = _matmul(a["hn"], dproj, mode="tn", out_dtype=BF16, name=f"{name}_proj_dw")
    dw_f = _matmul(a["hn"], dfp, mode="tn", out_dtype=BF16, name=f"{name}_proj_f_dw")
    g["w_in"] = jnp.concatenate([dw_qkvg, dw_f[:, :FOX_H]], axis=1)
    dh, g["mix_g"] = _rms_bwd(a["h"], p["mix_g"], dhn, gw=D_MODEL, ncol=1, add=dh1, name=f"{name}_norm_b")
    g["w_out_received"] = riding[-1]
    return dh, g, riding[:-1]


def _ffn_forward(h, p, name, rider=None):
    hn = _rms_fwd(h, p["ffn_g"], gw=D_MODEL, ncol=1, name=f"{name}_norm")
    u = _matmul(hn, p["w_up"], mode="nn", name=f"{name}_up")
    act, riding = _conv_fwd(u, p["conv_w8"], p["conv_b"], kw=FFN_K, width=D_FF, u_col0=0, mul_col0=D_FF, out_dtype=BF16,
                            name=f"{name}_glu", rider=rider)
    out = _matmul(act, p["w_down"], mode="nn", add=h, name=f"{name}_down")
    return out, dict(h=h, hn=hn, u=u, act=act), riding


def _ffn_backward(dh2, p, a, name, ride=()):
    g = {}
    dact = _matmul(dh2, p["w_down"], mode="nt", name=f"{name}_down_dx")
    g["w_down"] = _matmul(a["act"], dh2, mode="tn", out_dtype=BF16, name=f"{name}_down_dw")
    du, dwb, riding = _conv_bwd(a["u"], p["conv_w8"], p["conv_b"], dact, kw=FFN_K, width=D_FF, u_col0=0, mul_col0=D_FF,
                                name=f"{name}_glu_b", rider=_Exchange(list(ride)) if ride else None)
    g["conv_w"], g["conv_b"] = dwb[:FFN_K], dwb[7]
    dhn = _matmul(du, p["w_up"], mode="nt", name=f"{name}_up_dx")
    g["w_up"] = _matmul(a["hn"], du, mode="tn", out_dtype=BF16, name=f"{name}_up_dw")
    dh, g["ffn_g"] = _rms_bwd(a["h"], p["ffn_g"], dhn, gw=D_MODEL, ncol=1, add=dh2, name=f"{name}_norm_b")
    return dh, g, riding


def _to_slabs(dw, col_sharded):
    rows, cols = dw.shape
    if col_sharded:
        return jnp.transpose(dw.reshape(rows, 4, cols // 4), (1, 0, 2)).reshape(4, 2, rows // 2, cols // 4)
    return dw.reshape(4, 2, rows // 8, cols)


def kernel(x, mix_norm_g, ffn_norm_g, ssd_w_in, ssd_conv_w, ssd_conv_b, ssd_dt_bias, ssd_a_log, ssd_d, ssd_norm_g, ssd_w_out, fox_w_in, fox_b_f, fox_q_norm_g, fox_k_norm_g, fox_w_out, ffn_w_up, ffn_conv_w, ffn_conv_b, ffn_w_down, final_norm_g, loss_target, m_mix_norm_g, m_ffn_norm_g, m_ssd_w_in, m_ssd_conv_w, m_ssd_conv_b, m_ssd_dt_bias, m_ssd_a_log, m_ssd_d, m_ssd_norm_g, m_ssd_w_out, m_fox_w_in, m_fox_b_f, m_fox_q_norm_g, m_fox_k_norm_g, m_fox_w_out, m_ffn_w_up, m_ffn_conv_w, m_ffn_conv_b, m_ffn_w_down, m_final_norm_g, v_mix_norm_g, v_ffn_norm_g, v_ssd_w_in, v_ssd_conv_w, v_ssd_conv_b, v_ssd_dt_bias, v_ssd_a_log, v_ssd_d, v_ssd_norm_g, v_ssd_w_out, v_fox_w_in, v_fox_b_f, v_fox_q_norm_g, v_fox_k_norm_g, v_fox_w_out, v_ffn_w_up, v_ffn_conv_w, v_ffn_conv_b, v_ffn_w_down, v_final_norm_g):
    w = dict(mix_norm_g=mix_norm_g, ffn_norm_g=ffn_norm_g, ssd_w_in=ssd_w_in, ssd_conv_w=ssd_conv_w, ssd_conv_b=ssd_conv_b,
             ssd_dt_bias=ssd_dt_bias, ssd_a_log=ssd_a_log, ssd_d=ssd_d, ssd_norm_g=ssd_norm_g, ssd_w_out=ssd_w_out,
             fox_w_in=fox_w_in, fox_b_f=fox_b_f, fox_q_norm_g=fox_q_norm_g, fox_k_norm_g=fox_k_norm_g, fox_w_out=fox_w_out,
             ffn_w_up=ffn_w_up, ffn_conv_w=ffn_conv_w, ffn_conv_b=ffn_conv_b, ffn_w_down=ffn_w_down, final_norm_g=final_norm_g)
    m_in = dict(zip(NAMES, (m_mix_norm_g, m_ffn_norm_g, m_ssd_w_in, m_ssd_conv_w, m_ssd_conv_b, m_ssd_dt_bias, m_ssd_a_log,
                            m_ssd_d, m_ssd_norm_g, m_ssd_w_out, m_fox_w_in, m_fox_b_f, m_fox_q_norm_g, m_fox_k_norm_g,
                            m_fox_w_out, m_ffn_w_up, m_ffn_conv_w, m_ffn_conv_b, m_ffn_w_down, m_final_norm_g)))
    v_in = dict(zip(NAMES, (v_mix_norm_g, v_ffn_norm_g, v_ssd_w_in, v_ssd_conv_w, v_ssd_conv_b, v_ssd_dt_bias, v_ssd_a_log,
                            v_ssd_d, v_ssd_norm_g, v_ssd_w_out, v_fox_w_in, v_fox_b_f, v_fox_q_norm_g, v_fox_k_norm_g,
                            v_fox_w_out, v_ffn_w_up, v_ffn_conv_w, v_ffn_conv_b, v_ffn_w_down, v_final_norm_g)))
    cx, cy, cc = _coords()
    chip = 2 * cx + cy
    h = x[0]
    target = loss_target[0]

    conv_shapes = [ssd_conv_w.shape, ffn_conv_w.shape]
    slots = _allgather_small(_pack([ssd_conv_w, ffn_conv_w]), name="gather_conv_w", with_sum=False)[0]
    per_chip = [_unpack(slots[2 * q], conv_shapes) for q in range(4)]
    ssd_conv_full = jnp.concatenate([pc[0] for pc in per_chip], axis=2)
    ffn_conv_full = jnp.concatenate([pc[1] for pc in per_chip], axis=2)
    low = {n: w[n].astype(BF16) for n in BIG}
    sub_weights = dict(ssd=("ssd_w_in", "ssd_w_out"), fox=("fox_w_in", "fox_w_out"), ffn=("ffn_w_up", "ffn_w_down"))

    def shards_of(kind, idx):
        return [low[n][idx].reshape(2, low[n].shape[1] // 2, low[n].shape[2]) for n in sub_weights[kind]]

    def assemble(kind, idx, gathered):
        full = []
        for n, own, gth in zip(sub_weights[kind], shards_of(kind, idx), gathered):
            gth = lax.dynamic_update_slice(gth, own[None], (chip, 0, 0, 0))
            _, _, half, cols = gth.shape
            if n in COL_SHARDED:
                full.append(jnp.transpose(gth.reshape(4, 2 * half, cols), (1, 0, 2)).reshape(2 * half, 4 * cols))
            else:
                full.append(gth.reshape(8 * half, cols))
        return full

    def ssd_params(j, i, weights):
        w_in, w_out = weights
        g3 = lambda v: v.reshape(SSD_G, 1, SSD_HPG)
        g3c = lambda v: v.reshape(SSD_G, SSD_HPG, 1)
        return dict(mix_g=mix_norm_g[i][None], w_zx=w_in[:, :SSD_ZX], w_dt=_pad_lanes(w_in[:, SSD_ZX:]),
                    conv_w8=_pad8(ssd_conv_full[j]), conv_b=ssd_conv_b[j][None], bias_r=g3(ssd_dt_bias[j]),
                    bias_c=g3c(ssd_dt_bias[j]), alog_r=g3(ssd_a_log[j]), alog_c=g3c(ssd_a_log[j]), d_r=g3(ssd_d[j]),
                    norm_g=ssd_norm_g[j][None], w_out=w_out)

    def fox_params(j, i, weights):
        w_in, w_out = weights
        return dict(mix_g=mix_norm_g[i][None], w_qkvg=w_in[:, :4 * FOX_D], w_f=_pad_lanes(w_in[:, 4 * FOX_D:]),
                    gq=jnp.tile(fox_q_norm_g[j], FOX_H)[None], gk=jnp.tile(fox_k_norm_g[j], FOX_H)[None],
                    b_f=fox_b_f[j][:, None], w_out=w_out)

    def ffn_params(i, weights):
        w_up, w_down = weights
        return dict(ffn_g=ffn_norm_g[i][None], w_up=w_up, conv_w8=_pad8(ffn_conv_full[i]), conv_b=ffn_conv_b[i][None],
                    w_down=w_down)

    order = [("ssd", 0), ("ffn", 0), ("fox", 0), ("ffn", 1), ("ssd", 1), ("ffn", 2), ("fox", 1), ("ffn", 3)]
    fetch = {("ssd", 0): [("ffn", 0)], ("ffn", 0): [("fox", 0)], ("fox", 0): [("ffn", 1), ("ssd", 1), ("ffn", 2)],
             ("ssd", 1): [("fox", 1)], ("fox", 1): [("ffn", 3)]}
    ready = {("ssd", 0): assemble("ssd", 0, _run_rider(_Gather(shards_of("ssd", 0)), name="gather_first"))}
    params, acts = {}, {}
    forward = dict(ssd=_ssd_forward, fox=_fox_forward, ffn=_ffn_forward)
    for kind, idx in order:
        if kind == "ssd":
            params[kind, idx] = ssd_params(idx, 2 * idx, ready.pop((kind, idx)))
        elif kind == "fox":
            params[kind, idx] = fox_params(idx, 2 * idx + 1, ready.pop((kind, idx)))
        else:
            params[kind, idx] = ffn_params(idx, ready.pop((kind, idx)))
        wanted = fetch.get((kind, idx), [])
        rider = _Gather([s for sub in wanted for s in shards_of(*sub)]) if wanted else None
        h, acts[kind, idx], riding = forward[kind](h, params[kind, idx], f"{kind}{idx}", rider=rider)
        for q, sub in enumerate(wanted):
            ready[sub] = assemble(*sub, riding[2 * q:2 * q + 2])
    loss_part, dh, d_final_g = _loss_head(h, final_norm_g[None], target, name="loss_head")

    backward = dict(ssd=_ssd_backward, fox=_fox_backward, ffn=_ffn_backward)
    grad_keys = dict(ssd=("w_in", "w_out"), fox=("w_in", "w_out"), ffn=("w_up", "w_down"))
    sub_g, slabs, received = {}, {}, {}
    waiting = []
    for sub in reversed(order):
        kind = sub[0]
        dh, sub_g[sub], got = backward[kind](dh, params[sub], acts[sub], f"{kind}{sub[1]}",
                                             ride=[slabs[key] for key in waiting])
        received.update(zip(waiting, got))
        waiting = []
        for q, (key, n) in enumerate(zip(grad_keys[kind], sub_weights[kind])):
            slabs[sub, q] = _to_slabs(sub_g[sub][key], n in COL_SHARDED)
            if kind != "ffn" and q == 1:
                received[sub, q] = sub_g[sub]["w_out_received"]
            else:
                waiting.append((sub, q))
    received.update(zip(waiting, _run_rider(_Exchange([slabs[key] for key in waiting]), name="rs_last_exchange")))
    grad_x = dh[None]
    ssd_g, fox_g = [sub_g["ssd", 0], sub_g["ssd", 1]], [sub_g["fox", 0], sub_g["fox", 1]]
    mix_g = [ssd_g[0], fox_g[0], ssd_g[1], fox_g[1]]
    ffn_g = [sub_g["ffn", i] for i in range(DEPTH)]

    me = jnp.reshape(2 * chip + cc, (1,)).astype(jnp.int32)
    finals = {}
    for sub in order:
        for q in range(2):
            _, _, m, cols = slabs[sub, q].shape
            finals[sub, q] = _add_selected(slabs[sub, q].reshape(8, m, cols), received[sub, q], me,
                                           name=f"rs_add_{sub[0]}{sub[1]}_{q}")
    keys = list(finals)
    others = dict(zip(keys, _sibling_swap([finals[key] for key in keys], name="rs_result_swap")))
    grads = {}
    for kind, names in sub_weights.items():
        for q, n in enumerate(names):
            subs = [sub for sub in sorted(set(order)) if sub[0] == kind]
            mine = jnp.stack([finals[sub, q] for sub in subs])
            theirs = jnp.stack([others[sub, q] for sub in subs])
            halves = jnp.stack([jnp.where(cc == 0, mine, theirs), jnp.where(cc == 0, theirs, mine)], axis=1)
            grads[n] = halves.reshape(w[n].shape)
    small = dict(
        mix_norm_g=jnp.concatenate([g["mix_g"] for g in mix_g], axis=0),
        ffn_norm_g=jnp.concatenate([g["ffn_g"] for g in ffn_g], axis=0),
        ssd_conv_w=jnp.stack([g["conv_w"] for g in ssd_g]), ssd_conv_b=jnp.stack([g["conv_b"] for g in ssd_g]),
        ssd_dt_bias=jnp.stack([g["dt_bias"].reshape(SSD_H) for g in ssd_g]),
        ssd_a_log=jnp.stack([g["a_log"].reshape(SSD_H) for g in ssd_g]),
        ssd_d=jnp.stack([g["d"].reshape(SSD_H) for g in ssd_g]),
        ssd_norm_g=jnp.concatenate([g["norm_g"] for g in ssd_g], axis=0),
        fox_b_f=jnp.stack([g["b_f"] for g in fox_g]), fox_q_norm_g=jnp.stack([g["gq"] for g in fox_g]),
        fox_k_norm_g=jnp.stack([g["gk"] for g in fox_g]),
        ffn_conv_w=jnp.stack([g["conv_w"] for g in ffn_g]), ffn_conv_b=jnp.stack([g["conv_b"] for g in ffn_g]),
        final_norm_g=d_final_g[0], loss=loss_part[0, :1])
    _, total = _allgather_small(_pack([small[n] for n, _ in SMALL]), name="reduce_small", with_sum=True)
    for (n, shp), val in zip(SMALL, _unpack(total, [shp for _, shp in SMALL])):
        grads[n] = val
    loss = grads.pop("loss")[0]
    grads["ssd_conv_w"] = lax.dynamic_slice_in_dim(grads["ssd_conv_w"], chip * ssd_conv_w.shape[2], ssd_conv_w.shape[2], axis=2)
    grads["ffn_conv_w"] = lax.dynamic_slice_in_dim(grads["ffn_conv_w"], chip * ffn_conv_w.shape[2], ffn_conv_w.shape[2], axis=2)

    deltas, new_m, new_v = {}, {}, {}
    for n in NAMES:
        shp = w[n].shape
        two_d = (1, shp[0]) if len(shp) == 1 else (-1, shp[-1])
        r2 = lambda a: a.reshape(two_d)
        d, nm, nv = _adamw(r2(w[n]), r2(grads[n]), r2(m_in[n]), r2(v_in[n]), name=f"adamw_{n}")
        deltas[n], new_m[n], new_v[n] = d.reshape(shp), nm.reshape(shp), nv.reshape(shp)
    return (loss, grad_x, *[grads[n] for n in NAMES], *[deltas[n] for n in NAMES], *[new_m[n] for n in NAMES],
            *[new_v[n] for n in NAMES])
```

```python
import functools

import jax
import jax.numpy as jnp
from jax import lax
from jax.experimental import pallas as pl
from jax.experimental.pallas import tpu as pltpu

F32 = jnp.float32
BF16 = jnp.bfloat16
HI = lax.Precision.HIGHEST
MESH = pl.DeviceIdType.MESH

D_MODEL = 1024
DEPTH = 4
EPS = 1e-6
SSD_DI = 2048
SSD_HD = 64
SSD_G = 4
SSD_HPG = 8
SSD_N = 128
SSD_K = 4
CHUNK = 128
SSD_CONV_DIM = 3072
SSD_ZX = SSD_DI + SSD_CONV_DIM
SSD_H = 32
FOX_HD = 64
FOX_H = 16
FOX_D = 1024
D_FF = 2816
FFN_K = 3
LANES = 128
VMEM_LIMIT = 56 * 1024 * 1024

ADAM_LR = 0.001
ADAM_B1 = 0.9
ADAM_B2 = 0.999
ADAM_EPS = 1e-08
ADAM_WD = 0.01
ADAM_STEP = 10

NN = (((1,), (0,)), ((), ()))
NT = (((1,), (1,)), ((), ()))
TN = (((0,), (0,)), ((), ()))


def _pick(n, cap, mult=LANES):
    best = None
    for t in range(mult, min(n, cap) + 1, mult):
        if n % t == 0:
            best = t
    return best if best is not None else n


def _cp(sem):
    return pltpu.CompilerParams(dimension_semantics=sem, vmem_limit_bytes=VMEM_LIMIT)


def _sigmoid(x):
    return jax.nn.sigmoid(x)


def _silu(x):
    return x * _sigmoid(x)


def _dsilu(x):
    s = _sigmoid(x)
    return s * (1.0 + x * (1.0 - s))


def _softplus(x):
    e = jnp.exp(-jnp.abs(x))
    u = 1.0 + e
    l1p = jnp.where(u == 1.0, e, jnp.log(u) * (e / (u - 1.0)))
    return jnp.maximum(x, 0.0) + l1p


def _dotf(a, b, dn=NN, *, onehot="b", pieces=2):
    x, e = (a, b) if onehot == "b" else (b, a)
    e = e.astype(BF16)
    acc = None
    for n in range(pieces):
        hi = x.astype(BF16)
        part = lax.dot_general(hi, e, dn, preferred_element_type=F32) if onehot == "b" else \
            lax.dot_general(e, hi, dn, preferred_element_type=F32)
        acc = part if acc is None else acc + part
        if n + 1 < pieces:
            x = x - hi.astype(F32)
    return acc


def _dotb(a, b, dn=NN):
    return lax.dot_general(a.astype(BF16), b.astype(BF16), dn, preferred_element_type=F32)


def _group_matrix(width, sub, transpose=False):
    ng = width // sub
    shape = (ng, width) if transpose else (width, ng)
    lane = lax.broadcasted_iota(jnp.int32, shape, 1 if transpose else 0)
    grp = lax.broadcasted_iota(jnp.int32, shape, 0 if transpose else 1)
    return (lane // sub == grp).astype(F32)


def _gmean(v, sub):
    width = v.shape[-1]
    if sub == width:
        return jnp.mean(v, axis=-1, keepdims=True)
    s = _dotf(v, _group_matrix(width, sub))
    return _dotf(s, _group_matrix(width, sub, transpose=True)) * (1.0 / sub)


def _matmul(a, b, *, mode, name, out_dtype=F32, add=None):
    a_planes = a.shape[0] if (mode == "nt" and a.ndim == 3) else 0
    b_planes = b.shape[0] if (mode == "tn" and b.ndim == 3) else 0
    a2 = (a.shape[1], a.shape[0] * a.shape[2]) if a_planes else a.shape
    b2 = (b.shape[1], b.shape[0] * b.shape[2]) if b_planes else b.shape
    if mode == "nn":
        (m, k), (k2, n) = a2, b2
    elif mode == "nt":
        (m, k), (n, k2) = a2, b2
    else:
        (k, m), (k2, n) = a2, b2
    assert k == k2, (a.shape, b.shape, mode)
    tm, tn = _pick(m, 1536), _pick(n // b_planes if b_planes else n, 1536)
    tk = _pick(k // a_planes if a_planes else k, 1536)
    nk = k // tk
    dn = {"nn": NN, "nt": NT, "tn": TN}[mode]
    has_add = add is not None

    def body(*refs):
        if has_add:
            a_ref, b_ref, add_ref, o_ref, acc_ref = refs
        else:
            a_ref, b_ref, o_ref, acc_ref = refs
            add_ref = None
        kk = pl.program_id(2)
        part = _dotb(a_ref[0] if a_planes else a_ref[...], b_ref[0] if b_planes else b_ref[...], dn)

        def finish(r):
            if has_add:
                r = r + add_ref[...]
            o_ref[...] = r.astype(out_dtype)

        if nk == 1:
            finish(part)
        else:
            @pl.when(kk == 0)
            def _():
                acc_ref[...] = part

            @pl.when(kk > 0)
            def _():
                acc_ref[...] += part

            @pl.when(kk == nk - 1)
            def _():
                finish(acc_ref[...])

    if mode == "nn":
        a_spec = pl.BlockSpec((tm, tk), lambda i, j, q: (i, q))
        b_spec = pl.BlockSpec((tk, tn), lambda i, j, q: (q, j))
    elif mode == "nt":
        per = (k // a_planes) // tk if a_planes else 0
        a_spec = (pl.BlockSpec((1, tm, tk), lambda i, j, q: (q // per, i, q % per)) if a_planes
                  else pl.BlockSpec((tm, tk), lambda i, j, q: (i, q)))
        b_spec = pl.BlockSpec((tn, tk), lambda i, j, q: (j, q))
    else:
        per = (n // b_planes) // tn if b_planes else 0
        a_spec = pl.BlockSpec((tk, tm), lambda i, j, q: (q, i))
        b_spec = (pl.BlockSpec((1, tk, tn), lambda i, j, q: (j // per, q, j % per)) if b_planes
                  else pl.BlockSpec((tk, tn), lambda i, j, q: (q, j)))
    o_spec = pl.BlockSpec((tm, tn), lambda i, j, q: (i, j))
    in_specs = [a_spec, b_spec] + ([o_spec] if has_add else [])
    args = (a, b) + ((add,) if has_add else ())
    return pl.pallas_call(
        body, name=name, grid=(m // tm, n // tn, nk), in_specs=in_specs, out_specs=o_spec,
        out_shape=jax.ShapeDtypeStruct((m, n), out_dtype),
        scratch_shapes=[pltpu.VMEM((tm, tn) if nk > 1 else (8, LANES), F32)],
        compiler_params=_cp(("parallel", "parallel", "arbitrary")),
    )(*args)


def _rms_fwd(x, g, *, gw, ncol, name, x_col0=0, sub=None, z=None, z_col0=0, out_dtype=BF16):
    rows = x.shape[0]
    tr = _pick(rows, 512, 8)
    sub = gw if sub is None else sub
    gated = z is not None

    def body(*refs):
        if gated:
            x_ref, z_ref, g_ref, o_ref = refs
            xv = x_ref[...] * _silu(z_ref[...])
        else:
            x_ref, g_ref, o_ref = refs
            xv = x_ref[...]
        r = lax.rsqrt(_gmean(xv * xv, sub) + EPS)
        o_ref[...] = (xv * r * g_ref[...]).astype(out_dtype)

    specs = [pl.BlockSpec((tr, gw), lambda j, i: (i, x_col0 + j))]
    args = [x]
    if gated:
        specs.append(pl.BlockSpec((tr, gw), lambda j, i: (i, z_col0 + j)))
        args.append(z)
    specs.append(pl.BlockSpec((1, gw), lambda j, i: (0, j)))
    args.append(g)
    return pl.pallas_call(
        body, name=name, grid=(ncol, rows // tr), in_specs=specs,
        out_specs=pl.BlockSpec((tr, gw), lambda j, i: (i, j)),
        out_shape=jax.ShapeDtypeStruct((rows, gw * ncol), out_dtype),
        compiler_params=_cp(("parallel", "parallel")),
    )(*args)


def _rms_bwd(x, g, dy, *, gw, ncol, name, x_col0=0, sub=None, z=None, z_col0=0, add=None, dx_dtype=F32):
    rows = x.shape[0]
    tr = _pick(rows, 512, 8)
    sub = gw if sub is None else sub
    gated = z is not None
    has_add = add is not None

    def body(*refs):
        refs = list(refs)
        x_ref = refs.pop(0)
        z_ref = refs.pop(0) if gated else None
        g_ref = refs.pop(0)
        dy_ref = refs.pop(0)
        add_ref = refs.pop(0) if has_add else None
        dx_ref = refs.pop(0)
        dz_ref = refs.pop(0) if gated else None
        dg_ref = refs.pop(0)
        i = pl.program_id(1)
        xv = x_ref[...]
        if gated:
            zz = z_ref[...]
            yz = xv * _silu(zz)
        else:
            yz = xv
        r = lax.rsqrt(_gmean(yz * yz, sub) + EPS)
        xh = yz * r
        dy = dy_ref[...].astype(F32)
        dyg = dy * g_ref[...]
        d_yz = r * (dyg - xh * _gmean(dyg * xh, sub))
        if gated:
            dx_ref[...] = (d_yz * _silu(zz)).astype(dx_dtype)
            dz_ref[...] = (d_yz * xv * _dsilu(zz)).astype(dx_dtype)
        elif has_add:
            dx_ref[...] = (d_yz + add_ref[...]).astype(dx_dtype)
        else:
            dx_ref[...] = d_yz.astype(dx_dtype)
        part = jnp.sum(dy * xh, axis=0, keepdims=True)

        @pl.when(i == 0)
        def _():
            dg_ref[...] = part

        @pl.when(i > 0)
        def _():
            dg_ref[...] += part

    tile = pl.BlockSpec((tr, gw), lambda j, i: (i, j))
    specs = [pl.BlockSpec((tr, gw), lambda j, i: (i, x_col0 + j))]
    args = [x]
    if gated:
        specs.append(pl.BlockSpec((tr, gw), lambda j, i: (i, z_col0 + j)))
        args.append(z)
    specs += [pl.BlockSpec((1, gw), lambda j, i: (0, j)), tile]
    args += [g, dy]
    if has_add:
        specs.append(tile)
        args.append(add)
    width = gw * ncol
    out_shape = [jax.ShapeDtypeStruct((rows, width), dx_dtype)]
    out_specs = [tile]
    if gated:
        out_shape.append(jax.ShapeDtypeStruct((rows, width), dx_dtype))
        out_specs.append(tile)
    out_shape.append(jax.ShapeDtypeStruct((1, width), F32))
    out_specs.append(pl.BlockSpec((1, gw), lambda j, i: (0, j)))
    return pl.pallas_call(
        body, name=name, grid=(ncol, rows // tr), in_specs=specs, out_specs=out_specs, out_shape=out_shape,
        compiler_params=_cp(("parallel", "arbitrary")),
    )(*args)


HALO = 8


def _conv_rows(tc):
    return 16 * 8 * LANES // tc


def _conv_fwd(u, w8, b, *, kw, width, name, u_col0=0, mul_col0=None, out_dtype=F32, rider=None):
    rows = u.shape[0]
    ts = _pick(rows, 2048, 8)
    tc = _pick(width, 512)
    gated = mul_col0 is not None
    c0 = u_col0 // tc
    m0 = (mul_col0 // tc) if gated else 0
    assert u_col0 % tc == 0 and (not gated or mul_col0 % tc == 0)

    def body(*refs):
        if gated:
            cur_ref, halo_ref, mul_ref, w_ref, b_ref, o_ref, ext = refs
        else:
            cur_ref, halo_ref, w_ref, b_ref, o_ref, ext = refs
        i = pl.program_id(0)
        ext[pl.ds(0, HALO), :] = jnp.where(i == 0, 0.0, halo_ref[...])
        ext[pl.ds(HALO, ts), :] = cur_ref[...]
        bias = b_ref[...]
        taps = [w_ref[k:k + 1, :] for k in range(kw)]
        rb = _conv_rows(tc)
        for r0 in range(0, ts, rb):
            pre = bias + taps[0] * ext[pl.ds(r0 + HALO - (kw - 1), rb), :]
            for k in range(1, kw):
                pre = pre + taps[k] * ext[pl.ds(r0 + HALO - (kw - 1) + k, rb), :]
            act = _silu(pre)
            if gated:
                act = act * mul_ref[pl.ds(r0, rb), :]
            o_ref[pl.ds(r0, rb), :] = act.astype(out_dtype)

    hb = ts // HALO
    specs = [pl.BlockSpec((ts, tc), lambda i, j: (i, c0 + j)),
             pl.BlockSpec((HALO, tc), lambda i, j: (jnp.maximum(i * hb - 1, 0), c0 + j))]
    args = [u, u]
    if gated:
        specs.append(pl.BlockSpec((ts, tc), lambda i, j: (i, m0 + j)))
        args.append(u)
    specs += [pl.BlockSpec((8, tc), lambda i, j: (0, j)), pl.BlockSpec((1, tc), lambda i, j: (0, j))]
    args += [w8, b]
    outs, riding = _call(
        body, name=name, grid=(rows // ts, width // tc), in_specs=specs,
        out_specs=[pl.BlockSpec((ts, tc), lambda i, j: (i, j))],
        out_shape=[jax.ShapeDtypeStruct((rows, width), out_dtype)],
        scratch_shapes=[pltpu.VMEM((ts + HALO, tc), F32)], sem=("parallel", "parallel"), rider=rider, args=args)
    return outs + [riding]


def _conv_bwd(u, w8, b, dact, *, kw, width, name, u_col0=0, mul_col0=None, du_dtype=BF16, rider=None):
    rows = u.shape[0]
    ts = _pick(rows, 2048, 8)
    tc = _pick(width, 512)
    pieces = list(dact) if isinstance(dact, (list, tuple)) else [dact]
    firsts, seen = [], 0
    for piece in pieces:
        assert piece.shape[1] % tc == 0, (piece.shape, tc)
        firsts.append(seen // tc)
        seen += piece.shape[1]
    assert seen == width
    gated = mul_col0 is not None
    c0 = u_col0 // tc
    m0 = (mul_col0 // tc) if gated else 0
    nt = rows // ts
    hb = ts // HALO

    def body(*refs):
        refs = list(refs)
        cur_ref, halo_ref = refs.pop(0), refs.pop(0)
        mul_ref = refs.pop(0) if gated else None
        w_ref, b_ref = refs.pop(0), refs.pop(0)
        da_refs = [refs.pop(0) for _ in pieces]
        col_tile = pl.program_id(0)
        du_ref = refs.pop(0)
        dwb_ref, ext_u, ext_d = refs
        t = pl.program_id(1)
        ti = nt - 1 - t
        ext_u[pl.ds(0, HALO), :] = jnp.where(ti == 0, 0.0, halo_ref[...])
        ext_u[pl.ds(HALO, ts), :] = cur_ref[...]

        @pl.when(t == 0)
        def _():
            ext_d[pl.ds(ts, HALO), :] = jnp.zeros((HALO, tc), F32)
            dwb_ref[...] = jnp.zeros((8, tc), F32)

        bias = b_ref[...]
        taps = [w_ref[k:k + 1, :] for k in range(kw)]
        rb = _conv_rows(tc)
        dw_acc = [jnp.zeros((1, tc), F32) for _ in range(kw)]
        db_acc = jnp.zeros((1, tc), F32)
        for r0 in reversed(range(0, ts, rb)):
            shifted = [ext_u[pl.ds(r0 + HALO - (kw - 1) + k, rb), :] for k in range(kw)]
            pre = bias + taps[0] * shifted[0]
            for k in range(1, kw):
                pre = pre + taps[k] * shifted[k]
            sg = _sigmoid(pre)
            dsilu = sg * (1.0 + pre * (1.0 - sg))
            da = da_refs[0][pl.ds(r0, rb), :].astype(F32)
            for first, ref in zip(firsts[1:], da_refs[1:]):
                da = jnp.where(col_tile >= first, ref[pl.ds(r0, rb), :].astype(F32), da)
            if gated:
                du_ref[1, pl.ds(r0, rb), :] = (da * (pre * sg)).astype(du_dtype)
                dgp = da * mul_ref[pl.ds(r0, rb), :] * dsilu
            else:
                dgp = da * dsilu
            ext_d[pl.ds(r0, rb), :] = dgp
            du = taps[kw - 1] * dgp
            for k in range(kw - 1):
                du = du + taps[k] * ext_d[pl.ds(r0 + kw - 1 - k, rb), :]
            if gated:
                du_ref[0, pl.ds(r0, rb), :] = du.astype(du_dtype)
            else:
                du_ref[pl.ds(r0, rb), :] = du.astype(du_dtype)
            for k in range(kw):
                dw_acc[k] = dw_acc[k] + jnp.sum(dgp * shifted[k], axis=0, keepdims=True)
            db_acc = db_acc + jnp.sum(dgp, axis=0, keepdims=True)
        for k in range(kw):
            dwb_ref[k:k + 1, :] += dw_acc[k]
        dwb_ref[7:8, :] += db_acc
        ext_d[pl.ds(ts, HALO), :] = ext_d[pl.ds(0, HALO), :]

    specs = [pl.BlockSpec((ts, tc), lambda j, t: (nt - 1 - t, c0 + j)),
             pl.BlockSpec((HALO, tc), lambda j, t: (jnp.maximum((nt - 1 - t) * hb - 1, 0), c0 + j))]
    args = [u, u]
    if gated:
        specs.append(pl.BlockSpec((ts, tc), lambda j, t: (nt - 1 - t, m0 + j)))
        args.append(u)
    tile = pl.BlockSpec((ts, tc), lambda j, t: (nt - 1 - t, j))
    specs += [pl.BlockSpec((8, tc), lambda j, t: (0, j)), pl.BlockSpec((1, tc), lambda j, t: (0, j))]
    args += [w8, b]
    for first, piece in zip(firsts, pieces):
        count = piece.shape[1] // tc

        def piece_map(j, t, first=first, count=count):
            mine = jnp.logical_and(j >= first, j < first + count)
            return (jnp.where(mine, nt - 1 - t, 0), jnp.clip(j - first, 0, count - 1))

        specs.append(pl.BlockSpec((ts, tc), piece_map))
        args.append(piece)
    if gated:
        out_shape = [jax.ShapeDtypeStruct((2, rows, width), du_dtype)]
        out_specs = [pl.BlockSpec((2, ts, tc), lambda j, t: (0, nt - 1 - t, j))]
    else:
        out_shape = [jax.ShapeDtypeStruct((rows, width), du_dtype)]
        out_specs = [tile]
    out_shape.append(jax.ShapeDtypeStruct((8, width), F32))
    out_specs.append(pl.BlockSpec((8, tc), lambda j, t: (0, j)))
    outs, riding = _call(
        body, name=name, grid=(width // tc, nt), in_specs=specs, out_specs=out_specs, out_shape=out_shape,
        scratch_shapes=[pltpu.VMEM((ts + HALO, tc), F32), pltpu.VMEM((ts + HALO, tc), F32)],
        sem=("parallel", "arbitrary"), rider=rider, args=args)
    return outs + [riding]


GW = SSD_HPG * SSD_HD


def _ssd_common(x, bm, cm, dt_raw, dt_raw_t, bias_r, bias_c, alog_r, alog_c):
    row = lax.broadcasted_iota(jnp.int32, (CHUNK, CHUNK), 0)
    col = lax.broadcasted_iota(jnp.int32, (CHUNK, CHUNK), 1)
    causal = row >= col
    tril = causal.astype(F32)
    triu = (row <= col).astype(F32)
    spread = _group_matrix(GW, SSD_HD, transpose=True)
    dt = _softplus(dt_raw + bias_r)
    dt_t = _softplus(dt_raw_t + bias_c)
    a_r = -jnp.exp(alog_r)
    a_c = -jnp.exp(alog_c)
    acs = _dotf(tril, dt * a_r, onehot="a", pieces=3)
    acs_t = _dotf(dt_t * a_c, triu, pieces=3)
    last = acs[CHUNK - 1:CHUNK, :]
    ds = jnp.exp(last - acs)
    cd = jnp.exp(last)
    c = dict(causal=causal, tril=tril, triu=triu, spread=spread, dt=dt, a_r=a_r, acs=acs, acs_t=acs_t, ds=ds, cd=cd)
    c["eb"] = _dotf(jnp.exp(acs), spread)
    c["dsb"] = _dotf(ds, spread)
    c["cdb"] = _dotf(cd, spread)
    c["dtb"] = _dotf(dt, spread)
    c["xdt"] = x * c["dtb"]
    c["cb"] = _dotb(cm, bm, NT)
    return c


def _ssd_lam(c, r):
    diff = c["acs"][:, r:r + 1] - c["acs_t"][r:r + 1, :]
    return jnp.exp(jnp.where(c["causal"], diff, -jnp.inf))


GP = 2


def _ssd_specs(nc, rev):
    def ci(t):
        return (nc - 1 - t) if rev else t
    xs = pl.BlockSpec((CHUNK, GP * GW), lambda g, t: (ci(t), g))
    bs = pl.BlockSpec((CHUNK, GP * SSD_N), lambda g, t: (ci(t), SSD_DI // (GP * SSD_N) + g))
    cs = pl.BlockSpec((CHUNK, GP * SSD_N), lambda g, t: (ci(t), (SSD_DI // SSD_N + SSD_G) // GP + g))
    dts = pl.BlockSpec((GP, CHUNK, 8), lambda g, t: (g, ci(t), 0))
    dtts = pl.BlockSpec((GP, 8, CHUNK), lambda g, t: (g, 0, ci(t)))
    pr = pl.BlockSpec((GP, 1, 8), lambda g, t: (g, 0, 0))
    pc = pl.BlockSpec((GP, 8, 1), lambda g, t: (g, 0, 0))
    hs = pl.BlockSpec((1, GP, SSD_N, GW), lambda g, t: (ci(t), g, 0, 0))
    return xs, bs, cs, dts, dtts, pr, pc, hs


def _ssd_fwd(xbc, dtg, dtg_t, bias_r, bias_c, alog_r, alog_c, d_r, *, name, rider=None):
    s = xbc.shape[0]
    nc = s // CHUNK
    xs, bs, cs, dts, dtts, pr, pc, hs = _ssd_specs(nc, False)

    def body(x_ref, b_ref, c_ref, dt_ref, dtt_ref, br_ref, bc_ref, ar_ref, ac_ref, d_ref, y_ref, hp_ref, h_sc):
        t = pl.program_id(1)

        @pl.when(t == 0)
        def _():
            h_sc[...] = jnp.zeros_like(h_sc)

        for gg in range(GP):
            wide, narrow = slice(gg * GW, (gg + 1) * GW), slice(gg * SSD_N, (gg + 1) * SSD_N)
            x, bm, cm = x_ref[:, wide], b_ref[:, narrow], c_ref[:, narrow]
            c = _ssd_common(x, bm, cm, dt_ref[gg], dtt_ref[gg], br_ref[gg], bc_ref[gg], ar_ref[gg], ac_ref[gg])
            h = h_sc[gg]
            hp_ref[0, gg] = h
            xdt = c["xdt"]
            pieces = []
            for r in range(SSD_HPG):
                m = c["cb"] * _ssd_lam(c, r)
                pieces.append(_dotb(m, xdt[:, r * SSD_HD:(r + 1) * SSD_HD]))
            y = jnp.concatenate(pieces, axis=1) + c["eb"] * _dotb(cm, h) + x * _dotf(d_ref[gg], c["spread"])
            y_ref[:, wide] = y
            h_sc[gg] = h * c["cdb"] + _dotb(bm, xdt * c["dsb"], TN)

    outs, riding = _call(
        body, name=name, grid=(SSD_G // GP, nc),
        in_specs=[xs, bs, cs, dts, dtts, pr, pc, pr, pc, pr],
        out_specs=[xs, hs],
        out_shape=[jax.ShapeDtypeStruct((s, SSD_DI), F32), jax.ShapeDtypeStruct((nc, SSD_G, SSD_N, GW), F32)],
        scratch_shapes=[pltpu.VMEM((GP, SSD_N, GW), F32)], sem=("parallel", "arbitrary"), rider=rider,
        args=(xbc, xbc, xbc, dtg, dtg_t, bias_r, bias_c, alog_r, alog_c, d_r))
    return outs + [riding]


def _ssd_bwd(xbc, dtg, dtg_t, bias_r, bias_c, alog_r, alog_c, d_r, hprev, dy, *, name, rider=None):
    s = xbc.shape[0]
    nc = s // CHUNK
    xs, bs, cs, dts, dtts, pr, pc, hs = _ssd_specs(nc, True)
    gsum = functools.partial(_group_matrix, GW, SSD_HD)

    def body(x_ref, b_ref, c_ref, dt_ref, dtt_ref, br_ref, bc_ref, ar_ref, ac_ref, d_ref, hp_ref, dy_ref,
             dx_ref, db_ref, dc_ref, ddt_ref, dbias_ref, dalog_ref, dd_ref, dh_sc):
        t = pl.program_id(1)

        @pl.when(t == 0)
        def _():
            dh_sc[...] = jnp.zeros_like(dh_sc)
            dbias_ref[...] = jnp.zeros_like(dbias_ref)
            dalog_ref[...] = jnp.zeros_like(dalog_ref)
            dd_ref[...] = jnp.zeros_like(dd_ref)

        for gg in range(GP):
            wide, narrow = slice(gg * GW, (gg + 1) * GW), slice(gg * SSD_N, (gg + 1) * SSD_N)
            x, bm, cm = x_ref[:, wide], b_ref[:, narrow], c_ref[:, narrow]
            c = _ssd_common(x, bm, cm, dt_ref[gg], dtt_ref[gg], br_ref[gg], bc_ref[gg], ar_ref[gg], ac_ref[gg])
            lanesum = gsum()
            h = hp_ref[0, gg]
            dh = dh_sc[gg]
            dy = dy_ref[:, wide]
            xdt, dsb = c["xdt"], c["dsb"]
            skip = _dotf(d_ref[gg], c["spread"])
            dd_ref[gg] += jnp.sum(_dotf(dy * x, lanesum), axis=0, keepdims=True)
            dacs = _dotf(dy * (c["eb"] * _dotb(cm, h)), lanesum)
            edy = c["eb"] * dy
            dcm = _dotb(edy, h, NT)
            dh_prev = _dotb(cm, edy, TN)
            bdh = _dotb(bm, dh)
            dxdt = dsb * bdh
            dbm = _dotb(dsb * xdt, dh, NT)
            t1 = _dotf(xdt * bdh, lanesum) * c["ds"]
            dacs = dacs - t1
            dlast = (jnp.sum(t1, axis=0, keepdims=True)
                     + jnp.sum(_dotf(dh * h, lanesum), axis=0, keepdims=True) * c["cd"])
            dcb = jnp.zeros((CHUNK, CHUNK), F32)
            pieces = []
            ones8 = jnp.ones((CHUNK, 8), F32)
            head = lax.broadcasted_iota(jnp.int32, (1, 8), 1)
            for r in range(SSD_HPG):
                sl = slice(r * SSD_HD, (r + 1) * SSD_HD)
                lam = _ssd_lam(c, r)
                m = c["cb"] * lam
                dm = _dotb(dy[:, sl], xdt[:, sl], NT)
                dcb = dcb + dm * lam
                gm = dm * m
                dacs = dacs + ((jnp.sum(gm, axis=1, keepdims=True) - _dotf(gm, ones8, TN, pieces=3))
                               * (head == r).astype(F32))
                pieces.append(_dotb(m, dy[:, sl], TN))
            dxdt = dxdt + jnp.concatenate(pieces, axis=1)
            dcm = dcm + _dotb(dcb, bm)
            dbm = dbm + _dotb(dcb, cm, TN)
            dx_ref[:, wide] = dy * skip + dxdt * c["dtb"]
            db_ref[:, narrow] = dbm
            dc_ref[:, narrow] = dcm
            rowid = lax.broadcasted_iota(jnp.int32, (CHUNK, 8), 0)
            dacs = dacs + jnp.where(rowid == CHUNK - 1, dlast, 0.0)
            dda = _dotf(c["triu"], dacs, onehot="a", pieces=3)
            ddt = _dotf(dxdt * x, lanesum) + dda * c["a_r"]
            ddt_raw = ddt * _sigmoid(dt_ref[gg] + br_ref[gg])
            ddt_ref[gg] = ddt_raw
            dbias_ref[gg] += jnp.sum(ddt_raw, axis=0, keepdims=True)
            dalog_ref[gg] += jnp.sum(dda * c["dt"], axis=0, keepdims=True) * c["a_r"]
            dh_sc[gg] = dh_prev + dh * c["cdb"]

    ci = lambda t: nc - 1 - t
    nspec = pl.BlockSpec((CHUNK, GP * SSD_N), lambda g, t: (ci(t), g))
    outs, riding = _call(
        body, name=name, grid=(SSD_G // GP, nc),
        in_specs=[xs, bs, cs, dts, dtts, pr, pc, pr, pc, pr, hs, xs],
        out_specs=[xs, nspec, nspec, dts, pr, pr, pr],
        out_shape=[jax.ShapeDtypeStruct((s, SSD_DI), F32), jax.ShapeDtypeStruct((s, SSD_G * SSD_N), F32),
                   jax.ShapeDtypeStruct((s, SSD_G * SSD_N), F32), jax.ShapeDtypeStruct((SSD_G, s, 8), F32),
                   jax.ShapeDtypeStruct((SSD_G, 1, 8), F32), jax.ShapeDtypeStruct((SSD_G, 1, 8), F32),
                   jax.ShapeDtypeStruct((SSD_G, 1, 8), F32)],
        scratch_shapes=[pltpu.VMEM((GP, SSD_N, GW), F32)], sem=("parallel", "arbitrary"), rider=rider,
        args=(xbc, xbc, xbc, dtg, dtg_t, bias_r, bias_c, alog_r, alog_c, d_r, hprev, dy))
    return outs + [riding]


FOX_PAIRS = FOX_H // 2
FOX_SCALE = FOX_HD ** -0.5
NEG_INF = -jnp.inf


def _fgate_fwd(f_t, b_c, *, name):
    hh, s = f_t.shape
    tb = _pick(s, 512)
    nb = s // tb

    def body(f_ref, b_ref, o_ref, carry):
        t = pl.program_id(0)

        @pl.when(t == 0)
        def _():
            carry[...] = jnp.zeros_like(carry)

        lf = -_softplus(-(f_ref[...] + b_ref[...]))
        row = lax.broadcasted_iota(jnp.int32, (tb, tb), 0)
        col = lax.broadcasted_iota(jnp.int32, (tb, tb), 1)
        cum = _dotf(lf, (row <= col).astype(F32), pieces=3) + carry[:, 0:1]
        o_ref[...] = cum
        carry[:, 0:1] = cum[:, tb - 1:tb]

    return pl.pallas_call(
        body, name=name, grid=(nb,),
        in_specs=[pl.BlockSpec((hh, tb), lambda t: (0, t)), pl.BlockSpec((hh, 1), lambda t: (0, 0))],
        out_specs=pl.BlockSpec((hh, tb), lambda t: (0, t)),
        out_shape=jax.ShapeDtypeStruct((hh, s), F32),
        scratch_shapes=[pltpu.VMEM((hh, LANES), F32)],
        compiler_params=_cp(("arbitrary",)),
    )(f_t, b_c)


def _fgate_bwd(dcum_q_t, dcum_k_t, f_t, b_c, *, name):
    hh, s = f_t.shape
    tb = _pick(s, 512)
    nb = s // tb

    def body(dq_ref, d_ref, f_ref, b_ref, df_ref, db_ref, carry):
        t = pl.program_id(0)

        @pl.when(t == 0)
        def _():
            carry[...] = jnp.zeros_like(carry)
            db_ref[...] = jnp.zeros_like(db_ref)

        d = d_ref[...] + dq_ref[...]
        row = lax.broadcasted_iota(jnp.int32, (tb, tb), 0)
        col = lax.broadcasted_iota(jnp.int32, (tb, tb), 1)
        rev = _dotf(d, (row >= col).astype(F32), pieces=3) + carry[:, 0:1]
        df = rev * _sigmoid(-(f_ref[...] + b_ref[...]))
        df_ref[...] = df
        db_ref[...] += jnp.sum(df, axis=1, keepdims=True)
        carry[:, 0:1] = rev[:, 0:1]

    blk = pl.BlockSpec((hh, tb), lambda t: (0, nb - 1 - t))
    return pl.pallas_call(
        body, name=name, grid=(nb,),
        in_specs=[blk, blk, blk, pl.BlockSpec((hh, 1), lambda t: (0, 0))],
        out_specs=[blk, pl.BlockSpec((hh, 1), lambda t: (0, 0))],
        out_shape=[jax.ShapeDtypeStruct((hh, s), F32), jax.ShapeDtypeStruct((hh, 1), F32)],
        scratch_shapes=[pltpu.VMEM((hh, LANES), F32)],
        compiler_params=_cp(("arbitrary",)),
    )(dcum_q_t, dcum_k_t, f_t, b_c)


def _fox_tile(s):
    return min(512, max(s // 2, 8))


def _tri_tables(nq, kv_major):
    if kv_major:
        pairs = [(i, j) for j in range(nq) for i in range(j, nq)]
    else:
        pairs = [(i, j) for i in range(nq) for j in range(i + 1)]
    return (jnp.asarray([p[0] for p in pairs], jnp.int32), jnp.asarray([p[1] for p in pairs], jnp.int32))


def _lane_tile(col, width):
    return col if width == LANES else jnp.tile(col, (1, width // LANES))


def _flash_fwd(qs, kn, qkvg, ck, *, name, rider=None):
    s = qs.shape[0]
    tt = _fox_tile(s)
    nq = s // tt
    itab, jtab = _tri_tables(nq, kv_major=False)
    v0 = 2 * FOX_D // LANES

    def body(itab_ref, jtab_ref, q_ref, k_ref, v_ref, ck_ref, o_ref, lse_ref, m_sc, l_sc, acc_sc):
        t = pl.program_id(1)
        i, j = itab_ref[t], jtab_ref[t]

        @pl.when(j == 0)
        def _():
            m_sc[...] = jnp.full_like(m_sc, NEG_INF)
            l_sc[...] = jnp.zeros_like(l_sc)
            acc_sc[...] = jnp.zeros_like(acc_sc)

        low = lax.broadcasted_iota(jnp.int32, (tt, LANES), 1) < FOX_HD

        def step(diagonal):
            q2, k2 = q_ref[...], k_ref[...]
            v2 = v_ref[...].astype(BF16)
            alphas, outs = [], []
            for hh in range(2):
                qh = jnp.where(low if hh == 0 else jnp.logical_not(low), q2, jnp.zeros_like(q2))
                sc = lax.dot_general(qh, k2, NT, preferred_element_type=F32) - ck_ref[0][hh:hh + 1, :]
                if diagonal:
                    row = lax.broadcasted_iota(jnp.int32, sc.shape, 0)
                    col = lax.broadcasted_iota(jnp.int32, sc.shape, 1)
                    sc = jnp.where(row >= col, sc, NEG_INF)
                m_prev = m_sc[hh]
                m_new = jnp.maximum(m_prev, jnp.max(sc, axis=1, keepdims=True))
                alpha = jnp.exp(m_prev - m_new)
                p = jnp.exp(sc - _lane_tile(m_new, tt))
                l_sc[hh] = alpha * l_sc[hh] + jnp.sum(p, axis=1, keepdims=True)
                m_sc[hh] = m_new
                alphas.append(alpha)
                outs.append(lax.dot_general(p.astype(BF16), v2, NN, preferred_element_type=F32))
            acc_sc[...] = jnp.where(low, alphas[0], alphas[1]) * acc_sc[...] + jnp.where(low, outs[0], outs[1])

        @pl.when(j < i)
        def _():
            step(False)

        @pl.when(j == i)
        def _():
            step(True)
            o_ref[...] = acc_sc[...] / jnp.where(low, l_sc[0], l_sc[1])
            lse_ref[0] = jnp.concatenate([m_sc[hh][:, 0:1] + jnp.log(l_sc[hh][:, 0:1]) for hh in range(2)], axis=1)

    outs, riding = _call(
        body, name=name, grid=(FOX_PAIRS, int(itab.shape[0])), prefetch=(itab, jtab),
        in_specs=[pl.BlockSpec((tt, LANES), lambda p, t, it, jt: (it[t], p)),
                  pl.BlockSpec((tt, LANES), lambda p, t, it, jt: (jt[t], p)),
                  pl.BlockSpec((tt, LANES), lambda p, t, it, jt: (jt[t], v0 + p)),
                  pl.BlockSpec((1, 2, tt), lambda p, t, it, jt: (p, 0, jt[t]))],
        out_specs=[pl.BlockSpec((tt, LANES), lambda p, t, it, jt: (it[t], p)),
                   pl.BlockSpec((1, tt, 2), lambda p, t, it, jt: (p, it[t], 0))],
        scratch_shapes=[pltpu.VMEM((2, tt, LANES), F32), pltpu.VMEM((2, tt, LANES), F32), pltpu.VMEM((tt, LANES), F32)],
        out_shape=[jax.ShapeDtypeStruct((s, FOX_D), F32), jax.ShapeDtypeStruct((FOX_PAIRS, s, 2), F32)],
        sem=("parallel", "arbitrary"), rider=rider, args=(qs, kn, qkvg, ck))
    return outs + [riding]


def _flash_bwd(qs, kn, qkvg, do, lse_t, delta_t, ck_c, *, name, rider=None):
    s = qs.shape[0]
    tt = _fox_tile(s)
    nq = s // tt
    nl = tt // LANES
    itab, jtab = _tri_tables(nq, kv_major=True)
    nsteps = itab.shape[0]
    v0 = 2 * FOX_D // LANES

    def body(itab_ref, jtab_ref, q_ref, k_ref, v_ref, do_ref, lse_ref, dl_ref, ck_ref,
             dq_ref, dk_ref, dv_ref, dcq_ref, dck_ref, dqt_sc, rs_sc, dk_sc, dv_sc, dc_sc, kt_sc, ckb_sc):
        t = pl.program_id(1)
        i, j = itab_ref[t], jtab_ref[t]

        @pl.when(t == 0)
        def _():
            dqt_sc[...] = jnp.zeros_like(dqt_sc)
            rs_sc[...] = jnp.zeros_like(rs_sc)

        @pl.when(i == j)
        def _():
            dk_sc[...] = jnp.zeros_like(dk_sc)
            dv_sc[...] = jnp.zeros_like(dv_sc)
            dc_sc[...] = jnp.zeros_like(dc_sc)
            kt_sc[...] = k_ref[...].astype(F32).T.astype(BF16)
            for hh in range(2):
                ckb_sc[hh] = jnp.broadcast_to(ck_ref[0][:, hh:hh + 1], (tt, LANES))

        low = lax.broadcasted_iota(jnp.int32, (tt, LANES), 1) < FOX_HD
        top = lax.broadcasted_iota(jnp.int32, (LANES, tt), 0) < FOX_HD

        def step(diagonal):
            q2, k2, kt = q_ref[...], k_ref[...], kt_sc[...]
            v2 = v_ref[...].astype(BF16)
            do2 = do_ref[...].astype(BF16)
            dqs, dks, dvs = [], [], []
            for hh in range(2):
                sel = low if hh == 0 else jnp.logical_not(low)
                qh = jnp.where(sel, q2, jnp.zeros_like(q2))
                doh = jnp.where(sel, do2, jnp.zeros_like(do2))
                st = lax.dot_general(k2, qh, NT, preferred_element_type=F32)
                st = st - _lane_tile(ckb_sc[hh], tt) - lse_ref[0][hh:hh + 1, :]
                if diagonal:
                    key = lax.broadcasted_iota(jnp.int32, st.shape, 0)
                    qry = lax.broadcasted_iota(jnp.int32, st.shape, 1)
                    st = jnp.where(qry >= key, st, NEG_INF)
                pt = jnp.exp(st)
                dpt = lax.dot_general(v2, doh, NT, preferred_element_type=F32)
                dst = pt * (dpt - dl_ref[0][hh:hh + 1, :])
                ptb, dstb = pt.astype(BF16), dst.astype(BF16)
                dvs.append(lax.dot_general(ptb, do2, NN, preferred_element_type=F32))
                dks.append(lax.dot_general(dstb, q2, NN, preferred_element_type=F32))
                dqs.append(lax.dot_general(kt, dstb, NN, preferred_element_type=F32))
                rs_sc[hh, i] += jnp.sum(dst, axis=0, keepdims=True)
                part = dst[:, 0:LANES]
                for b in range(1, nl):
                    part = part + dst[:, b * LANES:(b + 1) * LANES]
                dc_sc[hh] += part
            dv_sc[...] += jnp.where(low, dvs[0], dvs[1])
            dk_sc[...] += jnp.where(low, dks[0], dks[1])
            dqt_sc[i] += jnp.where(top, dqs[0], dqs[1])

        @pl.when(j < i)
        def _():
            step(False)

        @pl.when(j == i)
        def _():
            step(True)

        @pl.when(i == nq - 1)
        def _():
            dk_ref[...] = dk_sc[...]
            dv_ref[...] = dv_sc[...]
            dck_ref[0] = -jnp.concatenate([jnp.sum(dc_sc[hh], axis=1, keepdims=True) for hh in range(2)], axis=1)

        @pl.when(t == nsteps - 1)
        def _():
            for b in range(nq):
                dq_ref[pl.ds(b * tt, tt), :] = dqt_sc[b].T * FOX_SCALE
                dcq_ref[0, :, pl.ds(b * tt, tt)] = jnp.concatenate([rs_sc[hh, b] for hh in range(2)], axis=0)

    qside = pl.BlockSpec((tt, LANES), lambda p, t, it, jt: (it[t], p))
    kside = pl.BlockSpec((tt, LANES), lambda p, t, it, jt: (jt[t], p))
    qstat = pl.BlockSpec((1, 2, tt), lambda p, t, it, jt: (p, 0, it[t]))
    kstat = pl.BlockSpec((1, tt, 2), lambda p, t, it, jt: (p, jt[t], 0))
    outs, riding = _call(
        body, name=name, grid=(FOX_PAIRS, nsteps), prefetch=(itab, jtab),
        in_specs=[qside, kside, pl.BlockSpec((tt, LANES), lambda p, t, it, jt: (jt[t], v0 + p)), qside, qstat, qstat, kstat],
        out_specs=[pl.BlockSpec((s, LANES), lambda p, t, it, jt: (0, p)), kside, kside,
                   pl.BlockSpec((1, 2, s), lambda p, t, it, jt: (p, 0, 0)), kstat],
        scratch_shapes=[pltpu.VMEM((nq, LANES, tt), F32), pltpu.VMEM((2, nq, 1, tt), F32), pltpu.VMEM((tt, LANES), F32),
                        pltpu.VMEM((tt, LANES), F32), pltpu.VMEM((2, tt, LANES), F32), pltpu.VMEM((LANES, tt), BF16),
                        pltpu.VMEM((2, tt, LANES), F32)],
        out_shape=[jax.ShapeDtypeStruct((s, FOX_D), F32), jax.ShapeDtypeStruct((s, FOX_D), F32),
                   jax.ShapeDtypeStruct((s, FOX_D), F32), jax.ShapeDtypeStruct((FOX_PAIRS, 2, s), F32),
                   jax.ShapeDtypeStruct((FOX_PAIRS, s, 2), F32)],
        sem=("parallel", "arbitrary"), rider=rider, args=(qs, kn, qkvg, do, lse_t, delta_t, ck_c))
    return outs + [riding]


def _ogate_fwd(o, qkvg, *, name):
    s = o.shape[0]
    tr = _pick(s, 512, 8)

    def body(o_ref, g_ref, out_ref):
        out_ref[...] = (o_ref[...] * _sigmoid(g_ref[...])).astype(BF16)

    tile = pl.BlockSpec((tr, FOX_D), lambda i: (i, 0))
    return pl.pallas_call(
        body, name=name, grid=(s // tr,), in_specs=[tile, pl.BlockSpec((tr, FOX_D), lambda i: (i, 3))],
        out_specs=tile, out_shape=jax.ShapeDtypeStruct((s, FOX_D), BF16), compiler_params=_cp(("parallel",)),
    )(o, qkvg)


def _ogate_bwd(dog, o, qkvg, *, name):
    s = o.shape[0]
    tr = _pick(s, 512, 8)

    def body(dog_ref, o_ref, g_ref, do_ref, dg_ref, dl_ref):
        sg = _sigmoid(g_ref[...])
        ov = o_ref[...]
        dog_v = dog_ref[...]
        do = dog_v * sg
        do_ref[...] = do
        dg_ref[...] = (dog_v * ov * sg * (1.0 - sg)).astype(BF16)
        dl_ref[...] = _dotf(do * ov, _group_matrix(FOX_D, FOX_HD))

    tile = pl.BlockSpec((tr, FOX_D), lambda i: (i, 0))
    return pl.pallas_call(
        body, name=name, grid=(s // tr,), in_specs=[tile, tile, pl.BlockSpec((tr, FOX_D), lambda i: (i, 3))],
        out_specs=[tile, tile, pl.BlockSpec((tr, FOX_H), lambda i: (i, 0))],
        out_shape=[jax.ShapeDtypeStruct((s, FOX_D), F32), jax.ShapeDtypeStruct((s, FOX_D), BF16),
                   jax.ShapeDtypeStruct((s, FOX_H), F32)],
        compiler_params=_cp(("parallel",)),
    )(dog, o, qkvg)


def _loss_head(h, g, target, *, name):
    s, d = h.shape
    tr = _pick(s, 512, 8)

    def body(h_ref, g_ref, t_ref, loss_ref, dh_ref, dg_ref):
        i = pl.program_id(0)
        x = h_ref[...]
        gv = g_ref[...]
        r = lax.rsqrt(jnp.mean(x * x, axis=-1, keepdims=True) + EPS)
        xh = x * r
        err = xh * gv - t_ref[...]
        part = 0.5 * jnp.sum(jnp.sum(err * err, axis=1, keepdims=True) * (1.0 / d), axis=0, keepdims=True)
        dy = err * (1.0 / d)
        dyg = dy * gv
        dh_ref[...] = r * (dyg - xh * jnp.mean(dyg * xh, axis=-1, keepdims=True))
        dgp = jnp.sum(dy * xh, axis=0, keepdims=True)

        @pl.when(i == 0)
        def _():
            loss_ref[...] = jnp.zeros_like(loss_ref) + part
            dg_ref[...] = dgp

        @pl.when(i > 0)
        def _():
            loss_ref[...] += part
            dg_ref[...] += dgp

    tile = pl.BlockSpec((tr, d), lambda i: (i, 0))
    vec = pl.BlockSpec((1, d), lambda i: (0, 0))
    return pl.pallas_call(
        body, name=name, grid=(s // tr,), in_specs=[tile, vec, tile],
        out_specs=[pl.BlockSpec((1, LANES), lambda i: (0, 0)), tile, vec],
        out_shape=[jax.ShapeDtypeStruct((1, LANES), F32), jax.ShapeDtypeStruct((s, d), F32),
                   jax.ShapeDtypeStruct((1, d), F32)],
        compiler_params=_cp(("arbitrary",)),
    )(h, g, target)


def _adamw(w, g, m, v, *, name):
    rows, cols = w.shape
    tr = _pick(rows, 256, 8)
    c1 = 1.0 - ADAM_B1 ** ADAM_STEP
    c2 = 1.0 - ADAM_B2 ** ADAM_STEP

    def body(w_ref, g_ref, m_ref, v_ref, d_ref, nm_ref, nv_ref):
        gv = g_ref[...]
        nm = ADAM_B1 * m_ref[...] + (1.0 - ADAM_B1) * gv
        nv = ADAM_B2 * v_ref[...] + (1.0 - ADAM_B2) * (gv * gv)
        d_ref[...] = -ADAM_LR * ((nm / c1) / (jnp.sqrt(nv / c2) + ADAM_EPS) + ADAM_WD * w_ref[...])
        nm_ref[...] = nm
        nv_ref[...] = nv

    tile = pl.BlockSpec((tr, cols), lambda i: (i, 0))
    shp = jax.ShapeDtypeStruct((rows, cols), F32)
    return pl.pallas_call(
        body, name=name, grid=(rows // tr,), in_specs=[tile] * 4, out_specs=[tile] * 3, out_shape=[shp] * 3,
        compiler_params=_cp(("parallel",)),
    )(w, g, m, v)


ANY = pl.BlockSpec(memory_space=pl.ANY)
N_DEV = 8


def _coords():
    return lax.axis_index("x"), lax.axis_index("y"), lax.axis_index("c")


def _other_chips(x, y):
    return [(1 - x, y), (x, 1 - y), (1 - x, 1 - y)]


def _allgather_small(buf, *, name, with_sum):
    rows = buf.shape[0]

    def body(*refs):
        if with_sum:
            x_ref, out_ref, sum_ref, send_sems, recv_sems = refs
        else:
            x_ref, out_ref, send_sems, recv_sems = refs
        x, y, c = _coords()
        me = 4 * x + 2 * y + c
        out_ref[me] = x_ref[...]
        copies = []
        for rel in range(1, N_DEV):
            px = (1 - x) if rel & 4 else x
            py = (1 - y) if rel & 2 else y
            pc = (1 - c) if rel & 1 else c
            cp = pltpu.make_async_remote_copy(
                src_ref=x_ref, dst_ref=out_ref.at[me], send_sem=send_sems.at[rel - 1], recv_sem=recv_sems.at[rel - 1],
                device_id=(px, py, pc), device_id_type=MESH)
            cp.start()
            copies.append(cp)
        for cp in copies:
            cp.wait()
        if with_sum:
            acc = out_ref[0]
            for k in range(1, N_DEV):
                acc = acc + out_ref[k]
            sum_ref[...] = acc

    slots = jax.ShapeDtypeStruct((N_DEV, rows, LANES), F32)
    vm = pl.BlockSpec(memory_space=pltpu.VMEM)
    out_shape = [slots, jax.ShapeDtypeStruct((rows, LANES), F32)] if with_sum else [slots]
    return pl.pallas_call(
        body, name=name, in_specs=[vm], out_specs=[vm] * len(out_shape), out_shape=out_shape,
        scratch_shapes=[pltpu.SemaphoreType.DMA((N_DEV - 1,)), pltpu.SemaphoreType.DMA((N_DEV - 1,))],
    )(buf)


class _Gather:
    per_array = 6

    def __init__(self, arrays):
        self.arrays = list(arrays)

    def out_shapes(self):
        return [jax.ShapeDtypeStruct((4,) + a.shape, a.dtype) for a in self.arrays]

    @staticmethod
    def _ici(ins, outs, send_sems, recv_sems, t, j, px, py, c, slot):
        return pltpu.make_async_remote_copy(
            src_ref=ins[t].at[c], dst_ref=outs[t].at[slot, c], send_sem=send_sems.at[6 * t + j],
            recv_sem=recv_sems.at[6 * t + j], device_id=(px, py, c), device_id_type=MESH)

    @staticmethod
    def _d2d(outs, send_sems, recv_sems, t, j, kj, half, sibling):
        return pltpu.make_async_remote_copy(
            src_ref=outs[t].at[kj, half], dst_ref=outs[t].at[kj, half], send_sem=send_sems.at[6 * t + 3 + j],
            recv_sem=recv_sems.at[6 * t + 3 + j], device_id=sibling, device_id_type=MESH)

    def start(self, ins, outs, send_sems, recv_sems):
        x, y, c = _coords()
        for t in range(len(ins)):
            for j, (px, py) in enumerate(_other_chips(x, y)):
                self._ici(ins, outs, send_sems, recv_sems, t, j, px, py, c, 2 * x + y).start()

    def finish(self, ins, outs, send_sems, recv_sems):
        x, y, c = _coords()
        chips = _other_chips(x, y)
        sibling = (x, y, 1 - c)
        started = []
        for t in range(len(ins)):
            for j, (px, py) in enumerate(chips):
                ici = self._ici(ins, outs, send_sems, recv_sems, t, j, px, py, c, 2 * px + py)
                ici.wait_recv()
                fwd = self._d2d(outs, send_sems, recv_sems, t, j, 2 * px + py, c, sibling)
                fwd.start()
                started += [ici, fwd]
        for t in range(len(ins)):
            for j, (px, py) in enumerate(chips):
                self._d2d(outs, send_sems, recv_sems, t, j, 2 * px + py, 1 - c, sibling).wait_recv()
        for cp in started:
            cp.wait_send()


class _Exchange:
    per_array = 7

    def __init__(self, arrays):
        self.arrays = list(arrays)

    def out_shapes(self):
        return [jax.ShapeDtypeStruct((7,) + a.shape[2:], a.dtype) for a in self.arrays]

    @staticmethod
    def _copies(ins, outs, send_sems, recv_sems):
        x, y, c = _coords()
        for t in range(len(ins)):
            for rel in range(1, N_DEV):
                px = (1 - x) if rel & 4 else x
                py = (1 - y) if rel & 2 else y
                pc = (1 - c) if rel & 1 else c
                yield pltpu.make_async_remote_copy(
                    src_ref=ins[t].at[2 * px + py, pc], dst_ref=outs[t].at[rel - 1], send_sem=send_sems.at[7 * t + rel - 1],
                    recv_sem=recv_sems.at[7 * t + rel - 1], device_id=(px, py, pc), device_id_type=MESH)

    def start(self, ins, outs, send_sems, recv_sems):
        for cp in self._copies(ins, outs, send_sems, recv_sems):
            cp.start()

    def finish(self, ins, outs, send_sems, recv_sems):
        for cp in self._copies(ins, outs, send_sems, recv_sems):
            cp.wait()


def _call(body, *, name, grid, in_specs, out_specs, out_shape, scratch_shapes, args, sem, rider=None, prefetch=()):
    n_in, n_out, n_pre = len(in_specs), len(out_specs), len(prefetch)
    n_c = len(rider.arrays) if rider is not None else 0

    def wrapped(*refs):
        pre, rest = refs[:n_pre], refs[n_pre:]
        ins, cins = rest[:n_in], rest[n_in:n_in + n_c]
        outs = rest[n_in + n_c:n_in + n_c + n_out]
        couts = rest[n_in + n_c + n_out:n_in + 2 * n_c + n_out]
        scratch = rest[n_in + 2 * n_c + n_out:]
        if rider is None:
            body(*pre, *ins, *outs, *scratch)
            return
        send_sems, recv_sems = scratch[-2:]
        ids = [pl.program_id(a) for a in range(len(grid))]
        first = functools.reduce(jnp.logical_and, [i == 0 for i in ids])
        last = functools.reduce(jnp.logical_and, [i == g - 1 for i, g in zip(ids, grid)])

        @pl.when(first)
        def _():
            rider.start(cins, couts, send_sems, recv_sems)

        body(*pre, *ins, *outs, *scratch[:-2])

        @pl.when(last)
        def _():
            rider.finish(cins, couts, send_sems, recv_sems)

    if rider is not None:
        nsem = rider.per_array * n_c
        in_specs = list(in_specs) + [ANY] * n_c
        out_specs = list(out_specs) + [ANY] * n_c
        out_shape = list(out_shape) + rider.out_shapes()
        scratch_shapes = list(scratch_shapes) + [pltpu.SemaphoreType.DMA((nsem,)), pltpu.SemaphoreType.DMA((nsem,))]
        args = list(args) + rider.arrays
        sem = ("arbitrary",) * len(grid)
    if n_pre:
        res = pl.pallas_call(
            wrapped, name=name, out_shape=out_shape, compiler_params=_cp(sem),
            grid_spec=pltpu.PrefetchScalarGridSpec(num_scalar_prefetch=n_pre, grid=grid, in_specs=in_specs,
                                                   out_specs=out_specs, scratch_shapes=scratch_shapes),
        )(*prefetch, *args)
    else:
        res = pl.pallas_call(
            wrapped, name=name, grid=grid, in_specs=in_specs, out_specs=out_specs, out_shape=out_shape,
            scratch_shapes=scratch_shapes, compiler_params=_cp(sem),
        )(*args)
    return list(res[:n_out]), list(res[n_out:])


def _run_rider(rider, *, name):
    n = len(rider.arrays)

    def body(*refs):
        ins, outs = refs[:n], refs[n:2 * n]
        send_sems, recv_sems = refs[2 * n:]
        rider.start(ins, outs, send_sems, recv_sems)
        rider.finish(ins, outs, send_sems, recv_sems)

    nsem = rider.per_array * n
    return pl.pallas_call(
        body, name=name, in_specs=[ANY] * n, out_specs=[ANY] * n, out_shape=rider.out_shapes(),
        scratch_shapes=[pltpu.SemaphoreType.DMA((nsem,)), pltpu.SemaphoreType.DMA((nsem,))],
    )(*rider.arrays)


def _sibling_swap(arrs, *, name):
    n = len(arrs)

    def body(*refs):
        ins, outs = refs[:n], refs[n:2 * n]
        send_sems, recv_sems = refs[2 * n:]
        x, y, c = _coords()
        copies = []
        for t in range(n):
            cp = pltpu.make_async_remote_copy(
                src_ref=ins[t], dst_ref=outs[t], send_sem=send_sems.at[t], recv_sem=recv_sems.at[t],
                device_id=(x, y, 1 - c), device_id_type=MESH)
            cp.start()
            copies.append(cp)
        for cp in copies:
            cp.wait()

    return pl.pallas_call(
        body, name=name, in_specs=[ANY] * n, out_specs=[ANY] * n,
        out_shape=[jax.ShapeDtypeStruct(a.shape, a.dtype) for a in arrs],
        scratch_shapes=[pltpu.SemaphoreType.DMA((n,)), pltpu.SemaphoreType.DMA((n,))],
    )(*arrs)


def _add_selected(stack, others, sel, *, name):
    _, m, cols = stack.shape
    q = others.shape[0]
    tr = _pick(m, 256, 16)

    def body(sel_ref, s_ref, o_ref, out_ref):
        acc = s_ref[0].astype(F32)
        for i in range(q):
            acc = acc + o_ref[i].astype(F32)
        out_ref[...] = acc

    return pl.pallas_call(
        body, name=name,
        grid_spec=pltpu.PrefetchScalarGridSpec(
            num_scalar_prefetch=1, grid=(m // tr,),
            in_specs=[pl.BlockSpec((1, tr, cols), lambda i, sel_ref: (sel_ref[0], i, 0)),
                      pl.BlockSpec((q, tr, cols), lambda i, sel_ref: (0, i, 0))],
            out_specs=pl.BlockSpec((tr, cols), lambda i, sel_ref: (i, 0))),
        out_shape=jax.ShapeDtypeStruct((m, cols), F32),
        compiler_params=_cp(("parallel",)),
    )(sel, stack, others)


BIG = ("ssd_w_in", "ssd_w_out", "fox_w_in", "fox_w_out", "ffn_w_up", "ffn_w_down")
COL_SHARDED = ("ssd_w_in", "fox_w_in", "ffn_w_up")
SMALL = (("mix_norm_g", (4, 1024)), ("ffn_norm_g", (4, 1024)), ("ssd_conv_w", (2, 4, 3072)), ("ssd_conv_b", (2, 3072)),
         ("ssd_dt_bias", (2, 32)), ("ssd_a_log", (2, 32)), ("ssd_d", (2, 32)), ("ssd_norm_g", (2, 2048)),
         ("fox_b_f", (2, 16)), ("fox_q_norm_g", (2, 64)), ("fox_k_norm_g", (2, 64)), ("ffn_conv_w", (4, 3, 2816)),
         ("ffn_conv_b", (4, 2816)), ("final_norm_g", (1024,)), ("loss", (1,)))
NAMES = ("mix_norm_g", "ffn_norm_g", "ssd_w_in", "ssd_conv_w", "ssd_conv_b", "ssd_dt_bias", "ssd_a_log", "ssd_d",
         "ssd_norm_g", "ssd_w_out", "fox_w_in", "fox_b_f", "fox_q_norm_g", "fox_k_norm_g", "fox_w_out", "ffn_w_up",
         "ffn_conv_w", "ffn_conv_b", "ffn_w_down", "final_norm_g")


def _pack(parts):
    flat = jnp.concatenate([jnp.reshape(p, (-1,)).astype(F32) for p in parts])
    rows = -(-flat.shape[0] // (8 * LANES)) * 8
    return jnp.pad(flat, (0, rows * LANES - flat.shape[0])).reshape(rows, LANES)


def _unpack(buf, shapes):
    flat = buf.reshape(-1)
    out, off = [], 0
    for shp in shapes:
        size = 1
        for d in shp:
            size *= d
        out.append(flat[off:off + size].reshape(shp))
        off += size
    return out


def _pad_lanes(a):
    return jnp.pad(a, ((0, 0), (0, LANES - a.shape[1])))


def _pad8(w):
    return jnp.pad(w, ((0, 8 - w.shape[0]), (0, 0)))


def _ssd_forward(h, p, name, rider=None):
    s = h.shape[0]
    hn = _rms_fwd(h, p["mix_g"], gw=D_MODEL, ncol=1, name=f"{name}_norm")
    zx = _matmul(hn, p["w_zx"], mode="nn", name=f"{name}_proj")
    dtp = _matmul(hn, p["w_dt"], mode="nn", name=f"{name}_proj_dt")
    xbc, _ = _conv_fwd(zx, p["conv_w8"], p["conv_b"], kw=SSD_K, width=SSD_CONV_DIM, u_col0=SSD_DI, name=f"{name}_conv")
    dt3 = dtp[:, :SSD_H].reshape(s, SSD_G, SSD_HPG)
    dtg, dtg_t = jnp.transpose(dt3, (1, 0, 2)), jnp.transpose(dt3, (1, 2, 0))
    sp = (p["bias_r"], p["bias_c"], p["alog_r"], p["alog_c"], p["d_r"])
    y, hprev, riding = _ssd_fwd(xbc, dtg, dtg_t, *sp, name=f"{name}_scan", rider=rider)
    y2 = _rms_fwd(y, p["norm_g"], gw=SSD_DI // SSD_G, ncol=SSD_G, z=zx, name=f"{name}_gnorm")
    out = _matmul(y2, p["w_out"], mode="nn", add=h, name=f"{name}_out")
    return out, dict(h=h, hn=hn, zx=zx, xbc=xbc, dtg=dtg, dtg_t=dtg_t, y=y, hprev=hprev, y2=y2), riding


def _ssd_backward(dh1, p, a, name, ride=()):
    s = dh1.shape[0]
    g = {}
    dy2 = _matmul(dh1, p["w_out"], mode="nt", name=f"{name}_out_dx")
    g["w_out"] = _matmul(a["y2"], dh1, mode="tn", out_dtype=BF16, name=f"{name}_out_dw")
    rider = _Exchange(list(ride) + [_to_slabs(g["w_out"], False)])
    dy, dz, g["norm_g"] = _rms_bwd(a["y"], p["norm_g"], dy2, gw=SSD_DI // SSD_G, ncol=SSD_G, z=a["zx"], name=f"{name}_gnorm_b")
    sp = (p["bias_r"], p["bias_c"], p["alog_r"], p["alog_c"], p["d_r"])
    dx, dbm, dcm, ddt, g["dt_bias"], g["a_log"], g["d"], riding = _ssd_bwd(
        a["xbc"], a["dtg"], a["dtg_t"], *sp, a["hprev"], dy, name=f"{name}_scan_b", rider=rider)
    dxbc, dwb, _ = _conv_bwd(a["zx"], p["conv_w8"], p["conv_b"], [dx, dbm, dcm], kw=SSD_K, width=SSD_CONV_DIM,
                             u_col0=SSD_DI, name=f"{name}_conv_b")
    g["conv_w"], g["conv_b"] = dwb[:SSD_K], dwb[7]
    dzx = jnp.concatenate([dz.astype(BF16), dxbc], axis=1)
    ddtp = _pad_lanes(jnp.transpose(ddt, (1, 0, 2)).reshape(s, SSD_H))
    dhn = _matmul(dzx, p["w_zx"], mode="nt", name=f"{name}_proj_dx")
    dhn = _matmul(ddtp, p["w_dt"], mode="nt", add=dhn, name=f"{name}_proj_dt_dx")
    dw_zx = _matmul(a["hn"], dzx, mode="tn", out_dtype=BF16, name=f"{name}_proj_dw")
    dw_dt = _matmul(a["hn"], ddtp, mode="tn", out_dtype=BF16, name=f"{name}_proj_dt_dw")
    g["w_in"] = jnp.concatenate([dw_zx, dw_dt[:, :SSD_H]], axis=1)
    dh, g["mix_g"] = _rms_bwd(a["h"], p["mix_g"], dhn, gw=D_MODEL, ncol=1, add=dh1, name=f"{name}_norm_b")
    g["w_out_received"] = riding[-1]
    return dh, g, riding[:-1]


def _fox_forward(h, p, name, rider=None):
    s = h.shape[0]
    hn = _rms_fwd(h, p["mix_g"], gw=D_MODEL, ncol=1, name=f"{name}_norm")
    qkvg = _matmul(hn, p["w_qkvg"], mode="nn", name=f"{name}_proj")
    fp = _matmul(hn, p["w_f"], mode="nn", name=f"{name}_proj_f")
    qs = _rms_fwd(qkvg, p["gq"] * FOX_SCALE, gw=FOX_D, ncol=1, x_col0=0, sub=FOX_HD, name=f"{name}_qnorm")
    kn = _rms_fwd(qkvg, p["gk"], gw=FOX_D, ncol=1, x_col0=1, sub=FOX_HD, name=f"{name}_knorm")
    f_t = jnp.transpose(fp[:, :FOX_H])
    cum_t = _fgate_fwd(f_t, p["b_f"], name=f"{name}_fgate")
    ck = cum_t.reshape(FOX_PAIRS, 2, s)
    o, lse, riding = _flash_fwd(qs, kn, qkvg, ck, name=f"{name}_attn", rider=rider)
    og = _ogate_fwd(o, qkvg, name=f"{name}_ogate")
    out = _matmul(og, p["w_out"], mode="nn", add=h, name=f"{name}_out")
    return out, dict(h=h, hn=hn, qkvg=qkvg, qs=qs, kn=kn, f_t=f_t, ck=ck, o=o, lse=lse, og=og), riding


def _fox_backward(dh1, p, a, name, ride=()):
    s = dh1.shape[0]
    g = {}
    dog = _matmul(dh1, p["w_out"], mode="nt", name=f"{name}_out_dx")
    g["w_out"] = _matmul(a["og"], dh1, mode="tn", out_dtype=BF16, name=f"{name}_out_dw")
    rider = _Exchange(list(ride) + [_to_slabs(g["w_out"], False)])
    do, dgate, delta = _ogate_bwd(dog, a["o"], a["qkvg"], name=f"{name}_ogate_b")
    swap = lambda v: jnp.transpose(v, (0, 2, 1))
    dl_t = jnp.transpose(delta.reshape(s, FOX_PAIRS, 2), (1, 2, 0))
    dq, dk, dv, dcq, dck, riding = _flash_bwd(a["qs"], a["kn"], a["qkvg"], do, swap(a["lse"]), dl_t, swap(a["ck"]),
                                              name=f"{name}_attn_b", rider=rider)
    dq_raw, dgq = _rms_bwd(a["qkvg"], p["gq"], dq, gw=FOX_D, ncol=1, x_col0=0, sub=FOX_HD, dx_dtype=BF16, name=f"{name}_qnorm_b")
    dk_raw, dgk = _rms_bwd(a["qkvg"], p["gk"], dk, gw=FOX_D, ncol=1, x_col0=1, sub=FOX_HD, dx_dtype=BF16, name=f"{name}_knorm_b")
    g["gq"] = dgq.reshape(FOX_H, FOX_HD).sum(axis=0)
    g["gk"] = dgk.reshape(FOX_H, FOX_HD).sum(axis=0)
    df_t, dbf = _fgate_bwd(dcq.reshape(FOX_H, s), swap(dck).reshape(FOX_H, s), a["f_t"], p["b_f"], name=f"{name}_fgate_b")
    g["b_f"] = dbf[:, 0]
    dproj = jnp.concatenate([dq_raw, dk_raw, dv.astype(BF16), dgate], axis=1)
    dfp = _pad_lanes(jnp.transpose(df_t))
    dhn = _matmul(dproj, p["w_qkvg"], mode="nt", name=f"{name}_proj_dx")
    dhn = _matmul(dfp, p["w_f"], mode="nt", add=dhn, name=f"{name}_proj_f_dx")
    dw_qkvg = _matmul(a["hn"], dproj, mode="tn", out_dtype=BF16, name=f"{name}_proj_dw")
    dw_f = _matmul(a["hn"], dfp, mode="tn", out_dtype=BF16, name=f"{name}_proj_f_dw")
    g["w_in"] = jnp.concatenate([dw_qkvg, dw_f[:, :FOX_H]], axis=1)
    dh, g["mix_g"] = _rms_bwd(a["h"], p["mix_g"], dhn, gw=D_MODEL, ncol=1, add=dh1, name=f"{name}_norm_b")
    g["w_out_received"] = riding[-1]
    return dh, g, riding[:-1]


def _ffn_forward(h, p, name, rider=None):
    hn = _rms_fwd(h, p["ffn_g"], gw=D_MODEL, ncol=1, name=f"{name}_norm")
    u = _matmul(hn, p["w_up"], mode="nn", name=f"{name}_up")
    act, riding = _conv_fwd(u, p["conv_w8"], p["conv_b"], kw=FFN_K, width=D_FF, u_col0=0, mul_col0=D_FF, out_dtype=BF16,
                            name=f"{name}_glu", rider=rider)
    out = _matmul(act, p["w_down"], mode="nn", add=h, name=f"{name}_down")
    return out, dict(h=h, hn=hn, u=u, act=act), riding


def _ffn_backward(dh2, p, a, name, ride=()):
    g = {}
    dact = _matmul(dh2, p["w_down"], mode="nt", name=f"{name}_down_dx")
    g["w_down"] = _matmul(a["act"], dh2, mode="tn", out_dtype=BF16, name=f"{name}_down_dw")
    du, dwb, riding = _conv_bwd(a["u"], p["conv_w8"], p["conv_b"], dact, kw=FFN_K, width=D_FF, u_col0=0, mul_col0=D_FF,
                                name=f"{name}_glu_b", rider=_Exchange(list(ride)) if ride else None)
    g["conv_w"], g["conv_b"] = dwb[:FFN_K], dwb[7]
    dhn = _matmul(du, p["w_up"], mode="nt", name=f"{name}_up_dx")
    g["w_up"] = _matmul(a["hn"], du, mode="tn", out_dtype=BF16, name=f"{name}_up_dw")
    dh, g["ffn_g"] = _rms_bwd(a["h"], p["ffn_g"], dhn, gw=D_MODEL, ncol=1, add=dh2, name=f"{name}_norm_b")
    return dh, g, riding


def _to_slabs(dw, col_sharded):
    rows, cols = dw.shape
    if col_sharded:
        return jnp.transpose(dw.reshape(rows, 4, cols // 4), (1, 0, 2)).reshape(4, 2, rows // 2, cols // 4)
    return dw.reshape(4, 2, rows // 8, cols)


def kernel(x, mix_norm_g, ffn_norm_g, ssd_w_in, ssd_conv_w, ssd_conv_b, ssd_dt_bias, ssd_a_log, ssd_d, ssd_norm_g, ssd_w_out, fox_w_in, fox_b_f, fox_q_norm_g, fox_k_norm_g, fox_w_out, ffn_w_up, ffn_conv_w, ffn_conv_b, ffn_w_down, final_norm_g, loss_target, m_mix_norm_g, m_ffn_norm_g, m_ssd_w_in, m_ssd_conv_w, m_ssd_conv_b, m_ssd_dt_bias, m_ssd_a_log, m_ssd_d, m_ssd_norm_g, m_ssd_w_out, m_fox_w_in, m_fox_b_f, m_fox_q_norm_g, m_fox_k_norm_g, m_fox_w_out, m_ffn_w_up, m_ffn_conv_w, m_ffn_conv_b, m_ffn_w_down, m_final_norm_g, v_mix_norm_g, v_ffn_norm_g, v_ssd_w_in, v_ssd_conv_w, v_ssd_conv_b, v_ssd_dt_bias, v_ssd_a_log, v_ssd_d, v_ssd_norm_g, v_ssd_w_out, v_fox_w_in, v_fox_b_f, v_fox_q_norm_g, v_fox_k_norm_g, v_fox_w_out, v_ffn_w_up, v_ffn_conv_w, v_ffn_conv_b, v_ffn_w_down, v_final_norm_g):
    w = dict(mix_norm_g=mix_norm_g, ffn_norm_g=ffn_norm_g, ssd_w_in=ssd_w_in, ssd_conv_w=ssd_conv_w, ssd_conv_b=ssd_conv_b,
             ssd_dt_bias=ssd_dt_bias, ssd_a_log=ssd_a_log, ssd_d=ssd_d, ssd_norm_g=ssd_norm_g, ssd_w_out=ssd_w_out,
             fox_w_in=fox_w_in, fox_b_f=fox_b_f, fox_q_norm_g=fox_q_norm_g, fox_k_norm_g=fox_k_norm_g, fox_w_out=fox_w_out,
             ffn_w_up=ffn_w_up, ffn_conv_w=ffn_conv_w, ffn_conv_b=ffn_conv_b, ffn_w_down=ffn_w_down, final_norm_g=final_norm_g)
    m_in = dict(zip(NAMES, (m_mix_norm_g, m_ffn_norm_g, m_ssd_w_in, m_ssd_conv_w, m_ssd_conv_b, m_ssd_dt_bias, m_ssd_a_log,
                            m_ssd_d, m_ssd_norm_g, m_ssd_w_out, m_fox_w_in, m_fox_b_f, m_fox_q_norm_g, m_fox_k_norm_g,
                            m_fox_w_out, m_ffn_w_up, m_ffn_conv_w, m_ffn_conv_b, m_ffn_w_down, m_final_norm_g)))
    v_in = dict(zip(NAMES, (v_mix_norm_g, v_ffn_norm_g, v_ssd_w_in, v_ssd_conv_w, v_ssd_conv_b, v_ssd_dt_bias, v_ssd_a_log,
                            v_ssd_d, v_ssd_norm_g, v_ssd_w_out, v_fox_w_in, v_fox_b_f, v_fox_q_norm_g, v_fox_k_norm_g,
                            v_fox_w_out, v_ffn_w_up, v_ffn_conv_w, v_ffn_conv_b, v_ffn_w_down, v_final_norm_g)))
    cx, cy, cc = _coords()
    chip = 2 * cx + cy
    h = x[0]
    target = loss_target[0]

    conv_shapes = [ssd_conv_w.shape, ffn_conv_w.shape]
    slots = _allgather_small(_pack([ssd_conv_w, ffn_conv_w]), name="gather_conv_w", with_sum=False)[0]
    per_chip = [_unpack(slots[2 * q], conv_shapes) for q in range(4)]
    ssd_conv_full = jnp.concatenate([pc[0] for pc in per_chip], axis=2)
    ffn_conv_full = jnp.concatenate([pc[1] for pc in per_chip], axis=2)
    low = {n: w[n].astype(BF16) for n in BIG}
    sub_weights = dict(ssd=("ssd_w_in", "ssd_w_out"), fox=("fox_w_in", "fox_w_out"), ffn=("ffn_w_up", "ffn_w_down"))

    def shards_of(kind, idx):
        return [low[n][idx].reshape(2, low[n].shape[1] // 2, low[n].shape[2]) for n in sub_weights[kind]]

    def assemble(kind, idx, gathered):
        full = []
        for n, own, gth in zip(sub_weights[kind], shards_of(kind, idx), gathered):
            gth = lax.dynamic_update_slice(gth, own[None], (chip, 0, 0, 0))
            _, _, half, cols = gth.shape
            if n in COL_SHARDED:
                full.append(jnp.transpose(gth.reshape(4, 2 * half, cols), (1, 0, 2)).reshape(2 * half, 4 * cols))
            else:
                full.append(gth.reshape(8 * half, cols))
        return full

    def ssd_params(j, i, weights):
        w_in, w_out = weights
        g3 = lambda v: v.reshape(SSD_G, 1, SSD_HPG)
        g3c = lambda v: v.reshape(SSD_G, SSD_HPG, 1)
        return dict(mix_g=mix_norm_g[i][None], w_zx=w_in[:, :SSD_ZX], w_dt=_pad_lanes(w_in[:, SSD_ZX:]),
                    conv_w8=_pad8(ssd_conv_full[j]), conv_b=ssd_conv_b[j][None], bias_r=g3(ssd_dt_bias[j]),
                    bias_c=g3c(ssd_dt_bias[j]), alog_r=g3(ssd_a_log[j]), alog_c=g3c(ssd_a_log[j]), d_r=g3(ssd_d[j]),
                    norm_g=ssd_norm_g[j][None], w_out=w_out)

    def fox_params(j, i, weights):
        w_in, w_out = weights
        return dict(mix_g=mix_norm_g[i][None], w_qkvg=w_in[:, :4 * FOX_D], w_f=_pad_lanes(w_in[:, 4 * FOX_D:]),
                    gq=jnp.tile(fox_q_norm_g[j], FOX_H)[None], gk=jnp.tile(fox_k_norm_g[j], FOX_H)[None],
                    b_f=fox_b_f[j][:, None], w_out=w_out)

    def ffn_params(i, weights):
        w_up, w_down = weights
        return dict(ffn_g=ffn_norm_g[i][None], w_up=w_up, conv_w8=_pad8(ffn_conv_full[i]), conv_b=ffn_conv_b[i][None],
                    w_down=w_down)

    order = [("ssd", 0), ("ffn", 0), ("fox", 0), ("ffn", 1), ("ssd", 1), ("ffn", 2), ("fox", 1), ("ffn", 3)]
    fetch = {("ssd", 0): [("ffn", 0)], ("ffn", 0): [("fox", 0)], ("fox", 0): [("ffn", 1), ("ssd", 1), ("ffn", 2)],
             ("ssd", 1): [("fox", 1)], ("fox", 1): [("ffn", 3)]}
    ready = {("ssd", 0): assemble("ssd", 0, _run_rider(_Gather(shards_of("ssd", 0)), name="gather_first"))}
    params, acts = {}, {}
    forward = dict(ssd=_ssd_forward, fox=_fox_forward, ffn=_ffn_forward)
    for kind, idx in order:
        if kind == "ssd":
            params[kind, idx] = ssd_params(idx, 2 * idx, ready.pop((kind, idx)))
        elif kind == "fox":
            params[kind, idx] = fox_params(idx, 2 * idx + 1, ready.pop((kind, idx)))
        else:
            params[kind, idx] = ffn_params(idx, ready.pop((kind, idx)))
        wanted = fetch.get((kind, idx), [])
        rider = _Gather([s for sub in wanted for s in shards_of(*sub)]) if wanted else None
        h, acts[kind, idx], riding = forward[kind](h, params[kind, idx], f"{kind}{idx}", rider=rider)
        for q, sub in enumerate(wanted):
            ready[sub] = assemble(*sub, riding[2 * q:2 * q + 2])
    loss_part, dh, d_final_g = _loss_head(h, final_norm_g[None], target, name="loss_head")

    backward = dict(ssd=_ssd_backward, fox=_fox_backward, ffn=_ffn_backward)
    grad_keys = dict(ssd=("w_in", "w_out"), fox=("w_in", "w_out"), ffn=("w_up", "w_down"))
    sub_g, slabs, received = {}, {}, {}
    waiting = []
    for sub in reversed(order):
        kind = sub[0]
        dh, sub_g[sub], got = backward[kind](dh, params[sub], acts[sub], f"{kind}{sub[1]}",
                                             ride=[slabs[key] for key in waiting])
        received.update(zip(waiting, got))
        waiting = []
        for q, (key, n) in enumerate(zip(grad_keys[kind], sub_weights[kind])):
            slabs[sub, q] = _to_slabs(sub_g[sub][key], n in COL_SHARDED)
            if kind != "ffn" and q == 1:
                received[sub, q] = sub_g[sub]["w_out_received"]
            else:
                waiting.append((sub, q))
    received.update(zip(waiting, _run_rider(_Exchange([slabs[key] for key in waiting]), name="rs_last_exchange")))
    grad_x = dh[None]
    ssd_g, fox_g = [sub_g["ssd", 0], sub_g["ssd", 1]], [sub_g["fox", 0], sub_g["fox", 1]]
    mix_g = [ssd_g[0], fox_g[0], ssd_g[1], fox_g[1]]
    ffn_g = [sub_g["ffn", i] for i in range(DEPTH)]

    me = jnp.reshape(2 * chip + cc, (1,)).astype(jnp.int32)
    finals = {}
    for sub in order:
        for q in range(2):
            _, _, m, cols = slabs[sub, q].shape
            finals[sub, q] = _add_selected(slabs[sub, q].reshape(8, m, cols), received[sub, q], me,
                                           name=f"rs_add_{sub[0]}{sub[1]}_{q}")
    keys = list(finals)
    others = dict(zip(keys, _sibling_swap([finals[key] for key in keys], name="rs_result_swap")))
    grads = {}
    for kind, names in sub_weights.items():
        for q, n in enumerate(names):
            subs = [sub for sub in sorted(set(order)) if sub[0] == kind]
            mine = jnp.stack([finals[sub, q] for sub in subs])
            theirs = jnp.stack([others[sub, q] for sub in subs])
            halves = jnp.stack([jnp.where(cc == 0, mine, theirs), jnp.where(cc == 0, theirs, mine)], axis=1)
            grads[n] = halves.reshape(w[n].shape)
    small = dict(
        mix_norm_g=jnp.concatenate([g["mix_g"] for g in mix_g], axis=0),
        ffn_norm_g=jnp.concatenate([g["ffn_g"] for g in ffn_g], axis=0),
        ssd_conv_w=jnp.stack([g["conv_w"] for g in ssd_g]), ssd_conv_b=jnp.stack([g["conv_b"] for g in ssd_g]),
        ssd_dt_bias=jnp.stack([g["dt_bias"].reshape(SSD_H) for g in ssd_g]),
        ssd_a_log=jnp.stack([g["a_log"].reshape(SSD_H) for g in ssd_g]),
        ssd_d=jnp.stack([g["d"].reshape(SSD_H) for g in ssd_g]),
        ssd_norm_g=jnp.concatenate([g["norm_g"] for g in ssd_g], axis=0),
        fox_b_f=jnp.stack([g["b_f"] for g in fox_g]), fox_q_norm_g=jnp.stack([g["gq"] for g in fox_g]),
        fox_k_norm_g=jnp.stack([g["gk"] for g in fox_g]),
        ffn_conv_w=jnp.stack([g["conv_w"] for g in ffn_g]), ffn_conv_b=jnp.stack([g["conv_b"] for g in ffn_g]),
        final_norm_g=d_final_g[0], loss=loss_part[0, :1])
    _, total = _allgather_small(_pack([small[n] for n, _ in SMALL]), name="reduce_small", with_sum=True)
    for (n, shp), val in zip(SMALL, _unpack(total, [shp for _, shp in SMALL])):
        grads[n] = val
    loss = grads.pop("loss")[0]
    grads["ssd_conv_w"] = lax.dynamic_slice_in_dim(grads["ssd_conv_w"], chip * ssd_conv_w.shape[2], ssd_conv_w.shape[2], axis=2)
    grads["ffn_conv_w"] = lax.dynamic_slice_in_dim(grads["ffn_conv_w"], chip * ffn_conv_w.shape[2], ffn_conv_w.shape[2], axis=2)

    deltas, new_m, new_v = {}, {}, {}
    for n in NAMES:
        shp = w[n].shape
        two_d = (1, shp[0]) if len(shp) == 1 else (-1, shp[-1])
        r2 = lambda a: a.reshape(two_d)
        d, nm, nv = _adamw(r2(w[n]), r2(grads[n]), r2(m_in[n]), r2(v_in[n]), name=f"adamw_{n}")
        deltas[n], new_m[n], new_v[n] = d.reshape(shp), nm.reshape(shp), nv.reshape(shp)
    return (loss, grad_x, *[grads[n] for n in NAMES], *[deltas[n] for n in NAMES], *[new_m[n] for n in NAMES],
            *[new_v[n] for n in NAMES])
```

```python
import functools

import jax
import jax.numpy as jnp
from jax import lax
from jax.experimental import pallas as pl
from jax.experimental.pallas import tpu as pltpu

F32 = jnp.float32
BF16 = jnp.bfloat16
HI = lax.Precision.HIGHEST
MESH = pl.DeviceIdType.MESH

D_MODEL = 1024
DEPTH = 4
EPS = 1e-6
SSD_DI = 2048
SSD_HD = 64
SSD_G = 4
SSD_HPG = 8
SSD_N = 128
SSD_K = 4
CHUNK = 128
SSD_CONV_DIM = 3072
SSD_ZX = SSD_DI + SSD_CONV_DIM
SSD_H = 32
FOX_HD = 64
FOX_H = 16
FOX_D = 1024
D_FF = 2816
FFN_K = 3
LANES = 128
VMEM_LIMIT = 56 * 1024 * 1024

ADAM_LR = 0.001
ADAM_B1 = 0.9
ADAM_B2 = 0.999
ADAM_EPS = 1e-08
ADAM_WD = 0.01
ADAM_STEP = 10

NN = (((1,), (0,)), ((), ()))
NT = (((1,), (1,)), ((), ()))
TN = (((0,), (0,)), ((), ()))


def _pick(n, cap, mult=LANES):
    best = None
    for t in range(mult, min(n, cap) + 1, mult):
        if n % t == 0:
            best = t
    return best if best is not None else n


def _cp(sem):
    return pltpu.CompilerParams(dimension_semantics=sem, vmem_limit_bytes=VMEM_LIMIT)


def _sigmoid(x):
    return jax.nn.sigmoid(x)


def _silu(x):
    return x * _sigmoid(x)


def _dsilu(x):
    s = _sigmoid(x)
    return s * (1.0 + x * (1.0 - s))


def _softplus(x):
    e = jnp.exp(-jnp.abs(x))
    u = 1.0 + e
    l1p = jnp.where(u == 1.0, e, jnp.log(u) * (e / (u - 1.0)))
    return jnp.maximum(x, 0.0) + l1p


def _dotf(a, b, dn=NN, *, onehot="b", pieces=2):
    x, e = (a, b) if onehot == "b" else (b, a)
    e = e.astype(BF16)
    acc = None
    for n in range(pieces):
        hi = x.astype(BF16)
        part = lax.dot_general(hi, e, dn, preferred_element_type=F32) if onehot == "b" else \
            lax.dot_general(e, hi, dn, preferred_element_type=F32)
        acc = part if acc is None else acc + part
        if n + 1 < pieces:
            x = x - hi.astype(F32)
    return acc


def _dotb(a, b, dn=NN):
    return lax.dot_general(a.astype(BF16), b.astype(BF16), dn, preferred_element_type=F32)


def _group_matrix(width, sub, transpose=False):
    ng = width // sub
    shape = (ng, width) if transpose else (width, ng)
    lane = lax.broadcasted_iota(jnp.int32, shape, 1 if transpose else 0)
    grp = lax.broadcasted_iota(jnp.int32, shape, 0 if transpose else 1)
    return (lane // sub == grp).astype(F32)


def _gmean(v, sub):
    width = v.shape[-1]
    if sub == width:
        return jnp.mean(v, axis=-1, keepdims=True)
    s = _dotf(v, _group_matrix(width, sub))
    return _dotf(s, _group_matrix(width, sub, transpose=True)) * (1.0 / sub)


def _matmul(a, b, *, mode, name, out_dtype=F32, add=None):
    a_planes = a.shape[0] if (mode == "nt" and a.ndim == 3) else 0
    b_planes = b.shape[0] if (mode == "tn" and b.ndim == 3) else 0
    a2 = (a.shape[1], a.shape[0] * a.shape[2]) if a_planes else a.shape
    b2 = (b.shape[1], b.shape[0] * b.shape[2]) if b_planes else b.shape
    if mode == "nn":
        (m, k), (k2, n) = a2, b2
    elif mode == "nt":
        (m, k), (n, k2) = a2, b2
    else:
        (k, m), (k2, n) = a2, b2
    assert k == k2, (a.shape, b.shape, mode)
    tm, tn = _pick(m, 1536), _pick(n // b_planes if b_planes else n, 1536)
    tk = _pick(k // a_planes if a_planes else k, 1536)
    nk = k // tk
    dn = {"nn": NN, "nt": NT, "tn": TN}[mode]
    has_add = add is not None

    def body(*refs):
        if has_add:
            a_ref, b_ref, add_ref, o_ref, acc_ref = refs
        else:
            a_ref, b_ref, o_ref, acc_ref = refs
            add_ref = None
        kk = pl.program_id(2)
        part = _dotb(a_ref[0] if a_planes else a_ref[...], b_ref[0] if b_planes else b_ref[...], dn)

        def finish(r):
            if has_add:
                r = r + add_ref[...]
            o_ref[...] = r.astype(out_dtype)

        if nk == 1:
            finish(part)
        else:
            @pl.when(kk == 0)
            def _():
                acc_ref[...] = part

            @pl.when(kk > 0)
            def _():
                acc_ref[...] += part

            @pl.when(kk == nk - 1)
            def _():
                finish(acc_ref[...])

    if mode == "nn":
        a_spec = pl.BlockSpec((tm, tk), lambda i, j, q: (i, q))
        b_spec = pl.BlockSpec((tk, tn), lambda i, j, q: (q, j))
    elif mode == "nt":
        per = (k // a_planes) // tk if a_planes else 0
        a_spec = (pl.BlockSpec((1, tm, tk), lambda i, j, q: (q // per, i, q % per)) if a_planes
                  else pl.BlockSpec((tm, tk), lambda i, j, q: (i, q)))
        b_spec = pl.BlockSpec((tn, tk), lambda i, j, q: (j, q))
    else:
        per = (n // b_planes) // tn if b_planes else 0
        a_spec = pl.BlockSpec((tk, tm), lambda i, j, q: (q, i))
        b_spec = (pl.BlockSpec((1, tk, tn), lambda i, j, q: (j // per, q, j % per)) if b_planes
                  else pl.BlockSpec((tk, tn), lambda i, j, q: (q, j)))
    o_spec = pl.BlockSpec((tm, tn), lambda i, j, q: (i, j))
    in_specs = [a_spec, b_spec] + ([o_spec] if has_add else [])
    args = (a, b) + ((add,) if has_add else ())
    return pl.pallas_call(
        body, name=name, grid=(m // tm, n // tn, nk), in_specs=in_specs, out_specs=o_spec,
        out_shape=jax.ShapeDtypeStruct((m, n), out_dtype),
        scratch_shapes=[pltpu.VMEM((tm, tn) if nk > 1 else (8, LANES), F32)],
        compiler_params=_cp(("parallel", "parallel", "arbitrary")),
    )(*args)


def _rms_fwd(x, g, *, gw, ncol, name, x_col0=0, sub=None, z=None, z_col0=0, out_dtype=BF16):
    rows = x.shape[0]
    tr = _pick(rows, 1024, 8)
    sub = gw if sub is None else sub
    gated = z is not None

    def body(*refs):
        if gated:
            x_ref, z_ref, g_ref, o_ref = refs
            xv = x_ref[...] * _silu(z_ref[...])
        else:
            x_ref, g_ref, o_ref = refs
            xv = x_ref[...]
        r = lax.rsqrt(_gmean(xv * xv, sub) + EPS)
        o_ref[...] = (xv * r * g_ref[...]).astype(out_dtype)

    specs = [pl.BlockSpec((tr, gw), lambda j, i: (i, x_col0 + j))]
    args = [x]
    if gated:
        specs.append(pl.BlockSpec((tr, gw), lambda j, i: (i, z_col0 + j)))
        args.append(z)
    specs.append(pl.BlockSpec((1, gw), lambda j, i: (0, j)))
    args.append(g)
    return pl.pallas_call(
        body, name=name, grid=(ncol, rows // tr), in_specs=specs,
        out_specs=pl.BlockSpec((tr, gw), lambda j, i: (i, j)),
        out_shape=jax.ShapeDtypeStruct((rows, gw * ncol), out_dtype),
        compiler_params=_cp(("parallel", "parallel")),
    )(*args)


def _rms_bwd(x, g, dy, *, gw, ncol, name, x_col0=0, sub=None, z=None, z_col0=0, add=None, dx_dtype=F32):
    rows = x.shape[0]
    tr = _pick(rows, 1024, 8)
    sub = gw if sub is None else sub
    gated = z is not None
    has_add = add is not None

    def body(*refs):
        refs = list(refs)
        x_ref = refs.pop(0)
        z_ref = refs.pop(0) if gated else None
        g_ref = refs.pop(0)
        dy_ref = refs.pop(0)
        add_ref = refs.pop(0) if has_add else None
        dx_ref = refs.pop(0)
        dz_ref = refs.pop(0) if gated else None
        dg_ref = refs.pop(0)
        i = pl.program_id(1)
        xv = x_ref[...]
        if gated:
            zz = z_ref[...]
            yz = xv * _silu(zz)
        else:
            yz = xv
        r = lax.rsqrt(_gmean(yz * yz, sub) + EPS)
        xh = yz * r
        dy = dy_ref[...].astype(F32)
        dyg = dy * g_ref[...]
        d_yz = r * (dyg - xh * _gmean(dyg * xh, sub))
        if gated:
            dx_ref[...] = (d_yz * _silu(zz)).astype(dx_dtype)
            dz_ref[...] = (d_yz * xv * _dsilu(zz)).astype(dx_dtype)
        elif has_add:
            dx_ref[...] = (d_yz + add_ref[...]).astype(dx_dtype)
        else:
            dx_ref[...] = d_yz.astype(dx_dtype)
        part = jnp.sum(dy * xh, axis=0, keepdims=True)

        @pl.when(i == 0)
        def _():
            dg_ref[...] = part

        @pl.when(i > 0)
        def _():
            dg_ref[...] += part

    tile = pl.BlockSpec((tr, gw), lambda j, i: (i, j))
    specs = [pl.BlockSpec((tr, gw), lambda j, i: (i, x_col0 + j))]
    args = [x]
    if gated:
        specs.append(pl.BlockSpec((tr, gw), lambda j, i: (i, z_col0 + j)))
        args.append(z)
    specs += [pl.BlockSpec((1, gw), lambda j, i: (0, j)), tile]
    args += [g, dy]
    if has_add:
        specs.append(tile)
        args.append(add)
    width = gw * ncol
    out_shape = [jax.ShapeDtypeStruct((rows, width), dx_dtype)]
    out_specs = [tile]
    if gated:
        out_shape.append(jax.ShapeDtypeStruct((rows, width), dx_dtype))
        out_specs.append(tile)
    out_shape.append(jax.ShapeDtypeStruct((1, width), F32))
    out_specs.append(pl.BlockSpec((1, gw), lambda j, i: (0, j)))
    return pl.pallas_call(
        body, name=name, grid=(ncol, rows // tr), in_specs=specs, out_specs=out_specs, out_shape=out_shape,
        compiler_params=_cp(("parallel", "arbitrary")),
    )(*args)


HALO = 8


def _conv_rows(tc):
    return 16 * 8 * LANES // tc


def _conv_fwd(u, w8, b, *, kw, width, name, u_col0=0, mul_col0=None, out_dtype=F32, rider=None):
    rows = u.shape[0]
    ts = _pick(rows, 2048, 8)
    tc = _pick(width, 512)
    gated = mul_col0 is not None
    c0 = u_col0 // tc
    m0 = (mul_col0 // tc) if gated else 0
    assert u_col0 % tc == 0 and (not gated or mul_col0 % tc == 0)

    def body(*refs):
        if gated:
            cur_ref, halo_ref, mul_ref, w_ref, b_ref, o_ref, ext = refs
        else:
            cur_ref, halo_ref, w_ref, b_ref, o_ref, ext = refs
        i = pl.program_id(0)
        ext[pl.ds(0, HALO), :] = jnp.where(i == 0, 0.0, halo_ref[...])
        ext[pl.ds(HALO, ts), :] = cur_ref[...]
        bias = b_ref[...]
        taps = [w_ref[k:k + 1, :] for k in range(kw)]
        rb = _conv_rows(tc)
        for r0 in range(0, ts, rb):
            pre = bias + taps[0] * ext[pl.ds(r0 + HALO - (kw - 1), rb), :]
            for k in range(1, kw):
                pre = pre + taps[k] * ext[pl.ds(r0 + HALO - (kw - 1) + k, rb), :]
            act = _silu(pre)
            if gated:
                act = act * mul_ref[pl.ds(r0, rb), :]
            o_ref[pl.ds(r0, rb), :] = act.astype(out_dtype)

    hb = ts // HALO
    specs = [pl.BlockSpec((ts, tc), lambda i, j: (i, c0 + j)),
             pl.BlockSpec((HALO, tc), lambda i, j: (jnp.maximum(i * hb - 1, 0), c0 + j))]
    args = [u, u]
    if gated:
        specs.append(pl.BlockSpec((ts, tc), lambda i, j: (i, m0 + j)))
        args.append(u)
    specs += [pl.BlockSpec((8, tc), lambda i, j: (0, j)), pl.BlockSpec((1, tc), lambda i, j: (0, j))]
    args += [w8, b]
    outs, riding = _call(
        body, name=name, grid=(rows // ts, width // tc), in_specs=specs,
        out_specs=[pl.BlockSpec((ts, tc), lambda i, j: (i, j))],
        out_shape=[jax.ShapeDtypeStruct((rows, width), out_dtype)],
        scratch_shapes=[pltpu.VMEM((ts + HALO, tc), F32)], sem=("parallel", "parallel"), rider=rider, args=args)
    return outs + [riding]


def _conv_bwd(u, w8, b, dact, *, kw, width, name, u_col0=0, mul_col0=None, du_dtype=BF16, rider=None):
    rows = u.shape[0]
    ts = _pick(rows, 2048, 8)
    tc = _pick(width, 512)
    pieces = list(dact) if isinstance(dact, (list, tuple)) else [dact]
    firsts, seen = [], 0
    for piece in pieces:
        assert piece.shape[1] % tc == 0, (piece.shape, tc)
        firsts.append(seen // tc)
        seen += piece.shape[1]
    assert seen == width
    gated = mul_col0 is not None
    c0 = u_col0 // tc
    m0 = (mul_col0 // tc) if gated else 0
    nt = rows // ts
    hb = ts // HALO

    def body(*refs):
        refs = list(refs)
        cur_ref, halo_ref = refs.pop(0), refs.pop(0)
        mul_ref = refs.pop(0) if gated else None
        w_ref, b_ref = refs.pop(0), refs.pop(0)
        da_refs = [refs.pop(0) for _ in pieces]
        col_tile = pl.program_id(0)
        du_ref = refs.pop(0)
        dwb_ref, ext_u, ext_d = refs
        t = pl.program_id(1)
        ti = nt - 1 - t
        ext_u[pl.ds(0, HALO), :] = jnp.where(ti == 0, 0.0, halo_ref[...])
        ext_u[pl.ds(HALO, ts), :] = cur_ref[...]

        @pl.when(t == 0)
        def _():
            ext_d[pl.ds(ts, HALO), :] = jnp.zeros((HALO, tc), F32)
            dwb_ref[...] = jnp.zeros((8, tc), F32)

        bias = b_ref[...]
        taps = [w_ref[k:k + 1, :] for k in range(kw)]
        rb = _conv_rows(tc)
        dw_acc = [jnp.zeros((1, tc), F32) for _ in range(kw)]
        db_acc = jnp.zeros((1, tc), F32)
        for r0 in reversed(range(0, ts, rb)):
            shifted = [ext_u[pl.ds(r0 + HALO - (kw - 1) + k, rb), :] for k in range(kw)]
            pre = bias + taps[0] * shifted[0]
            for k in range(1, kw):
                pre = pre + taps[k] * shifted[k]
            sg = _sigmoid(pre)
            dsilu = sg * (1.0 + pre * (1.0 - sg))
            da = da_refs[0][pl.ds(r0, rb), :].astype(F32)
            for first, ref in zip(firsts[1:], da_refs[1:]):
                da = jnp.where(col_tile >= first, ref[pl.ds(r0, rb), :].astype(F32), da)
            if gated:
                du_ref[1, pl.ds(r0, rb), :] = (da * (pre * sg)).astype(du_dtype)
                dgp = da * mul_ref[pl.ds(r0, rb), :] * dsilu
            else:
                dgp = da * dsilu
            ext_d[pl.ds(r0, rb), :] = dgp
            du = taps[kw - 1] * dgp
            for k in range(kw - 1):
                du = du + taps[k] * ext_d[pl.ds(r0 + kw - 1 - k, rb), :]
            if gated:
                du_ref[0, pl.ds(r0, rb), :] = du.astype(du_dtype)
            else:
                du_ref[pl.ds(r0, rb), :] = du.astype(du_dtype)
            for k in range(kw):
                dw_acc[k] = dw_acc[k] + jnp.sum(dgp * shifted[k], axis=0, keepdims=True)
            db_acc = db_acc + jnp.sum(dgp, axis=0, keepdims=True)
        for k in range(kw):
            dwb_ref[k:k + 1, :] += dw_acc[k]
        dwb_ref[7:8, :] += db_acc
        ext_d[pl.ds(ts, HALO), :] = ext_d[pl.ds(0, HALO), :]

    specs = [pl.BlockSpec((ts, tc), lambda j, t: (nt - 1 - t, c0 + j)),
             pl.BlockSpec((HALO, tc), lambda j, t: (jnp.maximum((nt - 1 - t) * hb - 1, 0), c0 + j))]
    args = [u, u]
    if gated:
        specs.append(pl.BlockSpec((ts, tc), lambda j, t: (nt - 1 - t, m0 + j)))
        args.append(u)
    tile = pl.BlockSpec((ts, tc), lambda j, t: (nt - 1 - t, j))
    specs += [pl.BlockSpec((8, tc), lambda j, t: (0, j)), pl.BlockSpec((1, tc), lambda j, t: (0, j))]
    args += [w8, b]
    for first, piece in zip(firsts, pieces):
        count = piece.shape[1] // tc

        def piece_map(j, t, first=first, count=count):
            mine = jnp.logical_and(j >= first, j < first + count)
            return (jnp.where(mine, nt - 1 - t, 0), jnp.clip(j - first, 0, count - 1))

        specs.append(pl.BlockSpec((ts, tc), piece_map))
        args.append(piece)
    if gated:
        out_shape = [jax.ShapeDtypeStruct((2, rows, width), du_dtype)]
        out_specs = [pl.BlockSpec((2, ts, tc), lambda j, t: (0, nt - 1 - t, j))]
    else:
        out_shape = [jax.ShapeDtypeStruct((rows, width), du_dtype)]
        out_specs = [tile]
    out_shape.append(jax.ShapeDtypeStruct((8, width), F32))
    out_specs.append(pl.BlockSpec((8, tc), lambda j, t: (0, j)))
    outs, riding = _call(
        body, name=name, grid=(width // tc, nt), in_specs=specs, out_specs=out_specs, out_shape=out_shape,
        scratch_shapes=[pltpu.VMEM((ts + HALO, tc), F32), pltpu.VMEM((ts + HALO, tc), F32)],
        sem=("parallel", "arbitrary"), rider=rider, args=args)
    return outs + [riding]


GW = SSD_HPG * SSD_HD


def _ssd_common(x, bm, cm, dt_raw, dt_raw_t, bias_r, bias_c, alog_r, alog_c):
    row = lax.broadcasted_iota(jnp.int32, (CHUNK, CHUNK), 0)
    col = lax.broadcasted_iota(jnp.int32, (CHUNK, CHUNK), 1)
    causal = row >= col
    tril = causal.astype(F32)
    triu = (row <= col).astype(F32)
    spread = _group_matrix(GW, SSD_HD, transpose=True)
    dt = _softplus(dt_raw + bias_r)
    dt_t = _softplus(dt_raw_t + bias_c)
    a_r = -jnp.exp(alog_r)
    a_c = -jnp.exp(alog_c)
    acs = _dotf(tril, dt * a_r, onehot="a", pieces=3)
    acs_t = _dotf(dt_t * a_c, triu, pieces=3)
    last = acs[CHUNK - 1:CHUNK, :]
    ds = jnp.exp(last - acs)
    cd = jnp.exp(last)
    c = dict(causal=causal, tril=tril, triu=triu, spread=spread, dt=dt, a_r=a_r, acs=acs, acs_t=acs_t, ds=ds, cd=cd)
    c["eb"] = _dotf(jnp.exp(acs), spread)
    c["dsb"] = _dotf(ds, spread)
    c["cdb"] = _dotf(cd, spread)
    c["dtb"] = _dotf(dt, spread)
    c["xdt"] = x * c["dtb"]
    c["cb"] = _dotb(cm, bm, NT)
    return c


def _ssd_lam(c, r):
    diff = c["acs"][:, r:r + 1] - c["acs_t"][r:r + 1, :]
    return jnp.exp(jnp.where(c["causal"], diff, -jnp.inf))


GP = 2


def _ssd_specs(nc, rev):
    def ci(t):
        return (nc - 1 - t) if rev else t
    xs = pl.BlockSpec((CHUNK, GP * GW), lambda g, t: (ci(t), g))
    bs = pl.BlockSpec((CHUNK, GP * SSD_N), lambda g, t: (ci(t), SSD_DI // (GP * SSD_N) + g))
    cs = pl.BlockSpec((CHUNK, GP * SSD_N), lambda g, t: (ci(t), (SSD_DI // SSD_N + SSD_G) // GP + g))
    dts = pl.BlockSpec((GP, CHUNK, 8), lambda g, t: (g, ci(t), 0))
    dtts = pl.BlockSpec((GP, 8, CHUNK), lambda g, t: (g, 0, ci(t)))
    pr = pl.BlockSpec((GP, 1, 8), lambda g, t: (g, 0, 0))
    pc = pl.BlockSpec((GP, 8, 1), lambda g, t: (g, 0, 0))
    hs = pl.BlockSpec((1, GP, SSD_N, GW), lambda g, t: (ci(t), g, 0, 0))
    return xs, bs, cs, dts, dtts, pr, pc, hs


def _ssd_fwd(xbc, dtg, dtg_t, bias_r, bias_c, alog_r, alog_c, d_r, *, name, rider=None):
    s = xbc.shape[0]
    nc = s // CHUNK
    xs, bs, cs, dts, dtts, pr, pc, hs = _ssd_specs(nc, False)

    def body(x_ref, b_ref, c_ref, dt_ref, dtt_ref, br_ref, bc_ref, ar_ref, ac_ref, d_ref, y_ref, hp_ref, h_sc):
        t = pl.program_id(1)

        @pl.when(t == 0)
        def _():
            h_sc[...] = jnp.zeros_like(h_sc)

        for gg in range(GP):
            wide, narrow = slice(gg * GW, (gg + 1) * GW), slice(gg * SSD_N, (gg + 1) * SSD_N)
            x, bm, cm = x_ref[:, wide], b_ref[:, narrow], c_ref[:, narrow]
            c = _ssd_common(x, bm, cm, dt_ref[gg], dtt_ref[gg], br_ref[gg], bc_ref[gg], ar_ref[gg], ac_ref[gg])
            h = h_sc[gg]
            hp_ref[0, gg] = h
            xdt = c["xdt"]
            pieces = []
            for r in range(SSD_HPG):
                m = c["cb"] * _ssd_lam(c, r)
                pieces.append(_dotb(m, xdt[:, r * SSD_HD:(r + 1) * SSD_HD]))
            y = jnp.concatenate(pieces, axis=1) + c["eb"] * _dotb(cm, h) + x * _dotf(d_ref[gg], c["spread"])
            y_ref[:, wide] = y
            h_sc[gg] = h * c["cdb"] + _dotb(bm, xdt * c["dsb"], TN)

    outs, riding = _call(
        body, name=name, grid=(SSD_G // GP, nc),
        in_specs=[xs, bs, cs, dts, dtts, pr, pc, pr, pc, pr],
        out_specs=[xs, hs],
        out_shape=[jax.ShapeDtypeStruct((s, SSD_DI), F32), jax.ShapeDtypeStruct((nc, SSD_G, SSD_N, GW), F32)],
        scratch_shapes=[pltpu.VMEM((GP, SSD_N, GW), F32)], sem=("parallel", "arbitrary"), rider=rider,
        args=(xbc, xbc, xbc, dtg, dtg_t, bias_r, bias_c, alog_r, alog_c, d_r))
    return outs + [riding]


def _ssd_bwd(xbc, dtg, dtg_t, bias_r, bias_c, alog_r, alog_c, d_r, hprev, dy, *, name, rider=None):
    s = xbc.shape[0]
    nc = s // CHUNK
    xs, bs, cs, dts, dtts, pr, pc, hs = _ssd_specs(nc, True)
    gsum = functools.partial(_group_matrix, GW, SSD_HD)

    def body(x_ref, b_ref, c_ref, dt_ref, dtt_ref, br_ref, bc_ref, ar_ref, ac_ref, d_ref, hp_ref, dy_ref,
             dx_ref, db_ref, dc_ref, ddt_ref, dbias_ref, dalog_ref, dd_ref, dh_sc):
        t = pl.program_id(1)

        @pl.when(t == 0)
        def _():
            dh_sc[...] = jnp.zeros_like(dh_sc)
            dbias_ref[...] = jnp.zeros_like(dbias_ref)
            dalog_ref[...] = jnp.zeros_like(dalog_ref)
            dd_ref[...] = jnp.zeros_like(dd_ref)

        for gg in range(GP):
            wide, narrow = slice(gg * GW, (gg + 1) * GW), slice(gg * SSD_N, (gg + 1) * SSD_N)
            x, bm, cm = x_ref[:, wide], b_ref[:, narrow], c_ref[:, narrow]
            c = _ssd_common(x, bm, cm, dt_ref[gg], dtt_ref[gg], br_ref[gg], bc_ref[gg], ar_ref[gg], ac_ref[gg])
            lanesum = gsum()
            h = hp_ref[0, gg]
            dh = dh_sc[gg]
            dy = dy_ref[:, wide]
            xdt, dsb = c["xdt"], c["dsb"]
            skip = _dotf(d_ref[gg], c["spread"])
            dd_ref[gg] += jnp.sum(_dotf(dy * x, lanesum), axis=0, keepdims=True)
            dacs = _dotf(dy * (c["eb"] * _dotb(cm, h)), lanesum)
            edy = c["eb"] * dy
            dcm = _dotb(edy, h, NT)
            dh_prev = _dotb(cm, edy, TN)
            bdh = _dotb(bm, dh)
            dxdt = dsb * bdh
            dbm = _dotb(dsb * xdt, dh, NT)
            t1 = _dotf(xdt * bdh, lanesum) * c["ds"]
            dacs = dacs - t1
            dlast = (jnp.sum(t1, axis=0, keepdims=True)
                     + jnp.sum(_dotf(dh * h, lanesum), axis=0, keepdims=True) * c["cd"])
            dcb = jnp.zeros((CHUNK, CHUNK), F32)
            pieces = []
            ones8 = jnp.ones((CHUNK, 8), F32)
            head = lax.broadcasted_iota(jnp.int32, (1, 8), 1)
            for r in range(SSD_HPG):
                sl = slice(r * SSD_HD, (r + 1) * SSD_HD)
                lam = _ssd_lam(c, r)
                m = c["cb"] * lam
                dm = _dotb(dy[:, sl], xdt[:, sl], NT)
                dcb = dcb + dm * lam
                gm = dm * m
                dacs = dacs + ((jnp.sum(gm, axis=1, keepdims=True) - _dotf(gm, ones8, TN, pieces=3))
                               * (head == r).astype(F32))
                pieces.append(_dotb(m, dy[:, sl], TN))
            dxdt = dxdt + jnp.concatenate(pieces, axis=1)
            dcm = dcm + _dotb(dcb, bm)
            dbm = dbm + _dotb(dcb, cm, TN)
            dx_ref[:, wide] = dy * skip + dxdt * c["dtb"]
            db_ref[:, narrow] = dbm
            dc_ref[:, narrow] = dcm
            rowid = lax.broadcasted_iota(jnp.int32, (CHUNK, 8), 0)
            dacs = dacs + jnp.where(rowid == CHUNK - 1, dlast, 0.0)
            dda = _dotf(c["triu"], dacs, onehot="a", pieces=3)
            ddt = _dotf(dxdt * x, lanesum) + dda * c["a_r"]
            ddt_raw = ddt * _sigmoid(dt_ref[gg] + br_ref[gg])
            ddt_ref[gg] = ddt_raw
            dbias_ref[gg] += jnp.sum(ddt_raw, axis=0, keepdims=True)
            dalog_ref[gg] += jnp.sum(dda * c["dt"], axis=0, keepdims=True) * c["a_r"]
            dh_sc[gg] = dh_prev + dh * c["cdb"]

    ci = lambda t: nc - 1 - t
    nspec = pl.BlockSpec((CHUNK, GP * SSD_N), lambda g, t: (ci(t), g))
    outs, riding = _call(
        body, name=name, grid=(SSD_G // GP, nc),
        in_specs=[xs, bs, cs, dts, dtts, pr, pc, pr, pc, pr, hs, xs],
        out_specs=[xs, nspec, nspec, dts, pr, pr, pr],
        out_shape=[jax.ShapeDtypeStruct((s, SSD_DI), F32), jax.ShapeDtypeStruct((s, SSD_G * SSD_N), F32),
                   jax.ShapeDtypeStruct((s, SSD_G * SSD_N), F32), jax.ShapeDtypeStruct((SSD_G, s, 8), F32),
                   jax.ShapeDtypeStruct((SSD_G, 1, 8), F32), jax.ShapeDtypeStruct((SSD_G, 1, 8), F32),
                   jax.ShapeDtypeStruct((SSD_G, 1, 8), F32)],
        scratch_shapes=[pltpu.VMEM((GP, SSD_N, GW), F32)], sem=("parallel", "arbitrary"), rider=rider,
        args=(xbc, xbc, xbc, dtg, dtg_t, bias_r, bias_c, alog_r, alog_c, d_r, hprev, dy))
    return outs + [riding]


FOX_PAIRS = FOX_H // 2
FOX_SCALE = FOX_HD ** -0.5
NEG_INF = -jnp.inf


def _fgate_fwd(f_t, b_c, *, name):
    hh, s = f_t.shape
    tb = _pick(s, 512)
    nb = s // tb

    def body(f_ref, b_ref, o_ref, carry):
        t = pl.program_id(0)

        @pl.when(t == 0)
        def _():
            carry[...] = jnp.zeros_like(carry)

        lf = -_softplus(-(f_ref[...] + b_ref[...]))
        row = lax.broadcasted_iota(jnp.int32, (tb, tb), 0)
        col = lax.broadcasted_iota(jnp.int32, (tb, tb), 1)
        cum = _dotf(lf, (row <= col).astype(F32), pieces=3) + carry[:, 0:1]
        o_ref[...] = cum
        carry[:, 0:1] = cum[:, tb - 1:tb]

    return pl.pallas_call(
        body, name=name, grid=(nb,),
        in_specs=[pl.BlockSpec((hh, tb), lambda t: (0, t)), pl.BlockSpec((hh, 1), lambda t: (0, 0))],
        out_specs=pl.BlockSpec((hh, tb), lambda t: (0, t)),
        out_shape=jax.ShapeDtypeStruct((hh, s), F32),
        scratch_shapes=[pltpu.VMEM((hh, LANES), F32)],
        compiler_params=_cp(("arbitrary",)),
    )(f_t, b_c)


def _fgate_bwd(dcum_q_t, dcum_k_t, f_t, b_c, *, name):
    hh, s = f_t.shape
    tb = _pick(s, 512)
    nb = s // tb

    def body(dq_ref, d_ref, f_ref, b_ref, df_ref, db_ref, carry):
        t = pl.program_id(0)

        @pl.when(t == 0)
        def _():
            carry[...] = jnp.zeros_like(carry)
            db_ref[...] = jnp.zeros_like(db_ref)

        d = d_ref[...] + dq_ref[...]
        row = lax.broadcasted_iota(jnp.int32, (tb, tb), 0)
        col = lax.broadcasted_iota(jnp.int32, (tb, tb), 1)
        rev = _dotf(d, (row >= col).astype(F32), pieces=3) + carry[:, 0:1]
        df = rev * _sigmoid(-(f_ref[...] + b_ref[...]))
        df_ref[...] = df
        db_ref[...] += jnp.sum(df, axis=1, keepdims=True)
        carry[:, 0:1] = rev[:, 0:1]

    blk = pl.BlockSpec((hh, tb), lambda t: (0, nb - 1 - t))
    return pl.pallas_call(
        body, name=name, grid=(nb,),
        in_specs=[blk, blk, blk, pl.BlockSpec((hh, 1), lambda t: (0, 0))],
        out_specs=[blk, pl.BlockSpec((hh, 1), lambda t: (0, 0))],
        out_shape=[jax.ShapeDtypeStruct((hh, s), F32), jax.ShapeDtypeStruct((hh, 1), F32)],
        scratch_shapes=[pltpu.VMEM((hh, LANES), F32)],
        compiler_params=_cp(("arbitrary",)),
    )(dcum_q_t, dcum_k_t, f_t, b_c)


def _fox_tile(s):
    return min(512, max(s // 2, 8))


def _tri_tables(nq, kv_major):
    if kv_major:
        pairs = [(i, j) for j in range(nq) for i in range(j, nq)]
    else:
        pairs = [(i, j) for i in range(nq) for j in range(i + 1)]
    return (jnp.asarray([p[0] for p in pairs], jnp.int32), jnp.asarray([p[1] for p in pairs], jnp.int32))


def _lane_tile(col, width):
    return col if width == LANES else jnp.tile(col, (1, width // LANES))


def _flash_fwd(qs, kn, qkvg, ck, *, name, rider=None):
    s = qs.shape[0]
    tt = _fox_tile(s)
    nq = s // tt
    itab, jtab = _tri_tables(nq, kv_major=False)
    v0 = 2 * FOX_D // LANES

    def body(itab_ref, jtab_ref, q_ref, k_ref, v_ref, ck_ref, o_ref, lse_ref, m_sc, l_sc, acc_sc):
        t = pl.program_id(1)
        i, j = itab_ref[t], jtab_ref[t]

        @pl.when(j == 0)
        def _():
            m_sc[...] = jnp.full_like(m_sc, NEG_INF)
            l_sc[...] = jnp.zeros_like(l_sc)
            acc_sc[...] = jnp.zeros_like(acc_sc)

        low = lax.broadcasted_iota(jnp.int32, (tt, LANES), 1) < FOX_HD

        def step(diagonal):
            q2, k2 = q_ref[...], k_ref[...]
            v2 = v_ref[...].astype(BF16)
            alphas, outs = [], []
            for hh in range(2):
                qh = jnp.where(low if hh == 0 else jnp.logical_not(low), q2, jnp.zeros_like(q2))
                sc = lax.dot_general(qh, k2, NT, preferred_element_type=F32) - ck_ref[0][hh:hh + 1, :]
                if diagonal:
                    row = lax.broadcasted_iota(jnp.int32, sc.shape, 0)
                    col = lax.broadcasted_iota(jnp.int32, sc.shape, 1)
                    sc = jnp.where(row >= col, sc, NEG_INF)
                m_prev = m_sc[hh]
                m_new = jnp.maximum(m_prev, jnp.max(sc, axis=1, keepdims=True))
                alpha = jnp.exp(m_prev - m_new)
                p = jnp.exp(sc - _lane_tile(m_new, tt))
                l_sc[hh] = alpha * l_sc[hh] + jnp.sum(p, axis=1, keepdims=True)
                m_sc[hh] = m_new
                alphas.append(alpha)
                outs.append(lax.dot_general(p.astype(BF16), v2, NN, preferred_element_type=F32))
            acc_sc[...] = jnp.where(low, alphas[0], alphas[1]) * acc_sc[...] + jnp.where(low, outs[0], outs[1])

        @pl.when(j < i)
        def _():
            step(False)

        @pl.when(j == i)
        def _():
            step(True)
            o_ref[...] = acc_sc[...] / jnp.where(low, l_sc[0], l_sc[1])
            lse_ref[0] = jnp.concatenate([m_sc[hh][:, 0:1] + jnp.log(l_sc[hh][:, 0:1]) for hh in range(2)], axis=1)

    outs, riding = _call(
        body, name=name, grid=(FOX_PAIRS, int(itab.shape[0])), prefetch=(itab, jtab),
        in_specs=[pl.BlockSpec((tt, LANES), lambda p, t, it, jt: (it[t], p)),
                  pl.BlockSpec((tt, LANES), lambda p, t, it, jt: (jt[t], p)),
                  pl.BlockSpec((tt, LANES), lambda p, t, it, jt: (jt[t], v0 + p)),
                  pl.BlockSpec((1, 2, tt), lambda p, t, it, jt: (p, 0, jt[t]))],
        out_specs=[pl.BlockSpec((tt, LANES), lambda p, t, it, jt: (it[t], p)),
                   pl.BlockSpec((1, tt, 2), lambda p, t, it, jt: (p, it[t], 0))],
        scratch_shapes=[pltpu.VMEM((2, tt, LANES), F32), pltpu.VMEM((2, tt, LANES), F32), pltpu.VMEM((tt, LANES), F32)],
        out_shape=[jax.ShapeDtypeStruct((s, FOX_D), F32), jax.ShapeDtypeStruct((FOX_PAIRS, s, 2), F32)],
        sem=("parallel", "arbitrary"), rider=rider, args=(qs, kn, qkvg, ck))
    return outs + [riding]


def _flash_bwd(qs, kn, qkvg, do, lse_t, delta_t, ck_c, *, name, rider=None):
    s = qs.shape[0]
    tt = _fox_tile(s)
    nq = s // tt
    nl = tt // LANES
    itab, jtab = _tri_tables(nq, kv_major=True)
    nsteps = itab.shape[0]
    v0 = 2 * FOX_D // LANES

    def body(itab_ref, jtab_ref, q_ref, k_ref, v_ref, do_ref, lse_ref, dl_ref, ck_ref,
             dq_ref, dk_ref, dv_ref, dcq_ref, dck_ref, dqt_sc, rs_sc, dk_sc, dv_sc, dc_sc, kt_sc, ckb_sc):
        t = pl.program_id(1)
        i, j = itab_ref[t], jtab_ref[t]

        @pl.when(t == 0)
        def _():
            dqt_sc[...] = jnp.zeros_like(dqt_sc)
            rs_sc[...] = jnp.zeros_like(rs_sc)

        @pl.when(i == j)
        def _():
            dk_sc[...] = jnp.zeros_like(dk_sc)
            dv_sc[...] = jnp.zeros_like(dv_sc)
            dc_sc[...] = jnp.zeros_like(dc_sc)
            kt_sc[...] = k_ref[...].astype(F32).T.astype(BF16)
            for hh in range(2):
                ckb_sc[hh] = jnp.broadcast_to(ck_ref[0][:, hh:hh + 1], (tt, LANES))

        low = lax.broadcasted_iota(jnp.int32, (tt, LANES), 1) < FOX_HD
        top = lax.broadcasted_iota(jnp.int32, (LANES, tt), 0) < FOX_HD

        def step(diagonal):
            q2, k2, kt = q_ref[...], k_ref[...], kt_sc[...]
            v2 = v_ref[...].astype(BF16)
            do2 = do_ref[...].astype(BF16)
            dqs, dks, dvs = [], [], []
            for hh in range(2):
                sel = low if hh == 0 else jnp.logical_not(low)
                qh = jnp.where(sel, q2, jnp.zeros_like(q2))
                doh = jnp.where(sel, do2, jnp.zeros_like(do2))
                st = lax.dot_general(k2, qh, NT, preferred_element_type=F32)
                st = st - _lane_tile(ckb_sc[hh], tt) - lse_ref[0][hh:hh + 1, :]
                if diagonal:
                    key = lax.broadcasted_iota(jnp.int32, st.shape, 0)
                    qry = lax.broadcasted_iota(jnp.int32, st.shape, 1)
                    st = jnp.where(qry >= key, st, NEG_INF)
                pt = jnp.exp(st)
                dpt = lax.dot_general(v2, doh, NT, preferred_element_type=F32)
                dst = pt * (dpt - dl_ref[0][hh:hh + 1, :])
                ptb, dstb = pt.astype(BF16), dst.astype(BF16)
                dvs.append(lax.dot_general(ptb, do2, NN, preferred_element_type=F32))
                dks.append(lax.dot_general(dstb, q2, NN, preferred_element_type=F32))
                dqs.append(lax.dot_general(kt, dstb, NN, preferred_element_type=F32))
                rs_sc[hh, i] += jnp.sum(dst, axis=0, keepdims=True)
                part = dst[:, 0:LANES]
                for b in range(1, nl):
                    part = part + dst[:, b * LANES:(b + 1) * LANES]
                dc_sc[hh] += part
            dv_sc[...] += jnp.where(low, dvs[0], dvs[1])
            dk_sc[...] += jnp.where(low, dks[0], dks[1])
            dqt_sc[i] += jnp.where(top, dqs[0], dqs[1])

        @pl.when(j < i)
        def _():
            step(False)

        @pl.when(j == i)
        def _():
            step(True)

        @pl.when(i == nq - 1)
        def _():
            dk_ref[...] = dk_sc[...]
            dv_ref[...] = dv_sc[...]
            dck_ref[0] = -jnp.concatenate([jnp.sum(dc_sc[hh], axis=1, keepdims=True) for hh in range(2)], axis=1)

        @pl.when(t == nsteps - 1)
        def _():
            for b in range(nq):
                dq_ref[pl.ds(b * tt, tt), :] = dqt_sc[b].T * FOX_SCALE
                dcq_ref[0, :, pl.ds(b * tt, tt)] = jnp.concatenate([rs_sc[hh, b] for hh in range(2)], axis=0)

    qside = pl.BlockSpec((tt, LANES), lambda p, t, it, jt: (it[t], p))
    kside = pl.BlockSpec((tt, LANES), lambda p, t, it, jt: (jt[t], p))
    qstat = pl.BlockSpec((1, 2, tt), lambda p, t, it, jt: (p, 0, it[t]))
    kstat = pl.BlockSpec((1, tt, 2), lambda p, t, it, jt: (p, jt[t], 0))
    outs, riding = _call(
        body, name=name, grid=(FOX_PAIRS, nsteps), prefetch=(itab, jtab),
        in_specs=[qside, kside, pl.BlockSpec((tt, LANES), lambda p, t, it, jt: (jt[t], v0 + p)), qside, qstat, qstat, kstat],
        out_specs=[pl.BlockSpec((s, LANES), lambda p, t, it, jt: (0, p)), kside, kside,
                   pl.BlockSpec((1, 2, s), lambda p, t, it, jt: (p, 0, 0)), kstat],
        scratch_shapes=[pltpu.VMEM((nq, LANES, tt), F32), pltpu.VMEM((2, nq, 1, tt), F32), pltpu.VMEM((tt, LANES), F32),
                        pltpu.VMEM((tt, LANES), F32), pltpu.VMEM((2, tt, LANES), F32), pltpu.VMEM((LANES, tt), BF16),
                        pltpu.VMEM((2, tt, LANES), F32)],
        out_shape=[jax.ShapeDtypeStruct((s, FOX_D), F32), jax.ShapeDtypeStruct((s, FOX_D), F32),
                   jax.ShapeDtypeStruct((s, FOX_D), F32), jax.ShapeDtypeStruct((FOX_PAIRS, 2, s), F32),
                   jax.ShapeDtypeStruct((FOX_PAIRS, s, 2), F32)],
        sem=("parallel", "arbitrary"), rider=rider, args=(qs, kn, qkvg, do, lse_t, delta_t, ck_c))
    return outs + [riding]


def _ogate_fwd(o, qkvg, *, name):
    s = o.shape[0]
    tr = _pick(s, 1024, 8)

    def body(o_ref, g_ref, out_ref):
        out_ref[...] = (o_ref[...] * _sigmoid(g_ref[...])).astype(BF16)

    tile = pl.BlockSpec((tr, FOX_D), lambda i: (i, 0))
    return pl.pallas_call(
        body, name=name, grid=(s // tr,), in_specs=[tile, pl.BlockSpec((tr, FOX_D), lambda i: (i, 3))],
        out_specs=tile, out_shape=jax.ShapeDtypeStruct((s, FOX_D), BF16), compiler_params=_cp(("parallel",)),
    )(o, qkvg)


def _ogate_bwd(dog, o, qkvg, *, name):
    s = o.shape[0]
    tr = _pick(s, 1024, 8)

    def body(dog_ref, o_ref, g_ref, do_ref, dg_ref, dl_ref):
        sg = _sigmoid(g_ref[...])
        ov = o_ref[...]
        dog_v = dog_ref[...]
        do = dog_v * sg
        do_ref[...] = do
        dg_ref[...] = (dog_v * ov * sg * (1.0 - sg)).astype(BF16)
        dl_ref[...] = _dotf(do * ov, _group_matrix(FOX_D, FOX_HD))

    tile = pl.BlockSpec((tr, FOX_D), lambda i: (i, 0))
    return pl.pallas_call(
        body, name=name, grid=(s // tr,), in_specs=[tile, tile, pl.BlockSpec((tr, FOX_D), lambda i: (i, 3))],
        out_specs=[tile, tile, pl.BlockSpec((tr, FOX_H), lambda i: (i, 0))],
        out_shape=[jax.ShapeDtypeStruct((s, FOX_D), F32), jax.ShapeDtypeStruct((s, FOX_D), BF16),
                   jax.ShapeDtypeStruct((s, FOX_H), F32)],
        compiler_params=_cp(("parallel",)),
    )(dog, o, qkvg)


def _loss_head(h, g, target, *, name):
    s, d = h.shape
    tr = _pick(s, 1024, 8)

    def body(h_ref, g_ref, t_ref, loss_ref, dh_ref, dg_ref):
        i = pl.program_id(0)
        x = h_ref[...]
        gv = g_ref[...]
        r = lax.rsqrt(jnp.mean(x * x, axis=-1, keepdims=True) + EPS)
        xh = x * r
        err = xh * gv - t_ref[...]
        part = 0.5 * jnp.sum(jnp.sum(err * err, axis=1, keepdims=True) * (1.0 / d), axis=0, keepdims=True)
        dy = err * (1.0 / d)
        dyg = dy * gv
        dh_ref[...] = r * (dyg - xh * jnp.mean(dyg * xh, axis=-1, keepdims=True))
        dgp = jnp.sum(dy * xh, axis=0, keepdims=True)

        @pl.when(i == 0)
        def _():
            loss_ref[...] = jnp.zeros_like(loss_ref) + part
            dg_ref[...] = dgp

        @pl.when(i > 0)
        def _():
            loss_ref[...] += part
            dg_ref[...] += dgp

    tile = pl.BlockSpec((tr, d), lambda i: (i, 0))
    vec = pl.BlockSpec((1, d), lambda i: (0, 0))
    return pl.pallas_call(
        body, name=name, grid=(s // tr,), in_specs=[tile, vec, tile],
        out_specs=[pl.BlockSpec((1, LANES), lambda i: (0, 0)), tile, vec],
        out_shape=[jax.ShapeDtypeStruct((1, LANES), F32), jax.ShapeDtypeStruct((s, d), F32),
                   jax.ShapeDtypeStruct((1, d), F32)],
        compiler_params=_cp(("arbitrary",)),
    )(h, g, target)


def _adamw(w, g, m, v, *, name):
    rows, cols = w.shape
    tr = _pick(rows, 256, 8)
    c1 = 1.0 - ADAM_B1 ** ADAM_STEP
    c2 = 1.0 - ADAM_B2 ** ADAM_STEP

    def body(w_ref, g_ref, m_ref, v_ref, d_ref, nm_ref, nv_ref):
        gv = g_ref[...]
        nm = ADAM_B1 * m_ref[...] + (1.0 - ADAM_B1) * gv
        nv = ADAM_B2 * v_ref[...] + (1.0 - ADAM_B2) * (gv * gv)
        d_ref[...] = -ADAM_LR * ((nm / c1) / (jnp.sqrt(nv / c2) + ADAM_EPS) + ADAM_WD * w_ref[...])
        nm_ref[...] = nm
        nv_ref[...] = nv

    tile = pl.BlockSpec((tr, cols), lambda i: (i, 0))
    shp = jax.ShapeDtypeStruct((rows, cols), F32)
    return pl.pallas_call(
        body, name=name, grid=(rows // tr,), in_specs=[tile] * 4, out_specs=[tile] * 3, out_shape=[shp] * 3,
        compiler_params=_cp(("parallel",)),
    )(w, g, m, v)


ANY = pl.BlockSpec(memory_space=pl.ANY)
N_DEV = 8


def _coords():
    return lax.axis_index("x"), lax.axis_index("y"), lax.axis_index("c")


def _other_chips(x, y):
    return [(1 - x, y), (x, 1 - y), (1 - x, 1 - y)]


def _allgather_small(buf, *, name, with_sum):
    rows = buf.shape[0]

    def body(*refs):
        if with_sum:
            x_ref, out_ref, sum_ref, send_sems, recv_sems = refs
        else:
            x_ref, out_ref, send_sems, recv_sems = refs
        x, y, c = _coords()
        me = 4 * x + 2 * y + c
        out_ref[me] = x_ref[...]
        copies = []
        for rel in range(1, N_DEV):
            px = (1 - x) if rel & 4 else x
            py = (1 - y) if rel & 2 else y
            pc = (1 - c) if rel & 1 else c
            cp = pltpu.make_async_remote_copy(
                src_ref=x_ref, dst_ref=out_ref.at[me], send_sem=send_sems.at[rel - 1], recv_sem=recv_sems.at[rel - 1],
                device_id=(px, py, pc), device_id_type=MESH)
            cp.start()
            copies.append(cp)
        for cp in copies:
            cp.wait()
        if with_sum:
            acc = out_ref[0]
            for k in range(1, N_DEV):
                acc = acc + out_ref[k]
            sum_ref[...] = acc

    slots = jax.ShapeDtypeStruct((N_DEV, rows, LANES), F32)
    vm = pl.BlockSpec(memory_space=pltpu.VMEM)
    out_shape = [slots, jax.ShapeDtypeStruct((rows, LANES), F32)] if with_sum else [slots]
    return pl.pallas_call(
        body, name=name, in_specs=[vm], out_specs=[vm] * len(out_shape), out_shape=out_shape,
        scratch_shapes=[pltpu.SemaphoreType.DMA((N_DEV - 1,)), pltpu.SemaphoreType.DMA((N_DEV - 1,))],
    )(buf)


class _Gather:
    per_array = 6

    def __init__(self, arrays):
        self.arrays = list(arrays)

    def out_shapes(self):
        return [jax.ShapeDtypeStruct((4,) + a.shape, a.dtype) for a in self.arrays]

    @staticmethod
    def _ici(ins, outs, send_sems, recv_sems, t, j, px, py, c, slot):
        return pltpu.make_async_remote_copy(
            src_ref=ins[t].at[c], dst_ref=outs[t].at[slot, c], send_sem=send_sems.at[6 * t + j],
            recv_sem=recv_sems.at[6 * t + j], device_id=(px, py, c), device_id_type=MESH)

    @staticmethod
    def _d2d(outs, send_sems, recv_sems, t, j, kj, half, sibling):
        return pltpu.make_async_remote_copy(
            src_ref=outs[t].at[kj, half], dst_ref=outs[t].at[kj, half], send_sem=send_sems.at[6 * t + 3 + j],
            recv_sem=recv_sems.at[6 * t + 3 + j], device_id=sibling, device_id_type=MESH)

    def start(self, ins, outs, send_sems, recv_sems):
        x, y, c = _coords()
        for t in range(len(ins)):
            for j, (px, py) in enumerate(_other_chips(x, y)):
                self._ici(ins, outs, send_sems, recv_sems, t, j, px, py, c, 2 * x + y).start()

    def finish(self, ins, outs, send_sems, recv_sems):
        x, y, c = _coords()
        chips = _other_chips(x, y)
        sibling = (x, y, 1 - c)
        started = []
        for t in range(len(ins)):
            for j, (px, py) in enumerate(chips):
                ici = self._ici(ins, outs, send_sems, recv_sems, t, j, px, py, c, 2 * px + py)
                ici.wait_recv()
                fwd = self._d2d(outs, send_sems, recv_sems, t, j, 2 * px + py, c, sibling)
                fwd.start()
                started += [ici, fwd]
        for t in range(len(ins)):
            for j, (px, py) in enumerate(chips):
                self._d2d(outs, send_sems, recv_sems, t, j, 2 * px + py, 1 - c, sibling).wait_recv()
        for cp in started:
            cp.wait_send()


class _Exchange:
    per_array = 7

    def __init__(self, arrays):
        self.arrays = list(arrays)

    def out_shapes(self):
        return [jax.ShapeDtypeStruct((7,) + a.shape[2:], a.dtype) for a in self.arrays]

    @staticmethod
    def _copies(ins, outs, send_sems, recv_sems):
        x, y, c = _coords()
        for t in range(len(ins)):
            for rel in range(1, N_DEV):
                px = (1 - x) if rel & 4 else x
                py = (1 - y) if rel & 2 else y
                pc = (1 - c) if rel & 1 else c
                yield pltpu.make_async_remote_copy(
                    src_ref=ins[t].at[2 * px + py, pc], dst_ref=outs[t].at[rel - 1], send_sem=send_sems.at[7 * t + rel - 1],
                    recv_sem=recv_sems.at[7 * t + rel - 1], device_id=(px, py, pc), device_id_type=MESH)

    def start(self, ins, outs, send_sems, recv_sems):
        for cp in self._copies(ins, outs, send_sems, recv_sems):
            cp.start()

    def finish(self, ins, outs, send_sems, recv_sems):
        for cp in self._copies(ins, outs, send_sems, recv_sems):
            cp.wait()


def _call(body, *, name, grid, in_specs, out_specs, out_shape, scratch_shapes, args, sem, rider=None, prefetch=()):
    n_in, n_out, n_pre = len(in_specs), len(out_specs), len(prefetch)
    n_c = len(rider.arrays) if rider is not None else 0

    def wrapped(*refs):
        pre, rest = refs[:n_pre], refs[n_pre:]
        ins, cins = rest[:n_in], rest[n_in:n_in + n_c]
        outs = rest[n_in + n_c:n_in + n_c + n_out]
        couts = rest[n_in + n_c + n_out:n_in + 2 * n_c + n_out]
        scratch = rest[n_in + 2 * n_c + n_out:]
        if rider is None:
            body(*pre, *ins, *outs, *scratch)
            return
        send_sems, recv_sems = scratch[-2:]
        ids = [pl.program_id(a) for a in range(len(grid))]
        first = functools.reduce(jnp.logical_and, [i == 0 for i in ids])
        last = functools.reduce(jnp.logical_and, [i == g - 1 for i, g in zip(ids, grid)])

        @pl.when(first)
        def _():
            rider.start(cins, couts, send_sems, recv_sems)

        body(*pre, *ins, *outs, *scratch[:-2])

        @pl.when(last)
        def _():
            rider.finish(cins, couts, send_sems, recv_sems)

    if rider is not None:
        nsem = rider.per_array * n_c
        in_specs = list(in_specs) + [ANY] * n_c
        out_specs = list(out_specs) + [ANY] * n_c
        out_shape = list(out_shape) + rider.out_shapes()
        scratch_shapes = list(scratch_shapes) + [pltpu.SemaphoreType.DMA((nsem,)), pltpu.SemaphoreType.DMA((nsem,))]
        args = list(args) + rider.arrays
        sem = ("arbitrary",) * len(grid)
    if n_pre:
        res = pl.pallas_call(
            wrapped, name=name, out_shape=out_shape, compiler_params=_cp(sem),
            grid_spec=pltpu.PrefetchScalarGridSpec(num_scalar_prefetch=n_pre, grid=grid, in_specs=in_specs,
                                                   out_specs=out_specs, scratch_shapes=scratch_shapes),
        )(*prefetch, *args)
    else:
        res = pl.pallas_call(
            wrapped, name=name, grid=grid, in_specs=in_specs, out_specs=out_specs, out_shape=out_shape,
            scratch_shapes=scratch_shapes, compiler_params=_cp(sem),
        )(*args)
    return list(res[:n_out]), list(res[n_out:])


def _run_rider(rider, *, name):
    n = len(rider.arrays)

    def body(*refs):
        ins, outs = refs[:n], refs[n:2 * n]
        send_sems, recv_sems = refs[2 * n:]
        rider.start(ins, outs, send_sems, recv_sems)
        rider.finish(ins, outs, send_sems, recv_sems)

    nsem = rider.per_array * n
    return pl.pallas_call(
        body, name=name, in_specs=[ANY] * n, out_specs=[ANY] * n, out_shape=rider.out_shapes(),
        scratch_shapes=[pltpu.SemaphoreType.DMA((nsem,)), pltpu.SemaphoreType.DMA((nsem,))],
    )(*rider.arrays)


def _sibling_swap(arrs, *, name):
    n = len(arrs)

    def body(*refs):
        ins, outs = refs[:n], refs[n:2 * n]
        send_sems, recv_sems = refs[2 * n:]
        x, y, c = _coords()
        copies = []
        for t in range(n):
            cp = pltpu.make_async_remote_copy(
                src_ref=ins[t], dst_ref=outs[t], send_sem=send_sems.at[t], recv_sem=recv_sems.at[t],
                device_id=(x, y, 1 - c), device_id_type=MESH)
            cp.start()
            copies.append(cp)
        for cp in copies:
            cp.wait()

    return pl.pallas_call(
        body, name=name, in_specs=[ANY] * n, out_specs=[ANY] * n,
        out_shape=[jax.ShapeDtypeStruct(a.shape, a.dtype) for a in arrs],
        scratch_shapes=[pltpu.SemaphoreType.DMA((n,)), pltpu.SemaphoreType.DMA((n,))],
    )(*arrs)


def _add_selected(stack, others, sel, *, name):
    _, m, cols = stack.shape
    q = others.shape[0]
    tr = _pick(m, 256, 16)

    def body(sel_ref, s_ref, o_ref, out_ref):
        acc = s_ref[0].astype(F32)
        for i in range(q):
            acc = acc + o_ref[i].astype(F32)
        out_ref[...] = acc

    return pl.pallas_call(
        body, name=name,
        grid_spec=pltpu.PrefetchScalarGridSpec(
            num_scalar_prefetch=1, grid=(m // tr,),
            in_specs=[pl.BlockSpec((1, tr, cols), lambda i, sel_ref: (sel_ref[0], i, 0)),
                      pl.BlockSpec((q, tr, cols), lambda i, sel_ref: (0, i, 0))],
            out_specs=pl.BlockSpec((tr, cols), lambda i, sel_ref: (i, 0))),
        out_shape=jax.ShapeDtypeStruct((m, cols), F32),
        compiler_params=_cp(("parallel",)),
    )(sel, stack, others)


BIG = ("ssd_w_in", "ssd_w_out", "fox_w_in", "fox_w_out", "ffn_w_up", "ffn_w_down")
COL_SHARDED = ("ssd_w_in", "fox_w_in", "ffn_w_up")
SMALL = (("mix_norm_g", (4, 1024)), ("ffn_norm_g", (4, 1024)), ("ssd_conv_w", (2, 4, 3072)), ("ssd_conv_b", (2, 3072)),
         ("ssd_dt_bias", (2, 32)), ("ssd_a_log", (2, 32)), ("ssd_d", (2, 32)), ("ssd_norm_g", (2, 2048)),
         ("fox_b_f", (2, 16)), ("fox_q_norm_g", (2, 64)), ("fox_k_norm_g", (2, 64)), ("ffn_conv_w", (4, 3, 2816)),
         ("ffn_conv_b", (4, 2816)), ("final_norm_g", (1024,)), ("loss", (1,)))
NAMES = ("mix_norm_g", "ffn_norm_g", "ssd_w_in", "ssd_conv_w", "ssd_conv_b", "ssd_dt_bias", "ssd_a_log", "ssd_d",
         "ssd_norm_g", "ssd_w_out", "fox_w_in", "fox_b_f", "fox_q_norm_g", "fox_k_norm_g", "fox_w_out", "ffn_w_up",
         "ffn_conv_w", "ffn_conv_b", "ffn_w_down", "final_norm_g")


def _pack(parts):
    flat = jnp.concatenate([jnp.reshape(p, (-1,)).astype(F32) for p in parts])
    rows = -(-flat.shape[0] // (8 * LANES)) * 8
    return jnp.pad(flat, (0, rows * LANES - flat.shape[0])).reshape(rows, LANES)


def _unpack(buf, shapes):
    flat = buf.reshape(-1)
    out, off = [], 0
    for shp in shapes:
        size = 1
        for d in shp:
            size *= d
        out.append(flat[off:off + size].reshape(shp))
        off += size
    return out


def _pad_lanes(a):
    return jnp.pad(a, ((0, 0), (0, LANES - a.shape[1])))


def _pad8(w):
    return jnp.pad(w, ((0, 8 - w.shape[0]), (0, 0)))


def _ssd_forward(h, p, name, rider=None):
    s = h.shape[0]
    hn = _rms_fwd(h, p["mix_g"], gw=D_MODEL, ncol=1, name=f"{name}_norm")
    zx = _matmul(hn, p["w_zx"], mode="nn", name=f"{name}_proj")
    dtp = _matmul(hn, p["w_dt"], mode="nn", name=f"{name}_proj_dt")
    xbc, _ = _conv_fwd(zx, p["conv_w8"], p["conv_b"], kw=SSD_K, width=SSD_CONV_DIM, u_col0=SSD_DI, name=f"{name}_conv")
    dt3 = dtp[:, :SSD_H].reshape(s, SSD_G, SSD_HPG)
    dtg, dtg_t = jnp.transpose(dt3, (1, 0, 2)), jnp.transpose(dt3, (1, 2, 0))
    sp = (p["bias_r"], p["bias_c"], p["alog_r"], p["alog_c"], p["d_r"])
    y, hprev, riding = _ssd_fwd(xbc, dtg, dtg_t, *sp, name=f"{name}_scan", rider=rider)
    y2 = _rms_fwd(y, p["norm_g"], gw=SSD_DI // SSD_G, ncol=SSD_G, z=zx, name=f"{name}_gnorm")
    out = _matmul(y2, p["w_out"], mode="nn", add=h, name=f"{name}_out")
    return out, dict(h=h, hn=hn, zx=zx, xbc=xbc, dtg=dtg, dtg_t=dtg_t, y=y, hprev=hprev, y2=y2), riding


def _ssd_backward(dh1, p, a, name, ride=()):
    s = dh1.shape[0]
    g = {}
    dy2 = _matmul(dh1, p["w_out"], mode="nt", name=f"{name}_out_dx")
    g["w_out"] = _matmul(a["y2"], dh1, mode="tn", out_dtype=BF16, name=f"{name}_out_dw")
    rider = _Exchange(list(ride) + [_to_slabs(g["w_out"], False)])
    dy, dz, g["norm_g"] = _rms_bwd(a["y"], p["norm_g"], dy2, gw=SSD_DI // SSD_G, ncol=SSD_G, z=a["zx"], name=f"{name}_gnorm_b")
    sp = (p["bias_r"], p["bias_c"], p["alog_r"], p["alog_c"], p["d_r"])
    dx, dbm, dcm, ddt, g["dt_bias"], g["a_log"], g["d"], riding = _ssd_bwd(
        a["xbc"], a["dtg"], a["dtg_t"], *sp, a["hprev"], dy, name=f"{name}_scan_b", rider=rider)
    dxbc, dwb, _ = _conv_bwd(a["zx"], p["conv_w8"], p["conv_b"], [dx, dbm, dcm], kw=SSD_K, width=SSD_CONV_DIM,
                             u_col0=SSD_DI, name=f"{name}_conv_b")
    g["conv_w"], g["conv_b"] = dwb[:SSD_K], dwb[7]
    dzx = jnp.concatenate([dz.astype(BF16), dxbc], axis=1)
    ddtp = _pad_lanes(jnp.transpose(ddt, (1, 0, 2)).reshape(s, SSD_H))
    dhn = _matmul(dzx, p["w_zx"], mode="nt", name=f"{name}_proj_dx")
    dhn = _matmul(ddtp, p["w_dt"], mode="nt", add=dhn, name=f"{name}_proj_dt_dx")
    dw_zx = _matmul(a["hn"], dzx, mode="tn", out_dtype=BF16, name=f"{name}_proj_dw")
    dw_dt = _matmul(a["hn"], ddtp, mode="tn", out_dtype=BF16, name=f"{name}_proj_dt_dw")
    g["w_in"] = jnp.concatenate([dw_zx, dw_dt[:, :SSD_H]], axis=1)
    dh, g["mix_g"] = _rms_bwd(a["h"], p["mix_g"], dhn, gw=D_MODEL, ncol=1, add=dh1, name=f"{name}_norm_b")
    g["w_out_received"] = riding[-1]
    return dh, g, riding[:-1]


def _fox_forward(h, p, name, rider=None):
    s = h.shape[0]
    hn = _rms_fwd(h, p["mix_g"], gw=D_MODEL, ncol=1, name=f"{name}_norm")
    qkvg = _matmul(hn, p["w_qkvg"], mode="nn", name=f"{name}_proj")
    fp = _matmul(hn, p["w_f"], mode="nn", name=f"{name}_proj_f")
    qs = _rms_fwd(qkvg, p["gq"] * FOX_SCALE, gw=FOX_D, ncol=1, x_col0=0, sub=FOX_HD, name=f"{name}_qnorm")
    kn = _rms_fwd(qkvg, p["gk"], gw=FOX_D, ncol=1, x_col0=1, sub=FOX_HD, name=f"{name}_knorm")
    f_t = jnp.transpose(fp[:, :FOX_H])
    cum_t = _fgate_fwd(f_t, p["b_f"], name=f"{name}_fgate")
    ck = cum_t.reshape(FOX_PAIRS, 2, s)
    o, lse, riding = _flash_fwd(qs, kn, qkvg, ck, name=f"{name}_attn", rider=rider)
    og = _ogate_fwd(o, qkvg, name=f"{name}_ogate")
    out = _matmul(og, p["w_out"], mode="nn", add=h, name=f"{name}_out")
    return out, dict(h=h, hn=hn, qkvg=qkvg, qs=qs, kn=kn, f_t=f_t, ck=ck, o=o, lse=lse, og=og), riding


def _fox_backward(dh1, p, a, name, ride=()):
    s = dh1.shape[0]
    g = {}
    dog = _matmul(dh1, p["w_out"], mode="nt", name=f"{name}_out_dx")
    g["w_out"] = _matmul(a["og"], dh1, mode="tn", out_dtype=BF16, name=f"{name}_out_dw")
    rider = _Exchange(list(ride) + [_to_slabs(g["w_out"], False)])
    do, dgate, delta = _ogate_bwd(dog, a["o"], a["qkvg"], name=f"{name}_ogate_b")
    swap = lambda v: jnp.transpose(v, (0, 2, 1))
    dl_t = jnp.transpose(delta.reshape(s, FOX_PAIRS, 2), (1, 2, 0))
    dq, dk, dv, dcq, dck, riding = _flash_bwd(a["qs"], a["kn"], a["qkvg"], do, swap(a["lse"]), dl_t, swap(a["ck"]),
                                              name=f"{name}_attn_b", rider=rider)
    dq_raw, dgq = _rms_bwd(a["qkvg"], p["gq"], dq, gw=FOX_D, ncol=1, x_col0=0, sub=FOX_HD, dx_dtype=BF16, name=f"{name}_qnorm_b")
    dk_raw, dgk = _rms_bwd(a["qkvg"], p["gk"], dk, gw=FOX_D, ncol=1, x_col0=1, sub=FOX_HD, dx_dtype=BF16, name=f"{name}_knorm_b")
    g["gq"] = dgq.reshape(FOX_H, FOX_HD).sum(axis=0)
    g["gk"] = dgk.reshape(FOX_H, FOX_HD).sum(axis=0)
    df_t, dbf = _fgate_bwd(dcq.reshape(FOX_H, s), swap(dck).reshape(FOX_H, s), a["f_t"], p["b_f"], name=f"{name}_fgate_b")
    g["b_f"] = dbf[:, 0]
    dproj = jnp.concatenate([dq_raw, dk_raw, dv.astype(BF16), dgate], axis=1)
    dfp = _pad_lanes(jnp.transpose(df_t))
    dhn = _matmul(dproj, p["w_qkvg"], mode="nt", name=f"{name}_proj_dx")
    dhn = _matmul(dfp, p["w_f"], mode="nt", add=dhn, name=f"{name}_proj_f_dx")
    dw_qkvg = _matmul(a["hn"], dproj, mode="tn", out_dtype=BF16, name=f"{name}_proj_dw")
    dw_f = _matmul(a["hn"], dfp, mode="tn", out_dtype=BF16, name=f"{name}_proj_f_dw")
    g["w_in"] = jnp.concatenate([dw_qkvg, dw_f[:, :FOX_H]], axis=1)
    dh, g["mix_g"] = _rms_bwd(a["h"], p["mix_g"], dhn, gw=D_MODEL, ncol=1, add=dh1, name=f"{name}_norm_b")
    g["w_out_received"] = riding[-1]
    return dh, g, riding[:-1]


def _ffn_forward(h, p, name, rider=None):
    hn = _rms_fwd(h, p["ffn_g"], gw=D_MODEL, ncol=1, name=f"{name}_norm")
    u = _matmul(hn, p["w_up"], mode="nn", name=f"{name}_up")
    act, riding = _conv_fwd(u, p["conv_w8"], p["conv_b"], kw=FFN_K, width=D_FF, u_col0=0, mul_col0=D_FF, out_dtype=BF16,
                            name=f"{name}_glu", rider=rider)
    out = _matmul(act, p["w_down"], mode="nn", add=h, name=f"{name}_down")
    return out, dict(h=h, hn=hn, u=u, act=act), riding


def _ffn_backward(dh2, p, a, name, ride=()):
    g = {}
    dact = _matmul(dh2, p["w_down"], mode="nt", name=f"{name}_down_dx")
    g["w_down"] = _matmul(a["act"], dh2, mode="tn", out_dtype=BF16, name=f"{name}_down_dw")
    du, dwb, riding = _conv_bwd(a["u"], p["conv_w8"], p["conv_b"], dact, kw=FFN_K, width=D_FF, u_col0=0, mul_col0=D_FF,
                                name=f"{name}_glu_b", rider=_Exchange(list(ride)) if ride else None)
    g["conv_w"], g["conv_b"] = dwb[:FFN_K], dwb[7]
    dhn = _matmul(du, p["w_up"], mode="nt", name=f"{name}_up_dx")
    g["w_up"] = _matmul(a["hn"], du, mode="tn", out_dtype=BF16, name=f"{name}_up_dw")
    dh, g["ffn_g"] = _rms_bwd(a["h"], p["ffn_g"], dhn, gw=D_MODEL, ncol=1, add=dh2, name=f"{name}_norm_b")
    return dh, g, riding


def _to_slabs(dw, col_sharded):
    rows, cols = dw.shape
    if col_sharded:
        return jnp.transpose(dw.reshape(rows, 4, cols // 4), (1, 0, 2)).reshape(4, 2, rows // 2, cols // 4)
    return dw.reshape(4, 2, rows // 8, cols)


def kernel(x, mix_norm_g, ffn_norm_g, ssd_w_in, ssd_conv_w, ssd_conv_b, ssd_dt_bias, ssd_a_log, ssd_d, ssd_norm_g, ssd_w_out, fox_w_in, fox_b_f, fox_q_norm_g, fox_k_norm_g, fox_w_out, ffn_w_up, ffn_conv_w, ffn_conv_b, ffn_w_down, final_norm_g, loss_target, m_mix_norm_g, m_ffn_norm_g, m_ssd_w_in, m_ssd_conv_w, m_ssd_conv_b, m_ssd_dt_bias, m_ssd_a_log, m_ssd_d, m_ssd_norm_g, m_ssd_w_out, m_fox_w_in, m_fox_b_f, m_fox_q_norm_g, m_fox_k_norm_g, m_fox_w_out, m_ffn_w_up, m_ffn_conv_w, m_ffn_conv_b, m_ffn_w_down, m_final_norm_g, v_mix_norm_g, v_ffn_norm_g, v_ssd_w_in, v_ssd_conv_w, v_ssd_conv_b, v_ssd_dt_bias, v_ssd_a_log, v_ssd_d, v_ssd_norm_g, v_ssd_w_out, v_fox_w_in, v_fox_b_f, v_fox_q_norm_g, v_fox_k_norm_g, v_fox_w_out, v_ffn_w_up, v_ffn_conv_w, v_ffn_conv_b, v_ffn_w_down, v_final_norm_g):
    w = dict(mix_norm_g=mix_norm_g, ffn_norm_g=ffn_norm_g, ssd_w_in=ssd_w_in, ssd_conv_w=ssd_conv_w, ssd_conv_b=ssd_conv_b,
             ssd_dt_bias=ssd_dt_bias, ssd_a_log=ssd_a_log, ssd_d=ssd_d, ssd_norm_g=ssd_norm_g, ssd_w_out=ssd_w_out,
             fox_w_in=fox_w_in, fox_b_f=fox_b_f, fox_q_norm_g=fox_q_norm_g, fox_k_norm_g=fox_k_norm_g, fox_w_out=fox_w_out,
             ffn_w_up=ffn_w_up, ffn_conv_w=ffn_conv_w, ffn_conv_b=ffn_conv_b, ffn_w_down=ffn_w_down, final_norm_g=final_norm_g)
    m_in = dict(zip(NAMES, (m_mix_norm_g, m_ffn_norm_g, m_ssd_w_in, m_ssd_conv_w, m_ssd_conv_b, m_ssd_dt_bias, m_ssd_a_log,
                            m_ssd_d, m_ssd_norm_g, m_ssd_w_out, m_fox_w_in, m_fox_b_f, m_fox_q_norm_g, m_fox_k_norm_g,
                            m_fox_w_out, m_ffn_w_up, m_ffn_conv_w, m_ffn_conv_b, m_ffn_w_down, m_final_norm_g)))
    v_in = dict(zip(NAMES, (v_mix_norm_g, v_ffn_norm_g, v_ssd_w_in, v_ssd_conv_w, v_ssd_conv_b, v_ssd_dt_bias, v_ssd_a_log,
                            v_ssd_d, v_ssd_norm_g, v_ssd_w_out, v_fox_w_in, v_fox_b_f, v_fox_q_norm_g, v_fox_k_norm_g,
                            v_fox_w_out, v_ffn_w_up, v_ffn_conv_w, v_ffn_conv_b, v_ffn_w_down, v_final_norm_g)))
    cx, cy, cc = _coords()
    chip = 2 * cx + cy
    h = x[0]
    target = loss_target[0]

    conv_shapes = [ssd_conv_w.shape, ffn_conv_w.shape]
    slots = _allgather_small(_pack([ssd_conv_w, ffn_conv_w]), name="gather_conv_w", with_sum=False)[0]
    per_chip = [_unpack(slots[2 * q], conv_shapes) for q in range(4)]
    ssd_conv_full = jnp.concatenate([pc[0] for pc in per_chip], axis=2)
    ffn_conv_full = jnp.concatenate([pc[1] for pc in per_chip], axis=2)
    low = {n: w[n].astype(BF16) for n in BIG}
    sub_weights = dict(ssd=("ssd_w_in", "ssd_w_out"), fox=("fox_w_in", "fox_w_out"), ffn=("ffn_w_up", "ffn_w_down"))

    def shards_of(kind, idx):
        return [low[n][idx].reshape(2, low[n].shape[1] // 2, low[n].shape[2]) for n in sub_weights[kind]]

    def assemble(kind, idx, gathered):
        full = []
        for n, own, gth in zip(sub_weights[kind], shards_of(kind, idx), gathered):
            gth = lax.dynamic_update_slice(gth, own[None], (chip, 0, 0, 0))
            _, _, half, cols = gth.shape
            if n in COL_SHARDED:
                full.append(jnp.transpose(gth.reshape(4, 2 * half, cols), (1, 0, 2)).reshape(2 * half, 4 * cols))
            else:
                full.append(gth.reshape(8 * half, cols))
        return full

    def ssd_params(j, i, weights):
        w_in, w_out = weights
        g3 = lambda v: v.reshape(SSD_G, 1, SSD_HPG)
        g3c = lambda v: v.reshape(SSD_G, SSD_HPG, 1)
        return dict(mix_g=mix_norm_g[i][None], w_zx=w_in[:, :SSD_ZX], w_dt=_pad_lanes(w_in[:, SSD_ZX:]),
                    conv_w8=_pad8(ssd_conv_full[j]), conv_b=ssd_conv_b[j][None], bias_r=g3(ssd_dt_bias[j]),
                    bias_c=g3c(ssd_dt_bias[j]), alog_r=g3(ssd_a_log[j]), alog_c=g3c(ssd_a_log[j]), d_r=g3(ssd_d[j]),
                    norm_g=ssd_norm_g[j][None], w_out=w_out)

    def fox_params(j, i, weights):
        w_in, w_out = weights
        return dict(mix_g=mix_norm_g[i][None], w_qkvg=w_in[:, :4 * FOX_D], w_f=_pad_lanes(w_in[:, 4 * FOX_D:]),
                    gq=jnp.tile(fox_q_norm_g[j], FOX_H)[None], gk=jnp.tile(fox_k_norm_g[j], FOX_H)[None],
                    b_f=fox_b_f[j][:, None], w_out=w_out)

    def ffn_params(i, weights):
        w_up, w_down = weights
        return dict(ffn_g=ffn_norm_g[i][None], w_up=w_up, conv_w8=_pad8(ffn_conv_full[i]), conv_b=ffn_conv_b[i][None],
                    w_down=w_down)

    order = [("ssd", 0), ("ffn", 0), ("fox", 0), ("ffn", 1), ("ssd", 1), ("ffn", 2), ("fox", 1), ("ffn", 3)]
    fetch = {("ssd", 0): [("ffn", 0)], ("ffn", 0): [("fox", 0)], ("fox", 0): [("ffn", 1), ("ssd", 1), ("ffn", 2)],
             ("ssd", 1): [("fox", 1)], ("fox", 1): [("ffn", 3)]}
    ready = {("ssd", 0): assemble("ssd", 0, _run_rider(_Gather(shards_of("ssd", 0)), name="gather_first"))}
    params, acts = {}, {}
    forward = dict(ssd=_ssd_forward, fox=_fox_forward, ffn=_ffn_forward)
    for kind, idx in order:
        if kind == "ssd":
            params[kind, idx] = ssd_params(idx, 2 * idx, ready.pop((kind, idx)))
        elif kind == "fox":
            params[kind, idx] = fox_params(idx, 2 * idx + 1, ready.pop((kind, idx)))
        else:
            params[kind, idx] = ffn_params(idx, ready.pop((kind, idx)))
        wanted = fetch.get((kind, idx), [])
        rider = _Gather([s for sub in wanted for s in shards_of(*sub)]) if wanted else None
        h, acts[kind, idx], riding = forward[kind](h, params[kind, idx], f"{kind}{idx}", rider=rider)
        for q, sub in enumerate(wanted):
            ready[sub] = assemble(*sub, riding[2 * q:2 * q + 2])
    loss_part, dh, d_final_g = _loss_head(h, final_norm_g[None], target, name="loss_head")

    backward = dict(ssd=_ssd_backward, fox=_fox_backward, ffn=_ffn_backward)
    grad_keys = dict(ssd=("w_in", "w_out"), fox=("w_in", "w_out"), ffn=("w_up", "w_down"))
    sub_g, slabs, received = {}, {}, {}
    waiting = []
    for sub in reversed(order):
        kind = sub[0]
        dh, sub_g[sub], got = backward[kind](dh, params[sub], acts[sub], f"{kind}{sub[1]}",
                                             ride=[slabs[key] for key in waiting])
        received.update(zip(waiting, got))
        waiting = []
        for q, (key, n) in enumerate(zip(grad_keys[kind], sub_weights[kind])):
            slabs[sub, q] = _to_slabs(sub_g[sub][key], n in COL_SHARDED)
            if kind != "ffn" and q == 1:
                received[sub, q] = sub_g[sub]["w_out_received"]
            else:
                waiting.append((sub, q))
    received.update(zip(waiting, _run_rider(_Exchange([slabs[key] for key in waiting]), name="rs_last_exchange")))
    grad_x = dh[None]
    ssd_g, fox_g = [sub_g["ssd", 0], sub_g["ssd", 1]], [sub_g["fox", 0], sub_g["fox", 1]]
    mix_g = [ssd_g[0], fox_g[0], ssd_g[1], fox_g[1]]
    ffn_g = [sub_g["ffn", i] for i in range(DEPTH)]

    me = jnp.reshape(2 * chip + cc, (1,)).astype(jnp.int32)
    finals = {}
    for sub in order:
        for q in range(2):
            _, _, m, cols = slabs[sub, q].shape
            finals[sub, q] = _add_selected(slabs[sub, q].reshape(8, m, cols), received[sub, q], me,
                                           name=f"rs_add_{sub[0]}{sub[1]}_{q}")
    keys = list(finals)
    others = dict(zip(keys, _sibling_swap([finals[key] for key in keys], name="rs_result_swap")))
    grads = {}
    for kind, names in sub_weights.items():
        for q, n in enumerate(names):
            subs = [sub for sub in sorted(set(order)) if sub[0] == kind]
            mine = jnp.stack([finals[sub, q] for sub in subs])
            theirs = jnp.stack([others[sub, q] for sub in subs])
            halves = jnp.stack([jnp.where(cc == 0, mine, theirs), jnp.where(cc == 0, theirs, mine)], axis=1)
            grads[n] = halves.reshape(w[n].shape)
    small = dict(
        mix_norm_g=jnp.concatenate([g["mix_g"] for g in mix_g], axis=0),
        ffn_norm_g=jnp.concatenate([g["ffn_g"] for g in ffn_g], axis=0),
        ssd_conv_w=jnp.stack([g["conv_w"] for g in ssd_g]), ssd_conv_b=jnp.stack([g["conv_b"] for g in ssd_g]),
        ssd_dt_bias=jnp.stack([g["dt_bias"].reshape(SSD_H) for g in ssd_g]),
        ssd_a_log=jnp.stack([g["a_log"].reshape(SSD_H) for g in ssd_g]),
        ssd_d=jnp.stack([g["d"].reshape(SSD_H) for g in ssd_g]),
        ssd_norm_g=jnp.concatenate([g["norm_g"] for g in ssd_g], axis=0),
        fox_b_f=jnp.stack([g["b_f"] for g in fox_g]), fox_q_norm_g=jnp.stack([g["gq"] for g in fox_g]),
        fox_k_norm_g=jnp.stack([g["gk"] for g in fox_g]),
        ffn_conv_w=jnp.stack([g["conv_w"] for g in ffn_g]), ffn_conv_b=jnp.stack([g["conv_b"] for g in ffn_g]),
        final_norm_g=d_final_g[0], loss=loss_part[0, :1])
    _, total = _allgather_small(_pack([small[n] for n, _ in SMALL]), name="reduce_small", with_sum=True)
    for (n, shp), val in zip(SMALL, _unpack(total, [shp for _, shp in SMALL])):
        grads[n] = val
    loss = grads.pop("loss")[0]
    grads["ssd_conv_w"] = lax.dynamic_slice_in_dim(grads["ssd_conv_w"], chip * ssd_conv_w.shape[2], ssd_conv_w.shape[2], axis=2)
    grads["ffn_conv_w"] = lax.dynamic_slice_in_dim(grads["ffn_conv_w"], chip * ffn_conv_w.shape[2], ffn_conv_w.shape[2], axis=2)

    deltas, new_m, new_v = {}, {}, {}
    for n in NAMES:
        shp = w[n].shape
        two_d = (1, shp[0]) if len(shp) == 1 else (-1, shp[-1])
        r2 = lambda a: a.reshape(two_d)
        d, nm, nv = _adamw(r2(w[n]), r2(grads[n]), r2(m_in[n]), r2(v_in[n]), name=f"adamw_{n}")
        deltas[n], new_m[n], new_v[n] = d.reshape(shp), nm.reshape(shp), nv.reshape(shp)
    return (loss, grad_x, *[grads[n] for n in NAMES], *[deltas[n] for n in NAMES], *[new_m[n] for n in NAMES],
            *[new_v[n] for n in NAMES])
```

```python
import functools

import jax
import jax.numpy as jnp
from jax import lax
from jax.experimental import pallas as pl
from jax.experimental.pallas import tpu as pltpu

F32 = jnp.float32
BF16 = jnp.bfloat16
HI = lax.Precision.HIGHEST
MESH = pl.DeviceIdType.MESH

D_MODEL = 1024
DEPTH = 4
EPS = 1e-6
SSD_DI = 2048
SSD_HD = 64
SSD_G = 4
SSD_HPG = 8
SSD_N = 128
SSD_K = 4
CHUNK = 128
SSD_CONV_DIM = 3072
SSD_ZX = SSD_DI + SSD_CONV_DIM
SSD_H = 32
FOX_HD = 64
FOX_H = 16
FOX_D = 1024
D_FF = 2816
FFN_K = 3
LANES = 128
VMEM_LIMIT = 56 * 1024 * 1024

ADAM_LR = 0.001
ADAM_B1 = 0.9
ADAM_B2 = 0.999
ADAM_EPS = 1e-08
ADAM_WD = 0.01
ADAM_STEP = 10

NN = (((1,), (0,)), ((), ()))
NT = (((1,), (1,)), ((), ()))
TN = (((0,), (0,)), ((), ()))


def _pick(n, cap, mult=LANES):
    best = None
    for t in range(mult, min(n, cap) + 1, mult):
        if n % t == 0:
            best = t
    return best if best is not None else n


def _cp(sem):
    return pltpu.CompilerParams(dimension_semantics=sem, vmem_limit_bytes=VMEM_LIMIT)


def _sigmoid(x):
    return jax.nn.sigmoid(x)


def _silu(x):
    return x * _sigmoid(x)


def _dsilu(x):
    s = _sigmoid(x)
    return s * (1.0 + x * (1.0 - s))


def _softplus(x):
    e = jnp.exp(-jnp.abs(x))
    u = 1.0 + e
    l1p = jnp.where(u == 1.0, e, jnp.log(u) * (e / (u - 1.0)))
    return jnp.maximum(x, 0.0) + l1p


def _dotf(a, b, dn=NN, *, onehot="b", pieces=2):
    x, e = (a, b) if onehot == "b" else (b, a)
    e = e.astype(BF16)
    acc = None
    for n in range(pieces):
        hi = x.astype(BF16)
        part = lax.dot_general(hi, e, dn, preferred_element_type=F32) if onehot == "b" else \
            lax.dot_general(e, hi, dn, preferred_element_type=F32)
        acc = part if acc is None else acc + part
        if n + 1 < pieces:
            x = x - hi.astype(F32)
    return acc


def _dotb(a, b, dn=NN):
    return lax.dot_general(a.astype(BF16), b.astype(BF16), dn, preferred_element_type=F32)


def _group_matrix(width, sub, transpose=False):
    ng = width // sub
    shape = (ng, width) if transpose else (width, ng)
    lane = lax.broadcasted_iota(jnp.int32, shape, 1 if transpose else 0)
    grp = lax.broadcasted_iota(jnp.int32, shape, 0 if transpose else 1)
    return (lane // sub == grp).astype(F32)


def _gmean(v, sub):
    width = v.shape[-1]
    if sub == width:
        return jnp.mean(v, axis=-1, keepdims=True)
    s = _dotf(v, _group_matrix(width, sub))
    return _dotf(s, _group_matrix(width, sub, transpose=True)) * (1.0 / sub)


def _matmul(a, b, *, mode, name, out_dtype=F32, add=None):
    a_planes = a.shape[0] if (mode == "nt" and a.ndim == 3) else 0
    b_planes = b.shape[0] if (mode == "tn" and b.ndim == 3) else 0
    a2 = (a.shape[1], a.shape[0] * a.shape[2]) if a_planes else a.shape
    b2 = (b.shape[1], b.shape[0] * b.shape[2]) if b_planes else b.shape
    if mode == "nn":
        (m, k), (k2, n) = a2, b2
    elif mode == "nt":
        (m, k), (n, k2) = a2, b2
    else:
        (k, m), (k2, n) = a2, b2
    assert k == k2, (a.shape, b.shape, mode)
    tm, tn = _pick(m, 1536), _pick(n // b_planes if b_planes else n, 1536)
    tk = _pick(k // a_planes if a_planes else k, 1536)
    nk = k // tk
    dn = {"nn": NN, "nt": NT, "tn": TN}[mode]
    has_add = add is not None

    def body(*refs):
        if has_add:
            a_ref, b_ref, add_ref, o_ref, acc_ref = refs
        else:
            a_ref, b_ref, o_ref, acc_ref = refs
            add_ref = None
        kk = pl.program_id(2)
        part = _dotb(a_ref[0] if a_planes else a_ref[...], b_ref[0] if b_planes else b_ref[...], dn)

        def finish(r):
            if has_add:
                r = r + add_ref[...]
            o_ref[...] = r.astype(out_dtype)

        if nk == 1:
            finish(part)
        else:
            @pl.when(kk == 0)
            def _():
                acc_ref[...] = part

            @pl.when(kk > 0)
            def _():
                acc_ref[...] += part

            @pl.when(kk == nk - 1)
            def _():
                finish(acc_ref[...])

    if mode == "nn":
        a_spec = pl.BlockSpec((tm, tk), lambda i, j, q: (i, q))
        b_spec = pl.BlockSpec((tk, tn), lambda i, j, q: (q, j))
    elif mode == "nt":
        per = (k // a_planes) // tk if a_planes else 0
        a_spec = (pl.BlockSpec((1, tm, tk), lambda i, j, q: (q // per, i, q % per)) if a_planes
                  else pl.BlockSpec((tm, tk), lambda i, j, q: (i, q)))
        b_spec = pl.BlockSpec((tn, tk), lambda i, j, q: (j, q))
    else:
        per = (n // b_planes) // tn if b_planes else 0
        a_spec = pl.BlockSpec((tk, tm), lambda i, j, q: (q, i))
        b_spec = (pl.BlockSpec((1, tk, tn), lambda i, j, q: (j // per, q, j % per)) if b_planes
                  else pl.BlockSpec((tk, tn), lambda i, j, q: (q, j)))
    o_spec = pl.BlockSpec((tm, tn), lambda i, j, q: (i, j))
    in_specs = [a_spec, b_spec] + ([o_spec] if has_add else [])
    args = (a, b) + ((add,) if has_add else ())
    return pl.pallas_call(
        body, name=name, grid=(m // tm, n // tn, nk), in_specs=in_specs, out_specs=o_spec,
        out_shape=jax.ShapeDtypeStruct((m, n), out_dtype),
        scratch_shapes=[pltpu.VMEM((tm, tn) if nk > 1 else (8, LANES), F32)],
        compiler_params=_cp(("parallel", "parallel", "arbitrary")),
    )(*args)


def _rms_fwd(x, g, *, gw, ncol, name, x_col0=0, sub=None, z=None, z_col0=0, out_dtype=BF16):
    rows = x.shape[0]
    tr = _pick(rows, 1024, 8)
    sub = gw if sub is None else sub
    gated = z is not None

    def body(*refs):
        if gated:
            x_ref, z_ref, g_ref, o_ref = refs
            xv = x_ref[...] * _silu(z_ref[...])
        else:
            x_ref, g_ref, o_ref = refs
            xv = x_ref[...]
        r = lax.rsqrt(_gmean(xv * xv, sub) + EPS)
        o_ref[...] = (xv * r * g_ref[...]).astype(out_dtype)

    specs = [pl.BlockSpec((tr, gw), lambda j, i: (i, x_col0 + j))]
    args = [x]
    if gated:
        specs.append(pl.BlockSpec((tr, gw), lambda j, i: (i, z_col0 + j)))
        args.append(z)
    specs.append(pl.BlockSpec((1, gw), lambda j, i: (0, j)))
    args.append(g)
    return pl.pallas_call(
        body, name=name, grid=(ncol, rows // tr), in_specs=specs,
        out_specs=pl.BlockSpec((tr, gw), lambda j, i: (i, j)),
        out_shape=jax.ShapeDtypeStruct((rows, gw * ncol), out_dtype),
        compiler_params=_cp(("parallel", "parallel")),
    )(*args)


def _rms_bwd(x, g, dy, *, gw, ncol, name, x_col0=0, sub=None, z=None, z_col0=0, add=None, dx_dtype=F32):
    rows = x.shape[0]
    tr = _pick(rows, 1024, 8)
    sub = gw if sub is None else sub
    gated = z is not None
    has_add = add is not None

    def body(*refs):
        refs = list(refs)
        x_ref = refs.pop(0)
        z_ref = refs.pop(0) if gated else None
        g_ref = refs.pop(0)
        dy_ref = refs.pop(0)
        add_ref = refs.pop(0) if has_add else None
        dx_ref = refs.pop(0)
        dz_ref = refs.pop(0) if gated else None
        dg_ref = refs.pop(0)
        i = pl.program_id(1)
        xv = x_ref[...]
        if gated:
            zz = z_ref[...]
            yz = xv * _silu(zz)
        else:
            yz = xv
        r = lax.rsqrt(_gmean(yz * yz, sub) + EPS)
        xh = yz * r
        dy = dy_ref[...].astype(F32)
        dyg = dy * g_ref[...]
        d_yz = r * (dyg - xh * _gmean(dyg * xh, sub))
        if gated:
            dx_ref[...] = (d_yz * _silu(zz)).astype(dx_dtype)
            dz_ref[...] = (d_yz * xv * _dsilu(zz)).astype(dx_dtype)
        elif has_add:
            dx_ref[...] = (d_yz + add_ref[...]).astype(dx_dtype)
        else:
            dx_ref[...] = d_yz.astype(dx_dtype)
        part = jnp.sum(dy * xh, axis=0, keepdims=True)

        @pl.when(i == 0)
        def _():
            dg_ref[...] = part

        @pl.when(i > 0)
        def _():
            dg_ref[...] += part

    tile = pl.BlockSpec((tr, gw), lambda j, i: (i, j))
    specs = [pl.BlockSpec((tr, gw), lambda j, i: (i, x_col0 + j))]
    args = [x]
    if gated:
        specs.append(pl.BlockSpec((tr, gw), lambda j, i: (i, z_col0 + j)))
        args.append(z)
    specs += [pl.BlockSpec((1, gw), lambda j, i: (0, j)), tile]
    args += [g, dy]
    if has_add:
        specs.append(tile)
        args.append(add)
    width = gw * ncol
    out_shape = [jax.ShapeDtypeStruct((rows, width), dx_dtype)]
    out_specs = [tile]
    if gated:
        out_shape.append(jax.ShapeDtypeStruct((rows, width), dx_dtype))
        out_specs.append(tile)
    out_shape.append(jax.ShapeDtypeStruct((1, width), F32))
    out_specs.append(pl.BlockSpec((1, gw), lambda j, i: (0, j)))
    return pl.pallas_call(
        body, name=name, grid=(ncol, rows // tr), in_specs=specs, out_specs=out_specs, out_shape=out_shape,
        compiler_params=_cp(("parallel", "arbitrary")),
    )(*args)


HALO = 8


def _conv_rows(tc):
    return 16 * 8 * LANES // tc


def _conv_fwd(u, w8, b, *, kw, width, name, u_col0=0, mul_col0=None, out_dtype=F32, rider=None):
    rows = u.shape[0]
    ts = _pick(rows, 2048, 8)
    tc = _pick(width, 512)
    gated = mul_col0 is not None
    c0 = u_col0 // tc
    m0 = (mul_col0 // tc) if gated else 0
    assert u_col0 % tc == 0 and (not gated or mul_col0 % tc == 0)

    def body(*refs):
        if gated:
            cur_ref, halo_ref, mul_ref, w_ref, b_ref, o_ref, ext = refs
        else:
            cur_ref, halo_ref, w_ref, b_ref, o_ref, ext = refs
        i = pl.program_id(0)
        ext[pl.ds(0, HALO), :] = jnp.where(i == 0, 0.0, halo_ref[...])
        ext[pl.ds(HALO, ts), :] = cur_ref[...]
        bias = b_ref[...]
        taps = [w_ref[k:k + 1, :] for k in range(kw)]
        rb = _conv_rows(tc)
        for r0 in range(0, ts, rb):
            pre = bias + taps[0] * ext[pl.ds(r0 + HALO - (kw - 1), rb), :]
            for k in range(1, kw):
                pre = pre + taps[k] * ext[pl.ds(r0 + HALO - (kw - 1) + k, rb), :]
            act = _silu(pre)
            if gated:
                act = act * mul_ref[pl.ds(r0, rb), :]
            o_ref[pl.ds(r0, rb), :] = act.astype(out_dtype)

    hb = ts // HALO
    specs = [pl.BlockSpec((ts, tc), lambda i, j: (i, c0 + j)),
             pl.BlockSpec((HALO, tc), lambda i, j: (jnp.maximum(i * hb - 1, 0), c0 + j))]
    args = [u, u]
    if gated:
        specs.append(pl.BlockSpec((ts, tc), lambda i, j: (i, m0 + j)))
        args.append(u)
    specs += [pl.BlockSpec((8, tc), lambda i, j: (0, j)), pl.BlockSpec((1, tc), lambda i, j: (0, j))]
    args += [w8, b]
    outs, riding = _call(
        body, name=name, grid=(rows // ts, width // tc), in_specs=specs,
        out_specs=[pl.BlockSpec((ts, tc), lambda i, j: (i, j))],
        out_shape=[jax.ShapeDtypeStruct((rows, width), out_dtype)],
        scratch_shapes=[pltpu.VMEM((ts + HALO, tc), F32)], sem=("parallel", "parallel"), rider=rider, args=args)
    return outs + [riding]


def _conv_bwd(u, w8, b, dact, *, kw, width, name, u_col0=0, mul_col0=None, du_dtype=BF16, rider=None):
    rows = u.shape[0]
    ts = _pick(rows, 2048, 8)
    tc = _pick(width, 512)
    pieces = list(dact) if isinstance(dact, (list, tuple)) else [dact]
    firsts, seen = [], 0
    for piece in pieces:
        assert piece.shape[1] % tc == 0, (piece.shape, tc)
        firsts.append(seen // tc)
        seen += piece.shape[1]
    assert seen == width
    gated = mul_col0 is not None
    c0 = u_col0 // tc
    m0 = (mul_col0 // tc) if gated else 0
    nt = rows // ts
    hb = ts // HALO

    def body(*refs):
        refs = list(refs)
        cur_ref, halo_ref = refs.pop(0), refs.pop(0)
        mul_ref = refs.pop(0) if gated else None
        w_ref, b_ref = refs.pop(0), refs.pop(0)
        da_refs = [refs.pop(0) for _ in pieces]
        col_tile = pl.program_id(0)
        du_ref = refs.pop(0)
        dwb_ref, ext_u, ext_d = refs
        t = pl.program_id(1)
        ti = nt - 1 - t
        ext_u[pl.ds(0, HALO), :] = jnp.where(ti == 0, 0.0, halo_ref[...])
        ext_u[pl.ds(HALO, ts), :] = cur_ref[...]

        @pl.when(t == 0)
        def _():
            ext_d[pl.ds(ts, HALO), :] = jnp.zeros((HALO, tc), F32)
            dwb_ref[...] = jnp.zeros((8, tc), F32)

        bias = b_ref[...]
        taps = [w_ref[k:k + 1, :] for k in range(kw)]
        rb = _conv_rows(tc)
        dw_acc = [jnp.zeros((1, tc), F32) for _ in range(kw)]
        db_acc = jnp.zeros((1, tc), F32)
        for r0 in reversed(range(0, ts, rb)):
            shifted = [ext_u[pl.ds(r0 + HALO - (kw - 1) + k, rb), :] for k in range(kw)]
            pre = bias + taps[0] * shifted[0]
            for k in range(1, kw):
                pre = pre + taps[k] * shifted[k]
            sg = _sigmoid(pre)
            dsilu = sg * (1.0 + pre * (1.0 - sg))
            da = da_refs[0][pl.ds(r0, rb), :].astype(F32)
            for first, ref in zip(firsts[1:], da_refs[1:]):
                da = jnp.where(col_tile >= first, ref[pl.ds(r0, rb), :].astype(F32), da)
            if gated:
                du_ref[1, pl.ds(r0, rb), :] = (da * (pre * sg)).astype(du_dtype)
                dgp = da * mul_ref[pl.ds(r0, rb), :] * dsilu
            else:
                dgp = da * dsilu
            ext_d[pl.ds(r0, rb), :] = dgp
            du = taps[kw - 1] * dgp
            for k in range(kw - 1):
                du = du + taps[k] * ext_d[pl.ds(r0 + kw - 1 - k, rb), :]
            if gated:
                du_ref[0, pl.ds(r0, rb), :] = du.astype(du_dtype)
            else:
                du_ref[pl.ds(r0, rb), :] = du.astype(du_dtype)
            for k in range(kw):
                dw_acc[k] = dw_acc[k] + jnp.sum(dgp * shifted[k], axis=0, keepdims=True)
            db_acc = db_acc + jnp.sum(dgp, axis=0, keepdims=True)
        for k in range(kw):
            dwb_ref[k:k + 1, :] += dw_acc[k]
        dwb_ref[7:8, :] += db_acc
        ext_d[pl.ds(ts, HALO), :] = ext_d[pl.ds(0, HALO), :]

    specs = [pl.BlockSpec((ts, tc), lambda j, t: (nt - 1 - t, c0 + j)),
             pl.BlockSpec((HALO, tc), lambda j, t: (jnp.maximum((nt - 1 - t) * hb - 1, 0), c0 + j))]
    args = [u, u]
    if gated:
        specs.append(pl.BlockSpec((ts, tc), lambda j, t: (nt - 1 - t, m0 + j)))
        args.append(u)
    tile = pl.BlockSpec((ts, tc), lambda j, t: (nt - 1 - t, j))
    specs += [pl.BlockSpec((8, tc), lambda j, t: (0, j)), pl.BlockSpec((1, tc), lambda j, t: (0, j))]
    args += [w8, b]
    for first, piece in zip(firsts, pieces):
        count = piece.shape[1] // tc

        def piece_map(j, t, first=first, count=count):
            mine = jnp.logical_and(j >= first, j < first + count)
            return (jnp.where(mine, nt - 1 - t, 0), jnp.clip(j - first, 0, count - 1))

        specs.append(pl.BlockSpec((ts, tc), piece_map))
        args.append(piece)
    if gated:
        out_shape = [jax.ShapeDtypeStruct((2, rows, width), du_dtype)]
        out_specs = [pl.BlockSpec((2, ts, tc), lambda j, t: (0, nt - 1 - t, j))]
    else:
        out_shape = [jax.ShapeDtypeStruct((rows, width), du_dtype)]
        out_specs = [tile]
    out_shape.append(jax.ShapeDtypeStruct((8, width), F32))
    out_specs.append(pl.BlockSpec((8, tc), lambda j, t: (0, j)))
    outs, riding = _call(
        body, name=name, grid=(width // tc, nt), in_specs=specs, out_specs=out_specs, out_shape=out_shape,
        scratch_shapes=[pltpu.VMEM((ts + HALO, tc), F32), pltpu.VMEM((ts + HALO, tc), F32)],
        sem=("parallel", "arbitrary"), rider=rider, args=args)
    return outs + [riding]


GW = SSD_HPG * SSD_HD


def _ssd_common(x, bm, cm, dt_raw, dt_raw_t, bias_r, bias_c, alog_r, alog_c):
    row = lax.broadcasted_iota(jnp.int32, (CHUNK, CHUNK), 0)
    col = lax.broadcasted_iota(jnp.int32, (CHUNK, CHUNK), 1)
    causal = row >= col
    tril = causal.astype(F32)
    triu = (row <= col).astype(F32)
    spread = _group_matrix(GW, SSD_HD, transpose=True)
    dt = _softplus(dt_raw + bias_r)
    dt_t = _softplus(dt_raw_t + bias_c)
    a_r = -jnp.exp(alog_r)
    a_c = -jnp.exp(alog_c)
    acs = _dotf(tril, dt * a_r, onehot="a", pieces=3)
    acs_t = _dotf(dt_t * a_c, triu, pieces=3)
    last = acs[CHUNK - 1:CHUNK, :]
    ds = jnp.exp(last - acs)
    cd = jnp.exp(last)
    c = dict(causal=causal, tril=tril, triu=triu, spread=spread, dt=dt, a_r=a_r, acs=acs, acs_t=acs_t, ds=ds, cd=cd)
    c["eb"] = _dotf(jnp.exp(acs), spread)
    c["dsb"] = _dotf(ds, spread)
    c["cdb"] = _dotf(cd, spread)
    c["dtb"] = _dotf(dt, spread)
    c["xdt"] = x * c["dtb"]
    c["cb"] = _dotb(cm, bm, NT)
    return c


def _ssd_lam(c, r):
    diff = c["acs"][:, r:r + 1] - c["acs_t"][r:r + 1, :]
    return jnp.exp(jnp.where(c["causal"], diff, -jnp.inf))


GP = 2


def _ssd_specs(nc, rev):
    def ci(t):
        return (nc - 1 - t) if rev else t
    xs = pl.BlockSpec((CHUNK, GP * GW), lambda g, t: (ci(t), g))
    bs = pl.BlockSpec((CHUNK, GP * SSD_N), lambda g, t: (ci(t), SSD_DI // (GP * SSD_N) + g))
    cs = pl.BlockSpec((CHUNK, GP * SSD_N), lambda g, t: (ci(t), (SSD_DI // SSD_N + SSD_G) // GP + g))
    dts = pl.BlockSpec((GP, CHUNK, 8), lambda g, t: (g, ci(t), 0))
    dtts = pl.BlockSpec((GP, 8, CHUNK), lambda g, t: (g, 0, ci(t)))
    pr = pl.BlockSpec((GP, 1, 8), lambda g, t: (g, 0, 0))
    pc = pl.BlockSpec((GP, 8, 1), lambda g, t: (g, 0, 0))
    hs = pl.BlockSpec((1, GP, SSD_N, GW), lambda g, t: (ci(t), g, 0, 0))
    return xs, bs, cs, dts, dtts, pr, pc, hs


def _ssd_fwd(xbc, dtg, dtg_t, bias_r, bias_c, alog_r, alog_c, d_r, *, name, rider=None):
    s = xbc.shape[0]
    nc = s // CHUNK
    xs, bs, cs, dts, dtts, pr, pc, hs = _ssd_specs(nc, False)

    def body(x_ref, b_ref, c_ref, dt_ref, dtt_ref, br_ref, bc_ref, ar_ref, ac_ref, d_ref, y_ref, hp_ref, h_sc):
        t = pl.program_id(1)

        @pl.when(t == 0)
        def _():
            h_sc[...] = jnp.zeros_like(h_sc)

        for gg in range(GP):
            wide, narrow = slice(gg * GW, (gg + 1) * GW), slice(gg * SSD_N, (gg + 1) * SSD_N)
            x, bm, cm = x_ref[:, wide], b_ref[:, narrow], c_ref[:, narrow]
            c = _ssd_common(x, bm, cm, dt_ref[gg], dtt_ref[gg], br_ref[gg], bc_ref[gg], ar_ref[gg], ac_ref[gg])
            h = h_sc[gg]
            hp_ref[0, gg] = h
            xdt = c["xdt"]
            pieces = []
            for r in range(SSD_HPG):
                m = c["cb"] * _ssd_lam(c, r)
                pieces.append(_dotb(m, xdt[:, r * SSD_HD:(r + 1) * SSD_HD]))
            y = jnp.concatenate(pieces, axis=1) + c["eb"] * _dotb(cm, h) + x * _dotf(d_ref[gg], c["spread"])
            y_ref[:, wide] = y
            h_sc[gg] = h * c["cdb"] + _dotb(bm, xdt * c["dsb"], TN)

    outs, riding = _call(
        body, name=name, grid=(SSD_G // GP, nc),
        in_specs=[xs, bs, cs, dts, dtts, pr, pc, pr, pc, pr],
        out_specs=[xs, hs],
        out_shape=[jax.ShapeDtypeStruct((s, SSD_DI), F32), jax.ShapeDtypeStruct((nc, SSD_G, SSD_N, GW), F32)],
        scratch_shapes=[pltpu.VMEM((GP, SSD_N, GW), F32)], sem=("parallel", "arbitrary"), rider=rider,
        args=(xbc, xbc, xbc, dtg, dtg_t, bias_r, bias_c, alog_r, alog_c, d_r))
    return outs + [riding]


def _ssd_bwd(xbc, dtg, dtg_t, bias_r, bias_c, alog_r, alog_c, d_r, hprev, dy, *, name, rider=None):
    s = xbc.shape[0]
    nc = s // CHUNK
    xs, bs, cs, dts, dtts, pr, pc, hs = _ssd_specs(nc, True)
    gsum = functools.partial(_group_matrix, GW, SSD_HD)

    def body(x_ref, b_ref, c_ref, dt_ref, dtt_ref, br_ref, bc_ref, ar_ref, ac_ref, d_ref, hp_ref, dy_ref,
             dx_ref, db_ref, dc_ref, ddt_ref, dbias_ref, dalog_ref, dd_ref, dh_sc):
        t = pl.program_id(1)

        @pl.when(t == 0)
        def _():
            dh_sc[...] = jnp.zeros_like(dh_sc)
            dbias_ref[...] = jnp.zeros_like(dbias_ref)
            dalog_ref[...] = jnp.zeros_like(dalog_ref)
            dd_ref[...] = jnp.zeros_like(dd_ref)

        for gg in range(GP):
            wide, narrow = slice(gg * GW, (gg + 1) * GW), slice(gg * SSD_N, (gg + 1) * SSD_N)
            x, bm, cm = x_ref[:, wide], b_ref[:, narrow], c_ref[:, narrow]
            c = _ssd_common(x, bm, cm, dt_ref[gg], dtt_ref[gg], br_ref[gg], bc_ref[gg], ar_ref[gg], ac_ref[gg])
            lanesum = gsum()
            h = hp_ref[0, gg]
            dh = dh_sc[gg]
            dy = dy_ref[:, wide]
            xdt, dsb = c["xdt"], c["dsb"]
            skip = _dotf(d_ref[gg], c["spread"])
            dd_ref[gg] += jnp.sum(_dotf(dy * x, lanesum), axis=0, keepdims=True)
            dacs = _dotf(dy * (c["eb"] * _dotb(cm, h)), lanesum)
            edy = c["eb"] * dy
            dcm = _dotb(edy, h, NT)
            dh_prev = _dotb(cm, edy, TN)
            bdh = _dotb(bm, dh)
            dxdt = dsb * bdh
            dbm = _dotb(dsb * xdt, dh, NT)
            t1 = _dotf(xdt * bdh, lanesum) * c["ds"]
            dacs = dacs - t1
            dlast = (jnp.sum(t1, axis=0, keepdims=True)
                     + jnp.sum(_dotf(dh * h, lanesum), axis=0, keepdims=True) * c["cd"])
            dcb = jnp.zeros((CHUNK, CHUNK), F32)
            pieces = []
            ones8 = jnp.ones((CHUNK, 8), F32)
            head = lax.broadcasted_iota(jnp.int32, (1, 8), 1)
            for r in range(SSD_HPG):
                sl = slice(r * SSD_HD, (r + 1) * SSD_HD)
                lam = _ssd_lam(c, r)
                m = c["cb"] * lam
                dm = _dotb(dy[:, sl], xdt[:, sl], NT)
                dcb = dcb + dm * lam
                gm = dm * m
                dacs = dacs + ((jnp.sum(gm, axis=1, keepdims=True) - _dotf(gm, ones8, TN, pieces=3))
                               * (head == r).astype(F32))
                pieces.append(_dotb(m, dy[:, sl], TN))
            dxdt = dxdt + jnp.concatenate(pieces, axis=1)
            dcm = dcm + _dotb(dcb, bm)
            dbm = dbm + _dotb(dcb, cm, TN)
            dx_ref[:, wide] = dy * skip + dxdt * c["dtb"]
            db_ref[:, narrow] = dbm
            dc_ref[:, narrow] = dcm
            rowid = lax.broadcasted_iota(jnp.int32, (CHUNK, 8), 0)
            dacs = dacs + jnp.where(rowid == CHUNK - 1, dlast, 0.0)
            dda = _dotf(c["triu"], dacs, onehot="a", pieces=3)
            ddt = _dotf(dxdt * x, lanesum) + dda * c["a_r"]
            ddt_raw = ddt * _sigmoid(dt_ref[gg] + br_ref[gg])
            ddt_ref[gg] = ddt_raw
            dbias_ref[gg] += jnp.sum(ddt_raw, axis=0, keepdims=True)
            dalog_ref[gg] += jnp.sum(dda * c["dt"], axis=0, keepdims=True) * c["a_r"]
            dh_sc[gg] = dh_prev + dh * c["cdb"]

    ci = lambda t: nc - 1 - t
    nspec = pl.BlockSpec((CHUNK, GP * SSD_N), lambda g, t: (ci(t), g))
    outs, riding = _call(
        body, name=name, grid=(SSD_G // GP, nc),
        in_specs=[xs, bs, cs, dts, dtts, pr, pc, pr, pc, pr, hs, xs],
        out_specs=[xs, nspec, nspec, dts, pr, pr, pr],
        out_shape=[jax.ShapeDtypeStruct((s, SSD_DI), F32), jax.ShapeDtypeStruct((s, SSD_G * SSD_N), F32),
                   jax.ShapeDtypeStruct((s, SSD_G * SSD_N), F32), jax.ShapeDtypeStruct((SSD_G, s, 8), F32),
                   jax.ShapeDtypeStruct((SSD_G, 1, 8), F32), jax.ShapeDtypeStruct((SSD_G, 1, 8), F32),
                   jax.ShapeDtypeStruct((SSD_G, 1, 8), F32)],
        scratch_shapes=[pltpu.VMEM((GP, SSD_N, GW), F32)], sem=("parallel", "arbitrary"), rider=rider,
        args=(xbc, xbc, xbc, dtg, dtg_t, bias_r, bias_c, alog_r, alog_c, d_r, hprev, dy))
    return outs + [riding]


FOX_PAIRS = FOX_H // 2
FOX_SCALE = FOX_HD ** -0.5
NEG_INF = -jnp.inf


def _fgate_fwd(f_t, b_c, *, name):
    hh, s = f_t.shape
    tb = _pick(s, 512)
    nb = s // tb

    def body(f_ref, b_ref, o_ref, carry):
        t = pl.program_id(0)

        @pl.when(t == 0)
        def _():
            carry[...] = jnp.zeros_like(carry)

        lf = -_softplus(-(f_ref[...] + b_ref[...]))
        row = lax.broadcasted_iota(jnp.int32, (tb, tb), 0)
        col = lax.broadcasted_iota(jnp.int32, (tb, tb), 1)
        cum = _dotf(lf, (row <= col).astype(F32), pieces=3) + carry[:, 0:1]
        o_ref[...] = cum
        carry[:, 0:1] = cum[:, tb - 1:tb]

    return pl.pallas_call(
        body, name=name, grid=(nb,),
        in_specs=[pl.BlockSpec((hh, tb), lambda t: (0, t)), pl.BlockSpec((hh, 1), lambda t: (0, 0))],
        out_specs=pl.BlockSpec((hh, tb), lambda t: (0, t)),
        out_shape=jax.ShapeDtypeStruct((hh, s), F32),
        scratch_shapes=[pltpu.VMEM((hh, LANES), F32)],
        compiler_params=_cp(("arbitrary",)),
    )(f_t, b_c)


def _fgate_bwd(dcum_q_t, dcum_k_t, f_t, b_c, *, name):
    hh, s = f_t.shape
    tb = _pick(s, 512)
    nb = s // tb

    def body(dq_ref, d_ref, f_ref, b_ref, df_ref, db_ref, carry):
        t = pl.program_id(0)

        @pl.when(t == 0)
        def _():
            carry[...] = jnp.zeros_like(carry)
            db_ref[...] = jnp.zeros_like(db_ref)

        d = d_ref[...] + dq_ref[...]
        row = lax.broadcasted_iota(jnp.int32, (tb, tb), 0)
        col = lax.broadcasted_iota(jnp.int32, (tb, tb), 1)
        rev = _dotf(d, (row >= col).astype(F32), pieces=3) + carry[:, 0:1]
        df = rev * _sigmoid(-(f_ref[...] + b_ref[...]))
        df_ref[...] = df
        db_ref[...] += jnp.sum(df, axis=1, keepdims=True)
        carry[:, 0:1] = rev[:, 0:1]

    blk = pl.BlockSpec((hh, tb), lambda t: (0, nb - 1 - t))
    return pl.pallas_call(
        body, name=name, grid=(nb,),
        in_specs=[blk, blk, blk, pl.BlockSpec((hh, 1), lambda t: (0, 0))],
        out_specs=[blk, pl.BlockSpec((hh, 1), lambda t: (0, 0))],
        out_shape=[jax.ShapeDtypeStruct((hh, s), F32), jax.ShapeDtypeStruct((hh, 1), F32)],
        scratch_shapes=[pltpu.VMEM((hh, LANES), F32)],
        compiler_params=_cp(("arbitrary",)),
    )(dcum_q_t, dcum_k_t, f_t, b_c)


def _fox_tile(s):
    return min(512, max(s // 2, 8))


def _tri_tables(nq, kv_major):
    if kv_major:
        pairs = [(i, j) for j in range(nq) for i in range(j, nq)]
    else:
        pairs = [(i, j) for i in range(nq) for j in range(i + 1)]
    return (jnp.asarray([p[0] for p in pairs], jnp.int32), jnp.asarray([p[1] for p in pairs], jnp.int32))


def _lane_tile(col, width):
    return col if width == LANES else jnp.tile(col, (1, width // LANES))


def _flash_fwd(qs, kn, qkvg, ck, *, name, rider=None, k_tile0=0):
    s = qs.shape[0]
    tt = _fox_tile(s)
    nq = s // tt
    itab, jtab = _tri_tables(nq, kv_major=False)
    v0 = 2 * FOX_D // LANES

    def body(itab_ref, jtab_ref, q_ref, k_ref, v_ref, ck_ref, o_ref, lse_ref, m_sc, l_sc, acc_sc):
        t = pl.program_id(1)
        i, j = itab_ref[t], jtab_ref[t]

        @pl.when(j == 0)
        def _():
            m_sc[...] = jnp.full_like(m_sc, NEG_INF)
            l_sc[...] = jnp.zeros_like(l_sc)
            acc_sc[...] = jnp.zeros_like(acc_sc)

        low = lax.broadcasted_iota(jnp.int32, (tt, LANES), 1) < FOX_HD

        def step(diagonal):
            q2, k2 = q_ref[...], k_ref[...]
            v2 = v_ref[...].astype(BF16)
            alphas, outs = [], []
            for hh in range(2):
                qh = jnp.where(low if hh == 0 else jnp.logical_not(low), q2, jnp.zeros_like(q2))
                sc = lax.dot_general(qh, k2, NT, preferred_element_type=F32) - ck_ref[0][hh:hh + 1, :]
                if diagonal:
                    row = lax.broadcasted_iota(jnp.int32, sc.shape, 0)
                    col = lax.broadcasted_iota(jnp.int32, sc.shape, 1)
                    sc = jnp.where(row >= col, sc, NEG_INF)
                m_prev = m_sc[hh]
                m_new = jnp.maximum(m_prev, jnp.max(sc, axis=1, keepdims=True))
                alpha = jnp.exp(m_prev - m_new)
                p = jnp.exp(sc - _lane_tile(m_new, tt))
                l_sc[hh] = alpha * l_sc[hh] + jnp.sum(p, axis=1, keepdims=True)
                m_sc[hh] = m_new
                alphas.append(alpha)
                outs.append(lax.dot_general(p.astype(BF16), v2, NN, preferred_element_type=F32))
            acc_sc[...] = jnp.where(low, alphas[0], alphas[1]) * acc_sc[...] + jnp.where(low, outs[0], outs[1])

        @pl.when(j < i)
        def _():
            step(False)

        @pl.when(j == i)
        def _():
            step(True)
            o_ref[...] = acc_sc[...] / jnp.where(low, l_sc[0], l_sc[1])
            lse_ref[0] = jnp.concatenate([m_sc[hh][:, 0:1] + jnp.log(l_sc[hh][:, 0:1]) for hh in range(2)], axis=1)

    outs, riding = _call(
        body, name=name, grid=(FOX_PAIRS, int(itab.shape[0])), prefetch=(itab, jtab),
        in_specs=[pl.BlockSpec((tt, LANES), lambda p, t, it, jt: (it[t], p)),
                  pl.BlockSpec((tt, LANES), lambda p, t, it, jt: (jt[t], k_tile0 + p)),
                  pl.BlockSpec((tt, LANES), lambda p, t, it, jt: (jt[t], v0 + p)),
                  pl.BlockSpec((1, 2, tt), lambda p, t, it, jt: (p, 0, jt[t]))],
        out_specs=[pl.BlockSpec((tt, LANES), lambda p, t, it, jt: (it[t], p)),
                   pl.BlockSpec((1, tt, 2), lambda p, t, it, jt: (p, it[t], 0))],
        scratch_shapes=[pltpu.VMEM((2, tt, LANES), F32), pltpu.VMEM((2, tt, LANES), F32), pltpu.VMEM((tt, LANES), F32)],
        out_shape=[jax.ShapeDtypeStruct((s, FOX_D), F32), jax.ShapeDtypeStruct((FOX_PAIRS, s, 2), F32)],
        sem=("parallel", "arbitrary"), rider=rider, args=(qs, kn, qkvg, ck))
    return outs + [riding]


def _flash_bwd(qs, kn, qkvg, do, lse_t, delta_t, ck_c, *, name, rider=None, k_tile0=0):
    s = qs.shape[0]
    tt = _fox_tile(s)
    nq = s // tt
    nl = tt // LANES
    itab, jtab = _tri_tables(nq, kv_major=True)
    nsteps = itab.shape[0]
    v0 = 2 * FOX_D // LANES

    def body(itab_ref, jtab_ref, q_ref, k_ref, v_ref, do_ref, lse_ref, dl_ref, ck_ref,
             dq_ref, dk_ref, dv_ref, dcq_ref, dck_ref, dqt_sc, rs_sc, dk_sc, dv_sc, dc_sc, kt_sc, ckb_sc):
        t = pl.program_id(1)
        i, j = itab_ref[t], jtab_ref[t]

        @pl.when(t == 0)
        def _():
            dqt_sc[...] = jnp.zeros_like(dqt_sc)
            rs_sc[...] = jnp.zeros_like(rs_sc)

        @pl.when(i == j)
        def _():
            dk_sc[...] = jnp.zeros_like(dk_sc)
            dv_sc[...] = jnp.zeros_like(dv_sc)
            dc_sc[...] = jnp.zeros_like(dc_sc)
            kt_sc[...] = k_ref[...].astype(F32).T.astype(BF16)
            for hh in range(2):
                ckb_sc[hh] = jnp.broadcast_to(ck_ref[0][:, hh:hh + 1], (tt, LANES))

        low = lax.broadcasted_iota(jnp.int32, (tt, LANES), 1) < FOX_HD
        top = lax.broadcasted_iota(jnp.int32, (LANES, tt), 0) < FOX_HD

        def step(diagonal):
            q2, k2, kt = q_ref[...], k_ref[...], kt_sc[...]
            v2 = v_ref[...].astype(BF16)
            do2 = do_ref[...].astype(BF16)
            dqs, dks, dvs = [], [], []
            for hh in range(2):
                sel = low if hh == 0 else jnp.logical_not(low)
                qh = jnp.where(sel, q2, jnp.zeros_like(q2))
                doh = jnp.where(sel, do2, jnp.zeros_like(do2))
                st = lax.dot_general(k2, qh, NT, preferred_element_type=F32)
                st = st - _lane_tile(ckb_sc[hh], tt) - lse_ref[0][hh:hh + 1, :]
                if diagonal:
                    key = lax.broadcasted_iota(jnp.int32, st.shape, 0)
                    qry = lax.broadcasted_iota(jnp.int32, st.shape, 1)
                    st = jnp.where(qry >= key, st, NEG_INF)
                pt = jnp.exp(st)
                dpt = lax.dot_general(v2, doh, NT, preferred_element_type=F32)
                dst = pt * (dpt - dl_ref[0][hh:hh + 1, :])
                ptb, dstb = pt.astype(BF16), dst.astype(BF16)
                dvs.append(lax.dot_general(ptb, do2, NN, preferred_element_type=F32))
                dks.append(lax.dot_general(dstb, q2, NN, preferred_element_type=F32))
                dqs.append(lax.dot_general(kt, dstb, NN, preferred_element_type=F32))
                rs_sc[hh, i] += jnp.sum(dst, axis=0, keepdims=True)
                part = dst[:, 0:LANES]
                for b in range(1, nl):
                    part = part + dst[:, b * LANES:(b + 1) * LANES]
                dc_sc[hh] += part
            dv_sc[...] += jnp.where(low, dvs[0], dvs[1])
            dk_sc[...] += jnp.where(low, dks[0], dks[1])
            dqt_sc[i] += jnp.where(top, dqs[0], dqs[1])

        @pl.when(j < i)
        def _():
            step(False)

        @pl.when(j == i)
        def _():
            step(True)

        @pl.when(i == nq - 1)
        def _():
            dk_ref[...] = dk_sc[...]
            dv_ref[...] = dv_sc[...]
            dck_ref[0] = -jnp.concatenate([jnp.sum(dc_sc[hh], axis=1, keepdims=True) for hh in range(2)], axis=1)

        @pl.when(t == nsteps - 1)
        def _():
            for b in range(nq):
                dq_ref[pl.ds(b * tt, tt), :] = dqt_sc[b].T * FOX_SCALE
                dcq_ref[0, :, pl.ds(b * tt, tt)] = jnp.concatenate([rs_sc[hh, b] for hh in range(2)], axis=0)

    qside = pl.BlockSpec((tt, LANES), lambda p, t, it, jt: (it[t], p))
    kside = pl.BlockSpec((tt, LANES), lambda p, t, it, jt: (jt[t], p))
    qstat = pl.BlockSpec((1, 2, tt), lambda p, t, it, jt: (p, 0, it[t]))
    kstat = pl.BlockSpec((1, tt, 2), lambda p, t, it, jt: (p, jt[t], 0))
    outs, riding = _call(
        body, name=name, grid=(FOX_PAIRS, nsteps), prefetch=(itab, jtab),
        in_specs=[qside, pl.BlockSpec((tt, LANES), lambda p, t, it, jt: (jt[t], k_tile0 + p)),
                  pl.BlockSpec((tt, LANES), lambda p, t, it, jt: (jt[t], v0 + p)), qside, qstat, qstat, kstat],
        out_specs=[pl.BlockSpec((s, LANES), lambda p, t, it, jt: (0, p)), kside, kside,
                   pl.BlockSpec((1, 2, s), lambda p, t, it, jt: (p, 0, 0)), kstat],
        scratch_shapes=[pltpu.VMEM((nq, LANES, tt), F32), pltpu.VMEM((2, nq, 1, tt), F32), pltpu.VMEM((tt, LANES), F32),
                        pltpu.VMEM((tt, LANES), F32), pltpu.VMEM((2, tt, LANES), F32), pltpu.VMEM((LANES, tt), BF16),
                        pltpu.VMEM((2, tt, LANES), F32)],
        out_shape=[jax.ShapeDtypeStruct((s, FOX_D), F32), jax.ShapeDtypeStruct((s, FOX_D), F32),
                   jax.ShapeDtypeStruct((s, FOX_D), F32), jax.ShapeDtypeStruct((FOX_PAIRS, 2, s), F32),
                   jax.ShapeDtypeStruct((FOX_PAIRS, s, 2), F32)],
        sem=("parallel", "arbitrary"), rider=rider, args=(qs, kn, qkvg, do, lse_t, delta_t, ck_c))
    return outs + [riding]


def _ogate_fwd(o, qkvg, *, name):
    s = o.shape[0]
    tr = _pick(s, 1024, 8)

    def body(o_ref, g_ref, out_ref):
        out_ref[...] = (o_ref[...] * _sigmoid(g_ref[...])).astype(BF16)

    tile = pl.BlockSpec((tr, FOX_D), lambda i: (i, 0))
    return pl.pallas_call(
        body, name=name, grid=(s // tr,), in_specs=[tile, pl.BlockSpec((tr, FOX_D), lambda i: (i, 3))],
        out_specs=tile, out_shape=jax.ShapeDtypeStruct((s, FOX_D), BF16), compiler_params=_cp(("parallel",)),
    )(o, qkvg)


def _ogate_bwd(dog, o, qkvg, *, name):
    s = o.shape[0]
    tr = _pick(s, 1024, 8)

    def body(dog_ref, o_ref, g_ref, do_ref, dg_ref, dl_ref):
        sg = _sigmoid(g_ref[...])
        ov = o_ref[...]
        dog_v = dog_ref[...]
        do = dog_v * sg
        do_ref[...] = do
        dg_ref[...] = (dog_v * ov * sg * (1.0 - sg)).astype(BF16)
        dl_ref[...] = _dotf(do * ov, _group_matrix(FOX_D, FOX_HD))

    tile = pl.BlockSpec((tr, FOX_D), lambda i: (i, 0))
    return pl.pallas_call(
        body, name=name, grid=(s // tr,), in_specs=[tile, tile, pl.BlockSpec((tr, FOX_D), lambda i: (i, 3))],
        out_specs=[tile, tile, pl.BlockSpec((tr, FOX_H), lambda i: (i, 0))],
        out_shape=[jax.ShapeDtypeStruct((s, FOX_D), F32), jax.ShapeDtypeStruct((s, FOX_D), BF16),
                   jax.ShapeDtypeStruct((s, FOX_H), F32)],
        compiler_params=_cp(("parallel",)),
    )(dog, o, qkvg)


def _loss_head(h, g, target, *, name):
    s, d = h.shape
    tr = _pick(s, 1024, 8)

    def body(h_ref, g_ref, t_ref, loss_ref, dh_ref, dg_ref):
        i = pl.program_id(0)
        x = h_ref[...]
        gv = g_ref[...]
        r = lax.rsqrt(jnp.mean(x * x, axis=-1, keepdims=True) + EPS)
        xh = x * r
        err = xh * gv - t_ref[...]
        part = 0.5 * jnp.sum(jnp.sum(err * err, axis=1, keepdims=True) * (1.0 / d), axis=0, keepdims=True)
        dy = err * (1.0 / d)
        dyg = dy * gv
        dh_ref[...] = r * (dyg - xh * jnp.mean(dyg * xh, axis=-1, keepdims=True))
        dgp = jnp.sum(dy * xh, axis=0, keepdims=True)

        @pl.when(i == 0)
        def _():
            loss_ref[...] = jnp.zeros_like(loss_ref) + part
            dg_ref[...] = dgp

        @pl.when(i > 0)
        def _():
            loss_ref[...] += part
            dg_ref[...] += dgp

    tile = pl.BlockSpec((tr, d), lambda i: (i, 0))
    vec = pl.BlockSpec((1, d), lambda i: (0, 0))
    return pl.pallas_call(
        body, name=name, grid=(s // tr,), in_specs=[tile, vec, tile],
        out_specs=[pl.BlockSpec((1, LANES), lambda i: (0, 0)), tile, vec],
        out_shape=[jax.ShapeDtypeStruct((1, LANES), F32), jax.ShapeDtypeStruct((s, d), F32),
                   jax.ShapeDtypeStruct((1, d), F32)],
        compiler_params=_cp(("arbitrary",)),
    )(h, g, target)


def _adamw(w, g, m, v, *, name):
    rows, cols = w.shape
    tr = _pick(rows, 256, 8)
    c1 = 1.0 - ADAM_B1 ** ADAM_STEP
    c2 = 1.0 - ADAM_B2 ** ADAM_STEP

    def body(w_ref, g_ref, m_ref, v_ref, d_ref, nm_ref, nv_ref):
        gv = g_ref[...]
        nm = ADAM_B1 * m_ref[...] + (1.0 - ADAM_B1) * gv
        nv = ADAM_B2 * v_ref[...] + (1.0 - ADAM_B2) * (gv * gv)
        d_ref[...] = -ADAM_LR * ((nm / c1) / (jnp.sqrt(nv / c2) + ADAM_EPS) + ADAM_WD * w_ref[...])
        nm_ref[...] = nm
        nv_ref[...] = nv

    tile = pl.BlockSpec((tr, cols), lambda i: (i, 0))
    shp = jax.ShapeDtypeStruct((rows, cols), F32)
    return pl.pallas_call(
        body, name=name, grid=(rows // tr,), in_specs=[tile] * 4, out_specs=[tile] * 3, out_shape=[shp] * 3,
        compiler_params=_cp(("parallel",)),
    )(w, g, m, v)


ANY = pl.BlockSpec(memory_space=pl.ANY)
N_DEV = 8


def _coords():
    return lax.axis_index("x"), lax.axis_index("y"), lax.axis_index("c")


def _other_chips(x, y):
    return [(1 - x, y), (x, 1 - y), (1 - x, 1 - y)]


def _allgather_small(buf, *, name, with_sum):
    rows = buf.shape[0]

    def body(*refs):
        if with_sum:
            x_ref, out_ref, sum_ref, send_sems, recv_sems = refs
        else:
            x_ref, out_ref, send_sems, recv_sems = refs
        x, y, c = _coords()
        me = 4 * x + 2 * y + c
        out_ref[me] = x_ref[...]
        copies = []
        for rel in range(1, N_DEV):
            px = (1 - x) if rel & 4 else x
            py = (1 - y) if rel & 2 else y
            pc = (1 - c) if rel & 1 else c
            cp = pltpu.make_async_remote_copy(
                src_ref=x_ref, dst_ref=out_ref.at[me], send_sem=send_sems.at[rel - 1], recv_sem=recv_sems.at[rel - 1],
                device_id=(px, py, pc), device_id_type=MESH)
            cp.start()
            copies.append(cp)
        for cp in copies:
            cp.wait()
        if with_sum:
            acc = out_ref[0]
            for k in range(1, N_DEV):
                acc = acc + out_ref[k]
            sum_ref[...] = acc

    slots = jax.ShapeDtypeStruct((N_DEV, rows, LANES), F32)
    vm = pl.BlockSpec(memory_space=pltpu.VMEM)
    out_shape = [slots, jax.ShapeDtypeStruct((rows, LANES), F32)] if with_sum else [slots]
    return pl.pallas_call(
        body, name=name, in_specs=[vm], out_specs=[vm] * len(out_shape), out_shape=out_shape,
        scratch_shapes=[pltpu.SemaphoreType.DMA((N_DEV - 1,)), pltpu.SemaphoreType.DMA((N_DEV - 1,))],
    )(buf)


class _Gather:
    per_array = 6

    def __init__(self, arrays):
        self.arrays = list(arrays)

    def out_shapes(self):
        return [jax.ShapeDtypeStruct((4,) + a.shape, a.dtype) for a in self.arrays]

    @staticmethod
    def _ici(ins, outs, send_sems, recv_sems, t, j, px, py, c, slot):
        return pltpu.make_async_remote_copy(
            src_ref=ins[t].at[c], dst_ref=outs[t].at[slot, c], send_sem=send_sems.at[6 * t + j],
            recv_sem=recv_sems.at[6 * t + j], device_id=(px, py, c), device_id_type=MESH)

    @staticmethod
    def _d2d(outs, send_sems, recv_sems, t, j, kj, half, sibling):
        return pltpu.make_async_remote_copy(
            src_ref=outs[t].at[kj, half], dst_ref=outs[t].at[kj, half], send_sem=send_sems.at[6 * t + 3 + j],
            recv_sem=recv_sems.at[6 * t + 3 + j], device_id=sibling, device_id_type=MESH)

    def start(self, ins, outs, send_sems, recv_sems):
        x, y, c = _coords()
        for t in range(len(ins)):
            for j, (px, py) in enumerate(_other_chips(x, y)):
                self._ici(ins, outs, send_sems, recv_sems, t, j, px, py, c, 2 * x + y).start()

    def finish(self, ins, outs, send_sems, recv_sems):
        x, y, c = _coords()
        chips = _other_chips(x, y)
        sibling = (x, y, 1 - c)
        started = []
        for t in range(len(ins)):
            for j, (px, py) in enumerate(chips):
                ici = self._ici(ins, outs, send_sems, recv_sems, t, j, px, py, c, 2 * px + py)
                ici.wait_recv()
                fwd = self._d2d(outs, send_sems, recv_sems, t, j, 2 * px + py, c, sibling)
                fwd.start()
                started += [ici, fwd]
        for t in range(len(ins)):
            for j, (px, py) in enumerate(chips):
                self._d2d(outs, send_sems, recv_sems, t, j, 2 * px + py, 1 - c, sibling).wait_recv()
        for cp in started:
            cp.wait_send()


class _Exchange:
    per_array = 7

    def __init__(self, arrays):
        self.arrays = list(arrays)

    def out_shapes(self):
        return [jax.ShapeDtypeStruct((7,) + a.shape[2:], a.dtype) for a in self.arrays]

    @staticmethod
    def _copies(ins, outs, send_sems, recv_sems):
        x, y, c = _coords()
        for t in range(len(ins)):
            for rel in range(1, N_DEV):
                px = (1 - x) if rel & 4 else x
                py = (1 - y) if rel & 2 else y
                pc = (1 - c) if rel & 1 else c
                yield pltpu.make_async_remote_copy(
                    src_ref=ins[t].at[2 * px + py, pc], dst_ref=outs[t].at[rel - 1], send_sem=send_sems.at[7 * t + rel - 1],
                    recv_sem=recv_sems.at[7 * t + rel - 1], device_id=(px, py, pc), device_id_type=MESH)

    def start(self, ins, outs, send_sems, recv_sems):
        for cp in self._copies(ins, outs, send_sems, recv_sems):
            cp.start()

    def finish(self, ins, outs, send_sems, recv_sems):
        for cp in self._copies(ins, outs, send_sems, recv_sems):
            cp.wait()


def _call(body, *, name, grid, in_specs, out_specs, out_shape, scratch_shapes, args, sem, rider=None, prefetch=()):
    n_in, n_out, n_pre = len(in_specs), len(out_specs), len(prefetch)
    n_c = len(rider.arrays) if rider is not None else 0

    def wrapped(*refs):
        pre, rest = refs[:n_pre], refs[n_pre:]
        ins, cins = rest[:n_in], rest[n_in:n_in + n_c]
        outs = rest[n_in + n_c:n_in + n_c + n_out]
        couts = rest[n_in + n_c + n_out:n_in + 2 * n_c + n_out]
        scratch = rest[n_in + 2 * n_c + n_out:]
        if rider is None:
            body(*pre, *ins, *outs, *scratch)
            return
        send_sems, recv_sems = scratch[-2:]
        ids = [pl.program_id(a) for a in range(len(grid))]
        first = functools.reduce(jnp.logical_and, [i == 0 for i in ids])
        last = functools.reduce(jnp.logical_and, [i == g - 1 for i, g in zip(ids, grid)])

        @pl.when(first)
        def _():
            rider.start(cins, couts, send_sems, recv_sems)

        body(*pre, *ins, *outs, *scratch[:-2])

        @pl.when(last)
        def _():
            rider.finish(cins, couts, send_sems, recv_sems)

    if rider is not None:
        nsem = rider.per_array * n_c
        in_specs = list(in_specs) + [ANY] * n_c
        out_specs = list(out_specs) + [ANY] * n_c
        out_shape = list(out_shape) + rider.out_shapes()
        scratch_shapes = list(scratch_shapes) + [pltpu.SemaphoreType.DMA((nsem,)), pltpu.SemaphoreType.DMA((nsem,))]
        args = list(args) + rider.arrays
        sem = ("arbitrary",) * len(grid)
    if n_pre:
        res = pl.pallas_call(
            wrapped, name=name, out_shape=out_shape, compiler_params=_cp(sem),
            grid_spec=pltpu.PrefetchScalarGridSpec(num_scalar_prefetch=n_pre, grid=grid, in_specs=in_specs,
                                                   out_specs=out_specs, scratch_shapes=scratch_shapes),
        )(*prefetch, *args)
    else:
        res = pl.pallas_call(
            wrapped, name=name, grid=grid, in_specs=in_specs, out_specs=out_specs, out_shape=out_shape,
            scratch_shapes=scratch_shapes, compiler_params=_cp(sem),
        )(*args)
    return list(res[:n_out]), list(res[n_out:])


def _run_rider(rider, *, name):
    n = len(rider.arrays)

    def body(*refs):
        ins, outs = refs[:n], refs[n:2 * n]
        send_sems, recv_sems = refs[2 * n:]
        rider.start(ins, outs, send_sems, recv_sems)
        rider.finish(ins, outs, send_sems, recv_sems)

    nsem = rider.per_array * n
    return pl.pallas_call(
        body, name=name, in_specs=[ANY] * n, out_specs=[ANY] * n, out_shape=rider.out_shapes(),
        scratch_shapes=[pltpu.SemaphoreType.DMA((nsem,)), pltpu.SemaphoreType.DMA((nsem,))],
    )(*rider.arrays)


def _sibling_swap(arrs, *, name):
    n = len(arrs)

    def body(*refs):
        ins, outs = refs[:n], refs[n:2 * n]
        send_sems, recv_sems = refs[2 * n:]
        x, y, c = _coords()
        copies = []
        for t in range(n):
            cp = pltpu.make_async_remote_copy(
                src_ref=ins[t], dst_ref=outs[t], send_sem=send_sems.at[t], recv_sem=recv_sems.at[t],
                device_id=(x, y, 1 - c), device_id_type=MESH)
            cp.start()
            copies.append(cp)
        for cp in copies:
            cp.wait()

    return pl.pallas_call(
        body, name=name, in_specs=[ANY] * n, out_specs=[ANY] * n,
        out_shape=[jax.ShapeDtypeStruct(a.shape, a.dtype) for a in arrs],
        scratch_shapes=[pltpu.SemaphoreType.DMA((n,)), pltpu.SemaphoreType.DMA((n,))],
    )(*arrs)


def _add_selected(stack, others, sel, *, name):
    _, m, cols = stack.shape
    q = others.shape[0]
    tr = _pick(m, 256, 16)

    def body(sel_ref, s_ref, o_ref, out_ref):
        acc = s_ref[0].astype(F32)
        for i in range(q):
            acc = acc + o_ref[i].astype(F32)
        out_ref[...] = acc

    return pl.pallas_call(
        body, name=name,
        grid_spec=pltpu.PrefetchScalarGridSpec(
            num_scalar_prefetch=1, grid=(m // tr,),
            in_specs=[pl.BlockSpec((1, tr, cols), lambda i, sel_ref: (sel_ref[0], i, 0)),
                      pl.BlockSpec((q, tr, cols), lambda i, sel_ref: (0, i, 0))],
            out_specs=pl.BlockSpec((tr, cols), lambda i, sel_ref: (i, 0))),
        out_shape=jax.ShapeDtypeStruct((m, cols), F32),
        compiler_params=_cp(("parallel",)),
    )(sel, stack, others)


BIG = ("ssd_w_in", "ssd_w_out", "fox_w_in", "fox_w_out", "ffn_w_up", "ffn_w_down")
COL_SHARDED = ("ssd_w_in", "fox_w_in", "ffn_w_up")
SMALL = (("mix_norm_g", (4, 1024)), ("ffn_norm_g", (4, 1024)), ("ssd_conv_w", (2, 4, 3072)), ("ssd_conv_b", (2, 3072)),
         ("ssd_dt_bias", (2, 32)), ("ssd_a_log", (2, 32)), ("ssd_d", (2, 32)), ("ssd_norm_g", (2, 2048)),
         ("fox_b_f", (2, 16)), ("fox_q_norm_g", (2, 64)), ("fox_k_norm_g", (2, 64)), ("ffn_conv_w", (4, 3, 2816)),
         ("ffn_conv_b", (4, 2816)), ("final_norm_g", (1024,)), ("loss", (1,)))
NAMES = ("mix_norm_g", "ffn_norm_g", "ssd_w_in", "ssd_conv_w", "ssd_conv_b", "ssd_dt_bias", "ssd_a_log", "ssd_d",
         "ssd_norm_g", "ssd_w_out", "fox_w_in", "fox_b_f", "fox_q_norm_g", "fox_k_norm_g", "fox_w_out", "ffn_w_up",
         "ffn_conv_w", "ffn_conv_b", "ffn_w_down", "final_norm_g")


def _pack(parts):
    flat = jnp.concatenate([jnp.reshape(p, (-1,)).astype(F32) for p in parts])
    rows = -(-flat.shape[0] // (8 * LANES)) * 8
    return jnp.pad(flat, (0, rows * LANES - flat.shape[0])).reshape(rows, LANES)


def _unpack(buf, shapes):
    flat = buf.reshape(-1)
    out, off = [], 0
    for shp in shapes:
        size = 1
        for d in shp:
            size *= d
        out.append(flat[off:off + size].reshape(shp))
        off += size
    return out


def _pad_lanes(a):
    return jnp.pad(a, ((0, 0), (0, LANES - a.shape[1])))


def _pad8(w):
    return jnp.pad(w, ((0, 8 - w.shape[0]), (0, 0)))


def _ssd_forward(h, p, name, rider=None):
    s = h.shape[0]
    hn = _rms_fwd(h, p["mix_g"], gw=D_MODEL, ncol=1, name=f"{name}_norm")
    zx = _matmul(hn, p["w_zx"], mode="nn", name=f"{name}_proj")
    dtp = _matmul(hn, p["w_dt"], mode="nn", name=f"{name}_proj_dt")
    xbc, _ = _conv_fwd(zx, p["conv_w8"], p["conv_b"], kw=SSD_K, width=SSD_CONV_DIM, u_col0=SSD_DI, name=f"{name}_conv")
    dt3 = dtp[:, :SSD_H].reshape(s, SSD_G, SSD_HPG)
    dtg, dtg_t = jnp.transpose(dt3, (1, 0, 2)), jnp.transpose(dt3, (1, 2, 0))
    sp = (p["bias_r"], p["bias_c"], p["alog_r"], p["alog_c"], p["d_r"])
    y, hprev, riding = _ssd_fwd(xbc, dtg, dtg_t, *sp, name=f"{name}_scan", rider=rider)
    y2 = _rms_fwd(y, p["norm_g"], gw=SSD_DI // SSD_G, ncol=SSD_G, z=zx, name=f"{name}_gnorm")
    out = _matmul(y2, p["w_out"], mode="nn", add=h, name=f"{name}_out")
    return out, dict(h=h, hn=hn, zx=zx, xbc=xbc, dtg=dtg, dtg_t=dtg_t, y=y, hprev=hprev, y2=y2), riding


def _ssd_backward(dh1, p, a, name, ride=()):
    s = dh1.shape[0]
    g = {}
    dy2 = _matmul(dh1, p["w_out"], mode="nt", name=f"{name}_out_dx")
    g["w_out"] = _matmul(a["y2"], dh1, mode="tn", out_dtype=BF16, name=f"{name}_out_dw")
    rider = _Exchange(list(ride) + [_to_slabs(g["w_out"], False)])
    dy, dz, g["norm_g"] = _rms_bwd(a["y"], p["norm_g"], dy2, gw=SSD_DI // SSD_G, ncol=SSD_G, z=a["zx"], name=f"{name}_gnorm_b")
    sp = (p["bias_r"], p["bias_c"], p["alog_r"], p["alog_c"], p["d_r"])
    dx, dbm, dcm, ddt, g["dt_bias"], g["a_log"], g["d"], riding = _ssd_bwd(
        a["xbc"], a["dtg"], a["dtg_t"], *sp, a["hprev"], dy, name=f"{name}_scan_b", rider=rider)
    dxbc, dwb, _ = _conv_bwd(a["zx"], p["conv_w8"], p["conv_b"], [dx, dbm, dcm], kw=SSD_K, width=SSD_CONV_DIM,
                             u_col0=SSD_DI, name=f"{name}_conv_b")
    g["conv_w"], g["conv_b"] = dwb[:SSD_K], dwb[7]
    dzx = jnp.concatenate([dz.astype(BF16), dxbc], axis=1)
    ddtp = _pad_lanes(jnp.transpose(ddt, (1, 0, 2)).reshape(s, SSD_H))
    dhn = _matmul(dzx, p["w_zx"], mode="nt", name=f"{name}_proj_dx")
    dhn = _matmul(ddtp, p["w_dt"], mode="nt", add=dhn, name=f"{name}_proj_dt_dx")
    dw_zx = _matmul(a["hn"], dzx, mode="tn", out_dtype=BF16, name=f"{name}_proj_dw")
    dw_dt = _matmul(a["hn"], ddtp, mode="tn", out_dtype=BF16, name=f"{name}_proj_dt_dw")
    g["w_in"] = jnp.concatenate([dw_zx, dw_dt[:, :SSD_H]], axis=1)
    dh, g["mix_g"] = _rms_bwd(a["h"], p["mix_g"], dhn, gw=D_MODEL, ncol=1, add=dh1, name=f"{name}_norm_b")
    g["w_out_received"] = riding[-1]
    return dh, g, riding[:-1]


def _fox_forward(h, p, name, rider=None):
    s = h.shape[0]
    hn = _rms_fwd(h, p["mix_g"], gw=D_MODEL, ncol=1, name=f"{name}_norm")
    qkvg = _matmul(hn, p["w_qkvg"], mode="nn", name=f"{name}_proj")
    fp = _matmul(hn, p["w_f"], mode="nn", name=f"{name}_proj_f")
    qk = _rms_fwd(qkvg, jnp.concatenate([p["gq"] * FOX_SCALE, p["gk"]], axis=1), gw=FOX_D, ncol=2, x_col0=0, sub=FOX_HD,
                  name=f"{name}_qknorm")
    f_t = jnp.transpose(fp[:, :FOX_H])
    cum_t = _fgate_fwd(f_t, p["b_f"], name=f"{name}_fgate")
    ck = cum_t.reshape(FOX_PAIRS, 2, s)
    o, lse, riding = _flash_fwd(qk, qk, qkvg, ck, name=f"{name}_attn", rider=rider, k_tile0=FOX_D // LANES)
    og = _ogate_fwd(o, qkvg, name=f"{name}_ogate")
    out = _matmul(og, p["w_out"], mode="nn", add=h, name=f"{name}_out")
    return out, dict(h=h, hn=hn, qkvg=qkvg, qk=qk, f_t=f_t, ck=ck, o=o, lse=lse, og=og), riding


def _fox_backward(dh1, p, a, name, ride=()):
    s = dh1.shape[0]
    g = {}
    dog = _matmul(dh1, p["w_out"], mode="nt", name=f"{name}_out_dx")
    g["w_out"] = _matmul(a["og"], dh1, mode="tn", out_dtype=BF16, name=f"{name}_out_dw")
    rider = _Exchange(list(ride) + [_to_slabs(g["w_out"], False)])
    do, dgate, delta = _ogate_bwd(dog, a["o"], a["qkvg"], name=f"{name}_ogate_b")
    swap = lambda v: jnp.transpose(v, (0, 2, 1))
    dl_t = jnp.transpose(delta.reshape(s, FOX_PAIRS, 2), (1, 2, 0))
    dq, dk, dv, dcq, dck, riding = _flash_bwd(a["qk"], a["qk"], a["qkvg"], do, swap(a["lse"]), dl_t, swap(a["ck"]),
                                              name=f"{name}_attn_b", rider=rider, k_tile0=FOX_D // LANES)
    dq_raw, dgq = _rms_bwd(a["qkvg"], p["gq"], dq, gw=FOX_D, ncol=1, x_col0=0, sub=FOX_HD, dx_dtype=BF16, name=f"{name}_qnorm_b")
    dk_raw, dgk = _rms_bwd(a["qkvg"], p["gk"], dk, gw=FOX_D, ncol=1, x_col0=1, sub=FOX_HD, dx_dtype=BF16, name=f"{name}_knorm_b")
    g["gq"] = dgq.reshape(FOX_H, FOX_HD).sum(axis=0)
    g["gk"] = dgk.reshape(FOX_H, FOX_HD).sum(axis=0)
    df_t, dbf = _fgate_bwd(dcq.reshape(FOX_H, s), swap(dck).reshape(FOX_H, s), a["f_t"], p["b_f"], name=f"{name}_fgate_b")
    g["b_f"] = dbf[:, 0]
    dproj = jnp.concatenate([dq_raw, dk_raw, dv.astype(BF16), dgate], axis=1)
    dfp = _pad_lanes(jnp.transpose(df_t))
    dhn = _matmul(dproj, p["w_qkvg"], mode="nt", name=f"{name}_proj_dx")
    dhn = _matmul(dfp, p["w_f"], mode="nt", add=dhn, name=f"{name}_proj_f_dx")
    dw_qkvg = _matmul(a["hn"], dproj, mode="tn", out_dtype=BF16, name=f"{name}_proj_dw")
    dw_f = _matmul(a["hn"], dfp, mode="tn", out_dtype=BF16, name=f"{name}_proj_f_dw")
    g["w_in"] = jnp.concatenate([dw_qkvg, dw_f[:, :FOX_H]], axis=1)
    dh, g["mix_g"] = _rms_bwd(a["h"], p["mix_g"], dhn, gw=D_MODEL, ncol=1, add=dh1, name=f"{name}_norm_b")
    g["w_out_received"] = riding[-1]
    return dh, g, riding[:-1]


def _ffn_forward(h, p, name, rider=None):
    hn = _rms_fwd(h, p["ffn_g"], gw=D_MODEL, ncol=1, name=f"{name}_norm")
    u = _matmul(hn, p["w_up"], mode="nn", name=f"{name}_up")
    act, riding = _conv_fwd(u, p["conv_w8"], p["conv_b"], kw=FFN_K, width=D_FF, u_col0=0, mul_col0=D_FF, out_dtype=BF16,
                            name=f"{name}_glu", rider=rider)
    out = _matmul(act, p["w_down"], mode="nn", add=h, name=f"{name}_down")
    return out, dict(h=h, hn=hn, u=u, act=act), riding


def _ffn_backward(dh2, p, a, name, ride=()):
    g = {}
    dact = _matmul(dh2, p["w_down"], mode="nt", name=f"{name}_down_dx")
    g["w_down"] = _matmul(a["act"], dh2, mode="tn", out_dtype=BF16, name=f"{name}_down_dw")
    du, dwb, riding = _conv_bwd(a["u"], p["conv_w8"], p["conv_b"], dact, kw=FFN_K, width=D_FF, u_col0=0, mul_col0=D_FF,
                                name=f"{name}_glu_b", rider=_Exchange(list(ride)) if ride else None)
    g["conv_w"], g["conv_b"] = dwb[:FFN_K], dwb[7]
    dhn = _matmul(du, p["w_up"], mode="nt", name=f"{name}_up_dx")
    g["w_up"] = _matmul(a["hn"], du, mode="tn", out_dtype=BF16, name=f"{name}_up_dw")
    dh, g["ffn_g"] = _rms_bwd(a["h"], p["ffn_g"], dhn, gw=D_MODEL, ncol=1, add=dh2, name=f"{name}_norm_b")
    return dh, g, riding


def _to_slabs(dw, col_sharded):
    rows, cols = dw.shape
    if col_sharded:
        return jnp.transpose(dw.reshape(rows, 4, cols // 4), (1, 0, 2)).reshape(4, 2, rows // 2, cols // 4)
    return dw.reshape(4, 2, rows // 8, cols)


def kernel(x, mix_norm_g, ffn_norm_g, ssd_w_in, ssd_conv_w, ssd_conv_b, ssd_dt_bias, ssd_a_log, ssd_d, ssd_norm_g, ssd_w_out, fox_w_in, fox_b_f, fox_q_norm_g, fox_k_norm_g, fox_w_out, ffn_w_up, ffn_conv_w, ffn_conv_b, ffn_w_down, final_norm_g, loss_target, m_mix_norm_g, m_ffn_norm_g, m_ssd_w_in, m_ssd_conv_w, m_ssd_conv_b, m_ssd_dt_bias, m_ssd_a_log, m_ssd_d, m_ssd_norm_g, m_ssd_w_out, m_fox_w_in, m_fox_b_f, m_fox_q_norm_g, m_fox_k_norm_g, m_fox_w_out, m_ffn_w_up, m_ffn_conv_w, m_ffn_conv_b, m_ffn_w_down, m_final_norm_g, v_mix_norm_g, v_ffn_norm_g, v_ssd_w_in, v_ssd_conv_w, v_ssd_conv_b, v_ssd_dt_bias, v_ssd_a_log, v_ssd_d, v_ssd_norm_g, v_ssd_w_out, v_fox_w_in, v_fox_b_f, v_fox_q_norm_g, v_fox_k_norm_g, v_fox_w_out, v_ffn_w_up, v_ffn_conv_w, v_ffn_conv_b, v_ffn_w_down, v_final_norm_g):
    w = dict(mix_norm_g=mix_norm_g, ffn_norm_g=ffn_norm_g, ssd_w_in=ssd_w_in, ssd_conv_w=ssd_conv_w, ssd_conv_b=ssd_conv_b,
             ssd_dt_bias=ssd_dt_bias, ssd_a_log=ssd_a_log, ssd_d=ssd_d, ssd_norm_g=ssd_norm_g, ssd_w_out=ssd_w_out,
             fox_w_in=fox_w_in, fox_b_f=fox_b_f, fox_q_norm_g=fox_q_norm_g, fox_k_norm_g=fox_k_norm_g, fox_w_out=fox_w_out,
             ffn_w_up=ffn_w_up, ffn_conv_w=ffn_conv_w, ffn_conv_b=ffn_conv_b, ffn_w_down=ffn_w_down, final_norm_g=final_norm_g)
    m_in = dict(zip(NAMES, (m_mix_norm_g, m_ffn_norm_g, m_ssd_w_in, m_ssd_conv_w, m_ssd_conv_b, m_ssd_dt_bias, m_ssd_a_log,
                            m_ssd_d, m_ssd_norm_g, m_ssd_w_out, m_fox_w_in, m_fox_b_f, m_fox_q_norm_g, m_fox_k_norm_g,
                            m_fox_w_out, m_ffn_w_up, m_ffn_conv_w, m_ffn_conv_b, m_ffn_w_down, m_final_norm_g)))
    v_in = dict(zip(NAMES, (v_mix_norm_g, v_ffn_norm_g, v_ssd_w_in, v_ssd_conv_w, v_ssd_conv_b, v_ssd_dt_bias, v_ssd_a_log,
                            v_ssd_d, v_ssd_norm_g, v_ssd_w_out, v_fox_w_in, v_fox_b_f, v_fox_q_norm_g, v_fox_k_norm_g,
                            v_fox_w_out, v_ffn_w_up, v_ffn_conv_w, v_ffn_conv_b, v_ffn_w_down, v_final_norm_g)))
    cx, cy, cc = _coords()
    chip = 2 * cx + cy
    h = x[0]
    target = loss_target[0]

    conv_shapes = [ssd_conv_w.shape, ffn_conv_w.shape]
    slots = _allgather_small(_pack([ssd_conv_w, ffn_conv_w]), name="gather_conv_w", with_sum=False)[0]
    per_chip = [_unpack(slots[2 * q], conv_shapes) for q in range(4)]
    ssd_conv_full = jnp.concatenate([pc[0] for pc in per_chip], axis=2)
    ffn_conv_full = jnp.concatenate([pc[1] for pc in per_chip], axis=2)
    low = {n: w[n].astype(BF16) for n in BIG}
    sub_weights = dict(ssd=("ssd_w_in", "ssd_w_out"), fox=("fox_w_in", "fox_w_out"), ffn=("ffn_w_up", "ffn_w_down"))

    def shards_of(kind, idx):
        return [low[n][idx].reshape(2, low[n].shape[1] // 2, low[n].shape[2]) for n in sub_weights[kind]]

    def assemble(kind, idx, gathered):
        full = []
        for n, own, gth in zip(sub_weights[kind], shards_of(kind, idx), gathered):
            gth = lax.dynamic_update_slice(gth, own[None], (chip, 0, 0, 0))
            _, _, half, cols = gth.shape
            if n in COL_SHARDED:
                full.append(jnp.transpose(gth.reshape(4, 2 * half, cols), (1, 0, 2)).reshape(2 * half, 4 * cols))
            else:
                full.append(gth.reshape(8 * half, cols))
        return full

    def ssd_params(j, i, weights):
        w_in, w_out = weights
        g3 = lambda v: v.reshape(SSD_G, 1, SSD_HPG)
        g3c = lambda v: v.reshape(SSD_G, SSD_HPG, 1)
        return dict(mix_g=mix_norm_g[i][None], w_zx=w_in[:, :SSD_ZX], w_dt=_pad_lanes(w_in[:, SSD_ZX:]),
                    conv_w8=_pad8(ssd_conv_full[j]), conv_b=ssd_conv_b[j][None], bias_r=g3(ssd_dt_bias[j]),
                    bias_c=g3c(ssd_dt_bias[j]), alog_r=g3(ssd_a_log[j]), alog_c=g3c(ssd_a_log[j]), d_r=g3(ssd_d[j]),
                    norm_g=ssd_norm_g[j][None], w_out=w_out)

    def fox_params(j, i, weights):
        w_in, w_out = weights
        return dict(mix_g=mix_norm_g[i][None], w_qkvg=w_in[:, :4 * FOX_D], w_f=_pad_lanes(w_in[:, 4 * FOX_D:]),
                    gq=jnp.tile(fox_q_norm_g[j], FOX_H)[None], gk=jnp.tile(fox_k_norm_g[j], FOX_H)[None],
                    b_f=fox_b_f[j][:, None], w_out=w_out)

    def ffn_params(i, weights):
        w_up, w_down = weights
        return dict(ffn_g=ffn_norm_g[i][None], w_up=w_up, conv_w8=_pad8(ffn_conv_full[i]), conv_b=ffn_conv_b[i][None],
                    w_down=w_down)

    order = [("ssd", 0), ("ffn", 0), ("fox", 0), ("ffn", 1), ("ssd", 1), ("ffn", 2), ("fox", 1), ("ffn", 3)]
    fetch = {("ssd", 0): [("ffn", 0)], ("ffn", 0): [("fox", 0)], ("fox", 0): [("ffn", 1), ("ssd", 1), ("ffn", 2)],
             ("ssd", 1): [("fox", 1)], ("fox", 1): [("ffn", 3)]}
    ready = {("ssd", 0): assemble("ssd", 0, _run_rider(_Gather(shards_of("ssd", 0)), name="gather_first"))}
    params, acts = {}, {}
    forward = dict(ssd=_ssd_forward, fox=_fox_forward, ffn=_ffn_forward)
    for kind, idx in order:
        if kind == "ssd":
            params[kind, idx] = ssd_params(idx, 2 * idx, ready.pop((kind, idx)))
        elif kind == "fox":
            params[kind, idx] = fox_params(idx, 2 * idx + 1, ready.pop((kind, idx)))
        else:
            params[kind, idx] = ffn_params(idx, ready.pop((kind, idx)))
        wanted = fetch.get((kind, idx), [])
        rider = _Gather([s for sub in wanted for s in shards_of(*sub)]) if wanted else None
        h, acts[kind, idx], riding = forward[kind](h, params[kind, idx], f"{kind}{idx}", rider=rider)
        for q, sub in enumerate(wanted):
            ready[sub] = assemble(*sub, riding[2 * q:2 * q + 2])
    loss_part, dh, d_final_g = _loss_head(h, final_norm_g[None], target, name="loss_head")

    backward = dict(ssd=_ssd_backward, fox=_fox_backward, ffn=_ffn_backward)
    grad_keys = dict(ssd=("w_in", "w_out"), fox=("w_in", "w_out"), ffn=("w_up", "w_down"))
    sub_g, slabs, received = {}, {}, {}
    waiting = []
    for sub in reversed(order):
        kind = sub[0]
        dh, sub_g[sub], got = backward[kind](dh, params[sub], acts[sub], f"{kind}{sub[1]}",
                                             ride=[slabs[key] for key in waiting])
        received.update(zip(waiting, got))
        waiting = []
        for q, (key, n) in enumerate(zip(grad_keys[kind], sub_weights[kind])):
            slabs[sub, q] = _to_slabs(sub_g[sub][key], n in COL_SHARDED)
            if kind != "ffn" and q == 1:
                received[sub, q] = sub_g[sub]["w_out_received"]
            else:
                waiting.append((sub, q))
    received.update(zip(waiting, _run_rider(_Exchange([slabs[key] for key in waiting]), name="rs_last_exchange")))
    grad_x = dh[None]
    ssd_g, fox_g = [sub_g["ssd", 0], sub_g["ssd", 1]], [sub_g["fox", 0], sub_g["fox", 1]]
    mix_g = [ssd_g[0], fox_g[0], ssd_g[1], fox_g[1]]
    ffn_g = [sub_g["ffn", i] for i in range(DEPTH)]

    me = jnp.reshape(2 * chip + cc, (1,)).astype(jnp.int32)
    finals = {}
    for sub in order:
        for q in range(2):
            _, _, m, cols = slabs[sub, q].shape
            finals[sub, q] = _add_selected(slabs[sub, q].reshape(8, m, cols), received[sub, q], me,
                                           name=f"rs_add_{sub[0]}{sub[1]}_{q}")
    keys = list(finals)
    others = dict(zip(keys, _sibling_swap([finals[key] for key in keys], name="rs_result_swap")))
    grads = {}
    for kind, names in sub_weights.items():
        for q, n in enumerate(names):
            subs = [sub for sub in sorted(set(order)) if sub[0] == kind]
            mine = jnp.stack([finals[sub, q] for sub in subs])
            theirs = jnp.stack([others[sub, q] for sub in subs])
            halves = jnp.stack([jnp.where(cc == 0, mine, theirs), jnp.where(cc == 0, theirs, mine)], axis=1)
            grads[n] = halves.reshape(w[n].shape)
    small = dict(
        mix_norm_g=jnp.concatenate([g["mix_g"] for g in mix_g], axis=0),
        ffn_norm_g=jnp.concatenate([g["ffn_g"] for g in ffn_g], axis=0),
        ssd_conv_w=jnp.stack([g["conv_w"] for g in ssd_g]), ssd_conv_b=jnp.stack([g["conv_b"] for g in ssd_g]),
        ssd_dt_bias=jnp.stack([g["dt_bias"].reshape(SSD_H) for g in ssd_g]),
        ssd_a_log=jnp.stack([g["a_log"].reshape(SSD_H) for g in ssd_g]),
        ssd_d=jnp.stack([g["d"].reshape(SSD_H) for g in ssd_g]),
        ssd_norm_g=jnp.concatenate([g["norm_g"] for g in ssd_g], axis=0),
        fox_b_f=jnp.stack([g["b_f"] for g in fox_g]), fox_q_norm_g=jnp.stack([g["gq"] for g in fox_g]),
        fox_k_norm_g=jnp.stack([g["gk"] for g in fox_g]),
        ffn_conv_w=jnp.stack([g["conv_w"] for g in ffn_g]), ffn_conv_b=jnp.stack([g["conv_b"] for g in ffn_g]),
        final_norm_g=d_final_g[0], loss=loss_part[0, :1])
    _, total = _allgather_small(_pack([small[n] for n, _ in SMALL]), name="reduce_small", with_sum=True)
    for (n, shp), val in zip(SMALL, _unpack(total, [shp for _, shp in SMALL])):
        grads[n] = val
    loss = grads.pop("loss")[0]
    grads["ssd_conv_w"] = lax.dynamic_slice_in_dim(grads["ssd_conv_w"], chip * ssd_conv_w.shape[2], ssd_conv_w.shape[2], axis=2)
    grads["ffn_conv_w"] = lax.dynamic_slice_in_dim(grads["ffn_conv_w"], chip * ffn_conv_w.shape[2], ffn_conv_w.shape[2], axis=2)

    deltas, new_m, new_v = {}, {}, {}
    for n in NAMES:
        shp = w[n].shape
        two_d = (1, shp[0]) if len(shp) == 1 else (-1, shp[-1])
        r2 = lambda a: a.reshape(two_d)
        d, nm, nv = _adamw(r2(w[n]), r2(grads[n]), r2(m_in[n]), r2(v_in[n]), name=f"adamw_{n}")
        deltas[n], new_m[n], new_v[n] = d.reshape(shp), nm.reshape(shp), nv.reshape(shp)
    return (loss, grad_x, *[grads[n] for n in NAMES], *[deltas[n] for n in NAMES], *[new_m[n] for n in NAMES],
            *[new_v[n] for n in NAMES])
```
